```python
import jax
import jax.numpy as jnp
from jax import lax
import numpy as np

D_MODEL = 1024
BATCH = 8
SEQ = 8192
DEPTH = 1

DSA_PATTERNS = ((128, 1), (512, 4), (2048, 16))
DSA_GROUPS = 3
DSA_HEADS = 8
DSA_HEAD_DIM = 64
DSA_WIDTH = DSA_HEADS * DSA_HEAD_DIM
DSA_BLOCK = 128
ROPE_THETA = 10000.0

GDN_HEADS = 8
GDN_KEY_DIM = 64
GDN_VAL_DIM = 64
GDN_K_WIDTH = GDN_HEADS * GDN_KEY_DIM
GDN_V_WIDTH = GDN_HEADS * GDN_VAL_DIM
GDN_CONV = 4
GDN_CHUNK = 64

NORM_EPS = 1e-6

IN_SIZES = (
    DSA_GROUPS * 3 * DSA_WIDTH,
    DSA_WIDTH,
    2 * GDN_K_WIDTH + GDN_V_WIDTH,
    GDN_V_WIDTH,
    GDN_HEADS,
    GDN_HEADS,
    D_MODEL,
    D_MODEL,
)
IN_WIDTH = sum(IN_SIZES)

kernel_name = 'hybrid_dilated_attn_gated_deltanet_block'


def _rmsnorm(x, w):
    xf = x.astype(jnp.float32)
    y = xf * lax.rsqrt(jnp.mean(xf * xf, axis=-1, keepdims=True) + NORM_EPS)
    return (y * w.astype(jnp.float32)).astype(x.dtype)


def _l2norm(t):
    return t * lax.rsqrt(jnp.sum(t * t, axis=-1, keepdims=True) + NORM_EPS)


def _split_columns(h):
    parts, start = [], 0
    for size in IN_SIZES:
        parts.append(h[..., start:start + size])
        start += size
    return parts


def _rope_tables(seq, dim):
    inv_freq = ROPE_THETA ** (-jnp.arange(0, dim, 2, dtype=jnp.float32) / dim)
    ang = jnp.arange(seq, dtype=jnp.float32)[:, None] * inv_freq[None, :]
    ang = jnp.concatenate([ang, ang], axis=-1)
    return jnp.cos(ang), jnp.sin(ang)


def _apply_rope(t, cos, sin):
    half = t.shape[-1] // 2
    rot = jnp.concatenate([-t[..., half:], t[..., :half]], axis=-1)
    return t * cos[:, None, None, :] + rot * sin[:, None, None, :]


def _dilated_window_attention(q, k, v, window, dilation):
    b, s, h, dh = q.shape
    n_back = window // dilation
    sub_len = s // dilation
    n_blk = -(-sub_len // DSA_BLOCK)
    pad = n_blk * DSA_BLOCK - sub_len

    def to_blocks(t):
        t = t.reshape(b, sub_len, dilation, h, dh).transpose(0, 2, 1, 3, 4)
        t = jnp.pad(t, ((0, 0), (0, 0), (0, pad), (0, 0), (0, 0)))
        return t.reshape(b, dilation, n_blk, DSA_BLOCK, h, dh)

    def with_previous_block(t):
        prev = jnp.pad(t, ((0, 0), (0, 0), (1, 0), (0, 0), (0, 0), (0, 0)))[:, :, :-1]
        return jnp.concatenate([prev, t], axis=3)

    qb = to_blocks(q)
    kb = with_previous_block(to_blocks(k))
    vb = with_previous_block(to_blocks(v))
    scores = jnp.einsum('brnqhd,brnkhd->brnhqk', qb, kb) * (dh ** -0.5)
    qi = jnp.arange(DSA_BLOCK)[:, None]
    kj = jnp.arange(2 * DSA_BLOCK)[None, :]
    dist = qi + DSA_BLOCK - kj
    key_idx = jnp.arange(n_blk)[:, None, None] * DSA_BLOCK + kj - DSA_BLOCK
    valid = (dist >= 0) & (dist <= n_back) & (key_idx >= 0)
    scores = jnp.where(valid[:, None], scores, -jnp.inf)
    m = jnp.max(scores, axis=-1, keepdims=True)
    p = jnp.exp(scores - m)
    den = jnp.sum(p, axis=-1, keepdims=True)
    o = jnp.einsum('brnhqk,brnkhd->brnqhd', p / den, vb)
    lse = (m + jnp.log(den))[..., 0].transpose(0, 1, 2, 4, 3)

    def from_blocks(t):
        t = t.reshape(b, dilation, n_blk * DSA_BLOCK, *t.shape[4:])[:, :, :sub_len]
        t = jnp.moveaxis(t, 1, 2)
        return t.reshape(b, s, *t.shape[3:])

    return from_blocks(o), from_blocks(lse)


def _dilated_mixture(q, k, v):
    outs, lses = [], []
    for g, (window, dilation) in enumerate(DSA_PATTERNS):
        o, lse = _dilated_window_attention(q[:, :, g], k[:, :, g], v[:, :, g], window, dilation)
        outs.append(o)
        lses.append(lse)
    wts = jax.nn.softmax(jnp.stack(lses), axis=0)
    o = jnp.einsum('gbsh,gbshd->bshd', wts, jnp.stack(outs))
    return o.reshape(o.shape[0], o.shape[1], -1)


def _causal_depthwise_conv(x, w):
    k, c = w.shape
    return lax.conv_general_dilated(
        x, w[:, None, :].astype(x.dtype), window_strides=(1,), padding=((k - 1, 0),),
        dimension_numbers=('NWC', 'WIO', 'NWC'), feature_group_count=c)


def _gated_delta_rule(q, k, v, g, beta):
    b, s, h, dk = q.shape
    dv = v.shape[-1]
    c = GDN_CHUNK
    n = s // c

    def chunk(t):
        return t.reshape(b, n, c, h, -1).transpose(0, 3, 1, 2, 4)

    qc, kc, vc = chunk(q), chunk(k), chunk(v)
    gc = g.reshape(b, n, c, h).transpose(0, 3, 1, 2)
    bc = beta.reshape(b, n, c, h).transpose(0, 3, 1, 2)
    G = jnp.cumsum(gc, axis=-1)
    causal = jnp.tril(jnp.ones((c, c), dtype=bool))
    strict = jnp.tril(jnp.ones((c, c), dtype=bool), k=-1)
    decay_incl = jnp.exp(jnp.where(causal, G[..., :, None] - G[..., None, :], -jnp.inf))
    decay_strict = jnp.where(strict, decay_incl, 0.0)
    k_beta = kc * bc[..., None]
    a = jnp.einsum('bhnid,bhnjd->bhnij', k_beta, kc) * decay_strict
    eye = jnp.eye(c, dtype=a.dtype)
    t_inv = lax.linalg.triangular_solve(eye + a, jnp.broadcast_to(eye, a.shape),
                                        left_side=True, lower=True, unit_diagonal=True)
    u = t_inv @ (vc * bc[..., None])
    w = t_inv @ (k_beta * jnp.exp(G)[..., None])
    attn = jnp.einsum('bhnid,bhnjd->bhnij', qc, kc) * decay_incl
    q_dec = qc * jnp.exp(G)[..., None]
    g_last = G[..., -1:]
    k_dec = kc * jnp.exp(g_last - G)[..., None]
    chunk_decay = jnp.exp(g_last[..., 0])
    xs = tuple(jnp.moveaxis(t, 2, 0) for t in (q_dec, attn, u, w, k_dec, chunk_decay))

    def step(state, inp):
        q_e, at, u_c, w_c, k_d, dec = inp
        v_new = u_c - jnp.einsum('bhck,bhkv->bhcv', w_c, state)
        o = jnp.einsum('bhck,bhkv->bhcv', q_e, state) + jnp.einsum('bhij,bhjv->bhiv', at, v_new)
        state = state * dec[..., None, None] + jnp.einsum('bhck,bhcv->bhkv', k_d, v_new)
        return state, o

    state0 = jnp.zeros((b, h, dk, dv), dtype=q.dtype)
    _, o = lax.scan(step, state0, xs)
    return o.transpose(1, 0, 3, 2, 4).reshape(b, s, h, dv)


def _fwd_setup_inputs(seed: int = 0) -> dict:
    key = jax.random.key(seed)
    ks = jax.random.split(key, 12)
    f32 = jnp.float32
    x = jax.random.normal(ks[0], (BATCH, SEQ, D_MODEL), f32)
    norm_w = 1.0 + 0.05 * jax.random.normal(ks[1], (DEPTH, D_MODEL), f32)
    w_in = jax.random.normal(ks[2], (DEPTH, D_MODEL, IN_WIDTH), f32) * D_MODEL ** -0.5
    conv_w = jax.random.normal(ks[3], (DEPTH, GDN_CONV, 2 * GDN_K_WIDTH + GDN_V_WIDTH), f32) * GDN_CONV ** -0.5
    a_log = jnp.log(jax.random.uniform(ks[4], (DEPTH, GDN_HEADS), f32, 1.0, 16.0))
    dt_bias = 0.5 * jax.random.normal(ks[5], (DEPTH, GDN_HEADS), f32)
    gdn_norm_w = 1.0 + 0.05 * jax.random.normal(ks[6], (DEPTH, GDN_VAL_DIM), f32)
    w_up_a = jax.random.normal(ks[7], (DEPTH, DSA_WIDTH, D_MODEL), f32) * DSA_WIDTH ** -0.5
    w_up_b = jax.random.normal(ks[8], (DEPTH, GDN_V_WIDTH, D_MODEL), f32) * GDN_V_WIDTH ** -0.5
    w_out = jax.random.normal(ks[9], (DEPTH, D_MODEL, D_MODEL), f32) * D_MODEL ** -0.5
    final_norm_w = 1.0 + 0.05 * jax.random.normal(ks[10], (D_MODEL,), f32)
    return {'x': x, 'norm_w': norm_w, 'w_in': w_in, 'conv_w': conv_w, 'a_log': a_log,
            'dt_bias': dt_bias, 'gdn_norm_w': gdn_norm_w, 'w_up_a': w_up_a, 'w_up_b': w_up_b,
            'w_out': w_out, 'final_norm_w': final_norm_w}


def _fwd_reference(x, norm_w, w_in, conv_w, a_log, dt_bias, gdn_norm_w, w_up_a, w_up_b, w_out, final_norm_w):
    f32 = jnp.float32
    b, s, _ = x.shape
    cos, sin = _rope_tables(s, DSA_HEAD_DIM)
    for layer in range(DEPTH):
        h = _rmsnorm(x, norm_w[layer])
        proj = h @ w_in[layer]
        dsa_qkv, dsa_z, gdn_qkv, gdn_z, gdn_b, gdn_a, gate_a, gate_b = _split_columns(proj)

        qkv = dsa_qkv.astype(f32).reshape(b, s, DSA_GROUPS, 3, DSA_HEADS, DSA_HEAD_DIM)
        q_a = _apply_rope(qkv[:, :, :, 0], cos, sin)
        k_a = _apply_rope(qkv[:, :, :, 1], cos, sin)
        v_a = qkv[:, :, :, 2]
        o_a = _dilated_mixture(q_a, k_a, v_a).astype(x.dtype)
        y_a = (o_a * jax.nn.silu(dsa_z)) @ w_up_a[layer]

        cqkv = jax.nn.silu(_causal_depthwise_conv(gdn_qkv, conv_w[layer])).astype(f32)
        gq, gk, gv = jnp.split(cqkv, [GDN_K_WIDTH, 2 * GDN_K_WIDTH], axis=-1)
        gq = _l2norm(gq.reshape(b, s, GDN_HEADS, GDN_KEY_DIM)) * GDN_KEY_DIM ** -0.5
        gk = _l2norm(gk.reshape(b, s, GDN_HEADS, GDN_KEY_DIM))
        gv = gv.reshape(b, s, GDN_HEADS, GDN_VAL_DIM)
        beta = jax.nn.sigmoid(gdn_b.astype(f32))
        g = -jnp.exp(a_log[layer].astype(f32)) * jax.nn.softplus(gdn_a.astype(f32) + dt_bias[layer].astype(f32))
        o_b = _gated_delta_rule(gq, gk, gv, g, beta)
        o_b = _rmsnorm(o_b, gdn_norm_w[layer]).reshape(b, s, GDN_V_WIDTH).astype(x.dtype)
        y_b = (o_b * jax.nn.silu(gdn_z)) @ w_up_b[layer]

        merged = jax.nn.sigmoid(gate_a) * y_a + jax.nn.sigmoid(gate_b) * y_b
        x = x + merged @ w_out[layer]
    return _rmsnorm(x, final_norm_w)


import jax as _jax
import jax.numpy as _jnp

TWIN_FORMAT = 'train_step'
FWD_PARAMS = ['x', 'norm_w', 'w_in', 'conv_w', 'a_log', 'dt_bias', 'gdn_norm_w', 'w_up_a', 'w_up_b', 'w_out', 'final_norm_w']
TWIN_WEIGHTS = ['norm_w', 'w_in', 'conv_w', 'a_log', 'dt_bias', 'gdn_norm_w', 'w_up_a', 'w_up_b', 'w_out', 'final_norm_w']
TWIN_DIFF_INPUT = 'x'
TWIN_INPUTS = ['x', 'norm_w', 'w_in', 'conv_w', 'a_log', 'dt_bias', 'gdn_norm_w', 'w_up_a', 'w_up_b', 'w_out', 'final_norm_w', 'loss_target', 'm_norm_w', 'm_w_in', 'm_conv_w', 'm_a_log', 'm_dt_bias', 'm_gdn_norm_w', 'm_w_up_a', 'm_w_up_b', 'm_w_out', 'm_final_norm_w', 'v_norm_w', 'v_w_in', 'v_conv_w', 'v_a_log', 'v_dt_bias', 'v_gdn_norm_w', 'v_w_up_a', 'v_w_up_b', 'v_w_out', 'v_final_norm_w']
TWIN_OUTPUTS = ['loss', 'grad_x', 'grad_norm_w', 'grad_w_in', 'grad_conv_w', 'grad_a_log', 'grad_dt_bias', 'grad_gdn_norm_w', 'grad_w_up_a', 'grad_w_up_b', 'grad_w_out', 'grad_final_norm_w', 'delta_norm_w', 'delta_w_in', 'delta_conv_w', 'delta_a_log', 'delta_dt_bias', 'delta_gdn_norm_w', 'delta_w_up_a', 'delta_w_up_b', 'delta_w_out', 'delta_final_norm_w', 'new_m_norm_w', 'new_m_w_in', 'new_m_conv_w', 'new_m_a_log', 'new_m_dt_bias', 'new_m_gdn_norm_w', 'new_m_w_up_a', 'new_m_w_up_b', 'new_m_w_out', 'new_m_final_norm_w', 'new_v_norm_w', 'new_v_w_in', 'new_v_conv_w', 'new_v_a_log', 'new_v_dt_bias', 'new_v_gdn_norm_w', 'new_v_w_up_a', 'new_v_w_up_b', 'new_v_w_out', 'new_v_final_norm_w']
TWIN_LEAF_KINDS = {'loss': 'loss', 'grad_x': 'grad_x', 'grad_norm_w': 'grad_w', 'grad_w_in': 'grad_w', 'grad_conv_w': 'grad_w', 'grad_a_log': 'grad_w', 'grad_dt_bias': 'grad_w', 'grad_gdn_norm_w': 'grad_w', 'grad_w_up_a': 'grad_w', 'grad_w_up_b': 'grad_w', 'grad_w_out': 'grad_w', 'grad_final_norm_w': 'grad_w', 'delta_norm_w': 'delta_w', 'delta_w_in': 'delta_w', 'delta_conv_w': 'delta_w', 'delta_a_log': 'delta_w', 'delta_dt_bias': 'delta_w', 'delta_gdn_norm_w': 'delta_w', 'delta_w_up_a': 'delta_w', 'delta_w_up_b': 'delta_w', 'delta_w_out': 'delta_w', 'delta_final_norm_w': 'delta_w', 'new_m_norm_w': 'new_m', 'new_m_w_in': 'new_m', 'new_m_conv_w': 'new_m', 'new_m_a_log': 'new_m', 'new_m_dt_bias': 'new_m', 'new_m_gdn_norm_w': 'new_m', 'new_m_w_up_a': 'new_m', 'new_m_w_up_b': 'new_m', 'new_m_w_out': 'new_m', 'new_m_final_norm_w': 'new_m', 'new_v_norm_w': 'new_v', 'new_v_w_in': 'new_v', 'new_v_conv_w': 'new_v', 'new_v_a_log': 'new_v', 'new_v_dt_bias': 'new_v', 'new_v_gdn_norm_w': 'new_v', 'new_v_w_up_a': 'new_v', 'new_v_w_up_b': 'new_v', 'new_v_w_out': 'new_v', 'new_v_final_norm_w': 'new_v'}


def _forward(args):
    return _fwd_reference(*[args[k] for k in FWD_PARAMS])


def _output_shape():
    def fwd():
        inp = _fwd_setup_inputs(0)
        return _fwd_reference(*[inp[k] for k in FWD_PARAMS])
    out = _jax.eval_shape(fwd)
    return out.shape, out.dtype

N_MICROBATCH = 1
ADAM_LR = 0.001
ADAM_B1 = 0.9
ADAM_B2 = 0.999
ADAM_EPS = 1e-08
ADAM_WD = 0.01
ADAM_STEP = 10
PER_EXAMPLE_BATCH_AXIS = {'x': 0, 'loss_target': 0}
SHARED_INPUTS = []
_WEIGHT_DTYPES = {'norm_w': _jnp.float32, 'w_in': _jnp.float32, 'conv_w': _jnp.float32, 'a_log': _jnp.float32, 'dt_bias': _jnp.float32, 'gdn_norm_w': _jnp.float32, 'w_up_a': _jnp.float32, 'w_up_b': _jnp.float32, 'w_out': _jnp.float32, 'final_norm_w': _jnp.float32}
MOMENT_SCALE = {'norm_w': 1.752941e-01, 'w_in': 5.805031e-02, 'conv_w': 1.268689e-01, 'a_log': 4.152068e-01, 'dt_bias': 3.811360e-01, 'gdn_norm_w': 2.768240e-01, 'w_up_a': 1.475523e-02, 'w_up_b': 7.048715e-02, 'w_out': 7.038827e-02, 'final_norm_w': 6.410674e+01}


def _to_microbatches(a, axis):
    t = _jnp.moveaxis(a, axis, 0)
    t = t.reshape((N_MICROBATCH, t.shape[0] // N_MICROBATCH) + t.shape[1:])
    return _jnp.moveaxis(t, 1, axis + 1)


def setup_inputs(seed: int = 0) -> dict:
    inp = _fwd_setup_inputs(seed)
    key = _jax.random.fold_in(_jax.random.key(seed), 7919)
    shape, _ = _output_shape()
    out = dict(inp)
    out["loss_target"] = _jax.random.normal(_jax.random.fold_in(key, 0), shape, _jnp.float32)
    for i, name in enumerate(TWIN_WEIGHTS):
        w = inp[name].astype(_jnp.float32)
        if MOMENT_SCALE is None:
            s = _jnp.sqrt(_jnp.mean(_jnp.square(w)) + 1e-30)
        else:
            s = MOMENT_SCALE[name]
        km, kv = _jax.random.split(_jax.random.fold_in(key, i + 1))
        out[name] = w
        out["m_" + name] = s * _jax.random.normal(km, w.shape, _jnp.float32)
        out["v_" + name] = (s * s) * _jax.random.uniform(kv, w.shape, _jnp.float32, 0.5, 1.5)
    if N_MICROBATCH > 1:
        for name, axis in PER_EXAMPLE_BATCH_AXIS.items():
            out[name] = _to_microbatches(out[name], axis)
    return {'x': out['x'], 'norm_w': out['norm_w'], 'w_in': out['w_in'], 'conv_w': out['conv_w'], 'a_log': out['a_log'], 'dt_bias': out['dt_bias'], 'gdn_norm_w': out['gdn_norm_w'], 'w_up_a': out['w_up_a'], 'w_up_b': out['w_up_b'], 'w_out': out['w_out'], 'final_norm_w': out['final_norm_w'], 'loss_target': out['loss_target'], 'm_norm_w': out['m_norm_w'], 'm_w_in': out['m_w_in'], 'm_conv_w': out['m_conv_w'], 'm_a_log': out['m_a_log'], 'm_dt_bias': out['m_dt_bias'], 'm_gdn_norm_w': out['m_gdn_norm_w'], 'm_w_up_a': out['m_w_up_a'], 'm_w_up_b': out['m_w_up_b'], 'm_w_out': out['m_w_out'], 'm_final_norm_w': out['m_final_norm_w'], 'v_norm_w': out['v_norm_w'], 'v_w_in': out['v_w_in'], 'v_conv_w': out['v_conv_w'], 'v_a_log': out['v_a_log'], 'v_dt_bias': out['v_dt_bias'], 'v_gdn_norm_w': out['v_gdn_norm_w'], 'v_w_up_a': out['v_w_up_a'], 'v_w_up_b': out['v_w_up_b'], 'v_w_out': out['v_w_out'], 'v_final_norm_w': out['v_final_norm_w']}


def _loss(weights, diff, rest, loss_target):
    with _jax.named_scope("forward"):
        args = {**rest, TWIN_DIFF_INPUT: diff, **{k: w.astype(_WEIGHT_DTYPES[k]) for k, w in weights.items()}}
        y = _forward(args)
    with _jax.named_scope("loss_head"):
        err = _jnp.square(y.astype(_jnp.float32) - loss_target)
        return 0.5 * _jnp.sum(_jnp.mean(err, axis=-1)) if err.ndim else 0.5 * err


def _adamw(w, g, m, v):
    m = ADAM_B1 * m + (1.0 - ADAM_B1) * g
    v = ADAM_B2 * v + (1.0 - ADAM_B2) * _jnp.square(g)
    m_hat = m / (1.0 - ADAM_B1 ** ADAM_STEP)
    v_hat = v / (1.0 - ADAM_B2 ** ADAM_STEP)
    delta = -ADAM_LR * (m_hat / (_jnp.sqrt(v_hat) + ADAM_EPS) + ADAM_WD * w)
    return delta, m, v


def reference(x, norm_w, w_in, conv_w, a_log, dt_bias, gdn_norm_w, w_up_a, w_up_b, w_out, final_norm_w, loss_target, m_norm_w, m_w_in, m_conv_w, m_a_log, m_dt_bias, m_gdn_norm_w, m_w_up_a, m_w_up_b, m_w_out, m_final_norm_w, v_norm_w, v_w_in, v_conv_w, v_a_log, v_dt_bias, v_gdn_norm_w, v_w_up_a, v_w_up_b, v_w_out, v_final_norm_w):
    given = dict(x=x, norm_w=norm_w, w_in=w_in, conv_w=conv_w, a_log=a_log, dt_bias=dt_bias, gdn_norm_w=gdn_norm_w, w_up_a=w_up_a, w_up_b=w_up_b, w_out=w_out, final_norm_w=final_norm_w, loss_target=loss_target, m_norm_w=m_norm_w, m_w_in=m_w_in, m_conv_w=m_conv_w, m_a_log=m_a_log, m_dt_bias=m_dt_bias, m_gdn_norm_w=m_gdn_norm_w, m_w_up_a=m_w_up_a, m_w_up_b=m_w_up_b, m_w_out=m_w_out, m_final_norm_w=m_final_norm_w, v_norm_w=v_norm_w, v_w_in=v_w_in, v_conv_w=v_conv_w, v_a_log=v_a_log, v_dt_bias=v_dt_bias, v_gdn_norm_w=v_gdn_norm_w, v_w_up_a=v_w_up_a, v_w_up_b=v_w_up_b, v_w_out=v_w_out, v_final_norm_w=v_final_norm_w)
    weights = {n: given[n] for n in TWIN_WEIGHTS}
    shared = {n: given[n] for n in SHARED_INPUTS}
    per_example = {n: given[n] for n in ['x']}
    grad_fn = _jax.value_and_grad(_loss, argnums=(0, 1))

    def one_microbatch(ex, loss_target):
        ex = dict(ex)
        diff = ex.pop(TWIN_DIFF_INPUT)
        return grad_fn(weights, diff, {**shared, **ex}, loss_target)

    if N_MICROBATCH == 1:
        loss, (grad_w, grad_x) = one_microbatch(per_example, given["loss_target"])
    else:
        def body(carry, xs):
            loss_sum, grad_sum = carry
            l_k, (gw_k, gx_k) = one_microbatch(xs[0], xs[1])
            with _jax.named_scope("update"):
                return (loss_sum + l_k, _jax.tree.map(_jnp.add, grad_sum, gw_k)), gx_k

        init = (_jnp.zeros((), _jnp.float32), _jax.tree.map(_jnp.zeros_like, weights))
        (loss, grad_w), grad_x = _jax.lax.scan(body, init, (per_example, given["loss_target"]))
    with _jax.named_scope("update"):
        delta_w, new_m, new_v = {}, {}, {}
        for n in TWIN_WEIGHTS:
            delta_w[n], new_m[n], new_v[n] = _adamw(weights[n], grad_w[n], given["m_" + n], given["v_" + n])
    return (loss, grad_x, *[grad_w[n] for n in TWIN_WEIGHTS], *[delta_w[n] for n in TWIN_WEIGHTS],
            *[new_m[n] for n in TWIN_WEIGHTS], *[new_v[n] for n in TWIN_WEIGHTS])
```

```python
import functools

import jax
import jax.numpy as jnp
from jax import lax
from jax.experimental import pallas as pl
from jax.experimental.pallas import tpu as pltpu

F32 = jnp.float32
BF16 = jnp.bfloat16
MESH = pl.DeviceIdType.MESH
N_DEV = 8
LANES = 128
SUBLANES = 8

GROUPS = 3
HEADS = 8
HEAD_DIM = 64
WIDTH = HEADS * HEAD_DIM
ATT_BLOCK = 128
DILATIONS = (1, 4, 16)
N_BACK = 128
CONV_K = 4
CHUNK = 64
QKV_B = 3 * WIDTH
QKV_A = GROUPS * 3 * WIDTH
BA_PAD = 512
NORM_EPS = 1e-6
ROPE_THETA = 10000.0
ADAM_LR, ADAM_B1, ADAM_B2, ADAM_EPS, ADAM_WD, ADAM_STEP = 0.001, 0.9, 0.999, 1e-08, 0.01, 10

VMEM_LIMIT = 56 * 1024 * 1024

OFF_ZA = QKV_A
OFF_QKVB = OFF_ZA + WIDTH
OFF_ZB = OFF_QKVB + QKV_B
OFF_BA = OFF_ZB + WIDTH
OFF_GATE = OFF_BA + 2 * HEADS


def _params(*sem):
    return pltpu.CompilerParams(dimension_semantics=sem, vmem_limit_bytes=VMEM_LIMIT)


def _dg(a, b, ca, cb):
    return lax.dot_general(a, b, (((ca,), (cb,)), ((), ())), preferred_element_type=F32)


@jax.custom_vjp
def _mm(a, b):
    return _dg(a.astype(BF16), b.astype(BF16), 1, 0)


def _mm_fwd(a, b):
    return _mm(a, b), (a.astype(BF16), b.astype(BF16))


def _mm_bwd(res, ct):
    a16, b16 = res
    c16 = ct.astype(BF16)
    return _dg(c16, b16, 1, 1), _dg(a16, c16, 0, 0)


_mm.defvjp(_mm_fwd, _mm_bwd)


@jax.custom_vjp
def _mm_nt(a, b):
    return _dg(a.astype(BF16), b.astype(BF16), 1, 1)


def _mm_nt_fwd(a, b):
    return _mm_nt(a, b), (a.astype(BF16), b.astype(BF16))


def _mm_nt_bwd(res, ct):
    a16, b16 = res
    c16 = ct.astype(BF16)
    return _dg(c16, b16, 1, 0), _dg(c16, a16, 0, 0)


_mm_nt.defvjp(_mm_nt_fwd, _mm_nt_bwd)


@jax.custom_vjp
def _mm_tn(a, b):
    return _dg(a.astype(BF16), b.astype(BF16), 0, 0)


def _mm_tn_fwd(a, b):
    return _mm_tn(a, b), (a.astype(BF16), b.astype(BF16))


def _mm_tn_bwd(res, ct):
    a16, b16 = res
    c16 = ct.astype(BF16)
    return _dg(b16, c16, 1, 1), _dg(a16, c16, 1, 0)


_mm_tn.defvjp(_mm_tn_fwd, _mm_tn_bwd)


def _split16(a):
    hi = a.astype(BF16)
    lo = (a - hi.astype(F32)).astype(BF16)
    return hi, lo


def _dot3(a, b, ca, cb):
    ah, al = _split16(a)
    bh, bl = _split16(b)
    return _dg(ah, bh, ca, cb) + (_dg(ah, bl, ca, cb) + _dg(al, bh, ca, cb))


def _tri_inv_impl(a):
    n = a.shape[0]
    eye = (lax.broadcasted_iota(jnp.int32, (n, n), 0) == lax.broadcasted_iota(jnp.int32, (n, n), 1)).astype(F32)
    x = eye - a
    p = a
    for _ in range(5):
        p = _dot3(p, p, 1, 0)
        x = x + _dot3(x, p, 1, 0)
    return x


@jax.custom_vjp
def _tri_inv(a):
    return _tri_inv_impl(a)


def _tri_inv_fwd(a):
    t = _tri_inv_impl(a)
    return t, t


def _tri_inv_bwd(t, ct):
    return (-_dot3(_dot3(t, ct, 0, 0), t, 1, 1),)


_tri_inv.defvjp(_tri_inv_fwd, _tri_inv_bwd)


def _sigmoid(x):
    return 1.0 / (1.0 + jnp.exp(-x))


def _silu(x):
    return x * _sigmoid(x)


def _softplus(x):
    return jnp.maximum(x, 0.0) + jnp.log(1.0 + jnp.exp(-jnp.abs(x)))


def _rmsnorm(x, w):
    return x * lax.rsqrt(jnp.mean(x * x, axis=-1, keepdims=True) + NORM_EPS) * w


def _row_block(rows, cap):
    best = None
    for cand in range(SUBLANES, min(rows, cap) + 1, SUBLANES):
        if rows % cand == 0:
            best = cand
    assert best is not None, rows
    return best


def _mesh_peers():
    x, y, c = lax.axis_index("x"), lax.axis_index("y"), lax.axis_index("c")
    me = 4 * x + 2 * y + c
    peers = []
    for k in range(1, N_DEV):
        px = 1 - x if (k >> 2) & 1 else x
        py = 1 - y if (k >> 1) & 1 else y
        pc = 1 - c if k & 1 else c
        peers.append(((px, py, pc), 4 * px + 2 * py + pc))
    return me, peers


def _all_gather(wpack, cpack):
    def body(w_ref, c_ref, wout_ref, cout_ref, send_sems, recv_sems, loc_sems):
        me, peers = _mesh_peers()
        own_w = pltpu.make_async_copy(w_ref, wout_ref.at[me], loc_sems.at[0])
        own_c = pltpu.make_async_copy(c_ref, cout_ref.at[me], loc_sems.at[1])
        own_w.start()
        own_c.start()

        def copies(k, dev, slot):
            return (
                pltpu.make_async_remote_copy(src_ref=w_ref, dst_ref=wout_ref.at[slot], send_sem=send_sems.at[2 * k],
                                             recv_sem=recv_sems.at[2 * k], device_id=dev, device_id_type=MESH),
                pltpu.make_async_remote_copy(src_ref=c_ref, dst_ref=cout_ref.at[slot], send_sem=send_sems.at[2 * k + 1],
                                             recv_sem=recv_sems.at[2 * k + 1], device_id=dev, device_id_type=MESH),
            )

        sends = []
        for k, (dev, _) in enumerate(peers):
            cw, cc = copies(k, dev, me)
            cw.start()
            cc.start()
            sends += [cw, cc]
        for k, (dev, pid) in enumerate(peers):
            cw, cc = copies(k, dev, pid)
            cw.wait_recv()
            cc.wait_recv()
        for cp in sends:
            cp.wait_send()
        own_w.wait()
        own_c.wait()

    any_spec = pl.BlockSpec(memory_space=pl.ANY)
    return pl.pallas_call(
        body, name="weights_all_gather",
        out_shape=(jax.ShapeDtypeStruct((N_DEV,) + wpack.shape, wpack.dtype),
                   jax.ShapeDtypeStruct((N_DEV,) + cpack.shape, cpack.dtype)),
        in_specs=[any_spec, any_spec], out_specs=(any_spec, any_spec),
        scratch_shapes=[pltpu.SemaphoreType.DMA((2 * (N_DEV - 1),)), pltpu.SemaphoreType.DMA((2 * (N_DEV - 1),)),
                        pltpu.SemaphoreType.DMA((2,))],
    )(wpack, cpack)


def _reduce_scatter_exchange(gpack):
    def body(g_ref, out_ref, send_sems, recv_sems, loc_sem):
        me, peers = _mesh_peers()
        own = pltpu.make_async_copy(g_ref.at[me], out_ref.at[me], loc_sem.at[0])
        own.start()
        sends = []
        for k, (dev, pid) in enumerate(peers):
            cp = pltpu.make_async_remote_copy(src_ref=g_ref.at[pid], dst_ref=out_ref.at[me], send_sem=send_sems.at[k],
                                              recv_sem=recv_sems.at[k], device_id=dev, device_id_type=MESH)
            cp.start()
            sends.append(cp)
        for k, (dev, pid) in enumerate(peers):
            pltpu.make_async_remote_copy(src_ref=g_ref.at[me], dst_ref=out_ref.at[pid], send_sem=send_sems.at[k],
                                         recv_sem=recv_sems.at[k], device_id=dev, device_id_type=MESH).wait_recv()
        for cp in sends:
            cp.wait_send()
        own.wait()

    any_spec = pl.BlockSpec(memory_space=pl.ANY)
    return pl.pallas_call(
        body, name="grads_reduce_scatter_exchange",
        out_shape=jax.ShapeDtypeStruct(gpack.shape, gpack.dtype),
        in_specs=[any_spec], out_specs=any_spec,
        scratch_shapes=[pltpu.SemaphoreType.DMA((N_DEV - 1,)), pltpu.SemaphoreType.DMA((N_DEV - 1,)),
                        pltpu.SemaphoreType.DMA((1,))],
    )(gpack)


def _small_all_reduce(part):
    rows = part.shape[0]

    def body(p_ref, o_ref, buf_ref, send_sems, recv_sems):
        me, peers = _mesh_peers()
        buf_ref[me] = p_ref[...]
        sends = []
        for k, (dev, pid) in enumerate(peers):
            cp = pltpu.make_async_remote_copy(src_ref=p_ref, dst_ref=buf_ref.at[me], send_sem=send_sems.at[k],
                                              recv_sem=recv_sems.at[k], device_id=dev, device_id_type=MESH)
            cp.start()
            sends.append(cp)
        for k, (dev, pid) in enumerate(peers):
            pltpu.make_async_remote_copy(src_ref=p_ref, dst_ref=buf_ref.at[pid], send_sem=send_sems.at[k],
                                         recv_sem=recv_sems.at[k], device_id=dev, device_id_type=MESH).wait_recv()
        for cp in sends:
            cp.wait_send()
        acc = buf_ref[0]
        for i in range(1, N_DEV):
            acc = acc + buf_ref[i]
        o_ref[...] = acc

    vmem = pl.BlockSpec(memory_space=pltpu.VMEM)
    return pl.pallas_call(
        body, name="small_all_reduce",
        out_shape=jax.ShapeDtypeStruct(part.shape, F32),
        in_specs=[vmem], out_specs=vmem,
        scratch_shapes=[pltpu.VMEM((N_DEV, rows, LANES), F32), pltpu.SemaphoreType.DMA((N_DEV - 1,)),
                        pltpu.SemaphoreType.DMA((N_DEV - 1,))],
    )(part)


def _adamw_vals(w, g, m, v):
    m = ADAM_B1 * m + (1.0 - ADAM_B1) * g
    v = ADAM_B2 * v + (1.0 - ADAM_B2) * (g * g)
    m_hat = m / (1.0 - ADAM_B1 ** ADAM_STEP)
    v_hat = v / (1.0 - ADAM_B2 ** ADAM_STEP)
    delta = -ADAM_LR * (m_hat / (jnp.sqrt(v_hat) + ADAM_EPS) + ADAM_WD * w)
    return delta, m, v


def _adamw(contrib, w, m, v, name):
    n, rows, _ = contrib.shape
    tr = _row_block(rows, 1024)

    def body(c_ref, w_ref, m_ref, v_ref, g_ref, d_ref, nm_ref, nv_ref):
        g = c_ref[0]
        for i in range(1, n):
            g = g + c_ref[i]
        delta, nm, nv = _adamw_vals(w_ref[...], g, m_ref[...], v_ref[...])
        g_ref[...] = g
        d_ref[...] = delta
        nm_ref[...] = nm
        nv_ref[...] = nv

    row = pl.BlockSpec((tr, LANES), lambda i: (i, 0))
    shp = jax.ShapeDtypeStruct((rows, LANES), F32)
    return pl.pallas_call(
        body, name=name, grid=(rows // tr,),
        in_specs=[pl.BlockSpec((n, tr, LANES), lambda i: (0, i, 0)), row, row, row],
        out_specs=(row, row, row, row), out_shape=(shp, shp, shp, shp),
        compiler_params=_params("parallel"),
    )(contrib, w, m, v)


def _matmul(a, b, out_dtype, name, tm=1024, tn=512, tk=512):
    g, m, k = a.shape
    n = b.shape[2]
    tm, tn, tk = min(tm, m), min(tn, n), min(tk, k)
    assert m % tm == 0 and n % tn == 0 and k % tk == 0, (a.shape, b.shape)
    nk = k // tk

    def body(a_ref, b_ref, o_ref, acc_ref):
        kk = pl.program_id(3)

        @pl.when(kk == 0)
        def _():
            acc_ref[...] = jnp.zeros_like(acc_ref)

        acc_ref[...] += jnp.dot(a_ref[...], b_ref[...], preferred_element_type=F32)

        @pl.when(kk == nk - 1)
        def _():
            o_ref[...] = acc_ref[...].astype(o_ref.dtype)

    return pl.pallas_call(
        body, name=name, grid=(g, m // tm, n // tn, nk),
        in_specs=[pl.BlockSpec((None, tm, tk), lambda gi, i, j, kk: (gi, i, kk)),
                  pl.BlockSpec((None, tk, tn), lambda gi, i, j, kk: (gi, kk, j))],
        out_specs=pl.BlockSpec((None, tm, tn), lambda gi, i, j, kk: (gi, i, j)),
        out_shape=jax.ShapeDtypeStruct((g, m, n), out_dtype),
        scratch_shapes=[pltpu.VMEM((tm, tn), F32)],
        compiler_params=_params("parallel", "parallel", "parallel", "arbitrary"),
    )(a, b)


def _rms_fwd(x, w):
    s, d = x.shape
    tm = _row_block(s, 512)

    def body(x_ref, w_ref, h_ref):
        h_ref[...] = _rmsnorm(x_ref[...], w_ref[...]).astype(BF16)

    return pl.pallas_call(
        body, name="input_rmsnorm", grid=(s // tm,),
        in_specs=[pl.BlockSpec((tm, d), lambda i: (i, 0)), pl.BlockSpec((1, d), lambda i: (0, 0))],
        out_specs=pl.BlockSpec((tm, d), lambda i: (i, 0)),
        out_shape=jax.ShapeDtypeStruct((s, d), BF16),
        compiler_params=_params("parallel"),
    )(x, w)


def _rms_bwd(x, w, dh_parts, dx_res):
    s, d = x.shape
    tm = _row_block(s, 256)
    n_parts = len(dh_parts)

    def body(x_ref, w_ref, *rest):
        part_refs = rest[:n_parts]
        res_ref, gx_ref, gw_ref = rest[n_parts:]
        dh = part_refs[0][...]
        for r in part_refs[1:]:
            dh = dh + r[...]
        _, vjp = jax.vjp(_rmsnorm, x_ref[...], w_ref[...])
        dx, dw = vjp(dh)
        gx_ref[...] = dx + res_ref[...]

        @pl.when(pl.program_id(0) == 0)
        def _():
            gw_ref[...] = jnp.zeros_like(gw_ref)

        gw_ref[...] += dw

    row = pl.BlockSpec((tm, d), lambda i: (i, 0))
    vec = pl.BlockSpec((1, d), lambda i: (0, 0))
    return pl.pallas_call(
        body, name="input_rmsnorm_bwd", grid=(s // tm,),
        in_specs=[row, vec] + [row] * n_parts + [row],
        out_specs=(row, vec),
        out_shape=(jax.ShapeDtypeStruct((s, d), F32), jax.ShapeDtypeStruct((1, d), F32)),
        compiler_params=_params("arbitrary"),
    )(x, w, *dh_parts, dx_res)


def _lane_masks(rows):
    lane = lax.broadcasted_iota(jnp.int32, (rows, LANES), 1)
    return lane < HEAD_DIM, (lane & (HEAD_DIM - 1)) < HEAD_DIM // 2


def _swap_halves(t, lo_half):
    return jnp.where(lo_half, pltpu.roll(t, LANES - HEAD_DIM // 2, 1), pltpu.roll(t, HEAD_DIM // 2, 1))


def _rope(t, cos, sin_signed, lo_half):
    return t * cos + _swap_halves(t, lo_half) * sin_signed


def _rope_bwd(d, cos, sin_signed, lo_half):
    return d * cos - _swap_halves(d, lo_half) * sin_signed


def _window_mask(first):
    qi = lax.broadcasted_iota(jnp.int32, (ATT_BLOCK, 2 * ATT_BLOCK), 0)
    kj = lax.broadcasted_iota(jnp.int32, (ATT_BLOCK, 2 * ATT_BLOCK), 1)
    dist = qi + ATT_BLOCK - kj
    return (dist >= 0) & (dist <= N_BACK) & ((kj >= ATT_BLOCK) | jnp.logical_not(first))


def _blocks_per_subsequence(g, nb):
    return lax.shift_right_logical(jnp.int32(nb), 2 * g)


def _attn_fwd(qkv, cos, sin):
    _, s, _ = qkv.shape
    nb = s // ATT_BLOCK

    def body(qkv_ref, cos_ref, sin_ref, o_ref, lse_ref, kp_ref, vp_ref):
        g, t = pl.program_id(0), pl.program_id(1)
        first = (t & (_blocks_per_subsequence(g, nb) - 1)) == 0

        @pl.when(first)
        def _():
            kp_ref[...] = jnp.zeros_like(kp_ref)
            vp_ref[...] = jnp.zeros_like(vp_ref)

        cos_b, sin_b = cos_ref[...], sin_ref[...]
        head0, lo_half = _lane_masks(ATT_BLOCK)
        valid = _window_mask(first)
        for sl in range(WIDTH // LANES):
            cq = pl.ds(sl * LANES, LANES)
            ck = pl.ds(WIDTH + sl * LANES, LANES)
            cv = pl.ds(2 * WIDTH + sl * LANES, LANES)
            qr = (_rope(qkv_ref[:, cq], cos_b, sin_b, lo_half) * (HEAD_DIM ** -0.5)).astype(BF16)
            kr = _rope(qkv_ref[:, ck], cos_b, sin_b, lo_half).astype(BF16)
            v16 = qkv_ref[:, cv].astype(BF16)
            kcat = jnp.concatenate([kp_ref[:, cq], kr], axis=0)
            vcat = jnp.concatenate([vp_ref[:, cq], v16], axis=0)
            outs, lses = [], []
            for hm in (head0, jnp.logical_not(head0)):
                sc = _dg(jnp.where(hm, qr, jnp.zeros_like(qr)), kcat, 1, 1)
                sc = jnp.where(valid, sc, -jnp.inf)
                mx = jnp.max(sc, axis=1, keepdims=True)
                p = jnp.exp(sc - mx)
                den = jnp.sum(p, axis=1, keepdims=True)
                outs.append(_dg((p / den).astype(BF16), vcat, 1, 0))
                lses.append(mx + jnp.log(den))
            o_ref[:, cq] = jnp.where(head0, outs[0], outs[1])
            lse_ref[:, cq] = jnp.where(head0, lses[0], lses[1])
            kp_ref[:, cq] = kr
            vp_ref[:, cq] = v16

    blk = lambda w: pl.BlockSpec((None, ATT_BLOCK, w), lambda g, t: (g, t, 0))
    shp = jax.ShapeDtypeStruct((GROUPS, s, WIDTH), F32)
    return pl.pallas_call(
        body, name="dilated_attention_fwd", grid=(GROUPS, nb),
        in_specs=[blk(3 * WIDTH), blk(LANES), blk(LANES)],
        out_specs=(blk(WIDTH), blk(WIDTH)), out_shape=(shp, shp),
        scratch_shapes=[pltpu.VMEM((ATT_BLOCK, WIDTH), BF16), pltpu.VMEM((ATT_BLOCK, WIDTH), BF16)],
        compiler_params=_params("arbitrary", "arbitrary"),
    )(qkv, cos, sin)


def _attn_bwd(qkv, cos, sin, o, lse, do, dlse):
    _, s, _ = qkv.shape
    nb = s // ATT_BLOCK

    def body(qkv_ref, cos_ref, sin_ref, cosp_ref, sinp_ref, o_ref, lse_ref, do_ref, dlse_ref,
             dq_ref, dk_ref, dv_ref, kp_ref, vp_ref, dka_ref, dva_ref):
        g, t = pl.program_id(0), pl.program_id(1)
        first = (t & (_blocks_per_subsequence(g, nb) - 1)) == 0
        active = t < nb
        head0, lo_half = _lane_masks(ATT_BLOCK)
        head0_2, _ = _lane_masks(2 * ATT_BLOCK)
        cos_p, sin_p = cosp_ref[...], sinp_ref[...]

        @pl.when(t == 0)
        def _():
            dka_ref[...] = jnp.zeros_like(dka_ref)
            dva_ref[...] = jnp.zeros_like(dva_ref)

        @pl.when(active & first)
        def _():
            kp_ref[...] = jnp.zeros_like(kp_ref)
            vp_ref[...] = jnp.zeros_like(vp_ref)

        @pl.when(active)
        def _():
            cos_b, sin_b = cos_ref[...], sin_ref[...]
            valid = _window_mask(first)
            for sl in range(WIDTH // LANES):
                cq = pl.ds(sl * LANES, LANES)
                ck = pl.ds(WIDTH + sl * LANES, LANES)
                cv = pl.ds(2 * WIDTH + sl * LANES, LANES)
                qr = (_rope(qkv_ref[:, cq], cos_b, sin_b, lo_half) * (HEAD_DIM ** -0.5)).astype(BF16)
                kr = _rope(qkv_ref[:, ck], cos_b, sin_b, lo_half).astype(BF16)
                v16 = qkv_ref[:, cv].astype(BF16)
                kcat = jnp.concatenate([kp_ref[:, cq], kr], axis=0)
                vcat = jnp.concatenate([vp_ref[:, cq], v16], axis=0)
                do_b = do_ref[:, cq]
                do16 = do_b.astype(BF16)
                lse_b = lse_ref[:, cq]
                cterm = dlse_ref[:, cq] - do_b * o_ref[:, cq]
                dqs, dks, dvs = [], [], []
                for hm in (head0, jnp.logical_not(head0)):
                    sc = _dg(jnp.where(hm, qr, jnp.zeros_like(qr)), kcat, 1, 1)
                    sc = jnp.where(valid, sc, -jnp.inf)
                    lse_h = jnp.max(jnp.where(hm, lse_b, -jnp.inf), axis=1, keepdims=True)
                    p = jnp.exp(sc - lse_h)
                    dp = _dg(jnp.where(hm, do16, jnp.zeros_like(do16)), vcat, 1, 1)
                    c = jnp.sum(jnp.where(hm, cterm, 0.0), axis=1, keepdims=True)
                    ds16 = (p * (dp + c)).astype(BF16)
                    dvs.append(_dg(p.astype(BF16), do16, 0, 0))
                    dqs.append(_dg(ds16, kcat, 1, 0))
                    dks.append(_dg(ds16, qr, 0, 0))
                dq = jnp.where(head0, dqs[0], dqs[1]) * (HEAD_DIM ** -0.5)
                dq_ref[:, cq] = _rope_bwd(dq, cos_b, sin_b, lo_half).astype(BF16)
                dkc = jnp.where(head0_2, dks[0], dks[1])
                dvc = jnp.where(head0_2, dvs[0], dvs[1])
                dk_ref[:, cq] = _rope_bwd(dka_ref[:, cq] + dkc[:ATT_BLOCK], cos_p, sin_p, lo_half).astype(BF16)
                dv_ref[:, cq] = (dva_ref[:, cq] + dvc[:ATT_BLOCK]).astype(BF16)
                dka_ref[:, cq] = dkc[ATT_BLOCK:]
                dva_ref[:, cq] = dvc[ATT_BLOCK:]
                kp_ref[:, cq] = kr
                vp_ref[:, cq] = v16

        @pl.when(jnp.logical_not(active))
        def _():
            for sl in range(WIDTH // LANES):
                cq = pl.ds(sl * LANES, LANES)
                dk_ref[:, cq] = _rope_bwd(dka_ref[:, cq], cos_p, sin_p, lo_half).astype(BF16)
                dv_ref[:, cq] = dva_ref[:, cq].astype(BF16)

    cur = lambda w: pl.BlockSpec((None, ATT_BLOCK, w), lambda g, t: (g, jnp.minimum(t, nb - 1), 0))
    prev = lambda w: pl.BlockSpec((None, ATT_BLOCK, w), lambda g, t: (g, jnp.maximum(t - 1, 0), 0))
    shp = jax.ShapeDtypeStruct((GROUPS, s, WIDTH), BF16)
    return pl.pallas_call(
        body, name="dilated_attention_bwd", grid=(GROUPS, nb + 1),
        in_specs=[cur(3 * WIDTH), cur(LANES), cur(LANES), prev(LANES), prev(LANES),
                  cur(WIDTH), cur(WIDTH), cur(WIDTH), cur(WIDTH)],
        out_specs=(cur(WIDTH), prev(WIDTH), prev(WIDTH)), out_shape=(shp, shp, shp),
        scratch_shapes=[pltpu.VMEM((ATT_BLOCK, WIDTH), BF16), pltpu.VMEM((ATT_BLOCK, WIDTH), BF16),
                        pltpu.VMEM((ATT_BLOCK, WIDTH), F32), pltpu.VMEM((ATT_BLOCK, WIDTH), F32)],
        compiler_params=_params("arbitrary", "arbitrary"),
    )(qkv, cos, sin, cos, sin, o, lse, do, dlse)


CONV_PAD = SUBLANES


def _gdn_post(y, is_q, is_k):
    head0, _ = _lane_masks(y.shape[0])
    c = _silu(y)
    sq = c * c
    ss0 = jnp.sum(jnp.where(head0, sq, 0.0), axis=1, keepdims=True)
    ss1 = jnp.sum(jnp.where(head0, 0.0, sq), axis=1, keepdims=True)
    r = jnp.where(head0, lax.rsqrt(ss0 + NORM_EPS), lax.rsqrt(ss1 + NORM_EPS))
    scale = jnp.where(is_q, HEAD_DIM ** -0.5, 1.0).astype(F32)
    return jnp.where(is_q | is_k, c * r * scale, c)


def _conv_rows(xp_ref, w, c0, rows):
    y = w[0:1, :] * xp_ref[pl.ds(c0 + CONV_PAD - (CONV_K - 1), rows), :]
    for k in range(1, CONV_K):
        y = y + w[k:k + 1, :] * xp_ref[pl.ds(c0 + CONV_PAD - (CONV_K - 1) + k, rows), :]
    return y


def _gdn_pre_fwd(proj_r, conv8, col0):
    s = proj_r.shape[0]
    tr = _row_block(s, 512)
    nblk = QKV_B // LANES
    nq = WIDTH // LANES

    def body(x_ref, w_ref, out_ref, xp_ref):
        j = pl.program_id(0)
        is_q, is_k = j < nq, (j >= nq) & (j < 2 * nq)
        xp_ref[pl.ds(0, CONV_PAD), :] = jnp.zeros((CONV_PAD, LANES), F32)
        xp_ref[pl.ds(CONV_PAD, s), :] = x_ref[...]
        w = w_ref[...]
        for c in range(s // tr):
            out_ref[pl.ds(c * tr, tr), :] = _gdn_post(_conv_rows(xp_ref, w, c * tr, tr), is_q, is_k)

    return pl.pallas_call(
        body, name="gdn_conv_fwd", grid=(nblk,),
        in_specs=[pl.BlockSpec((s, LANES), lambda j: (0, col0 + j)), pl.BlockSpec((SUBLANES, LANES), lambda j: (0, j))],
        out_specs=pl.BlockSpec((s, LANES), lambda j: (0, j)),
        out_shape=jax.ShapeDtypeStruct((s, QKV_B), F32),
        scratch_shapes=[pltpu.VMEM((s + CONV_PAD, LANES), F32)],
        compiler_params=_params("parallel"),
    )(proj_r, conv8)


def _gdn_pre_bwd(proj_r, conv8, dc, col0):
    s = proj_r.shape[0]
    tr = _row_block(s, 512)
    nblk = QKV_B // LANES
    nq = WIDTH // LANES

    def body(x_ref, w_ref, dc_ref, dx_ref, dw_ref, xp_ref, dyp_ref):
        j = pl.program_id(0)
        is_q, is_k = j < nq, (j >= nq) & (j < 2 * nq)
        xp_ref[pl.ds(0, CONV_PAD), :] = jnp.zeros((CONV_PAD, LANES), F32)
        xp_ref[pl.ds(CONV_PAD, s), :] = x_ref[...]
        dyp_ref[pl.ds(s, CONV_PAD), :] = jnp.zeros((CONV_PAD, LANES), F32)
        w = w_ref[...]
        for c in range(s // tr):
            y = _conv_rows(xp_ref, w, c * tr, tr)
            _, vjp = jax.vjp(lambda yy: _gdn_post(yy, is_q, is_k), y)
            dyp_ref[pl.ds(c * tr, tr), :] = vjp(dc_ref[pl.ds(c * tr, tr), :])[0]
        dws = [jnp.zeros((1, LANES), F32) for _ in range(CONV_K)]
        for c in range(s // tr):
            c0 = c * tr
            dy = dyp_ref[pl.ds(c0, tr), :]
            dx = w[0:1, :] * dyp_ref[pl.ds(c0 + CONV_K - 1, tr), :]
            for k in range(1, CONV_K):
                dx = dx + w[k:k + 1, :] * dyp_ref[pl.ds(c0 + CONV_K - 1 - k, tr), :]
            dx_ref[pl.ds(c0, tr), :] = dx.astype(BF16)
            for k in range(CONV_K):
                xs = xp_ref[pl.ds(c0 + CONV_PAD - (CONV_K - 1) + k, tr), :]
                dws[k] = dws[k] + jnp.sum(dy * xs, axis=0, keepdims=True)
        row = lax.broadcasted_iota(jnp.int32, (SUBLANES, LANES), 0)
        dwb = jnp.zeros((SUBLANES, LANES), F32)
        for k in range(CONV_K):
            dwb = dwb + jnp.where(row == k, dws[k], 0.0)
        dw_ref[...] = dwb

    return pl.pallas_call(
        body, name="gdn_conv_bwd", grid=(nblk,),
        in_specs=[pl.BlockSpec((s, LANES), lambda j: (0, col0 + j)), pl.BlockSpec((SUBLANES, LANES), lambda j: (0, j)),
                  pl.BlockSpec((s, LANES), lambda j: (0, j))],
        out_specs=(pl.BlockSpec((s, LANES), lambda j: (0, j)), pl.BlockSpec((SUBLANES, LANES), lambda j: (0, j))),
        out_shape=(jax.ShapeDtypeStruct((s, QKV_B), BF16), jax.ShapeDtypeStruct((SUBLANES, QKV_B), F32)),
        scratch_shapes=[pltpu.VMEM((s + CONV_PAD, LANES), F32), pltpu.VMEM((s + CONV_PAD, LANES), F32)],
        compiler_params=_params("parallel"),
    )(proj_r, conv8, dc)


def _gdn_chunk(q, k, v, bcol, acol, alog, dtb, gnw, state):
    n = q.shape[0]
    row = lax.broadcasted_iota(jnp.int32, (n, n), 0)
    col = lax.broadcasted_iota(jnp.int32, (n, n), 1)
    beta = _sigmoid(bcol)
    g = -jnp.exp(alog) * _softplus(acol + dtb)
    g_row = jnp.sum(jnp.where(row == col, g, 0.0), axis=0, keepdims=True)
    big_g = jnp.sum(jnp.where(row >= col, g_row, 0.0), axis=1, keepdims=True)
    big_g_row = jnp.sum(jnp.where(row <= col, g, 0.0), axis=0, keepdims=True)
    decay_incl = jnp.exp(jnp.where(row >= col, big_g - big_g_row, -jnp.inf))
    decay_strict = jnp.where(row > col, decay_incl, 0.0)
    k_beta = k * beta
    t_inv = _tri_inv(_mm_nt(k_beta, k) * decay_strict)
    e_g = jnp.exp(big_g)
    u = _mm(t_inv, v * beta)
    w = _mm(t_inv, k_beta * e_g)
    attn = _mm_nt(q, k) * decay_incl
    v_new = u - _mm(w, state)
    o = _mm(q * e_g, state) + _mm(attn, v_new)
    total = jnp.sum(g, axis=0, keepdims=True)
    new_state = state * jnp.exp(total) + _mm_tn(k * jnp.exp(total - big_g), v_new)
    return _rmsnorm(o, gnw), new_state


def _head_columns(ba, h):
    lane = lax.broadcasted_iota(jnp.int32, ba.shape, 1)
    return (jnp.sum(jnp.where(lane == h, ba, 0.0), axis=1, keepdims=True),
            jnp.sum(jnp.where(lane == HEADS + h, ba, 0.0), axis=1, keepdims=True))


def _gdn_scan_fwd(gq, gk, gv, proj_r, ba_col, alog_b, dtb_b, gnw):
    _, s, _ = gq.shape
    nc = s // CHUNK

    def body(q_ref, k_ref, v_ref, ba_ref, al_ref, dt_ref, gnw_ref, o_ref, st_ref, state_ref):
        @pl.when(pl.program_id(0) == 0)
        def _():
            state_ref[...] = jnp.zeros_like(state_ref)

        ba = ba_ref[...]
        gnw_v = gnw_ref[...]
        for h in range(HEADS):
            bcol, acol = _head_columns(ba, h)
            st = state_ref[h]
            st_ref[h] = st
            o, new_st = _gdn_chunk(q_ref[h], k_ref[h], v_ref[h], bcol, acol, al_ref[h][:, 0:1], dt_ref[h][:, 0:1], gnw_v, st)
            o_ref[h] = o
            state_ref[h] = new_st

    hm = pl.BlockSpec((HEADS, CHUNK, HEAD_DIM), lambda n: (0, n, 0))
    par = pl.BlockSpec((HEADS, 1, HEAD_DIM), lambda n: (0, 0, 0))
    return pl.pallas_call(
        body, name="gdn_scan_fwd", grid=(nc,),
        in_specs=[hm, hm, hm, pl.BlockSpec((CHUNK, LANES), lambda n: (n, ba_col)), par, par,
                  pl.BlockSpec((1, HEAD_DIM), lambda n: (0, 0))],
        out_specs=(hm, pl.BlockSpec((None, HEADS, HEAD_DIM, HEAD_DIM), lambda n: (n, 0, 0, 0))),
        out_shape=(jax.ShapeDtypeStruct((HEADS, s, HEAD_DIM), F32),
                   jax.ShapeDtypeStruct((nc, HEADS, HEAD_DIM, HEAD_DIM), F32)),
        scratch_shapes=[pltpu.VMEM((HEADS, HEAD_DIM, HEAD_DIM), F32)],
        compiler_params=_params("arbitrary"),
    )(gq, gk, gv, proj_r, alog_b, dtb_b, gnw)


def _gdn_scan_bwd(gq, gk, gv, proj_r, ba_col, alog_b, dtb_b, gnw, states, do):
    _, s, _ = gq.shape
    nc = s // CHUNK

    def body(q_ref, k_ref, v_ref, ba_ref, al_ref, dt_ref, gnw_ref, st_ref, do_ref,
             dq_ref, dk_ref, dv_ref, dba_ref, dal_ref, ddt_ref, dgnw_ref, dstate_ref):
        @pl.when(pl.program_id(0) == 0)
        def _():
            dstate_ref[...] = jnp.zeros_like(dstate_ref)
            dal_ref[...] = jnp.zeros_like(dal_ref)
            ddt_ref[...] = jnp.zeros_like(ddt_ref)
            dgnw_ref[...] = jnp.zeros_like(dgnw_ref)

        ba = ba_ref[...]
        gnw_v = gnw_ref[...]
        lane = lax.broadcasted_iota(jnp.int32, ba.shape, 1)
        dba = jnp.zeros(ba.shape, F32)
        dgnw = jnp.zeros(gnw_v.shape, F32)
        for h in range(HEADS):
            bcol, acol = _head_columns(ba, h)
            _, vjp = jax.vjp(_gdn_chunk, q_ref[h], k_ref[h], v_ref[h], bcol, acol, al_ref[h][:, 0:1],
                             dt_ref[h][:, 0:1], gnw_v, st_ref[h])
            dq, dk, dv, dbc, dac, dal, ddt, dgn, dst = vjp((do_ref[h], dstate_ref[h]))
            dq_ref[h] = dq
            dk_ref[h] = dk
            dv_ref[h] = dv
            dstate_ref[h] = dst
            dba = dba + jnp.where(lane == h, dbc, 0.0) + jnp.where(lane == HEADS + h, dac, 0.0)
            dal_ref[h] += jnp.broadcast_to(dal, (1, HEAD_DIM))
            ddt_ref[h] += jnp.broadcast_to(ddt, (1, HEAD_DIM))
            dgnw = dgnw + dgn
        dba_ref[...] = dba
        dgnw_ref[...] += dgnw

    rev = lambda n: nc - 1 - n
    hm = pl.BlockSpec((HEADS, CHUNK, HEAD_DIM), lambda n: (0, rev(n), 0))
    par = pl.BlockSpec((HEADS, 1, HEAD_DIM), lambda n: (0, 0, 0))
    vec = pl.BlockSpec((1, HEAD_DIM), lambda n: (0, 0))
    hm_shape = jax.ShapeDtypeStruct((HEADS, s, HEAD_DIM), F32)
    par_shape = jax.ShapeDtypeStruct((HEADS, 1, HEAD_DIM), F32)
    return pl.pallas_call(
        body, name="gdn_scan_bwd", grid=(nc,),
        in_specs=[hm, hm, hm, pl.BlockSpec((CHUNK, LANES), lambda n: (rev(n), ba_col)), par, par, vec,
                  pl.BlockSpec((None, HEADS, HEAD_DIM, HEAD_DIM), lambda n: (rev(n), 0, 0, 0)), hm],
        out_specs=(hm, hm, hm, pl.BlockSpec((CHUNK, LANES), lambda n: (rev(n), 0)), par, par, vec),
        out_shape=(hm_shape, hm_shape, hm_shape, jax.ShapeDtypeStruct((s, LANES), F32), par_shape, par_shape,
                   jax.ShapeDtypeStruct((1, HEAD_DIM), F32)),
        scratch_shapes=[pltpu.VMEM((HEADS, HEAD_DIM, HEAD_DIM), F32)],
        compiler_params=_params("arbitrary"),
    )(gq, gk, gv, proj_r, alog_b, dtb_b, gnw, states, do)


def _tail_loss(x, tgt, o0, o1, o2, l0, l1, l2, ga, gb, za, zb, ob, wua, wub, wo, fnw):
    lm = jnp.maximum(jnp.maximum(l0, l1), l2)
    e0, e1, e2 = jnp.exp(l0 - lm), jnp.exp(l1 - lm), jnp.exp(l2 - lm)
    o_a = (e0 * o0 + e1 * o1 + e2 * o2) / (e0 + e1 + e2)
    y_a = _mm(o_a * _silu(za), wua)
    y_b = _mm(ob * _silu(zb), wub)
    merged = _sigmoid(ga) * y_a + _sigmoid(gb) * y_b
    y = _rmsnorm(x + _mm(merged, wo), fnw)
    err = y - tgt
    per_token = jnp.sum(err * err, axis=1, keepdims=True) * (0.5 / x.shape[1])
    return jnp.sum(per_token, axis=0, keepdims=True)


def _tail(x, tgt, og, lg, proj_r, ob, wua, wub, wo, fnw):
    s, d = x.shape
    tm = _row_block(s, 128)
    col_za = 2 * d // WIDTH
    col_zb = (2 * d + WIDTH + QKV_B) // WIDTH

    def body(x_ref, t_ref, o0_ref, o1_ref, o2_ref, l0_ref, l1_ref, l2_ref, ga_ref, gb_ref, za_ref, zb_ref, ob_ref,
             wua_ref, wub_ref, wo_ref, fnw_ref,
             loss_ref, dx_ref, do0_ref, do1_ref, do2_ref, dl0_ref, dl1_ref, dl2_ref, dga_ref, dgb_ref, dza_ref,
             dzb_ref, dob_ref, dwua_ref, dwub_ref, dwo_ref, dfnw_ref):
        @pl.when(pl.program_id(0) == 0)
        def _():
            for r in (loss_ref, dwua_ref, dwub_ref, dwo_ref, dfnw_ref):
                r[...] = jnp.zeros_like(r)

        args = (x_ref[...], t_ref[...], o0_ref[...], o1_ref[...], o2_ref[...], l0_ref[...], l1_ref[...], l2_ref[...],
                ga_ref[...], gb_ref[...], za_ref[...], zb_ref[...], ob_ref[...],
                wua_ref[...].astype(F32), wub_ref[...].astype(F32), wo_ref[...].astype(F32), fnw_ref[...])
        loss, vjp = jax.vjp(_tail_loss, *args)
        (dx, _, do0, do1, do2, dl0, dl1, dl2, dga, dgb, dza, dzb, dob, dwua, dwub, dwo, dfnw) = vjp(jnp.ones((1, 1), F32))
        loss_ref[...] += jnp.broadcast_to(loss, loss_ref.shape)
        dx_ref[...] = dx
        do0_ref[...], do1_ref[...], do2_ref[...] = do0, do1, do2
        dl0_ref[...], dl1_ref[...], dl2_ref[...] = dl0, dl1, dl2
        dga_ref[...] = dga.astype(BF16)
        dgb_ref[...] = dgb.astype(BF16)
        dza_ref[...] = dza.astype(BF16)
        dzb_ref[...] = dzb.astype(BF16)
        dob_ref[...] = dob
        dwua_ref[...] += dwua
        dwub_ref[...] += dwub
        dwo_ref[...] += dwo
        dfnw_ref[...] += dfnw

    row = lambda w, c=0: pl.BlockSpec((tm, w), lambda i: (i, c))
    full = lambda a, b: pl.BlockSpec((a, b), lambda i: (0, 0))
    f32 = lambda a, b: jax.ShapeDtypeStruct((a, b), F32)
    b16 = lambda a, b: jax.ShapeDtypeStruct((a, b), BF16)
    in_specs = ([row(d), row(d)] + [row(WIDTH)] * 6 + [row(d, 0), row(d, 1), row(WIDTH, col_za), row(WIDTH, col_zb),
                row(WIDTH), full(WIDTH, d), full(WIDTH, d), full(d, d), full(1, d)])
    out_specs = ([full(SUBLANES, LANES), row(d)] + [row(WIDTH)] * 6 + [row(d), row(d), row(WIDTH), row(WIDTH), row(WIDTH),
                 full(WIDTH, d), full(WIDTH, d), full(d, d), full(1, d)])
    out_shape = ([f32(SUBLANES, LANES), f32(s, d)] + [f32(s, WIDTH)] * 6 + [b16(s, d), b16(s, d), b16(s, WIDTH),
                 b16(s, WIDTH), f32(s, WIDTH), f32(WIDTH, d), f32(WIDTH, d), f32(d, d), f32(1, d)])
    return pl.pallas_call(
        body, name="tail_fwd_bwd", grid=(s // tm,),
        in_specs=in_specs, out_specs=tuple(out_specs), out_shape=tuple(out_shape),
        compiler_params=_params("arbitrary"),
    )(x, tgt, og[0], og[1], og[2], lg[0], lg[1], lg[2], proj_r, proj_r, proj_r, proj_r, ob, wua, wub, wo, fnw)


def _to_dilated(a, dil):
    if dil == 1:
        return a
    s = a.shape[0]
    return a.reshape(s // dil, dil, -1).transpose(1, 0, 2).reshape(a.shape)


def _from_dilated(a, dil):
    if dil == 1:
        return a
    s = a.shape[0]
    return a.reshape(dil, s // dil, -1).transpose(1, 0, 2).reshape(a.shape)


def _head_major(a):
    return a.reshape(a.shape[0], HEADS, HEAD_DIM).transpose(1, 0, 2)


def _from_head_major(a):
    return a.transpose(1, 0, 2).reshape(a.shape[1], WIDTH)


def _rope_tables(s):
    inv_freq = ROPE_THETA ** (-jnp.arange(0, HEAD_DIM, 2, dtype=F32) / HEAD_DIM)
    sign = jnp.where(jnp.arange(HEAD_DIM) < HEAD_DIM // 2, -1.0, 1.0).astype(F32)
    cos, sin = [], []
    for dil in DILATIONS:
        pos = _to_dilated(jnp.arange(s, dtype=F32)[:, None], dil)
        ang = pos * inv_freq[None, :]
        ang = jnp.concatenate([ang, ang], axis=-1)
        cos.append(jnp.tile(jnp.cos(ang), (1, 2)))
        sin.append(jnp.tile(jnp.sin(ang) * sign[None, :], (1, 2)))
    return jnp.stack(cos), jnp.stack(sin)


def _pack_rows(parts, dtype, row_multiple):
    flat = jnp.concatenate([p.reshape(-1).astype(dtype) for p in parts])
    tile = row_multiple * LANES
    pad = (-flat.shape[0]) % tile
    return jnp.pad(flat, (0, pad)).reshape(-1, LANES)


def _unpack_rows(packed, shapes):
    flat = packed.reshape(-1)
    out, start = [], 0
    for shp in shapes:
        size = 1
        for n in shp:
            size *= n
        out.append(flat[start:start + size].reshape(shp))
        start += size
    return out


def kernel(x, norm_w, w_in, conv_w, a_log, dt_bias, gdn_norm_w, w_up_a, w_up_b, w_out, final_norm_w, loss_target, m_norm_w, m_w_in, m_conv_w, m_a_log, m_dt_bias, m_gdn_norm_w, m_w_up_a, m_w_up_b, m_w_out, m_final_norm_w, v_norm_w, v_w_in, v_conv_w, v_a_log, v_dt_bias, v_gdn_norm_w, v_w_up_a, v_w_up_b, v_w_out, v_final_norm_w):
    x2, tgt = x[0], loss_target[0]
    s, d = x2.shape
    me = 4 * lax.axis_index("x") + 2 * lax.axis_index("y") + lax.axis_index("c")
    big_shapes = [w_in.shape[1:], w_up_a.shape[1:], w_up_b.shape[1:], w_out.shape[1:]]
    win8 = w_in.shape[2]
    conv8w = conv_w.shape[2]
    nr = 2 * d + 2 * WIDTH + QKV_B + BA_PAD

    wall, call = _all_gather(_pack_rows([w_in, w_up_a, w_up_b, w_out], BF16, 16), _pack_rows([conv_w], F32, SUBLANES))
    per_dev = [_unpack_rows(wall[i], big_shapes) for i in range(N_DEV)]
    w_in_f = jnp.concatenate([p[0] for p in per_dev], axis=1)
    wua = jnp.concatenate([p[1] for p in per_dev], axis=1)
    wub = jnp.concatenate([p[2] for p in per_dev], axis=1)
    wo = jnp.concatenate([p[3] for p in per_dev], axis=0)
    conv_full = jnp.concatenate([_unpack_rows(call[i], [conv_w.shape[1:]])[0] for i in range(N_DEV)], axis=1)
    conv8 = jnp.pad(conv_full, ((0, SUBLANES - CONV_K), (0, 0)))

    w_qkv = w_in_f[:, :QKV_A].reshape(d, GROUPS, QKV_B).transpose(1, 0, 2)
    w_rest = jnp.concatenate([
        w_in_f[:, OFF_GATE:OFF_GATE + 2 * d], w_in_f[:, OFF_ZA:OFF_ZA + WIDTH], w_in_f[:, OFF_QKVB:OFF_QKVB + QKV_B],
        w_in_f[:, OFF_ZB:OFF_ZB + WIDTH], w_in_f[:, OFF_BA:OFF_BA + 2 * HEADS],
        jnp.zeros((d, BA_PAD - 2 * HEADS), BF16)], axis=1)
    col_qkvb = (2 * d + WIDTH) // LANES
    col_ba = (2 * d + 2 * WIDTH + QKV_B) // LANES

    h = _rms_fwd(x2, norm_w)
    h_all = jnp.stack([_to_dilated(h, dil) for dil in DILATIONS])
    qkv_all = _matmul(h_all, w_qkv, F32, "in_proj_attention")
    proj_r = _matmul(h[None], w_rest[None], F32, "in_proj_rest")[0]
    cos, sin = _rope_tables(s)
    o_all, lse_all = _attn_fwd(qkv_all, cos, sin)
    og = [_from_dilated(o_all[g], dil) for g, dil in enumerate(DILATIONS)]
    lg = [_from_dilated(lse_all[g], dil) for g, dil in enumerate(DILATIONS)]

    cqkv = _gdn_pre_fwd(proj_r, conv8, col_qkvb)
    gq, gk, gv = (_head_major(cqkv[:, i * WIDTH:(i + 1) * WIDTH]) for i in range(3))
    alog_b = jnp.broadcast_to(a_log.reshape(HEADS, 1, 1), (HEADS, 1, HEAD_DIM))
    dtb_b = jnp.broadcast_to(dt_bias.reshape(HEADS, 1, 1), (HEADS, 1, HEAD_DIM))
    ob_hm, states = _gdn_scan_fwd(gq, gk, gv, proj_r, col_ba, alog_b, dtb_b, gdn_norm_w)
    ob = _from_head_major(ob_hm)

    (loss_blk, dx_res, do0, do1, do2, dl0, dl1, dl2, dga, dgb, dza, dzb, dob, dwua, dwub, dwo, dfnw) = _tail(
        x2, tgt, og, lg, proj_r, ob, wua, wub, wo, final_norm_w.reshape(1, d))

    do_all = jnp.stack([_to_dilated(t, dil) for t, dil in zip((do0, do1, do2), DILATIONS)])
    dl_all = jnp.stack([_to_dilated(t, dil) for t, dil in zip((dl0, dl1, dl2), DILATIONS)])
    dq_a, dk_a, dv_a = _attn_bwd(qkv_all, cos, sin, o_all, lse_all, do_all, dl_all)
    dqkv_all = jnp.concatenate([dq_a, dk_a, dv_a], axis=2)

    dgq, dgk, dgv, dba, dalog_b, ddtb_b, dgnw = _gdn_scan_bwd(gq, gk, gv, proj_r, col_ba, alog_b, dtb_b, gdn_norm_w,
                                                              states, _head_major(dob))
    dcqkv = jnp.concatenate([_from_head_major(t) for t in (dgq, dgk, dgv)], axis=1)
    dqkv_b, dconv8 = _gdn_pre_bwd(proj_r, conv8, dcqkv, col_qkvb)
    dproj_r = jnp.concatenate([dga, dgb, dza, dqkv_b, dzb,
                               jnp.pad(dba.astype(BF16), ((0, 0), (0, BA_PAD - LANES)))], axis=1)

    ht_all = h_all.transpose(0, 2, 1)
    dw_qkv = _matmul(ht_all, dqkv_all, F32, "in_proj_attention_dw")
    dw_rest = _matmul(ht_all[0:1], dproj_r[None], F32, "in_proj_rest_dw")[0]
    dh_a = _matmul(dqkv_all, w_qkv.transpose(0, 2, 1), F32, "in_proj_attention_dh")
    dh_r = _matmul(dproj_r[None], w_rest.T[None], F32, "in_proj_rest_dh")[0]
    dh_parts = [dh_r] + [_from_dilated(dh_a[g], dil) for g, dil in enumerate(DILATIONS)]
    grad_x, dnorm_w = _rms_bwd(x2, norm_w, dh_parts, dx_res)

    o2 = 2 * d
    dw_in = jnp.concatenate([
        dw_qkv.transpose(1, 0, 2).reshape(d, QKV_A),
        dw_rest[:, o2:o2 + WIDTH], dw_rest[:, o2 + WIDTH:o2 + WIDTH + QKV_B],
        dw_rest[:, o2 + WIDTH + QKV_B:o2 + 2 * WIDTH + QKV_B],
        dw_rest[:, o2 + 2 * WIDTH + QKV_B:o2 + 2 * WIDTH + QKV_B + 2 * HEADS],
        dw_rest[:, :o2]], axis=1)

    def col_slabs(a, width):
        return a.reshape(a.shape[0], N_DEV, width).transpose(1, 0, 2).reshape(N_DEV, -1)

    slabs = jnp.concatenate([col_slabs(dw_in, win8), col_slabs(dwua, d // N_DEV), col_slabs(dwub, d // N_DEV),
                             dwo.reshape(N_DEV, -1)], axis=1)
    pad = (-slabs.shape[1]) % (16 * LANES)
    gpack = jnp.pad(slabs, ((0, 0), (0, pad))).reshape(N_DEV, -1, LANES)
    contrib = _reduce_scatter_exchange(gpack)

    small_parts = [dnorm_w, dfnw, dconv8[:CONV_K], dalog_b[:, 0, 0], ddtb_b[:, 0, 0], dgnw, loss_blk[0, 0:1]]
    small_rows = [-(-p.size // LANES) for p in small_parts]
    small = jnp.concatenate([jnp.pad(p.reshape(-1), (0, r * LANES - p.size)).reshape(r, LANES)
                             for p, r in zip(small_parts, small_rows)])
    small = jnp.pad(small, ((0, (-small.shape[0]) % SUBLANES), (0, 0)))
    small_sum = _small_all_reduce(small)
    pieces, r0 = [], 0
    for p, r in zip(small_parts, small_rows):
        pieces.append(small_sum[r0:r0 + r].reshape(-1)[:p.size].reshape(p.shape))
        r0 += r
    g_norm_w, g_fnw, g_conv_full, g_alog, g_dtb, g_gnw, loss_sum = pieces
    g_conv = lax.dynamic_slice(g_conv_full, (0, me * conv8w), (CONV_K, conv8w))

    w_big = _pack_rows([w_in, w_up_a, w_up_b, w_out], F32, 16)
    m_big = _pack_rows([m_w_in, m_w_up_a, m_w_up_b, m_w_out], F32, 16)
    v_big = _pack_rows([v_w_in, v_w_up_a, v_w_up_b, v_w_out], F32, 16)
    big = _adamw(contrib, w_big, m_big, v_big, "adamw_sharded")
    g_big, d_big, nm_big, nv_big = (_unpack_rows(t, big_shapes) for t in big)

    small_ws = [norm_w, final_norm_w, conv_w, a_log, dt_bias, gdn_norm_w]
    small_ms = [m_norm_w, m_final_norm_w, m_conv_w, m_a_log, m_dt_bias, m_gdn_norm_w]
    small_vs = [v_norm_w, v_final_norm_w, v_conv_w, v_a_log, v_dt_bias, v_gdn_norm_w]
    small_gs = [g_norm_w, g_fnw, g_conv, g_alog, g_dtb, g_gnw]
    small_shapes = [t.shape for t in small_ws]
    sm = _adamw(_pack_rows(small_gs, F32, SUBLANES)[None], _pack_rows(small_ws, F32, SUBLANES),
                _pack_rows(small_ms, F32, SUBLANES), _pack_rows(small_vs, F32, SUBLANES), "adamw_small")
    g_sm, d_sm, nm_sm, nv_sm = (_unpack_rows(t, small_shapes) for t in sm)

    def ordered(bigs, smalls):
        nw, fnw_, cw, al, dtb, gn = smalls
        wi, ua, ub, wo_ = (t[None] for t in bigs)
        return [nw, wi, cw, al, dtb, gn, ua, ub, wo_, fnw_]

    return (loss_sum.reshape(()), grad_x[None], *ordered(g_big, g_sm), *ordered(d_big, d_sm),
            *ordered(nm_big, nm_sm), *ordered(nv_big, nv_sm))
```

```python
import functools

import jax
import jax.numpy as jnp
from jax import lax
from jax.experimental import pallas as pl
from jax.experimental.pallas import tpu as pltpu

F32 = jnp.float32
BF16 = jnp.bfloat16
MESH = pl.DeviceIdType.MESH
N_DEV = 8
LANES = 128
SUBLANES = 8

GROUPS = 3
HEADS = 8
HEAD_DIM = 64
WIDTH = HEADS * HEAD_DIM
ATT_BLOCK = 128
DILATIONS = (1, 4, 16)
N_BACK = 128
CONV_K = 4
CHUNK = 64
QKV_B = 3 * WIDTH
QKV_A = GROUPS * 3 * WIDTH
BA_PAD = 512
NORM_EPS = 1e-6
ROPE_THETA = 10000.0
ADAM_LR, ADAM_B1, ADAM_B2, ADAM_EPS, ADAM_WD, ADAM_STEP = 0.001, 0.9, 0.999, 1e-08, 0.01, 10

VMEM_LIMIT = 56 * 1024 * 1024

OFF_ZA = QKV_A
OFF_QKVB = OFF_ZA + WIDTH
OFF_ZB = OFF_QKVB + QKV_B
OFF_BA = OFF_ZB + WIDTH
OFF_GATE = OFF_BA + 2 * HEADS


def _params(*sem):
    return pltpu.CompilerParams(dimension_semantics=sem, vmem_limit_bytes=VMEM_LIMIT)


def _dg(a, b, ca, cb):
    nb = a.ndim - 2
    batch = tuple(range(nb))
    return lax.dot_general(a, b, (((nb + ca,), (nb + cb,)), (batch, batch)), preferred_element_type=F32)


@jax.custom_vjp
def _mm(a, b):
    return _dg(a.astype(BF16), b.astype(BF16), 1, 0)


def _mm_fwd(a, b):
    return _mm(a, b), (a.astype(BF16), b.astype(BF16))


def _mm_bwd(res, ct):
    a16, b16 = res
    c16 = ct.astype(BF16)
    return _dg(c16, b16, 1, 1), _dg(a16, c16, 0, 0)


_mm.defvjp(_mm_fwd, _mm_bwd)


@jax.custom_vjp
def _mm_nt(a, b):
    return _dg(a.astype(BF16), b.astype(BF16), 1, 1)


def _mm_nt_fwd(a, b):
    return _mm_nt(a, b), (a.astype(BF16), b.astype(BF16))


def _mm_nt_bwd(res, ct):
    a16, b16 = res
    c16 = ct.astype(BF16)
    return _dg(c16, b16, 1, 0), _dg(c16, a16, 0, 0)


_mm_nt.defvjp(_mm_nt_fwd, _mm_nt_bwd)


@jax.custom_vjp
def _mm_tn(a, b):
    return _dg(a.astype(BF16), b.astype(BF16), 0, 0)


def _mm_tn_fwd(a, b):
    return _mm_tn(a, b), (a.astype(BF16), b.astype(BF16))


def _mm_tn_bwd(res, ct):
    a16, b16 = res
    c16 = ct.astype(BF16)
    return _dg(b16, c16, 1, 1), _dg(a16, c16, 1, 0)


_mm_tn.defvjp(_mm_tn_fwd, _mm_tn_bwd)


def _split16(a):
    hi = a.astype(BF16)
    lo = (a - hi.astype(F32)).astype(BF16)
    return hi, lo


def _dot3(a, b, ca, cb):
    ah, al = _split16(a)
    bh, bl = _split16(b)
    return _dg(ah, bh, ca, cb) + (_dg(ah, bl, ca, cb) + _dg(al, bh, ca, cb))


def _tri_inv_impl(a):
    n = a.shape[-1]
    shp = (1,) * (a.ndim - 2) + (n, n)
    eye = (lax.broadcasted_iota(jnp.int32, shp, a.ndim - 2) == lax.broadcasted_iota(jnp.int32, shp, a.ndim - 1)).astype(F32)
    x = eye - a
    p = a
    for _ in range(5):
        p = _dot3(p, p, 1, 0)
        x = x + _dot3(x, p, 1, 0)
    return x


@jax.custom_vjp
def _tri_inv(a):
    return _tri_inv_impl(a)


def _tri_inv_fwd(a):
    t = _tri_inv_impl(a)
    return t, t


def _tri_inv_bwd(t, ct):
    return (-_dot3(_dot3(t, ct, 0, 0), t, 1, 1),)


_tri_inv.defvjp(_tri_inv_fwd, _tri_inv_bwd)


def _sigmoid(x):
    return 1.0 / (1.0 + jnp.exp(-x))


def _silu(x):
    return x * _sigmoid(x)


def _softplus(x):
    return jnp.maximum(x, 0.0) + jnp.log(1.0 + jnp.exp(-jnp.abs(x)))


def _rmsnorm(x, w):
    return x * lax.rsqrt(jnp.mean(x * x, axis=-1, keepdims=True) + NORM_EPS) * w


def _row_block(rows, cap):
    best = None
    for cand in range(SUBLANES, min(rows, cap) + 1, SUBLANES):
        if rows % cand == 0:
            best = cand
    assert best is not None, rows
    return best


def _mesh_peers():
    x, y, c = lax.axis_index("x"), lax.axis_index("y"), lax.axis_index("c")
    me = 4 * x + 2 * y + c
    peers = []
    for k in range(1, N_DEV):
        px = 1 - x if (k >> 2) & 1 else x
        py = 1 - y if (k >> 1) & 1 else y
        pc = 1 - c if k & 1 else c
        peers.append(((px, py, pc), 4 * px + 2 * py + pc))
    return me, peers


def _all_gather(shards):
    n_arr = len(shards)

    def body(*refs):
        in_refs, out_refs = refs[:n_arr], refs[n_arr:2 * n_arr]
        send_sems, recv_sems, loc_sems = refs[2 * n_arr:]
        me, peers = _mesh_peers()
        own = [pltpu.make_async_copy(in_refs[i], out_refs[i].at[me], loc_sems.at[i]) for i in range(n_arr)]
        for cp in own:
            cp.start()

        def copy(i, k, dev, slot):
            return pltpu.make_async_remote_copy(src_ref=in_refs[i], dst_ref=out_refs[i].at[slot],
                                                send_sem=send_sems.at[i * (N_DEV - 1) + k],
                                                recv_sem=recv_sems.at[i * (N_DEV - 1) + k],
                                                device_id=dev, device_id_type=MESH)

        sends = [copy(i, k, dev, me) for k, (dev, _) in enumerate(peers) for i in range(n_arr)]
        for cp in sends:
            cp.start()
        for k, (dev, pid) in enumerate(peers):
            for i in range(n_arr):
                copy(i, k, dev, pid).wait_recv()
        for cp in sends:
            cp.wait_send()
        for cp in own:
            cp.wait()

    any_spec = pl.BlockSpec(memory_space=pl.ANY)
    n_sem = n_arr * (N_DEV - 1)
    return pl.pallas_call(
        body, name="weights_all_gather",
        out_shape=tuple(jax.ShapeDtypeStruct((N_DEV,) + a.shape, a.dtype) for a in shards),
        in_specs=[any_spec] * n_arr, out_specs=tuple([any_spec] * n_arr),
        scratch_shapes=[pltpu.SemaphoreType.DMA((n_sem,)), pltpu.SemaphoreType.DMA((n_sem,)),
                        pltpu.SemaphoreType.DMA((n_arr,))],
    )(*shards)


def _reduce_scatter_exchange(slabs):
    n_arr = len(slabs)

    def body(*refs):
        in_refs, out_refs = refs[:n_arr], refs[n_arr:2 * n_arr]
        send_sems, recv_sems, loc_sems = refs[2 * n_arr:]
        me, peers = _mesh_peers()
        own = [pltpu.make_async_copy(in_refs[i].at[me], out_refs[i].at[me], loc_sems.at[i]) for i in range(n_arr)]
        for cp in own:
            cp.start()

        def copy(i, k, dev, src_slot, dst_slot):
            return pltpu.make_async_remote_copy(src_ref=in_refs[i].at[src_slot], dst_ref=out_refs[i].at[dst_slot],
                                                send_sem=send_sems.at[i * (N_DEV - 1) + k],
                                                recv_sem=recv_sems.at[i * (N_DEV - 1) + k],
                                                device_id=dev, device_id_type=MESH)

        sends = [copy(i, k, dev, pid, me) for k, (dev, pid) in enumerate(peers) for i in range(n_arr)]
        for cp in sends:
            cp.start()
        for k, (dev, pid) in enumerate(peers):
            for i in range(n_arr):
                copy(i, k, dev, me, pid).wait_recv()
        for cp in sends:
            cp.wait_send()
        for cp in own:
            cp.wait()

    any_spec = pl.BlockSpec(memory_space=pl.ANY)
    n_sem = n_arr * (N_DEV - 1)
    return pl.pallas_call(
        body, name="grads_reduce_scatter_exchange",
        out_shape=tuple(jax.ShapeDtypeStruct(a.shape, a.dtype) for a in slabs),
        in_specs=[any_spec] * n_arr, out_specs=tuple([any_spec] * n_arr),
        scratch_shapes=[pltpu.SemaphoreType.DMA((n_sem,)), pltpu.SemaphoreType.DMA((n_sem,)),
                        pltpu.SemaphoreType.DMA((n_arr,))],
    )(*slabs)


def _small_all_reduce(part):
    rows = part.shape[0]

    def body(p_ref, o_ref, buf_ref, send_sems, recv_sems):
        me, peers = _mesh_peers()
        buf_ref[me] = p_ref[...]
        sends = []
        for k, (dev, pid) in enumerate(peers):
            cp = pltpu.make_async_remote_copy(src_ref=p_ref, dst_ref=buf_ref.at[me], send_sem=send_sems.at[k],
                                              recv_sem=recv_sems.at[k], device_id=dev, device_id_type=MESH)
            cp.start()
            sends.append(cp)
        for k, (dev, pid) in enumerate(peers):
            pltpu.make_async_remote_copy(src_ref=p_ref, dst_ref=buf_ref.at[pid], send_sem=send_sems.at[k],
                                         recv_sem=recv_sems.at[k], device_id=dev, device_id_type=MESH).wait_recv()
        for cp in sends:
            cp.wait_send()
        acc = buf_ref[0]
        for i in range(1, N_DEV):
            acc = acc + buf_ref[i]
        o_ref[...] = acc

    vmem = pl.BlockSpec(memory_space=pltpu.VMEM)
    return pl.pallas_call(
        body, name="small_all_reduce",
        out_shape=jax.ShapeDtypeStruct(part.shape, F32),
        in_specs=[vmem], out_specs=vmem,
        scratch_shapes=[pltpu.VMEM((N_DEV, rows, LANES), F32), pltpu.SemaphoreType.DMA((N_DEV - 1,)),
                        pltpu.SemaphoreType.DMA((N_DEV - 1,))],
    )(part)


def _adamw_vals(w, g, m, v):
    m = ADAM_B1 * m + (1.0 - ADAM_B1) * g
    v = ADAM_B2 * v + (1.0 - ADAM_B2) * (g * g)
    m_hat = m / (1.0 - ADAM_B1 ** ADAM_STEP)
    v_hat = v / (1.0 - ADAM_B2 ** ADAM_STEP)
    delta = -ADAM_LR * (m_hat / (jnp.sqrt(v_hat) + ADAM_EPS) + ADAM_WD * w)
    return delta, m, v


def _adamw(contrib, w, m, v, name):
    n, rows, cols = contrib.shape
    tr = _row_block(rows, max(SUBLANES, (128 * 1024) // cols // SUBLANES * SUBLANES))

    def body(c_ref, w_ref, m_ref, v_ref, g_ref, d_ref, nm_ref, nv_ref):
        g = c_ref[0].astype(F32)
        for i in range(1, n):
            g = g + c_ref[i].astype(F32)
        delta, nm, nv = _adamw_vals(w_ref[...], g, m_ref[...], v_ref[...])
        g_ref[...] = g
        d_ref[...] = delta
        nm_ref[...] = nm
        nv_ref[...] = nv

    row = pl.BlockSpec((tr, cols), lambda i: (i, 0))
    shp = jax.ShapeDtypeStruct((rows, cols), F32)
    return pl.pallas_call(
        body, name=name, grid=(rows // tr,),
        in_specs=[pl.BlockSpec((n, tr, cols), lambda i: (0, i, 0)), row, row, row],
        out_specs=(row, row, row, row), out_shape=(shp, shp, shp, shp),
        compiler_params=_params("parallel"),
    )(contrib, w, m, v)


def _lane_block(n, cap):
    if n <= cap:
        return n
    best = None
    for cand in range(LANES, cap + 1, LANES):
        if n % cand == 0:
            best = cand
    assert best is not None, n
    return best


def _matmul(a, b, out_dtype, name, tm=1024, tn=1024, tk=1024):
    g, m, k = a.shape
    n = b.shape[2]
    tm, tn, tk = _lane_block(m, tm), _lane_block(n, tn), _lane_block(k, tk)
    nk = k // tk

    def body(a_ref, b_ref, o_ref, *acc):
        part = jnp.dot(a_ref[...], b_ref[...], preferred_element_type=F32)
        if nk == 1:
            o_ref[...] = part.astype(o_ref.dtype)
            return
        acc_ref, = acc
        kk = pl.program_id(3)

        @pl.when(kk == 0)
        def _():
            acc_ref[...] = part

        @pl.when((kk > 0) & (kk < nk - 1))
        def _():
            acc_ref[...] += part

        @pl.when(kk == nk - 1)
        def _():
            o_ref[...] = (acc_ref[...] + part).astype(o_ref.dtype)

    return pl.pallas_call(
        body, name=name, grid=(g, m // tm, n // tn, nk),
        in_specs=[pl.BlockSpec((None, tm, tk), lambda gi, i, j, kk: (gi, i, kk)),
                  pl.BlockSpec((None, tk, tn), lambda gi, i, j, kk: (gi, kk, j))],
        out_specs=pl.BlockSpec((None, tm, tn), lambda gi, i, j, kk: (gi, i, j)),
        out_shape=jax.ShapeDtypeStruct((g, m, n), out_dtype),
        scratch_shapes=[] if nk == 1 else [pltpu.VMEM((tm, tn), F32)],
        compiler_params=_params("parallel", "parallel", "parallel", "arbitrary"),
    )(a, b)


def _rms_fwd(x, w):
    s, d = x.shape
    tm = _row_block(s, 512)

    def body(x_ref, w_ref, h_ref):
        h_ref[...] = _rmsnorm(x_ref[...], w_ref[...]).astype(BF16)

    return pl.pallas_call(
        body, name="input_rmsnorm", grid=(s // tm,),
        in_specs=[pl.BlockSpec((tm, d), lambda i: (i, 0)), pl.BlockSpec((1, d), lambda i: (0, 0))],
        out_specs=pl.BlockSpec((tm, d), lambda i: (i, 0)),
        out_shape=jax.ShapeDtypeStruct((s, d), BF16),
        compiler_params=_params("parallel"),
    )(x, w)


def _rms_bwd(x, w, dh_parts, dx_res):
    s, d = x.shape
    tm = _row_block(s, 256)
    n_parts = len(dh_parts)

    def body(x_ref, w_ref, *rest):
        part_refs = rest[:n_parts]
        res_ref, gx_ref, gw_ref = rest[n_parts:]
        dh = part_refs[0][...]
        for r in part_refs[1:]:
            dh = dh + r[...]
        _, vjp = jax.vjp(_rmsnorm, x_ref[...], w_ref[...])
        dx, dw = vjp(dh)
        gx_ref[...] = dx + res_ref[...]

        @pl.when(pl.program_id(0) == 0)
        def _():
            gw_ref[...] = jnp.zeros_like(gw_ref)

        gw_ref[...] += dw

    row = pl.BlockSpec((tm, d), lambda i: (i, 0))
    vec = pl.BlockSpec((1, d), lambda i: (0, 0))
    return pl.pallas_call(
        body, name="input_rmsnorm_bwd", grid=(s // tm,),
        in_specs=[row, vec] + [row] * n_parts + [row],
        out_specs=(row, vec),
        out_shape=(jax.ShapeDtypeStruct((s, d), F32), jax.ShapeDtypeStruct((1, d), F32)),
        compiler_params=_params("arbitrary"),
    )(x, w, *dh_parts, dx_res)


def _lane_masks(rows):
    lane = lax.broadcasted_iota(jnp.int32, (rows, LANES), 1)
    return lane < HEAD_DIM, (lane & (HEAD_DIM - 1)) < HEAD_DIM // 2


def _swap_halves(t, lo_half):
    return jnp.where(lo_half, pltpu.roll(t, LANES - HEAD_DIM // 2, 1), pltpu.roll(t, HEAD_DIM // 2, 1))


def _rope(t, cos, sin_signed, lo_half):
    return t * cos + _swap_halves(t, lo_half) * sin_signed


def _rope_bwd(d, cos, sin_signed, lo_half):
    return d * cos - _swap_halves(d, lo_half) * sin_signed


def _window_mask(first):
    qi = lax.broadcasted_iota(jnp.int32, (ATT_BLOCK, 2 * ATT_BLOCK), 0)
    kj = lax.broadcasted_iota(jnp.int32, (ATT_BLOCK, 2 * ATT_BLOCK), 1)
    dist = qi + ATT_BLOCK - kj
    return (dist >= 0) & (dist <= N_BACK) & ((kj >= ATT_BLOCK) | jnp.logical_not(first))


def _blocks_per_subsequence(g, nb):
    return lax.shift_right_logical(jnp.int32(nb), 2 * g)


def _attn_fwd(qkv, cos, sin):
    _, s, _ = qkv.shape
    nb = s // ATT_BLOCK

    def body(qkv_ref, cos_ref, sin_ref, o_ref, lse_ref, kp_ref, vp_ref):
        g, t = pl.program_id(0), pl.program_id(1)
        first = (t & (_blocks_per_subsequence(g, nb) - 1)) == 0

        @pl.when(first)
        def _():
            kp_ref[...] = jnp.zeros_like(kp_ref)
            vp_ref[...] = jnp.zeros_like(vp_ref)

        cos_b, sin_b = cos_ref[...], sin_ref[...]
        head0, lo_half = _lane_masks(ATT_BLOCK)
        valid = _window_mask(first)
        for sl in range(WIDTH // LANES):
            cq = pl.ds(sl * LANES, LANES)
            ck = pl.ds(WIDTH + sl * LANES, LANES)
            cv = pl.ds(2 * WIDTH + sl * LANES, LANES)
            qr = (_rope(qkv_ref[:, cq], cos_b, sin_b, lo_half) * (HEAD_DIM ** -0.5)).astype(BF16)
            kr = _rope(qkv_ref[:, ck], cos_b, sin_b, lo_half).astype(BF16)
            v16 = qkv_ref[:, cv].astype(BF16)
            kcat = jnp.concatenate([kp_ref[:, cq], kr], axis=0)
            vcat = jnp.concatenate([vp_ref[:, cq], v16], axis=0)
            outs, lses = [], []
            for hm in (head0, jnp.logical_not(head0)):
                sc = _dg(jnp.where(hm, qr, jnp.zeros_like(qr)), kcat, 1, 1)
                sc = jnp.where(valid, sc, -jnp.inf)
                mx = jnp.max(sc, axis=1, keepdims=True)
                p = jnp.exp(sc - mx)
                den = jnp.sum(p, axis=1, keepdims=True)
                outs.append(_dg((p / den).astype(BF16), vcat, 1, 0))
                lses.append(mx + jnp.log(den))
            o_ref[:, cq] = jnp.where(head0, outs[0], outs[1])
            lse_ref[:, cq] = jnp.where(head0, lses[0], lses[1])
            kp_ref[:, cq] = kr
            vp_ref[:, cq] = v16

    blk = lambda w: pl.BlockSpec((None, ATT_BLOCK, w), lambda g, t: (g, t, 0))
    shp = jax.ShapeDtypeStruct((GROUPS, s, WIDTH), F32)
    return pl.pallas_call(
        body, name="dilated_attention_fwd", grid=(GROUPS, nb),
        in_specs=[blk(3 * WIDTH), blk(LANES), blk(LANES)],
        out_specs=(blk(WIDTH), blk(WIDTH)), out_shape=(shp, shp),
        scratch_shapes=[pltpu.VMEM((ATT_BLOCK, WIDTH), BF16), pltpu.VMEM((ATT_BLOCK, WIDTH), BF16)],
        compiler_params=_params("arbitrary", "arbitrary"),
    )(qkv, cos, sin)


def _attn_bwd(qkv, cos, sin, o, lse, do, dlse):
    _, s, _ = qkv.shape
    nb = s // ATT_BLOCK

    def body(qkv_ref, cos_ref, sin_ref, cosp_ref, sinp_ref, o_ref, lse_ref, do_ref, dlse_ref,
             dq_ref, dk_ref, dv_ref, kp_ref, vp_ref, dka_ref, dva_ref):
        g, t = pl.program_id(0), pl.program_id(1)
        first = (t & (_blocks_per_subsequence(g, nb) - 1)) == 0
        active = t < nb
        head0, lo_half = _lane_masks(ATT_BLOCK)
        head0_2, _ = _lane_masks(2 * ATT_BLOCK)
        cos_p, sin_p = cosp_ref[...], sinp_ref[...]

        @pl.when(t == 0)
        def _():
            dka_ref[...] = jnp.zeros_like(dka_ref)
            dva_ref[...] = jnp.zeros_like(dva_ref)

        @pl.when(active & first)
        def _():
            kp_ref[...] = jnp.zeros_like(kp_ref)
            vp_ref[...] = jnp.zeros_like(vp_ref)

        @pl.when(active)
        def _():
            cos_b, sin_b = cos_ref[...], sin_ref[...]
            valid = _window_mask(first)
            for sl in range(WIDTH // LANES):
                cq = pl.ds(sl * LANES, LANES)
                ck = pl.ds(WIDTH + sl * LANES, LANES)
                cv = pl.ds(2 * WIDTH + sl * LANES, LANES)
                qr = (_rope(qkv_ref[:, cq], cos_b, sin_b, lo_half) * (HEAD_DIM ** -0.5)).astype(BF16)
                kr = _rope(qkv_ref[:, ck], cos_b, sin_b, lo_half).astype(BF16)
                v16 = qkv_ref[:, cv].astype(BF16)
                kcat = jnp.concatenate([kp_ref[:, cq], kr], axis=0)
                vcat = jnp.concatenate([vp_ref[:, cq], v16], axis=0)
                do_b = do_ref[:, cq]
                do16 = do_b.astype(BF16)
                lse_b = lse_ref[:, cq]
                cterm = dlse_ref[:, cq] - do_b * o_ref[:, cq]
                dqs, dks, dvs = [], [], []
                for hm in (head0, jnp.logical_not(head0)):
                    sc = _dg(jnp.where(hm, qr, jnp.zeros_like(qr)), kcat, 1, 1)
                    sc = jnp.where(valid, sc, -jnp.inf)
                    lse_h = jnp.max(jnp.where(hm, lse_b, -jnp.inf), axis=1, keepdims=True)
                    p = jnp.exp(sc - lse_h)
                    dp = _dg(jnp.where(hm, do16, jnp.zeros_like(do16)), vcat, 1, 1)
                    c = jnp.sum(jnp.where(hm, cterm, 0.0), axis=1, keepdims=True)
                    ds16 = (p * (dp + c)).astype(BF16)
                    dvs.append(_dg(p.astype(BF16), do16, 0, 0))
                    dqs.append(_dg(ds16, kcat, 1, 0))
                    dks.append(_dg(ds16, qr, 0, 0))
                dq = jnp.where(head0, dqs[0], dqs[1]) * (HEAD_DIM ** -0.5)
                dq_ref[:, cq] = _rope_bwd(dq, cos_b, sin_b, lo_half).astype(BF16)
                dkc = jnp.where(head0_2, dks[0], dks[1])
                dvc = jnp.where(head0_2, dvs[0], dvs[1])
                dk_ref[:, cq] = _rope_bwd(dka_ref[:, cq] + dkc[:ATT_BLOCK], cos_p, sin_p, lo_half).astype(BF16)
                dv_ref[:, cq] = (dva_ref[:, cq] + dvc[:ATT_BLOCK]).astype(BF16)
                dka_ref[:, cq] = dkc[ATT_BLOCK:]
                dva_ref[:, cq] = dvc[ATT_BLOCK:]
                kp_ref[:, cq] = kr
                vp_ref[:, cq] = v16

        @pl.when(jnp.logical_not(active))
        def _():
            for sl in range(WIDTH // LANES):
                cq = pl.ds(sl * LANES, LANES)
                dk_ref[:, cq] = _rope_bwd(dka_ref[:, cq], cos_p, sin_p, lo_half).astype(BF16)
                dv_ref[:, cq] = dva_ref[:, cq].astype(BF16)

    cur = lambda w: pl.BlockSpec((None, ATT_BLOCK, w), lambda g, t: (g, jnp.minimum(t, nb - 1), 0))
    prev = lambda w: pl.BlockSpec((None, ATT_BLOCK, w), lambda g, t: (g, jnp.maximum(t - 1, 0), 0))
    shp = jax.ShapeDtypeStruct((GROUPS, s, WIDTH), BF16)
    return pl.pallas_call(
        body, name="dilated_attention_bwd", grid=(GROUPS, nb + 1),
        in_specs=[cur(3 * WIDTH), cur(LANES), cur(LANES), prev(LANES), prev(LANES),
                  cur(WIDTH), cur(WIDTH), cur(WIDTH), cur(WIDTH)],
        out_specs=(cur(WIDTH), prev(WIDTH), prev(WIDTH)), out_shape=(shp, shp, shp),
        scratch_shapes=[pltpu.VMEM((ATT_BLOCK, WIDTH), BF16), pltpu.VMEM((ATT_BLOCK, WIDTH), BF16),
                        pltpu.VMEM((ATT_BLOCK, WIDTH), F32), pltpu.VMEM((ATT_BLOCK, WIDTH), F32)],
        compiler_params=_params("arbitrary", "arbitrary"),
    )(qkv, cos, sin, cos, sin, o, lse, do, dlse)


CONV_PAD = SUBLANES


def _gdn_post(y, is_q, is_k):
    head0, _ = _lane_masks(y.shape[0])
    c = _silu(y)
    sq = c * c
    ss0 = jnp.sum(jnp.where(head0, sq, 0.0), axis=1, keepdims=True)
    ss1 = jnp.sum(jnp.where(head0, 0.0, sq), axis=1, keepdims=True)
    r = jnp.where(head0, lax.rsqrt(ss0 + NORM_EPS), lax.rsqrt(ss1 + NORM_EPS))
    scale = jnp.where(is_q, HEAD_DIM ** -0.5, 1.0).astype(F32)
    return jnp.where(is_q | is_k, c * r * scale, c)


def _conv_rows(xp_ref, w, c0, rows):
    y = w[0:1, :] * xp_ref[pl.ds(c0 + CONV_PAD - (CONV_K - 1), rows), :]
    for k in range(1, CONV_K):
        y = y + w[k:k + 1, :] * xp_ref[pl.ds(c0 + CONV_PAD - (CONV_K - 1) + k, rows), :]
    return y


def _gdn_pre_fwd(proj_r, conv8, col0):
    s = proj_r.shape[0]
    tr = _row_block(s, 512)
    nblk = QKV_B // LANES
    nq = WIDTH // LANES

    def body(x_ref, w_ref, out_ref, xp_ref):
        j = pl.program_id(0)
        is_q, is_k = j < nq, (j >= nq) & (j < 2 * nq)
        xp_ref[pl.ds(0, CONV_PAD), :] = jnp.zeros((CONV_PAD, LANES), F32)
        xp_ref[pl.ds(CONV_PAD, s), :] = x_ref[...]
        w = w_ref[...]
        for c in range(s // tr):
            out_ref[pl.ds(c * tr, tr), :] = _gdn_post(_conv_rows(xp_ref, w, c * tr, tr), is_q, is_k)

    return pl.pallas_call(
        body, name="gdn_conv_fwd", grid=(nblk,),
        in_specs=[pl.BlockSpec((s, LANES), lambda j: (0, col0 + j)), pl.BlockSpec((SUBLANES, LANES), lambda j: (0, j))],
        out_specs=pl.BlockSpec((s, LANES), lambda j: (0, j)),
        out_shape=jax.ShapeDtypeStruct((s, QKV_B), F32),
        scratch_shapes=[pltpu.VMEM((s + CONV_PAD, LANES), F32)],
        compiler_params=_params("parallel"),
    )(proj_r, conv8)


def _gdn_pre_bwd(proj_r, conv8, dc, col0):
    s = proj_r.shape[0]
    tr = _row_block(s, 512)
    nblk = QKV_B // LANES
    nq = WIDTH // LANES

    def body(x_ref, w_ref, dc_ref, dx_ref, dw_ref, xp_ref, dyp_ref):
        j = pl.program_id(0)
        is_q, is_k = j < nq, (j >= nq) & (j < 2 * nq)
        xp_ref[pl.ds(0, CONV_PAD), :] = jnp.zeros((CONV_PAD, LANES), F32)
        xp_ref[pl.ds(CONV_PAD, s), :] = x_ref[...]
        dyp_ref[pl.ds(s, CONV_PAD), :] = jnp.zeros((CONV_PAD, LANES), F32)
        w = w_ref[...]
        for c in range(s // tr):
            y = _conv_rows(xp_ref, w, c * tr, tr)
            _, vjp = jax.vjp(lambda yy: _gdn_post(yy, is_q, is_k), y)
            dyp_ref[pl.ds(c * tr, tr), :] = vjp(dc_ref[pl.ds(c * tr, tr), :])[0]
        dws = [jnp.zeros((1, LANES), F32) for _ in range(CONV_K)]
        for c in range(s // tr):
            c0 = c * tr
            dy = dyp_ref[pl.ds(c0, tr), :]
            dx = w[0:1, :] * dyp_ref[pl.ds(c0 + CONV_K - 1, tr), :]
            for k in range(1, CONV_K):
                dx = dx + w[k:k + 1, :] * dyp_ref[pl.ds(c0 + CONV_K - 1 - k, tr), :]
            dx_ref[pl.ds(c0, tr), :] = dx.astype(BF16)
            for k in range(CONV_K):
                xs = xp_ref[pl.ds(c0 + CONV_PAD - (CONV_K - 1) + k, tr), :]
                dws[k] = dws[k] + jnp.sum(dy * xs, axis=0, keepdims=True)
        row = lax.broadcasted_iota(jnp.int32, (SUBLANES, LANES), 0)
        dwb = jnp.zeros((SUBLANES, LANES), F32)
        for k in range(CONV_K):
            dwb = dwb + jnp.where(row == k, dws[k], 0.0)
        dw_ref[...] = dwb

    return pl.pallas_call(
        body, name="gdn_conv_bwd", grid=(nblk,),
        in_specs=[pl.BlockSpec((s, LANES), lambda j: (0, col0 + j)), pl.BlockSpec((SUBLANES, LANES), lambda j: (0, j)),
                  pl.BlockSpec((s, LANES), lambda j: (0, j))],
        out_specs=(pl.BlockSpec((s, LANES), lambda j: (0, j)), pl.BlockSpec((SUBLANES, LANES), lambda j: (0, j))),
        out_shape=(jax.ShapeDtypeStruct((s, QKV_B), BF16), jax.ShapeDtypeStruct((SUBLANES, QKV_B), F32)),
        scratch_shapes=[pltpu.VMEM((s + CONV_PAD, LANES), F32), pltpu.VMEM((s + CONV_PAD, LANES), F32)],
        compiler_params=_params("parallel"),
    )(proj_r, conv8, dc)


def _gdn_chunk(q, k, v, bcol, acol, alog, dtb, gnw, state):
    n = q.shape[-2]
    shp = (1, n, n)
    row = lax.broadcasted_iota(jnp.int32, shp, 1)
    col = lax.broadcasted_iota(jnp.int32, shp, 2)
    beta = _sigmoid(bcol)
    g = -jnp.exp(alog) * _softplus(acol + dtb)
    g_row = jnp.sum(jnp.where(row == col, g, 0.0), axis=-2, keepdims=True)
    big_g = jnp.sum(jnp.where(row >= col, g_row, 0.0), axis=-1, keepdims=True)
    big_g_row = jnp.sum(jnp.where(row <= col, g, 0.0), axis=-2, keepdims=True)
    decay_incl = jnp.exp(jnp.where(row >= col, big_g - big_g_row, -jnp.inf))
    decay_strict = jnp.where(row > col, decay_incl, 0.0)
    k_beta = k * beta
    t_inv = _tri_inv(_mm_nt(k_beta, k) * decay_strict)
    e_g = jnp.exp(big_g)
    u = _mm(t_inv, v * beta)
    w = _mm(t_inv, k_beta * e_g)
    attn = _mm_nt(q, k) * decay_incl
    v_new = u - _mm(w, state)
    o = _mm(q * e_g, state) + _mm(attn, v_new)
    total = jnp.sum(g, axis=-2, keepdims=True)
    new_state = state * jnp.exp(total) + _mm_tn(k * jnp.exp(total - big_g), v_new)
    return _rmsnorm(o, gnw), new_state


def _gdn_scan_fwd(gq, gk, gv, b_hm, a_hm, alog, dtb, gnw):
    _, s, _ = gq.shape
    nc = s // CHUNK

    def body(q_ref, k_ref, v_ref, b_ref, a_ref, al_ref, dt_ref, gnw_ref, o_ref, st_ref, state_ref):
        @pl.when(pl.program_id(0) == 0)
        def _():
            state_ref[...] = jnp.zeros_like(state_ref)

        st = state_ref[...]
        st_ref[...] = st
        o, new_st = _gdn_chunk(q_ref[...], k_ref[...], v_ref[...], b_ref[...], a_ref[...], al_ref[...], dt_ref[...],
                               gnw_ref[...], st)
        o_ref[...] = o
        state_ref[...] = new_st

    hm = pl.BlockSpec((HEADS, CHUNK, HEAD_DIM), lambda n: (0, n, 0))
    colv = pl.BlockSpec((HEADS, CHUNK, 1), lambda n: (0, n, 0))
    par = pl.BlockSpec((HEADS, 1, 1), lambda n: (0, 0, 0))
    return pl.pallas_call(
        body, name="gdn_scan_fwd", grid=(nc,),
        in_specs=[hm, hm, hm, colv, colv, par, par, pl.BlockSpec((1, 1, HEAD_DIM), lambda n: (0, 0, 0))],
        out_specs=(hm, pl.BlockSpec((None, HEADS, HEAD_DIM, HEAD_DIM), lambda n: (n, 0, 0, 0))),
        out_shape=(jax.ShapeDtypeStruct((HEADS, s, HEAD_DIM), F32),
                   jax.ShapeDtypeStruct((nc, HEADS, HEAD_DIM, HEAD_DIM), F32)),
        scratch_shapes=[pltpu.VMEM((HEADS, HEAD_DIM, HEAD_DIM), F32)],
        compiler_params=_params("arbitrary"),
    )(gq, gk, gv, b_hm, a_hm, alog, dtb, gnw)


def _gdn_scan_bwd(gq, gk, gv, b_hm, a_hm, alog, dtb, gnw, states, do):
    _, s, _ = gq.shape
    nc = s // CHUNK

    def body(q_ref, k_ref, v_ref, b_ref, a_ref, al_ref, dt_ref, gnw_ref, st_ref, do_ref,
             dq_ref, dk_ref, dv_ref, db_ref, da_ref, dal_ref, ddt_ref, dgnw_ref, dstate_ref):
        @pl.when(pl.program_id(0) == 0)
        def _():
            dstate_ref[...] = jnp.zeros_like(dstate_ref)
            dal_ref[...] = jnp.zeros_like(dal_ref)
            ddt_ref[...] = jnp.zeros_like(ddt_ref)
            dgnw_ref[...] = jnp.zeros_like(dgnw_ref)

        _, vjp = jax.vjp(_gdn_chunk, q_ref[...], k_ref[...], v_ref[...], b_ref[...], a_ref[...], al_ref[...],
                         dt_ref[...], gnw_ref[...], st_ref[...])
        dq, dk, dv, dbc, dac, dal, ddt, dgn, dst = vjp((do_ref[...], dstate_ref[...]))
        dq_ref[...] = dq
        dk_ref[...] = dk
        dv_ref[...] = dv
        db_ref[...] = dbc
        da_ref[...] = dac
        dstate_ref[...] = dst
        dal_ref[...] += dal
        ddt_ref[...] += ddt
        dgnw_ref[...] += dgn

    rev = lambda n: nc - 1 - n
    hm = pl.BlockSpec((HEADS, CHUNK, HEAD_DIM), lambda n: (0, rev(n), 0))
    colv = pl.BlockSpec((HEADS, CHUNK, 1), lambda n: (0, rev(n), 0))
    par = pl.BlockSpec((HEADS, 1, 1), lambda n: (0, 0, 0))
    vec = pl.BlockSpec((1, 1, HEAD_DIM), lambda n: (0, 0, 0))
    hm_shape = jax.ShapeDtypeStruct((HEADS, s, HEAD_DIM), F32)
    col_shape = jax.ShapeDtypeStruct((HEADS, s, 1), F32)
    par_shape = jax.ShapeDtypeStruct((HEADS, 1, 1), F32)
    return pl.pallas_call(
        body, name="gdn_scan_bwd", grid=(nc,),
        in_specs=[hm, hm, hm, colv, colv, par, par, vec,
                  pl.BlockSpec((None, HEADS, HEAD_DIM, HEAD_DIM), lambda n: (rev(n), 0, 0, 0)), hm],
        out_specs=(hm, hm, hm, colv, colv, par, par, vec),
        out_shape=(hm_shape, hm_shape, hm_shape, col_shape, col_shape, par_shape, par_shape,
                   jax.ShapeDtypeStruct((1, 1, HEAD_DIM), F32)),
        scratch_shapes=[pltpu.VMEM((HEADS, HEAD_DIM, HEAD_DIM), F32)],
        compiler_params=_params("arbitrary"),
    )(gq, gk, gv, b_hm, a_hm, alog, dtb, gnw, states, do)


def _tail_loss(x, tgt, o0, o1, o2, l0, l1, l2, ga, gb, za, zb, ob, wua, wub, wo, fnw):
    lm = jnp.maximum(jnp.maximum(l0, l1), l2)
    e0, e1, e2 = jnp.exp(l0 - lm), jnp.exp(l1 - lm), jnp.exp(l2 - lm)
    o_a = (e0 * o0 + e1 * o1 + e2 * o2) / (e0 + e1 + e2)
    y_a = _mm(o_a * _silu(za), wua)
    y_b = _mm(ob * _silu(zb), wub)
    merged = _sigmoid(ga) * y_a + _sigmoid(gb) * y_b
    y = _rmsnorm(x + _mm(merged, wo), fnw)
    err = y - tgt
    per_token = jnp.sum(err * err, axis=1, keepdims=True) * (0.5 / x.shape[1])
    return jnp.sum(per_token, axis=0, keepdims=True)


def _tail(x, tgt, og, lg, proj_r, ob, wua, wub, wo, fnw):
    s, d = x.shape
    tm = _row_block(s, 128)
    col_za = 2 * d // WIDTH
    col_zb = (2 * d + WIDTH + QKV_B) // WIDTH

    def body(x_ref, t_ref, o0_ref, o1_ref, o2_ref, l0_ref, l1_ref, l2_ref, ga_ref, gb_ref, za_ref, zb_ref, ob_ref,
             wua_ref, wub_ref, wo_ref, fnw_ref,
             loss_ref, dx_ref, do0_ref, do1_ref, do2_ref, dl0_ref, dl1_ref, dl2_ref, dga_ref, dgb_ref, dza_ref,
             dzb_ref, dob_ref, dwua_ref, dwub_ref, dwo_ref, dfnw_ref):
        @pl.when(pl.program_id(0) == 0)
        def _():
            for r in (loss_ref, dwua_ref, dwub_ref, dwo_ref, dfnw_ref):
                r[...] = jnp.zeros_like(r)

        args = (x_ref[...], t_ref[...], o0_ref[...], o1_ref[...], o2_ref[...], l0_ref[...], l1_ref[...], l2_ref[...],
                ga_ref[...], gb_ref[...], za_ref[...], zb_ref[...], ob_ref[...],
                wua_ref[...].astype(F32), wub_ref[...].astype(F32), wo_ref[...].astype(F32), fnw_ref[...])
        loss, vjp = jax.vjp(_tail_loss, *args)
        (dx, _, do0, do1, do2, dl0, dl1, dl2, dga, dgb, dza, dzb, dob, dwua, dwub, dwo, dfnw) = vjp(jnp.ones((1, 1), F32))
        loss_ref[...] += jnp.broadcast_to(loss, loss_ref.shape)
        dx_ref[...] = dx
        do0_ref[...], do1_ref[...], do2_ref[...] = do0, do1, do2
        dl0_ref[...], dl1_ref[...], dl2_ref[...] = dl0, dl1, dl2
        dga_ref[...] = dga.astype(BF16)
        dgb_ref[...] = dgb.astype(BF16)
        dza_ref[...] = dza.astype(BF16)
        dzb_ref[...] = dzb.astype(BF16)
        dob_ref[...] = dob
        dwua_ref[...] += dwua
        dwub_ref[...] += dwub
        dwo_ref[...] += dwo
        dfnw_ref[...] += dfnw

    row = lambda w, c=0: pl.BlockSpec((tm, w), lambda i: (i, c))
    full = lambda a, b: pl.BlockSpec((a, b), lambda i: (0, 0))
    f32 = lambda a, b: jax.ShapeDtypeStruct((a, b), F32)
    b16 = lambda a, b: jax.ShapeDtypeStruct((a, b), BF16)
    in_specs = ([row(d), row(d)] + [row(WIDTH)] * 6 + [row(d, 0), row(d, 1), row(WIDTH, col_za), row(WIDTH, col_zb),
                row(WIDTH), full(WIDTH, d), full(WIDTH, d), full(d, d), full(1, d)])
    out_specs = ([full(SUBLANES, LANES), row(d)] + [row(WIDTH)] * 6 + [row(d), row(d), row(WIDTH), row(WIDTH), row(WIDTH),
                 full(WIDTH, d), full(WIDTH, d), full(d, d), full(1, d)])
    out_shape = ([f32(SUBLANES, LANES), f32(s, d)] + [f32(s, WIDTH)] * 6 + [b16(s, d), b16(s, d), b16(s, WIDTH),
                 b16(s, WIDTH), f32(s, WIDTH), f32(WIDTH, d), f32(WIDTH, d), f32(d, d), f32(1, d)])
    return pl.pallas_call(
        body, name="tail_fwd_bwd", grid=(s // tm,),
        in_specs=in_specs, out_specs=tuple(out_specs), out_shape=tuple(out_shape),
        compiler_params=_params("arbitrary"),
    )(x, tgt, og[0], og[1], og[2], lg[0], lg[1], lg[2], proj_r, proj_r, proj_r, proj_r, ob, wua, wub, wo, fnw)


def _to_dilated(a, dil):
    if dil == 1:
        return a
    s = a.shape[0]
    return a.reshape(s // dil, dil, -1).transpose(1, 0, 2).reshape(a.shape)


def _from_dilated(a, dil):
    if dil == 1:
        return a
    s = a.shape[0]
    return a.reshape(dil, s // dil, -1).transpose(1, 0, 2).reshape(a.shape)


def _head_major(a):
    return a.reshape(a.shape[0], HEADS, HEAD_DIM).transpose(1, 0, 2)


def _from_head_major(a):
    return a.transpose(1, 0, 2).reshape(a.shape[1], WIDTH)


def _rope_tables(s):
    inv_freq = ROPE_THETA ** (-jnp.arange(0, HEAD_DIM, 2, dtype=F32) / HEAD_DIM)
    sign = jnp.where(jnp.arange(HEAD_DIM) < HEAD_DIM // 2, -1.0, 1.0).astype(F32)
    cos, sin = [], []
    for dil in DILATIONS:
        pos = _to_dilated(jnp.arange(s, dtype=F32)[:, None], dil)
        ang = pos * inv_freq[None, :]
        ang = jnp.concatenate([ang, ang], axis=-1)
        cos.append(jnp.tile(jnp.cos(ang), (1, 2)))
        sin.append(jnp.tile(jnp.sin(ang) * sign[None, :], (1, 2)))
    return jnp.stack(cos), jnp.stack(sin)


def _pack_rows(parts, dtype, row_multiple):
    flat = jnp.concatenate([p.reshape(-1).astype(dtype) for p in parts])
    tile = row_multiple * LANES
    pad = (-flat.shape[0]) % tile
    return jnp.pad(flat, (0, pad)).reshape(-1, LANES)


def _unpack_rows(packed, shapes):
    flat = packed.reshape(-1)
    out, start = [], 0
    for shp in shapes:
        size = 1
        for n in shp:
            size *= n
        out.append(flat[start:start + size].reshape(shp))
        start += size
    return out


def kernel(x, norm_w, w_in, conv_w, a_log, dt_bias, gdn_norm_w, w_up_a, w_up_b, w_out, final_norm_w, loss_target, m_norm_w, m_w_in, m_conv_w, m_a_log, m_dt_bias, m_gdn_norm_w, m_w_up_a, m_w_up_b, m_w_out, m_final_norm_w, v_norm_w, v_w_in, v_conv_w, v_a_log, v_dt_bias, v_gdn_norm_w, v_w_up_a, v_w_up_b, v_w_out, v_final_norm_w):
    x2, tgt = x[0], loss_target[0]
    s, d = x2.shape
    me = 4 * lax.axis_index("x") + 2 * lax.axis_index("y") + lax.axis_index("c")
    win8 = w_in.shape[2]
    conv8w = conv_w.shape[2]

    conv_shard = jnp.pad(conv_w[0], ((0, SUBLANES - CONV_K), (0, 0)))
    w_in_g, wua_g, wub_g, wo_g, conv_g = _all_gather(
        [w_in[0].astype(BF16), w_up_a[0].astype(BF16), w_up_b[0].astype(BF16), w_out[0].astype(BF16), conv_shard])
    w_in_f = jnp.concatenate([w_in_g[i] for i in range(N_DEV)], axis=1)
    wua = jnp.concatenate([wua_g[i] for i in range(N_DEV)], axis=1)
    wub = jnp.concatenate([wub_g[i] for i in range(N_DEV)], axis=1)
    wo = wo_g.reshape(d, d)
    conv8 = jnp.concatenate([conv_g[i] for i in range(N_DEV)], axis=1)

    w_qkv = w_in_f[:, :QKV_A].reshape(d, GROUPS, QKV_B).transpose(1, 0, 2)
    w_rest = jnp.concatenate([
        w_in_f[:, OFF_GATE:OFF_GATE + 2 * d], w_in_f[:, OFF_ZA:OFF_ZA + WIDTH], w_in_f[:, OFF_QKVB:OFF_QKVB + QKV_B],
        w_in_f[:, OFF_ZB:OFF_ZB + WIDTH], w_in_f[:, OFF_BA:OFF_BA + 2 * HEADS],
        jnp.zeros((d, BA_PAD - 2 * HEADS), BF16)], axis=1)
    col_qkvb = (2 * d + WIDTH) // LANES
    col_ba = (2 * d + 2 * WIDTH + QKV_B) // LANES

    h = _rms_fwd(x2, norm_w)
    h_all = jnp.stack([_to_dilated(h, dil) for dil in DILATIONS])
    qkv_all = _matmul(h_all, w_qkv, F32, "in_proj_attention")
    proj_r = _matmul(h[None], w_rest[None], F32, "in_proj_rest")[0]
    cos, sin = _rope_tables(s)
    o_all, lse_all = _attn_fwd(qkv_all, cos, sin)
    og = [_from_dilated(o_all[g], dil) for g, dil in enumerate(DILATIONS)]
    lg = [_from_dilated(lse_all[g], dil) for g, dil in enumerate(DILATIONS)]

    cqkv = _gdn_pre_fwd(proj_r, conv8, col_qkvb)
    gq, gk, gv = (_head_major(cqkv[:, i * WIDTH:(i + 1) * WIDTH]) for i in range(3))
    ba_off = col_ba * LANES
    ba_hm = proj_r[:, ba_off:ba_off + 2 * HEADS].T.reshape(2 * HEADS, s, 1)
    b_hm, a_hm = ba_hm[:HEADS], ba_hm[HEADS:]
    alog3, dtb3, gnw3 = a_log.reshape(HEADS, 1, 1), dt_bias.reshape(HEADS, 1, 1), gdn_norm_w.reshape(1, 1, HEAD_DIM)
    ob_hm, states = _gdn_scan_fwd(gq, gk, gv, b_hm, a_hm, alog3, dtb3, gnw3)
    ob = _from_head_major(ob_hm)

    (loss_blk, dx_res, do0, do1, do2, dl0, dl1, dl2, dga, dgb, dza, dzb, dob, dwua, dwub, dwo, dfnw) = _tail(
        x2, tgt, og, lg, proj_r, ob, wua, wub, wo, final_norm_w.reshape(1, d))

    do_all = jnp.stack([_to_dilated(t, dil) for t, dil in zip((do0, do1, do2), DILATIONS)])
    dl_all = jnp.stack([_to_dilated(t, dil) for t, dil in zip((dl0, dl1, dl2), DILATIONS)])
    dq_a, dk_a, dv_a = _attn_bwd(qkv_all, cos, sin, o_all, lse_all, do_all, dl_all)
    dqkv_all = jnp.concatenate([dq_a, dk_a, dv_a], axis=2)

    dgq, dgk, dgv, db_hm, da_hm, dalog3, ddtb3, dgnw3 = _gdn_scan_bwd(gq, gk, gv, b_hm, a_hm, alog3, dtb3, gnw3,
                                                                      states, _head_major(dob))
    dcqkv = jnp.concatenate([_from_head_major(t) for t in (dgq, dgk, dgv)], axis=1)
    dqkv_b, dconv8 = _gdn_pre_bwd(proj_r, conv8, dcqkv, col_qkvb)
    dba = jnp.concatenate([db_hm, da_hm], axis=0).reshape(2 * HEADS, s).T
    dproj_r = jnp.concatenate([dga, dgb, dza, dqkv_b, dzb,
                               jnp.pad(dba.astype(BF16), ((0, 0), (0, BA_PAD - 2 * HEADS)))], axis=1)

    ht_all = h_all.transpose(0, 2, 1)
    dw_qkv = _matmul(ht_all, dqkv_all, F32, "in_proj_attention_dw")
    dw_rest = _matmul(ht_all[0:1], dproj_r[None], F32, "in_proj_rest_dw")[0]
    dh_a = _matmul(dqkv_all, w_qkv.transpose(0, 2, 1), F32, "in_proj_attention_dh")
    dh_r = _matmul(dproj_r[None], w_rest.T[None], F32, "in_proj_rest_dh")[0]
    dh_parts = [dh_r] + [_from_dilated(dh_a[g], dil) for g, dil in enumerate(DILATIONS)]
    grad_x, dnorm_w = _rms_bwd(x2, norm_w, dh_parts, dx_res)

    o2 = 2 * d
    dw_in = jnp.concatenate([
        dw_qkv.transpose(1, 0, 2).reshape(d, QKV_A),
        dw_rest[:, o2:o2 + WIDTH], dw_rest[:, o2 + WIDTH:o2 + WIDTH + QKV_B],
        dw_rest[:, o2 + WIDTH + QKV_B:o2 + 2 * WIDTH + QKV_B],
        dw_rest[:, o2 + 2 * WIDTH + QKV_B:o2 + 2 * WIDTH + QKV_B + 2 * HEADS],
        dw_rest[:, :o2]], axis=1)

    def col_slabs(a, width):
        return jnp.stack([a[:, j * width:(j + 1) * width] for j in range(N_DEV)])

    contrib = _reduce_scatter_exchange([col_slabs(dw_in, win8), col_slabs(dwua, d // N_DEV), col_slabs(dwub, d // N_DEV),
                                        dwo.reshape(N_DEV, d // N_DEV, d)])

    small_parts = [dnorm_w, dfnw, dconv8[:CONV_K], dalog3[:, 0, 0], ddtb3[:, 0, 0], dgnw3[0], loss_blk[0, 0:1]]
    small_rows = [-(-p.size // LANES) for p in small_parts]
    small = jnp.concatenate([jnp.pad(p.reshape(-1), (0, r * LANES - p.size)).reshape(r, LANES)
                             for p, r in zip(small_parts, small_rows)])
    small = jnp.pad(small, ((0, (-small.shape[0]) % SUBLANES), (0, 0)))
    small_sum = _small_all_reduce(small)
    pieces, r0 = [], 0
    for p, r in zip(small_parts, small_rows):
        pieces.append(small_sum[r0:r0 + r].reshape(-1)[:p.size].reshape(p.shape))
        r0 += r
    g_norm_w, g_fnw, g_conv_full, g_alog, g_dtb, g_gnw, loss_sum = pieces
    g_conv = lax.dynamic_slice(g_conv_full, (0, me * conv8w), (CONV_K, conv8w))

    big = [_adamw(c, w[0], m[0], v[0], name) for c, w, m, v, name in (
        (contrib[0], w_in, m_w_in, v_w_in, "adamw_w_in"), (contrib[1], w_up_a, m_w_up_a, v_w_up_a, "adamw_w_up_a"),
        (contrib[2], w_up_b, m_w_up_b, v_w_up_b, "adamw_w_up_b"), (contrib[3], w_out, m_w_out, v_w_out, "adamw_w_out"))]
    g_big, d_big, nm_big, nv_big = ([t[i] for t in big] for i in range(4))

    small_ws = [norm_w, final_norm_w, conv_w, a_log, dt_bias, gdn_norm_w]
    small_ms = [m_norm_w, m_final_norm_w, m_conv_w, m_a_log, m_dt_bias, m_gdn_norm_w]
    small_vs = [v_norm_w, v_final_norm_w, v_conv_w, v_a_log, v_dt_bias, v_gdn_norm_w]
    small_gs = [g_norm_w, g_fnw, g_conv, g_alog, g_dtb, g_gnw]
    small_shapes = [t.shape for t in small_ws]
    sm = _adamw(_pack_rows(small_gs, F32, SUBLANES)[None], _pack_rows(small_ws, F32, SUBLANES),
                _pack_rows(small_ms, F32, SUBLANES), _pack_rows(small_vs, F32, SUBLANES), "adamw_small")
    g_sm, d_sm, nm_sm, nv_sm = (_unpack_rows(t, small_shapes) for t in sm)

    def ordered(bigs, smalls):
        nw, fnw_, cw, al, dtb, gn = smalls
        wi, ua, ub, wo_ = (t[None] for t in bigs)
        return [nw, wi, cw, al, dtb, gn, ua, ub, wo_, fnw_]

    return (loss_sum.reshape(()), grad_x[None], *ordered(g_big, g_sm), *ordered(d_big, d_sm),
            *ordered(nm_big, nm_sm), *ordered(nv_big, nv_sm))
```

```python
import functools

import jax
import jax.numpy as jnp
from jax import lax
from jax.experimental import pallas as pl
from jax.experimental.pallas import tpu as pltpu

F32 = jnp.float32
BF16 = jnp.bfloat16
MESH = pl.DeviceIdType.MESH
N_DEV = 8
LANES = 128
SUBLANES = 8

GROUPS = 3
HEADS = 8
HEAD_DIM = 64
WIDTH = HEADS * HEAD_DIM
ATT_BLOCK = 128
DILATIONS = (1, 4, 16)
N_BACK = 128
CONV_K = 4
CHUNK = 64
QKV_B = 3 * WIDTH
QKV_A = GROUPS * 3 * WIDTH
BA_PAD = 512
NORM_EPS = 1e-6
ROPE_THETA = 10000.0
ADAM_LR, ADAM_B1, ADAM_B2, ADAM_EPS, ADAM_WD, ADAM_STEP = 0.001, 0.9, 0.999, 1e-08, 0.01, 10

VMEM_LIMIT = 56 * 1024 * 1024

OFF_ZA = QKV_A
OFF_QKVB = OFF_ZA + WIDTH
OFF_ZB = OFF_QKVB + QKV_B
OFF_BA = OFF_ZB + WIDTH
OFF_GATE = OFF_BA + 2 * HEADS


def _params(*sem):
    return pltpu.CompilerParams(dimension_semantics=sem, vmem_limit_bytes=VMEM_LIMIT)


def _dg(a, b, ca, cb):
    nb = a.ndim - 2
    batch = tuple(range(nb))
    return lax.dot_general(a, b, (((nb + ca,), (nb + cb,)), (batch, batch)), preferred_element_type=F32)


@jax.custom_vjp
def _mm(a, b):
    return _dg(a.astype(BF16), b.astype(BF16), 1, 0)


def _mm_fwd(a, b):
    return _mm(a, b), (a.astype(BF16), b.astype(BF16))


def _mm_bwd(res, ct):
    a16, b16 = res
    c16 = ct.astype(BF16)
    return _dg(c16, b16, 1, 1), _dg(a16, c16, 0, 0)


_mm.defvjp(_mm_fwd, _mm_bwd)


@jax.custom_vjp
def _mm_nt(a, b):
    return _dg(a.astype(BF16), b.astype(BF16), 1, 1)


def _mm_nt_fwd(a, b):
    return _mm_nt(a, b), (a.astype(BF16), b.astype(BF16))


def _mm_nt_bwd(res, ct):
    a16, b16 = res
    c16 = ct.astype(BF16)
    return _dg(c16, b16, 1, 0), _dg(c16, a16, 0, 0)


_mm_nt.defvjp(_mm_nt_fwd, _mm_nt_bwd)


@jax.custom_vjp
def _mm_tn(a, b):
    return _dg(a.astype(BF16), b.astype(BF16), 0, 0)


def _mm_tn_fwd(a, b):
    return _mm_tn(a, b), (a.astype(BF16), b.astype(BF16))


def _mm_tn_bwd(res, ct):
    a16, b16 = res
    c16 = ct.astype(BF16)
    return _dg(b16, c16, 1, 1), _dg(a16, c16, 1, 0)


_mm_tn.defvjp(_mm_tn_fwd, _mm_tn_bwd)


def _split16(a):
    hi = a.astype(BF16)
    lo = (a - hi.astype(F32)).astype(BF16)
    return hi, lo


def _dot3(a, b, ca, cb):
    ah, al = _split16(a)
    bh, bl = _split16(b)
    return _dg(ah, bh, ca, cb) + (_dg(ah, bl, ca, cb) + _dg(al, bh, ca, cb))


def _tri_inv_impl(a):
    n = a.shape[-1]
    shp = (1,) * (a.ndim - 2) + (n, n)
    eye = (lax.broadcasted_iota(jnp.int32, shp, a.ndim - 2) == lax.broadcasted_iota(jnp.int32, shp, a.ndim - 1)).astype(F32)
    x = eye - a
    p = a
    for _ in range(5):
        p = _dot3(p, p, 1, 0)
        x = x + _dot3(x, p, 1, 0)
    return x


@jax.custom_vjp
def _tri_inv(a):
    return _tri_inv_impl(a)


def _tri_inv_fwd(a):
    t = _tri_inv_impl(a)
    return t, t


def _tri_inv_bwd(t, ct):
    return (-_dot3(_dot3(t, ct, 0, 0), t, 1, 1),)


_tri_inv.defvjp(_tri_inv_fwd, _tri_inv_bwd)


def _sigmoid(x):
    return 1.0 / (1.0 + jnp.exp(-x))


def _silu(x):
    return x * _sigmoid(x)


def _softplus(x):
    return jnp.maximum(x, 0.0) + jnp.log(1.0 + jnp.exp(-jnp.abs(x)))


def _rmsnorm(x, w):
    return x * lax.rsqrt(jnp.mean(x * x, axis=-1, keepdims=True) + NORM_EPS) * w


def _row_block(rows, cap):
    best = None
    for cand in range(SUBLANES, min(rows, cap) + 1, SUBLANES):
        if rows % cand == 0:
            best = cand
    assert best is not None, rows
    return best


def _mesh_peers():
    x, y, c = lax.axis_index("x"), lax.axis_index("y"), lax.axis_index("c")
    me = 4 * x + 2 * y + c
    peers = []
    for k in range(1, N_DEV):
        px = 1 - x if (k >> 2) & 1 else x
        py = 1 - y if (k >> 1) & 1 else y
        pc = 1 - c if k & 1 else c
        peers.append(((px, py, pc), 4 * px + 2 * py + pc))
    return me, peers


N_CHIPS = 4
OTHER_CHIPS = 3


def _chip_peers():
    x, y, c = lax.axis_index("x"), lax.axis_index("y"), lax.axis_index("c")
    return x, y, c, [(1 - x, y), (x, 1 - y), (1 - x, 1 - y)]


def _all_gather(shards):
    n_arr = len(shards)
    per = 1 + 2 * OTHER_CHIPS

    def body(*refs):
        in_refs, out_refs = refs[:n_arr], refs[n_arr:2 * n_arr]
        send_sems, recv_sems, loc_sems = refs[2 * n_arr:]
        x, y, c, chips = _chip_peers()
        me, sibling = (x, y, c), (x, y, 1 - c)

        def slot(px, py, pc):
            return 4 * px + 2 * py + pc

        def copy(i, k, block, to, src=None):
            dst = out_refs[i].at[slot(*block)]
            return pltpu.make_async_remote_copy(src_ref=dst if src is None else src, dst_ref=dst,
                                                send_sem=send_sems.at[i * per + k], recv_sem=recv_sems.at[i * per + k],
                                                device_id=to, device_id_type=MESH)

        own = [pltpu.make_async_copy(in_refs[i], out_refs[i].at[slot(*me)], loc_sems.at[i]) for i in range(n_arr)]
        for cp in own:
            cp.start()
        first = []
        for i in range(n_arr):
            first += [copy(i, 1 + j, me, (*chip, c), src=in_refs[i]) for j, chip in enumerate(chips)]
            first.append(copy(i, 0, me, sibling, src=in_refs[i]))
        for cp in first:
            cp.start()
        passed = []
        for j, chip in enumerate(chips):
            for i in range(n_arr):
                copy(i, 1 + j, (*chip, c), me).wait_recv()
                fwd = copy(i, 1 + OTHER_CHIPS + j, (*chip, c), sibling)
                fwd.start()
                passed.append(fwd)
        for i in range(n_arr):
            copy(i, 0, sibling, me).wait_recv()
            for j, chip in enumerate(chips):
                copy(i, 1 + OTHER_CHIPS + j, (*chip, 1 - c), me).wait_recv()
        for cp in first + passed:
            cp.wait_send()
        for cp in own:
            cp.wait()

    any_spec = pl.BlockSpec(memory_space=pl.ANY)
    return pl.pallas_call(
        body, name="weights_all_gather",
        out_shape=tuple(jax.ShapeDtypeStruct((N_DEV,) + a.shape, a.dtype) for a in shards),
        in_specs=[any_spec] * n_arr, out_specs=tuple([any_spec] * n_arr),
        scratch_shapes=[pltpu.SemaphoreType.DMA((n_arr * per,)), pltpu.SemaphoreType.DMA((n_arr * per,)),
                        pltpu.SemaphoreType.DMA((n_arr,))],
    )(*shards)


def _sibling_exchange(slabs):
    n_arr = len(slabs)

    def body(*refs):
        in_refs, out_refs = refs[:n_arr], refs[n_arr:2 * n_arr]
        send_sems, recv_sems = refs[2 * n_arr:]
        x, y, c, _ = _chip_peers()
        sends = []
        for i in range(n_arr):
            for q in range(N_CHIPS):
                cp = pltpu.make_async_remote_copy(src_ref=in_refs[i].at[2 * q + (1 - c)], dst_ref=out_refs[i].at[q],
                                                  send_sem=send_sems.at[i * N_CHIPS + q],
                                                  recv_sem=recv_sems.at[i * N_CHIPS + q],
                                                  device_id=(x, y, 1 - c), device_id_type=MESH)
                cp.start()
                sends.append(cp)
        for cp in sends:
            cp.wait_recv()
        for cp in sends:
            cp.wait_send()

    any_spec = pl.BlockSpec(memory_space=pl.ANY)
    return pl.pallas_call(
        body, name="grads_sibling_exchange",
        out_shape=tuple(jax.ShapeDtypeStruct((N_CHIPS,) + a.shape[1:], a.dtype) for a in slabs),
        in_specs=[any_spec] * n_arr, out_specs=tuple([any_spec] * n_arr),
        scratch_shapes=[pltpu.SemaphoreType.DMA((n_arr * N_CHIPS,)), pltpu.SemaphoreType.DMA((n_arr * N_CHIPS,))],
    )(*slabs)


def _pair_sum(slabs, from_sibling, core, name):
    _, rows, cols = slabs.shape
    tr = _row_block(rows, max(SUBLANES, (256 * 1024) // cols // SUBLANES * SUBLANES))

    def body(core_ref, a_ref, b_ref, o_ref):
        o_ref[...] = (a_ref[...] + b_ref[...]).astype(BF16)

    grid_spec = pltpu.PrefetchScalarGridSpec(
        num_scalar_prefetch=1, grid=(N_CHIPS, rows // tr),
        in_specs=[pl.BlockSpec((None, tr, cols), lambda q, r, core_ref: (2 * q + core_ref[0], r, 0)),
                  pl.BlockSpec((None, tr, cols), lambda q, r, core_ref: (q, r, 0))],
        out_specs=pl.BlockSpec((None, tr, cols), lambda q, r, core_ref: (q, r, 0)))
    return pl.pallas_call(
        body, name=name, grid_spec=grid_spec,
        out_shape=jax.ShapeDtypeStruct((N_CHIPS, rows, cols), BF16),
        compiler_params=_params("parallel", "parallel"),
    )(core, slabs, from_sibling)


def _chip_exchange(partials):
    n_arr = len(partials)

    def body(*refs):
        in_refs, out_refs = refs[:n_arr], refs[n_arr:2 * n_arr]
        send_sems, recv_sems, loc_sems = refs[2 * n_arr:]
        x, y, c, chips = _chip_peers()
        mine = 2 * x + y
        own = [pltpu.make_async_copy(in_refs[i].at[mine], out_refs[i].at[mine], loc_sems.at[i]) for i in range(n_arr)]
        for cp in own:
            cp.start()

        def copy(i, j, chip, src_slot, dst_slot):
            return pltpu.make_async_remote_copy(src_ref=in_refs[i].at[src_slot], dst_ref=out_refs[i].at[dst_slot],
                                                send_sem=send_sems.at[i * OTHER_CHIPS + j],
                                                recv_sem=recv_sems.at[i * OTHER_CHIPS + j],
                                                device_id=(*chip, c), device_id_type=MESH)

        sends = [copy(i, j, chip, 2 * chip[0] + chip[1], mine) for j, chip in enumerate(chips) for i in range(n_arr)]
        for cp in sends:
            cp.start()
        for j, chip in enumerate(chips):
            for i in range(n_arr):
                copy(i, j, chip, mine, 2 * chip[0] + chip[1]).wait_recv()
        for cp in sends:
            cp.wait_send()
        for cp in own:
            cp.wait()

    any_spec = pl.BlockSpec(memory_space=pl.ANY)
    return pl.pallas_call(
        body, name="grads_chip_exchange",
        out_shape=tuple(jax.ShapeDtypeStruct(a.shape, a.dtype) for a in partials),
        in_specs=[any_spec] * n_arr, out_specs=tuple([any_spec] * n_arr),
        scratch_shapes=[pltpu.SemaphoreType.DMA((n_arr * OTHER_CHIPS,)), pltpu.SemaphoreType.DMA((n_arr * OTHER_CHIPS,)),
                        pltpu.SemaphoreType.DMA((n_arr,))],
    )(*partials)


def _small_all_reduce(part):
    rows = part.shape[0]

    def body(p_ref, o_ref, buf_ref, send_sems, recv_sems):
        me, peers = _mesh_peers()
        buf_ref[me] = p_ref[...]
        sends = []
        for k, (dev, pid) in enumerate(peers):
            cp = pltpu.make_async_remote_copy(src_ref=p_ref, dst_ref=buf_ref.at[me], send_sem=send_sems.at[k],
                                              recv_sem=recv_sems.at[k], device_id=dev, device_id_type=MESH)
            cp.start()
            sends.append(cp)
        for k, (dev, pid) in enumerate(peers):
            pltpu.make_async_remote_copy(src_ref=p_ref, dst_ref=buf_ref.at[pid], send_sem=send_sems.at[k],
                                         recv_sem=recv_sems.at[k], device_id=dev, device_id_type=MESH).wait_recv()
        for cp in sends:
            cp.wait_send()
        acc = buf_ref[0]
        for i in range(1, N_DEV):
            acc = acc + buf_ref[i]
        o_ref[...] = acc

    vmem = pl.BlockSpec(memory_space=pltpu.VMEM)
    return pl.pallas_call(
        body, name="small_all_reduce",
        out_shape=jax.ShapeDtypeStruct(part.shape, F32),
        in_specs=[vmem], out_specs=vmem,
        scratch_shapes=[pltpu.VMEM((N_DEV, rows, LANES), F32), pltpu.SemaphoreType.DMA((N_DEV - 1,)),
                        pltpu.SemaphoreType.DMA((N_DEV - 1,))],
    )(part)


def _adamw_vals(w, g, m, v):
    m = ADAM_B1 * m + (1.0 - ADAM_B1) * g
    v = ADAM_B2 * v + (1.0 - ADAM_B2) * (g * g)
    m_hat = m / (1.0 - ADAM_B1 ** ADAM_STEP)
    v_hat = v / (1.0 - ADAM_B2 ** ADAM_STEP)
    delta = -ADAM_LR * (m_hat / (jnp.sqrt(v_hat) + ADAM_EPS) + ADAM_WD * w)
    return delta, m, v


def _adamw(contrib, w, m, v, name):
    n, rows, cols = contrib.shape
    tr = _row_block(rows, max(SUBLANES, (128 * 1024) // cols // SUBLANES * SUBLANES))

    def body(c_ref, w_ref, m_ref, v_ref, g_ref, d_ref, nm_ref, nv_ref):
        g = c_ref[0].astype(F32)
        for i in range(1, n):
            g = g + c_ref[i].astype(F32)
        delta, nm, nv = _adamw_vals(w_ref[...], g, m_ref[...], v_ref[...])
        g_ref[...] = g
        d_ref[...] = delta
        nm_ref[...] = nm
        nv_ref[...] = nv

    row = pl.BlockSpec((tr, cols), lambda i: (i, 0))
    shp = jax.ShapeDtypeStruct((rows, cols), F32)
    return pl.pallas_call(
        body, name=name, grid=(rows // tr,),
        in_specs=[pl.BlockSpec((n, tr, cols), lambda i: (0, i, 0)), row, row, row],
        out_specs=(row, row, row, row), out_shape=(shp, shp, shp, shp),
        compiler_params=_params("parallel"),
    )(contrib, w, m, v)


def _lane_block(n, cap):
    if n <= cap:
        return n
    best = None
    for cand in range(LANES, cap + 1, LANES):
        if n % cand == 0:
            best = cand
    assert best is not None, n
    return best


def _matmul(a, b, out_dtype, name, mode="nn", tm=1024, tn=1024, tk=1024):
    g = a.shape[0]
    m, k = (a.shape[2], a.shape[1]) if mode == "tn" else (a.shape[1], a.shape[2])
    n = b.shape[1] if mode == "nt" else b.shape[2]
    tm, tn, tk = _lane_block(m, tm), _lane_block(n, tn), _lane_block(k, tk)
    nk = k // tk
    a_spec = (pl.BlockSpec((None, tk, tm), lambda gi, i, j, kk: (gi, kk, i)) if mode == "tn" else
              pl.BlockSpec((None, tm, tk), lambda gi, i, j, kk: (gi, i, kk)))
    b_spec = (pl.BlockSpec((None, tn, tk), lambda gi, i, j, kk: (gi, j, kk)) if mode == "nt" else
              pl.BlockSpec((None, tk, tn), lambda gi, i, j, kk: (gi, kk, j)))
    ca, cb = (0 if mode == "tn" else 1), (1 if mode == "nt" else 0)

    def body(a_ref, b_ref, o_ref, *acc):
        part = _dg(a_ref[...], b_ref[...], ca, cb)
        if nk == 1:
            o_ref[...] = part.astype(o_ref.dtype)
            return
        acc_ref, = acc
        kk = pl.program_id(3)

        @pl.when(kk == 0)
        def _():
            acc_ref[...] = part

        @pl.when((kk > 0) & (kk < nk - 1))
        def _():
            acc_ref[...] += part

        @pl.when(kk == nk - 1)
        def _():
            o_ref[...] = (acc_ref[...] + part).astype(o_ref.dtype)

    return pl.pallas_call(
        body, name=name, grid=(g, m // tm, n // tn, nk),
        in_specs=[a_spec, b_spec],
        out_specs=pl.BlockSpec((None, tm, tn), lambda gi, i, j, kk: (gi, i, j)),
        out_shape=jax.ShapeDtypeStruct((g, m, n), out_dtype),
        scratch_shapes=[] if nk == 1 else [pltpu.VMEM((tm, tn), F32)],
        compiler_params=_params("parallel", "parallel", "parallel", "arbitrary"),
    )(a, b)


def _rms_fwd(x, w):
    s, d = x.shape
    tm = _row_block(s, 512)

    def body(x_ref, w_ref, h_ref):
        h_ref[...] = _rmsnorm(x_ref[...], w_ref[...]).astype(BF16)

    return pl.pallas_call(
        body, name="input_rmsnorm", grid=(s // tm,),
        in_specs=[pl.BlockSpec((tm, d), lambda i: (i, 0)), pl.BlockSpec((1, d), lambda i: (0, 0))],
        out_specs=pl.BlockSpec((tm, d), lambda i: (i, 0)),
        out_shape=jax.ShapeDtypeStruct((s, d), BF16),
        compiler_params=_params("parallel"),
    )(x, w)


def _rms_bwd(x, w, dh_parts, dx_res):
    s, d = x.shape
    tm = _row_block(s, 256)
    n_parts = len(dh_parts)

    def body(x_ref, w_ref, *rest):
        part_refs = rest[:n_parts]
        res_ref, gx_ref, gw_ref = rest[n_parts:]
        dh = part_refs[0][...]
        for r in part_refs[1:]:
            dh = dh + r[...]
        _, vjp = jax.vjp(_rmsnorm, x_ref[...], w_ref[...])
        dx, dw = vjp(dh)
        gx_ref[...] = dx + res_ref[...]

        @pl.when(pl.program_id(0) == 0)
        def _():
            gw_ref[...] = jnp.zeros_like(gw_ref)

        gw_ref[...] += dw

    row = pl.BlockSpec((tm, d), lambda i: (i, 0))
    vec = pl.BlockSpec((1, d), lambda i: (0, 0))
    return pl.pallas_call(
        body, name="input_rmsnorm_bwd", grid=(s // tm,),
        in_specs=[row, vec] + [row] * n_parts + [row],
        out_specs=(row, vec),
        out_shape=(jax.ShapeDtypeStruct((s, d), F32), jax.ShapeDtypeStruct((1, d), F32)),
        compiler_params=_params("arbitrary"),
    )(x, w, *dh_parts, dx_res)


def _lane_masks(rows):
    lane = lax.broadcasted_iota(jnp.int32, (rows, LANES), 1)
    return lane < HEAD_DIM, (lane & (HEAD_DIM - 1)) < HEAD_DIM // 2


def _swap_halves(t, lo_half):
    return jnp.where(lo_half, pltpu.roll(t, LANES - HEAD_DIM // 2, 1), pltpu.roll(t, HEAD_DIM // 2, 1))


def _rope(t, cos, sin_signed, lo_half):
    return t * cos + _swap_halves(t, lo_half) * sin_signed


def _rope_bwd(d, cos, sin_signed, lo_half):
    return d * cos - _swap_halves(d, lo_half) * sin_signed


def _window_mask(first):
    qi = lax.broadcasted_iota(jnp.int32, (ATT_BLOCK, 2 * ATT_BLOCK), 0)
    kj = lax.broadcasted_iota(jnp.int32, (ATT_BLOCK, 2 * ATT_BLOCK), 1)
    dist = qi + ATT_BLOCK - kj
    return (dist >= 0) & (dist <= N_BACK) & ((kj >= ATT_BLOCK) | jnp.logical_not(first))


def _blocks_per_subsequence(g, nb):
    return lax.shift_right_logical(jnp.int32(nb), 2 * g)


def _attn_fwd(qkv, cos, sin):
    _, s, _ = qkv.shape
    nb = s // ATT_BLOCK

    def body(qkv_ref, cos_ref, sin_ref, o_ref, lse_ref, kp_ref, vp_ref):
        g, t = pl.program_id(0), pl.program_id(1)
        first = (t & (_blocks_per_subsequence(g, nb) - 1)) == 0

        @pl.when(first)
        def _():
            kp_ref[...] = jnp.zeros_like(kp_ref)
            vp_ref[...] = jnp.zeros_like(vp_ref)

        cos_b, sin_b = cos_ref[...], sin_ref[...]
        head0, lo_half = _lane_masks(ATT_BLOCK)
        valid = _window_mask(first)
        for sl in range(WIDTH // LANES):
            cq = pl.ds(sl * LANES, LANES)
            ck = pl.ds(WIDTH + sl * LANES, LANES)
            cv = pl.ds(2 * WIDTH + sl * LANES, LANES)
            qr = (_rope(qkv_ref[:, cq], cos_b, sin_b, lo_half) * (HEAD_DIM ** -0.5)).astype(BF16)
            kr = _rope(qkv_ref[:, ck], cos_b, sin_b, lo_half).astype(BF16)
            v16 = qkv_ref[:, cv].astype(BF16)
            kcat = jnp.concatenate([kp_ref[:, cq], kr], axis=0)
            vcat = jnp.concatenate([vp_ref[:, cq], v16], axis=0)
            outs, lses = [], []
            for hm in (head0, jnp.logical_not(head0)):
                sc = _dg(jnp.where(hm, qr, jnp.zeros_like(qr)), kcat, 1, 1)
                sc = jnp.where(valid, sc, -jnp.inf)
                mx = jnp.max(sc, axis=1, keepdims=True)
                p = jnp.exp(sc - mx)
                den = jnp.sum(p, axis=1, keepdims=True)
                outs.append(_dg((p / den).astype(BF16), vcat, 1, 0))
                lses.append(mx + jnp.log(den))
            o_ref[:, cq] = jnp.where(head0, outs[0], outs[1])
            lse_ref[:, cq] = jnp.where(head0, lses[0], lses[1])
            kp_ref[:, cq] = kr
            vp_ref[:, cq] = v16

    blk = lambda w: pl.BlockSpec((None, ATT_BLOCK, w), lambda g, t: (g, t, 0))
    shp = jax.ShapeDtypeStruct((GROUPS, s, WIDTH), F32)
    return pl.pallas_call(
        body, name="dilated_attention_fwd", grid=(GROUPS, nb),
        in_specs=[blk(3 * WIDTH), blk(LANES), blk(LANES)],
        out_specs=(blk(WIDTH), blk(WIDTH)), out_shape=(shp, shp),
        scratch_shapes=[pltpu.VMEM((ATT_BLOCK, WIDTH), BF16), pltpu.VMEM((ATT_BLOCK, WIDTH), BF16)],
        compiler_params=_params("arbitrary", "arbitrary"),
    )(qkv, cos, sin)


def _attn_bwd(qkv, cos, sin, o, lse, do, dlse):
    _, s, _ = qkv.shape
    nb = s // ATT_BLOCK

    def body(qkv_ref, cos_ref, sin_ref, cosp_ref, sinp_ref, o_ref, lse_ref, do_ref, dlse_ref,
             dqkv_ref, kp_ref, vp_ref, dka_ref, dva_ref, dqp_ref):
        g, t = pl.program_id(0), pl.program_id(1)
        first = (t & (_blocks_per_subsequence(g, nb) - 1)) == 0
        active = t < nb
        head0, lo_half = _lane_masks(ATT_BLOCK)
        head0_2, _ = _lane_masks(2 * ATT_BLOCK)
        cos_p, sin_p = cosp_ref[...], sinp_ref[...]

        @pl.when(t == 0)
        def _():
            dka_ref[...] = jnp.zeros_like(dka_ref)
            dva_ref[...] = jnp.zeros_like(dva_ref)
            dqp_ref[...] = jnp.zeros_like(dqp_ref)

        dqkv_ref[:, pl.ds(0, WIDTH)] = dqp_ref[...]

        @pl.when(active & first)
        def _():
            kp_ref[...] = jnp.zeros_like(kp_ref)
            vp_ref[...] = jnp.zeros_like(vp_ref)

        @pl.when(active)
        def _():
            cos_b, sin_b = cos_ref[...], sin_ref[...]
            valid = _window_mask(first)
            for sl in range(WIDTH // LANES):
                cq = pl.ds(sl * LANES, LANES)
                ck = pl.ds(WIDTH + sl * LANES, LANES)
                cv = pl.ds(2 * WIDTH + sl * LANES, LANES)
                qr = (_rope(qkv_ref[:, cq], cos_b, sin_b, lo_half) * (HEAD_DIM ** -0.5)).astype(BF16)
                kr = _rope(qkv_ref[:, ck], cos_b, sin_b, lo_half).astype(BF16)
                v16 = qkv_ref[:, cv].astype(BF16)
                kcat = jnp.concatenate([kp_ref[:, cq], kr], axis=0)
                vcat = jnp.concatenate([vp_ref[:, cq], v16], axis=0)
                do_b = do_ref[:, cq]
                do16 = do_b.astype(BF16)
                lse_b = lse_ref[:, cq]
                cterm = dlse_ref[:, cq] - do_b * o_ref[:, cq]
                dqs, dks, dvs = [], [], []
                for hm in (head0, jnp.logical_not(head0)):
                    sc = _dg(jnp.where(hm, qr, jnp.zeros_like(qr)), kcat, 1, 1)
                    sc = jnp.where(valid, sc, -jnp.inf)
                    lse_h = jnp.max(jnp.where(hm, lse_b, -jnp.inf), axis=1, keepdims=True)
                    p = jnp.exp(sc - lse_h)
                    dp = _dg(jnp.where(hm, do16, jnp.zeros_like(do16)), vcat, 1, 1)
                    c = jnp.sum(jnp.where(hm, cterm, 0.0), axis=1, keepdims=True)
                    ds16 = (p * (dp + c)).astype(BF16)
                    dvs.append(_dg(p.astype(BF16), do16, 0, 0))
                    dqs.append(_dg(ds16, kcat, 1, 0))
                    dks.append(_dg(ds16, qr, 0, 0))
                dq = jnp.where(head0, dqs[0], dqs[1]) * (HEAD_DIM ** -0.5)
                dqp_ref[:, cq] = _rope_bwd(dq, cos_b, sin_b, lo_half).astype(BF16)
                dkc = jnp.where(head0_2, dks[0], dks[1])
                dvc = jnp.where(head0_2, dvs[0], dvs[1])
                dqkv_ref[:, ck] = _rope_bwd(dka_ref[:, cq] + dkc[:ATT_BLOCK], cos_p, sin_p, lo_half).astype(BF16)
                dqkv_ref[:, cv] = (dva_ref[:, cq] + dvc[:ATT_BLOCK]).astype(BF16)
                dka_ref[:, cq] = dkc[ATT_BLOCK:]
                dva_ref[:, cq] = dvc[ATT_BLOCK:]
                kp_ref[:, cq] = kr
                vp_ref[:, cq] = v16

        @pl.when(jnp.logical_not(active))
        def _():
            for sl in range(WIDTH // LANES):
                cq = pl.ds(sl * LANES, LANES)
                dqkv_ref[:, pl.ds(WIDTH + sl * LANES, LANES)] = _rope_bwd(dka_ref[:, cq], cos_p, sin_p, lo_half).astype(BF16)
                dqkv_ref[:, pl.ds(2 * WIDTH + sl * LANES, LANES)] = dva_ref[:, cq].astype(BF16)

    cur = lambda w: pl.BlockSpec((None, ATT_BLOCK, w), lambda g, t: (g, jnp.minimum(t, nb - 1), 0))
    prev = lambda w: pl.BlockSpec((None, ATT_BLOCK, w), lambda g, t: (g, jnp.maximum(t - 1, 0), 0))
    return pl.pallas_call(
        body, name="dilated_attention_bwd", grid=(GROUPS, nb + 1),
        in_specs=[cur(3 * WIDTH), cur(LANES), cur(LANES), prev(LANES), prev(LANES),
                  cur(WIDTH), cur(WIDTH), cur(WIDTH), cur(WIDTH)],
        out_specs=prev(3 * WIDTH), out_shape=jax.ShapeDtypeStruct((GROUPS, s, 3 * WIDTH), BF16),
        scratch_shapes=[pltpu.VMEM((ATT_BLOCK, WIDTH), BF16), pltpu.VMEM((ATT_BLOCK, WIDTH), BF16),
                        pltpu.VMEM((ATT_BLOCK, WIDTH), F32), pltpu.VMEM((ATT_BLOCK, WIDTH), F32),
                        pltpu.VMEM((ATT_BLOCK, WIDTH), BF16)],
        compiler_params=_params("arbitrary", "arbitrary"),
    )(qkv, cos, sin, cos, sin, o, lse, do, dlse)


CONV_PAD = SUBLANES


def _gdn_post(y, is_q, is_k):
    head0, _ = _lane_masks(y.shape[0])
    c = _silu(y)
    sq = c * c
    ss0 = jnp.sum(jnp.where(head0, sq, 0.0), axis=1, keepdims=True)
    ss1 = jnp.sum(jnp.where(head0, 0.0, sq), axis=1, keepdims=True)
    r = jnp.where(head0, lax.rsqrt(ss0 + NORM_EPS), lax.rsqrt(ss1 + NORM_EPS))
    scale = jnp.where(is_q, HEAD_DIM ** -0.5, 1.0).astype(F32)
    return jnp.where(is_q | is_k, c * r * scale, c)


def _conv_rows(xp_ref, w, c0, rows):
    y = w[0:1, :] * xp_ref[pl.ds(c0 + CONV_PAD - (CONV_K - 1), rows), :]
    for k in range(1, CONV_K):
        y = y + w[k:k + 1, :] * xp_ref[pl.ds(c0 + CONV_PAD - (CONV_K - 1) + k, rows), :]
    return y


def _gdn_pre_fwd(proj_r, conv8, col0):
    s = proj_r.shape[0]
    tr = _row_block(s, 512)
    nblk = QKV_B // LANES
    nq = WIDTH // LANES

    def body(x_ref, w_ref, out_ref, xp_ref):
        j = pl.program_id(0)
        is_q, is_k = j < nq, (j >= nq) & (j < 2 * nq)
        xp_ref[pl.ds(0, CONV_PAD), :] = jnp.zeros((CONV_PAD, LANES), F32)
        xp_ref[pl.ds(CONV_PAD, s), :] = x_ref[...]
        w = w_ref[...]
        for c in range(s // tr):
            out_ref[pl.ds(c * tr, tr), :] = _gdn_post(_conv_rows(xp_ref, w, c * tr, tr), is_q, is_k)

    return pl.pallas_call(
        body, name="gdn_conv_fwd", grid=(nblk,),
        in_specs=[pl.BlockSpec((s, LANES), lambda j: (0, col0 + j)), pl.BlockSpec((SUBLANES, LANES), lambda j: (0, j))],
        out_specs=pl.BlockSpec((s, LANES), lambda j: (0, j)),
        out_shape=jax.ShapeDtypeStruct((s, QKV_B), F32),
        scratch_shapes=[pltpu.VMEM((s + CONV_PAD, LANES), F32)],
        compiler_params=_params("parallel"),
    )(proj_r, conv8)


def _gdn_pre_bwd(proj_r, conv8, dc, col0):
    s = proj_r.shape[0]
    tr = _row_block(s, 512)
    nblk = QKV_B // LANES
    nq = WIDTH // LANES

    def body(x_ref, w_ref, dc_ref, dx_ref, dw_ref, xp_ref, dyp_ref):
        j = pl.program_id(0)
        is_q, is_k = j < nq, (j >= nq) & (j < 2 * nq)
        xp_ref[pl.ds(0, CONV_PAD), :] = jnp.zeros((CONV_PAD, LANES), F32)
        xp_ref[pl.ds(CONV_PAD, s), :] = x_ref[...]
        dyp_ref[pl.ds(s, CONV_PAD), :] = jnp.zeros((CONV_PAD, LANES), F32)
        w = w_ref[...]
        for c in range(s // tr):
            y = _conv_rows(xp_ref, w, c * tr, tr)
            _, vjp = jax.vjp(lambda yy: _gdn_post(yy, is_q, is_k), y)
            dyp_ref[pl.ds(c * tr, tr), :] = vjp(dc_ref[pl.ds(c * tr, tr), :])[0]
        dws = [jnp.zeros((1, LANES), F32) for _ in range(CONV_K)]
        for c in range(s // tr):
            c0 = c * tr
            dy = dyp_ref[pl.ds(c0, tr), :]
            dx = w[0:1, :] * dyp_ref[pl.ds(c0 + CONV_K - 1, tr), :]
            for k in range(1, CONV_K):
                dx = dx + w[k:k + 1, :] * dyp_ref[pl.ds(c0 + CONV_K - 1 - k, tr), :]
            dx_ref[pl.ds(c0, tr), :] = dx.astype(BF16)
            for k in range(CONV_K):
                xs = xp_ref[pl.ds(c0 + CONV_PAD - (CONV_K - 1) + k, tr), :]
                dws[k] = dws[k] + jnp.sum(dy * xs, axis=0, keepdims=True)
        row = lax.broadcasted_iota(jnp.int32, (SUBLANES, LANES), 0)
        dwb = jnp.zeros((SUBLANES, LANES), F32)
        for k in range(CONV_K):
            dwb = dwb + jnp.where(row == k, dws[k], 0.0)
        dw_ref[...] = dwb

    return pl.pallas_call(
        body, name="gdn_conv_bwd", grid=(nblk,),
        in_specs=[pl.BlockSpec((s, LANES), lambda j: (0, col0 + j)), pl.BlockSpec((SUBLANES, LANES), lambda j: (0, j)),
                  pl.BlockSpec((s, LANES), lambda j: (0, j))],
        out_specs=(pl.BlockSpec((s, LANES), lambda j: (0, j)), pl.BlockSpec((SUBLANES, LANES), lambda j: (0, j))),
        out_shape=(jax.ShapeDtypeStruct((s, QKV_B), BF16), jax.ShapeDtypeStruct((SUBLANES, QKV_B), F32)),
        scratch_shapes=[pltpu.VMEM((s + CONV_PAD, LANES), F32), pltpu.VMEM((s + CONV_PAD, LANES), F32)],
        compiler_params=_params("parallel"),
    )(proj_r, conv8, dc)


def _gdn_chunk(q, k, v, bcol, acol, alog, dtb, gnw, state):
    n = q.shape[-2]
    shp = (1, n, n)
    row = lax.broadcasted_iota(jnp.int32, shp, 1)
    col = lax.broadcasted_iota(jnp.int32, shp, 2)
    beta = _sigmoid(bcol)
    g = -jnp.exp(alog) * _softplus(acol + dtb)
    g_row = jnp.sum(jnp.where(row == col, g, 0.0), axis=-2, keepdims=True)
    big_g = jnp.sum(jnp.where(row >= col, g_row, 0.0), axis=-1, keepdims=True)
    big_g_row = jnp.sum(jnp.where(row <= col, g, 0.0), axis=-2, keepdims=True)
    decay_incl = jnp.exp(jnp.where(row >= col, big_g - big_g_row, -jnp.inf))
    decay_strict = jnp.where(row > col, decay_incl, 0.0)
    k_beta = k * beta
    t_inv = _tri_inv(_mm_nt(k_beta, k) * decay_strict)
    e_g = jnp.exp(big_g)
    u = _mm(t_inv, v * beta)
    w = _mm(t_inv, k_beta * e_g)
    attn = _mm_nt(q, k) * decay_incl
    v_new = u - _mm(w, state)
    o = _mm(q * e_g, state) + _mm(attn, v_new)
    total = jnp.sum(g, axis=-2, keepdims=True)
    new_state = state * jnp.exp(total) + _mm_tn(k * jnp.exp(total - big_g), v_new)
    return _rmsnorm(o, gnw), new_state


def _gdn_scan_fwd(gq, gk, gv, b_hm, a_hm, alog, dtb, gnw):
    _, s, _ = gq.shape
    nc = s // CHUNK

    def body(q_ref, k_ref, v_ref, b_ref, a_ref, al_ref, dt_ref, gnw_ref, o_ref, st_ref, state_ref):
        @pl.when(pl.program_id(0) == 0)
        def _():
            state_ref[...] = jnp.zeros_like(state_ref)

        st = state_ref[...]
        st_ref[...] = st
        o, new_st = _gdn_chunk(q_ref[...], k_ref[...], v_ref[...], b_ref[...], a_ref[...], al_ref[...], dt_ref[...],
                               gnw_ref[...], st)
        o_ref[...] = o
        state_ref[...] = new_st

    hm = pl.BlockSpec((HEADS, CHUNK, HEAD_DIM), lambda n: (0, n, 0))
    colv = pl.BlockSpec((HEADS, CHUNK, 1), lambda n: (0, n, 0))
    par = pl.BlockSpec((HEADS, 1, 1), lambda n: (0, 0, 0))
    return pl.pallas_call(
        body, name="gdn_scan_fwd", grid=(nc,),
        in_specs=[hm, hm, hm, colv, colv, par, par, pl.BlockSpec((1, 1, HEAD_DIM), lambda n: (0, 0, 0))],
        out_specs=(hm, pl.BlockSpec((None, HEADS, HEAD_DIM, HEAD_DIM), lambda n: (n, 0, 0, 0))),
        out_shape=(jax.ShapeDtypeStruct((HEADS, s, HEAD_DIM), F32),
                   jax.ShapeDtypeStruct((nc, HEADS, HEAD_DIM, HEAD_DIM), F32)),
        scratch_shapes=[pltpu.VMEM((HEADS, HEAD_DIM, HEAD_DIM), F32)],
        compiler_params=_params("arbitrary"),
    )(gq, gk, gv, b_hm, a_hm, alog, dtb, gnw)


def _gdn_scan_bwd(gq, gk, gv, b_hm, a_hm, alog, dtb, gnw, states, do):
    _, s, _ = gq.shape
    nc = s // CHUNK

    def body(q_ref, k_ref, v_ref, b_ref, a_ref, al_ref, dt_ref, gnw_ref, st_ref, do_ref,
             dq_ref, dk_ref, dv_ref, db_ref, da_ref, dal_ref, ddt_ref, dgnw_ref, dstate_ref):
        @pl.when(pl.program_id(0) == 0)
        def _():
            dstate_ref[...] = jnp.zeros_like(dstate_ref)
            dal_ref[...] = jnp.zeros_like(dal_ref)
            ddt_ref[...] = jnp.zeros_like(ddt_ref)
            dgnw_ref[...] = jnp.zeros_like(dgnw_ref)

        _, vjp = jax.vjp(_gdn_chunk, q_ref[...], k_ref[...], v_ref[...], b_ref[...], a_ref[...], al_ref[...],
                         dt_ref[...], gnw_ref[...], st_ref[...])
        dq, dk, dv, dbc, dac, dal, ddt, dgn, dst = vjp((do_ref[...], dstate_ref[...]))
        dq_ref[...] = dq
        dk_ref[...] = dk
        dv_ref[...] = dv
        db_ref[...] = dbc
        da_ref[...] = dac
        dstate_ref[...] = dst
        dal_ref[...] += dal
        ddt_ref[...] += ddt
        dgnw_ref[...] += dgn

    rev = lambda n: nc - 1 - n
    hm = pl.BlockSpec((HEADS, CHUNK, HEAD_DIM), lambda n: (0, rev(n), 0))
    colv = pl.BlockSpec((HEADS, CHUNK, 1), lambda n: (0, rev(n), 0))
    par = pl.BlockSpec((HEADS, 1, 1), lambda n: (0, 0, 0))
    vec = pl.BlockSpec((1, 1, HEAD_DIM), lambda n: (0, 0, 0))
    hm_shape = jax.ShapeDtypeStruct((HEADS, s, HEAD_DIM), F32)
    col_shape = jax.ShapeDtypeStruct((HEADS, s, 1), F32)
    par_shape = jax.ShapeDtypeStruct((HEADS, 1, 1), F32)
    return pl.pallas_call(
        body, name="gdn_scan_bwd", grid=(nc,),
        in_specs=[hm, hm, hm, colv, colv, par, par, vec,
                  pl.BlockSpec((None, HEADS, HEAD_DIM, HEAD_DIM), lambda n: (rev(n), 0, 0, 0)), hm],
        out_specs=(hm, hm, hm, colv, colv, par, par, vec),
        out_shape=(hm_shape, hm_shape, hm_shape, col_shape, col_shape, par_shape, par_shape,
                   jax.ShapeDtypeStruct((1, 1, HEAD_DIM), F32)),
        scratch_shapes=[pltpu.VMEM((HEADS, HEAD_DIM, HEAD_DIM), F32)],
        compiler_params=_params("arbitrary"),
    )(gq, gk, gv, b_hm, a_hm, alog, dtb, gnw, states, do)


def _tail_loss(x, tgt, o0, o1, o2, l0, l1, l2, ga, gb, za, zb, ob, wua, wub, wo, fnw):
    lm = jnp.maximum(jnp.maximum(l0, l1), l2)
    e0, e1, e2 = jnp.exp(l0 - lm), jnp.exp(l1 - lm), jnp.exp(l2 - lm)
    o_a = (e0 * o0 + e1 * o1 + e2 * o2) / (e0 + e1 + e2)
    y_a = _mm(o_a * _silu(za), wua)
    y_b = _mm(ob * _silu(zb), wub)
    merged = _sigmoid(ga) * y_a + _sigmoid(gb) * y_b
    y = _rmsnorm(x + _mm(merged, wo), fnw)
    err = y - tgt
    per_token = jnp.sum(err * err, axis=1, keepdims=True) * (0.5 / x.shape[1])
    return jnp.sum(per_token, axis=0, keepdims=True)


def _tail(x, tgt, og, lg, proj_r, ob, wua, wub, wo, fnw):
    s, d = x.shape
    tm = _row_block(s, 128)
    col_za = 2 * d // WIDTH
    col_zb = (2 * d + WIDTH + QKV_B) // WIDTH

    def body(x_ref, t_ref, o0_ref, o1_ref, o2_ref, l0_ref, l1_ref, l2_ref, ga_ref, gb_ref, za_ref, zb_ref, ob_ref,
             wua_ref, wub_ref, wo_ref, fnw_ref,
             loss_ref, dx_ref, do0_ref, do1_ref, do2_ref, dl0_ref, dl1_ref, dl2_ref, dga_ref, dgb_ref, dza_ref,
             dzb_ref, dob_ref, dwua_ref, dwub_ref, dwo_ref, dfnw_ref):
        @pl.when(pl.program_id(0) == 0)
        def _():
            for r in (loss_ref, dwua_ref, dwub_ref, dwo_ref, dfnw_ref):
                r[...] = jnp.zeros_like(r)

        args = (x_ref[...], t_ref[...], o0_ref[...], o1_ref[...], o2_ref[...], l0_ref[...], l1_ref[...], l2_ref[...],
                ga_ref[...], gb_ref[...], za_ref[...], zb_ref[...], ob_ref[...],
                wua_ref[...].astype(F32), wub_ref[...].astype(F32), wo_ref[...].astype(F32), fnw_ref[...])
        loss, vjp = jax.vjp(_tail_loss, *args)
        (dx, _, do0, do1, do2, dl0, dl1, dl2, dga, dgb, dza, dzb, dob, dwua, dwub, dwo, dfnw) = vjp(jnp.ones((1, 1), F32))
        loss_ref[...] += jnp.broadcast_to(loss, loss_ref.shape)
        dx_ref[...] = dx
        do0_ref[...], do1_ref[...], do2_ref[...] = do0, do1, do2
        dl0_ref[...], dl1_ref[...], dl2_ref[...] = dl0, dl1, dl2
        dga_ref[...] = dga.astype(BF16)
        dgb_ref[...] = dgb.astype(BF16)
        dza_ref[...] = dza.astype(BF16)
        dzb_ref[...] = dzb.astype(BF16)
        dob_ref[...] = dob
        dwua_ref[...] += dwua
        dwub_ref[...] += dwub
        dwo_ref[...] += dwo
        dfnw_ref[...] += dfnw

    row = lambda w, c=0: pl.BlockSpec((tm, w), lambda i: (i, c))
    full = lambda a, b: pl.BlockSpec((a, b), lambda i: (0, 0))
    f32 = lambda a, b: jax.ShapeDtypeStruct((a, b), F32)
    b16 = lambda a, b: jax.ShapeDtypeStruct((a, b), BF16)
    in_specs = ([row(d), row(d)] + [row(WIDTH)] * 6 + [row(d, 0), row(d, 1), row(WIDTH, col_za), row(WIDTH, col_zb),
                row(WIDTH), full(WIDTH, d), full(WIDTH, d), full(d, d), full(1, d)])
    out_specs = ([full(SUBLANES, LANES), row(d)] + [row(WIDTH)] * 6 + [row(d), row(d), row(WIDTH), row(WIDTH), row(WIDTH),
                 full(WIDTH, d), full(WIDTH, d), full(d, d), full(1, d)])
    out_shape = ([f32(SUBLANES, LANES), f32(s, d)] + [f32(s, WIDTH)] * 6 + [b16(s, d), b16(s, d), b16(s, WIDTH),
                 b16(s, WIDTH), f32(s, WIDTH), f32(WIDTH, d), f32(WIDTH, d), f32(d, d), f32(1, d)])
    return pl.pallas_call(
        body, name="tail_fwd_bwd", grid=(s // tm,),
        in_specs=in_specs, out_specs=tuple(out_specs), out_shape=tuple(out_shape),
        compiler_params=_params("arbitrary"),
    )(x, tgt, og[0], og[1], og[2], lg[0], lg[1], lg[2], proj_r, proj_r, proj_r, proj_r, ob, wua, wub, wo, fnw)


def _to_dilated(a, dil):
    if dil == 1:
        return a
    s = a.shape[0]
    return a.reshape(s // dil, dil, -1).transpose(1, 0, 2).reshape(a.shape)


def _from_dilated(a, dil):
    if dil == 1:
        return a
    s = a.shape[0]
    return a.reshape(dil, s // dil, -1).transpose(1, 0, 2).reshape(a.shape)


def _head_major(a):
    return a.reshape(a.shape[0], HEADS, HEAD_DIM).transpose(1, 0, 2)


def _from_head_major(a):
    return a.transpose(1, 0, 2).reshape(a.shape[1], WIDTH)


def _rope_tables(s):
    inv_freq = ROPE_THETA ** (-jnp.arange(0, HEAD_DIM, 2, dtype=F32) / HEAD_DIM)
    ang = jnp.arange(s, dtype=F32)[:, None] * inv_freq[None, :]
    cos_n = jnp.tile(jnp.cos(ang), (1, 2 * LANES // HEAD_DIM))
    sin_h = jnp.sin(ang)
    sin_n = jnp.tile(jnp.concatenate([-sin_h, sin_h], axis=1), (1, LANES // HEAD_DIM))
    return (jnp.stack([_to_dilated(cos_n, dil) for dil in DILATIONS]),
            jnp.stack([_to_dilated(sin_n, dil) for dil in DILATIONS]))


def _pack_rows(parts, dtype, row_multiple):
    flat = jnp.concatenate([p.reshape(-1).astype(dtype) for p in parts])
    tile = row_multiple * LANES
    pad = (-flat.shape[0]) % tile
    return jnp.pad(flat, (0, pad)).reshape(-1, LANES)


def _unpack_rows(packed, shapes):
    flat = packed.reshape(-1)
    out, start = [], 0
    for shp in shapes:
        size = 1
        for n in shp:
            size *= n
        out.append(flat[start:start + size].reshape(shp))
        start += size
    return out


def kernel(x, norm_w, w_in, conv_w, a_log, dt_bias, gdn_norm_w, w_up_a, w_up_b, w_out, final_norm_w, loss_target, m_norm_w, m_w_in, m_conv_w, m_a_log, m_dt_bias, m_gdn_norm_w, m_w_up_a, m_w_up_b, m_w_out, m_final_norm_w, v_norm_w, v_w_in, v_conv_w, v_a_log, v_dt_bias, v_gdn_norm_w, v_w_up_a, v_w_up_b, v_w_out, v_final_norm_w):
    x2, tgt = x[0], loss_target[0]
    s, d = x2.shape
    me = 4 * lax.axis_index("x") + 2 * lax.axis_index("y") + lax.axis_index("c")
    win8 = w_in.shape[2]
    conv8w = conv_w.shape[2]

    conv_shard = jnp.pad(conv_w[0], ((0, SUBLANES - CONV_K), (0, 0)))
    w_in_g, wua_g, wub_g, wo_g, conv_g = _all_gather(
        [w_in[0].astype(BF16), w_up_a[0].astype(BF16), w_up_b[0].astype(BF16), w_out[0].astype(BF16), conv_shard])
    w_in_f = jnp.concatenate([w_in_g[i] for i in range(N_DEV)], axis=1)
    wua = jnp.concatenate([wua_g[i] for i in range(N_DEV)], axis=1)
    wub = jnp.concatenate([wub_g[i] for i in range(N_DEV)], axis=1)
    wo = wo_g.reshape(d, d)
    conv8 = jnp.concatenate([conv_g[i] for i in range(N_DEV)], axis=1)

    w_qkv = w_in_f[:, :QKV_A].reshape(d, GROUPS, QKV_B).transpose(1, 0, 2)
    w_rest = jnp.concatenate([
        w_in_f[:, OFF_GATE:OFF_GATE + 2 * d], w_in_f[:, OFF_ZA:OFF_ZA + WIDTH], w_in_f[:, OFF_QKVB:OFF_QKVB + QKV_B],
        w_in_f[:, OFF_ZB:OFF_ZB + WIDTH], w_in_f[:, OFF_BA:OFF_BA + 2 * HEADS],
        jnp.zeros((d, BA_PAD - 2 * HEADS), BF16)], axis=1)
    col_qkvb = (2 * d + WIDTH) // LANES
    col_ba = (2 * d + 2 * WIDTH + QKV_B) // LANES

    h = _rms_fwd(x2, norm_w)
    h_all = jnp.stack([_to_dilated(h, dil) for dil in DILATIONS])
    qkv_all = _matmul(h_all, w_qkv, F32, "in_proj_attention")
    proj_r = _matmul(h[None], w_rest[None], F32, "in_proj_rest")[0]
    cos, sin = _rope_tables(s)
    o_all, lse_all = _attn_fwd(qkv_all, cos, sin)
    og = [_from_dilated(o_all[g], dil) for g, dil in enumerate(DILATIONS)]
    lg = [_from_dilated(lse_all[g], dil) for g, dil in enumerate(DILATIONS)]

    cqkv = _gdn_pre_fwd(proj_r, conv8, col_qkvb)
    gq, gk, gv = (_head_major(cqkv[:, i * WIDTH:(i + 1) * WIDTH]) for i in range(3))
    ba_off = col_ba * LANES
    ba_hm = proj_r[:, ba_off:ba_off + 2 * HEADS].T.reshape(2 * HEADS, s, 1)
    b_hm, a_hm = ba_hm[:HEADS], ba_hm[HEADS:]
    alog3, dtb3, gnw3 = a_log.reshape(HEADS, 1, 1), dt_bias.reshape(HEADS, 1, 1), gdn_norm_w.reshape(1, 1, HEAD_DIM)
    ob_hm, states = _gdn_scan_fwd(gq, gk, gv, b_hm, a_hm, alog3, dtb3, gnw3)
    ob = _from_head_major(ob_hm)

    (loss_blk, dx_res, do0, do1, do2, dl0, dl1, dl2, dga, dgb, dza, dzb, dob, dwua, dwub, dwo, dfnw) = _tail(
        x2, tgt, og, lg, proj_r, ob, wua, wub, wo, final_norm_w.reshape(1, d))

    do_all = jnp.stack([_to_dilated(t, dil) for t, dil in zip((do0, do1, do2), DILATIONS)])
    dl_all = jnp.stack([_to_dilated(t, dil) for t, dil in zip((dl0, dl1, dl2), DILATIONS)])
    dqkv_all = _attn_bwd(qkv_all, cos, sin, o_all, lse_all, do_all, dl_all)

    dgq, dgk, dgv, db_hm, da_hm, dalog3, ddtb3, dgnw3 = _gdn_scan_bwd(gq, gk, gv, b_hm, a_hm, alog3, dtb3, gnw3,
                                                                      states, _head_major(dob))
    dcqkv = jnp.concatenate([_from_head_major(t) for t in (dgq, dgk, dgv)], axis=1)
    dqkv_b, dconv8 = _gdn_pre_bwd(proj_r, conv8, dcqkv, col_qkvb)
    dba = jnp.concatenate([db_hm, da_hm], axis=0).reshape(2 * HEADS, s).T
    dproj_r = jnp.concatenate([dga, dgb, dza, dqkv_b, dzb,
                               jnp.pad(dba.astype(BF16), ((0, 0), (0, BA_PAD - 2 * HEADS)))], axis=1)

    dw_qkv = _matmul(h_all, dqkv_all, F32, "in_proj_attention_dw", mode="tn")
    dw_rest = _matmul(h[None], dproj_r[None], F32, "in_proj_rest_dw", mode="tn")[0]
    dh_a = _matmul(dqkv_all, w_qkv, F32, "in_proj_attention_dh", mode="nt")
    dh_r = _matmul(dproj_r[None], w_rest[None], F32, "in_proj_rest_dh", mode="nt")[0]
    dh_parts = [dh_r] + [_from_dilated(dh_a[g], dil) for g, dil in enumerate(DILATIONS)]
    grad_x, dnorm_w = _rms_bwd(x2, norm_w, dh_parts, dx_res)

    o2 = 2 * d
    dw_in = jnp.concatenate([
        dw_qkv.transpose(1, 0, 2).reshape(d, QKV_A),
        dw_rest[:, o2:o2 + WIDTH], dw_rest[:, o2 + WIDTH:o2 + WIDTH + QKV_B],
        dw_rest[:, o2 + WIDTH + QKV_B:o2 + 2 * WIDTH + QKV_B],
        dw_rest[:, o2 + 2 * WIDTH + QKV_B:o2 + 2 * WIDTH + QKV_B + 2 * HEADS],
        dw_rest[:, :o2]], axis=1)

    def col_slabs(a, width):
        return jnp.stack([a[:, j * width:(j + 1) * width] for j in range(N_DEV)])

    slabs = [col_slabs(dw_in, win8), col_slabs(dwua, d // N_DEV), col_slabs(dwub, d // N_DEV),
             dwo.reshape(N_DEV, d // N_DEV, d)]
    from_sibling = _sibling_exchange(slabs)
    core = lax.axis_index("c").astype(jnp.int32).reshape(1)
    partials = [_pair_sum(a, b, core, "grads_pair_sum_%d" % i) for i, (a, b) in enumerate(zip(slabs, from_sibling))]
    contrib = _chip_exchange(partials)

    small_parts = [dnorm_w, dfnw, dconv8[:CONV_K], dalog3[:, 0, 0], ddtb3[:, 0, 0], dgnw3[0], loss_blk[0, 0:1]]
    small_rows = [-(-p.size // LANES) for p in small_parts]
    small = jnp.concatenate([jnp.pad(p.reshape(-1), (0, r * LANES - p.size)).reshape(r, LANES)
                             for p, r in zip(small_parts, small_rows)])
    small = jnp.pad(small, ((0, (-small.shape[0]) % SUBLANES), (0, 0)))
    small_sum = _small_all_reduce(small)
    pieces, r0 = [], 0
    for p, r in zip(small_parts, small_rows):
        pieces.append(small_sum[r0:r0 + r].reshape(-1)[:p.size].reshape(p.shape))
        r0 += r
    g_norm_w, g_fnw, g_conv_full, g_alog, g_dtb, g_gnw, loss_sum = pieces
    g_conv = lax.dynamic_slice(g_conv_full, (0, me * conv8w), (CONV_K, conv8w))

    big = [_adamw(c, w[0], m[0], v[0], name) for c, w, m, v, name in (
        (contrib[0], w_in, m_w_in, v_w_in, "adamw_w_in"), (contrib[1], w_up_a, m_w_up_a, v_w_up_a, "adamw_w_up_a"),
        (contrib[2], w_up_b, m_w_up_b, v_w_up_b, "adamw_w_up_b"), (contrib[3], w_out, m_w_out, v_w_out, "adamw_w_out"))]
    g_big, d_big, nm_big, nv_big = ([t[i] for t in big] for i in range(4))

    small_ws = [norm_w, final_norm_w, conv_w, a_log, dt_bias, gdn_norm_w]
    small_ms = [m_norm_w, m_final_norm_w, m_conv_w, m_a_log, m_dt_bias, m_gdn_norm_w]
    small_vs = [v_norm_w, v_final_norm_w, v_conv_w, v_a_log, v_dt_bias, v_gdn_norm_w]
    small_gs = [g_norm_w, g_fnw, g_conv, g_alog, g_dtb, g_gnw]
    small_shapes = [t.shape for t in small_ws]
    sm = _adamw(_pack_rows(small_gs, F32, SUBLANES)[None], _pack_rows(small_ws, F32, SUBLANES),
                _pack_rows(small_ms, F32, SUBLANES), _pack_rows(small_vs, F32, SUBLANES), "adamw_small")
    g_sm, d_sm, nm_sm, nv_sm = (_unpack_rows(t, small_shapes) for t in sm)

    def ordered(bigs, smalls):
        nw, fnw_, cw, al, dtb, gn = smalls
        wi, ua, ub, wo_ = (t[None] for t in bigs)
        return [nw, wi, cw, al, dtb, gn, ua, ub, wo_, fnw_]

    return (loss_sum.reshape(()), grad_x[None], *ordered(g_big, g_sm), *ordered(d_big, d_sm),
            *ordered(nm_big, nm_sm), *ordered(nv_big, nv_sm))
```

```python
import functools

import jax
import jax.numpy as jnp
from jax import lax
from jax.experimental import pallas as pl
from jax.experimental.pallas import tpu as pltpu

F32 = jnp.float32
BF16 = jnp.bfloat16
MESH = pl.DeviceIdType.MESH
N_DEV = 8
LANES = 128
SUBLANES = 8

GROUPS = 3
HEADS = 8
HEAD_DIM = 64
WIDTH = HEADS * HEAD_DIM
ATT_BLOCK = 128
DILATIONS = (1, 4, 16)
N_BACK = 128
CONV_K = 4
CHUNK = 64
QKV_B = 3 * WIDTH
QKV_A = GROUPS * 3 * WIDTH
BA_PAD = 512
NORM_EPS = 1e-6
ROPE_THETA = 10000.0
ADAM_LR, ADAM_B1, ADAM_B2, ADAM_EPS, ADAM_WD, ADAM_STEP = 0.001, 0.9, 0.999, 1e-08, 0.01, 10

VMEM_LIMIT = 56 * 1024 * 1024

OFF_ZA = QKV_A
OFF_QKVB = OFF_ZA + WIDTH
OFF_ZB = OFF_QKVB + QKV_B
OFF_BA = OFF_ZB + WIDTH
OFF_GATE = OFF_BA + 2 * HEADS


def _params(*sem):
    return pltpu.CompilerParams(dimension_semantics=sem, vmem_limit_bytes=VMEM_LIMIT)


def _dg(a, b, ca, cb):
    nb = a.ndim - 2
    batch = tuple(range(nb))
    return lax.dot_general(a, b, (((nb + ca,), (nb + cb,)), (batch, batch)), preferred_element_type=F32)


@jax.custom_vjp
def _mm(a, b):
    return _dg(a.astype(BF16), b.astype(BF16), 1, 0)


def _mm_fwd(a, b):
    return _mm(a, b), (a.astype(BF16), b.astype(BF16))


def _mm_bwd(res, ct):
    a16, b16 = res
    c16 = ct.astype(BF16)
    return _dg(c16, b16, 1, 1), _dg(a16, c16, 0, 0)


_mm.defvjp(_mm_fwd, _mm_bwd)


@jax.custom_vjp
def _mm_nt(a, b):
    return _dg(a.astype(BF16), b.astype(BF16), 1, 1)


def _mm_nt_fwd(a, b):
    return _mm_nt(a, b), (a.astype(BF16), b.astype(BF16))


def _mm_nt_bwd(res, ct):
    a16, b16 = res
    c16 = ct.astype(BF16)
    return _dg(c16, b16, 1, 0), _dg(c16, a16, 0, 0)


_mm_nt.defvjp(_mm_nt_fwd, _mm_nt_bwd)


@jax.custom_vjp
def _mm_tn(a, b):
    return _dg(a.astype(BF16), b.astype(BF16), 0, 0)


def _mm_tn_fwd(a, b):
    return _mm_tn(a, b), (a.astype(BF16), b.astype(BF16))


def _mm_tn_bwd(res, ct):
    a16, b16 = res
    c16 = ct.astype(BF16)
    return _dg(b16, c16, 1, 1), _dg(a16, c16, 1, 0)


_mm_tn.defvjp(_mm_tn_fwd, _mm_tn_bwd)


def _split16(a):
    hi = a.astype(BF16)
    lo = (a - hi.astype(F32)).astype(BF16)
    return hi, lo


def _dot3(a, b, ca, cb):
    ah, al = _split16(a)
    bh, bl = _split16(b)
    return _dg(ah, bh, ca, cb) + (_dg(ah, bl, ca, cb) + _dg(al, bh, ca, cb))


def _tri_inv_impl(a):
    n = a.shape[-1]
    shp = (1,) * (a.ndim - 2) + (n, n)
    eye = (lax.broadcasted_iota(jnp.int32, shp, a.ndim - 2) == lax.broadcasted_iota(jnp.int32, shp, a.ndim - 1)).astype(F32)
    x = eye - a
    p = a
    for _ in range(5):
        p = _dot3(p, p, 1, 0)
        x = x + _dot3(x, p, 1, 0)
    return x


@jax.custom_vjp
def _tri_inv(a):
    return _tri_inv_impl(a)


def _tri_inv_fwd(a):
    t = _tri_inv_impl(a)
    return t, t


def _tri_inv_bwd(t, ct):
    return (-_dot3(_dot3(t, ct, 0, 0), t, 1, 1),)


_tri_inv.defvjp(_tri_inv_fwd, _tri_inv_bwd)


def _sigmoid(x):
    return 1.0 / (1.0 + jnp.exp(-x))


def _silu(x):
    return x * _sigmoid(x)


def _softplus(x):
    return jnp.maximum(x, 0.0) + jnp.log(1.0 + jnp.exp(-jnp.abs(x)))


def _rmsnorm(x, w):
    return x * lax.rsqrt(jnp.mean(x * x, axis=-1, keepdims=True) + NORM_EPS) * w


def _row_block(rows, cap):
    best = None
    for cand in range(SUBLANES, min(rows, cap) + 1, SUBLANES):
        if rows % cand == 0:
            best = cand
    assert best is not None, rows
    return best


def _mesh_peers():
    x, y, c = lax.axis_index("x"), lax.axis_index("y"), lax.axis_index("c")
    me = 4 * x + 2 * y + c
    peers = []
    for k in range(1, N_DEV):
        px = 1 - x if (k >> 2) & 1 else x
        py = 1 - y if (k >> 1) & 1 else y
        pc = 1 - c if k & 1 else c
        peers.append(((px, py, pc), 4 * px + 2 * py + pc))
    return me, peers


N_CHIPS = 4
OTHER_CHIPS = 3


def _chip_peers():
    x, y, c = lax.axis_index("x"), lax.axis_index("y"), lax.axis_index("c")
    return x, y, c, [(1 - x, y), (x, 1 - y), (1 - x, 1 - y)]


def _all_gather(shards):
    n_arr = len(shards)
    per = 1 + 2 * OTHER_CHIPS

    def body(*refs):
        in_refs, out_refs = refs[:n_arr], refs[n_arr:2 * n_arr]
        send_sems, recv_sems, loc_sems = refs[2 * n_arr:]
        x, y, c, chips = _chip_peers()
        me, sibling = (x, y, c), (x, y, 1 - c)

        def slot(px, py, pc):
            return 4 * px + 2 * py + pc

        def copy(i, k, block, to, src=None):
            dst = out_refs[i].at[slot(*block)]
            return pltpu.make_async_remote_copy(src_ref=dst if src is None else src, dst_ref=dst,
                                                send_sem=send_sems.at[i * per + k], recv_sem=recv_sems.at[i * per + k],
                                                device_id=to, device_id_type=MESH)

        own = [pltpu.make_async_copy(in_refs[i], out_refs[i].at[slot(*me)], loc_sems.at[i]) for i in range(n_arr)]
        for cp in own:
            cp.start()
        first = []
        for i in range(n_arr):
            first += [copy(i, 1 + j, me, (*chip, c), src=in_refs[i]) for j, chip in enumerate(chips)]
            first.append(copy(i, 0, me, sibling, src=in_refs[i]))
        for cp in first:
            cp.start()
        passed = []
        for j, chip in enumerate(chips):
            for i in range(n_arr):
                copy(i, 1 + j, (*chip, c), me).wait_recv()
                fwd = copy(i, 1 + OTHER_CHIPS + j, (*chip, c), sibling)
                fwd.start()
                passed.append(fwd)
        for i in range(n_arr):
            copy(i, 0, sibling, me).wait_recv()
            for j, chip in enumerate(chips):
                copy(i, 1 + OTHER_CHIPS + j, (*chip, 1 - c), me).wait_recv()
        for cp in first + passed:
            cp.wait_send()
        for cp in own:
            cp.wait()

    any_spec = pl.BlockSpec(memory_space=pl.ANY)
    return pl.pallas_call(
        body, name="weights_all_gather",
        out_shape=tuple(jax.ShapeDtypeStruct((N_DEV,) + a.shape, a.dtype) for a in shards),
        in_specs=[any_spec] * n_arr, out_specs=tuple([any_spec] * n_arr),
        scratch_shapes=[pltpu.SemaphoreType.DMA((n_arr * per,)), pltpu.SemaphoreType.DMA((n_arr * per,)),
                        pltpu.SemaphoreType.DMA((n_arr,))],
    )(*shards)


def _sibling_exchange(slabs):
    n_arr = len(slabs)

    def body(*refs):
        in_refs, out_refs = refs[:n_arr], refs[n_arr:2 * n_arr]
        send_sems, recv_sems = refs[2 * n_arr:]
        x, y, c, _ = _chip_peers()
        sends = []
        for i in range(n_arr):
            for q in range(N_CHIPS):
                cp = pltpu.make_async_remote_copy(src_ref=in_refs[i].at[2 * q + (1 - c)], dst_ref=out_refs[i].at[q],
                                                  send_sem=send_sems.at[i * N_CHIPS + q],
                                                  recv_sem=recv_sems.at[i * N_CHIPS + q],
                                                  device_id=(x, y, 1 - c), device_id_type=MESH)
                cp.start()
                sends.append(cp)
        for cp in sends:
            cp.wait_recv()
        for cp in sends:
            cp.wait_send()

    any_spec = pl.BlockSpec(memory_space=pl.ANY)
    return pl.pallas_call(
        body, name="grads_sibling_exchange",
        out_shape=tuple(jax.ShapeDtypeStruct((N_CHIPS,) + a.shape[1:], a.dtype) for a in slabs),
        in_specs=[any_spec] * n_arr, out_specs=tuple([any_spec] * n_arr),
        scratch_shapes=[pltpu.SemaphoreType.DMA((n_arr * N_CHIPS,)), pltpu.SemaphoreType.DMA((n_arr * N_CHIPS,))],
    )(*slabs)


def _pair_sum(slabs, from_sibling, core, name):
    _, rows, cols = slabs.shape
    tr = _row_block(rows, max(SUBLANES, (256 * 1024) // cols // SUBLANES * SUBLANES))

    def body(core_ref, a_ref, b_ref, o_ref):
        o_ref[...] = (a_ref[...] + b_ref[...]).astype(BF16)

    grid_spec = pltpu.PrefetchScalarGridSpec(
        num_scalar_prefetch=1, grid=(N_CHIPS, rows // tr),
        in_specs=[pl.BlockSpec((None, tr, cols), lambda q, r, core_ref: (2 * q + core_ref[0], r, 0)),
                  pl.BlockSpec((None, tr, cols), lambda q, r, core_ref: (q, r, 0))],
        out_specs=pl.BlockSpec((None, tr, cols), lambda q, r, core_ref: (q, r, 0)))
    return pl.pallas_call(
        body, name=name, grid_spec=grid_spec,
        out_shape=jax.ShapeDtypeStruct((N_CHIPS, rows, cols), BF16),
        compiler_params=_params("parallel", "parallel"),
    )(core, slabs, from_sibling)


def _chip_exchange(partials):
    n_arr = len(partials)

    def body(*refs):
        in_refs, out_refs = refs[:n_arr], refs[n_arr:2 * n_arr]
        send_sems, recv_sems, loc_sems = refs[2 * n_arr:]
        x, y, c, chips = _chip_peers()
        mine = 2 * x + y
        own = [pltpu.make_async_copy(in_refs[i].at[mine], out_refs[i].at[mine], loc_sems.at[i]) for i in range(n_arr)]
        for cp in own:
            cp.start()

        def copy(i, j, chip, src_slot, dst_slot):
            return pltpu.make_async_remote_copy(src_ref=in_refs[i].at[src_slot], dst_ref=out_refs[i].at[dst_slot],
                                                send_sem=send_sems.at[i * OTHER_CHIPS + j],
                                                recv_sem=recv_sems.at[i * OTHER_CHIPS + j],
                                                device_id=(*chip, c), device_id_type=MESH)

        sends = [copy(i, j, chip, 2 * chip[0] + chip[1], mine) for j, chip in enumerate(chips) for i in range(n_arr)]
        for cp in sends:
            cp.start()
        for j, chip in enumerate(chips):
            for i in range(n_arr):
                copy(i, j, chip, mine, 2 * chip[0] + chip[1]).wait_recv()
        for cp in sends:
            cp.wait_send()
        for cp in own:
            cp.wait()

    any_spec = pl.BlockSpec(memory_space=pl.ANY)
    return pl.pallas_call(
        body, name="grads_chip_exchange",
        out_shape=tuple(jax.ShapeDtypeStruct(a.shape, a.dtype) for a in partials),
        in_specs=[any_spec] * n_arr, out_specs=tuple([any_spec] * n_arr),
        scratch_shapes=[pltpu.SemaphoreType.DMA((n_arr * OTHER_CHIPS,)), pltpu.SemaphoreType.DMA((n_arr * OTHER_CHIPS,)),
                        pltpu.SemaphoreType.DMA((n_arr,))],
    )(*partials)


def _small_all_reduce(part):
    rows = part.shape[0]

    def body(p_ref, o_ref, buf_ref, send_sems, recv_sems):
        me, peers = _mesh_peers()
        buf_ref[me] = p_ref[...]
        sends = []
        for k, (dev, pid) in enumerate(peers):
            cp = pltpu.make_async_remote_copy(src_ref=p_ref, dst_ref=buf_ref.at[me], send_sem=send_sems.at[k],
                                              recv_sem=recv_sems.at[k], device_id=dev, device_id_type=MESH)
            cp.start()
            sends.append(cp)
        for k, (dev, pid) in enumerate(peers):
            pltpu.make_async_remote_copy(src_ref=p_ref, dst_ref=buf_ref.at[pid], send_sem=send_sems.at[k],
                                         recv_sem=recv_sems.at[k], device_id=dev, device_id_type=MESH).wait_recv()
        for cp in sends:
            cp.wait_send()
        acc = buf_ref[0]
        for i in range(1, N_DEV):
            acc = acc + buf_ref[i]
        o_ref[...] = acc

    vmem = pl.BlockSpec(memory_space=pltpu.VMEM)
    return pl.pallas_call(
        body, name="small_all_reduce",
        out_shape=jax.ShapeDtypeStruct(part.shape, F32),
        in_specs=[vmem], out_specs=vmem,
        scratch_shapes=[pltpu.VMEM((N_DEV, rows, LANES), F32), pltpu.SemaphoreType.DMA((N_DEV - 1,)),
                        pltpu.SemaphoreType.DMA((N_DEV - 1,))],
    )(part)


def _adamw_vals(w, g, m, v):
    m = ADAM_B1 * m + (1.0 - ADAM_B1) * g
    v = ADAM_B2 * v + (1.0 - ADAM_B2) * (g * g)
    m_hat = m / (1.0 - ADAM_B1 ** ADAM_STEP)
    v_hat = v / (1.0 - ADAM_B2 ** ADAM_STEP)
    delta = -ADAM_LR * (m_hat / (jnp.sqrt(v_hat) + ADAM_EPS) + ADAM_WD * w)
    return delta, m, v


def _adamw(contrib, w, m, v, name):
    n, rows, cols = contrib.shape
    tr = _row_block(rows, max(SUBLANES, (128 * 1024) // cols // SUBLANES * SUBLANES))

    def body(c_ref, w_ref, m_ref, v_ref, g_ref, d_ref, nm_ref, nv_ref):
        g = c_ref[0].astype(F32)
        for i in range(1, n):
            g = g + c_ref[i].astype(F32)
        delta, nm, nv = _adamw_vals(w_ref[...], g, m_ref[...], v_ref[...])
        g_ref[...] = g
        d_ref[...] = delta
        nm_ref[...] = nm
        nv_ref[...] = nv

    row = pl.BlockSpec((tr, cols), lambda i: (i, 0))
    shp = jax.ShapeDtypeStruct((rows, cols), F32)
    return pl.pallas_call(
        body, name=name, grid=(rows // tr,),
        in_specs=[pl.BlockSpec((n, tr, cols), lambda i: (0, i, 0)), row, row, row],
        out_specs=(row, row, row, row), out_shape=(shp, shp, shp, shp),
        compiler_params=_params("parallel"),
    )(contrib, w, m, v)


def _lane_block(n, cap):
    if n <= cap:
        return n
    best = None
    for cand in range(LANES, cap + 1, LANES):
        if n % cand == 0:
            best = cand
    assert best is not None, n
    return best


def _matmul(a, b, out_dtype, name, mode="nn", tm=1024, tn=1024, tk=1024):
    g = a.shape[0]
    m, k = (a.shape[2], a.shape[1]) if mode == "tn" else (a.shape[1], a.shape[2])
    n = b.shape[1] if mode == "nt" else b.shape[2]
    tm, tn, tk = _lane_block(m, tm), _lane_block(n, tn), _lane_block(k, tk)
    nk = k // tk
    a_spec = (pl.BlockSpec((None, tk, tm), lambda gi, i, j, kk: (gi, kk, i)) if mode == "tn" else
              pl.BlockSpec((None, tm, tk), lambda gi, i, j, kk: (gi, i, kk)))
    b_spec = (pl.BlockSpec((None, tn, tk), lambda gi, i, j, kk: (gi, j, kk)) if mode == "nt" else
              pl.BlockSpec((None, tk, tn), lambda gi, i, j, kk: (gi, kk, j)))
    ca, cb = (0 if mode == "tn" else 1), (1 if mode == "nt" else 0)

    def body(a_ref, b_ref, o_ref, *acc):
        part = _dg(a_ref[...], b_ref[...], ca, cb)
        if nk == 1:
            o_ref[...] = part.astype(o_ref.dtype)
            return
        acc_ref, = acc
        kk = pl.program_id(3)

        @pl.when(kk == 0)
        def _():
            acc_ref[...] = part

        @pl.when((kk > 0) & (kk < nk - 1))
        def _():
            acc_ref[...] += part

        @pl.when(kk == nk - 1)
        def _():
            o_ref[...] = (acc_ref[...] + part).astype(o_ref.dtype)

    return pl.pallas_call(
        body, name=name, grid=(g, m // tm, n // tn, nk),
        in_specs=[a_spec, b_spec],
        out_specs=pl.BlockSpec((None, tm, tn), lambda gi, i, j, kk: (gi, i, j)),
        out_shape=jax.ShapeDtypeStruct((g, m, n), out_dtype),
        scratch_shapes=[] if nk == 1 else [pltpu.VMEM((tm, tn), F32)],
        compiler_params=_params("parallel", "parallel", "parallel", "arbitrary"),
    )(a, b)


def _rms_fwd(x, w):
    s, d = x.shape
    tm = _row_block(s, 512)

    def body(x_ref, w_ref, h_ref):
        h_ref[...] = _rmsnorm(x_ref[...], w_ref[...]).astype(BF16)

    return pl.pallas_call(
        body, name="input_rmsnorm", grid=(s // tm,),
        in_specs=[pl.BlockSpec((tm, d), lambda i: (i, 0)), pl.BlockSpec((1, d), lambda i: (0, 0))],
        out_specs=pl.BlockSpec((tm, d), lambda i: (i, 0)),
        out_shape=jax.ShapeDtypeStruct((s, d), BF16),
        compiler_params=_params("parallel"),
    )(x, w)


def _rms_bwd(x, w, dh_parts, dx_res):
    s, d = x.shape
    tm = _row_block(s, 256)
    n_parts = len(dh_parts)

    def body(x_ref, w_ref, *rest):
        part_refs = rest[:n_parts]
        res_ref, gx_ref, gw_ref = rest[n_parts:]
        dh = part_refs[0][...]
        for r in part_refs[1:]:
            dh = dh + r[...]
        _, vjp = jax.vjp(_rmsnorm, x_ref[...], w_ref[...])
        dx, dw = vjp(dh)
        gx_ref[...] = dx + res_ref[...]

        @pl.when(pl.program_id(0) == 0)
        def _():
            gw_ref[...] = jnp.zeros_like(gw_ref)

        gw_ref[...] += dw

    row = pl.BlockSpec((tm, d), lambda i: (i, 0))
    vec = pl.BlockSpec((1, d), lambda i: (0, 0))
    return pl.pallas_call(
        body, name="input_rmsnorm_bwd", grid=(s // tm,),
        in_specs=[row, vec] + [row] * n_parts + [row],
        out_specs=(row, vec),
        out_shape=(jax.ShapeDtypeStruct((s, d), F32), jax.ShapeDtypeStruct((1, d), F32)),
        compiler_params=_params("arbitrary"),
    )(x, w, *dh_parts, dx_res)


def _lane_masks(rows):
    lane = lax.broadcasted_iota(jnp.int32, (rows, LANES), 1)
    return lane < HEAD_DIM, (lane & (HEAD_DIM - 1)) < HEAD_DIM // 2


def _swap_halves(t, lo_half):
    return jnp.where(lo_half, pltpu.roll(t, LANES - HEAD_DIM // 2, 1), pltpu.roll(t, HEAD_DIM // 2, 1))


def _rope(t, cos, sin_signed, lo_half):
    return t * cos + _swap_halves(t, lo_half) * sin_signed


def _rope_bwd(d, cos, sin_signed, lo_half):
    return d * cos - _swap_halves(d, lo_half) * sin_signed


def _window_mask(first):
    qi = lax.broadcasted_iota(jnp.int32, (ATT_BLOCK, 2 * ATT_BLOCK), 0)
    kj = lax.broadcasted_iota(jnp.int32, (ATT_BLOCK, 2 * ATT_BLOCK), 1)
    dist = qi + ATT_BLOCK - kj
    return (dist >= 0) & (dist <= N_BACK) & ((kj >= ATT_BLOCK) | jnp.logical_not(first))


def _blocks_per_subsequence(g, nb):
    return lax.shift_right_logical(jnp.int32(nb), 2 * g)


def _attn_fwd(qkv, cos, sin):
    _, s, _ = qkv.shape
    nb = s // ATT_BLOCK

    def body(qkv_ref, cos_ref, sin_ref, o_ref, lse_ref, kp_ref, vp_ref):
        g, t = pl.program_id(0), pl.program_id(1)
        first = (t & (_blocks_per_subsequence(g, nb) - 1)) == 0

        @pl.when(first)
        def _():
            kp_ref[...] = jnp.zeros_like(kp_ref)
            vp_ref[...] = jnp.zeros_like(vp_ref)

        cos_b, sin_b = cos_ref[...], sin_ref[...]
        head0, lo_half = _lane_masks(ATT_BLOCK)
        valid = _window_mask(first)
        for sl in range(WIDTH // LANES):
            cq = pl.ds(sl * LANES, LANES)
            ck = pl.ds(WIDTH + sl * LANES, LANES)
            cv = pl.ds(2 * WIDTH + sl * LANES, LANES)
            qr = (_rope(qkv_ref[:, cq], cos_b, sin_b, lo_half) * (HEAD_DIM ** -0.5)).astype(BF16)
            kr = _rope(qkv_ref[:, ck], cos_b, sin_b, lo_half).astype(BF16)
            v16 = qkv_ref[:, cv].astype(BF16)
            kcat = jnp.concatenate([kp_ref[:, cq], kr], axis=0)
            vcat = jnp.concatenate([vp_ref[:, cq], v16], axis=0)
            outs, lses = [], []
            for hm in (head0, jnp.logical_not(head0)):
                sc = _dg(jnp.where(hm, qr, jnp.zeros_like(qr)), kcat, 1, 1)
                sc = jnp.where(valid, sc, -jnp.inf)
                mx = jnp.max(sc, axis=1, keepdims=True)
                p = jnp.exp(sc - mx)
                den = jnp.sum(p, axis=1, keepdims=True)
                outs.append(_dg((p / den).astype(BF16), vcat, 1, 0))
                lses.append(mx + jnp.log(den))
            o_ref[:, cq] = jnp.where(head0, outs[0], outs[1])
            lse_ref[:, cq] = jnp.where(head0, lses[0], lses[1])
            kp_ref[:, cq] = kr
            vp_ref[:, cq] = v16

    blk = lambda w: pl.BlockSpec((None, ATT_BLOCK, w), lambda g, t: (g, t, 0))
    shp = jax.ShapeDtypeStruct((GROUPS, s, WIDTH), F32)
    return pl.pallas_call(
        body, name="dilated_attention_fwd", grid=(GROUPS, nb),
        in_specs=[blk(3 * WIDTH), blk(LANES), blk(LANES)],
        out_specs=(blk(WIDTH), blk(WIDTH)), out_shape=(shp, shp),
        scratch_shapes=[pltpu.VMEM((ATT_BLOCK, WIDTH), BF16), pltpu.VMEM((ATT_BLOCK, WIDTH), BF16)],
        compiler_params=_params("arbitrary", "arbitrary"),
    )(qkv, cos, sin)


def _attn_bwd(qkv, cos, sin, o, lse, do, dlse):
    _, s, _ = qkv.shape
    nb = s // ATT_BLOCK

    def body(qkv_ref, cos_ref, sin_ref, cosp_ref, sinp_ref, o_ref, lse_ref, do_ref, dlse_ref,
             dqkv_ref, kp_ref, vp_ref, dka_ref, dva_ref, dqp_ref):
        g, t = pl.program_id(0), pl.program_id(1)
        first = (t & (_blocks_per_subsequence(g, nb) - 1)) == 0
        active = t < nb
        head0, lo_half = _lane_masks(ATT_BLOCK)
        head0_2, _ = _lane_masks(2 * ATT_BLOCK)
        cos_p, sin_p = cosp_ref[...], sinp_ref[...]

        @pl.when(t == 0)
        def _():
            dka_ref[...] = jnp.zeros_like(dka_ref)
            dva_ref[...] = jnp.zeros_like(dva_ref)
            dqp_ref[...] = jnp.zeros_like(dqp_ref)

        dqkv_ref[:, pl.ds(0, WIDTH)] = dqp_ref[...]

        @pl.when(active & first)
        def _():
            kp_ref[...] = jnp.zeros_like(kp_ref)
            vp_ref[...] = jnp.zeros_like(vp_ref)

        @pl.when(active)
        def _():
            cos_b, sin_b = cos_ref[...], sin_ref[...]
            valid = _window_mask(first)
            for sl in range(WIDTH // LANES):
                cq = pl.ds(sl * LANES, LANES)
                ck = pl.ds(WIDTH + sl * LANES, LANES)
                cv = pl.ds(2 * WIDTH + sl * LANES, LANES)
                qr = (_rope(qkv_ref[:, cq], cos_b, sin_b, lo_half) * (HEAD_DIM ** -0.5)).astype(BF16)
                kr = _rope(qkv_ref[:, ck], cos_b, sin_b, lo_half).astype(BF16)
                v16 = qkv_ref[:, cv].astype(BF16)
                kcat = jnp.concatenate([kp_ref[:, cq], kr], axis=0)
                vcat = jnp.concatenate([vp_ref[:, cq], v16], axis=0)
                do_b = do_ref[:, cq]
                do16 = do_b.astype(BF16)
                lse_b = lse_ref[:, cq]
                cterm = dlse_ref[:, cq] - do_b * o_ref[:, cq]
                dqs, dks, dvs = [], [], []
                for hm in (head0, jnp.logical_not(head0)):
                    sc = _dg(jnp.where(hm, qr, jnp.zeros_like(qr)), kcat, 1, 1)
                    sc = jnp.where(valid, sc, -jnp.inf)
                    lse_h = jnp.max(jnp.where(hm, lse_b, -jnp.inf), axis=1, keepdims=True)
                    p = jnp.exp(sc - lse_h)
                    dp = _dg(jnp.where(hm, do16, jnp.zeros_like(do16)), vcat, 1, 1)
                    c = jnp.sum(jnp.where(hm, cterm, 0.0), axis=1, keepdims=True)
                    ds16 = (p * (dp + c)).astype(BF16)
                    dvs.append(_dg(p.astype(BF16), do16, 0, 0))
                    dqs.append(_dg(ds16, kcat, 1, 0))
                    dks.append(_dg(ds16, qr, 0, 0))
                dq = jnp.where(head0, dqs[0], dqs[1]) * (HEAD_DIM ** -0.5)
                dqp_ref[:, cq] = _rope_bwd(dq, cos_b, sin_b, lo_half).astype(BF16)
                dkc = jnp.where(head0_2, dks[0], dks[1])
                dvc = jnp.where(head0_2, dvs[0], dvs[1])
                dqkv_ref[:, ck] = _rope_bwd(dka_ref[:, cq] + dkc[:ATT_BLOCK], cos_p, sin_p, lo_half).astype(BF16)
                dqkv_ref[:, cv] = (dva_ref[:, cq] + dvc[:ATT_BLOCK]).astype(BF16)
                dka_ref[:, cq] = dkc[ATT_BLOCK:]
                dva_ref[:, cq] = dvc[ATT_BLOCK:]
                kp_ref[:, cq] = kr
                vp_ref[:, cq] = v16

        @pl.when(jnp.logical_not(active))
        def _():
            for sl in range(WIDTH // LANES):
                cq = pl.ds(sl * LANES, LANES)
                dqkv_ref[:, pl.ds(WIDTH + sl * LANES, LANES)] = _rope_bwd(dka_ref[:, cq], cos_p, sin_p, lo_half).astype(BF16)
                dqkv_ref[:, pl.ds(2 * WIDTH + sl * LANES, LANES)] = dva_ref[:, cq].astype(BF16)

    cur = lambda w: pl.BlockSpec((None, ATT_BLOCK, w), lambda g, t: (g, jnp.minimum(t, nb - 1), 0))
    prev = lambda w: pl.BlockSpec((None, ATT_BLOCK, w), lambda g, t: (g, jnp.maximum(t - 1, 0), 0))
    return pl.pallas_call(
        body, name="dilated_attention_bwd", grid=(GROUPS, nb + 1),
        in_specs=[cur(3 * WIDTH), cur(LANES), cur(LANES), prev(LANES), prev(LANES),
                  cur(WIDTH), cur(WIDTH), cur(WIDTH), cur(WIDTH)],
        out_specs=prev(3 * WIDTH), out_shape=jax.ShapeDtypeStruct((GROUPS, s, 3 * WIDTH), BF16),
        scratch_shapes=[pltpu.VMEM((ATT_BLOCK, WIDTH), BF16), pltpu.VMEM((ATT_BLOCK, WIDTH), BF16),
                        pltpu.VMEM((ATT_BLOCK, WIDTH), F32), pltpu.VMEM((ATT_BLOCK, WIDTH), F32),
                        pltpu.VMEM((ATT_BLOCK, WIDTH), BF16)],
        compiler_params=_params("arbitrary", "arbitrary"),
    )(qkv, cos, sin, cos, sin, o, lse, do, dlse)


CONV_PAD = SUBLANES


def _gdn_post(y, is_q, is_k):
    head0, _ = _lane_masks(y.shape[0])
    c = _silu(y)
    sq = c * c
    ss0 = jnp.sum(jnp.where(head0, sq, 0.0), axis=1, keepdims=True)
    ss1 = jnp.sum(jnp.where(head0, 0.0, sq), axis=1, keepdims=True)
    r = jnp.where(head0, lax.rsqrt(ss0 + NORM_EPS), lax.rsqrt(ss1 + NORM_EPS))
    scale = jnp.where(is_q, HEAD_DIM ** -0.5, 1.0).astype(F32)
    return jnp.where(is_q | is_k, c * r * scale, c)


def _conv_rows(xp_ref, w, c0, rows):
    y = w[0:1, :] * xp_ref[pl.ds(c0 + CONV_PAD - (CONV_K - 1), rows), :]
    for k in range(1, CONV_K):
        y = y + w[k:k + 1, :] * xp_ref[pl.ds(c0 + CONV_PAD - (CONV_K - 1) + k, rows), :]
    return y


def _gdn_pre_fwd(proj_r, conv8, col0):
    s = proj_r.shape[0]
    tr = _row_block(s, 512)
    nblk = QKV_B // LANES
    nq = WIDTH // LANES

    def body(x_ref, w_ref, out_ref, xp_ref):
        j = pl.program_id(0)
        is_q, is_k = j < nq, (j >= nq) & (j < 2 * nq)
        xp_ref[pl.ds(0, CONV_PAD), :] = jnp.zeros((CONV_PAD, LANES), F32)
        xp_ref[pl.ds(CONV_PAD, s), :] = x_ref[...]
        w = w_ref[...]
        for c in range(s // tr):
            out_ref[pl.ds(c * tr, tr), :] = _gdn_post(_conv_rows(xp_ref, w, c * tr, tr), is_q, is_k)

    return pl.pallas_call(
        body, name="gdn_conv_fwd", grid=(nblk,),
        in_specs=[pl.BlockSpec((s, LANES), lambda j: (0, col0 + j)), pl.BlockSpec((SUBLANES, LANES), lambda j: (0, j))],
        out_specs=pl.BlockSpec((s, LANES), lambda j: (0, j)),
        out_shape=jax.ShapeDtypeStruct((s, QKV_B), F32),
        scratch_shapes=[pltpu.VMEM((s + CONV_PAD, LANES), F32)],
        compiler_params=_params("parallel"),
    )(proj_r, conv8)


def _gdn_pre_bwd(proj_r, conv8, dc, col0):
    s = proj_r.shape[0]
    tr = _row_block(s, 512)
    nblk = QKV_B // LANES
    nq = WIDTH // LANES

    def body(x_ref, w_ref, dc_ref, dx_ref, dw_ref, xp_ref, dyp_ref):
        j = pl.program_id(0)
        is_q, is_k = j < nq, (j >= nq) & (j < 2 * nq)
        xp_ref[pl.ds(0, CONV_PAD), :] = jnp.zeros((CONV_PAD, LANES), F32)
        xp_ref[pl.ds(CONV_PAD, s), :] = x_ref[...]
        dyp_ref[pl.ds(s, CONV_PAD), :] = jnp.zeros((CONV_PAD, LANES), F32)
        w = w_ref[...]
        for c in range(s // tr):
            y = _conv_rows(xp_ref, w, c * tr, tr)
            _, vjp = jax.vjp(lambda yy: _gdn_post(yy, is_q, is_k), y)
            dyp_ref[pl.ds(c * tr, tr), :] = vjp(dc_ref[pl.ds(c * tr, tr), :])[0]
        dws = [jnp.zeros((1, LANES), F32) for _ in range(CONV_K)]
        for c in range(s // tr):
            c0 = c * tr
            dy = dyp_ref[pl.ds(c0, tr), :]
            dx = w[0:1, :] * dyp_ref[pl.ds(c0 + CONV_K - 1, tr), :]
            for k in range(1, CONV_K):
                dx = dx + w[k:k + 1, :] * dyp_ref[pl.ds(c0 + CONV_K - 1 - k, tr), :]
            dx_ref[pl.ds(c0, tr), :] = dx.astype(BF16)
            for k in range(CONV_K):
                xs = xp_ref[pl.ds(c0 + CONV_PAD - (CONV_K - 1) + k, tr), :]
                dws[k] = dws[k] + jnp.sum(dy * xs, axis=0, keepdims=True)
        row = lax.broadcasted_iota(jnp.int32, (SUBLANES, LANES), 0)
        dwb = jnp.zeros((SUBLANES, LANES), F32)
        for k in range(CONV_K):
            dwb = dwb + jnp.where(row == k, dws[k], 0.0)
        dw_ref[...] = dwb

    return pl.pallas_call(
        body, name="gdn_conv_bwd", grid=(nblk,),
        in_specs=[pl.BlockSpec((s, LANES), lambda j: (0, col0 + j)), pl.BlockSpec((SUBLANES, LANES), lambda j: (0, j)),
                  pl.BlockSpec((s, LANES), lambda j: (0, j))],
        out_specs=(pl.BlockSpec((s, LANES), lambda j: (0, j)), pl.BlockSpec((SUBLANES, LANES), lambda j: (0, j))),
        out_shape=(jax.ShapeDtypeStruct((s, QKV_B), BF16), jax.ShapeDtypeStruct((SUBLANES, QKV_B), F32)),
        scratch_shapes=[pltpu.VMEM((s + CONV_PAD, LANES), F32), pltpu.VMEM((s + CONV_PAD, LANES), F32)],
        compiler_params=_params("parallel"),
    )(proj_r, conv8, dc)


def _gdn_chunk(q, k, v, bcol, acol, alog, dtb, gnw, state):
    n = q.shape[-2]
    shp = (1, n, n)
    row = lax.broadcasted_iota(jnp.int32, shp, 1)
    col = lax.broadcasted_iota(jnp.int32, shp, 2)
    beta = _sigmoid(bcol)
    g = -jnp.exp(alog) * _softplus(acol + dtb)
    g_row = jnp.sum(jnp.where(row == col, g, 0.0), axis=-2, keepdims=True)
    big_g = jnp.sum(jnp.where(row >= col, g_row, 0.0), axis=-1, keepdims=True)
    big_g_row = jnp.sum(jnp.where(row <= col, g, 0.0), axis=-2, keepdims=True)
    decay_incl = jnp.exp(jnp.where(row >= col, big_g - big_g_row, -jnp.inf))
    decay_strict = jnp.where(row > col, decay_incl, 0.0)
    k_beta = k * beta
    t_inv = _tri_inv(_mm_nt(k_beta, k) * decay_strict)
    e_g = jnp.exp(big_g)
    u = _mm(t_inv, v * beta)
    w = _mm(t_inv, k_beta * e_g)
    attn = _mm_nt(q, k) * decay_incl
    v_new = u - _mm(w, state)
    o = _mm(q * e_g, state) + _mm(attn, v_new)
    total = jnp.sum(g, axis=-2, keepdims=True)
    new_state = state * jnp.exp(total) + _mm_tn(k * jnp.exp(total - big_g), v_new)
    return _rmsnorm(o, gnw), new_state


def _split_heads(x):
    return jnp.stack([x[:, h * HEAD_DIM:(h + 1) * HEAD_DIM] for h in range(HEADS)], axis=0)


def _merge_heads(x):
    return jnp.concatenate([x[h] for h in range(HEADS)], axis=1)


def _logit_columns(ba):
    lane = lax.broadcasted_iota(jnp.int32, ba.shape, 1)

    def cols(off):
        return jnp.stack([jnp.sum(jnp.where(lane == off + h, ba, 0.0), axis=1, keepdims=True) for h in range(HEADS)], axis=0)

    return cols(0), cols(HEADS)


def _logit_block(dbc, dac, shape):
    lane = lax.broadcasted_iota(jnp.int32, shape, 1)
    out = jnp.zeros(shape, F32)
    for h in range(HEADS):
        out = out + jnp.where(lane == h, dbc[h], 0.0) + jnp.where(lane == HEADS + h, dac[h], 0.0)
    return out


def _gdn_scan_fwd(cqkv, proj_r, ba_col, alog, dtb, gnw):
    s = cqkv.shape[0]
    nc = s // CHUNK

    def body(q_ref, k_ref, v_ref, ba_ref, al_ref, dt_ref, gnw_ref, o_ref, st_ref, state_ref):
        @pl.when(pl.program_id(0) == 0)
        def _():
            state_ref[...] = jnp.zeros_like(state_ref)

        st = state_ref[...]
        st_ref[...] = st
        bcol, acol = _logit_columns(ba_ref[...])
        o, new_st = _gdn_chunk(_split_heads(q_ref[...]), _split_heads(k_ref[...]), _split_heads(v_ref[...]), bcol, acol,
                               al_ref[...], dt_ref[...], gnw_ref[...], st)
        o_ref[...] = _merge_heads(o)
        state_ref[...] = new_st

    part = lambda i: pl.BlockSpec((CHUNK, WIDTH), lambda n: (n, i))
    par = pl.BlockSpec((HEADS, 1, 1), lambda n: (0, 0, 0))
    return pl.pallas_call(
        body, name="gdn_scan_fwd", grid=(nc,),
        in_specs=[part(0), part(1), part(2), pl.BlockSpec((CHUNK, LANES), lambda n: (n, ba_col)), par, par,
                  pl.BlockSpec((1, 1, HEAD_DIM), lambda n: (0, 0, 0))],
        out_specs=(part(0), pl.BlockSpec((None, HEADS, HEAD_DIM, HEAD_DIM), lambda n: (n, 0, 0, 0))),
        out_shape=(jax.ShapeDtypeStruct((s, WIDTH), F32),
                   jax.ShapeDtypeStruct((nc, HEADS, HEAD_DIM, HEAD_DIM), F32)),
        scratch_shapes=[pltpu.VMEM((HEADS, HEAD_DIM, HEAD_DIM), F32)],
        compiler_params=_params("arbitrary"),
    )(cqkv, cqkv, cqkv, proj_r, alog, dtb, gnw)


def _gdn_scan_bwd(cqkv, proj_r, ba_col, alog, dtb, gnw, states, do):
    s = cqkv.shape[0]
    nc = s // CHUNK

    def body(q_ref, k_ref, v_ref, ba_ref, al_ref, dt_ref, gnw_ref, st_ref, do_ref,
             dqkv_ref, dba_ref, dal_ref, ddt_ref, dgnw_ref, dstate_ref):
        @pl.when(pl.program_id(0) == 0)
        def _():
            dstate_ref[...] = jnp.zeros_like(dstate_ref)
            dal_ref[...] = jnp.zeros_like(dal_ref)
            ddt_ref[...] = jnp.zeros_like(ddt_ref)
            dgnw_ref[...] = jnp.zeros_like(dgnw_ref)

        bcol, acol = _logit_columns(ba_ref[...])
        _, vjp = jax.vjp(_gdn_chunk, _split_heads(q_ref[...]), _split_heads(k_ref[...]), _split_heads(v_ref[...]),
                         bcol, acol, al_ref[...], dt_ref[...], gnw_ref[...], st_ref[...])
        dq, dk, dv, dbc, dac, dal, ddt, dgn, dst = vjp((_split_heads(do_ref[...]), dstate_ref[...]))
        dqkv_ref[:, pl.ds(0, WIDTH)] = _merge_heads(dq)
        dqkv_ref[:, pl.ds(WIDTH, WIDTH)] = _merge_heads(dk)
        dqkv_ref[:, pl.ds(2 * WIDTH, WIDTH)] = _merge_heads(dv)
        dba_ref[...] = _logit_block(dbc, dac, dba_ref.shape)
        dstate_ref[...] = dst
        dal_ref[...] += dal
        ddt_ref[...] += ddt
        dgnw_ref[...] += dgn

    rev = lambda n: nc - 1 - n
    part = lambda i: pl.BlockSpec((CHUNK, WIDTH), lambda n: (rev(n), i))
    par = pl.BlockSpec((HEADS, 1, 1), lambda n: (0, 0, 0))
    vec = pl.BlockSpec((1, 1, HEAD_DIM), lambda n: (0, 0, 0))
    par_shape = jax.ShapeDtypeStruct((HEADS, 1, 1), F32)
    return pl.pallas_call(
        body, name="gdn_scan_bwd", grid=(nc,),
        in_specs=[part(0), part(1), part(2), pl.BlockSpec((CHUNK, LANES), lambda n: (rev(n), ba_col)), par, par, vec,
                  pl.BlockSpec((None, HEADS, HEAD_DIM, HEAD_DIM), lambda n: (rev(n), 0, 0, 0)), part(0)],
        out_specs=(pl.BlockSpec((CHUNK, QKV_B), lambda n: (rev(n), 0)), pl.BlockSpec((CHUNK, LANES), lambda n: (rev(n), 0)),
                   par, par, vec),
        out_shape=(jax.ShapeDtypeStruct((s, QKV_B), F32), jax.ShapeDtypeStruct((s, LANES), F32), par_shape, par_shape,
                   jax.ShapeDtypeStruct((1, 1, HEAD_DIM), F32)),
        scratch_shapes=[pltpu.VMEM((HEADS, HEAD_DIM, HEAD_DIM), F32)],
        compiler_params=_params("arbitrary"),
    )(cqkv, cqkv, cqkv, proj_r, alog, dtb, gnw, states, do)


def _tail_loss(x, tgt, o0, o1, o2, l0, l1, l2, ga, gb, za, zb, ob, wua, wub, wo, fnw):
    lm = jnp.maximum(jnp.maximum(l0, l1), l2)
    e0, e1, e2 = jnp.exp(l0 - lm), jnp.exp(l1 - lm), jnp.exp(l2 - lm)
    o_a = (e0 * o0 + e1 * o1 + e2 * o2) / (e0 + e1 + e2)
    y_a = _mm(o_a * _silu(za), wua)
    y_b = _mm(ob * _silu(zb), wub)
    merged = _sigmoid(ga) * y_a + _sigmoid(gb) * y_b
    y = _rmsnorm(x + _mm(merged, wo), fnw)
    err = y - tgt
    per_token = jnp.sum(err * err, axis=1, keepdims=True) * (0.5 / x.shape[1])
    return jnp.sum(per_token, axis=0, keepdims=True)


def _tail(x, tgt, og, lg, proj_r, ob, wua, wub, wo, fnw):
    s, d = x.shape
    tm = _row_block(s, 128)
    col_za = 2 * d // WIDTH
    col_zb = (2 * d + WIDTH + QKV_B) // WIDTH

    def body(x_ref, t_ref, o0_ref, o1_ref, o2_ref, l0_ref, l1_ref, l2_ref, ga_ref, gb_ref, za_ref, zb_ref, ob_ref,
             wua_ref, wub_ref, wo_ref, fnw_ref,
             loss_ref, dx_ref, do0_ref, do1_ref, do2_ref, dl0_ref, dl1_ref, dl2_ref, dga_ref, dgb_ref, dza_ref,
             dzb_ref, dob_ref, dwua_ref, dwub_ref, dwo_ref, dfnw_ref):
        @pl.when(pl.program_id(0) == 0)
        def _():
            for r in (loss_ref, dwua_ref, dwub_ref, dwo_ref, dfnw_ref):
                r[...] = jnp.zeros_like(r)

        args = (x_ref[...], t_ref[...], o0_ref[...], o1_ref[...], o2_ref[...], l0_ref[...], l1_ref[...], l2_ref[...],
                ga_ref[...], gb_ref[...], za_ref[...], zb_ref[...], ob_ref[...],
                wua_ref[...].astype(F32), wub_ref[...].astype(F32), wo_ref[...].astype(F32), fnw_ref[...])
        loss, vjp = jax.vjp(_tail_loss, *args)
        (dx, _, do0, do1, do2, dl0, dl1, dl2, dga, dgb, dza, dzb, dob, dwua, dwub, dwo, dfnw) = vjp(jnp.ones((1, 1), F32))
        loss_ref[...] += jnp.broadcast_to(loss, loss_ref.shape)
        dx_ref[...] = dx
        do0_ref[...], do1_ref[...], do2_ref[...] = do0, do1, do2
        dl0_ref[...], dl1_ref[...], dl2_ref[...] = dl0, dl1, dl2
        dga_ref[...] = dga.astype(BF16)
        dgb_ref[...] = dgb.astype(BF16)
        dza_ref[...] = dza.astype(BF16)
        dzb_ref[...] = dzb.astype(BF16)
        dob_ref[...] = dob
        dwua_ref[...] += dwua
        dwub_ref[...] += dwub
        dwo_ref[...] += dwo
        dfnw_ref[...] += dfnw

    row = lambda w, c=0: pl.BlockSpec((tm, w), lambda i: (i, c))
    full = lambda a, b: pl.BlockSpec((a, b), lambda i: (0, 0))
    f32 = lambda a, b: jax.ShapeDtypeStruct((a, b), F32)
    b16 = lambda a, b: jax.ShapeDtypeStruct((a, b), BF16)
    in_specs = ([row(d), row(d)] + [row(WIDTH)] * 6 + [row(d, 0), row(d, 1), row(WIDTH, col_za), row(WIDTH, col_zb),
                row(WIDTH), full(WIDTH, d), full(WIDTH, d), full(d, d), full(1, d)])
    out_specs = ([full(SUBLANES, LANES), row(d)] + [row(WIDTH)] * 6 + [row(d), row(d), row(WIDTH), row(WIDTH), row(WIDTH),
                 full(WIDTH, d), full(WIDTH, d), full(d, d), full(1, d)])
    out_shape = ([f32(SUBLANES, LANES), f32(s, d)] + [f32(s, WIDTH)] * 6 + [b16(s, d), b16(s, d), b16(s, WIDTH),
                 b16(s, WIDTH), f32(s, WIDTH), f32(WIDTH, d), f32(WIDTH, d), f32(d, d), f32(1, d)])
    return pl.pallas_call(
        body, name="tail_fwd_bwd", grid=(s // tm,),
        in_specs=in_specs, out_specs=tuple(out_specs), out_shape=tuple(out_shape),
        compiler_params=_params("arbitrary"),
    )(x, tgt, og[0], og[1], og[2], lg[0], lg[1], lg[2], proj_r, proj_r, proj_r, proj_r, ob, wua, wub, wo, fnw)


def _to_dilated(a, dil):
    if dil == 1:
        return a
    s = a.shape[0]
    return a.reshape(s // dil, dil, -1).transpose(1, 0, 2).reshape(a.shape)


def _from_dilated(a, dil):
    if dil == 1:
        return a
    s = a.shape[0]
    return a.reshape(dil, s // dil, -1).transpose(1, 0, 2).reshape(a.shape)


def _head_major(a):
    return a.reshape(a.shape[0], HEADS, HEAD_DIM).transpose(1, 0, 2)


def _from_head_major(a):
    return a.transpose(1, 0, 2).reshape(a.shape[1], WIDTH)


def _rope_tables(s):
    inv_freq = ROPE_THETA ** (-jnp.arange(0, HEAD_DIM, 2, dtype=F32) / HEAD_DIM)
    ang = jnp.arange(s, dtype=F32)[:, None] * inv_freq[None, :]
    cos_n = jnp.tile(jnp.cos(ang), (1, 2 * LANES // HEAD_DIM))
    sin_h = jnp.sin(ang)
    sin_n = jnp.tile(jnp.concatenate([-sin_h, sin_h], axis=1), (1, LANES // HEAD_DIM))
    return (jnp.stack([_to_dilated(cos_n, dil) for dil in DILATIONS]),
            jnp.stack([_to_dilated(sin_n, dil) for dil in DILATIONS]))


def _pack_rows(parts, dtype, row_multiple):
    flat = jnp.concatenate([p.reshape(-1).astype(dtype) for p in parts])
    tile = row_multiple * LANES
    pad = (-flat.shape[0]) % tile
    return jnp.pad(flat, (0, pad)).reshape(-1, LANES)


def _unpack_rows(packed, shapes):
    flat = packed.reshape(-1)
    out, start = [], 0
    for shp in shapes:
        size = 1
        for n in shp:
            size *= n
        out.append(flat[start:start + size].reshape(shp))
        start += size
    return out


def kernel(x, norm_w, w_in, conv_w, a_log, dt_bias, gdn_norm_w, w_up_a, w_up_b, w_out, final_norm_w, loss_target, m_norm_w, m_w_in, m_conv_w, m_a_log, m_dt_bias, m_gdn_norm_w, m_w_up_a, m_w_up_b, m_w_out, m_final_norm_w, v_norm_w, v_w_in, v_conv_w, v_a_log, v_dt_bias, v_gdn_norm_w, v_w_up_a, v_w_up_b, v_w_out, v_final_norm_w):
    x2, tgt = x[0], loss_target[0]
    s, d = x2.shape
    me = 4 * lax.axis_index("x") + 2 * lax.axis_index("y") + lax.axis_index("c")
    win8 = w_in.shape[2]
    conv8w = conv_w.shape[2]

    conv_shard = jnp.pad(conv_w[0], ((0, SUBLANES - CONV_K), (0, 0)))
    w_in_g, wua_g, wub_g, wo_g, conv_g = _all_gather(
        [w_in[0].astype(BF16), w_up_a[0].astype(BF16), w_up_b[0].astype(BF16), w_out[0].astype(BF16), conv_shard])
    w_in_f = jnp.concatenate([w_in_g[i] for i in range(N_DEV)], axis=1)
    wua = jnp.concatenate([wua_g[i] for i in range(N_DEV)], axis=1)
    wub = jnp.concatenate([wub_g[i] for i in range(N_DEV)], axis=1)
    wo = wo_g.reshape(d, d)
    conv8 = jnp.concatenate([conv_g[i] for i in range(N_DEV)], axis=1)

    w_qkv = w_in_f[:, :QKV_A].reshape(d, GROUPS, QKV_B).transpose(1, 0, 2)
    w_rest = jnp.concatenate([
        w_in_f[:, OFF_GATE:OFF_GATE + 2 * d], w_in_f[:, OFF_ZA:OFF_ZA + WIDTH], w_in_f[:, OFF_QKVB:OFF_QKVB + QKV_B],
        w_in_f[:, OFF_ZB:OFF_ZB + WIDTH], w_in_f[:, OFF_BA:OFF_BA + 2 * HEADS],
        jnp.zeros((d, BA_PAD - 2 * HEADS), BF16)], axis=1)
    col_qkvb = (2 * d + WIDTH) // LANES
    col_ba = (2 * d + 2 * WIDTH + QKV_B) // LANES

    h = _rms_fwd(x2, norm_w)
    h_all = jnp.stack([_to_dilated(h, dil) for dil in DILATIONS])
    qkv_all = _matmul(h_all, w_qkv, F32, "in_proj_attention")
    proj_r = _matmul(h[None], w_rest[None], F32, "in_proj_rest")[0]
    cos, sin = _rope_tables(s)
    o_all, lse_all = _attn_fwd(qkv_all, cos, sin)
    og = [_from_dilated(o_all[g], dil) for g, dil in enumerate(DILATIONS)]
    lg = [_from_dilated(lse_all[g], dil) for g, dil in enumerate(DILATIONS)]

    cqkv = _gdn_pre_fwd(proj_r, conv8, col_qkvb)
    alog3, dtb3, gnw3 = a_log.reshape(HEADS, 1, 1), dt_bias.reshape(HEADS, 1, 1), gdn_norm_w.reshape(1, 1, HEAD_DIM)
    ob, states = _gdn_scan_fwd(cqkv, proj_r, col_ba, alog3, dtb3, gnw3)

    (loss_blk, dx_res, do0, do1, do2, dl0, dl1, dl2, dga, dgb, dza, dzb, dob, dwua, dwub, dwo, dfnw) = _tail(
        x2, tgt, og, lg, proj_r, ob, wua, wub, wo, final_norm_w.reshape(1, d))

    do_all = jnp.stack([_to_dilated(t, dil) for t, dil in zip((do0, do1, do2), DILATIONS)])
    dl_all = jnp.stack([_to_dilated(t, dil) for t, dil in zip((dl0, dl1, dl2), DILATIONS)])
    dqkv_all = _attn_bwd(qkv_all, cos, sin, o_all, lse_all, do_all, dl_all)

    dcqkv, dba, dalog3, ddtb3, dgnw3 = _gdn_scan_bwd(cqkv, proj_r, col_ba, alog3, dtb3, gnw3, states, dob)
    dqkv_b, dconv8 = _gdn_pre_bwd(proj_r, conv8, dcqkv, col_qkvb)
    dproj_r = jnp.concatenate([dga, dgb, dza, dqkv_b, dzb,
                               jnp.pad(dba.astype(BF16), ((0, 0), (0, BA_PAD - LANES)))], axis=1)

    dw_qkv = _matmul(h_all, dqkv_all, F32, "in_proj_attention_dw", mode="tn")
    dw_rest = _matmul(h[None], dproj_r[None], F32, "in_proj_rest_dw", mode="tn")[0]
    dh_a = _matmul(dqkv_all, w_qkv, F32, "in_proj_attention_dh", mode="nt")
    dh_r = _matmul(dproj_r[None], w_rest[None], F32, "in_proj_rest_dh", mode="nt")[0]
    dh_parts = [dh_r] + [_from_dilated(dh_a[g], dil) for g, dil in enumerate(DILATIONS)]
    grad_x, dnorm_w = _rms_bwd(x2, norm_w, dh_parts, dx_res)

    o2 = 2 * d
    dw_in = jnp.concatenate([
        dw_qkv.transpose(1, 0, 2).reshape(d, QKV_A),
        dw_rest[:, o2:o2 + WIDTH], dw_rest[:, o2 + WIDTH:o2 + WIDTH + QKV_B],
        dw_rest[:, o2 + WIDTH + QKV_B:o2 + 2 * WIDTH + QKV_B],
        dw_rest[:, o2 + 2 * WIDTH + QKV_B:o2 + 2 * WIDTH + QKV_B + 2 * HEADS],
        dw_rest[:, :o2]], axis=1)

    def col_slabs(a, width):
        return jnp.stack([a[:, j * width:(j + 1) * width] for j in range(N_DEV)])

    slabs = [col_slabs(dw_in, win8), col_slabs(dwua, d // N_DEV), col_slabs(dwub, d // N_DEV),
             dwo.reshape(N_DEV, d // N_DEV, d)]
    from_sibling = _sibling_exchange(slabs)
    core = lax.axis_index("c").astype(jnp.int32).reshape(1)
    partials = [_pair_sum(a, b, core, "grads_pair_sum_%d" % i) for i, (a, b) in enumerate(zip(slabs, from_sibling))]
    contrib = _chip_exchange(partials)

    small_parts = [dnorm_w, dfnw, dconv8[:CONV_K], dalog3[:, 0, 0], ddtb3[:, 0, 0], dgnw3[0], loss_blk[0, 0:1]]
    small_rows = [-(-p.size // LANES) for p in small_parts]
    small = jnp.concatenate([jnp.pad(p.reshape(-1), (0, r * LANES - p.size)).reshape(r, LANES)
                             for p, r in zip(small_parts, small_rows)])
    small = jnp.pad(small, ((0, (-small.shape[0]) % SUBLANES), (0, 0)))
    small_sum = _small_all_reduce(small)
    pieces, r0 = [], 0
    for p, r in zip(small_parts, small_rows):
        pieces.append(small_sum[r0:r0 + r].reshape(-1)[:p.size].reshape(p.shape))
        r0 += r
    g_norm_w, g_fnw, g_conv_full, g_alog, g_dtb, g_gnw, loss_sum = pieces
    g_conv = lax.dynamic_slice(g_conv_full, (0, me * conv8w), (CONV_K, conv8w))

    big = [_adamw(c, w[0], m[0], v[0], name) for c, w, m, v, name in (
        (contrib[0], w_in, m_w_in, v_w_in, "adamw_w_in"), (contrib[1], w_up_a, m_w_up_a, v_w_up_a, "adamw_w_up_a"),
        (contrib[2], w_up_b, m_w_up_b, v_w_up_b, "adamw_w_up_b"), (contrib[3], w_out, m_w_out, v_w_out, "adamw_w_out"))]
    g_big, d_big, nm_big, nv_big = ([t[i] for t in big] for i in range(4))

    small_ws = [norm_w, final_norm_w, conv_w, a_log, dt_bias, gdn_norm_w]
    small_ms = [m_norm_w, m_final_norm_w, m_conv_w, m_a_log, m_dt_bias, m_gdn_norm_w]
    small_vs = [v_norm_w, v_final_norm_w, v_conv_w, v_a_log, v_dt_bias, v_gdn_norm_w]
    small_gs = [g_norm_w, g_fnw, g_conv, g_alog, g_dtb, g_gnw]
    small_shapes = [t.shape for t in small_ws]
    sm = _adamw(_pack_rows(small_gs, F32, SUBLANES)[None], _pack_rows(small_ws, F32, SUBLANES),
                _pack_rows(small_ms, F32, SUBLANES), _pack_rows(small_vs, F32, SUBLANES), "adamw_small")
    g_sm, d_sm, nm_sm, nv_sm = (_unpack_rows(t, small_shapes) for t in sm)

    def ordered(bigs, smalls):
        nw, fnw_, cw, al, dtb, gn = smalls
        wi, ua, ub, wo_ = (t[None] for t in bigs)
        return [nw, wi, cw, al, dtb, gn, ua, ub, wo_, fnw_]

    return (loss_sum.reshape(()), grad_x[None], *ordered(g_big, g_sm), *ordered(d_big, d_sm),
            *ordered(nm_big, nm_sm), *ordered(nv_big, nv_sm))
```

```python
import functools

import jax
import jax.numpy as jnp
from jax import lax
from jax.experimental import pallas as pl
from jax.experimental.pallas import tpu as pltpu

F32 = jnp.float32
BF16 = jnp.bfloat16
MESH = pl.DeviceIdType.MESH
N_DEV = 8
LANES = 128
SUBLANES = 8

GROUPS = 3
HEADS = 8
HEAD_DIM = 64
WIDTH = HEADS * HEAD_DIM
ATT_BLOCK = 128
DILATIONS = (1, 4, 16)
N_BACK = 128
CONV_K = 4
CHUNK = 64
QKV_B = 3 * WIDTH
QKV_A = GROUPS * 3 * WIDTH
BA_PAD = 512
NORM_EPS = 1e-6
ROPE_THETA = 10000.0
ADAM_LR, ADAM_B1, ADAM_B2, ADAM_EPS, ADAM_WD, ADAM_STEP = 0.001, 0.9, 0.999, 1e-08, 0.01, 10

VMEM_LIMIT = 56 * 1024 * 1024

OFF_ZA = QKV_A
OFF_QKVB = OFF_ZA + WIDTH
OFF_ZB = OFF_QKVB + QKV_B
OFF_BA = OFF_ZB + WIDTH
OFF_GATE = OFF_BA + 2 * HEADS


def _params(*sem):
    return pltpu.CompilerParams(dimension_semantics=sem, vmem_limit_bytes=VMEM_LIMIT)


def _dg(a, b, ca, cb):
    nb = a.ndim - 2
    batch = tuple(range(nb))
    return lax.dot_general(a, b, (((nb + ca,), (nb + cb,)), (batch, batch)), preferred_element_type=F32)


@jax.custom_vjp
def _mm(a, b):
    return _dg(a.astype(BF16), b.astype(BF16), 1, 0)


def _mm_fwd(a, b):
    return _mm(a, b), (a.astype(BF16), b.astype(BF16))


def _mm_bwd(res, ct):
    a16, b16 = res
    c16 = ct.astype(BF16)
    return _dg(c16, b16, 1, 1), _dg(a16, c16, 0, 0)


_mm.defvjp(_mm_fwd, _mm_bwd)


@jax.custom_vjp
def _mm_nt(a, b):
    return _dg(a.astype(BF16), b.astype(BF16), 1, 1)


def _mm_nt_fwd(a, b):
    return _mm_nt(a, b), (a.astype(BF16), b.astype(BF16))


def _mm_nt_bwd(res, ct):
    a16, b16 = res
    c16 = ct.astype(BF16)
    return _dg(c16, b16, 1, 0), _dg(c16, a16, 0, 0)


_mm_nt.defvjp(_mm_nt_fwd, _mm_nt_bwd)


@jax.custom_vjp
def _mm_tn(a, b):
    return _dg(a.astype(BF16), b.astype(BF16), 0, 0)


def _mm_tn_fwd(a, b):
    return _mm_tn(a, b), (a.astype(BF16), b.astype(BF16))


def _mm_tn_bwd(res, ct):
    a16, b16 = res
    c16 = ct.astype(BF16)
    return _dg(b16, c16, 1, 1), _dg(a16, c16, 1, 0)


_mm_tn.defvjp(_mm_tn_fwd, _mm_tn_bwd)


@jax.custom_vjp
def _mm_tap(a, w16, tap):
    return _dg(a.astype(BF16), w16, 1, 0)


def _mm_tap_fwd(a, w16, tap):
    return _mm_tap(a, w16, tap), (a.astype(BF16), w16)


def _mm_tap_bwd(res, ct):
    a16, w16 = res
    c16 = ct.astype(BF16)
    return _dg(c16, w16, 1, 1), jnp.zeros_like(w16), _dg(a16, c16, 0, 0)


_mm_tap.defvjp(_mm_tap_fwd, _mm_tap_bwd)


def _split16(a):
    hi = a.astype(BF16)
    lo = (a - hi.astype(F32)).astype(BF16)
    return hi, lo


def _dot3(a, b, ca, cb):
    ah, al = _split16(a)
    bh, bl = _split16(b)
    return _dg(ah, bh, ca, cb) + (_dg(ah, bl, ca, cb) + _dg(al, bh, ca, cb))


def _tri_inv_impl(a):
    n = a.shape[-1]
    shp = (1,) * (a.ndim - 2) + (n, n)
    eye = (lax.broadcasted_iota(jnp.int32, shp, a.ndim - 2) == lax.broadcasted_iota(jnp.int32, shp, a.ndim - 1)).astype(F32)
    x = eye - a
    p = a
    for it in range(5):
        dot = _dot3 if it < 2 else (lambda u, v, cu, cv: _dg(u.astype(BF16), v.astype(BF16), cu, cv))
        p = dot(p, p, 1, 0)
        x = x + dot(x, p, 1, 0)
    return x


@jax.custom_vjp
def _tri_inv(a):
    return _tri_inv_impl(a)


def _tri_inv_fwd(a):
    t = _tri_inv_impl(a)
    return t, t


def _tri_inv_bwd(t, ct):
    t16 = t.astype(BF16)
    return (-_dg(_dg(t16, ct.astype(BF16), 0, 0).astype(BF16), t16, 1, 1),)


_tri_inv.defvjp(_tri_inv_fwd, _tri_inv_bwd)


def _sigmoid(x):
    return 1.0 / (1.0 + jnp.exp(-x))


def _silu(x):
    return x * _sigmoid(x)


def _softplus(x):
    return jnp.maximum(x, 0.0) + jnp.log(1.0 + jnp.exp(-jnp.abs(x)))


def _rmsnorm(x, w):
    return x * lax.rsqrt(jnp.mean(x * x, axis=-1, keepdims=True) + NORM_EPS) * w


def _row_block(rows, cap):
    best = None
    for cand in range(SUBLANES, min(rows, cap) + 1, SUBLANES):
        if rows % cand == 0:
            best = cand
    assert best is not None, rows
    return best


def _mesh_peers():
    x, y, c = lax.axis_index("x"), lax.axis_index("y"), lax.axis_index("c")
    me = 4 * x + 2 * y + c
    peers = []
    for k in range(1, N_DEV):
        px = 1 - x if (k >> 2) & 1 else x
        py = 1 - y if (k >> 1) & 1 else y
        pc = 1 - c if k & 1 else c
        peers.append(((px, py, pc), 4 * px + 2 * py + pc))
    return me, peers


N_CHIPS = 4
OTHER_CHIPS = 3


def _chip_peers():
    x, y, c = lax.axis_index("x"), lax.axis_index("y"), lax.axis_index("c")
    return x, y, c, [(1 - x, y), (x, 1 - y), (1 - x, 1 - y)]


def _all_gather(shards):
    n_arr = len(shards)
    per = 1 + 2 * OTHER_CHIPS

    def body(*refs):
        in_refs, out_refs = refs[:n_arr], refs[n_arr:2 * n_arr]
        send_sems, recv_sems, loc_sems = refs[2 * n_arr:]
        x, y, c, chips = _chip_peers()
        me, sibling = (x, y, c), (x, y, 1 - c)

        def slot(px, py, pc):
            return 4 * px + 2 * py + pc

        def copy(i, k, block, to, src=None):
            dst = out_refs[i].at[slot(*block)]
            return pltpu.make_async_remote_copy(src_ref=dst if src is None else src, dst_ref=dst,
                                                send_sem=send_sems.at[i * per + k], recv_sem=recv_sems.at[i * per + k],
                                                device_id=to, device_id_type=MESH)

        own = [pltpu.make_async_copy(in_refs[i], out_refs[i].at[slot(*me)], loc_sems.at[i]) for i in range(n_arr)]
        for cp in own:
            cp.start()
        first = []
        for i in range(n_arr):
            first += [copy(i, 1 + j, me, (*chip, c), src=in_refs[i]) for j, chip in enumerate(chips)]
            first.append(copy(i, 0, me, sibling, src=in_refs[i]))
        for cp in first:
            cp.start()
        passed = []
        for j, chip in enumerate(chips):
            for i in range(n_arr):
                copy(i, 1 + j, (*chip, c), me).wait_recv()
                fwd = copy(i, 1 + OTHER_CHIPS + j, (*chip, c), sibling)
                fwd.start()
                passed.append(fwd)
        for i in range(n_arr):
            copy(i, 0, sibling, me).wait_recv()
            for j, chip in enumerate(chips):
                copy(i, 1 + OTHER_CHIPS + j, (*chip, 1 - c), me).wait_recv()
        for cp in first + passed:
            cp.wait_send()
        for cp in own:
            cp.wait()

    any_spec = pl.BlockSpec(memory_space=pl.ANY)
    return pl.pallas_call(
        body, name="weights_all_gather",
        out_shape=tuple(jax.ShapeDtypeStruct((N_DEV,) + a.shape, a.dtype) for a in shards),
        in_specs=[any_spec] * n_arr, out_specs=tuple([any_spec] * n_arr),
        scratch_shapes=[pltpu.SemaphoreType.DMA((n_arr * per,)), pltpu.SemaphoreType.DMA((n_arr * per,)),
                        pltpu.SemaphoreType.DMA((n_arr,))],
    )(*shards)


def _sibling_exchange(slabs):
    n_arr = len(slabs)

    def body(*refs):
        in_refs, out_refs = refs[:n_arr], refs[n_arr:2 * n_arr]
        send_sems, recv_sems = refs[2 * n_arr:]
        x, y, c, _ = _chip_peers()
        sends = []
        for i in range(n_arr):
            for q in range(N_CHIPS):
                cp = pltpu.make_async_remote_copy(src_ref=in_refs[i].at[2 * q + (1 - c)], dst_ref=out_refs[i].at[q],
                                                  send_sem=send_sems.at[i * N_CHIPS + q],
                                                  recv_sem=recv_sems.at[i * N_CHIPS + q],
                                                  device_id=(x, y, 1 - c), device_id_type=MESH)
                cp.start()
                sends.append(cp)
        for cp in sends:
            cp.wait_recv()
        for cp in sends:
            cp.wait_send()

    any_spec = pl.BlockSpec(memory_space=pl.ANY)
    return pl.pallas_call(
        body, name="grads_sibling_exchange",
        out_shape=tuple(jax.ShapeDtypeStruct((N_CHIPS,) + a.shape[1:], a.dtype) for a in slabs),
        in_specs=[any_spec] * n_arr, out_specs=tuple([any_spec] * n_arr),
        scratch_shapes=[pltpu.SemaphoreType.DMA((n_arr * N_CHIPS,)), pltpu.SemaphoreType.DMA((n_arr * N_CHIPS,))],
    )(*slabs)


def _pair_sum(slabs, from_sibling, core, name):
    _, rows, cols = slabs.shape
    tr = _row_block(rows, max(SUBLANES, (256 * 1024) // cols // SUBLANES * SUBLANES))

    def body(core_ref, a_ref, b_ref, o_ref):
        o_ref[...] = (a_ref[...] + b_ref[...]).astype(BF16)

    grid_spec = pltpu.PrefetchScalarGridSpec(
        num_scalar_prefetch=1, grid=(N_CHIPS, rows // tr),
        in_specs=[pl.BlockSpec((None, tr, cols), lambda q, r, core_ref: (2 * q + core_ref[0], r, 0)),
                  pl.BlockSpec((None, tr, cols), lambda q, r, core_ref: (q, r, 0))],
        out_specs=pl.BlockSpec((None, tr, cols), lambda q, r, core_ref: (q, r, 0)))
    return pl.pallas_call(
        body, name=name, grid_spec=grid_spec,
        out_shape=jax.ShapeDtypeStruct((N_CHIPS, rows, cols), BF16),
        compiler_params=_params("parallel", "parallel"),
    )(core, slabs, from_sibling)


def _chip_exchange(partials):
    n_arr = len(partials)

    def body(*refs):
        in_refs, out_refs = refs[:n_arr], refs[n_arr:2 * n_arr]
        send_sems, recv_sems, loc_sems = refs[2 * n_arr:]
        x, y, c, chips = _chip_peers()
        mine = 2 * x + y
        own = [pltpu.make_async_copy(in_refs[i].at[mine], out_refs[i].at[mine], loc_sems.at[i]) for i in range(n_arr)]
        for cp in own:
            cp.start()

        def copy(i, j, chip, src_slot, dst_slot):
            return pltpu.make_async_remote_copy(src_ref=in_refs[i].at[src_slot], dst_ref=out_refs[i].at[dst_slot],
                                                send_sem=send_sems.at[i * OTHER_CHIPS + j],
                                                recv_sem=recv_sems.at[i * OTHER_CHIPS + j],
                                                device_id=(*chip, c), device_id_type=MESH)

        sends = [copy(i, j, chip, 2 * chip[0] + chip[1], mine) for j, chip in enumerate(chips) for i in range(n_arr)]
        for cp in sends:
            cp.start()
        for j, chip in enumerate(chips):
            for i in range(n_arr):
                copy(i, j, chip, mine, 2 * chip[0] + chip[1]).wait_recv()
        for cp in sends:
            cp.wait_send()
        for cp in own:
            cp.wait()

    any_spec = pl.BlockSpec(memory_space=pl.ANY)
    return pl.pallas_call(
        body, name="grads_chip_exchange",
        out_shape=tuple(jax.ShapeDtypeStruct(a.shape, a.dtype) for a in partials),
        in_specs=[any_spec] * n_arr, out_specs=tuple([any_spec] * n_arr),
        scratch_shapes=[pltpu.SemaphoreType.DMA((n_arr * OTHER_CHIPS,)), pltpu.SemaphoreType.DMA((n_arr * OTHER_CHIPS,)),
                        pltpu.SemaphoreType.DMA((n_arr,))],
    )(*partials)


def _small_all_reduce(part):
    rows = part.shape[0]

    def body(p_ref, o_ref, buf_ref, send_sems, recv_sems):
        me, peers = _mesh_peers()
        buf_ref[me] = p_ref[...]
        sends = []
        for k, (dev, pid) in enumerate(peers):
            cp = pltpu.make_async_remote_copy(src_ref=p_ref, dst_ref=buf_ref.at[me], send_sem=send_sems.at[k],
                                              recv_sem=recv_sems.at[k], device_id=dev, device_id_type=MESH)
            cp.start()
            sends.append(cp)
        for k, (dev, pid) in enumerate(peers):
            pltpu.make_async_remote_copy(src_ref=p_ref, dst_ref=buf_ref.at[pid], send_sem=send_sems.at[k],
                                         recv_sem=recv_sems.at[k], device_id=dev, device_id_type=MESH).wait_recv()
        for cp in sends:
            cp.wait_send()
        acc = buf_ref[0]
        for i in range(1, N_DEV):
            acc = acc + buf_ref[i]
        o_ref[...] = acc

    vmem = pl.BlockSpec(memory_space=pltpu.VMEM)
    return pl.pallas_call(
        body, name="small_all_reduce",
        out_shape=jax.ShapeDtypeStruct(part.shape, F32),
        in_specs=[vmem], out_specs=vmem,
        scratch_shapes=[pltpu.VMEM((N_DEV, rows, LANES), F32), pltpu.SemaphoreType.DMA((N_DEV - 1,)),
                        pltpu.SemaphoreType.DMA((N_DEV - 1,))],
    )(part)


def _adamw_vals(w, g, m, v):
    m = ADAM_B1 * m + (1.0 - ADAM_B1) * g
    v = ADAM_B2 * v + (1.0 - ADAM_B2) * (g * g)
    m_hat = m / (1.0 - ADAM_B1 ** ADAM_STEP)
    v_hat = v / (1.0 - ADAM_B2 ** ADAM_STEP)
    delta = -ADAM_LR * (m_hat / (jnp.sqrt(v_hat) + ADAM_EPS) + ADAM_WD * w)
    return delta, m, v


def _adamw(contrib, w, m, v, name):
    n, rows, cols = contrib.shape
    tr = _row_block(rows, max(SUBLANES, (128 * 1024) // cols // SUBLANES * SUBLANES))

    def body(c_ref, w_ref, m_ref, v_ref, g_ref, d_ref, nm_ref, nv_ref):
        g = c_ref[0].astype(F32)
        for i in range(1, n):
            g = g + c_ref[i].astype(F32)
        delta, nm, nv = _adamw_vals(w_ref[...], g, m_ref[...], v_ref[...])
        g_ref[...] = g
        d_ref[...] = delta
        nm_ref[...] = nm
        nv_ref[...] = nv

    row = pl.BlockSpec((tr, cols), lambda i: (i, 0))
    shp = jax.ShapeDtypeStruct((rows, cols), F32)
    return pl.pallas_call(
        body, name=name, grid=(rows // tr,),
        in_specs=[pl.BlockSpec((n, tr, cols), lambda i: (0, i, 0)), row, row, row],
        out_specs=(row, row, row, row), out_shape=(shp, shp, shp, shp),
        compiler_params=_params("parallel"),
    )(contrib, w, m, v)


def _lane_block(n, cap):
    if n <= cap:
        return n
    best = None
    for cand in range(LANES, cap + 1, LANES):
        if n % cand == 0:
            best = cand
    assert best is not None, n
    return best


def _matmul(a, b, out_dtype, name, mode="nn", tm=1024, tn=1024, tk=1024):
    g = a.shape[0]
    m, k = (a.shape[2], a.shape[1]) if mode == "tn" else (a.shape[1], a.shape[2])
    n = b.shape[1] if mode == "nt" else b.shape[2]
    tm, tn, tk = _lane_block(m, tm), _lane_block(n, tn), _lane_block(k, tk)
    nk = k // tk
    a_spec = (pl.BlockSpec((None, tk, tm), lambda gi, i, j, kk: (gi, kk, i)) if mode == "tn" else
              pl.BlockSpec((None, tm, tk), lambda gi, i, j, kk: (gi, i, kk)))
    b_spec = (pl.BlockSpec((None, tn, tk), lambda gi, i, j, kk: (gi, j, kk)) if mode == "nt" else
              pl.BlockSpec((None, tk, tn), lambda gi, i, j, kk: (gi, kk, j)))
    ca, cb = (0 if mode == "tn" else 1), (1 if mode == "nt" else 0)

    def body(a_ref, b_ref, o_ref, *acc):
        part = _dg(a_ref[...], b_ref[...], ca, cb)
        if nk == 1:
            o_ref[...] = part.astype(o_ref.dtype)
            return
        acc_ref, = acc
        kk = pl.program_id(3)

        @pl.when(kk == 0)
        def _():
            acc_ref[...] = part

        @pl.when((kk > 0) & (kk < nk - 1))
        def _():
            acc_ref[...] += part

        @pl.when(kk == nk - 1)
        def _():
            o_ref[...] = (acc_ref[...] + part).astype(o_ref.dtype)

    return pl.pallas_call(
        body, name=name, grid=(g, m // tm, n // tn, nk),
        in_specs=[a_spec, b_spec],
        out_specs=pl.BlockSpec((None, tm, tn), lambda gi, i, j, kk: (gi, i, j)),
        out_shape=jax.ShapeDtypeStruct((g, m, n), out_dtype),
        scratch_shapes=[] if nk == 1 else [pltpu.VMEM((tm, tn), F32)],
        compiler_params=_params("parallel", "parallel", "parallel", "arbitrary"),
    )(a, b)


def _rms_fwd(x, w):
    s, d = x.shape
    tm = _row_block(s, 512)

    def body(x_ref, w_ref, h_ref):
        h_ref[...] = _rmsnorm(x_ref[...], w_ref[...]).astype(BF16)

    return pl.pallas_call(
        body, name="input_rmsnorm", grid=(s // tm,),
        in_specs=[pl.BlockSpec((tm, d), lambda i: (i, 0)), pl.BlockSpec((1, d), lambda i: (0, 0))],
        out_specs=pl.BlockSpec((tm, d), lambda i: (i, 0)),
        out_shape=jax.ShapeDtypeStruct((s, d), BF16),
        compiler_params=_params("parallel"),
    )(x, w)


def _rms_bwd(x, w, dh_stacked, dh_parts, dx_res):
    s, d = x.shape
    tm = _row_block(s, 256)
    n_parts = 1 + len(dh_parts)

    def body(x_ref, w_ref, *rest):
        part_refs = rest[:n_parts]
        res_ref, gx_ref, gw_ref = rest[n_parts:]
        dh = part_refs[0][...]
        for r in part_refs[1:]:
            dh = dh + r[...]
        _, vjp = jax.vjp(_rmsnorm, x_ref[...], w_ref[...])
        dx, dw = vjp(dh)
        gx_ref[...] = dx + res_ref[...]

        @pl.when(pl.program_id(0) == 0)
        def _():
            gw_ref[...] = jnp.zeros_like(gw_ref)

        gw_ref[...] += dw

    row = pl.BlockSpec((tm, d), lambda i: (i, 0))
    vec = pl.BlockSpec((1, d), lambda i: (0, 0))
    return pl.pallas_call(
        body, name="input_rmsnorm_bwd", grid=(s // tm,),
        in_specs=[row, vec, pl.BlockSpec((None, tm, d), lambda i: (0, i, 0))] + [row] * (n_parts - 1) + [row],
        out_specs=(row, vec),
        out_shape=(jax.ShapeDtypeStruct((s, d), F32), jax.ShapeDtypeStruct((1, d), F32)),
        compiler_params=_params("arbitrary"),
    )(x, w, dh_stacked, *dh_parts, dx_res)


def _lane_masks(rows):
    lane = lax.broadcasted_iota(jnp.int32, (rows, LANES), 1)
    return lane < HEAD_DIM, (lane & (HEAD_DIM - 1)) < HEAD_DIM // 2


def _swap_halves(t, lo_half):
    return jnp.where(lo_half, pltpu.roll(t, LANES - HEAD_DIM // 2, 1), pltpu.roll(t, HEAD_DIM // 2, 1))


def _rope(t, cos, sin_signed, lo_half):
    return t * cos + _swap_halves(t, lo_half) * sin_signed


def _rope_bwd(d, cos, sin_signed, lo_half):
    return d * cos - _swap_halves(d, lo_half) * sin_signed


def _window_mask(first):
    qi = lax.broadcasted_iota(jnp.int32, (ATT_BLOCK, 2 * ATT_BLOCK), 0)
    kj = lax.broadcasted_iota(jnp.int32, (ATT_BLOCK, 2 * ATT_BLOCK), 1)
    dist = qi + ATT_BLOCK - kj
    return (dist >= 0) & (dist <= N_BACK) & ((kj >= ATT_BLOCK) | jnp.logical_not(first))


def _blocks_per_subsequence(g, nb):
    return lax.shift_right_logical(jnp.int32(nb), 2 * g)


def _attn_fwd(qkv, cos, sin):
    _, s, _ = qkv.shape
    nb = s // ATT_BLOCK

    def body(qkv_ref, cos_ref, sin_ref, o_ref, lse_ref, kp_ref, vp_ref):
        g, t = pl.program_id(0), pl.program_id(1)
        first = (t & (_blocks_per_subsequence(g, nb) - 1)) == 0

        @pl.when(first)
        def _():
            kp_ref[...] = jnp.zeros_like(kp_ref)
            vp_ref[...] = jnp.zeros_like(vp_ref)

        cos_b, sin_b = cos_ref[...], sin_ref[...]
        head0, lo_half = _lane_masks(ATT_BLOCK)
        valid = _window_mask(first)
        for sl in range(WIDTH // LANES):
            cq = pl.ds(sl * LANES, LANES)
            ck = pl.ds(WIDTH + sl * LANES, LANES)
            cv = pl.ds(2 * WIDTH + sl * LANES, LANES)
            qr = (_rope(qkv_ref[:, cq], cos_b, sin_b, lo_half) * (HEAD_DIM ** -0.5)).astype(BF16)
            kr = _rope(qkv_ref[:, ck], cos_b, sin_b, lo_half).astype(BF16)
            v16 = qkv_ref[:, cv].astype(BF16)
            kcat = jnp.concatenate([kp_ref[:, cq], kr], axis=0)
            vcat = jnp.concatenate([vp_ref[:, cq], v16], axis=0)
            outs, lses = [], []
            for hm in (head0, jnp.logical_not(head0)):
                sc = _dg(jnp.where(hm, qr, jnp.zeros_like(qr)), kcat, 1, 1)
                sc = jnp.where(valid, sc, -jnp.inf)
                mx = jnp.max(sc, axis=1, keepdims=True)
                p = jnp.exp(sc - mx)
                den = jnp.sum(p, axis=1, keepdims=True)
                outs.append(_dg((p * (1.0 / den)).astype(BF16), vcat, 1, 0))
                lses.append(mx + jnp.log(den))
            o_ref[:, cq] = jnp.where(head0, outs[0], outs[1])
            lse_ref[:, cq] = jnp.where(head0, lses[0], lses[1])
            kp_ref[:, cq] = kr
            vp_ref[:, cq] = v16

    blk = lambda w: pl.BlockSpec((None, ATT_BLOCK, w), lambda g, t: (g, t, 0))
    shp = jax.ShapeDtypeStruct((GROUPS, s, WIDTH), F32)
    return pl.pallas_call(
        body, name="dilated_attention_fwd", grid=(GROUPS, nb),
        in_specs=[blk(3 * WIDTH), blk(LANES), blk(LANES)],
        out_specs=(blk(WIDTH), blk(WIDTH)), out_shape=(shp, shp),
        scratch_shapes=[pltpu.VMEM((ATT_BLOCK, WIDTH), BF16), pltpu.VMEM((ATT_BLOCK, WIDTH), BF16)],
        compiler_params=_params("arbitrary", "arbitrary"),
    )(qkv, cos, sin)


def _attn_bwd(qkv, cos, sin, o, lse, do, dlse):
    _, s, _ = qkv.shape
    nb = s // ATT_BLOCK

    def body(qkv_ref, cos_ref, sin_ref, cosp_ref, sinp_ref, o_ref, lse_ref, do_ref, dlse_ref,
             dqkv_ref, kp_ref, vp_ref, dka_ref, dva_ref, dqp_ref):
        g, t = pl.program_id(0), pl.program_id(1)
        first = (t & (_blocks_per_subsequence(g, nb) - 1)) == 0
        active = t < nb
        head0, lo_half = _lane_masks(ATT_BLOCK)
        head0_2, _ = _lane_masks(2 * ATT_BLOCK)
        cos_p, sin_p = cosp_ref[...], sinp_ref[...]

        @pl.when(t == 0)
        def _():
            dka_ref[...] = jnp.zeros_like(dka_ref)
            dva_ref[...] = jnp.zeros_like(dva_ref)
            dqp_ref[...] = jnp.zeros_like(dqp_ref)

        dqkv_ref[:, pl.ds(0, WIDTH)] = dqp_ref[...]

        @pl.when(active & first)
        def _():
            kp_ref[...] = jnp.zeros_like(kp_ref)
            vp_ref[...] = jnp.zeros_like(vp_ref)

        @pl.when(active)
        def _():
            cos_b, sin_b = cos_ref[...], sin_ref[...]
            valid = _window_mask(first)
            for sl in range(WIDTH // LANES):
                cq = pl.ds(sl * LANES, LANES)
                ck = pl.ds(WIDTH + sl * LANES, LANES)
                cv = pl.ds(2 * WIDTH + sl * LANES, LANES)
                qr = (_rope(qkv_ref[:, cq], cos_b, sin_b, lo_half) * (HEAD_DIM ** -0.5)).astype(BF16)
                kr = _rope(qkv_ref[:, ck], cos_b, sin_b, lo_half).astype(BF16)
                v16 = qkv_ref[:, cv].astype(BF16)
                kcat = jnp.concatenate([kp_ref[:, cq], kr], axis=0)
                vcat = jnp.concatenate([vp_ref[:, cq], v16], axis=0)
                do_b = do_ref[:, cq]
                do16 = do_b.astype(BF16)
                lse_b = lse_ref[:, cq]
                cterm = dlse_ref[:, cq] - do_b * o_ref[:, cq]
                dqs, dks, dvs = [], [], []
                for hm in (head0, jnp.logical_not(head0)):
                    sc = _dg(jnp.where(hm, qr, jnp.zeros_like(qr)), kcat, 1, 1)
                    sc = jnp.where(valid, sc, -jnp.inf)
                    lse_h = jnp.max(jnp.where(hm, lse_b, -jnp.inf), axis=1, keepdims=True)
                    p = jnp.exp(sc - lse_h)
                    dp = _dg(jnp.where(hm, do16, jnp.zeros_like(do16)), vcat, 1, 1)
                    c = jnp.sum(jnp.where(hm, cterm, 0.0), axis=1, keepdims=True)
                    ds16 = (p * (dp + c)).astype(BF16)
                    dvs.append(_dg(p.astype(BF16), do16, 0, 0))
                    dqs.append(_dg(ds16, kcat, 1, 0))
                    dks.append(_dg(ds16, qr, 0, 0))
                dq = jnp.where(head0, dqs[0], dqs[1]) * (HEAD_DIM ** -0.5)
                dqp_ref[:, cq] = _rope_bwd(dq, cos_b, sin_b, lo_half).astype(BF16)
                dkc = jnp.where(head0_2, dks[0], dks[1])
                dvc = jnp.where(head0_2, dvs[0], dvs[1])
                dqkv_ref[:, ck] = _rope_bwd(dka_ref[:, cq] + dkc[:ATT_BLOCK], cos_p, sin_p, lo_half).astype(BF16)
                dqkv_ref[:, cv] = (dva_ref[:, cq] + dvc[:ATT_BLOCK]).astype(BF16)
                dka_ref[:, cq] = dkc[ATT_BLOCK:]
                dva_ref[:, cq] = dvc[ATT_BLOCK:]
                kp_ref[:, cq] = kr
                vp_ref[:, cq] = v16

        @pl.when(jnp.logical_not(active))
        def _():
            for sl in range(WIDTH // LANES):
                cq = pl.ds(sl * LANES, LANES)
                dqkv_ref[:, pl.ds(WIDTH + sl * LANES, LANES)] = _rope_bwd(dka_ref[:, cq], cos_p, sin_p, lo_half).astype(BF16)
                dqkv_ref[:, pl.ds(2 * WIDTH + sl * LANES, LANES)] = dva_ref[:, cq].astype(BF16)

    cur = lambda w: pl.BlockSpec((None, ATT_BLOCK, w), lambda g, t: (g, jnp.minimum(t, nb - 1), 0))
    prev = lambda w: pl.BlockSpec((None, ATT_BLOCK, w), lambda g, t: (g, jnp.maximum(t - 1, 0), 0))
    return pl.pallas_call(
        body, name="dilated_attention_bwd", grid=(GROUPS, nb + 1),
        in_specs=[cur(3 * WIDTH), cur(LANES), cur(LANES), prev(LANES), prev(LANES),
                  cur(WIDTH), cur(WIDTH), cur(WIDTH), cur(WIDTH)],
        out_specs=prev(3 * WIDTH), out_shape=jax.ShapeDtypeStruct((GROUPS, s, 3 * WIDTH), BF16),
        scratch_shapes=[pltpu.VMEM((ATT_BLOCK, WIDTH), BF16), pltpu.VMEM((ATT_BLOCK, WIDTH), BF16),
                        pltpu.VMEM((ATT_BLOCK, WIDTH), F32), pltpu.VMEM((ATT_BLOCK, WIDTH), F32),
                        pltpu.VMEM((ATT_BLOCK, WIDTH), BF16)],
        compiler_params=_params("arbitrary", "arbitrary"),
    )(qkv, cos, sin, cos, sin, o, lse, do, dlse)


CONV_PAD = SUBLANES


def _gdn_post(y, is_q, is_k):
    head0, _ = _lane_masks(y.shape[0])
    c = _silu(y)
    sq = c * c
    ss0 = jnp.sum(jnp.where(head0, sq, 0.0), axis=1, keepdims=True)
    ss1 = jnp.sum(jnp.where(head0, 0.0, sq), axis=1, keepdims=True)
    r = jnp.where(head0, lax.rsqrt(ss0 + NORM_EPS), lax.rsqrt(ss1 + NORM_EPS))
    scale = jnp.where(is_q, HEAD_DIM ** -0.5, 1.0).astype(F32)
    return jnp.where(is_q | is_k, c * r * scale, c)


def _conv_rows(xp_ref, w, c0, rows):
    y = w[0:1, :] * xp_ref[pl.ds(c0 + CONV_PAD - (CONV_K - 1), rows), :]
    for k in range(1, CONV_K):
        y = y + w[k:k + 1, :] * xp_ref[pl.ds(c0 + CONV_PAD - (CONV_K - 1) + k, rows), :]
    return y


def _gdn_pre_fwd(proj_r, conv8, col0):
    s = proj_r.shape[0]
    tr = _row_block(s, 512)
    nblk = QKV_B // LANES
    nq = WIDTH // LANES

    def body(x_ref, w_ref, out_ref, xp_ref):
        j = pl.program_id(0)
        is_q, is_k = j < nq, (j >= nq) & (j < 2 * nq)
        xp_ref[pl.ds(0, CONV_PAD), :] = jnp.zeros((CONV_PAD, LANES), F32)
        xp_ref[pl.ds(CONV_PAD, s), :] = x_ref[...]
        w = w_ref[...]
        for c in range(s // tr):
            out_ref[pl.ds(c * tr, tr), :] = _gdn_post(_conv_rows(xp_ref, w, c * tr, tr), is_q, is_k)

    return pl.pallas_call(
        body, name="gdn_conv_fwd", grid=(nblk,),
        in_specs=[pl.BlockSpec((s, LANES), lambda j: (0, col0 + j)), pl.BlockSpec((SUBLANES, LANES), lambda j: (0, j))],
        out_specs=pl.BlockSpec((s, LANES), lambda j: (0, j)),
        out_shape=jax.ShapeDtypeStruct((s, QKV_B), F32),
        scratch_shapes=[pltpu.VMEM((s + CONV_PAD, LANES), F32)],
        compiler_params=_params("parallel"),
    )(proj_r, conv8)


def _gdn_pre_bwd(proj_r, conv8, dc, col0):
    s = proj_r.shape[0]
    tr = _row_block(s, 512)
    nblk = QKV_B // LANES
    nq = WIDTH // LANES

    def body(x_ref, w_ref, dc_ref, dx_ref, dw_ref, xp_ref, dyp_ref):
        j = pl.program_id(0)
        is_q, is_k = j < nq, (j >= nq) & (j < 2 * nq)
        xp_ref[pl.ds(0, CONV_PAD), :] = jnp.zeros((CONV_PAD, LANES), F32)
        xp_ref[pl.ds(CONV_PAD, s), :] = x_ref[...]
        dyp_ref[pl.ds(s, CONV_PAD), :] = jnp.zeros((CONV_PAD, LANES), F32)
        w = w_ref[...]
        for c in range(s // tr):
            y = _conv_rows(xp_ref, w, c * tr, tr)
            _, vjp = jax.vjp(lambda yy: _gdn_post(yy, is_q, is_k), y)
            dyp_ref[pl.ds(c * tr, tr), :] = vjp(dc_ref[pl.ds(c * tr, tr), :])[0]
        dws = [jnp.zeros((1, LANES), F32) for _ in range(CONV_K)]
        for c in range(s // tr):
            c0 = c * tr
            dy = dyp_ref[pl.ds(c0, tr), :]
            dx = w[0:1, :] * dyp_ref[pl.ds(c0 + CONV_K - 1, tr), :]
            for k in range(1, CONV_K):
                dx = dx + w[k:k + 1, :] * dyp_ref[pl.ds(c0 + CONV_K - 1 - k, tr), :]
            dx_ref[pl.ds(c0, tr), :] = dx.astype(BF16)
            for k in range(CONV_K):
                xs = xp_ref[pl.ds(c0 + CONV_PAD - (CONV_K - 1) + k, tr), :]
                dws[k] = dws[k] + jnp.sum(dy * xs, axis=0, keepdims=True)
        row = lax.broadcasted_iota(jnp.int32, (SUBLANES, LANES), 0)
        dwb = jnp.zeros((SUBLANES, LANES), F32)
        for k in range(CONV_K):
            dwb = dwb + jnp.where(row == k, dws[k], 0.0)
        dw_ref[...] = dwb

    return pl.pallas_call(
        body, name="gdn_conv_bwd", grid=(nblk,),
        in_specs=[pl.BlockSpec((s, LANES), lambda j: (0, col0 + j)), pl.BlockSpec((SUBLANES, LANES), lambda j: (0, j)),
                  pl.BlockSpec((s, LANES), lambda j: (0, j))],
        out_specs=(pl.BlockSpec((s, LANES), lambda j: (0, j)), pl.BlockSpec((SUBLANES, LANES), lambda j: (0, j))),
        out_shape=(jax.ShapeDtypeStruct((s, QKV_B), BF16), jax.ShapeDtypeStruct((SUBLANES, QKV_B), F32)),
        scratch_shapes=[pltpu.VMEM((s + CONV_PAD, LANES), F32), pltpu.VMEM((s + CONV_PAD, LANES), F32)],
        compiler_params=_params("parallel"),
    )(proj_r, conv8, dc)


def _gdn_chunk(q, k, v, bcol, acol, alog, dtb, gnw, state):
    n = q.shape[-2]
    shp = (1, n, n)
    row = lax.broadcasted_iota(jnp.int32, shp, 1)
    col = lax.broadcasted_iota(jnp.int32, shp, 2)
    beta = _sigmoid(bcol)
    g = -jnp.exp(alog) * _softplus(acol + dtb)
    g_row = jnp.sum(jnp.where(row == col, g, 0.0), axis=-2, keepdims=True)
    big_g = jnp.sum(jnp.where(row >= col, g_row, 0.0), axis=-1, keepdims=True)
    big_g_row = jnp.sum(jnp.where(row <= col, g, 0.0), axis=-2, keepdims=True)
    decay_incl = jnp.exp(jnp.where(row >= col, big_g - big_g_row, -jnp.inf))
    decay_strict = jnp.where(row > col, decay_incl, 0.0)
    k_beta = k * beta
    t_inv = _tri_inv(_mm_nt(k_beta, k) * decay_strict)
    e_g = jnp.exp(big_g)
    u = _mm(t_inv, v * beta)
    w = _mm(t_inv, k_beta * e_g)
    attn = _mm_nt(q, k) * decay_incl
    v_new = u - _mm(w, state)
    o = _mm(q * e_g, state) + _mm(attn, v_new)
    total = jnp.sum(g, axis=-2, keepdims=True)
    new_state = state * jnp.exp(total) + _mm_tn(k * jnp.exp(total - big_g), v_new)
    return _rmsnorm(o, gnw), new_state


def _split_heads(x):
    return jnp.stack([x[:, h * HEAD_DIM:(h + 1) * HEAD_DIM] for h in range(HEADS)], axis=0)


def _merge_heads(x):
    return jnp.concatenate([x[h] for h in range(HEADS)], axis=1)


def _logit_columns(ba):
    lane = lax.broadcasted_iota(jnp.int32, ba.shape, 1)

    def cols(off):
        return jnp.stack([jnp.sum(jnp.where(lane == off + h, ba, 0.0), axis=1, keepdims=True) for h in range(HEADS)], axis=0)

    return cols(0), cols(HEADS)


def _logit_block(dbc, dac, shape):
    lane = lax.broadcasted_iota(jnp.int32, shape, 1)
    out = jnp.zeros(shape, F32)
    for h in range(HEADS):
        out = out + jnp.where(lane == h, dbc[h], 0.0) + jnp.where(lane == HEADS + h, dac[h], 0.0)
    return out


def _gdn_scan_fwd(cqkv, proj_r, ba_col, alog, dtb, gnw):
    s = cqkv.shape[0]
    nc = s // CHUNK

    def body(q_ref, k_ref, v_ref, ba_ref, al_ref, dt_ref, gnw_ref, o_ref, st_ref, state_ref):
        @pl.when(pl.program_id(0) == 0)
        def _():
            state_ref[...] = jnp.zeros_like(state_ref)

        st = state_ref[...]
        st_ref[...] = st
        bcol, acol = _logit_columns(ba_ref[...])
        o, new_st = _gdn_chunk(_split_heads(q_ref[...]), _split_heads(k_ref[...]), _split_heads(v_ref[...]), bcol, acol,
                               al_ref[...], dt_ref[...], gnw_ref[...], st)
        o_ref[...] = _merge_heads(o)
        state_ref[...] = new_st

    part = lambda i: pl.BlockSpec((CHUNK, WIDTH), lambda n: (n, i))
    par = pl.BlockSpec((HEADS, 1, 1), lambda n: (0, 0, 0))
    return pl.pallas_call(
        body, name="gdn_scan_fwd", grid=(nc,),
        in_specs=[part(0), part(1), part(2), pl.BlockSpec((CHUNK, LANES), lambda n: (n, ba_col)), par, par,
                  pl.BlockSpec((1, 1, HEAD_DIM), lambda n: (0, 0, 0))],
        out_specs=(part(0), pl.BlockSpec((None, HEADS, HEAD_DIM, HEAD_DIM), lambda n: (n, 0, 0, 0))),
        out_shape=(jax.ShapeDtypeStruct((s, WIDTH), F32),
                   jax.ShapeDtypeStruct((nc, HEADS, HEAD_DIM, HEAD_DIM), F32)),
        scratch_shapes=[pltpu.VMEM((HEADS, HEAD_DIM, HEAD_DIM), F32)],
        compiler_params=_params("arbitrary"),
    )(cqkv, cqkv, cqkv, proj_r, alog, dtb, gnw)


def _gdn_scan_bwd(cqkv, proj_r, ba_col, alog, dtb, gnw, states, do):
    s = cqkv.shape[0]
    nc = s // CHUNK

    def body(q_ref, k_ref, v_ref, ba_ref, al_ref, dt_ref, gnw_ref, st_ref, do_ref,
             dqkv_ref, dba_ref, dal_ref, ddt_ref, dgnw_ref, dstate_ref):
        @pl.when(pl.program_id(0) == 0)
        def _():
            dstate_ref[...] = jnp.zeros_like(dstate_ref)
            dal_ref[...] = jnp.zeros_like(dal_ref)
            ddt_ref[...] = jnp.zeros_like(ddt_ref)
            dgnw_ref[...] = jnp.zeros_like(dgnw_ref)

        bcol, acol = _logit_columns(ba_ref[...])
        _, vjp = jax.vjp(_gdn_chunk, _split_heads(q_ref[...]), _split_heads(k_ref[...]), _split_heads(v_ref[...]),
                         bcol, acol, al_ref[...], dt_ref[...], gnw_ref[...], st_ref[...])
        dq, dk, dv, dbc, dac, dal, ddt, dgn, dst = vjp((_split_heads(do_ref[...]), dstate_ref[...]))
        dqkv_ref[:, pl.ds(0, WIDTH)] = _merge_heads(dq)
        dqkv_ref[:, pl.ds(WIDTH, WIDTH)] = _merge_heads(dk)
        dqkv_ref[:, pl.ds(2 * WIDTH, WIDTH)] = _merge_heads(dv)
        dba_ref[...] = _logit_block(dbc, dac, dba_ref.shape)
        dstate_ref[...] = dst
        dal_ref[...] += dal
        ddt_ref[...] += ddt
        dgnw_ref[...] += dgn

    rev = lambda n: nc - 1 - n
    part = lambda i: pl.BlockSpec((CHUNK, WIDTH), lambda n: (rev(n), i))
    par = pl.BlockSpec((HEADS, 1, 1), lambda n: (0, 0, 0))
    vec = pl.BlockSpec((1, 1, HEAD_DIM), lambda n: (0, 0, 0))
    par_shape = jax.ShapeDtypeStruct((HEADS, 1, 1), F32)
    return pl.pallas_call(
        body, name="gdn_scan_bwd", grid=(nc,),
        in_specs=[part(0), part(1), part(2), pl.BlockSpec((CHUNK, LANES), lambda n: (rev(n), ba_col)), par, par, vec,
                  pl.BlockSpec((None, HEADS, HEAD_DIM, HEAD_DIM), lambda n: (rev(n), 0, 0, 0)), part(0)],
        out_specs=(pl.BlockSpec((CHUNK, QKV_B), lambda n: (rev(n), 0)), pl.BlockSpec((CHUNK, LANES), lambda n: (rev(n), 0)),
                   par, par, vec),
        out_shape=(jax.ShapeDtypeStruct((s, QKV_B), F32), jax.ShapeDtypeStruct((s, LANES), F32), par_shape, par_shape,
                   jax.ShapeDtypeStruct((1, 1, HEAD_DIM), F32)),
        scratch_shapes=[pltpu.VMEM((HEADS, HEAD_DIM, HEAD_DIM), F32)],
        compiler_params=_params("arbitrary"),
    )(cqkv, cqkv, cqkv, proj_r, alog, dtb, gnw, states, do)


def _tail_loss(x, tgt, o0, o1, o2, l0, l1, l2, ga, gb, za, zb, ob, fnw, wua, wub, wo, tap_a, tap_b, tap_o):
    lm = jnp.maximum(jnp.maximum(l0, l1), l2)
    e0, e1, e2 = jnp.exp(l0 - lm), jnp.exp(l1 - lm), jnp.exp(l2 - lm)
    o_a = (e0 * o0 + e1 * o1 + e2 * o2) / (e0 + e1 + e2)
    y_a = _mm_tap(o_a * _silu(za), wua, tap_a)
    y_b = _mm_tap(ob * _silu(zb), wub, tap_b)
    merged = _sigmoid(ga) * y_a + _sigmoid(gb) * y_b
    y = _rmsnorm(x + _mm_tap(merged, wo, tap_o), fnw)
    err = y - tgt
    per_token = jnp.sum(err * err, axis=1, keepdims=True) * (0.5 / x.shape[1])
    return jnp.sum(per_token, axis=0, keepdims=True)


def _tail(x, tgt, o_all, lse_all, og12, lg12, proj_r, ob, wua, wub, wo, fnw):
    s, d = x.shape
    tm = _row_block(s, 128)
    col_za = 2 * d // WIDTH
    col_zb = (2 * d + WIDTH + QKV_B) // WIDTH

    def body(x_ref, t_ref, o0_ref, o1_ref, o2_ref, l0_ref, l1_ref, l2_ref, ga_ref, gb_ref, za_ref, zb_ref, ob_ref,
             wua_ref, wub_ref, wo_ref, fnw_ref,
             loss_ref, dx_ref, do0_ref, do1_ref, do2_ref, dl0_ref, dl1_ref, dl2_ref, dga_ref, dgb_ref, dza_ref,
             dzb_ref, dob_ref, dwua_ref, dwub_ref, dwo_ref, dfnw_ref):
        @pl.when(pl.program_id(0) == 0)
        def _():
            for r in (loss_ref, dwua_ref, dwub_ref, dwo_ref, dfnw_ref):
                r[...] = jnp.zeros_like(r)

        args = (x_ref[...], t_ref[...], o0_ref[...], o1_ref[...], o2_ref[...], l0_ref[...], l1_ref[...], l2_ref[...],
                ga_ref[...], gb_ref[...], za_ref[...], zb_ref[...], ob_ref[...], fnw_ref[...],
                wua_ref[...], wub_ref[...], wo_ref[...],
                jnp.zeros(wua_ref.shape, F32), jnp.zeros(wub_ref.shape, F32), jnp.zeros(wo_ref.shape, F32))
        loss, vjp = jax.vjp(_tail_loss, *args)
        (dx, _, do0, do1, do2, dl0, dl1, dl2, dga, dgb, dza, dzb, dob, dfnw, _, _, _, dwua, dwub, dwo) = vjp(
            jnp.ones((1, 1), F32))
        loss_ref[...] += jnp.broadcast_to(loss, loss_ref.shape)
        dx_ref[...] = dx
        do0_ref[...], do1_ref[...], do2_ref[...] = do0, do1, do2
        dl0_ref[...], dl1_ref[...], dl2_ref[...] = dl0, dl1, dl2
        dga_ref[...] = dga.astype(BF16)
        dgb_ref[...] = dgb.astype(BF16)
        dza_ref[...] = dza.astype(BF16)
        dzb_ref[...] = dzb.astype(BF16)
        dob_ref[...] = dob
        dwua_ref[...] += dwua
        dwub_ref[...] += dwub
        dwo_ref[...] += dwo
        dfnw_ref[...] += dfnw

    row = lambda w, c=0: pl.BlockSpec((tm, w), lambda i: (i, c))
    grp0 = pl.BlockSpec((None, tm, WIDTH), lambda i: (0, i, 0))
    full = lambda a, b: pl.BlockSpec((a, b), lambda i: (0, 0))
    f32 = lambda a, b: jax.ShapeDtypeStruct((a, b), F32)
    b16 = lambda a, b: jax.ShapeDtypeStruct((a, b), BF16)
    stacked = jax.ShapeDtypeStruct((GROUPS, s, WIDTH), F32)
    gspecs = [grp0, row(WIDTH), row(WIDTH)]
    in_specs = ([row(d), row(d)] + gspecs * 2 + [row(d, 0), row(d, 1), row(WIDTH, col_za), row(WIDTH, col_zb),
                row(WIDTH), full(WIDTH, d), full(WIDTH, d), full(d, d), full(1, d)])
    out_specs = ([full(SUBLANES, LANES), row(d)] + gspecs * 2 + [row(d), row(d), row(WIDTH), row(WIDTH), row(WIDTH),
                 full(WIDTH, d), full(WIDTH, d), full(d, d), full(1, d)])
    gshapes = [stacked, f32(s, WIDTH), f32(s, WIDTH)]
    out_shape = ([f32(SUBLANES, LANES), f32(s, d)] + gshapes * 2 + [b16(s, d), b16(s, d), b16(s, WIDTH),
                 b16(s, WIDTH), f32(s, WIDTH), f32(WIDTH, d), f32(WIDTH, d), f32(d, d), f32(1, d)])
    return pl.pallas_call(
        body, name="tail_fwd_bwd", grid=(s // tm,),
        in_specs=in_specs, out_specs=tuple(out_specs), out_shape=tuple(out_shape),
        compiler_params=_params("arbitrary"),
    )(x, tgt, o_all, og12[0], og12[1], lse_all, lg12[0], lg12[1], proj_r, proj_r, proj_r, proj_r, ob, wua, wub, wo, fnw)


def _to_dilated(a, dil):
    if dil == 1:
        return a
    s = a.shape[0]
    return a.reshape(s // dil, dil, -1).transpose(1, 0, 2).reshape(a.shape)


def _from_dilated(a, dil):
    if dil == 1:
        return a
    s = a.shape[0]
    return a.reshape(dil, s // dil, -1).transpose(1, 0, 2).reshape(a.shape)


def _head_major(a):
    return a.reshape(a.shape[0], HEADS, HEAD_DIM).transpose(1, 0, 2)


def _from_head_major(a):
    return a.transpose(1, 0, 2).reshape(a.shape[1], WIDTH)


def _rope_tables(s):
    inv_freq = ROPE_THETA ** (-jnp.arange(0, HEAD_DIM, 2, dtype=F32) / HEAD_DIM)
    ang = jnp.arange(s, dtype=F32)[:, None] * inv_freq[None, :]
    cos_n = jnp.tile(jnp.cos(ang), (1, 2 * LANES // HEAD_DIM))
    sin_h = jnp.sin(ang)
    sin_n = jnp.tile(jnp.concatenate([-sin_h, sin_h], axis=1), (1, LANES // HEAD_DIM))
    return (jnp.stack([_to_dilated(cos_n, dil) for dil in DILATIONS]),
            jnp.stack([_to_dilated(sin_n, dil) for dil in DILATIONS]))


def _pack_rows(parts, dtype, row_multiple):
    flat = jnp.concatenate([p.reshape(-1).astype(dtype) for p in parts])
    tile = row_multiple * LANES
    pad = (-flat.shape[0]) % tile
    return jnp.pad(flat, (0, pad)).reshape(-1, LANES)


def _unpack_rows(packed, shapes):
    flat = packed.reshape(-1)
    out, start = [], 0
    for shp in shapes:
        size = 1
        for n in shp:
            size *= n
        out.append(flat[start:start + size].reshape(shp))
        start += size
    return out


def kernel(x, norm_w, w_in, conv_w, a_log, dt_bias, gdn_norm_w, w_up_a, w_up_b, w_out, final_norm_w, loss_target, m_norm_w, m_w_in, m_conv_w, m_a_log, m_dt_bias, m_gdn_norm_w, m_w_up_a, m_w_up_b, m_w_out, m_final_norm_w, v_norm_w, v_w_in, v_conv_w, v_a_log, v_dt_bias, v_gdn_norm_w, v_w_up_a, v_w_up_b, v_w_out, v_final_norm_w):
    x2, tgt = x[0], loss_target[0]
    s, d = x2.shape
    me = 4 * lax.axis_index("x") + 2 * lax.axis_index("y") + lax.axis_index("c")
    win8 = w_in.shape[2]
    conv8w = conv_w.shape[2]

    conv_shard = jnp.pad(conv_w[0], ((0, SUBLANES - CONV_K), (0, 0)))
    w_in_g, wua_g, wub_g, wo_g, conv_g = _all_gather(
        [w_in[0].astype(BF16), w_up_a[0].astype(BF16), w_up_b[0].astype(BF16), w_out[0].astype(BF16), conv_shard])
    w_in_f = jnp.concatenate([w_in_g[i] for i in range(N_DEV)], axis=1)
    wua = jnp.concatenate([wua_g[i] for i in range(N_DEV)], axis=1)
    wub = jnp.concatenate([wub_g[i] for i in range(N_DEV)], axis=1)
    wo = wo_g.reshape(d, d)
    conv8 = jnp.concatenate([conv_g[i] for i in range(N_DEV)], axis=1)

    w_qkv = w_in_f[:, :QKV_A].reshape(d, GROUPS, QKV_B).transpose(1, 0, 2)
    w_rest = jnp.concatenate([
        w_in_f[:, OFF_GATE:OFF_GATE + 2 * d], w_in_f[:, OFF_ZA:OFF_ZA + WIDTH], w_in_f[:, OFF_QKVB:OFF_QKVB + QKV_B],
        w_in_f[:, OFF_ZB:OFF_ZB + WIDTH], w_in_f[:, OFF_BA:OFF_BA + 2 * HEADS],
        jnp.zeros((d, BA_PAD - 2 * HEADS), BF16)], axis=1)
    col_qkvb = (2 * d + WIDTH) // LANES
    col_ba = (2 * d + 2 * WIDTH + QKV_B) // LANES

    h = _rms_fwd(x2, norm_w)
    h_all = jnp.stack([_to_dilated(h, dil) for dil in DILATIONS])
    qkv_all = _matmul(h_all, w_qkv, F32, "in_proj_attention")
    proj_r = _matmul(h[None], w_rest[None], F32, "in_proj_rest")[0]
    cos, sin = _rope_tables(s)
    o_all, lse_all = _attn_fwd(qkv_all, cos, sin)
    og12 = [_from_dilated(o_all[g], DILATIONS[g]) for g in (1, 2)]
    lg12 = [_from_dilated(lse_all[g], DILATIONS[g]) for g in (1, 2)]

    cqkv = _gdn_pre_fwd(proj_r, conv8, col_qkvb)
    alog3, dtb3, gnw3 = a_log.reshape(HEADS, 1, 1), dt_bias.reshape(HEADS, 1, 1), gdn_norm_w.reshape(1, 1, HEAD_DIM)
    ob, states = _gdn_scan_fwd(cqkv, proj_r, col_ba, alog3, dtb3, gnw3)

    (loss_blk, dx_res, do_all, do1, do2, dl_all, dl1, dl2, dga, dgb, dza, dzb, dob, dwua, dwub, dwo, dfnw) = _tail(
        x2, tgt, o_all, lse_all, og12, lg12, proj_r, ob, wua, wub, wo, final_norm_w.reshape(1, d))

    for g, (t_o, t_l) in ((1, (do1, dl1)), (2, (do2, dl2))):
        do_all = do_all.at[g].set(_to_dilated(t_o, DILATIONS[g]))
        dl_all = dl_all.at[g].set(_to_dilated(t_l, DILATIONS[g]))
    dqkv_all = _attn_bwd(qkv_all, cos, sin, o_all, lse_all, do_all, dl_all)

    dcqkv, dba, dalog3, ddtb3, dgnw3 = _gdn_scan_bwd(cqkv, proj_r, col_ba, alog3, dtb3, gnw3, states, dob)
    dqkv_b, dconv8 = _gdn_pre_bwd(proj_r, conv8, dcqkv, col_qkvb)
    dproj_r = jnp.concatenate([dga, dgb, dza, dqkv_b, dzb,
                               jnp.pad(dba.astype(BF16), ((0, 0), (0, BA_PAD - LANES)))], axis=1)

    dw_qkv = _matmul(h_all, dqkv_all, F32, "in_proj_attention_dw", mode="tn")
    dw_rest = _matmul(h[None], dproj_r[None], F32, "in_proj_rest_dw", mode="tn")[0]
    dh_a = _matmul(dqkv_all, w_qkv, F32, "in_proj_attention_dh", mode="nt")
    dh_r = _matmul(dproj_r[None], w_rest[None], F32, "in_proj_rest_dh", mode="nt")[0]
    dh_parts = [dh_r] + [_from_dilated(dh_a[g], DILATIONS[g]) for g in (1, 2)]
    grad_x, dnorm_w = _rms_bwd(x2, norm_w, dh_a, dh_parts, dx_res)

    o2 = 2 * d
    dw_in = jnp.concatenate([
        dw_qkv.transpose(1, 0, 2).reshape(d, QKV_A),
        dw_rest[:, o2:o2 + WIDTH], dw_rest[:, o2 + WIDTH:o2 + WIDTH + QKV_B],
        dw_rest[:, o2 + WIDTH + QKV_B:o2 + 2 * WIDTH + QKV_B],
        dw_rest[:, o2 + 2 * WIDTH + QKV_B:o2 + 2 * WIDTH + QKV_B + 2 * HEADS],
        dw_rest[:, :o2]], axis=1)

    def col_slabs(a, width):
        return jnp.stack([a[:, j * width:(j + 1) * width] for j in range(N_DEV)])

    slabs = [col_slabs(dw_in, win8), col_slabs(dwua, d // N_DEV), col_slabs(dwub, d // N_DEV),
             dwo.reshape(N_DEV, d // N_DEV, d)]
    from_sibling = _sibling_exchange(slabs)
    core = lax.axis_index("c").astype(jnp.int32).reshape(1)
    partials = [_pair_sum(a, b, core, "grads_pair_sum_%d" % i) for i, (a, b) in enumerate(zip(slabs, from_sibling))]
    contrib = _chip_exchange(partials)

    small_parts = [dnorm_w, dfnw, dconv8[:CONV_K], dalog3[:, 0, 0], ddtb3[:, 0, 0], dgnw3[0], loss_blk[0, 0:1]]
    small_rows = [-(-p.size // LANES) for p in small_parts]
    small = jnp.concatenate([jnp.pad(p.reshape(-1), (0, r * LANES - p.size)).reshape(r, LANES)
                             for p, r in zip(small_parts, small_rows)])
    small = jnp.pad(small, ((0, (-small.shape[0]) % SUBLANES), (0, 0)))
    small_sum = _small_all_reduce(small)
    pieces, r0 = [], 0
    for p, r in zip(small_parts, small_rows):
        pieces.append(small_sum[r0:r0 + r].reshape(-1)[:p.size].reshape(p.shape))
        r0 += r
    g_norm_w, g_fnw, g_conv_full, g_alog, g_dtb, g_gnw, loss_sum = pieces
    g_conv = lax.dynamic_slice(g_conv_full, (0, me * conv8w), (CONV_K, conv8w))

    big = [_adamw(c, w[0], m[0], v[0], name) for c, w, m, v, name in (
        (contrib[0], w_in, m_w_in, v_w_in, "adamw_w_in"), (contrib[1], w_up_a, m_w_up_a, v_w_up_a, "adamw_w_up_a"),
        (contrib[2], w_up_b, m_w_up_b, v_w_up_b, "adamw_w_up_b"), (contrib[3], w_out, m_w_out, v_w_out, "adamw_w_out"))]
    g_big, d_big, nm_big, nv_big = ([t[i] for t in big] for i in range(4))

    small_ws = [norm_w, final_norm_w, conv_w, a_log, dt_bias, gdn_norm_w]
    small_ms = [m_norm_w, m_final_norm_w, m_conv_w, m_a_log, m_dt_bias, m_gdn_norm_w]
    small_vs = [v_norm_w, v_final_norm_w, v_conv_w, v_a_log, v_dt_bias, v_gdn_norm_w]
    small_gs = [g_norm_w, g_fnw, g_conv, g_alog, g_dtb, g_gnw]
    small_shapes = [t.shape for t in small_ws]
    sm = _adamw(_pack_rows(small_gs, F32, SUBLANES)[None], _pack_rows(small_ws, F32, SUBLANES),
                _pack_rows(small_ms, F32, SUBLANES), _pack_rows(small_vs, F32, SUBLANES), "adamw_small")
    g_sm, d_sm, nm_sm, nv_sm = (_unpack_rows(t, small_shapes) for t in sm)

    def ordered(bigs, smalls):
        nw, fnw_, cw, al, dtb, gn = smalls
        wi, ua, ub, wo_ = (t[None] for t in bigs)
        return [nw, wi, cw, al, dtb, gn, ua, ub, wo_, fnw_]

    return (loss_sum.reshape(()), grad_x[None], *ordered(g_big, g_sm), *ordered(d_big, d_sm),
            *ordered(nm_big, nm_sm), *ordered(nv_big, nv_sm))
```

```python
import functools

import jax
import jax.numpy as jnp
from jax import lax
from jax.experimental import pallas as pl
from jax.experimental.pallas import tpu as pltpu

F32 = jnp.float32
BF16 = jnp.bfloat16
MESH = pl.DeviceIdType.MESH
N_DEV = 8
LANES = 128
SUBLANES = 8

GROUPS = 3
HEADS = 8
HEAD_DIM = 64
WIDTH = HEADS * HEAD_DIM
ATT_BLOCK = 128
DILATIONS = (1, 4, 16)
N_BACK = 128
CONV_K = 4
CHUNK = 64
QKV_B = 3 * WIDTH
QKV_A = GROUPS * 3 * WIDTH
BA_PAD = 512
NORM_EPS = 1e-6
ROPE_THETA = 10000.0
ADAM_LR, ADAM_B1, ADAM_B2, ADAM_EPS, ADAM_WD, ADAM_STEP = 0.001, 0.9, 0.999, 1e-08, 0.01, 10

VMEM_LIMIT = 56 * 1024 * 1024

OFF_ZA = QKV_A
OFF_QKVB = OFF_ZA + WIDTH
OFF_ZB = OFF_QKVB + QKV_B
OFF_BA = OFF_ZB + WIDTH
OFF_GATE = OFF_BA + 2 * HEADS


def _params(*sem):
    return pltpu.CompilerParams(dimension_semantics=sem, vmem_limit_bytes=VMEM_LIMIT)


def _dg(a, b, ca, cb):
    nb = a.ndim - 2
    batch = tuple(range(nb))
    return lax.dot_general(a, b, (((nb + ca,), (nb + cb,)), (batch, batch)), preferred_element_type=F32)


@jax.custom_vjp
def _mm(a, b):
    return _dg(a.astype(BF16), b.astype(BF16), 1, 0)


def _mm_fwd(a, b):
    return _mm(a, b), (a.astype(BF16), b.astype(BF16))


def _mm_bwd(res, ct):
    a16, b16 = res
    c16 = ct.astype(BF16)
    return _dg(c16, b16, 1, 1), _dg(a16, c16, 0, 0)


_mm.defvjp(_mm_fwd, _mm_bwd)


@jax.custom_vjp
def _mm_nt(a, b):
    return _dg(a.astype(BF16), b.astype(BF16), 1, 1)


def _mm_nt_fwd(a, b):
    return _mm_nt(a, b), (a.astype(BF16), b.astype(BF16))


def _mm_nt_bwd(res, ct):
    a16, b16 = res
    c16 = ct.astype(BF16)
    return _dg(c16, b16, 1, 0), _dg(c16, a16, 0, 0)


_mm_nt.defvjp(_mm_nt_fwd, _mm_nt_bwd)


@jax.custom_vjp
def _mm_tn(a, b):
    return _dg(a.astype(BF16), b.astype(BF16), 0, 0)


def _mm_tn_fwd(a, b):
    return _mm_tn(a, b), (a.astype(BF16), b.astype(BF16))


def _mm_tn_bwd(res, ct):
    a16, b16 = res
    c16 = ct.astype(BF16)
    return _dg(b16, c16, 1, 1), _dg(a16, c16, 1, 0)


_mm_tn.defvjp(_mm_tn_fwd, _mm_tn_bwd)


@jax.custom_vjp
def _mm_tap(a, w16, tap):
    return _dg(a.astype(BF16), w16, 1, 0)


def _mm_tap_fwd(a, w16, tap):
    return _mm_tap(a, w16, tap), (a.astype(BF16), w16)


def _mm_tap_bwd(res, ct):
    a16, w16 = res
    c16 = ct.astype(BF16)
    return _dg(c16, w16, 1, 1), jnp.zeros_like(w16), _dg(a16, c16, 0, 0)


_mm_tap.defvjp(_mm_tap_fwd, _mm_tap_bwd)


def _split16(a):
    hi = a.astype(BF16)
    lo = (a - hi.astype(F32)).astype(BF16)
    return hi, lo


def _dot3(a, b, ca, cb):
    ah, al = _split16(a)
    bh, bl = _split16(b)
    return _dg(ah, bh, ca, cb) + (_dg(ah, bl, ca, cb) + _dg(al, bh, ca, cb))


def _tri_inv_impl(a):
    n = a.shape[-1]
    shp = (1,) * (a.ndim - 2) + (n, n)
    eye = (lax.broadcasted_iota(jnp.int32, shp, a.ndim - 2) == lax.broadcasted_iota(jnp.int32, shp, a.ndim - 1)).astype(F32)
    x = eye - a
    p = a
    for it in range(5):
        dot = _dot3 if it < 2 else (lambda u, v, cu, cv: _dg(u.astype(BF16), v.astype(BF16), cu, cv))
        p = dot(p, p, 1, 0)
        x = x + dot(x, p, 1, 0)
    return x


@jax.custom_vjp
def _tri_inv(a):
    return _tri_inv_impl(a)


def _tri_inv_fwd(a):
    t = _tri_inv_impl(a)
    return t, t


def _tri_inv_bwd(t, ct):
    t16 = t.astype(BF16)
    return (-_dg(_dg(t16, ct.astype(BF16), 0, 0).astype(BF16), t16, 1, 1),)


_tri_inv.defvjp(_tri_inv_fwd, _tri_inv_bwd)


@jax.custom_vjp
def _tri_inv_saved(a, t):
    return t


def _tri_inv_saved_fwd(a, t):
    return t, t


def _tri_inv_saved_bwd(t, ct):
    return _tri_inv_bwd(t, ct) + (jnp.zeros_like(t),)


_tri_inv_saved.defvjp(_tri_inv_saved_fwd, _tri_inv_saved_bwd)


def _sigmoid(x):
    return 1.0 / (1.0 + jnp.exp(-x))


def _silu(x):
    return x * _sigmoid(x)


def _softplus(x):
    return jnp.maximum(x, 0.0) + jnp.log(1.0 + jnp.exp(-jnp.abs(x)))


def _rmsnorm(x, w):
    return x * lax.rsqrt(jnp.mean(x * x, axis=-1, keepdims=True) + NORM_EPS) * w


def _row_block(rows, cap):
    best = None
    for cand in range(SUBLANES, min(rows, cap) + 1, SUBLANES):
        if rows % cand == 0:
            best = cand
    assert best is not None, rows
    return best


def _mesh_peers():
    x, y, c = lax.axis_index("x"), lax.axis_index("y"), lax.axis_index("c")
    me = 4 * x + 2 * y + c
    peers = []
    for k in range(1, N_DEV):
        px = 1 - x if (k >> 2) & 1 else x
        py = 1 - y if (k >> 1) & 1 else y
        pc = 1 - c if k & 1 else c
        peers.append(((px, py, pc), 4 * px + 2 * py + pc))
    return me, peers


N_CHIPS = 4
OTHER_CHIPS = 3


def _chip_peers():
    x, y, c = lax.axis_index("x"), lax.axis_index("y"), lax.axis_index("c")
    return x, y, c, [(1 - x, y), (x, 1 - y), (1 - x, 1 - y)]


def _all_gather(shards):
    n_arr = len(shards)
    per = 1 + 2 * OTHER_CHIPS

    def body(*refs):
        in_refs, out_refs = refs[:n_arr], refs[n_arr:2 * n_arr]
        send_sems, recv_sems, loc_sems = refs[2 * n_arr:]
        x, y, c, chips = _chip_peers()
        me, sibling = (x, y, c), (x, y, 1 - c)

        def slot(px, py, pc):
            return 4 * px + 2 * py + pc

        def copy(i, k, block, to, src=None):
            dst = out_refs[i].at[slot(*block)]
            return pltpu.make_async_remote_copy(src_ref=dst if src is None else src, dst_ref=dst,
                                                send_sem=send_sems.at[i * per + k], recv_sem=recv_sems.at[i * per + k],
                                                device_id=to, device_id_type=MESH)

        own = [pltpu.make_async_copy(in_refs[i], out_refs[i].at[slot(*me)], loc_sems.at[i]) for i in range(n_arr)]
        for cp in own:
            cp.start()
        first = []
        for i in range(n_arr):
            first += [copy(i, 1 + j, me, (*chip, c), src=in_refs[i]) for j, chip in enumerate(chips)]
            first.append(copy(i, 0, me, sibling, src=in_refs[i]))
        for cp in first:
            cp.start()
        passed = []
        for j, chip in enumerate(chips):
            for i in range(n_arr):
                copy(i, 1 + j, (*chip, c), me).wait_recv()
                fwd = copy(i, 1 + OTHER_CHIPS + j, (*chip, c), sibling)
                fwd.start()
                passed.append(fwd)
        for i in range(n_arr):
            copy(i, 0, sibling, me).wait_recv()
            for j, chip in enumerate(chips):
                copy(i, 1 + OTHER_CHIPS + j, (*chip, 1 - c), me).wait_recv()
        for cp in first + passed:
            cp.wait_send()
        for cp in own:
            cp.wait()

    any_spec = pl.BlockSpec(memory_space=pl.ANY)
    return pl.pallas_call(
        body, name="weights_all_gather",
        out_shape=tuple(jax.ShapeDtypeStruct((N_DEV,) + a.shape, a.dtype) for a in shards),
        in_specs=[any_spec] * n_arr, out_specs=tuple([any_spec] * n_arr),
        scratch_shapes=[pltpu.SemaphoreType.DMA((n_arr * per,)), pltpu.SemaphoreType.DMA((n_arr * per,)),
                        pltpu.SemaphoreType.DMA((n_arr,))],
    )(*shards)


def _sibling_exchange(slabs):
    n_arr = len(slabs)

    def body(*refs):
        in_refs, out_refs = refs[:n_arr], refs[n_arr:2 * n_arr]
        send_sems, recv_sems = refs[2 * n_arr:]
        x, y, c, _ = _chip_peers()
        sends = []
        for i in range(n_arr):
            for q in range(N_CHIPS):
                cp = pltpu.make_async_remote_copy(src_ref=in_refs[i].at[2 * q + (1 - c)], dst_ref=out_refs[i].at[q],
                                                  send_sem=send_sems.at[i * N_CHIPS + q],
                                                  recv_sem=recv_sems.at[i * N_CHIPS + q],
                                                  device_id=(x, y, 1 - c), device_id_type=MESH)
                cp.start()
                sends.append(cp)
        for cp in sends:
            cp.wait_recv()
        for cp in sends:
            cp.wait_send()

    any_spec = pl.BlockSpec(memory_space=pl.ANY)
    return pl.pallas_call(
        body, name="grads_sibling_exchange",
        out_shape=tuple(jax.ShapeDtypeStruct((N_CHIPS,) + a.shape[1:], a.dtype) for a in slabs),
        in_specs=[any_spec] * n_arr, out_specs=tuple([any_spec] * n_arr),
        scratch_shapes=[pltpu.SemaphoreType.DMA((n_arr * N_CHIPS,)), pltpu.SemaphoreType.DMA((n_arr * N_CHIPS,))],
    )(*slabs)


def _pair_sum(slabs, from_sibling, core, name):
    _, rows, cols = slabs.shape
    tr = _row_block(rows, max(SUBLANES, (256 * 1024) // cols // SUBLANES * SUBLANES))

    def body(core_ref, a_ref, b_ref, o_ref):
        o_ref[...] = (a_ref[...] + b_ref[...]).astype(BF16)

    grid_spec = pltpu.PrefetchScalarGridSpec(
        num_scalar_prefetch=1, grid=(N_CHIPS, rows // tr),
        in_specs=[pl.BlockSpec((None, tr, cols), lambda q, r, core_ref: (2 * q + core_ref[0], r, 0)),
                  pl.BlockSpec((None, tr, cols), lambda q, r, core_ref: (q, r, 0))],
        out_specs=pl.BlockSpec((None, tr, cols), lambda q, r, core_ref: (q, r, 0)))
    return pl.pallas_call(
        body, name=name, grid_spec=grid_spec,
        out_shape=jax.ShapeDtypeStruct((N_CHIPS, rows, cols), BF16),
        compiler_params=_params("parallel", "parallel"),
    )(core, slabs, from_sibling)


def _chip_exchange(partials):
    n_arr = len(partials)

    def body(*refs):
        in_refs, out_refs = refs[:n_arr], refs[n_arr:2 * n_arr]
        send_sems, recv_sems, loc_sems = refs[2 * n_arr:]
        x, y, c, chips = _chip_peers()
        mine = 2 * x + y
        own = [pltpu.make_async_copy(in_refs[i].at[mine], out_refs[i].at[mine], loc_sems.at[i]) for i in range(n_arr)]
        for cp in own:
            cp.start()

        def copy(i, j, chip, src_slot, dst_slot):
            return pltpu.make_async_remote_copy(src_ref=in_refs[i].at[src_slot], dst_ref=out_refs[i].at[dst_slot],
                                                send_sem=send_sems.at[i * OTHER_CHIPS + j],
                                                recv_sem=recv_sems.at[i * OTHER_CHIPS + j],
                                                device_id=(*chip, c), device_id_type=MESH)

        sends = [copy(i, j, chip, 2 * chip[0] + chip[1], mine) for j, chip in enumerate(chips) for i in range(n_arr)]
        for cp in sends:
            cp.start()
        for j, chip in enumerate(chips):
            for i in range(n_arr):
                copy(i, j, chip, mine, 2 * chip[0] + chip[1]).wait_recv()
        for cp in sends:
            cp.wait_send()
        for cp in own:
            cp.wait()

    any_spec = pl.BlockSpec(memory_space=pl.ANY)
    return pl.pallas_call(
        body, name="grads_chip_exchange",
        out_shape=tuple(jax.ShapeDtypeStruct(a.shape, a.dtype) for a in partials),
        in_specs=[any_spec] * n_arr, out_specs=tuple([any_spec] * n_arr),
        scratch_shapes=[pltpu.SemaphoreType.DMA((n_arr * OTHER_CHIPS,)), pltpu.SemaphoreType.DMA((n_arr * OTHER_CHIPS,)),
                        pltpu.SemaphoreType.DMA((n_arr,))],
    )(*partials)


def _small_all_reduce(part):
    rows = part.shape[0]

    def body(p_ref, o_ref, buf_ref, send_sems, recv_sems):
        me, peers = _mesh_peers()
        buf_ref[me] = p_ref[...]
        sends = []
        for k, (dev, pid) in enumerate(peers):
            cp = pltpu.make_async_remote_copy(src_ref=p_ref, dst_ref=buf_ref.at[me], send_sem=send_sems.at[k],
                                              recv_sem=recv_sems.at[k], device_id=dev, device_id_type=MESH)
            cp.start()
            sends.append(cp)
        for k, (dev, pid) in enumerate(peers):
            pltpu.make_async_remote_copy(src_ref=p_ref, dst_ref=buf_ref.at[pid], send_sem=send_sems.at[k],
                                         recv_sem=recv_sems.at[k], device_id=dev, device_id_type=MESH).wait_recv()
        for cp in sends:
            cp.wait_send()
        acc = buf_ref[0]
        for i in range(1, N_DEV):
            acc = acc + buf_ref[i]
        o_ref[...] = acc

    vmem = pl.BlockSpec(memory_space=pltpu.VMEM)
    return pl.pallas_call(
        body, name="small_all_reduce",
        out_shape=jax.ShapeDtypeStruct(part.shape, F32),
        in_specs=[vmem], out_specs=vmem,
        scratch_shapes=[pltpu.VMEM((N_DEV, rows, LANES), F32), pltpu.SemaphoreType.DMA((N_DEV - 1,)),
                        pltpu.SemaphoreType.DMA((N_DEV - 1,))],
    )(part)


def _adamw_vals(w, g, m, v):
    m = ADAM_B1 * m + (1.0 - ADAM_B1) * g
    v = ADAM_B2 * v + (1.0 - ADAM_B2) * (g * g)
    m_hat = m / (1.0 - ADAM_B1 ** ADAM_STEP)
    v_hat = v / (1.0 - ADAM_B2 ** ADAM_STEP)
    delta = -ADAM_LR * (m_hat / (jnp.sqrt(v_hat) + ADAM_EPS) + ADAM_WD * w)
    return delta, m, v


def _adamw(contrib, w, m, v, name):
    n, rows, cols = contrib.shape
    tr = _row_block(rows, max(SUBLANES, (128 * 1024) // cols // SUBLANES * SUBLANES))

    def body(c_ref, w_ref, m_ref, v_ref, g_ref, d_ref, nm_ref, nv_ref):
        g = c_ref[0].astype(F32)
        for i in range(1, n):
            g = g + c_ref[i].astype(F32)
        delta, nm, nv = _adamw_vals(w_ref[...], g, m_ref[...], v_ref[...])
        g_ref[...] = g
        d_ref[...] = delta
        nm_ref[...] = nm
        nv_ref[...] = nv

    row = pl.BlockSpec((tr, cols), lambda i: (i, 0))
    shp = jax.ShapeDtypeStruct((rows, cols), F32)
    return pl.pallas_call(
        body, name=name, grid=(rows // tr,),
        in_specs=[pl.BlockSpec((n, tr, cols), lambda i: (0, i, 0)), row, row, row],
        out_specs=(row, row, row, row), out_shape=(shp, shp, shp, shp),
        compiler_params=_params("parallel"),
    )(contrib, w, m, v)


def _lane_block(n, cap):
    if n <= cap:
        return n
    best = None
    for cand in range(LANES, cap + 1, LANES):
        if n % cand == 0:
            best = cand
    assert best is not None, n
    return best


def _matmul(a, b, out_dtype, name, mode="nn", tm=1024, tn=1024, tk=1024):
    g = a.shape[0]
    m, k = (a.shape[2], a.shape[1]) if mode == "tn" else (a.shape[1], a.shape[2])
    n = b.shape[1] if mode == "nt" else b.shape[2]
    tm, tn, tk = _lane_block(m, tm), _lane_block(n, tn), _lane_block(k, tk)
    nk = k // tk
    a_spec = (pl.BlockSpec((None, tk, tm), lambda gi, i, j, kk: (gi, kk, i)) if mode == "tn" else
              pl.BlockSpec((None, tm, tk), lambda gi, i, j, kk: (gi, i, kk)))
    b_spec = (pl.BlockSpec((None, tn, tk), lambda gi, i, j, kk: (gi, j, kk)) if mode == "nt" else
              pl.BlockSpec((None, tk, tn), lambda gi, i, j, kk: (gi, kk, j)))
    ca, cb = (0 if mode == "tn" else 1), (1 if mode == "nt" else 0)

    def body(a_ref, b_ref, o_ref, *acc):
        part = _dg(a_ref[...], b_ref[...], ca, cb)
        if nk == 1:
            o_ref[...] = part.astype(o_ref.dtype)
            return
        acc_ref, = acc
        kk = pl.program_id(3)

        @pl.when(kk == 0)
        def _():
            acc_ref[...] = part

        @pl.when((kk > 0) & (kk < nk - 1))
        def _():
            acc_ref[...] += part

        @pl.when(kk == nk - 1)
        def _():
            o_ref[...] = (acc_ref[...] + part).astype(o_ref.dtype)

    return pl.pallas_call(
        body, name=name, grid=(g, m // tm, n // tn, nk),
        in_specs=[a_spec, b_spec],
        out_specs=pl.BlockSpec((None, tm, tn), lambda gi, i, j, kk: (gi, i, j)),
        out_shape=jax.ShapeDtypeStruct((g, m, n), out_dtype),
        scratch_shapes=[] if nk == 1 else [pltpu.VMEM((tm, tn), F32)],
        compiler_params=_params("parallel", "parallel", "parallel", "arbitrary"),
    )(a, b)


def _rms_fwd(x, w):
    s, d = x.shape
    tm = _row_block(s, 512)

    def body(x_ref, w_ref, h_ref):
        h_ref[...] = _rmsnorm(x_ref[...], w_ref[...]).astype(BF16)

    return pl.pallas_call(
        body, name="input_rmsnorm", grid=(s // tm,),
        in_specs=[pl.BlockSpec((tm, d), lambda i: (i, 0)), pl.BlockSpec((1, d), lambda i: (0, 0))],
        out_specs=pl.BlockSpec((tm, d), lambda i: (i, 0)),
        out_shape=jax.ShapeDtypeStruct((s, d), BF16),
        compiler_params=_params("parallel"),
    )(x, w)


def _rms_bwd(x, w, dh_stacked, dh_parts, dx_res):
    s, d = x.shape
    tm = _row_block(s, 256)
    n_parts = 1 + len(dh_parts)

    def body(x_ref, w_ref, *rest):
        part_refs = rest[:n_parts]
        res_ref, gx_ref, gw_ref = rest[n_parts:]
        dh = part_refs[0][...]
        for r in part_refs[1:]:
            dh = dh + r[...]
        _, vjp = jax.vjp(_rmsnorm, x_ref[...], w_ref[...])
        dx, dw = vjp(dh)
        gx_ref[...] = dx + res_ref[...]

        @pl.when(pl.program_id(0) == 0)
        def _():
            gw_ref[...] = jnp.zeros_like(gw_ref)

        gw_ref[...] += dw

    row = pl.BlockSpec((tm, d), lambda i: (i, 0))
    vec = pl.BlockSpec((1, d), lambda i: (0, 0))
    return pl.pallas_call(
        body, name="input_rmsnorm_bwd", grid=(s // tm,),
        in_specs=[row, vec, pl.BlockSpec((None, tm, d), lambda i: (0, i, 0))] + [row] * (n_parts - 1) + [row],
        out_specs=(row, vec),
        out_shape=(jax.ShapeDtypeStruct((s, d), F32), jax.ShapeDtypeStruct((1, d), F32)),
        compiler_params=_params("arbitrary"),
    )(x, w, dh_stacked, *dh_parts, dx_res)


def _lane_masks(rows):
    lane = lax.broadcasted_iota(jnp.int32, (rows, LANES), 1)
    return lane < HEAD_DIM, (lane & (HEAD_DIM - 1)) < HEAD_DIM // 2


def _swap_halves(t, lo_half):
    return jnp.where(lo_half, pltpu.roll(t, LANES - HEAD_DIM // 2, 1), pltpu.roll(t, HEAD_DIM // 2, 1))


def _rope(t, cos, sin_signed, lo_half):
    return t * cos + _swap_halves(t, lo_half) * sin_signed


def _rope_bwd(d, cos, sin_signed, lo_half):
    return d * cos - _swap_halves(d, lo_half) * sin_signed


def _window_mask(first):
    qi = lax.broadcasted_iota(jnp.int32, (2 * ATT_BLOCK, 2 * ATT_BLOCK), 0) & (ATT_BLOCK - 1)
    kj = lax.broadcasted_iota(jnp.int32, (2 * ATT_BLOCK, 2 * ATT_BLOCK), 1)
    dist = qi + ATT_BLOCK - kj
    return (dist >= 0) & (dist <= N_BACK) & ((kj >= ATT_BLOCK) | jnp.logical_not(first))


def _stack_heads(t, head0):
    zero = jnp.zeros_like(t)
    return jnp.concatenate([jnp.where(head0, t, zero), jnp.where(head0, zero, t)], axis=0)


def _unstack_heads(t2, head0):
    return jnp.where(head0, t2[:ATT_BLOCK], t2[ATT_BLOCK:])


def _blocks_per_subsequence(g, nb):
    return lax.shift_right_logical(jnp.int32(nb), 2 * g)


def _attn_fwd(qkv, cos, sin):
    _, s, _ = qkv.shape
    nb = s // ATT_BLOCK

    def body(qkv_ref, cos_ref, sin_ref, o_ref, lse_ref, kp_ref, vp_ref):
        g, t = pl.program_id(0), pl.program_id(1)
        first = (t & (_blocks_per_subsequence(g, nb) - 1)) == 0

        @pl.when(first)
        def _():
            kp_ref[...] = jnp.zeros_like(kp_ref)
            vp_ref[...] = jnp.zeros_like(vp_ref)

        cos_b, sin_b = cos_ref[...], sin_ref[...]
        head0, lo_half = _lane_masks(ATT_BLOCK)
        valid = _window_mask(first)
        for sl in range(WIDTH // LANES):
            cq = pl.ds(sl * LANES, LANES)
            ck = pl.ds(WIDTH + sl * LANES, LANES)
            cv = pl.ds(2 * WIDTH + sl * LANES, LANES)
            qr = (_rope(qkv_ref[:, cq], cos_b, sin_b, lo_half) * (HEAD_DIM ** -0.5)).astype(BF16)
            kr = _rope(qkv_ref[:, ck], cos_b, sin_b, lo_half).astype(BF16)
            v16 = qkv_ref[:, cv].astype(BF16)
            kcat = jnp.concatenate([kp_ref[:, cq], kr], axis=0)
            vcat = jnp.concatenate([vp_ref[:, cq], v16], axis=0)
            sc = jnp.where(valid, _dg(_stack_heads(qr, head0), kcat, 1, 1), -jnp.inf)
            mx = jnp.max(sc, axis=1, keepdims=True)
            p = jnp.exp(sc - mx)
            den = jnp.sum(p, axis=1, keepdims=True)
            o_ref[:, cq] = _unstack_heads(_dg((p * (1.0 / den)).astype(BF16), vcat, 1, 0), head0)
            lse2 = mx + jnp.log(den)
            lse_ref[:, cq] = jnp.where(head0, lse2[:ATT_BLOCK], lse2[ATT_BLOCK:])
            kp_ref[:, cq] = kr
            vp_ref[:, cq] = v16

    blk = lambda w: pl.BlockSpec((None, ATT_BLOCK, w), lambda g, t: (g, t, 0))
    shp = jax.ShapeDtypeStruct((GROUPS, s, WIDTH), F32)
    return pl.pallas_call(
        body, name="dilated_attention_fwd", grid=(GROUPS, nb),
        in_specs=[blk(3 * WIDTH), blk(LANES), blk(LANES)],
        out_specs=(blk(WIDTH), blk(WIDTH)), out_shape=(shp, shp),
        scratch_shapes=[pltpu.VMEM((ATT_BLOCK, WIDTH), BF16), pltpu.VMEM((ATT_BLOCK, WIDTH), BF16)],
        compiler_params=_params("arbitrary", "arbitrary"),
    )(qkv, cos, sin)


def _attn_bwd(qkv, cos, sin, o, lse, do, dlse):
    _, s, _ = qkv.shape
    nb = s // ATT_BLOCK

    def body(qkv_ref, cos_ref, sin_ref, cosp_ref, sinp_ref, o_ref, lse_ref, do_ref, dlse_ref,
             dqkv_ref, kp_ref, vp_ref, dka_ref, dva_ref, dqp_ref):
        g, t = pl.program_id(0), pl.program_id(1)
        first = (t & (_blocks_per_subsequence(g, nb) - 1)) == 0
        active = t < nb
        head0, lo_half = _lane_masks(ATT_BLOCK)
        cos_p, sin_p = cosp_ref[...], sinp_ref[...]

        @pl.when(t == 0)
        def _():
            dka_ref[...] = jnp.zeros_like(dka_ref)
            dva_ref[...] = jnp.zeros_like(dva_ref)
            dqp_ref[...] = jnp.zeros_like(dqp_ref)

        dqkv_ref[:, pl.ds(0, WIDTH)] = dqp_ref[...]

        @pl.when(active & first)
        def _():
            kp_ref[...] = jnp.zeros_like(kp_ref)
            vp_ref[...] = jnp.zeros_like(vp_ref)

        @pl.when(active)
        def _():
            cos_b, sin_b = cos_ref[...], sin_ref[...]
            valid = _window_mask(first)
            for sl in range(WIDTH // LANES):
                cq = pl.ds(sl * LANES, LANES)
                ck = pl.ds(WIDTH + sl * LANES, LANES)
                cv = pl.ds(2 * WIDTH + sl * LANES, LANES)
                qr = (_rope(qkv_ref[:, cq], cos_b, sin_b, lo_half) * (HEAD_DIM ** -0.5)).astype(BF16)
                kr = _rope(qkv_ref[:, ck], cos_b, sin_b, lo_half).astype(BF16)
                v16 = qkv_ref[:, cv].astype(BF16)
                kcat = jnp.concatenate([kp_ref[:, cq], kr], axis=0)
                vcat = jnp.concatenate([vp_ref[:, cq], v16], axis=0)
                do_b = do_ref[:, cq]
                do16 = do_b.astype(BF16)
                lse_b = lse_ref[:, cq]
                cterm = dlse_ref[:, cq] - do_b * o_ref[:, cq]
                qm = _stack_heads(qr, head0)
                dom = _stack_heads(do16, head0)
                sc = jnp.where(valid, _dg(qm, kcat, 1, 1), -jnp.inf)
                lse2 = jnp.concatenate([jnp.max(jnp.where(head0, lse_b, -jnp.inf), axis=1, keepdims=True),
                                        jnp.max(jnp.where(head0, -jnp.inf, lse_b), axis=1, keepdims=True)], axis=0)
                c2 = jnp.concatenate([jnp.sum(jnp.where(head0, cterm, 0.0), axis=1, keepdims=True),
                                      jnp.sum(jnp.where(head0, 0.0, cterm), axis=1, keepdims=True)], axis=0)
                p = jnp.exp(sc - lse2)
                ds16 = (p * (_dg(dom, vcat, 1, 1) + c2)).astype(BF16)
                dvc = _dg(p.astype(BF16), dom, 0, 0)
                dkc = _dg(ds16, qm, 0, 0)
                dq = _unstack_heads(_dg(ds16, kcat, 1, 0), head0) * (HEAD_DIM ** -0.5)
                dqp_ref[:, cq] = _rope_bwd(dq, cos_b, sin_b, lo_half).astype(BF16)
                dqkv_ref[:, ck] = _rope_bwd(dka_ref[:, cq] + dkc[:ATT_BLOCK], cos_p, sin_p, lo_half).astype(BF16)
                dqkv_ref[:, cv] = (dva_ref[:, cq] + dvc[:ATT_BLOCK]).astype(BF16)
                dka_ref[:, cq] = dkc[ATT_BLOCK:]
                dva_ref[:, cq] = dvc[ATT_BLOCK:]
                kp_ref[:, cq] = kr
                vp_ref[:, cq] = v16

        @pl.when(jnp.logical_not(active))
        def _():
            for sl in range(WIDTH // LANES):
                cq = pl.ds(sl * LANES, LANES)
                dqkv_ref[:, pl.ds(WIDTH + sl * LANES, LANES)] = _rope_bwd(dka_ref[:, cq], cos_p, sin_p, lo_half).astype(BF16)
                dqkv_ref[:, pl.ds(2 * WIDTH + sl * LANES, LANES)] = dva_ref[:, cq].astype(BF16)

    cur = lambda w: pl.BlockSpec((None, ATT_BLOCK, w), lambda g, t: (g, jnp.minimum(t, nb - 1), 0))
    prev = lambda w: pl.BlockSpec((None, ATT_BLOCK, w), lambda g, t: (g, jnp.maximum(t - 1, 0), 0))
    return pl.pallas_call(
        body, name="dilated_attention_bwd", grid=(GROUPS, nb + 1),
        in_specs=[cur(3 * WIDTH), cur(LANES), cur(LANES), prev(LANES), prev(LANES),
                  cur(WIDTH), cur(WIDTH), cur(WIDTH), cur(WIDTH)],
        out_specs=prev(3 * WIDTH), out_shape=jax.ShapeDtypeStruct((GROUPS, s, 3 * WIDTH), BF16),
        scratch_shapes=[pltpu.VMEM((ATT_BLOCK, WIDTH), BF16), pltpu.VMEM((ATT_BLOCK, WIDTH), BF16),
                        pltpu.VMEM((ATT_BLOCK, WIDTH), F32), pltpu.VMEM((ATT_BLOCK, WIDTH), F32),
                        pltpu.VMEM((ATT_BLOCK, WIDTH), BF16)],
        compiler_params=_params("arbitrary", "arbitrary"),
    )(qkv, cos, sin, cos, sin, o, lse, do, dlse)


CONV_PAD = SUBLANES


def _gdn_post(y, is_q, is_k):
    head0, _ = _lane_masks(y.shape[0])
    c = _silu(y)
    sq = c * c
    ss0 = jnp.sum(jnp.where(head0, sq, 0.0), axis=1, keepdims=True)
    ss1 = jnp.sum(jnp.where(head0, 0.0, sq), axis=1, keepdims=True)
    r = jnp.where(head0, lax.rsqrt(ss0 + NORM_EPS), lax.rsqrt(ss1 + NORM_EPS))
    scale = jnp.where(is_q, HEAD_DIM ** -0.5, 1.0).astype(F32)
    return jnp.where(is_q | is_k, c * r * scale, c)


def _conv_rows(xp_ref, w, c0, rows):
    y = w[0:1, :] * xp_ref[pl.ds(c0 + CONV_PAD - (CONV_K - 1), rows), :]
    for k in range(1, CONV_K):
        y = y + w[k:k + 1, :] * xp_ref[pl.ds(c0 + CONV_PAD - (CONV_K - 1) + k, rows), :]
    return y


def _gdn_pre_fwd(proj_r, conv8, col0):
    s = proj_r.shape[0]
    tr = _row_block(s, 512)
    nblk = QKV_B // LANES
    nq = WIDTH // LANES

    def body(x_ref, w_ref, out_ref, xp_ref):
        j = pl.program_id(0)
        is_q, is_k = j < nq, (j >= nq) & (j < 2 * nq)
        xp_ref[pl.ds(0, CONV_PAD), :] = jnp.zeros((CONV_PAD, LANES), F32)
        xp_ref[pl.ds(CONV_PAD, s), :] = x_ref[...]
        w = w_ref[...]
        for c in range(s // tr):
            out_ref[pl.ds(c * tr, tr), :] = _gdn_post(_conv_rows(xp_ref, w, c * tr, tr), is_q, is_k)

    return pl.pallas_call(
        body, name="gdn_conv_fwd", grid=(nblk,),
        in_specs=[pl.BlockSpec((s, LANES), lambda j: (0, col0 + j)), pl.BlockSpec((SUBLANES, LANES), lambda j: (0, j))],
        out_specs=pl.BlockSpec((s, LANES), lambda j: (0, j)),
        out_shape=jax.ShapeDtypeStruct((s, QKV_B), F32),
        scratch_shapes=[pltpu.VMEM((s + CONV_PAD, LANES), F32)],
        compiler_params=_params("parallel"),
    )(proj_r, conv8)


def _gdn_pre_bwd(proj_r, conv8, dc, col0):
    s = proj_r.shape[0]
    tr = _row_block(s, 512)
    nblk = QKV_B // LANES
    nq = WIDTH // LANES

    def body(x_ref, w_ref, dc_ref, dx_ref, dw_ref, xp_ref, dyp_ref):
        j = pl.program_id(0)
        is_q, is_k = j < nq, (j >= nq) & (j < 2 * nq)
        xp_ref[pl.ds(0, CONV_PAD), :] = jnp.zeros((CONV_PAD, LANES), F32)
        xp_ref[pl.ds(CONV_PAD, s), :] = x_ref[...]
        dyp_ref[pl.ds(s, CONV_PAD), :] = jnp.zeros((CONV_PAD, LANES), F32)
        w = w_ref[...]
        for c in range(s // tr):
            y = _conv_rows(xp_ref, w, c * tr, tr)
            _, vjp = jax.vjp(lambda yy: _gdn_post(yy, is_q, is_k), y)
            dyp_ref[pl.ds(c * tr, tr), :] = vjp(dc_ref[pl.ds(c * tr, tr), :])[0]
        dws = [jnp.zeros((1, LANES), F32) for _ in range(CONV_K)]
        for c in range(s // tr):
            c0 = c * tr
            dy = dyp_ref[pl.ds(c0, tr), :]
            dx = w[0:1, :] * dyp_ref[pl.ds(c0 + CONV_K - 1, tr), :]
            for k in range(1, CONV_K):
                dx = dx + w[k:k + 1, :] * dyp_ref[pl.ds(c0 + CONV_K - 1 - k, tr), :]
            dx_ref[pl.ds(c0, tr), :] = dx.astype(BF16)
            for k in range(CONV_K):
                xs = xp_ref[pl.ds(c0 + CONV_PAD - (CONV_K - 1) + k, tr), :]
                dws[k] = dws[k] + jnp.sum(dy * xs, axis=0, keepdims=True)
        row = lax.broadcasted_iota(jnp.int32, (SUBLANES, LANES), 0)
        dwb = jnp.zeros((SUBLANES, LANES), F32)
        for k in range(CONV_K):
            dwb = dwb + jnp.where(row == k, dws[k], 0.0)
        dw_ref[...] = dwb

    return pl.pallas_call(
        body, name="gdn_conv_bwd", grid=(nblk,),
        in_specs=[pl.BlockSpec((s, LANES), lambda j: (0, col0 + j)), pl.BlockSpec((SUBLANES, LANES), lambda j: (0, j)),
                  pl.BlockSpec((s, LANES), lambda j: (0, j))],
        out_specs=(pl.BlockSpec((s, LANES), lambda j: (0, j)), pl.BlockSpec((SUBLANES, LANES), lambda j: (0, j))),
        out_shape=(jax.ShapeDtypeStruct((s, QKV_B), BF16), jax.ShapeDtypeStruct((SUBLANES, QKV_B), F32)),
        scratch_shapes=[pltpu.VMEM((s + CONV_PAD, LANES), F32), pltpu.VMEM((s + CONV_PAD, LANES), F32)],
        compiler_params=_params("parallel"),
    )(proj_r, conv8, dc)


def _gdn_chunk(q, k, v, bcol, acol, alog, dtb, gnw, state, t_saved=None):
    n = q.shape[-2]
    shp = (1, n, n)
    row = lax.broadcasted_iota(jnp.int32, shp, 1)
    col = lax.broadcasted_iota(jnp.int32, shp, 2)
    beta = _sigmoid(bcol)
    g = -jnp.exp(alog) * _softplus(acol + dtb)
    g_row = jnp.sum(jnp.where(row == col, g, 0.0), axis=-2, keepdims=True)
    big_g = jnp.sum(jnp.where(row >= col, g_row, 0.0), axis=-1, keepdims=True)
    big_g_row = jnp.sum(jnp.where(row <= col, g, 0.0), axis=-2, keepdims=True)
    decay_incl = jnp.exp(jnp.where(row >= col, big_g - big_g_row, -jnp.inf))
    decay_strict = jnp.where(row > col, decay_incl, 0.0)
    k_beta = k * beta
    a_mat = _mm_nt(k_beta, k) * decay_strict
    t_inv = _tri_inv(a_mat) if t_saved is None else _tri_inv_saved(a_mat, t_saved)
    e_g = jnp.exp(big_g)
    u = _mm(t_inv, v * beta)
    w = _mm(t_inv, k_beta * e_g)
    attn = _mm_nt(q, k) * decay_incl
    v_new = u - _mm(w, state)
    o = _mm(q * e_g, state) + _mm(attn, v_new)
    total = jnp.sum(g, axis=-2, keepdims=True)
    new_state = state * jnp.exp(total) + _mm_tn(k * jnp.exp(total - big_g), v_new)
    return _rmsnorm(o, gnw), new_state, t_inv


def _split_heads(x):
    return jnp.stack([x[:, h * HEAD_DIM:(h + 1) * HEAD_DIM] for h in range(HEADS)], axis=0)


def _merge_heads(x):
    return jnp.concatenate([x[h] for h in range(HEADS)], axis=1)


def _logit_columns(ba):
    lane = lax.broadcasted_iota(jnp.int32, ba.shape, 1)

    def cols(off):
        return jnp.stack([jnp.sum(jnp.where(lane == off + h, ba, 0.0), axis=1, keepdims=True) for h in range(HEADS)], axis=0)

    return cols(0), cols(HEADS)


def _logit_block(dbc, dac, shape):
    lane = lax.broadcasted_iota(jnp.int32, shape, 1)
    out = jnp.zeros(shape, F32)
    for h in range(HEADS):
        out = out + jnp.where(lane == h, dbc[h], 0.0) + jnp.where(lane == HEADS + h, dac[h], 0.0)
    return out


def _gdn_scan_fwd(cqkv, proj_r, ba_col, alog, dtb, gnw):
    s = cqkv.shape[0]
    nc = s // CHUNK

    def body(q_ref, k_ref, v_ref, ba_ref, al_ref, dt_ref, gnw_ref, o_ref, st_ref, ti_ref, state_ref):
        @pl.when(pl.program_id(0) == 0)
        def _():
            state_ref[...] = jnp.zeros_like(state_ref)

        st = state_ref[...]
        st_ref[...] = st
        bcol, acol = _logit_columns(ba_ref[...])
        o, new_st, t_inv = _gdn_chunk(_split_heads(q_ref[...]), _split_heads(k_ref[...]), _split_heads(v_ref[...]),
                                      bcol, acol, al_ref[...], dt_ref[...], gnw_ref[...], st)
        o_ref[...] = _merge_heads(o)
        ti_ref[...] = t_inv
        state_ref[...] = new_st

    part = lambda i: pl.BlockSpec((CHUNK, WIDTH), lambda n: (n, i))
    par = pl.BlockSpec((HEADS, 1, 1), lambda n: (0, 0, 0))
    per_chunk = pl.BlockSpec((None, HEADS, HEAD_DIM, HEAD_DIM), lambda n: (n, 0, 0, 0))
    per_chunk_shape = jax.ShapeDtypeStruct((nc, HEADS, HEAD_DIM, HEAD_DIM), F32)
    return pl.pallas_call(
        body, name="gdn_scan_fwd", grid=(nc,),
        in_specs=[part(0), part(1), part(2), pl.BlockSpec((CHUNK, LANES), lambda n: (n, ba_col)), par, par,
                  pl.BlockSpec((1, 1, HEAD_DIM), lambda n: (0, 0, 0))],
        out_specs=(part(0), per_chunk, per_chunk),
        out_shape=(jax.ShapeDtypeStruct((s, WIDTH), F32), per_chunk_shape, per_chunk_shape),
        scratch_shapes=[pltpu.VMEM((HEADS, HEAD_DIM, HEAD_DIM), F32)],
        compiler_params=_params("arbitrary"),
    )(cqkv, cqkv, cqkv, proj_r, alog, dtb, gnw)


def _gdn_scan_bwd(cqkv, proj_r, ba_col, alog, dtb, gnw, states, t_invs, do):
    s = cqkv.shape[0]
    nc = s // CHUNK

    def body(q_ref, k_ref, v_ref, ba_ref, al_ref, dt_ref, gnw_ref, st_ref, ti_ref, do_ref,
             dqkv_ref, dba_ref, dal_ref, ddt_ref, dgnw_ref, dstate_ref):
        @pl.when(pl.program_id(0) == 0)
        def _():
            dstate_ref[...] = jnp.zeros_like(dstate_ref)
            dal_ref[...] = jnp.zeros_like(dal_ref)
            ddt_ref[...] = jnp.zeros_like(ddt_ref)
            dgnw_ref[...] = jnp.zeros_like(dgnw_ref)

        bcol, acol = _logit_columns(ba_ref[...])
        t_saved = ti_ref[...]
        _, vjp = jax.vjp(lambda *a: _gdn_chunk(*a, t_saved=t_saved)[:2],
                         _split_heads(q_ref[...]), _split_heads(k_ref[...]), _split_heads(v_ref[...]),
                         bcol, acol, al_ref[...], dt_ref[...], gnw_ref[...], st_ref[...])
        dq, dk, dv, dbc, dac, dal, ddt, dgn, dst = vjp((_split_heads(do_ref[...]), dstate_ref[...]))
        dqkv_ref[:, pl.ds(0, WIDTH)] = _merge_heads(dq)
        dqkv_ref[:, pl.ds(WIDTH, WIDTH)] = _merge_heads(dk)
        dqkv_ref[:, pl.ds(2 * WIDTH, WIDTH)] = _merge_heads(dv)
        dba_ref[...] = _logit_block(dbc, dac, dba_ref.shape)
        dstate_ref[...] = dst
        dal_ref[...] += dal
        ddt_ref[...] += ddt
        dgnw_ref[...] += dgn

    rev = lambda n: nc - 1 - n
    part = lambda i: pl.BlockSpec((CHUNK, WIDTH), lambda n: (rev(n), i))
    par = pl.BlockSpec((HEADS, 1, 1), lambda n: (0, 0, 0))
    vec = pl.BlockSpec((1, 1, HEAD_DIM), lambda n: (0, 0, 0))
    par_shape = jax.ShapeDtypeStruct((HEADS, 1, 1), F32)
    per_chunk = pl.BlockSpec((None, HEADS, HEAD_DIM, HEAD_DIM), lambda n: (rev(n), 0, 0, 0))
    return pl.pallas_call(
        body, name="gdn_scan_bwd", grid=(nc,),
        in_specs=[part(0), part(1), part(2), pl.BlockSpec((CHUNK, LANES), lambda n: (rev(n), ba_col)), par, par, vec,
                  per_chunk, per_chunk, part(0)],
        out_specs=(pl.BlockSpec((CHUNK, QKV_B), lambda n: (rev(n), 0)), pl.BlockSpec((CHUNK, LANES), lambda n: (rev(n), 0)),
                   par, par, vec),
        out_shape=(jax.ShapeDtypeStruct((s, QKV_B), F32), jax.ShapeDtypeStruct((s, LANES), F32), par_shape, par_shape,
                   jax.ShapeDtypeStruct((1, 1, HEAD_DIM), F32)),
        scratch_shapes=[pltpu.VMEM((HEADS, HEAD_DIM, HEAD_DIM), F32)],
        compiler_params=_params("arbitrary"),
    )(cqkv, cqkv, cqkv, proj_r, alog, dtb, gnw, states, t_invs, do)


def _tail_loss(x, tgt, o0, o1, o2, l0, l1, l2, ga, gb, za, zb, ob, fnw, wua, wub, wo, tap_a, tap_b, tap_o):
    lm = jnp.maximum(jnp.maximum(l0, l1), l2)
    e0, e1, e2 = jnp.exp(l0 - lm), jnp.exp(l1 - lm), jnp.exp(l2 - lm)
    o_a = (e0 * o0 + e1 * o1 + e2 * o2) / (e0 + e1 + e2)
    y_a = _mm_tap(o_a * _silu(za), wua, tap_a)
    y_b = _mm_tap(ob * _silu(zb), wub, tap_b)
    merged = _sigmoid(ga) * y_a + _sigmoid(gb) * y_b
    y = _rmsnorm(x + _mm_tap(merged, wo, tap_o), fnw)
    err = y - tgt
    per_token = jnp.sum(err * err, axis=1, keepdims=True) * (0.5 / x.shape[1])
    return jnp.sum(per_token, axis=0, keepdims=True)


def _tail(x, tgt, o_all, lse_all, og12, lg12, proj_r, ob, wua, wub, wo, fnw):
    s, d = x.shape
    tm = _row_block(s, 128)
    col_za = 2 * d // WIDTH
    col_zb = (2 * d + WIDTH + QKV_B) // WIDTH

    def body(x_ref, t_ref, o0_ref, o1_ref, o2_ref, l0_ref, l1_ref, l2_ref, ga_ref, gb_ref, za_ref, zb_ref, ob_ref,
             wua_ref, wub_ref, wo_ref, fnw_ref,
             loss_ref, dx_ref, do0_ref, do1_ref, do2_ref, dl0_ref, dl1_ref, dl2_ref, dga_ref, dgb_ref, dza_ref,
             dzb_ref, dob_ref, dwua_ref, dwub_ref, dwo_ref, dfnw_ref):
        @pl.when(pl.program_id(0) == 0)
        def _():
            for r in (loss_ref, dwua_ref, dwub_ref, dwo_ref, dfnw_ref):
                r[...] = jnp.zeros_like(r)

        args = (x_ref[...], t_ref[...], o0_ref[...], o1_ref[...], o2_ref[...], l0_ref[...], l1_ref[...], l2_ref[...],
                ga_ref[...], gb_ref[...], za_ref[...], zb_ref[...], ob_ref[...], fnw_ref[...],
                wua_ref[...], wub_ref[...], wo_ref[...],
                jnp.zeros(wua_ref.shape, F32), jnp.zeros(wub_ref.shape, F32), jnp.zeros(wo_ref.shape, F32))
        loss, vjp = jax.vjp(_tail_loss, *args)
        (dx, _, do0, do1, do2, dl0, dl1, dl2, dga, dgb, dza, dzb, dob, dfnw, _, _, _, dwua, dwub, dwo) = vjp(
            jnp.ones((1, 1), F32))
        loss_ref[...] += jnp.broadcast_to(loss, loss_ref.shape)
        dx_ref[...] = dx
        do0_ref[...], do1_ref[...], do2_ref[...] = do0, do1, do2
        dl0_ref[...], dl1_ref[...], dl2_ref[...] = dl0, dl1, dl2
        dga_ref[...] = dga.astype(BF16)
        dgb_ref[...] = dgb.astype(BF16)
        dza_ref[...] = dza.astype(BF16)
        dzb_ref[...] = dzb.astype(BF16)
        dob_ref[...] = dob
        dwua_ref[...] += dwua
        dwub_ref[...] += dwub
        dwo_ref[...] += dwo
        dfnw_ref[...] += dfnw

    row = lambda w, c=0: pl.BlockSpec((tm, w), lambda i: (i, c))
    grp0 = pl.BlockSpec((None, tm, WIDTH), lambda i: (0, i, 0))
    full = lambda a, b: pl.BlockSpec((a, b), lambda i: (0, 0))
    f32 = lambda a, b: jax.ShapeDtypeStruct((a, b), F32)
    b16 = lambda a, b: jax.ShapeDtypeStruct((a, b), BF16)
    stacked = jax.ShapeDtypeStruct((GROUPS, s, WIDTH), F32)
    gspecs = [grp0, row(WIDTH), row(WIDTH)]
    in_specs = ([row(d), row(d)] + gspecs * 2 + [row(d, 0), row(d, 1), row(WIDTH, col_za), row(WIDTH, col_zb),
                row(WIDTH), full(WIDTH, d), full(WIDTH, d), full(d, d), full(1, d)])
    out_specs = ([full(SUBLANES, LANES), row(d)] + gspecs * 2 + [row(d), row(d), row(WIDTH), row(WIDTH), row(WIDTH),
                 full(WIDTH, d), full(WIDTH, d), full(d, d), full(1, d)])
    gshapes = [stacked, f32(s, WIDTH), f32(s, WIDTH)]
    out_shape = ([f32(SUBLANES, LANES), f32(s, d)] + gshapes * 2 + [b16(s, d), b16(s, d), b16(s, WIDTH),
                 b16(s, WIDTH), f32(s, WIDTH), f32(WIDTH, d), f32(WIDTH, d), f32(d, d), f32(1, d)])
    return pl.pallas_call(
        body, name="tail_fwd_bwd", grid=(s // tm,),
        in_specs=in_specs, out_specs=tuple(out_specs), out_shape=tuple(out_shape),
        compiler_params=_params("arbitrary"),
    )(x, tgt, o_all, og12[0], og12[1], lse_all, lg12[0], lg12[1], proj_r, proj_r, proj_r, proj_r, ob, wua, wub, wo, fnw)


def _to_dilated(a, dil):
    if dil == 1:
        return a
    s = a.shape[0]
    return a.reshape(s // dil, dil, -1).transpose(1, 0, 2).reshape(a.shape)


def _from_dilated(a, dil):
    if dil == 1:
        return a
    s = a.shape[0]
    return a.reshape(dil, s // dil, -1).transpose(1, 0, 2).reshape(a.shape)


def _head_major(a):
    return a.reshape(a.shape[0], HEADS, HEAD_DIM).transpose(1, 0, 2)


def _from_head_major(a):
    return a.transpose(1, 0, 2).reshape(a.shape[1], WIDTH)


def _rope_tables(s):
    inv_freq = ROPE_THETA ** (-jnp.arange(0, HEAD_DIM, 2, dtype=F32) / HEAD_DIM)
    ang = jnp.arange(s, dtype=F32)[:, None] * inv_freq[None, :]
    cos_n = jnp.tile(jnp.cos(ang), (1, 2 * LANES // HEAD_DIM))
    sin_h = jnp.sin(ang)
    sin_n = jnp.tile(jnp.concatenate([-sin_h, sin_h], axis=1), (1, LANES // HEAD_DIM))
    return (jnp.stack([_to_dilated(cos_n, dil) for dil in DILATIONS]),
            jnp.stack([_to_dilated(sin_n, dil) for dil in DILATIONS]))


def _pack_rows(parts, dtype, row_multiple):
    flat = jnp.concatenate([p.reshape(-1).astype(dtype) for p in parts])
    tile = row_multiple * LANES
    pad = (-flat.shape[0]) % tile
    return jnp.pad(flat, (0, pad)).reshape(-1, LANES)


def _unpack_rows(packed, shapes):
    flat = packed.reshape(-1)
    out, start = [], 0
    for shp in shapes:
        size = 1
        for n in shp:
            size *= n
        out.append(flat[start:start + size].reshape(shp))
        start += size
    return out


def kernel(x, norm_w, w_in, conv_w, a_log, dt_bias, gdn_norm_w, w_up_a, w_up_b, w_out, final_norm_w, loss_target, m_norm_w, m_w_in, m_conv_w, m_a_log, m_dt_bias, m_gdn_norm_w, m_w_up_a, m_w_up_b, m_w_out, m_final_norm_w, v_norm_w, v_w_in, v_conv_w, v_a_log, v_dt_bias, v_gdn_norm_w, v_w_up_a, v_w_up_b, v_w_out, v_final_norm_w):
    x2, tgt = x[0], loss_target[0]
    s, d = x2.shape
    me = 4 * lax.axis_index("x") + 2 * lax.axis_index("y") + lax.axis_index("c")
    win8 = w_in.shape[2]
    conv8w = conv_w.shape[2]

    conv_shard = jnp.pad(conv_w[0], ((0, SUBLANES - CONV_K), (0, 0)))
    w_in_g, wua_g, wub_g, wo_g, conv_g = _all_gather(
        [w_in[0].astype(BF16), w_up_a[0].astype(BF16), w_up_b[0].astype(BF16), w_out[0].astype(BF16), conv_shard])
    w_in_f = jnp.concatenate([w_in_g[i] for i in range(N_DEV)], axis=1)
    wua = jnp.concatenate([wua_g[i] for i in range(N_DEV)], axis=1)
    wub = jnp.concatenate([wub_g[i] for i in range(N_DEV)], axis=1)
    wo = wo_g.reshape(d, d)
    conv8 = jnp.concatenate([conv_g[i] for i in range(N_DEV)], axis=1)

    w_qkv = w_in_f[:, :QKV_A].reshape(d, GROUPS, QKV_B).transpose(1, 0, 2)
    w_rest = jnp.concatenate([
        w_in_f[:, OFF_GATE:OFF_GATE + 2 * d], w_in_f[:, OFF_ZA:OFF_ZA + WIDTH], w_in_f[:, OFF_QKVB:OFF_QKVB + QKV_B],
        w_in_f[:, OFF_ZB:OFF_ZB + WIDTH], w_in_f[:, OFF_BA:OFF_BA + 2 * HEADS],
        jnp.zeros((d, BA_PAD - 2 * HEADS), BF16)], axis=1)
    col_qkvb = (2 * d + WIDTH) // LANES
    col_ba = (2 * d + 2 * WIDTH + QKV_B) // LANES

    h = _rms_fwd(x2, norm_w)
    h_all = jnp.stack([_to_dilated(h, dil) for dil in DILATIONS])
    qkv_all = _matmul(h_all, w_qkv, F32, "in_proj_attention")
    proj_r = _matmul(h[None], w_rest[None], F32, "in_proj_rest")[0]
    cos, sin = _rope_tables(s)
    o_all, lse_all = _attn_fwd(qkv_all, cos, sin)
    og12 = [_from_dilated(o_all[g], DILATIONS[g]) for g in (1, 2)]
    lg12 = [_from_dilated(lse_all[g], DILATIONS[g]) for g in (1, 2)]

    cqkv = _gdn_pre_fwd(proj_r, conv8, col_qkvb)
    alog3, dtb3, gnw3 = a_log.reshape(HEADS, 1, 1), dt_bias.reshape(HEADS, 1, 1), gdn_norm_w.reshape(1, 1, HEAD_DIM)
    ob, states, t_invs = _gdn_scan_fwd(cqkv, proj_r, col_ba, alog3, dtb3, gnw3)

    (loss_blk, dx_res, do_all, do1, do2, dl_all, dl1, dl2, dga, dgb, dza, dzb, dob, dwua, dwub, dwo, dfnw) = _tail(
        x2, tgt, o_all, lse_all, og12, lg12, proj_r, ob, wua, wub, wo, final_norm_w.reshape(1, d))

    for g, (t_o, t_l) in ((1, (do1, dl1)), (2, (do2, dl2))):
        do_all = do_all.at[g].set(_to_dilated(t_o, DILATIONS[g]))
        dl_all = dl_all.at[g].set(_to_dilated(t_l, DILATIONS[g]))
    dqkv_all = _attn_bwd(qkv_all, cos, sin, o_all, lse_all, do_all, dl_all)

    dcqkv, dba, dalog3, ddtb3, dgnw3 = _gdn_scan_bwd(cqkv, proj_r, col_ba, alog3, dtb3, gnw3, states, t_invs, dob)
    dqkv_b, dconv8 = _gdn_pre_bwd(proj_r, conv8, dcqkv, col_qkvb)
    dproj_r = jnp.concatenate([dga, dgb, dza, dqkv_b, dzb,
                               jnp.pad(dba.astype(BF16), ((0, 0), (0, BA_PAD - LANES)))], axis=1)

    dw_qkv = _matmul(h_all, dqkv_all, F32, "in_proj_attention_dw", mode="tn", tk=2048)
    dw_rest = _matmul(h[None], dproj_r[None], F32, "in_proj_rest_dw", mode="tn", tk=2048)[0]
    dh_a = _matmul(dqkv_all, w_qkv, F32, "in_proj_attention_dh", mode="nt", tk=2048)
    dh_r = _matmul(dproj_r[None], w_rest[None], F32, "in_proj_rest_dh", mode="nt")[0]
    dh_parts = [dh_r] + [_from_dilated(dh_a[g], DILATIONS[g]) for g in (1, 2)]
    grad_x, dnorm_w = _rms_bwd(x2, norm_w, dh_a, dh_parts, dx_res)

    o2 = 2 * d
    dw_in = jnp.concatenate([
        dw_qkv.transpose(1, 0, 2).reshape(d, QKV_A),
        dw_rest[:, o2:o2 + WIDTH], dw_rest[:, o2 + WIDTH:o2 + WIDTH + QKV_B],
        dw_rest[:, o2 + WIDTH + QKV_B:o2 + 2 * WIDTH + QKV_B],
        dw_rest[:, o2 + 2 * WIDTH + QKV_B:o2 + 2 * WIDTH + QKV_B + 2 * HEADS],
        dw_rest[:, :o2]], axis=1)

    def col_slabs(a, width):
        return jnp.stack([a[:, j * width:(j + 1) * width] for j in range(N_DEV)])

    slabs = [col_slabs(dw_in, win8), col_slabs(dwua, d // N_DEV), col_slabs(dwub, d // N_DEV),
             dwo.reshape(N_DEV, d // N_DEV, d)]
    from_sibling = _sibling_exchange(slabs)
    core = lax.axis_index("c").astype(jnp.int32).reshape(1)
    partials = [_pair_sum(a, b, core, "grads_pair_sum_%d" % i) for i, (a, b) in enumerate(zip(slabs, from_sibling))]
    contrib = _chip_exchange(partials)

    small_parts = [dnorm_w, dfnw, dconv8[:CONV_K], dalog3[:, 0, 0], ddtb3[:, 0, 0], dgnw3[0], loss_blk[0, 0:1]]
    small_rows = [-(-p.size // LANES) for p in small_parts]
    small = jnp.concatenate([jnp.pad(p.reshape(-1), (0, r * LANES - p.size)).reshape(r, LANES)
                             for p, r in zip(small_parts, small_rows)])
    small = jnp.pad(small, ((0, (-small.shape[0]) % SUBLANES), (0, 0)))
    small_sum = _small_all_reduce(small)
    pieces, r0 = [], 0
    for p, r in zip(small_parts, small_rows):
        pieces.append(small_sum[r0:r0 + r].reshape(-1)[:p.size].reshape(p.shape))
        r0 += r
    g_norm_w, g_fnw, g_conv_full, g_alog, g_dtb, g_gnw, loss_sum = pieces
    g_conv = lax.dynamic_slice(g_conv_full, (0, me * conv8w), (CONV_K, conv8w))

    big = [_adamw(c, w[0], m[0], v[0], name) for c, w, m, v, name in (
        (contrib[0], w_in, m_w_in, v_w_in, "adamw_w_in"), (contrib[1], w_up_a, m_w_up_a, v_w_up_a, "adamw_w_up_a"),
        (contrib[2], w_up_b, m_w_up_b, v_w_up_b, "adamw_w_up_b"), (contrib[3], w_out, m_w_out, v_w_out, "adamw_w_out"))]
    g_big, d_big, nm_big, nv_big = ([t[i] for t in big] for i in range(4))

    small_ws = [norm_w, final_norm_w, conv_w, a_log, dt_bias, gdn_norm_w]
    small_ms = [m_norm_w, m_final_norm_w, m_conv_w, m_a_log, m_dt_bias, m_gdn_norm_w]
    small_vs = [v_norm_w, v_final_norm_w, v_conv_w, v_a_log, v_dt_bias, v_gdn_norm_w]
    small_gs = [g_norm_w, g_fnw, g_conv, g_alog, g_dtb, g_gnw]
    small_shapes = [t.shape for t in small_ws]
    sm = _adamw(_pack_rows(small_gs, F32, SUBLANES)[None], _pack_rows(small_ws, F32, SUBLANES),
                _pack_rows(small_ms, F32, SUBLANES), _pack_rows(small_vs, F32, SUBLANES), "adamw_small")
    g_sm, d_sm, nm_sm, nv_sm = (_unpack_rows(t, small_shapes) for t in sm)

    def ordered(bigs, smalls):
        nw, fnw_, cw, al, dtb, gn = smalls
        wi, ua, ub, wo_ = (t[None] for t in bigs)
        return [nw, wi, cw, al, dtb, gn, ua, ub, wo_, fnw_]

    return (loss_sum.reshape(()), grad_x[None], *ordered(g_big, g_sm), *ordered(d_big, d_sm),
            *ordered(nm_big, nm_sm), *ordered(nv_big, nv_sm))
```

```python
import functools

import jax
import jax.numpy as jnp
from jax import lax
from jax.experimental import pallas as pl
from jax.experimental.pallas import tpu as pltpu

F32 = jnp.float32
BF16 = jnp.bfloat16
MESH = pl.DeviceIdType.MESH
N_DEV = 8
LANES = 128
SUBLANES = 8

GROUPS = 3
HEADS = 8
HEAD_DIM = 64
WIDTH = HEADS * HEAD_DIM
ATT_BLOCK = 128
DILATIONS = (1, 4, 16)
N_BACK = 128
CONV_K = 4
CHUNK = 64
QKV_B = 3 * WIDTH
QKV_A = GROUPS * 3 * WIDTH
BA_PAD = 512
NORM_EPS = 1e-6
ROPE_THETA = 10000.0
ADAM_LR, ADAM_B1, ADAM_B2, ADAM_EPS, ADAM_WD, ADAM_STEP = 0.001, 0.9, 0.999, 1e-08, 0.01, 10

VMEM_LIMIT = 56 * 1024 * 1024

OFF_ZA = QKV_A
OFF_QKVB = OFF_ZA + WIDTH
OFF_ZB = OFF_QKVB + QKV_B
OFF_BA = OFF_ZB + WIDTH
OFF_GATE = OFF_BA + 2 * HEADS


def _params(*sem):
    return pltpu.CompilerParams(dimension_semantics=sem, vmem_limit_bytes=VMEM_LIMIT)


def _dg(a, b, ca, cb):
    nb = a.ndim - 2
    batch = tuple(range(nb))
    return lax.dot_general(a, b, (((nb + ca,), (nb + cb,)), (batch, batch)), preferred_element_type=F32)


@jax.custom_vjp
def _mm(a, b):
    return _dg(a.astype(BF16), b.astype(BF16), 1, 0)


def _mm_fwd(a, b):
    return _mm(a, b), (a.astype(BF16), b.astype(BF16))


def _mm_bwd(res, ct):
    a16, b16 = res
    c16 = ct.astype(BF16)
    return _dg(c16, b16, 1, 1), _dg(a16, c16, 0, 0)


_mm.defvjp(_mm_fwd, _mm_bwd)


@jax.custom_vjp
def _mm_nt(a, b):
    return _dg(a.astype(BF16), b.astype(BF16), 1, 1)


def _mm_nt_fwd(a, b):
    return _mm_nt(a, b), (a.astype(BF16), b.astype(BF16))


def _mm_nt_bwd(res, ct):
    a16, b16 = res
    c16 = ct.astype(BF16)
    return _dg(c16, b16, 1, 0), _dg(c16, a16, 0, 0)


_mm_nt.defvjp(_mm_nt_fwd, _mm_nt_bwd)


@jax.custom_vjp
def _mm_tn(a, b):
    return _dg(a.astype(BF16), b.astype(BF16), 0, 0)


def _mm_tn_fwd(a, b):
    return _mm_tn(a, b), (a.astype(BF16), b.astype(BF16))


def _mm_tn_bwd(res, ct):
    a16, b16 = res
    c16 = ct.astype(BF16)
    return _dg(b16, c16, 1, 1), _dg(a16, c16, 1, 0)


_mm_tn.defvjp(_mm_tn_fwd, _mm_tn_bwd)


@jax.custom_vjp
def _mm_tap(a, w16, tap):
    return _dg(a.astype(BF16), w16, 1, 0)


def _mm_tap_fwd(a, w16, tap):
    return _mm_tap(a, w16, tap), (a.astype(BF16), w16)


def _mm_tap_bwd(res, ct):
    a16, w16 = res
    c16 = ct.astype(BF16)
    return _dg(c16, w16, 1, 1), jnp.zeros_like(w16), _dg(a16, c16, 0, 0)


_mm_tap.defvjp(_mm_tap_fwd, _mm_tap_bwd)


def _split16(a):
    hi = a.astype(BF16)
    lo = (a - hi.astype(F32)).astype(BF16)
    return hi, lo


def _dot3(a, b, ca, cb):
    ah, al = _split16(a)
    bh, bl = _split16(b)
    return _dg(ah, bh, ca, cb) + (_dg(ah, bl, ca, cb) + _dg(al, bh, ca, cb))


def _tri_inv_impl(a):
    n = a.shape[-1]
    shp = (1,) * (a.ndim - 2) + (n, n)
    eye = (lax.broadcasted_iota(jnp.int32, shp, a.ndim - 2) == lax.broadcasted_iota(jnp.int32, shp, a.ndim - 1)).astype(F32)
    x = eye - a
    p = a
    for it in range(5):
        dot = _dot3 if it < 2 else (lambda u, v, cu, cv: _dg(u.astype(BF16), v.astype(BF16), cu, cv))
        p = dot(p, p, 1, 0)
        x = x + dot(x, p, 1, 0)
    return x


@jax.custom_vjp
def _tri_inv(a):
    return _tri_inv_impl(a)


def _tri_inv_fwd(a):
    t = _tri_inv_impl(a)
    return t, t


def _tri_inv_bwd(t, ct):
    t16 = t.astype(BF16)
    return (-_dg(_dg(t16, ct.astype(BF16), 0, 0).astype(BF16), t16, 1, 1),)


_tri_inv.defvjp(_tri_inv_fwd, _tri_inv_bwd)


@jax.custom_vjp
def _tri_inv_saved(a, t):
    return t


def _tri_inv_saved_fwd(a, t):
    return t, t


def _tri_inv_saved_bwd(t, ct):
    return _tri_inv_bwd(t, ct) + (jnp.zeros_like(t),)


_tri_inv_saved.defvjp(_tri_inv_saved_fwd, _tri_inv_saved_bwd)


def _sigmoid(x):
    return 1.0 / (1.0 + jnp.exp(-x))


def _silu(x):
    return x * _sigmoid(x)


def _softplus(x):
    return jnp.maximum(x, 0.0) + jnp.log(1.0 + jnp.exp(-jnp.abs(x)))


def _rmsnorm(x, w):
    return x * lax.rsqrt(jnp.mean(x * x, axis=-1, keepdims=True) + NORM_EPS) * w


def _row_block(rows, cap):
    best = None
    for cand in range(SUBLANES, min(rows, cap) + 1, SUBLANES):
        if rows % cand == 0:
            best = cand
    assert best is not None, rows
    return best


def _mesh_peers():
    x, y, c = lax.axis_index("x"), lax.axis_index("y"), lax.axis_index("c")
    me = 4 * x + 2 * y + c
    peers = []
    for k in range(1, N_DEV):
        px = 1 - x if (k >> 2) & 1 else x
        py = 1 - y if (k >> 1) & 1 else y
        pc = 1 - c if k & 1 else c
        peers.append(((px, py, pc), 4 * px + 2 * py + pc))
    return me, peers


N_CHIPS = 4
OTHER_CHIPS = 3


def _chip_peers():
    x, y, c = lax.axis_index("x"), lax.axis_index("y"), lax.axis_index("c")
    return x, y, c, [(1 - x, y), (x, 1 - y), (1 - x, 1 - y)]


def _all_gather(shards):
    n_arr = len(shards)
    per = 1 + 2 * OTHER_CHIPS

    def body(*refs):
        in_refs, out_refs = refs[:n_arr], refs[n_arr:2 * n_arr]
        send_sems, recv_sems, loc_sems = refs[2 * n_arr:]
        x, y, c, chips = _chip_peers()
        me, sibling = (x, y, c), (x, y, 1 - c)

        def slot(px, py, pc):
            return 4 * px + 2 * py + pc

        def copy(i, k, block, to, src=None):
            dst = out_refs[i].at[slot(*block)]
            return pltpu.make_async_remote_copy(src_ref=dst if src is None else src, dst_ref=dst,
                                                send_sem=send_sems.at[i * per + k], recv_sem=recv_sems.at[i * per + k],
                                                device_id=to, device_id_type=MESH)

        own = [pltpu.make_async_copy(in_refs[i], out_refs[i].at[slot(*me)], loc_sems.at[i]) for i in range(n_arr)]
        for cp in own:
            cp.start()
        first = []
        for i in range(n_arr):
            first += [copy(i, 1 + j, me, (*chip, c), src=in_refs[i]) for j, chip in enumerate(chips)]
            first.append(copy(i, 0, me, sibling, src=in_refs[i]))
        for cp in first:
            cp.start()
        passed = []
        for j, chip in enumerate(chips):
            for i in range(n_arr):
                copy(i, 1 + j, (*chip, c), me).wait_recv()
                fwd = copy(i, 1 + OTHER_CHIPS + j, (*chip, c), sibling)
                fwd.start()
                passed.append(fwd)
        for i in range(n_arr):
            copy(i, 0, sibling, me).wait_recv()
            for j, chip in enumerate(chips):
                copy(i, 1 + OTHER_CHIPS + j, (*chip, 1 - c), me).wait_recv()
        for cp in first + passed:
            cp.wait_send()
        for cp in own:
            cp.wait()

    any_spec = pl.BlockSpec(memory_space=pl.ANY)
    return pl.pallas_call(
        body, name="weights_all_gather",
        out_shape=tuple(jax.ShapeDtypeStruct((N_DEV,) + a.shape, a.dtype) for a in shards),
        in_specs=[any_spec] * n_arr, out_specs=tuple([any_spec] * n_arr),
        scratch_shapes=[pltpu.SemaphoreType.DMA((n_arr * per,)), pltpu.SemaphoreType.DMA((n_arr * per,)),
                        pltpu.SemaphoreType.DMA((n_arr,))],
    )(*shards)


def _sibling_exchange(slabs):
    n_arr = len(slabs)

    def body(*refs):
        in_refs, out_refs = refs[:n_arr], refs[n_arr:2 * n_arr]
        send_sems, recv_sems = refs[2 * n_arr:]
        x, y, c, _ = _chip_peers()
        sends = []
        for i in range(n_arr):
            for q in range(N_CHIPS):
                cp = pltpu.make_async_remote_copy(src_ref=in_refs[i].at[2 * q + (1 - c)], dst_ref=out_refs[i].at[q],
                                                  send_sem=send_sems.at[i * N_CHIPS + q],
                                                  recv_sem=recv_sems.at[i * N_CHIPS + q],
                                                  device_id=(x, y, 1 - c), device_id_type=MESH)
                cp.start()
                sends.append(cp)
        for cp in sends:
            cp.wait_recv()
        for cp in sends:
            cp.wait_send()

    any_spec = pl.BlockSpec(memory_space=pl.ANY)
    return pl.pallas_call(
        body, name="grads_sibling_exchange",
        out_shape=tuple(jax.ShapeDtypeStruct((N_CHIPS,) + a.shape[1:], a.dtype) for a in slabs),
        in_specs=[any_spec] * n_arr, out_specs=tuple([any_spec] * n_arr),
        scratch_shapes=[pltpu.SemaphoreType.DMA((n_arr * N_CHIPS,)), pltpu.SemaphoreType.DMA((n_arr * N_CHIPS,))],
    )(*slabs)


def _pair_sum(slabs, from_sibling, core, name):
    _, rows, cols = slabs.shape
    tr = _row_block(rows, max(SUBLANES, (256 * 1024) // cols // SUBLANES * SUBLANES))

    def body(core_ref, a_ref, b_ref, o_ref):
        o_ref[...] = (a_ref[...] + b_ref[...]).astype(BF16)

    grid_spec = pltpu.PrefetchScalarGridSpec(
        num_scalar_prefetch=1, grid=(N_CHIPS, rows // tr),
        in_specs=[pl.BlockSpec((None, tr, cols), lambda q, r, core_ref: (2 * q + core_ref[0], r, 0)),
                  pl.BlockSpec((None, tr, cols), lambda q, r, core_ref: (q, r, 0))],
        out_specs=pl.BlockSpec((None, tr, cols), lambda q, r, core_ref: (q, r, 0)))
    return pl.pallas_call(
        body, name=name, grid_spec=grid_spec,
        out_shape=jax.ShapeDtypeStruct((N_CHIPS, rows, cols), BF16),
        compiler_params=_params("parallel", "parallel"),
    )(core, slabs, from_sibling)


def _chip_exchange(partials):
    n_arr = len(partials)

    def body(*refs):
        in_refs, out_refs = refs[:n_arr], refs[n_arr:2 * n_arr]
        send_sems, recv_sems, loc_sems = refs[2 * n_arr:]
        x, y, c, chips = _chip_peers()
        mine = 2 * x + y
        own = [pltpu.make_async_copy(in_refs[i].at[mine], out_refs[i].at[mine], loc_sems.at[i]) for i in range(n_arr)]
        for cp in own:
            cp.start()

        def copy(i, j, chip, src_slot, dst_slot):
            return pltpu.make_async_remote_copy(src_ref=in_refs[i].at[src_slot], dst_ref=out_refs[i].at[dst_slot],
                                                send_sem=send_sems.at[i * OTHER_CHIPS + j],
                                                recv_sem=recv_sems.at[i * OTHER_CHIPS + j],
                                                device_id=(*chip, c), device_id_type=MESH)

        sends = [copy(i, j, chip, 2 * chip[0] + chip[1], mine) for j, chip in enumerate(chips) for i in range(n_arr)]
        for cp in sends:
            cp.start()
        for j, chip in enumerate(chips):
            for i in range(n_arr):
                copy(i, j, chip, mine, 2 * chip[0] + chip[1]).wait_recv()
        for cp in sends:
            cp.wait_send()
        for cp in own:
            cp.wait()

    any_spec = pl.BlockSpec(memory_space=pl.ANY)
    return pl.pallas_call(
        body, name="grads_chip_exchange",
        out_shape=tuple(jax.ShapeDtypeStruct(a.shape, a.dtype) for a in partials),
        in_specs=[any_spec] * n_arr, out_specs=tuple([any_spec] * n_arr),
        scratch_shapes=[pltpu.SemaphoreType.DMA((n_arr * OTHER_CHIPS,)), pltpu.SemaphoreType.DMA((n_arr * OTHER_CHIPS,)),
                        pltpu.SemaphoreType.DMA((n_arr,))],
    )(*partials)


def _small_all_reduce(part):
    rows = part.shape[0]

    def body(p_ref, o_ref, buf_ref, send_sems, recv_sems):
        me, peers = _mesh_peers()
        buf_ref[me] = p_ref[...]
        sends = []
        for k, (dev, pid) in enumerate(peers):
            cp = pltpu.make_async_remote_copy(src_ref=p_ref, dst_ref=buf_ref.at[me], send_sem=send_sems.at[k],
                                              recv_sem=recv_sems.at[k], device_id=dev, device_id_type=MESH)
            cp.start()
            sends.append(cp)
        for k, (dev, pid) in enumerate(peers):
            pltpu.make_async_remote_copy(src_ref=p_ref, dst_ref=buf_ref.at[pid], send_sem=send_sems.at[k],
                                         recv_sem=recv_sems.at[k], device_id=dev, device_id_type=MESH).wait_recv()
        for cp in sends:
            cp.wait_send()
        acc = buf_ref[0]
        for i in range(1, N_DEV):
            acc = acc + buf_ref[i]
        o_ref[...] = acc

    vmem = pl.BlockSpec(memory_space=pltpu.VMEM)
    return pl.pallas_call(
        body, name="small_all_reduce",
        out_shape=jax.ShapeDtypeStruct(part.shape, F32),
        in_specs=[vmem], out_specs=vmem,
        scratch_shapes=[pltpu.VMEM((N_DEV, rows, LANES), F32), pltpu.SemaphoreType.DMA((N_DEV - 1,)),
                        pltpu.SemaphoreType.DMA((N_DEV - 1,))],
    )(part)


def _adamw_vals(w, g, m, v):
    m = ADAM_B1 * m + (1.0 - ADAM_B1) * g
    v = ADAM_B2 * v + (1.0 - ADAM_B2) * (g * g)
    m_hat = m / (1.0 - ADAM_B1 ** ADAM_STEP)
    v_hat = v / (1.0 - ADAM_B2 ** ADAM_STEP)
    delta = -ADAM_LR * (m_hat / (jnp.sqrt(v_hat) + ADAM_EPS) + ADAM_WD * w)
    return delta, m, v


def _adamw(contrib, w, m, v, name):
    n, rows, cols = contrib.shape
    tr = _row_block(rows, max(SUBLANES, (128 * 1024) // cols // SUBLANES * SUBLANES))

    def body(c_ref, w_ref, m_ref, v_ref, g_ref, d_ref, nm_ref, nv_ref):
        g = c_ref[0].astype(F32)
        for i in range(1, n):
            g = g + c_ref[i].astype(F32)
        delta, nm, nv = _adamw_vals(w_ref[...], g, m_ref[...], v_ref[...])
        g_ref[...] = g
        d_ref[...] = delta
        nm_ref[...] = nm
        nv_ref[...] = nv

    row = pl.BlockSpec((tr, cols), lambda i: (i, 0))
    shp = jax.ShapeDtypeStruct((rows, cols), F32)
    return pl.pallas_call(
        body, name=name, grid=(rows // tr,),
        in_specs=[pl.BlockSpec((n, tr, cols), lambda i: (0, i, 0)), row, row, row],
        out_specs=(row, row, row, row), out_shape=(shp, shp, shp, shp),
        compiler_params=_params("parallel"),
    )(contrib, w, m, v)


def _lane_block(n, cap):
    if n <= cap:
        return n
    best = None
    for cand in range(LANES, cap + 1, LANES):
        if n % cand == 0:
            best = cand
    assert best is not None, n
    return best


def _matmul(a, b, out_dtype, name, mode="nn", tm=1024, tn=1024, tk=1024):
    g = a.shape[0]
    m, k = (a.shape[2], a.shape[1]) if mode == "tn" else (a.shape[1], a.shape[2])
    n = b.shape[1] if mode == "nt" else b.shape[2]
    tm, tn, tk = _lane_block(m, tm), _lane_block(n, tn), _lane_block(k, tk)
    nk = k // tk
    a_spec = (pl.BlockSpec((None, tk, tm), lambda gi, i, j, kk: (gi, kk, i)) if mode == "tn" else
              pl.BlockSpec((None, tm, tk), lambda gi, i, j, kk: (gi, i, kk)))
    b_spec = (pl.BlockSpec((None, tn, tk), lambda gi, i, j, kk: (gi, j, kk)) if mode == "nt" else
              pl.BlockSpec((None, tk, tn), lambda gi, i, j, kk: (gi, kk, j)))
    ca, cb = (0 if mode == "tn" else 1), (1 if mode == "nt" else 0)

    def body(a_ref, b_ref, o_ref, *acc):
        part = _dg(a_ref[...], b_ref[...], ca, cb)
        if nk == 1:
            o_ref[...] = part.astype(o_ref.dtype)
            return
        acc_ref, = acc
        kk = pl.program_id(3)

        @pl.when(kk == 0)
        def _():
            acc_ref[...] = part

        @pl.when((kk > 0) & (kk < nk - 1))
        def _():
            acc_ref[...] += part

        @pl.when(kk == nk - 1)
        def _():
            o_ref[...] = (acc_ref[...] + part).astype(o_ref.dtype)

    return pl.pallas_call(
        body, name=name, grid=(g, m // tm, n // tn, nk),
        in_specs=[a_spec, b_spec],
        out_specs=pl.BlockSpec((None, tm, tn), lambda gi, i, j, kk: (gi, i, j)),
        out_shape=jax.ShapeDtypeStruct((g, m, n), out_dtype),
        scratch_shapes=[] if nk == 1 else [pltpu.VMEM((tm, tn), F32)],
        compiler_params=_params("parallel", "parallel", "parallel", "arbitrary"),
    )(a, b)


def _rms_fwd(x, w):
    s, d = x.shape
    tm = _row_block(s, 512)

    def body(x_ref, w_ref, h_ref):
        h_ref[...] = _rmsnorm(x_ref[...], w_ref[...]).astype(BF16)

    return pl.pallas_call(
        body, name="input_rmsnorm", grid=(s // tm,),
        in_specs=[pl.BlockSpec((tm, d), lambda i: (i, 0)), pl.BlockSpec((1, d), lambda i: (0, 0))],
        out_specs=pl.BlockSpec((tm, d), lambda i: (i, 0)),
        out_shape=jax.ShapeDtypeStruct((s, d), BF16),
        compiler_params=_params("parallel"),
    )(x, w)


def _rms_bwd(x, w, dh_stacked, dh_parts, dx_res):
    s, d = x.shape
    tm = _row_block(s, 256)
    n_parts = 1 + len(dh_parts)

    def body(x_ref, w_ref, *rest):
        part_refs = rest[:n_parts]
        res_ref, gx_ref, gw_ref = rest[n_parts:]
        dh = part_refs[0][...]
        for r in part_refs[1:]:
            dh = dh + r[...]
        _, vjp = jax.vjp(_rmsnorm, x_ref[...], w_ref[...])
        dx, dw = vjp(dh)
        gx_ref[...] = dx + res_ref[...]

        @pl.when(pl.program_id(0) == 0)
        def _():
            gw_ref[...] = jnp.zeros_like(gw_ref)

        gw_ref[...] += dw

    row = pl.BlockSpec((tm, d), lambda i: (i, 0))
    vec = pl.BlockSpec((1, d), lambda i: (0, 0))
    return pl.pallas_call(
        body, name="input_rmsnorm_bwd", grid=(s // tm,),
        in_specs=[row, vec, pl.BlockSpec((None, tm, d), lambda i: (0, i, 0))] + [row] * (n_parts - 1) + [row],
        out_specs=(row, vec),
        out_shape=(jax.ShapeDtypeStruct((s, d), F32), jax.ShapeDtypeStruct((1, d), F32)),
        compiler_params=_params("arbitrary"),
    )(x, w, dh_stacked, *dh_parts, dx_res)


def _lane_masks(rows):
    lane = lax.broadcasted_iota(jnp.int32, (rows, LANES), 1)
    return lane < HEAD_DIM, (lane & (HEAD_DIM - 1)) < HEAD_DIM // 2


def _swap_halves(t, lo_half):
    return jnp.where(lo_half, pltpu.roll(t, LANES - HEAD_DIM // 2, 1), pltpu.roll(t, HEAD_DIM // 2, 1))


def _rope(t, cos, sin_signed, lo_half):
    return t * cos + _swap_halves(t, lo_half) * sin_signed


def _rope_bwd(d, cos, sin_signed, lo_half):
    return d * cos - _swap_halves(d, lo_half) * sin_signed


def _window_mask(first):
    qi = lax.broadcasted_iota(jnp.int32, (2 * ATT_BLOCK, 2 * ATT_BLOCK), 0) & (ATT_BLOCK - 1)
    kj = lax.broadcasted_iota(jnp.int32, (2 * ATT_BLOCK, 2 * ATT_BLOCK), 1)
    dist = qi + ATT_BLOCK - kj
    return (dist >= 0) & (dist <= N_BACK) & ((kj >= ATT_BLOCK) | jnp.logical_not(first))


def _stack_heads(t, head0):
    zero = jnp.zeros_like(t)
    return jnp.concatenate([jnp.where(head0, t, zero), jnp.where(head0, zero, t)], axis=0)


def _unstack_heads(t2, head0):
    return jnp.where(head0, t2[:ATT_BLOCK], t2[ATT_BLOCK:])


def _blocks_per_subsequence(g, nb):
    return lax.shift_right_logical(jnp.int32(nb), 2 * g)


def _attn_fwd(qkv, cos, sin):
    _, s, _ = qkv.shape
    nb = s // ATT_BLOCK

    def body(qkv_ref, cos_ref, sin_ref, o_ref, lse_ref, kp_ref, vp_ref):
        g, t = pl.program_id(0), pl.program_id(1)
        first = (t & (_blocks_per_subsequence(g, nb) - 1)) == 0

        @pl.when(first)
        def _():
            kp_ref[...] = jnp.zeros_like(kp_ref)
            vp_ref[...] = jnp.zeros_like(vp_ref)

        cos_b, sin_b = cos_ref[...], sin_ref[...]
        head0, lo_half = _lane_masks(ATT_BLOCK)
        valid = _window_mask(first)
        for sl in range(WIDTH // LANES):
            cq = pl.ds(sl * LANES, LANES)
            ck = pl.ds(WIDTH + sl * LANES, LANES)
            cv = pl.ds(2 * WIDTH + sl * LANES, LANES)
            qr = (_rope(qkv_ref[:, cq], cos_b, sin_b, lo_half) * (HEAD_DIM ** -0.5)).astype(BF16)
            kr = _rope(qkv_ref[:, ck], cos_b, sin_b, lo_half).astype(BF16)
            v16 = qkv_ref[:, cv].astype(BF16)
            kcat = jnp.concatenate([kp_ref[:, cq], kr], axis=0)
            vcat = jnp.concatenate([vp_ref[:, cq], v16], axis=0)
            sc = jnp.where(valid, _dg(_stack_heads(qr, head0), kcat, 1, 1), -jnp.inf)
            mx = jnp.max(sc, axis=1, keepdims=True)
            p = jnp.exp(sc - mx)
            den = jnp.sum(p, axis=1, keepdims=True)
            o_ref[:, cq] = _unstack_heads(_dg((p * (1.0 / den)).astype(BF16), vcat, 1, 0), head0)
            lse2 = mx + jnp.log(den)
            lse_ref[:, cq] = jnp.where(head0, lse2[:ATT_BLOCK], lse2[ATT_BLOCK:])
            kp_ref[:, cq] = kr
            vp_ref[:, cq] = v16

    blk = lambda w: pl.BlockSpec((None, ATT_BLOCK, w), lambda g, t: (g, t, 0))
    shp = jax.ShapeDtypeStruct((GROUPS, s, WIDTH), F32)
    return pl.pallas_call(
        body, name="dilated_attention_fwd", grid=(GROUPS, nb),
        in_specs=[blk(3 * WIDTH), blk(LANES), blk(LANES)],
        out_specs=(blk(WIDTH), blk(WIDTH)), out_shape=(shp, shp),
        scratch_shapes=[pltpu.VMEM((ATT_BLOCK, WIDTH), BF16), pltpu.VMEM((ATT_BLOCK, WIDTH), BF16)],
        compiler_params=_params("arbitrary", "arbitrary"),
    )(qkv, cos, sin)


def _attn_bwd(qkv, cos, sin, o, lse, do, dlse):
    _, s, _ = qkv.shape
    nb = s // ATT_BLOCK

    def body(qkv_ref, cos_ref, sin_ref, cosp_ref, sinp_ref, o_ref, lse_ref, do_ref, dlse_ref,
             dqkv_ref, kp_ref, vp_ref, dka_ref, dva_ref, dqp_ref):
        g, t = pl.program_id(0), pl.program_id(1)
        first = (t & (_blocks_per_subsequence(g, nb) - 1)) == 0
        active = t < nb
        head0, lo_half = _lane_masks(ATT_BLOCK)
        cos_p, sin_p = cosp_ref[...], sinp_ref[...]

        @pl.when(t == 0)
        def _():
            dka_ref[...] = jnp.zeros_like(dka_ref)
            dva_ref[...] = jnp.zeros_like(dva_ref)
            dqp_ref[...] = jnp.zeros_like(dqp_ref)

        dqkv_ref[:, pl.ds(0, WIDTH)] = dqp_ref[...]

        @pl.when(active & first)
        def _():
            kp_ref[...] = jnp.zeros_like(kp_ref)
            vp_ref[...] = jnp.zeros_like(vp_ref)

        @pl.when(active)
        def _():
            cos_b, sin_b = cos_ref[...], sin_ref[...]
            valid = _window_mask(first)
            for sl in range(WIDTH // LANES):
                cq = pl.ds(sl * LANES, LANES)
                ck = pl.ds(WIDTH + sl * LANES, LANES)
                cv = pl.ds(2 * WIDTH + sl * LANES, LANES)
                qr = (_rope(qkv_ref[:, cq], cos_b, sin_b, lo_half) * (HEAD_DIM ** -0.5)).astype(BF16)
                kr = _rope(qkv_ref[:, ck], cos_b, sin_b, lo_half).astype(BF16)
                v16 = qkv_ref[:, cv].astype(BF16)
                kcat = jnp.concatenate([kp_ref[:, cq], kr], axis=0)
                vcat = jnp.concatenate([vp_ref[:, cq], v16], axis=0)
                do_b = do_ref[:, cq]
                do16 = do_b.astype(BF16)
                lse_b = lse_ref[:, cq]
                cterm = dlse_ref[:, cq] - do_b * o_ref[:, cq]
                dqs, dkc, dvc = [], None, None
                for hm in (head0, jnp.logical_not(head0)):
                    qm = jnp.where(hm, qr, jnp.zeros_like(qr))
                    dom = jnp.where(hm, do16, jnp.zeros_like(do16))
                    sc = jnp.where(valid[:ATT_BLOCK], _dg(qm, kcat, 1, 1), -jnp.inf)
                    lse_h = jnp.max(jnp.where(hm, lse_b, -jnp.inf), axis=1, keepdims=True)
                    c = jnp.sum(jnp.where(hm, cterm, 0.0), axis=1, keepdims=True)
                    p = jnp.exp(sc - lse_h)
                    ds16 = (p * (_dg(dom, vcat, 1, 1) + c)).astype(BF16)
                    dv_h, dk_h = _dg(p.astype(BF16), dom, 0, 0), _dg(ds16, qm, 0, 0)
                    dvc = dv_h if dvc is None else dvc + dv_h
                    dkc = dk_h if dkc is None else dkc + dk_h
                    dqs.append(_dg(ds16, kcat, 1, 0))
                dq = jnp.where(head0, dqs[0], dqs[1]) * (HEAD_DIM ** -0.5)
                dqp_ref[:, cq] = _rope_bwd(dq, cos_b, sin_b, lo_half).astype(BF16)
                dqkv_ref[:, ck] = _rope_bwd(dka_ref[:, cq] + dkc[:ATT_BLOCK], cos_p, sin_p, lo_half).astype(BF16)
                dqkv_ref[:, cv] = (dva_ref[:, cq] + dvc[:ATT_BLOCK]).astype(BF16)
                dka_ref[:, cq] = dkc[ATT_BLOCK:]
                dva_ref[:, cq] = dvc[ATT_BLOCK:]
                kp_ref[:, cq] = kr
                vp_ref[:, cq] = v16

        @pl.when(jnp.logical_not(active))
        def _():
            for sl in range(WIDTH // LANES):
                cq = pl.ds(sl * LANES, LANES)
                dqkv_ref[:, pl.ds(WIDTH + sl * LANES, LANES)] = _rope_bwd(dka_ref[:, cq], cos_p, sin_p, lo_half).astype(BF16)
                dqkv_ref[:, pl.ds(2 * WIDTH + sl * LANES, LANES)] = dva_ref[:, cq].astype(BF16)

    cur = lambda w: pl.BlockSpec((None, ATT_BLOCK, w), lambda g, t: (g, jnp.minimum(t, nb - 1), 0))
    prev = lambda w: pl.BlockSpec((None, ATT_BLOCK, w), lambda g, t: (g, jnp.maximum(t - 1, 0), 0))
    return pl.pallas_call(
        body, name="dilated_attention_bwd", grid=(GROUPS, nb + 1),
        in_specs=[cur(3 * WIDTH), cur(LANES), cur(LANES), prev(LANES), prev(LANES),
                  cur(WIDTH), cur(WIDTH), cur(WIDTH), cur(WIDTH)],
        out_specs=prev(3 * WIDTH), out_shape=jax.ShapeDtypeStruct((GROUPS, s, 3 * WIDTH), BF16),
        scratch_shapes=[pltpu.VMEM((ATT_BLOCK, WIDTH), BF16), pltpu.VMEM((ATT_BLOCK, WIDTH), BF16),
                        pltpu.VMEM((ATT_BLOCK, WIDTH), F32), pltpu.VMEM((ATT_BLOCK, WIDTH), F32),
                        pltpu.VMEM((ATT_BLOCK, WIDTH), BF16)],
        compiler_params=_params("arbitrary", "arbitrary"),
    )(qkv, cos, sin, cos, sin, o, lse, do, dlse)


CONV_PAD = SUBLANES


def _gdn_post(y, is_q, is_k):
    head0, _ = _lane_masks(y.shape[0])
    c = _silu(y)
    sq = c * c
    ss0 = jnp.sum(jnp.where(head0, sq, 0.0), axis=1, keepdims=True)
    ss1 = jnp.sum(jnp.where(head0, 0.0, sq), axis=1, keepdims=True)
    r = jnp.where(head0, lax.rsqrt(ss0 + NORM_EPS), lax.rsqrt(ss1 + NORM_EPS))
    scale = jnp.where(is_q, HEAD_DIM ** -0.5, 1.0).astype(F32)
    return jnp.where(is_q | is_k, c * r * scale, c)


def _conv_rows(xp_ref, w, c0, rows):
    y = w[0:1, :] * xp_ref[pl.ds(c0 + CONV_PAD - (CONV_K - 1), rows), :]
    for k in range(1, CONV_K):
        y = y + w[k:k + 1, :] * xp_ref[pl.ds(c0 + CONV_PAD - (CONV_K - 1) + k, rows), :]
    return y


def _gdn_pre_fwd(proj_r, conv8, col0):
    s = proj_r.shape[0]
    tr = _row_block(s, 512)
    nblk = QKV_B // LANES
    nq = WIDTH // LANES

    def body(x_ref, w_ref, out_ref, xp_ref):
        j = pl.program_id(0)
        is_q, is_k = j < nq, (j >= nq) & (j < 2 * nq)
        xp_ref[pl.ds(0, CONV_PAD), :] = jnp.zeros((CONV_PAD, LANES), F32)
        xp_ref[pl.ds(CONV_PAD, s), :] = x_ref[...]
        w = w_ref[...]
        for c in range(s // tr):
            out_ref[pl.ds(c * tr, tr), :] = _gdn_post(_conv_rows(xp_ref, w, c * tr, tr), is_q, is_k)

    return pl.pallas_call(
        body, name="gdn_conv_fwd", grid=(nblk,),
        in_specs=[pl.BlockSpec((s, LANES), lambda j: (0, col0 + j)), pl.BlockSpec((SUBLANES, LANES), lambda j: (0, j))],
        out_specs=pl.BlockSpec((s, LANES), lambda j: (0, j)),
        out_shape=jax.ShapeDtypeStruct((s, QKV_B), F32),
        scratch_shapes=[pltpu.VMEM((s + CONV_PAD, LANES), F32)],
        compiler_params=_params("parallel"),
    )(proj_r, conv8)


def _gdn_pre_bwd(proj_r, conv8, dc, col0):
    s = proj_r.shape[0]
    tr = _row_block(s, 512)
    nblk = QKV_B // LANES
    nq = WIDTH // LANES

    def body(x_ref, w_ref, dc_ref, dx_ref, dw_ref, xp_ref, dyp_ref):
        j = pl.program_id(0)
        is_q, is_k = j < nq, (j >= nq) & (j < 2 * nq)
        xp_ref[pl.ds(0, CONV_PAD), :] = jnp.zeros((CONV_PAD, LANES), F32)
        xp_ref[pl.ds(CONV_PAD, s), :] = x_ref[...]
        dyp_ref[pl.ds(s, CONV_PAD), :] = jnp.zeros((CONV_PAD, LANES), F32)
        w = w_ref[...]
        for c in range(s // tr):
            y = _conv_rows(xp_ref, w, c * tr, tr)
            _, vjp = jax.vjp(lambda yy: _gdn_post(yy, is_q, is_k), y)
            dyp_ref[pl.ds(c * tr, tr), :] = vjp(dc_ref[pl.ds(c * tr, tr), :])[0]
        dws = [jnp.zeros((1, LANES), F32) for _ in range(CONV_K)]
        for c in range(s // tr):
            c0 = c * tr
            dy = dyp_ref[pl.ds(c0, tr), :]
            dx = w[0:1, :] * dyp_ref[pl.ds(c0 + CONV_K - 1, tr), :]
            for k in range(1, CONV_K):
                dx = dx + w[k:k + 1, :] * dyp_ref[pl.ds(c0 + CONV_K - 1 - k, tr), :]
            dx_ref[pl.ds(c0, tr), :] = dx.astype(BF16)
            for k in range(CONV_K):
                xs = xp_ref[pl.ds(c0 + CONV_PAD - (CONV_K - 1) + k, tr), :]
                dws[k] = dws[k] + jnp.sum(dy * xs, axis=0, keepdims=True)
        row = lax.broadcasted_iota(jnp.int32, (SUBLANES, LANES), 0)
        dwb = jnp.zeros((SUBLANES, LANES), F32)
        for k in range(CONV_K):
            dwb = dwb + jnp.where(row == k, dws[k], 0.0)
        dw_ref[...] = dwb

    return pl.pallas_call(
        body, name="gdn_conv_bwd", grid=(nblk,),
        in_specs=[pl.BlockSpec((s, LANES), lambda j: (0, col0 + j)), pl.BlockSpec((SUBLANES, LANES), lambda j: (0, j)),
                  pl.BlockSpec((s, LANES), lambda j: (0, j))],
        out_specs=(pl.BlockSpec((s, LANES), lambda j: (0, j)), pl.BlockSpec((SUBLANES, LANES), lambda j: (0, j))),
        out_shape=(jax.ShapeDtypeStruct((s, QKV_B), BF16), jax.ShapeDtypeStruct((SUBLANES, QKV_B), F32)),
        scratch_shapes=[pltpu.VMEM((s + CONV_PAD, LANES), F32), pltpu.VMEM((s + CONV_PAD, LANES), F32)],
        compiler_params=_params("parallel"),
    )(proj_r, conv8, dc)


def _gdn_chunk(q, k, v, bcol, acol, alog, dtb, gnw, state, t_saved=None):
    n = q.shape[-2]
    shp = (1, n, n)
    row = lax.broadcasted_iota(jnp.int32, shp, 1)
    col = lax.broadcasted_iota(jnp.int32, shp, 2)
    beta = _sigmoid(bcol)
    g = -jnp.exp(alog) * _softplus(acol + dtb)
    g_row = jnp.sum(jnp.where(row == col, g, 0.0), axis=-2, keepdims=True)
    big_g = jnp.sum(jnp.where(row >= col, g_row, 0.0), axis=-1, keepdims=True)
    big_g_row = jnp.sum(jnp.where(row <= col, g, 0.0), axis=-2, keepdims=True)
    decay_incl = jnp.exp(jnp.where(row >= col, big_g - big_g_row, -jnp.inf))
    decay_strict = jnp.where(row > col, decay_incl, 0.0)
    k_beta = k * beta
    a_mat = _mm_nt(k_beta, k) * decay_strict
    t_inv = _tri_inv(a_mat) if t_saved is None else _tri_inv_saved(a_mat, t_saved)
    e_g = jnp.exp(big_g)
    u = _mm(t_inv, v * beta)
    w = _mm(t_inv, k_beta * e_g)
    attn = _mm_nt(q, k) * decay_incl
    v_new = u - _mm(w, state)
    o = _mm(q * e_g, state) + _mm(attn, v_new)
    total = jnp.sum(g, axis=-2, keepdims=True)
    new_state = state * jnp.exp(total) + _mm_tn(k * jnp.exp(total - big_g), v_new)
    return _rmsnorm(o, gnw), new_state, t_inv


def _split_heads(x):
    return jnp.stack([x[:, h * HEAD_DIM:(h + 1) * HEAD_DIM] for h in range(HEADS)], axis=0)


def _merge_heads(x):
    return jnp.concatenate([x[h] for h in range(HEADS)], axis=1)


def _logit_columns(ba):
    lane = lax.broadcasted_iota(jnp.int32, ba.shape, 1)

    def cols(off):
        return jnp.stack([jnp.sum(jnp.where(lane == off + h, ba, 0.0), axis=1, keepdims=True) for h in range(HEADS)], axis=0)

    return cols(0), cols(HEADS)


def _logit_block(dbc, dac, shape):
    lane = lax.broadcasted_iota(jnp.int32, shape, 1)
    out = jnp.zeros(shape, F32)
    for h in range(HEADS):
        out = out + jnp.where(lane == h, dbc[h], 0.0) + jnp.where(lane == HEADS + h, dac[h], 0.0)
    return out


def _gdn_scan_fwd(cqkv, proj_r, ba_col, alog, dtb, gnw):
    s = cqkv.shape[0]
    nc = s // CHUNK

    def body(q_ref, k_ref, v_ref, ba_ref, al_ref, dt_ref, gnw_ref, o_ref, st_ref, ti_ref, state_ref):
        @pl.when(pl.program_id(0) == 0)
        def _():
            state_ref[...] = jnp.zeros_like(state_ref)

        st = state_ref[...]
        st_ref[...] = st
        bcol, acol = _logit_columns(ba_ref[...])
        o, new_st, t_inv = _gdn_chunk(_split_heads(q_ref[...]), _split_heads(k_ref[...]), _split_heads(v_ref[...]),
                                      bcol, acol, al_ref[...], dt_ref[...], gnw_ref[...], st)
        o_ref[...] = _merge_heads(o)
        ti_ref[...] = t_inv
        state_ref[...] = new_st

    part = lambda i: pl.BlockSpec((CHUNK, WIDTH), lambda n: (n, i))
    par = pl.BlockSpec((HEADS, 1, 1), lambda n: (0, 0, 0))
    per_chunk = pl.BlockSpec((None, HEADS, HEAD_DIM, HEAD_DIM), lambda n: (n, 0, 0, 0))
    per_chunk_shape = jax.ShapeDtypeStruct((nc, HEADS, HEAD_DIM, HEAD_DIM), F32)
    return pl.pallas_call(
        body, name="gdn_scan_fwd", grid=(nc,),
        in_specs=[part(0), part(1), part(2), pl.BlockSpec((CHUNK, LANES), lambda n: (n, ba_col)), par, par,
                  pl.BlockSpec((1, 1, HEAD_DIM), lambda n: (0, 0, 0))],
        out_specs=(part(0), per_chunk, per_chunk),
        out_shape=(jax.ShapeDtypeStruct((s, WIDTH), F32), per_chunk_shape, per_chunk_shape),
        scratch_shapes=[pltpu.VMEM((HEADS, HEAD_DIM, HEAD_DIM), F32)],
        compiler_params=_params("arbitrary"),
    )(cqkv, cqkv, cqkv, proj_r, alog, dtb, gnw)


def _gdn_scan_bwd(cqkv, proj_r, ba_col, alog, dtb, gnw, states, t_invs, do):
    s = cqkv.shape[0]
    nc = s // CHUNK

    def body(q_ref, k_ref, v_ref, ba_ref, al_ref, dt_ref, gnw_ref, st_ref, ti_ref, do_ref,
             dqkv_ref, dba_ref, dal_ref, ddt_ref, dgnw_ref, dstate_ref):
        @pl.when(pl.program_id(0) == 0)
        def _():
            dstate_ref[...] = jnp.zeros_like(dstate_ref)
            dal_ref[...] = jnp.zeros_like(dal_ref)
            ddt_ref[...] = jnp.zeros_like(ddt_ref)
            dgnw_ref[...] = jnp.zeros_like(dgnw_ref)

        bcol, acol = _logit_columns(ba_ref[...])
        t_saved = ti_ref[...]
        _, vjp = jax.vjp(lambda *a: _gdn_chunk(*a, t_saved=t_saved)[:2],
                         _split_heads(q_ref[...]), _split_heads(k_ref[...]), _split_heads(v_ref[...]),
                         bcol, acol, al_ref[...], dt_ref[...], gnw_ref[...], st_ref[...])
        dq, dk, dv, dbc, dac, dal, ddt, dgn, dst = vjp((_split_heads(do_ref[...]), dstate_ref[...]))
        dqkv_ref[:, pl.ds(0, WIDTH)] = _merge_heads(dq)
        dqkv_ref[:, pl.ds(WIDTH, WIDTH)] = _merge_heads(dk)
        dqkv_ref[:, pl.ds(2 * WIDTH, WIDTH)] = _merge_heads(dv)
        dba_ref[...] = _logit_block(dbc, dac, dba_ref.shape)
        dstate_ref[...] = dst
        dal_ref[...] += dal
        ddt_ref[...] += ddt
        dgnw_ref[...] += dgn

    rev = lambda n: nc - 1 - n
    part = lambda i: pl.BlockSpec((CHUNK, WIDTH), lambda n: (rev(n), i))
    par = pl.BlockSpec((HEADS, 1, 1), lambda n: (0, 0, 0))
    vec = pl.BlockSpec((1, 1, HEAD_DIM), lambda n: (0, 0, 0))
    par_shape = jax.ShapeDtypeStruct((HEADS, 1, 1), F32)
    per_chunk = pl.BlockSpec((None, HEADS, HEAD_DIM, HEAD_DIM), lambda n: (rev(n), 0, 0, 0))
    return pl.pallas_call(
        body, name="gdn_scan_bwd", grid=(nc,),
        in_specs=[part(0), part(1), part(2), pl.BlockSpec((CHUNK, LANES), lambda n: (rev(n), ba_col)), par, par, vec,
                  per_chunk, per_chunk, part(0)],
        out_specs=(pl.BlockSpec((CHUNK, QKV_B), lambda n: (rev(n), 0)), pl.BlockSpec((CHUNK, LANES), lambda n: (rev(n), 0)),
                   par, par, vec),
        out_shape=(jax.ShapeDtypeStruct((s, QKV_B), F32), jax.ShapeDtypeStruct((s, LANES), F32), par_shape, par_shape,
                   jax.ShapeDtypeStruct((1, 1, HEAD_DIM), F32)),
        scratch_shapes=[pltpu.VMEM((HEADS, HEAD_DIM, HEAD_DIM), F32)],
        compiler_params=_params("arbitrary"),
    )(cqkv, cqkv, cqkv, proj_r, alog, dtb, gnw, states, t_invs, do)


def _tail_loss(x, tgt, o0, o1, o2, l0, l1, l2, ga, gb, za, zb, ob, fnw, wua, wub, wo, tap_a, tap_b, tap_o):
    lm = jnp.maximum(jnp.maximum(l0, l1), l2)
    e0, e1, e2 = jnp.exp(l0 - lm), jnp.exp(l1 - lm), jnp.exp(l2 - lm)
    o_a = (e0 * o0 + e1 * o1 + e2 * o2) / (e0 + e1 + e2)
    y_a = _mm_tap(o_a * _silu(za), wua, tap_a)
    y_b = _mm_tap(ob * _silu(zb), wub, tap_b)
    merged = _sigmoid(ga) * y_a + _sigmoid(gb) * y_b
    y = _rmsnorm(x + _mm_tap(merged, wo, tap_o), fnw)
    err = y - tgt
    per_token = jnp.sum(err * err, axis=1, keepdims=True) * (0.5 / x.shape[1])
    return jnp.sum(per_token, axis=0, keepdims=True)


def _tail(x, tgt, o_all, lse_all, og12, lg12, proj_r, ob, wua, wub, wo, fnw):
    s, d = x.shape
    tm = _row_block(s, 128)
    col_za = 2 * d // WIDTH
    col_zb = (2 * d + WIDTH + QKV_B) // WIDTH

    def body(x_ref, t_ref, o0_ref, o1_ref, o2_ref, l0_ref, l1_ref, l2_ref, ga_ref, gb_ref, za_ref, zb_ref, ob_ref,
             wua_ref, wub_ref, wo_ref, fnw_ref,
             loss_ref, dx_ref, do0_ref, do1_ref, do2_ref, dl0_ref, dl1_ref, dl2_ref, dga_ref, dgb_ref, dza_ref,
             dzb_ref, dob_ref, dwua_ref, dwub_ref, dwo_ref, dfnw_ref):
        @pl.when(pl.program_id(0) == 0)
        def _():
            for r in (loss_ref, dwua_ref, dwub_ref, dwo_ref, dfnw_ref):
                r[...] = jnp.zeros_like(r)

        args = (x_ref[...], t_ref[...], o0_ref[...], o1_ref[...], o2_ref[...], l0_ref[...], l1_ref[...], l2_ref[...],
                ga_ref[...], gb_ref[...], za_ref[...], zb_ref[...], ob_ref[...], fnw_ref[...],
                wua_ref[...], wub_ref[...], wo_ref[...],
                jnp.zeros(wua_ref.shape, F32), jnp.zeros(wub_ref.shape, F32), jnp.zeros(wo_ref.shape, F32))
        loss, vjp = jax.vjp(_tail_loss, *args)
        (dx, _, do0, do1, do2, dl0, dl1, dl2, dga, dgb, dza, dzb, dob, dfnw, _, _, _, dwua, dwub, dwo) = vjp(
            jnp.ones((1, 1), F32))
        loss_ref[...] += jnp.broadcast_to(loss, loss_ref.shape)
        dx_ref[...] = dx
        do0_ref[...], do1_ref[...], do2_ref[...] = do0, do1, do2
        dl0_ref[...], dl1_ref[...], dl2_ref[...] = dl0, dl1, dl2
        dga_ref[...] = dga.astype(BF16)
        dgb_ref[...] = dgb.astype(BF16)
        dza_ref[...] = dza.astype(BF16)
        dzb_ref[...] = dzb.astype(BF16)
        dob_ref[...] = dob
        dwua_ref[...] += dwua
        dwub_ref[...] += dwub
        dwo_ref[...] += dwo
        dfnw_ref[...] += dfnw

    row = lambda w, c=0: pl.BlockSpec((tm, w), lambda i: (i, c))
    grp0 = pl.BlockSpec((None, tm, WIDTH), lambda i: (0, i, 0))
    full = lambda a, b: pl.BlockSpec((a, b), lambda i: (0, 0))
    f32 = lambda a, b: jax.ShapeDtypeStruct((a, b), F32)
    b16 = lambda a, b: jax.ShapeDtypeStruct((a, b), BF16)
    stacked = jax.ShapeDtypeStruct((GROUPS, s, WIDTH), F32)
    gspecs = [grp0, row(WIDTH), row(WIDTH)]
    in_specs = ([row(d), row(d)] + gspecs * 2 + [row(d, 0), row(d, 1), row(WIDTH, col_za), row(WIDTH, col_zb),
                row(WIDTH), full(WIDTH, d), full(WIDTH, d), full(d, d), full(1, d)])
    out_specs = ([full(SUBLANES, LANES), row(d)] + gspecs * 2 + [row(d), row(d), row(WIDTH), row(WIDTH), row(WIDTH),
                 full(WIDTH, d), full(WIDTH, d), full(d, d), full(1, d)])
    gshapes = [stacked, f32(s, WIDTH), f32(s, WIDTH)]
    out_shape = ([f32(SUBLANES, LANES), f32(s, d)] + gshapes * 2 + [b16(s, d), b16(s, d), b16(s, WIDTH),
                 b16(s, WIDTH), f32(s, WIDTH), f32(WIDTH, d), f32(WIDTH, d), f32(d, d), f32(1, d)])
    return pl.pallas_call(
        body, name="tail_fwd_bwd", grid=(s // tm,),
        in_specs=in_specs, out_specs=tuple(out_specs), out_shape=tuple(out_shape),
        compiler_params=_params("arbitrary"),
    )(x, tgt, o_all, og12[0], og12[1], lse_all, lg12[0], lg12[1], proj_r, proj_r, proj_r, proj_r, ob, wua, wub, wo, fnw)


def _from_dilated_rows(stacked, g, dil, name):
    n_slots, s, c = stacked.shape
    view = stacked.reshape(n_slots, dil, s // dil, c)

    def body(in_ref, out_ref):
        for r in range(dil):
            out_ref[pl.ds(r, ATT_BLOCK, stride=dil), :] = in_ref[r]

    return pl.pallas_call(
        body, name=name, grid=(s // (ATT_BLOCK * dil), c // LANES),
        in_specs=[pl.BlockSpec((None, dil, ATT_BLOCK, LANES), lambda n, j: (g, 0, n, j))],
        out_specs=pl.BlockSpec((ATT_BLOCK * dil, LANES), lambda n, j: (n, j)),
        out_shape=jax.ShapeDtypeStruct((s, c), stacked.dtype),
        compiler_params=_params("parallel", "parallel"),
    )(view)


def _to_dilated_rows_into(nat, stacked, g, dil, name):
    n_slots, s, c = stacked.shape
    view = stacked.reshape(n_slots, dil, s // dil, c)

    def body(nat_ref, old_ref, out_ref):
        for r in range(dil):
            out_ref[r] = nat_ref[pl.ds(r, ATT_BLOCK, stride=dil), :]

    out = pl.pallas_call(
        body, name=name, grid=(s // (ATT_BLOCK * dil), c // LANES),
        in_specs=[pl.BlockSpec((ATT_BLOCK * dil, LANES), lambda n, j: (n, j)), pl.BlockSpec(memory_space=pl.ANY)],
        out_specs=pl.BlockSpec((None, dil, ATT_BLOCK, LANES), lambda n, j: (g, 0, n, j)),
        out_shape=jax.ShapeDtypeStruct(view.shape, stacked.dtype),
        input_output_aliases={1: 0},
        compiler_params=_params("parallel", "parallel"),
    )(nat, view)
    return out.reshape(stacked.shape)


def _to_dilated(a, dil):
    if dil == 1:
        return a
    s = a.shape[0]
    return a.reshape(s // dil, dil, -1).transpose(1, 0, 2).reshape(a.shape)


def _from_dilated(a, dil):
    if dil == 1:
        return a
    s = a.shape[0]
    return a.reshape(dil, s // dil, -1).transpose(1, 0, 2).reshape(a.shape)


def _head_major(a):
    return a.reshape(a.shape[0], HEADS, HEAD_DIM).transpose(1, 0, 2)


def _from_head_major(a):
    return a.transpose(1, 0, 2).reshape(a.shape[1], WIDTH)


def _rope_tables(s):
    inv_freq = ROPE_THETA ** (-jnp.arange(0, HEAD_DIM, 2, dtype=F32) / HEAD_DIM)
    ang = jnp.arange(s, dtype=F32)[:, None] * inv_freq[None, :]
    cos_n = jnp.tile(jnp.cos(ang), (1, 2 * LANES // HEAD_DIM))
    sin_h = jnp.sin(ang)
    sin_n = jnp.tile(jnp.concatenate([-sin_h, sin_h], axis=1), (1, LANES // HEAD_DIM))
    return (jnp.stack([_to_dilated(cos_n, dil) for dil in DILATIONS]),
            jnp.stack([_to_dilated(sin_n, dil) for dil in DILATIONS]))


def _pack_rows(parts, dtype, row_multiple):
    flat = jnp.concatenate([p.reshape(-1).astype(dtype) for p in parts])
    tile = row_multiple * LANES
    pad = (-flat.shape[0]) % tile
    return jnp.pad(flat, (0, pad)).reshape(-1, LANES)


def _unpack_rows(packed, shapes):
    flat = packed.reshape(-1)
    out, start = [], 0
    for shp in shapes:
        size = 1
        for n in shp:
            size *= n
        out.append(flat[start:start + size].reshape(shp))
        start += size
    return out


def kernel(x, norm_w, w_in, conv_w, a_log, dt_bias, gdn_norm_w, w_up_a, w_up_b, w_out, final_norm_w, loss_target, m_norm_w, m_w_in, m_conv_w, m_a_log, m_dt_bias, m_gdn_norm_w, m_w_up_a, m_w_up_b, m_w_out, m_final_norm_w, v_norm_w, v_w_in, v_conv_w, v_a_log, v_dt_bias, v_gdn_norm_w, v_w_up_a, v_w_up_b, v_w_out, v_final_norm_w):
    x2, tgt = x[0], loss_target[0]
    s, d = x2.shape
    me = 4 * lax.axis_index("x") + 2 * lax.axis_index("y") + lax.axis_index("c")
    win8 = w_in.shape[2]
    conv8w = conv_w.shape[2]

    conv_shard = jnp.pad(conv_w[0], ((0, SUBLANES - CONV_K), (0, 0)))
    w_in_g, wua_g, wub_g, wo_g, conv_g = _all_gather(
        [w_in[0].astype(BF16), w_up_a[0].astype(BF16), w_up_b[0].astype(BF16), w_out[0].astype(BF16), conv_shard])
    w_in_f = jnp.concatenate([w_in_g[i] for i in range(N_DEV)], axis=1)
    wua = jnp.concatenate([wua_g[i] for i in range(N_DEV)], axis=1)
    wub = jnp.concatenate([wub_g[i] for i in range(N_DEV)], axis=1)
    wo = wo_g.reshape(d, d)
    conv8 = jnp.concatenate([conv_g[i] for i in range(N_DEV)], axis=1)

    w_qkv = w_in_f[:, :QKV_A].reshape(d, GROUPS, QKV_B).transpose(1, 0, 2)
    w_rest = jnp.concatenate([
        w_in_f[:, OFF_GATE:OFF_GATE + 2 * d], w_in_f[:, OFF_ZA:OFF_ZA + WIDTH], w_in_f[:, OFF_QKVB:OFF_QKVB + QKV_B],
        w_in_f[:, OFF_ZB:OFF_ZB + WIDTH], w_in_f[:, OFF_BA:OFF_BA + 2 * HEADS],
        jnp.zeros((d, BA_PAD - 2 * HEADS), BF16)], axis=1)
    col_qkvb = (2 * d + WIDTH) // LANES
    col_ba = (2 * d + 2 * WIDTH + QKV_B) // LANES

    h = _rms_fwd(x2, norm_w)
    h_all = jnp.stack([_to_dilated(h, dil) for dil in DILATIONS])
    qkv_all = _matmul(h_all, w_qkv, F32, "in_proj_attention")
    proj_r = _matmul(h[None], w_rest[None], F32, "in_proj_rest")[0]
    cos, sin = _rope_tables(s)
    o_all, lse_all = _attn_fwd(qkv_all, cos, sin)
    og12 = [_from_dilated_rows(o_all, g, DILATIONS[g], "attn_out_to_natural_%d" % g) for g in (1, 2)]
    lg12 = [_from_dilated_rows(lse_all, g, DILATIONS[g], "attn_lse_to_natural_%d" % g) for g in (1, 2)]

    cqkv = _gdn_pre_fwd(proj_r, conv8, col_qkvb)
    alog3, dtb3, gnw3 = a_log.reshape(HEADS, 1, 1), dt_bias.reshape(HEADS, 1, 1), gdn_norm_w.reshape(1, 1, HEAD_DIM)
    ob, states, t_invs = _gdn_scan_fwd(cqkv, proj_r, col_ba, alog3, dtb3, gnw3)

    (loss_blk, dx_res, do_all, do1, do2, dl_all, dl1, dl2, dga, dgb, dza, dzb, dob, dwua, dwub, dwo, dfnw) = _tail(
        x2, tgt, o_all, lse_all, og12, lg12, proj_r, ob, wua, wub, wo, final_norm_w.reshape(1, d))

    for g, (t_o, t_l) in ((1, (do1, dl1)), (2, (do2, dl2))):
        do_all = _to_dilated_rows_into(t_o, do_all, g, DILATIONS[g], "attn_dout_to_dilated_%d" % g)
        dl_all = _to_dilated_rows_into(t_l, dl_all, g, DILATIONS[g], "attn_dlse_to_dilated_%d" % g)
    dqkv_all = _attn_bwd(qkv_all, cos, sin, o_all, lse_all, do_all, dl_all)

    dcqkv, dba, dalog3, ddtb3, dgnw3 = _gdn_scan_bwd(cqkv, proj_r, col_ba, alog3, dtb3, gnw3, states, t_invs, dob)
    dqkv_b, dconv8 = _gdn_pre_bwd(proj_r, conv8, dcqkv, col_qkvb)
    dproj_r = jnp.concatenate([dga, dgb, dza, dqkv_b, dzb,
                               jnp.pad(dba.astype(BF16), ((0, 0), (0, BA_PAD - LANES)))], axis=1)

    dw_qkv = _matmul(h_all, dqkv_all, F32, "in_proj_attention_dw", mode="tn", tk=2048)
    dw_rest = _matmul(h[None], dproj_r[None], F32, "in_proj_rest_dw", mode="tn", tk=2048)[0]
    dh_a = _matmul(dqkv_all, w_qkv, F32, "in_proj_attention_dh", mode="nt", tk=2048)
    dh_r = _matmul(dproj_r[None], w_rest[None], F32, "in_proj_rest_dh", mode="nt")[0]
    dh_parts = [dh_r] + [_from_dilated_rows(dh_a, g, DILATIONS[g], "dh_to_natural_%d" % g) for g in (1, 2)]
    grad_x, dnorm_w = _rms_bwd(x2, norm_w, dh_a, dh_parts, dx_res)

    o2 = 2 * d
    dw_in = jnp.concatenate([
        dw_qkv.transpose(1, 0, 2).reshape(d, QKV_A),
        dw_rest[:, o2:o2 + WIDTH], dw_rest[:, o2 + WIDTH:o2 + WIDTH + QKV_B],
        dw_rest[:, o2 + WIDTH + QKV_B:o2 + 2 * WIDTH + QKV_B],
        dw_rest[:, o2 + 2 * WIDTH + QKV_B:o2 + 2 * WIDTH + QKV_B + 2 * HEADS],
        dw_rest[:, :o2]], axis=1)

    def col_slabs(a, width):
        return jnp.stack([a[:, j * width:(j + 1) * width] for j in range(N_DEV)])

    slabs = [col_slabs(dw_in, win8), col_slabs(dwua, d // N_DEV), col_slabs(dwub, d // N_DEV),
             dwo.reshape(N_DEV, d // N_DEV, d)]
    from_sibling = _sibling_exchange(slabs)
    core = lax.axis_index("c").astype(jnp.int32).reshape(1)
    partials = [_pair_sum(a, b, core, "grads_pair_sum_%d" % i) for i, (a, b) in enumerate(zip(slabs, from_sibling))]
    contrib = _chip_exchange(partials)

    small_parts = [dnorm_w, dfnw, dconv8[:CONV_K], dalog3[:, 0, 0], ddtb3[:, 0, 0], dgnw3[0], loss_blk[0, 0:1]]
    small_rows = [-(-p.size // LANES) for p in small_parts]
    small = jnp.concatenate([jnp.pad(p.reshape(-1), (0, r * LANES - p.size)).reshape(r, LANES)
                             for p, r in zip(small_parts, small_rows)])
    small = jnp.pad(small, ((0, (-small.shape[0]) % SUBLANES), (0, 0)))
    small_sum = _small_all_reduce(small)
    pieces, r0 = [], 0
    for p, r in zip(small_parts, small_rows):
        pieces.append(small_sum[r0:r0 + r].reshape(-1)[:p.size].reshape(p.shape))
        r0 += r
    g_norm_w, g_fnw, g_conv_full, g_alog, g_dtb, g_gnw, loss_sum = pieces
    g_conv = lax.dynamic_slice(g_conv_full, (0, me * conv8w), (CONV_K, conv8w))

    big = [_adamw(c, w[0], m[0], v[0], name) for c, w, m, v, name in (
        (contrib[0], w_in, m_w_in, v_w_in, "adamw_w_in"), (contrib[1], w_up_a, m_w_up_a, v_w_up_a, "adamw_w_up_a"),
        (contrib[2], w_up_b, m_w_up_b, v_w_up_b, "adamw_w_up_b"), (contrib[3], w_out, m_w_out, v_w_out, "adamw_w_out"))]
    g_big, d_big, nm_big, nv_big = ([t[i] for t in big] for i in range(4))

    small_ws = [norm_w, final_norm_w, conv_w, a_log, dt_bias, gdn_norm_w]
    small_ms = [m_norm_w, m_final_norm_w, m_conv_w, m_a_log, m_dt_bias, m_gdn_norm_w]
    small_vs = [v_norm_w, v_final_norm_w, v_conv_w, v_a_log, v_dt_bias, v_gdn_norm_w]
    small_gs = [g_norm_w, g_fnw, g_conv, g_alog, g_dtb, g_gnw]
    small_shapes = [t.shape for t in small_ws]
    sm = _adamw(_pack_rows(small_gs, F32, SUBLANES)[None], _pack_rows(small_ws, F32, SUBLANES),
                _pack_rows(small_ms, F32, SUBLANES), _pack_rows(small_vs, F32, SUBLANES), "adamw_small")
    g_sm, d_sm, nm_sm, nv_sm = (_unpack_rows(t, small_shapes) for t in sm)

    def ordered(bigs, smalls):
        nw, fnw_, cw, al, dtb, gn = smalls
        wi, ua, ub, wo_ = (t[None] for t in bigs)
        return [nw, wi, cw, al, dtb, gn, ua, ub, wo_, fnw_]

    return (loss_sum.reshape(()), grad_x[None], *ordered(g_big, g_sm), *ordered(d_big, d_sm),
            *ordered(nm_big, nm_sm), *ordered(nv_big, nv_sm))
```

```python
import functools

import jax
import jax.numpy as jnp
from jax import lax
from jax.experimental import pallas as pl
from jax.experimental.pallas import tpu as pltpu

F32 = jnp.float32
BF16 = jnp.bfloat16
MESH = pl.DeviceIdType.MESH
N_DEV = 8
LANES = 128
SUBLANES = 8

GROUPS = 3
HEADS = 8
HEAD_DIM = 64
WIDTH = HEADS * HEAD_DIM
ATT_BLOCK = 128
DILATIONS = (1, 4, 16)
N_BACK = 128
CONV_K = 4
CHUNK = 64
QKV_B = 3 * WIDTH
QKV_A = GROUPS * 3 * WIDTH
BA_PAD = 512
NORM_EPS = 1e-6
ROPE_THETA = 10000.0
ADAM_LR, ADAM_B1, ADAM_B2, ADAM_EPS, ADAM_WD, ADAM_STEP = 0.001, 0.9, 0.999, 1e-08, 0.01, 10

VMEM_LIMIT = 56 * 1024 * 1024

OFF_ZA = QKV_A
OFF_QKVB = OFF_ZA + WIDTH
OFF_ZB = OFF_QKVB + QKV_B
OFF_BA = OFF_ZB + WIDTH
OFF_GATE = OFF_BA + 2 * HEADS


def _params(*sem):
    return pltpu.CompilerParams(dimension_semantics=sem, vmem_limit_bytes=VMEM_LIMIT)


def _dg(a, b, ca, cb):
    nb = a.ndim - 2
    batch = tuple(range(nb))
    return lax.dot_general(a, b, (((nb + ca,), (nb + cb,)), (batch, batch)), preferred_element_type=F32)


@jax.custom_vjp
def _mm(a, b):
    return _dg(a.astype(BF16), b.astype(BF16), 1, 0)


def _mm_fwd(a, b):
    return _mm(a, b), (a.astype(BF16), b.astype(BF16))


def _mm_bwd(res, ct):
    a16, b16 = res
    c16 = ct.astype(BF16)
    return _dg(c16, b16, 1, 1), _dg(a16, c16, 0, 0)


_mm.defvjp(_mm_fwd, _mm_bwd)


@jax.custom_vjp
def _mm_nt(a, b):
    return _dg(a.astype(BF16), b.astype(BF16), 1, 1)


def _mm_nt_fwd(a, b):
    return _mm_nt(a, b), (a.astype(BF16), b.astype(BF16))


def _mm_nt_bwd(res, ct):
    a16, b16 = res
    c16 = ct.astype(BF16)
    return _dg(c16, b16, 1, 0), _dg(c16, a16, 0, 0)


_mm_nt.defvjp(_mm_nt_fwd, _mm_nt_bwd)


@jax.custom_vjp
def _mm_tn(a, b):
    return _dg(a.astype(BF16), b.astype(BF16), 0, 0)


def _mm_tn_fwd(a, b):
    return _mm_tn(a, b), (a.astype(BF16), b.astype(BF16))


def _mm_tn_bwd(res, ct):
    a16, b16 = res
    c16 = ct.astype(BF16)
    return _dg(b16, c16, 1, 1), _dg(a16, c16, 1, 0)


_mm_tn.defvjp(_mm_tn_fwd, _mm_tn_bwd)


@jax.custom_vjp
def _mm_tap(a, w16, tap):
    return _dg(a.astype(BF16), w16, 1, 0)


def _mm_tap_fwd(a, w16, tap):
    return _mm_tap(a, w16, tap), (a.astype(BF16), w16)


def _mm_tap_bwd(res, ct):
    a16, w16 = res
    c16 = ct.astype(BF16)
    return _dg(c16, w16, 1, 1), jnp.zeros_like(w16), _dg(a16, c16, 0, 0)


_mm_tap.defvjp(_mm_tap_fwd, _mm_tap_bwd)


def _split16(a):
    hi = a.astype(BF16)
    lo = (a - hi.astype(F32)).astype(BF16)
    return hi, lo


def _dot3(a, b, ca, cb):
    ah, al = _split16(a)
    bh, bl = _split16(b)
    return _dg(ah, bh, ca, cb) + (_dg(ah, bl, ca, cb) + _dg(al, bh, ca, cb))


def _tri_inv_impl(a):
    n = a.shape[-1]
    shp = (1,) * (a.ndim - 2) + (n, n)
    eye = (lax.broadcasted_iota(jnp.int32, shp, a.ndim - 2) == lax.broadcasted_iota(jnp.int32, shp, a.ndim - 1)).astype(F32)
    x = eye - a
    p = a
    for it in range(5):
        dot = _dot3 if it < 2 else (lambda u, v, cu, cv: _dg(u.astype(BF16), v.astype(BF16), cu, cv))
        p = dot(p, p, 1, 0)
        x = x + dot(x, p, 1, 0)
    return x


@jax.custom_vjp
def _tri_inv(a):
    return _tri_inv_impl(a)


def _tri_inv_fwd(a):
    t = _tri_inv_impl(a)
    return t, t


def _tri_inv_bwd(t, ct):
    t16 = t.astype(BF16)
    return (-_dg(_dg(t16, ct.astype(BF16), 0, 0).astype(BF16), t16, 1, 1),)


_tri_inv.defvjp(_tri_inv_fwd, _tri_inv_bwd)


@jax.custom_vjp
def _tri_inv_saved(a, t):
    return t


def _tri_inv_saved_fwd(a, t):
    return t, t


def _tri_inv_saved_bwd(t, ct):
    return _tri_inv_bwd(t, ct) + (jnp.zeros_like(t),)


_tri_inv_saved.defvjp(_tri_inv_saved_fwd, _tri_inv_saved_bwd)


def _sigmoid(x):
    return 1.0 / (1.0 + jnp.exp(-x))


def _silu(x):
    return x * _sigmoid(x)


def _softplus(x):
    return jnp.maximum(x, 0.0) + jnp.log(1.0 + jnp.exp(-jnp.abs(x)))


def _rmsnorm(x, w):
    return x * lax.rsqrt(jnp.mean(x * x, axis=-1, keepdims=True) + NORM_EPS) * w


def _row_block(rows, cap):
    best = None
    for cand in range(SUBLANES, min(rows, cap) + 1, SUBLANES):
        if rows % cand == 0:
            best = cand
    assert best is not None, rows
    return best


def _mesh_peers():
    x, y, c = lax.axis_index("x"), lax.axis_index("y"), lax.axis_index("c")
    me = 4 * x + 2 * y + c
    peers = []
    for k in range(1, N_DEV):
        px = 1 - x if (k >> 2) & 1 else x
        py = 1 - y if (k >> 1) & 1 else y
        pc = 1 - c if k & 1 else c
        peers.append(((px, py, pc), 4 * px + 2 * py + pc))
    return me, peers


N_CHIPS = 4
OTHER_CHIPS = 3


def _chip_peers():
    x, y, c = lax.axis_index("x"), lax.axis_index("y"), lax.axis_index("c")
    return x, y, c, [(1 - x, y), (x, 1 - y), (1 - x, 1 - y)]


def _all_gather(shards):
    n_arr = len(shards)
    per = 1 + 2 * OTHER_CHIPS

    def body(*refs):
        in_refs, out_refs = refs[:n_arr], refs[n_arr:2 * n_arr]
        send_sems, recv_sems, loc_sems = refs[2 * n_arr:]
        x, y, c, chips = _chip_peers()
        me, sibling = (x, y, c), (x, y, 1 - c)

        def slot(px, py, pc):
            return 4 * px + 2 * py + pc

        def copy(i, k, block, to, src=None):
            dst = out_refs[i].at[slot(*block)]
            return pltpu.make_async_remote_copy(src_ref=dst if src is None else src, dst_ref=dst,
                                                send_sem=send_sems.at[i * per + k], recv_sem=recv_sems.at[i * per + k],
                                                device_id=to, device_id_type=MESH)

        own = [pltpu.make_async_copy(in_refs[i], out_refs[i].at[slot(*me)], loc_sems.at[i]) for i in range(n_arr)]
        for cp in own:
            cp.start()
        first = []
        for i in range(n_arr):
            first += [copy(i, 1 + j, me, (*chip, c), src=in_refs[i]) for j, chip in enumerate(chips)]
            first.append(copy(i, 0, me, sibling, src=in_refs[i]))
        for cp in first:
            cp.start()
        passed = []
        for j, chip in enumerate(chips):
            for i in range(n_arr):
                copy(i, 1 + j, (*chip, c), me).wait_recv()
                fwd = copy(i, 1 + OTHER_CHIPS + j, (*chip, c), sibling)
                fwd.start()
                passed.append(fwd)
        for i in range(n_arr):
            copy(i, 0, sibling, me).wait_recv()
            for j, chip in enumerate(chips):
                copy(i, 1 + OTHER_CHIPS + j, (*chip, 1 - c), me).wait_recv()
        for cp in first + passed:
            cp.wait_send()
        for cp in own:
            cp.wait()

    any_spec = pl.BlockSpec(memory_space=pl.ANY)
    return pl.pallas_call(
        body, name="weights_all_gather",
        out_shape=tuple(jax.ShapeDtypeStruct((N_DEV,) + a.shape, a.dtype) for a in shards),
        in_specs=[any_spec] * n_arr, out_specs=tuple([any_spec] * n_arr),
        scratch_shapes=[pltpu.SemaphoreType.DMA((n_arr * per,)), pltpu.SemaphoreType.DMA((n_arr * per,)),
                        pltpu.SemaphoreType.DMA((n_arr,))],
    )(*shards)


def _sibling_exchange(slabs):
    n_arr = len(slabs)

    def body(*refs):
        in_refs, out_refs = refs[:n_arr], refs[n_arr:2 * n_arr]
        send_sems, recv_sems = refs[2 * n_arr:]
        x, y, c, _ = _chip_peers()
        sends = []
        for i in range(n_arr):
            for q in range(N_CHIPS):
                cp = pltpu.make_async_remote_copy(src_ref=in_refs[i].at[2 * q + (1 - c)], dst_ref=out_refs[i].at[q],
                                                  send_sem=send_sems.at[i * N_CHIPS + q],
                                                  recv_sem=recv_sems.at[i * N_CHIPS + q],
                                                  device_id=(x, y, 1 - c), device_id_type=MESH)
                cp.start()
                sends.append(cp)
        for cp in sends:
            cp.wait_recv()
        for cp in sends:
            cp.wait_send()

    any_spec = pl.BlockSpec(memory_space=pl.ANY)
    return pl.pallas_call(
        body, name="grads_sibling_exchange",
        out_shape=tuple(jax.ShapeDtypeStruct((N_CHIPS,) + a.shape[1:], a.dtype) for a in slabs),
        in_specs=[any_spec] * n_arr, out_specs=tuple([any_spec] * n_arr),
        scratch_shapes=[pltpu.SemaphoreType.DMA((n_arr * N_CHIPS,)), pltpu.SemaphoreType.DMA((n_arr * N_CHIPS,))],
    )(*slabs)


def _pair_sum(slabs, from_sibling, core, name):
    _, rows, cols = slabs.shape
    tr = _row_block(rows, max(SUBLANES, (256 * 1024) // cols // SUBLANES * SUBLANES))

    def body(core_ref, a_ref, b_ref, o_ref):
        o_ref[...] = (a_ref[...] + b_ref[...]).astype(BF16)

    grid_spec = pltpu.PrefetchScalarGridSpec(
        num_scalar_prefetch=1, grid=(N_CHIPS, rows // tr),
        in_specs=[pl.BlockSpec((None, tr, cols), lambda q, r, core_ref: (2 * q + core_ref[0], r, 0)),
                  pl.BlockSpec((None, tr, cols), lambda q, r, core_ref: (q, r, 0))],
        out_specs=pl.BlockSpec((None, tr, cols), lambda q, r, core_ref: (q, r, 0)))
    return pl.pallas_call(
        body, name=name, grid_spec=grid_spec,
        out_shape=jax.ShapeDtypeStruct((N_CHIPS, rows, cols), BF16),
        compiler_params=_params("parallel", "parallel"),
    )(core, slabs, from_sibling)


def _chip_exchange(partials):
    n_arr = len(partials)

    def body(*refs):
        in_refs, out_refs = refs[:n_arr], refs[n_arr:2 * n_arr]
        send_sems, recv_sems, loc_sems = refs[2 * n_arr:]
        x, y, c, chips = _chip_peers()
        mine = 2 * x + y
        own = [pltpu.make_async_copy(in_refs[i].at[mine], out_refs[i].at[mine], loc_sems.at[i]) for i in range(n_arr)]
        for cp in own:
            cp.start()

        def copy(i, j, chip, src_slot, dst_slot):
            return pltpu.make_async_remote_copy(src_ref=in_refs[i].at[src_slot], dst_ref=out_refs[i].at[dst_slot],
                                                send_sem=send_sems.at[i * OTHER_CHIPS + j],
                                                recv_sem=recv_sems.at[i * OTHER_CHIPS + j],
                                                device_id=(*chip, c), device_id_type=MESH)

        sends = [copy(i, j, chip, 2 * chip[0] + chip[1], mine) for j, chip in enumerate(chips) for i in range(n_arr)]
        for cp in sends:
            cp.start()
        for j, chip in enumerate(chips):
            for i in range(n_arr):
                copy(i, j, chip, mine, 2 * chip[0] + chip[1]).wait_recv()
        for cp in sends:
            cp.wait_send()
        for cp in own:
            cp.wait()

    any_spec = pl.BlockSpec(memory_space=pl.ANY)
    return pl.pallas_call(
        body, name="grads_chip_exchange",
        out_shape=tuple(jax.ShapeDtypeStruct(a.shape, a.dtype) for a in partials),
        in_specs=[any_spec] * n_arr, out_specs=tuple([any_spec] * n_arr),
        scratch_shapes=[pltpu.SemaphoreType.DMA((n_arr * OTHER_CHIPS,)), pltpu.SemaphoreType.DMA((n_arr * OTHER_CHIPS,)),
                        pltpu.SemaphoreType.DMA((n_arr,))],
    )(*partials)


def _small_all_reduce(part):
    rows = part.shape[0]

    def body(p_ref, o_ref, buf_ref, send_sems, recv_sems):
        me, peers = _mesh_peers()
        buf_ref[me] = p_ref[...]
        sends = []
        for k, (dev, pid) in enumerate(peers):
            cp = pltpu.make_async_remote_copy(src_ref=p_ref, dst_ref=buf_ref.at[me], send_sem=send_sems.at[k],
                                              recv_sem=recv_sems.at[k], device_id=dev, device_id_type=MESH)
            cp.start()
            sends.append(cp)
        for k, (dev, pid) in enumerate(peers):
            pltpu.make_async_remote_copy(src_ref=p_ref, dst_ref=buf_ref.at[pid], send_sem=send_sems.at[k],
                                         recv_sem=recv_sems.at[k], device_id=dev, device_id_type=MESH).wait_recv()
        for cp in sends:
            cp.wait_send()
        acc = buf_ref[0]
        for i in range(1, N_DEV):
            acc = acc + buf_ref[i]
        o_ref[...] = acc

    vmem = pl.BlockSpec(memory_space=pltpu.VMEM)
    return pl.pallas_call(
        body, name="small_all_reduce",
        out_shape=jax.ShapeDtypeStruct(part.shape, F32),
        in_specs=[vmem], out_specs=vmem,
        scratch_shapes=[pltpu.VMEM((N_DEV, rows, LANES), F32), pltpu.SemaphoreType.DMA((N_DEV - 1,)),
                        pltpu.SemaphoreType.DMA((N_DEV - 1,))],
    )(part)


def _adamw_vals(w, g, m, v):
    m = ADAM_B1 * m + (1.0 - ADAM_B1) * g
    v = ADAM_B2 * v + (1.0 - ADAM_B2) * (g * g)
    m_hat = m / (1.0 - ADAM_B1 ** ADAM_STEP)
    v_hat = v / (1.0 - ADAM_B2 ** ADAM_STEP)
    delta = -ADAM_LR * (m_hat / (jnp.sqrt(v_hat) + ADAM_EPS) + ADAM_WD * w)
    return delta, m, v


def _adamw(contrib, w, m, v, name):
    n, rows, cols = contrib.shape
    tr = _row_block(rows, max(SUBLANES, (128 * 1024) // cols // SUBLANES * SUBLANES))

    def body(c_ref, w_ref, m_ref, v_ref, g_ref, d_ref, nm_ref, nv_ref):
        g = c_ref[0].astype(F32)
        for i in range(1, n):
            g = g + c_ref[i].astype(F32)
        delta, nm, nv = _adamw_vals(w_ref[...], g, m_ref[...], v_ref[...])
        g_ref[...] = g
        d_ref[...] = delta
        nm_ref[...] = nm
        nv_ref[...] = nv

    row = pl.BlockSpec((tr, cols), lambda i: (i, 0))
    shp = jax.ShapeDtypeStruct((rows, cols), F32)
    return pl.pallas_call(
        body, name=name, grid=(rows // tr,),
        in_specs=[pl.BlockSpec((n, tr, cols), lambda i: (0, i, 0)), row, row, row],
        out_specs=(row, row, row, row), out_shape=(shp, shp, shp, shp),
        compiler_params=_params("parallel"),
    )(contrib, w, m, v)


def _lane_block(n, cap):
    if n <= cap:
        return n
    best = None
    for cand in range(LANES, cap + 1, LANES):
        if n % cand == 0:
            best = cand
    assert best is not None, n
    return best


def _matmul(a, b, out_dtype, name, mode="nn", tm=1024, tn=1024, tk=1024):
    g = a.shape[0]
    m, k = (a.shape[2], a.shape[1]) if mode == "tn" else (a.shape[1], a.shape[2])
    n = b.shape[1] if mode == "nt" else b.shape[2]
    tm, tn, tk = _lane_block(m, tm), _lane_block(n, tn), _lane_block(k, tk)
    nk = k // tk
    a_spec = (pl.BlockSpec((None, tk, tm), lambda gi, i, j, kk: (gi, kk, i)) if mode == "tn" else
              pl.BlockSpec((None, tm, tk), lambda gi, i, j, kk: (gi, i, kk)))
    b_spec = (pl.BlockSpec((None, tn, tk), lambda gi, i, j, kk: (gi, j, kk)) if mode == "nt" else
              pl.BlockSpec((None, tk, tn), lambda gi, i, j, kk: (gi, kk, j)))
    ca, cb = (0 if mode == "tn" else 1), (1 if mode == "nt" else 0)

    def body(a_ref, b_ref, o_ref, *acc):
        part = _dg(a_ref[...], b_ref[...], ca, cb)
        if nk == 1:
            o_ref[...] = part.astype(o_ref.dtype)
            return
        acc_ref, = acc
        kk = pl.program_id(3)

        @pl.when(kk == 0)
        def _():
            acc_ref[...] = part

        @pl.when((kk > 0) & (kk < nk - 1))
        def _():
            acc_ref[...] += part

        @pl.when(kk == nk - 1)
        def _():
            o_ref[...] = (acc_ref[...] + part).astype(o_ref.dtype)

    return pl.pallas_call(
        body, name=name, grid=(g, m // tm, n // tn, nk),
        in_specs=[a_spec, b_spec],
        out_specs=pl.BlockSpec((None, tm, tn), lambda gi, i, j, kk: (gi, i, j)),
        out_shape=jax.ShapeDtypeStruct((g, m, n), out_dtype),
        scratch_shapes=[] if nk == 1 else [pltpu.VMEM((tm, tn), F32)],
        compiler_params=_params("parallel", "parallel", "parallel", "arbitrary"),
    )(a, b)


def _rms_fwd(x, w):
    s, d = x.shape
    tm = _row_block(s, 512)

    def body(x_ref, w_ref, h_ref):
        h_ref[...] = _rmsnorm(x_ref[...], w_ref[...]).astype(BF16)

    return pl.pallas_call(
        body, name="input_rmsnorm", grid=(s // tm,),
        in_specs=[pl.BlockSpec((tm, d), lambda i: (i, 0)), pl.BlockSpec((1, d), lambda i: (0, 0))],
        out_specs=pl.BlockSpec((tm, d), lambda i: (i, 0)),
        out_shape=jax.ShapeDtypeStruct((s, d), BF16),
        compiler_params=_params("parallel"),
    )(x, w)


def _rms_bwd(x, w, dh_stacked, dh_parts, dx_res):
    s, d = x.shape
    tm = _row_block(s, 256)
    n_parts = 1 + len(dh_parts)

    def body(x_ref, w_ref, *rest):
        part_refs = rest[:n_parts]
        res_ref, gx_ref, gw_ref = rest[n_parts:]
        dh = part_refs[0][...]
        for r in part_refs[1:]:
            dh = dh + r[...]
        _, vjp = jax.vjp(_rmsnorm, x_ref[...], w_ref[...])
        dx, dw = vjp(dh)
        gx_ref[...] = dx + res_ref[...]

        @pl.when(pl.program_id(0) == 0)
        def _():
            gw_ref[...] = jnp.zeros_like(gw_ref)

        gw_ref[...] += dw

    row = pl.BlockSpec((tm, d), lambda i: (i, 0))
    vec = pl.BlockSpec((1, d), lambda i: (0, 0))
    return pl.pallas_call(
        body, name="input_rmsnorm_bwd", grid=(s // tm,),
        in_specs=[row, vec, pl.BlockSpec((None, tm, d), lambda i: (0, i, 0))] + [row] * (n_parts - 1) + [row],
        out_specs=(row, vec),
        out_shape=(jax.ShapeDtypeStruct((s, d), F32), jax.ShapeDtypeStruct((1, d), F32)),
        compiler_params=_params("arbitrary"),
    )(x, w, dh_stacked, *dh_parts, dx_res)


def _lane_masks(rows):
    lane = lax.broadcasted_iota(jnp.int32, (rows, LANES), 1)
    return lane < HEAD_DIM, (lane & (HEAD_DIM - 1)) < HEAD_DIM // 2


def _swap_halves(t, lo_half):
    return jnp.where(lo_half, pltpu.roll(t, LANES - HEAD_DIM // 2, 1), pltpu.roll(t, HEAD_DIM // 2, 1))


def _rope(t, cos, sin_signed, lo_half):
    return t * cos + _swap_halves(t, lo_half) * sin_signed


def _rope_bwd(d, cos, sin_signed, lo_half):
    return d * cos - _swap_halves(d, lo_half) * sin_signed


def _window_mask(first):
    qi = lax.broadcasted_iota(jnp.int32, (2 * ATT_BLOCK, 2 * ATT_BLOCK), 0) & (ATT_BLOCK - 1)
    kj = lax.broadcasted_iota(jnp.int32, (2 * ATT_BLOCK, 2 * ATT_BLOCK), 1)
    dist = qi + ATT_BLOCK - kj
    return (dist >= 0) & (dist <= N_BACK) & ((kj >= ATT_BLOCK) | jnp.logical_not(first))


def _stack_heads(t, head0):
    zero = jnp.zeros_like(t)
    return jnp.concatenate([jnp.where(head0, t, zero), jnp.where(head0, zero, t)], axis=0)


def _unstack_heads(t2, head0):
    return jnp.where(head0, t2[:ATT_BLOCK], t2[ATT_BLOCK:])


def _blocks_per_subsequence(g, nb):
    return lax.shift_right_logical(jnp.int32(nb), 2 * g)


def _attn_fwd(qkv, cos, sin):
    _, s, _ = qkv.shape
    nb = s // ATT_BLOCK

    def body(qkv_ref, cos_ref, sin_ref, o_ref, lse_ref, kp_ref, vp_ref):
        g, t = pl.program_id(0), pl.program_id(1)
        first = (t & (_blocks_per_subsequence(g, nb) - 1)) == 0

        @pl.when(first)
        def _():
            kp_ref[...] = jnp.zeros_like(kp_ref)
            vp_ref[...] = jnp.zeros_like(vp_ref)

        cos_b, sin_b = cos_ref[...], sin_ref[...]
        head0, lo_half = _lane_masks(ATT_BLOCK)
        valid = _window_mask(first)
        for sl in range(WIDTH // LANES):
            cq = pl.ds(sl * LANES, LANES)
            ck = pl.ds(WIDTH + sl * LANES, LANES)
            cv = pl.ds(2 * WIDTH + sl * LANES, LANES)
            qr = (_rope(qkv_ref[:, cq], cos_b, sin_b, lo_half) * (HEAD_DIM ** -0.5)).astype(BF16)
            kr = _rope(qkv_ref[:, ck], cos_b, sin_b, lo_half).astype(BF16)
            v16 = qkv_ref[:, cv].astype(BF16)
            kcat = jnp.concatenate([kp_ref[:, cq], kr], axis=0)
            vcat = jnp.concatenate([vp_ref[:, cq], v16], axis=0)
            sc = jnp.where(valid, _dg(_stack_heads(qr, head0), kcat, 1, 1), -jnp.inf)
            mx = jnp.max(sc, axis=1, keepdims=True)
            p = jnp.exp(sc - mx)
            den = jnp.sum(p, axis=1, keepdims=True)
            o_ref[:, cq] = _unstack_heads(_dg((p * (1.0 / den)).astype(BF16), vcat, 1, 0), head0)
            lse2 = mx + jnp.log(den)
            lse_ref[:, cq] = jnp.where(head0, lse2[:ATT_BLOCK], lse2[ATT_BLOCK:])
            kp_ref[:, cq] = kr
            vp_ref[:, cq] = v16

    blk = lambda w: pl.BlockSpec((None, ATT_BLOCK, w), lambda g, t: (g, t, 0))
    shp = jax.ShapeDtypeStruct((GROUPS, s, WIDTH), F32)
    return pl.pallas_call(
        body, name="dilated_attention_fwd", grid=(GROUPS, nb),
        in_specs=[blk(3 * WIDTH), blk(LANES), blk(LANES)],
        out_specs=(blk(WIDTH), blk(WIDTH)), out_shape=(shp, shp),
        scratch_shapes=[pltpu.VMEM((ATT_BLOCK, WIDTH), BF16), pltpu.VMEM((ATT_BLOCK, WIDTH), BF16)],
        compiler_params=_params("arbitrary", "arbitrary"),
    )(qkv, cos, sin)


def _attn_bwd(qkv, cos, sin, o, lse, do, dlse):
    _, s, _ = qkv.shape
    nb = s // ATT_BLOCK

    def body(qkv_ref, cos_ref, sin_ref, cosp_ref, sinp_ref, o_ref, lse_ref, do_ref, dlse_ref,
             dqkv_ref, kp_ref, vp_ref, dka_ref, dva_ref, dqp_ref):
        g, t = pl.program_id(0), pl.program_id(1)
        first = (t & (_blocks_per_subsequence(g, nb) - 1)) == 0
        active = t < nb
        head0, lo_half = _lane_masks(ATT_BLOCK)
        cos_p, sin_p = cosp_ref[...], sinp_ref[...]

        @pl.when(t == 0)
        def _():
            dka_ref[...] = jnp.zeros_like(dka_ref)
            dva_ref[...] = jnp.zeros_like(dva_ref)
            dqp_ref[...] = jnp.zeros_like(dqp_ref)

        dqkv_ref[:, pl.ds(0, WIDTH)] = dqp_ref[...]

        @pl.when(active & first)
        def _():
            kp_ref[...] = jnp.zeros_like(kp_ref)
            vp_ref[...] = jnp.zeros_like(vp_ref)

        @pl.when(active)
        def _():
            cos_b, sin_b = cos_ref[...], sin_ref[...]
            valid = _window_mask(first)
            for sl in range(WIDTH // LANES):
                cq = pl.ds(sl * LANES, LANES)
                ck = pl.ds(WIDTH + sl * LANES, LANES)
                cv = pl.ds(2 * WIDTH + sl * LANES, LANES)
                qr = (_rope(qkv_ref[:, cq], cos_b, sin_b, lo_half) * (HEAD_DIM ** -0.5)).astype(BF16)
                kr = _rope(qkv_ref[:, ck], cos_b, sin_b, lo_half).astype(BF16)
                v16 = qkv_ref[:, cv].astype(BF16)
                kcat = jnp.concatenate([kp_ref[:, cq], kr], axis=0)
                vcat = jnp.concatenate([vp_ref[:, cq], v16], axis=0)
                do_b = do_ref[:, cq]
                do16 = do_b.astype(BF16)
                lse_b = lse_ref[:, cq]
                cterm = dlse_ref[:, cq] - do_b * o_ref[:, cq]
                dqs, dkc, dvc = [], None, None
                for hm in (head0, jnp.logical_not(head0)):
                    qm = jnp.where(hm, qr, jnp.zeros_like(qr))
                    dom = jnp.where(hm, do16, jnp.zeros_like(do16))
                    sc = jnp.where(valid[:ATT_BLOCK], _dg(qm, kcat, 1, 1), -jnp.inf)
                    lse_h = jnp.max(jnp.where(hm, lse_b, -jnp.inf), axis=1, keepdims=True)
                    c = jnp.sum(jnp.where(hm, cterm, 0.0), axis=1, keepdims=True)
                    p = jnp.exp(sc - lse_h)
                    ds16 = (p * (_dg(dom, vcat, 1, 1) + c)).astype(BF16)
                    dv_h, dk_h = _dg(p.astype(BF16), dom, 0, 0), _dg(ds16, qm, 0, 0)
                    dvc = dv_h if dvc is None else dvc + dv_h
                    dkc = dk_h if dkc is None else dkc + dk_h
                    dqs.append(_dg(ds16, kcat, 1, 0))
                dq = jnp.where(head0, dqs[0], dqs[1]) * (HEAD_DIM ** -0.5)
                dqp_ref[:, cq] = _rope_bwd(dq, cos_b, sin_b, lo_half).astype(BF16)
                dqkv_ref[:, ck] = _rope_bwd(dka_ref[:, cq] + dkc[:ATT_BLOCK], cos_p, sin_p, lo_half).astype(BF16)
                dqkv_ref[:, cv] = (dva_ref[:, cq] + dvc[:ATT_BLOCK]).astype(BF16)
                dka_ref[:, cq] = dkc[ATT_BLOCK:]
                dva_ref[:, cq] = dvc[ATT_BLOCK:]
                kp_ref[:, cq] = kr
                vp_ref[:, cq] = v16

        @pl.when(jnp.logical_not(active))
        def _():
            for sl in range(WIDTH // LANES):
                cq = pl.ds(sl * LANES, LANES)
                dqkv_ref[:, pl.ds(WIDTH + sl * LANES, LANES)] = _rope_bwd(dka_ref[:, cq], cos_p, sin_p, lo_half).astype(BF16)
                dqkv_ref[:, pl.ds(2 * WIDTH + sl * LANES, LANES)] = dva_ref[:, cq].astype(BF16)

    cur = lambda w: pl.BlockSpec((None, ATT_BLOCK, w), lambda g, t: (g, jnp.minimum(t, nb - 1), 0))
    prev = lambda w: pl.BlockSpec((None, ATT_BLOCK, w), lambda g, t: (g, jnp.maximum(t - 1, 0), 0))
    return pl.pallas_call(
        body, name="dilated_attention_bwd", grid=(GROUPS, nb + 1),
        in_specs=[cur(3 * WIDTH), cur(LANES), cur(LANES), prev(LANES), prev(LANES),
                  cur(WIDTH), cur(WIDTH), cur(WIDTH), cur(WIDTH)],
        out_specs=prev(3 * WIDTH), out_shape=jax.ShapeDtypeStruct((GROUPS, s, 3 * WIDTH), BF16),
        scratch_shapes=[pltpu.VMEM((ATT_BLOCK, WIDTH), BF16), pltpu.VMEM((ATT_BLOCK, WIDTH), BF16),
                        pltpu.VMEM((ATT_BLOCK, WIDTH), F32), pltpu.VMEM((ATT_BLOCK, WIDTH), F32),
                        pltpu.VMEM((ATT_BLOCK, WIDTH), BF16)],
        compiler_params=_params("arbitrary", "arbitrary"),
    )(qkv, cos, sin, cos, sin, o, lse, do, dlse)


CONV_PAD = SUBLANES


def _gdn_post(y, is_q, is_k):
    head0, _ = _lane_masks(y.shape[0])
    c = _silu(y)
    sq = c * c
    ss0 = jnp.sum(jnp.where(head0, sq, 0.0), axis=1, keepdims=True)
    ss1 = jnp.sum(jnp.where(head0, 0.0, sq), axis=1, keepdims=True)
    r = jnp.where(head0, lax.rsqrt(ss0 + NORM_EPS), lax.rsqrt(ss1 + NORM_EPS))
    scale = jnp.where(is_q, HEAD_DIM ** -0.5, 1.0).astype(F32)
    return jnp.where(is_q | is_k, c * r * scale, c)


def _conv_rows(xp_ref, w, c0, rows):
    y = w[0:1, :] * xp_ref[pl.ds(c0 + CONV_PAD - (CONV_K - 1), rows), :]
    for k in range(1, CONV_K):
        y = y + w[k:k + 1, :] * xp_ref[pl.ds(c0 + CONV_PAD - (CONV_K - 1) + k, rows), :]
    return y


def _gdn_pre_fwd(proj_r, conv8, col0):
    s = proj_r.shape[0]
    tr = _row_block(s, 512)
    nblk = QKV_B // LANES
    nq = WIDTH // LANES

    def body(x_ref, w_ref, out_ref, xp_ref):
        j = pl.program_id(0)
        is_q, is_k = j < nq, (j >= nq) & (j < 2 * nq)
        xp_ref[pl.ds(0, CONV_PAD), :] = jnp.zeros((CONV_PAD, LANES), F32)
        xp_ref[pl.ds(CONV_PAD, s), :] = x_ref[...]
        w = w_ref[...]
        for c in range(s // tr):
            out_ref[pl.ds(c * tr, tr), :] = _gdn_post(_conv_rows(xp_ref, w, c * tr, tr), is_q, is_k)

    return pl.pallas_call(
        body, name="gdn_conv_fwd", grid=(nblk,),
        in_specs=[pl.BlockSpec((s, LANES), lambda j: (0, col0 + j)), pl.BlockSpec((SUBLANES, LANES), lambda j: (0, j))],
        out_specs=pl.BlockSpec((s, LANES), lambda j: (0, j)),
        out_shape=jax.ShapeDtypeStruct((s, QKV_B), F32),
        scratch_shapes=[pltpu.VMEM((s + CONV_PAD, LANES), F32)],
        compiler_params=_params("parallel"),
    )(proj_r, conv8)


def _gdn_pre_bwd(proj_r, conv8, dc, col0):
    s = proj_r.shape[0]
    tr = _row_block(s, 512)
    nblk = QKV_B // LANES
    nq = WIDTH // LANES

    def body(x_ref, w_ref, dc_ref, dx_ref, dw_ref, xp_ref, dyp_ref):
        j = pl.program_id(0)
        is_q, is_k = j < nq, (j >= nq) & (j < 2 * nq)
        xp_ref[pl.ds(0, CONV_PAD), :] = jnp.zeros((CONV_PAD, LANES), F32)
        xp_ref[pl.ds(CONV_PAD, s), :] = x_ref[...]
        dyp_ref[pl.ds(s, CONV_PAD), :] = jnp.zeros((CONV_PAD, LANES), F32)
        w = w_ref[...]
        for c in range(s // tr):
            y = _conv_rows(xp_ref, w, c * tr, tr)
            _, vjp = jax.vjp(lambda yy: _gdn_post(yy, is_q, is_k), y)
            dyp_ref[pl.ds(c * tr, tr), :] = vjp(dc_ref[pl.ds(c * tr, tr), :])[0]
        dws = [jnp.zeros((1, LANES), F32) for _ in range(CONV_K)]
        for c in range(s // tr):
            c0 = c * tr
            dy = dyp_ref[pl.ds(c0, tr), :]
            dx = w[0:1, :] * dyp_ref[pl.ds(c0 + CONV_K - 1, tr), :]
            for k in range(1, CONV_K):
                dx = dx + w[k:k + 1, :] * dyp_ref[pl.ds(c0 + CONV_K - 1 - k, tr), :]
            dx_ref[pl.ds(c0, tr), :] = dx.astype(BF16)
            for k in range(CONV_K):
                xs = xp_ref[pl.ds(c0 + CONV_PAD - (CONV_K - 1) + k, tr), :]
                dws[k] = dws[k] + jnp.sum(dy * xs, axis=0, keepdims=True)
        row = lax.broadcasted_iota(jnp.int32, (SUBLANES, LANES), 0)
        dwb = jnp.zeros((SUBLANES, LANES), F32)
        for k in range(CONV_K):
            dwb = dwb + jnp.where(row == k, dws[k], 0.0)
        dw_ref[...] = dwb

    return pl.pallas_call(
        body, name="gdn_conv_bwd", grid=(nblk,),
        in_specs=[pl.BlockSpec((s, LANES), lambda j: (0, col0 + j)), pl.BlockSpec((SUBLANES, LANES), lambda j: (0, j)),
                  pl.BlockSpec((s, LANES), lambda j: (0, j))],
        out_specs=(pl.BlockSpec((s, LANES), lambda j: (0, j)), pl.BlockSpec((SUBLANES, LANES), lambda j: (0, j))),
        out_shape=(jax.ShapeDtypeStruct((s, QKV_B), BF16), jax.ShapeDtypeStruct((SUBLANES, QKV_B), F32)),
        scratch_shapes=[pltpu.VMEM((s + CONV_PAD, LANES), F32), pltpu.VMEM((s + CONV_PAD, LANES), F32)],
        compiler_params=_params("parallel"),
    )(proj_r, conv8, dc)


def _gdn_chunk(q, k, v, bcol, acol, alog, dtb, gnw, state, t_saved=None):
    n = q.shape[-2]
    shp = (1, n, n)
    row = lax.broadcasted_iota(jnp.int32, shp, 1)
    col = lax.broadcasted_iota(jnp.int32, shp, 2)
    beta = _sigmoid(bcol)
    g = -jnp.exp(alog) * _softplus(acol + dtb)
    g_row = jnp.sum(jnp.where(row == col, g, 0.0), axis=-2, keepdims=True)
    big_g = jnp.sum(jnp.where(row >= col, g_row, 0.0), axis=-1, keepdims=True)
    big_g_row = jnp.sum(jnp.where(row <= col, g, 0.0), axis=-2, keepdims=True)
    decay_incl = jnp.exp(jnp.where(row >= col, big_g - big_g_row, -jnp.inf))
    decay_strict = jnp.where(row > col, decay_incl, 0.0)
    k_beta = k * beta
    a_mat = _mm_nt(k_beta, k) * decay_strict
    t_inv = _tri_inv(a_mat) if t_saved is None else _tri_inv_saved(a_mat, t_saved)
    e_g = jnp.exp(big_g)
    u = _mm(t_inv, v * beta)
    w = _mm(t_inv, k_beta * e_g)
    attn = _mm_nt(q, k) * decay_incl
    v_new = u - _mm(w, state)
    o = _mm(q * e_g, state) + _mm(attn, v_new)
    total = jnp.sum(g, axis=-2, keepdims=True)
    new_state = state * jnp.exp(total) + _mm_tn(k * jnp.exp(total - big_g), v_new)
    return _rmsnorm(o, gnw), new_state, t_inv


def _split_heads(x):
    return jnp.stack([x[:, h * HEAD_DIM:(h + 1) * HEAD_DIM] for h in range(HEADS)], axis=0)


def _merge_heads(x):
    return jnp.concatenate([x[h] for h in range(HEADS)], axis=1)


def _logit_columns(ba):
    lane = lax.broadcasted_iota(jnp.int32, ba.shape, 1)

    def cols(off):
        return jnp.stack([jnp.sum(jnp.where(lane == off + h, ba, 0.0), axis=1, keepdims=True) for h in range(HEADS)], axis=0)

    return cols(0), cols(HEADS)


def _logit_block(dbc, dac, shape):
    lane = lax.broadcasted_iota(jnp.int32, shape, 1)
    out = jnp.zeros(shape, F32)
    for h in range(HEADS):
        out = out + jnp.where(lane == h, dbc[h], 0.0) + jnp.where(lane == HEADS + h, dac[h], 0.0)
    return out


def _gdn_scan_fwd(cqkv, proj_r, ba_col, alog, dtb, gnw):
    s = cqkv.shape[0]
    nc = s // CHUNK

    def body(q_ref, k_ref, v_ref, ba_ref, al_ref, dt_ref, gnw_ref, o_ref, st_ref, ti_ref, state_ref):
        @pl.when(pl.program_id(0) == 0)
        def _():
            state_ref[...] = jnp.zeros_like(state_ref)

        st = state_ref[...]
        st_ref[...] = st
        bcol, acol = _logit_columns(ba_ref[...])
        o, new_st, t_inv = _gdn_chunk(_split_heads(q_ref[...]), _split_heads(k_ref[...]), _split_heads(v_ref[...]),
                                      bcol, acol, al_ref[...], dt_ref[...], gnw_ref[...], st)
        o_ref[...] = _merge_heads(o)
        ti_ref[...] = t_inv
        state_ref[...] = new_st

    part = lambda i: pl.BlockSpec((CHUNK, WIDTH), lambda n: (n, i))
    par = pl.BlockSpec((HEADS, 1, 1), lambda n: (0, 0, 0))
    per_chunk = pl.BlockSpec((None, HEADS, HEAD_DIM, HEAD_DIM), lambda n: (n, 0, 0, 0))
    per_chunk_shape = jax.ShapeDtypeStruct((nc, HEADS, HEAD_DIM, HEAD_DIM), F32)
    return pl.pallas_call(
        body, name="gdn_scan_fwd", grid=(nc,),
        in_specs=[part(0), part(1), part(2), pl.BlockSpec((CHUNK, LANES), lambda n: (n, ba_col)), par, par,
                  pl.BlockSpec((1, 1, HEAD_DIM), lambda n: (0, 0, 0))],
        out_specs=(part(0), per_chunk, per_chunk),
        out_shape=(jax.ShapeDtypeStruct((s, WIDTH), F32), per_chunk_shape, per_chunk_shape),
        scratch_shapes=[pltpu.VMEM((HEADS, HEAD_DIM, HEAD_DIM), F32)],
        compiler_params=_params("arbitrary"),
    )(cqkv, cqkv, cqkv, proj_r, alog, dtb, gnw)


def _gdn_scan_bwd(cqkv, proj_r, ba_col, alog, dtb, gnw, states, t_invs, do):
    s = cqkv.shape[0]
    nc = s // CHUNK

    def body(q_ref, k_ref, v_ref, ba_ref, al_ref, dt_ref, gnw_ref, st_ref, ti_ref, do_ref,
             dqkv_ref, dba_ref, dal_ref, ddt_ref, dgnw_ref, dstate_ref):
        @pl.when(pl.program_id(0) == 0)
        def _():
            dstate_ref[...] = jnp.zeros_like(dstate_ref)
            dal_ref[...] = jnp.zeros_like(dal_ref)
            ddt_ref[...] = jnp.zeros_like(ddt_ref)
            dgnw_ref[...] = jnp.zeros_like(dgnw_ref)

        bcol, acol = _logit_columns(ba_ref[...])
        t_saved = ti_ref[...]
        _, vjp = jax.vjp(lambda *a: _gdn_chunk(*a, t_saved=t_saved)[:2],
                         _split_heads(q_ref[...]), _split_heads(k_ref[...]), _split_heads(v_ref[...]),
                         bcol, acol, al_ref[...], dt_ref[...], gnw_ref[...], st_ref[...])
        dq, dk, dv, dbc, dac, dal, ddt, dgn, dst = vjp((_split_heads(do_ref[...]), dstate_ref[...]))
        dqkv_ref[:, pl.ds(0, WIDTH)] = _merge_heads(dq)
        dqkv_ref[:, pl.ds(WIDTH, WIDTH)] = _merge_heads(dk)
        dqkv_ref[:, pl.ds(2 * WIDTH, WIDTH)] = _merge_heads(dv)
        dba_ref[...] = _logit_block(dbc, dac, dba_ref.shape)
        dstate_ref[...] = dst
        dal_ref[...] += dal
        ddt_ref[...] += ddt
        dgnw_ref[...] += dgn

    rev = lambda n: nc - 1 - n
    part = lambda i: pl.BlockSpec((CHUNK, WIDTH), lambda n: (rev(n), i))
    par = pl.BlockSpec((HEADS, 1, 1), lambda n: (0, 0, 0))
    vec = pl.BlockSpec((1, 1, HEAD_DIM), lambda n: (0, 0, 0))
    par_shape = jax.ShapeDtypeStruct((HEADS, 1, 1), F32)
    per_chunk = pl.BlockSpec((None, HEADS, HEAD_DIM, HEAD_DIM), lambda n: (rev(n), 0, 0, 0))
    return pl.pallas_call(
        body, name="gdn_scan_bwd", grid=(nc,),
        in_specs=[part(0), part(1), part(2), pl.BlockSpec((CHUNK, LANES), lambda n: (rev(n), ba_col)), par, par, vec,
                  per_chunk, per_chunk, part(0)],
        out_specs=(pl.BlockSpec((CHUNK, QKV_B), lambda n: (rev(n), 0)), pl.BlockSpec((CHUNK, LANES), lambda n: (rev(n), 0)),
                   par, par, vec),
        out_shape=(jax.ShapeDtypeStruct((s, QKV_B), F32), jax.ShapeDtypeStruct((s, LANES), F32), par_shape, par_shape,
                   jax.ShapeDtypeStruct((1, 1, HEAD_DIM), F32)),
        scratch_shapes=[pltpu.VMEM((HEADS, HEAD_DIM, HEAD_DIM), F32)],
        compiler_params=_params("arbitrary"),
    )(cqkv, cqkv, cqkv, proj_r, alog, dtb, gnw, states, t_invs, do)


def _tail_loss(x, tgt, o0, o1, o2, l0, l1, l2, ga, gb, za, zb, ob, fnw, wua, wub, wo, tap_a, tap_b, tap_o):
    lm = jnp.maximum(jnp.maximum(l0, l1), l2)
    e0, e1, e2 = jnp.exp(l0 - lm), jnp.exp(l1 - lm), jnp.exp(l2 - lm)
    o_a = (e0 * o0 + e1 * o1 + e2 * o2) / (e0 + e1 + e2)
    y_a = _mm_tap(o_a * _silu(za), wua, tap_a)
    y_b = _mm_tap(ob * _silu(zb), wub, tap_b)
    merged = _sigmoid(ga) * y_a + _sigmoid(gb) * y_b
    y = _rmsnorm(x + _mm_tap(merged, wo, tap_o), fnw)
    err = y - tgt
    per_token = jnp.sum(err * err, axis=1, keepdims=True) * (0.5 / x.shape[1])
    return jnp.sum(per_token, axis=0, keepdims=True)


def _tail(x, tgt, o_all, lse_all, og12, lg12, proj_r, ob, wua, wub, wo, fnw):
    s, d = x.shape
    tm = _row_block(s, 128)
    col_za = 2 * d // WIDTH
    col_zb = (2 * d + WIDTH + QKV_B) // WIDTH

    def body(x_ref, t_ref, o0_ref, o1_ref, o2_ref, l0_ref, l1_ref, l2_ref, ga_ref, gb_ref, za_ref, zb_ref, ob_ref,
             wua_ref, wub_ref, wo_ref, fnw_ref,
             loss_ref, dx_ref, do0_ref, do1_ref, do2_ref, dl0_ref, dl1_ref, dl2_ref, dga_ref, dgb_ref, dza_ref,
             dzb_ref, dob_ref, dwua_ref, dwub_ref, dwo_ref, dfnw_ref):
        @pl.when(pl.program_id(0) == 0)
        def _():
            for r in (loss_ref, dwua_ref, dwub_ref, dwo_ref, dfnw_ref):
                r[...] = jnp.zeros_like(r)

        args = (x_ref[...], t_ref[...], o0_ref[...], o1_ref[...], o2_ref[...], l0_ref[...], l1_ref[...], l2_ref[...],
                ga_ref[...], gb_ref[...], za_ref[...], zb_ref[...], ob_ref[...], fnw_ref[...],
                wua_ref[...], wub_ref[...], wo_ref[...],
                jnp.zeros(wua_ref.shape, F32), jnp.zeros(wub_ref.shape, F32), jnp.zeros(wo_ref.shape, F32))
        loss, vjp = jax.vjp(_tail_loss, *args)
        (dx, _, do0, do1, do2, dl0, dl1, dl2, dga, dgb, dza, dzb, dob, dfnw, _, _, _, dwua, dwub, dwo) = vjp(
            jnp.ones((1, 1), F32))
        loss_ref[...] += jnp.broadcast_to(loss, loss_ref.shape)
        dx_ref[...] = dx
        do0_ref[...], do1_ref[...], do2_ref[...] = do0, do1, do2
        dl0_ref[...], dl1_ref[...], dl2_ref[...] = dl0, dl1, dl2
        dga_ref[...] = dga.astype(BF16)
        dgb_ref[...] = dgb.astype(BF16)
        dza_ref[...] = dza.astype(BF16)
        dzb_ref[...] = dzb.astype(BF16)
        dob_ref[...] = dob
        dwua_ref[...] += dwua
        dwub_ref[...] += dwub
        dwo_ref[...] += dwo
        dfnw_ref[...] += dfnw

    row = lambda w, c=0: pl.BlockSpec((tm, w), lambda i: (i, c))
    grp0 = pl.BlockSpec((None, tm, WIDTH), lambda i: (0, i, 0))
    full = lambda a, b: pl.BlockSpec((a, b), lambda i: (0, 0))
    f32 = lambda a, b: jax.ShapeDtypeStruct((a, b), F32)
    b16 = lambda a, b: jax.ShapeDtypeStruct((a, b), BF16)
    stacked = jax.ShapeDtypeStruct((GROUPS, s, WIDTH), F32)
    gspecs = [grp0, row(WIDTH), row(WIDTH)]
    in_specs = ([row(d), row(d)] + gspecs * 2 + [row(d, 0), row(d, 1), row(WIDTH, col_za), row(WIDTH, col_zb),
                row(WIDTH), full(WIDTH, d), full(WIDTH, d), full(d, d), full(1, d)])
    out_specs = ([full(SUBLANES, LANES), row(d)] + gspecs * 2 + [row(d), row(d), row(WIDTH), row(WIDTH), row(WIDTH),
                 full(WIDTH, d), full(WIDTH, d), full(d, d), full(1, d)])
    gshapes = [stacked, f32(s, WIDTH), f32(s, WIDTH)]
    out_shape = ([f32(SUBLANES, LANES), f32(s, d)] + gshapes * 2 + [b16(s, d), b16(s, d), b16(s, WIDTH),
                 b16(s, WIDTH), f32(s, WIDTH), f32(WIDTH, d), f32(WIDTH, d), f32(d, d), f32(1, d)])
    return pl.pallas_call(
        body, name="tail_fwd_bwd", grid=(s // tm,),
        in_specs=in_specs, out_specs=tuple(out_specs), out_shape=tuple(out_shape),
        compiler_params=_params("arbitrary"),
    )(x, tgt, o_all, og12[0], og12[1], lse_all, lg12[0], lg12[1], proj_r, proj_r, proj_r, proj_r, ob, wua, wub, wo, fnw)


PERMUTE_SPAN = 4096


def _permute_span(s):
    return PERMUTE_SPAN if s % PERMUTE_SPAN == 0 else s


def _from_dilated_rows(stacked, g, dil, name):
    n_slots, s, c = stacked.shape
    view = stacked.reshape(n_slots, dil, s // dil, c)
    span = _permute_span(s)

    def body(in_ref, out_ref):
        for r in range(dil):
            out_ref[pl.ds(r, span // dil, stride=dil), :] = in_ref[r]

    return pl.pallas_call(
        body, name=name, grid=(s // span, c // LANES),
        in_specs=[pl.BlockSpec((None, dil, span // dil, LANES), lambda n, j: (g, 0, n, j))],
        out_specs=pl.BlockSpec((span, LANES), lambda n, j: (n, j)),
        out_shape=jax.ShapeDtypeStruct((s, c), stacked.dtype),
        compiler_params=_params("parallel", "parallel"),
    )(view)


def _to_dilated_rows_into(nat, stacked, g, dil, name):
    n_slots, s, c = stacked.shape
    view = stacked.reshape(n_slots, dil, s // dil, c)
    span = _permute_span(s)

    def body(nat_ref, old_ref, out_ref):
        for r in range(dil):
            out_ref[r] = nat_ref[pl.ds(r, span // dil, stride=dil), :]

    out = pl.pallas_call(
        body, name=name, grid=(s // span, c // LANES),
        in_specs=[pl.BlockSpec((span, LANES), lambda n, j: (n, j)), pl.BlockSpec(memory_space=pl.ANY)],
        out_specs=pl.BlockSpec((None, dil, span // dil, LANES), lambda n, j: (g, 0, n, j)),
        out_shape=jax.ShapeDtypeStruct(view.shape, stacked.dtype),
        input_output_aliases={1: 0},
        compiler_params=_params("parallel", "parallel"),
    )(nat, view)
    return out.reshape(stacked.shape)


def _to_dilated(a, dil):
    if dil == 1:
        return a
    s = a.shape[0]
    return a.reshape(s // dil, dil, -1).transpose(1, 0, 2).reshape(a.shape)


def _from_dilated(a, dil):
    if dil == 1:
        return a
    s = a.shape[0]
    return a.reshape(dil, s // dil, -1).transpose(1, 0, 2).reshape(a.shape)


def _head_major(a):
    return a.reshape(a.shape[0], HEADS, HEAD_DIM).transpose(1, 0, 2)


def _from_head_major(a):
    return a.transpose(1, 0, 2).reshape(a.shape[1], WIDTH)


def _rope_tables(s):
    inv_freq = ROPE_THETA ** (-jnp.arange(0, HEAD_DIM, 2, dtype=F32) / HEAD_DIM)
    ang = jnp.arange(s, dtype=F32)[:, None] * inv_freq[None, :]
    cos_n = jnp.tile(jnp.cos(ang), (1, 2 * LANES // HEAD_DIM))
    sin_h = jnp.sin(ang)
    sin_n = jnp.tile(jnp.concatenate([-sin_h, sin_h], axis=1), (1, LANES // HEAD_DIM))
    def per_group(table, tag):
        out = jnp.broadcast_to(table, (GROUPS,) + table.shape)
        for g in range(1, GROUPS):
            out = _to_dilated_rows_into(table, out, g, DILATIONS[g], "rope_%s_to_dilated_%d" % (tag, g))
        return out

    return per_group(cos_n, "cos"), per_group(sin_n, "sin")


def _pack_rows(parts, dtype, row_multiple):
    flat = jnp.concatenate([p.reshape(-1).astype(dtype) for p in parts])
    tile = row_multiple * LANES
    pad = (-flat.shape[0]) % tile
    return jnp.pad(flat, (0, pad)).reshape(-1, LANES)


def _unpack_rows(packed, shapes):
    flat = packed.reshape(-1)
    out, start = [], 0
    for shp in shapes:
        size = 1
        for n in shp:
            size *= n
        out.append(flat[start:start + size].reshape(shp))
        start += size
    return out


def kernel(x, norm_w, w_in, conv_w, a_log, dt_bias, gdn_norm_w, w_up_a, w_up_b, w_out, final_norm_w, loss_target, m_norm_w, m_w_in, m_conv_w, m_a_log, m_dt_bias, m_gdn_norm_w, m_w_up_a, m_w_up_b, m_w_out, m_final_norm_w, v_norm_w, v_w_in, v_conv_w, v_a_log, v_dt_bias, v_gdn_norm_w, v_w_up_a, v_w_up_b, v_w_out, v_final_norm_w):
    x2, tgt = x[0], loss_target[0]
    s, d = x2.shape
    me = 4 * lax.axis_index("x") + 2 * lax.axis_index("y") + lax.axis_index("c")
    win8 = w_in.shape[2]
    conv8w = conv_w.shape[2]

    conv_shard = jnp.pad(conv_w[0], ((0, SUBLANES - CONV_K), (0, 0)))
    w_in_g, wua_g, wub_g, wo_g, conv_g = _all_gather(
        [w_in[0].astype(BF16), w_up_a[0].astype(BF16), w_up_b[0].astype(BF16), w_out[0].astype(BF16), conv_shard])
    w_in_f = jnp.concatenate([w_in_g[i] for i in range(N_DEV)], axis=1)
    wua = jnp.concatenate([wua_g[i] for i in range(N_DEV)], axis=1)
    wub = jnp.concatenate([wub_g[i] for i in range(N_DEV)], axis=1)
    wo = wo_g.reshape(d, d)
    conv8 = jnp.concatenate([conv_g[i] for i in range(N_DEV)], axis=1)

    w_qkv = w_in_f[:, :QKV_A].reshape(d, GROUPS, QKV_B).transpose(1, 0, 2)
    w_rest = jnp.concatenate([
        w_in_f[:, OFF_GATE:OFF_GATE + 2 * d], w_in_f[:, OFF_ZA:OFF_ZA + WIDTH], w_in_f[:, OFF_QKVB:OFF_QKVB + QKV_B],
        w_in_f[:, OFF_ZB:OFF_ZB + WIDTH], w_in_f[:, OFF_BA:OFF_BA + 2 * HEADS],
        jnp.zeros((d, BA_PAD - 2 * HEADS), BF16)], axis=1)
    col_qkvb = (2 * d + WIDTH) // LANES
    col_ba = (2 * d + 2 * WIDTH + QKV_B) // LANES

    h = _rms_fwd(x2, norm_w)
    h_all = jnp.stack([_to_dilated(h, dil) for dil in DILATIONS])
    qkv_all = _matmul(h_all, w_qkv, F32, "in_proj_attention", tn=QKV_B)
    proj_r = _matmul(h[None], w_rest[None], F32, "in_proj_rest")[0]
    cos, sin = _rope_tables(s)
    o_all, lse_all = _attn_fwd(qkv_all, cos, sin)
    og12 = [_from_dilated_rows(o_all, g, DILATIONS[g], "attn_out_to_natural_%d" % g) for g in (1, 2)]
    lg12 = [_from_dilated_rows(lse_all, g, DILATIONS[g], "attn_lse_to_natural_%d" % g) for g in (1, 2)]

    cqkv = _gdn_pre_fwd(proj_r, conv8, col_qkvb)
    alog3, dtb3, gnw3 = a_log.reshape(HEADS, 1, 1), dt_bias.reshape(HEADS, 1, 1), gdn_norm_w.reshape(1, 1, HEAD_DIM)
    ob, states, t_invs = _gdn_scan_fwd(cqkv, proj_r, col_ba, alog3, dtb3, gnw3)

    (loss_blk, dx_res, do_all, do1, do2, dl_all, dl1, dl2, dga, dgb, dza, dzb, dob, dwua, dwub, dwo, dfnw) = _tail(
        x2, tgt, o_all, lse_all, og12, lg12, proj_r, ob, wua, wub, wo, final_norm_w.reshape(1, d))

    for g, (t_o, t_l) in ((1, (do1, dl1)), (2, (do2, dl2))):
        do_all = _to_dilated_rows_into(t_o, do_all, g, DILATIONS[g], "attn_dout_to_dilated_%d" % g)
        dl_all = _to_dilated_rows_into(t_l, dl_all, g, DILATIONS[g], "attn_dlse_to_dilated_%d" % g)
    dqkv_all = _attn_bwd(qkv_all, cos, sin, o_all, lse_all, do_all, dl_all)

    dcqkv, dba, dalog3, ddtb3, dgnw3 = _gdn_scan_bwd(cqkv, proj_r, col_ba, alog3, dtb3, gnw3, states, t_invs, dob)
    dqkv_b, dconv8 = _gdn_pre_bwd(proj_r, conv8, dcqkv, col_qkvb)
    dproj_r = jnp.concatenate([dga, dgb, dza, dqkv_b, dzb,
                               jnp.pad(dba.astype(BF16), ((0, 0), (0, BA_PAD - LANES)))], axis=1)

    dw_qkv = _matmul(h_all, dqkv_all, F32, "in_proj_attention_dw", mode="tn", tk=2048)
    dw_rest = _matmul(h[None], dproj_r[None], F32, "in_proj_rest_dw", mode="tn", tk=2048)[0]
    dh_a = _matmul(dqkv_all, w_qkv, F32, "in_proj_attention_dh", mode="nt", tk=2048)
    dh_r = _matmul(dproj_r[None], w_rest[None], F32, "in_proj_rest_dh", mode="nt", tk=2560)[0]
    dh_parts = [dh_r] + [_from_dilated_rows(dh_a, g, DILATIONS[g], "dh_to_natural_%d" % g) for g in (1, 2)]
    grad_x, dnorm_w = _rms_bwd(x2, norm_w, dh_a, dh_parts, dx_res)

    o2 = 2 * d
    dw_in = jnp.concatenate([
        dw_qkv.transpose(1, 0, 2).reshape(d, QKV_A),
        dw_rest[:, o2:o2 + WIDTH], dw_rest[:, o2 + WIDTH:o2 + WIDTH + QKV_B],
        dw_rest[:, o2 + WIDTH + QKV_B:o2 + 2 * WIDTH + QKV_B],
        dw_rest[:, o2 + 2 * WIDTH + QKV_B:o2 + 2 * WIDTH + QKV_B + 2 * HEADS],
        dw_rest[:, :o2]], axis=1)

    def col_slabs(a, width):
        return jnp.stack([a[:, j * width:(j + 1) * width] for j in range(N_DEV)])

    slabs = [col_slabs(dw_in, win8), col_slabs(dwua, d // N_DEV), col_slabs(dwub, d // N_DEV),
             dwo.reshape(N_DEV, d // N_DEV, d)]
    from_sibling = _sibling_exchange(slabs)
    core = lax.axis_index("c").astype(jnp.int32).reshape(1)
    partials = [_pair_sum(a, b, core, "grads_pair_sum_%d" % i) for i, (a, b) in enumerate(zip(slabs, from_sibling))]
    contrib = _chip_exchange(partials)

    small_parts = [dnorm_w, dfnw, dconv8[:CONV_K], dalog3[:, 0, 0], ddtb3[:, 0, 0], dgnw3[0], loss_blk[0, 0:1]]
    small_rows = [-(-p.size // LANES) for p in small_parts]
    small = jnp.concatenate([jnp.pad(p.reshape(-1), (0, r * LANES - p.size)).reshape(r, LANES)
                             for p, r in zip(small_parts, small_rows)])
    small = jnp.pad(small, ((0, (-small.shape[0]) % SUBLANES), (0, 0)))
    small_sum = _small_all_reduce(small)
    pieces, r0 = [], 0
    for p, r in zip(small_parts, small_rows):
        pieces.append(small_sum[r0:r0 + r].reshape(-1)[:p.size].reshape(p.shape))
        r0 += r
    g_norm_w, g_fnw, g_conv_full, g_alog, g_dtb, g_gnw, loss_sum = pieces
    g_conv = lax.dynamic_slice(g_conv_full, (0, me * conv8w), (CONV_K, conv8w))

    big = [_adamw(c, w[0], m[0], v[0], name) for c, w, m, v, name in (
        (contrib[0], w_in, m_w_in, v_w_in, "adamw_w_in"), (contrib[1], w_up_a, m_w_up_a, v_w_up_a, "adamw_w_up_a"),
        (contrib[2], w_up_b, m_w_up_b, v_w_up_b, "adamw_w_up_b"), (contrib[3], w_out, m_w_out, v_w_out, "adamw_w_out"))]
    g_big, d_big, nm_big, nv_big = ([t[i] for t in big] for i in range(4))

    small_ws = [norm_w, final_norm_w, conv_w, a_log, dt_bias, gdn_norm_w]
    small_ms = [m_norm_w, m_final_norm_w, m_conv_w, m_a_log, m_dt_bias, m_gdn_norm_w]
    small_vs = [v_norm_w, v_final_norm_w, v_conv_w, v_a_log, v_dt_bias, v_gdn_norm_w]
    small_gs = [g_norm_w, g_fnw, g_conv, g_alog, g_dtb, g_gnw]
    small_shapes = [t.shape for t in small_ws]
    sm = _adamw(_pack_rows(small_gs, F32, SUBLANES)[None], _pack_rows(small_ws, F32, SUBLANES),
                _pack_rows(small_ms, F32, SUBLANES), _pack_rows(small_vs, F32, SUBLANES), "adamw_small")
    g_sm, d_sm, nm_sm, nv_sm = (_unpack_rows(t, small_shapes) for t in sm)

    def ordered(bigs, smalls):
        nw, fnw_, cw, al, dtb, gn = smalls
        wi, ua, ub, wo_ = (t[None] for t in bigs)
        return [nw, wi, cw, al, dtb, gn, ua, ub, wo_, fnw_]

    return (loss_sum.reshape(()), grad_x[None], *ordered(g_big, g_sm), *ordered(d_big, d_sm),
            *ordered(nm_big, nm_sm), *ordered(nv_big, nv_sm))
```

```python
import functools

import jax
import jax.numpy as jnp
from jax import lax
from jax.experimental import pallas as pl
from jax.experimental.pallas import tpu as pltpu

F32 = jnp.float32
BF16 = jnp.bfloat16
MESH = pl.DeviceIdType.MESH
N_DEV = 8
LANES = 128
SUBLANES = 8

GROUPS = 3
HEADS = 8
HEAD_DIM = 64
WIDTH = HEADS * HEAD_DIM
ATT_BLOCK = 128
DILATIONS = (1, 4, 16)
N_BACK = 128
CONV_K = 4
CHUNK = 64
QKV_B = 3 * WIDTH
QKV_A = GROUPS * 3 * WIDTH
BA_PAD = 512
NORM_EPS = 1e-6
ROPE_THETA = 10000.0
ADAM_LR, ADAM_B1, ADAM_B2, ADAM_EPS, ADAM_WD, ADAM_STEP = 0.001, 0.9, 0.999, 1e-08, 0.01, 10

VMEM_LIMIT = 56 * 1024 * 1024

OFF_ZA = QKV_A
OFF_QKVB = OFF_ZA + WIDTH
OFF_ZB = OFF_QKVB + QKV_B
OFF_BA = OFF_ZB + WIDTH
OFF_GATE = OFF_BA + 2 * HEADS


def _params(*sem):
    return pltpu.CompilerParams(dimension_semantics=sem, vmem_limit_bytes=VMEM_LIMIT)


def _dg(a, b, ca, cb):
    nb = a.ndim - 2
    batch = tuple(range(nb))
    return lax.dot_general(a, b, (((nb + ca,), (nb + cb,)), (batch, batch)), preferred_element_type=F32)


@jax.custom_vjp
def _mm(a, b):
    return _dg(a.astype(BF16), b.astype(BF16), 1, 0)


def _mm_fwd(a, b):
    return _mm(a, b), (a.astype(BF16), b.astype(BF16))


def _mm_bwd(res, ct):
    a16, b16 = res
    c16 = ct.astype(BF16)
    return _dg(c16, b16, 1, 1), _dg(a16, c16, 0, 0)


_mm.defvjp(_mm_fwd, _mm_bwd)


@jax.custom_vjp
def _mm_nt(a, b):
    return _dg(a.astype(BF16), b.astype(BF16), 1, 1)


def _mm_nt_fwd(a, b):
    return _mm_nt(a, b), (a.astype(BF16), b.astype(BF16))


def _mm_nt_bwd(res, ct):
    a16, b16 = res
    c16 = ct.astype(BF16)
    return _dg(c16, b16, 1, 0), _dg(c16, a16, 0, 0)


_mm_nt.defvjp(_mm_nt_fwd, _mm_nt_bwd)


@jax.custom_vjp
def _mm_tn(a, b):
    return _dg(a.astype(BF16), b.astype(BF16), 0, 0)


def _mm_tn_fwd(a, b):
    return _mm_tn(a, b), (a.astype(BF16), b.astype(BF16))


def _mm_tn_bwd(res, ct):
    a16, b16 = res
    c16 = ct.astype(BF16)
    return _dg(b16, c16, 1, 1), _dg(a16, c16, 1, 0)


_mm_tn.defvjp(_mm_tn_fwd, _mm_tn_bwd)


@jax.custom_vjp
def _mm_tap(a, w16, tap):
    return _dg(a.astype(BF16), w16, 1, 0)


def _mm_tap_fwd(a, w16, tap):
    return _mm_tap(a, w16, tap), (a.astype(BF16), w16)


def _mm_tap_bwd(res, ct):
    a16, w16 = res
    c16 = ct.astype(BF16)
    return _dg(c16, w16, 1, 1), jnp.zeros_like(w16), _dg(a16, c16, 0, 0)


_mm_tap.defvjp(_mm_tap_fwd, _mm_tap_bwd)


def _split16(a):
    hi = a.astype(BF16)
    lo = (a - hi.astype(F32)).astype(BF16)
    return hi, lo


def _dot3(a, b, ca, cb):
    ah, al = _split16(a)
    bh, bl = _split16(b)
    return _dg(ah, bh, ca, cb) + (_dg(ah, bl, ca, cb) + _dg(al, bh, ca, cb))


def _tri_inv_impl(a):
    n = a.shape[-1]
    shp = (1,) * (a.ndim - 2) + (n, n)
    eye = (lax.broadcasted_iota(jnp.int32, shp, a.ndim - 2) == lax.broadcasted_iota(jnp.int32, shp, a.ndim - 1)).astype(F32)
    x = eye - a
    p = a
    for it in range(5):
        dot = _dot3 if it < 2 else (lambda u, v, cu, cv: _dg(u.astype(BF16), v.astype(BF16), cu, cv))
        p = dot(p, p, 1, 0)
        x = x + dot(x, p, 1, 0)
    return x


@jax.custom_vjp
def _tri_inv(a):
    return _tri_inv_impl(a)


def _tri_inv_fwd(a):
    t = _tri_inv_impl(a)
    return t, t


def _tri_inv_bwd(t, ct):
    t16 = t.astype(BF16)
    return (-_dg(_dg(t16, ct.astype(BF16), 0, 0).astype(BF16), t16, 1, 1),)


_tri_inv.defvjp(_tri_inv_fwd, _tri_inv_bwd)


@jax.custom_vjp
def _tri_inv_saved(a, t):
    return t


def _tri_inv_saved_fwd(a, t):
    return t, t


def _tri_inv_saved_bwd(t, ct):
    return _tri_inv_bwd(t, ct) + (jnp.zeros_like(t),)


_tri_inv_saved.defvjp(_tri_inv_saved_fwd, _tri_inv_saved_bwd)


def _sigmoid(x):
    return 1.0 / (1.0 + jnp.exp(-x))


def _silu(x):
    return x * _sigmoid(x)


def _softplus(x):
    return jnp.maximum(x, 0.0) + jnp.log(1.0 + jnp.exp(-jnp.abs(x)))


def _rmsnorm(x, w):
    return x * lax.rsqrt(jnp.mean(x * x, axis=-1, keepdims=True) + NORM_EPS) * w


def _row_block(rows, cap):
    best = None
    for cand in range(SUBLANES, min(rows, cap) + 1, SUBLANES):
        if rows % cand == 0:
            best = cand
    assert best is not None, rows
    return best


def _mesh_peers():
    x, y, c = lax.axis_index("x"), lax.axis_index("y"), lax.axis_index("c")
    me = 4 * x + 2 * y + c
    peers = []
    for k in range(1, N_DEV):
        px = 1 - x if (k >> 2) & 1 else x
        py = 1 - y if (k >> 1) & 1 else y
        pc = 1 - c if k & 1 else c
        peers.append(((px, py, pc), 4 * px + 2 * py + pc))
    return me, peers


N_CHIPS = 4
OTHER_CHIPS = 3


def _chip_peers():
    x, y, c = lax.axis_index("x"), lax.axis_index("y"), lax.axis_index("c")
    return x, y, c, [(1 - x, y), (x, 1 - y), (1 - x, 1 - y)]


def _all_gather(shards):
    n_arr = len(shards)
    per = 1 + 2 * OTHER_CHIPS

    def body(*refs):
        in_refs, out_refs = refs[:n_arr], refs[n_arr:2 * n_arr]
        send_sems, recv_sems, loc_sems = refs[2 * n_arr:]
        x, y, c, chips = _chip_peers()
        me, sibling = (x, y, c), (x, y, 1 - c)

        def slot(px, py, pc):
            return 4 * px + 2 * py + pc

        def copy(i, k, block, to, src=None):
            dst = out_refs[i].at[slot(*block)]
            return pltpu.make_async_remote_copy(src_ref=dst if src is None else src, dst_ref=dst,
                                                send_sem=send_sems.at[i * per + k], recv_sem=recv_sems.at[i * per + k],
                                                device_id=to, device_id_type=MESH)

        own = [pltpu.make_async_copy(in_refs[i], out_refs[i].at[slot(*me)], loc_sems.at[i]) for i in range(n_arr)]
        for cp in own:
            cp.start()
        first = []
        for i in range(n_arr):
            first += [copy(i, 1 + j, me, (*chip, c), src=in_refs[i]) for j, chip in enumerate(chips)]
            first.append(copy(i, 0, me, sibling, src=in_refs[i]))
        for cp in first:
            cp.start()
        passed = []
        for j, chip in enumerate(chips):
            for i in range(n_arr):
                copy(i, 1 + j, (*chip, c), me).wait_recv()
                fwd = copy(i, 1 + OTHER_CHIPS + j, (*chip, c), sibling)
                fwd.start()
                passed.append(fwd)
        for i in range(n_arr):
            copy(i, 0, sibling, me).wait_recv()
            for j, chip in enumerate(chips):
                copy(i, 1 + OTHER_CHIPS + j, (*chip, 1 - c), me).wait_recv()
        for cp in first + passed:
            cp.wait_send()
        for cp in own:
            cp.wait()

    any_spec = pl.BlockSpec(memory_space=pl.ANY)
    return pl.pallas_call(
        body, name="weights_all_gather",
        out_shape=tuple(jax.ShapeDtypeStruct((N_DEV,) + a.shape, a.dtype) for a in shards),
        in_specs=[any_spec] * n_arr, out_specs=tuple([any_spec] * n_arr),
        scratch_shapes=[pltpu.SemaphoreType.DMA((n_arr * per,)), pltpu.SemaphoreType.DMA((n_arr * per,)),
                        pltpu.SemaphoreType.DMA((n_arr,))],
    )(*shards)


class _Exchange:
    def __init__(self, arrays, out_shapes, n_sem, copies):
        self.arrays, self.out_shapes, self.n_sem, self.copies = list(arrays), list(out_shapes), n_sem, copies


def _sibling_exchange(slabs):
    n_arr = len(slabs)

    def copies(in_refs, out_refs, send_sems, recv_sems, loc_sems):
        x, y, c, _ = _chip_peers()
        sends = [pltpu.make_async_remote_copy(src_ref=in_refs[i].at[2 * q + (1 - c)], dst_ref=out_refs[i].at[q],
                                              send_sem=send_sems.at[i * N_CHIPS + q], recv_sem=recv_sems.at[i * N_CHIPS + q],
                                              device_id=(x, y, 1 - c), device_id_type=MESH)
                 for i in range(n_arr) for q in range(N_CHIPS)]

        def start():
            for cp in sends:
                cp.start()

        def finish():
            for cp in sends:
                cp.wait_recv()
            for cp in sends:
                cp.wait_send()

        return start, finish

    return _Exchange(slabs, [jax.ShapeDtypeStruct((N_CHIPS,) + a.shape[1:], a.dtype) for a in slabs],
                     n_arr * N_CHIPS, copies)


def _pair_sum(slabs, from_sibling, core, name):
    _, rows, cols = slabs.shape
    tr = _row_block(rows, max(SUBLANES, (256 * 1024) // cols // SUBLANES * SUBLANES))

    def body(core_ref, a_ref, b_ref, o_ref):
        o_ref[...] = (a_ref[...] + b_ref[...]).astype(BF16)

    grid_spec = pltpu.PrefetchScalarGridSpec(
        num_scalar_prefetch=1, grid=(N_CHIPS, rows // tr),
        in_specs=[pl.BlockSpec((None, tr, cols), lambda q, r, core_ref: (2 * q + core_ref[0], r, 0)),
                  pl.BlockSpec((None, tr, cols), lambda q, r, core_ref: (q, r, 0))],
        out_specs=pl.BlockSpec((None, tr, cols), lambda q, r, core_ref: (q, r, 0)))
    return pl.pallas_call(
        body, name=name, grid_spec=grid_spec,
        out_shape=jax.ShapeDtypeStruct((N_CHIPS, rows, cols), BF16),
        compiler_params=_params("parallel", "parallel"),
    )(core, slabs, from_sibling)


def _chip_exchange(partials):
    n_arr = len(partials)

    def copies(in_refs, out_refs, send_sems, recv_sems, loc_sems):
        x, y, c, chips = _chip_peers()
        mine = 2 * x + y
        own = [pltpu.make_async_copy(in_refs[i].at[mine], out_refs[i].at[mine], loc_sems.at[i]) for i in range(n_arr)]

        def copy(i, j, chip, src_slot, dst_slot):
            return pltpu.make_async_remote_copy(src_ref=in_refs[i].at[src_slot], dst_ref=out_refs[i].at[dst_slot],
                                                send_sem=send_sems.at[i * OTHER_CHIPS + j],
                                                recv_sem=recv_sems.at[i * OTHER_CHIPS + j],
                                                device_id=(*chip, c), device_id_type=MESH)

        sends = [copy(i, j, chip, 2 * chip[0] + chip[1], mine) for j, chip in enumerate(chips) for i in range(n_arr)]
        recvs = [copy(i, j, chip, mine, 2 * chip[0] + chip[1]) for j, chip in enumerate(chips) for i in range(n_arr)]

        def start():
            for cp in own + sends:
                cp.start()

        def finish():
            for cp in recvs:
                cp.wait_recv()
            for cp in sends:
                cp.wait_send()
            for cp in own:
                cp.wait()

        return start, finish

    return _Exchange(partials, [jax.ShapeDtypeStruct(a.shape, a.dtype) for a in partials], n_arr * OTHER_CHIPS, copies)


def _small_all_reduce(part):
    rows = part.shape[0]

    def body(p_ref, o_ref, buf_ref, send_sems, recv_sems):
        me, peers = _mesh_peers()
        buf_ref[me] = p_ref[...]
        sends = []
        for k, (dev, pid) in enumerate(peers):
            cp = pltpu.make_async_remote_copy(src_ref=p_ref, dst_ref=buf_ref.at[me], send_sem=send_sems.at[k],
                                              recv_sem=recv_sems.at[k], device_id=dev, device_id_type=MESH)
            cp.start()
            sends.append(cp)
        for k, (dev, pid) in enumerate(peers):
            pltpu.make_async_remote_copy(src_ref=p_ref, dst_ref=buf_ref.at[pid], send_sem=send_sems.at[k],
                                         recv_sem=recv_sems.at[k], device_id=dev, device_id_type=MESH).wait_recv()
        for cp in sends:
            cp.wait_send()
        acc = buf_ref[0]
        for i in range(1, N_DEV):
            acc = acc + buf_ref[i]
        o_ref[...] = acc

    vmem = pl.BlockSpec(memory_space=pltpu.VMEM)
    return pl.pallas_call(
        body, name="small_all_reduce",
        out_shape=jax.ShapeDtypeStruct(part.shape, F32),
        in_specs=[vmem], out_specs=vmem,
        scratch_shapes=[pltpu.VMEM((N_DEV, rows, LANES), F32), pltpu.SemaphoreType.DMA((N_DEV - 1,)),
                        pltpu.SemaphoreType.DMA((N_DEV - 1,))],
    )(part)


def _adamw_vals(w, g, m, v):
    m = ADAM_B1 * m + (1.0 - ADAM_B1) * g
    v = ADAM_B2 * v + (1.0 - ADAM_B2) * (g * g)
    m_hat = m / (1.0 - ADAM_B1 ** ADAM_STEP)
    v_hat = v / (1.0 - ADAM_B2 ** ADAM_STEP)
    delta = -ADAM_LR * (m_hat / (jnp.sqrt(v_hat) + ADAM_EPS) + ADAM_WD * w)
    return delta, m, v


def _adamw(contrib, w, m, v, name):
    n, rows, cols = contrib.shape
    tr = _row_block(rows, max(SUBLANES, (128 * 1024) // cols // SUBLANES * SUBLANES))

    def body(c_ref, w_ref, m_ref, v_ref, g_ref, d_ref, nm_ref, nv_ref):
        g = c_ref[0].astype(F32)
        for i in range(1, n):
            g = g + c_ref[i].astype(F32)
        delta, nm, nv = _adamw_vals(w_ref[...], g, m_ref[...], v_ref[...])
        g_ref[...] = g
        d_ref[...] = delta
        nm_ref[...] = nm
        nv_ref[...] = nv

    row = pl.BlockSpec((tr, cols), lambda i: (i, 0))
    shp = jax.ShapeDtypeStruct((rows, cols), F32)
    return pl.pallas_call(
        body, name=name, grid=(rows // tr,),
        in_specs=[pl.BlockSpec((n, tr, cols), lambda i: (0, i, 0)), row, row, row],
        out_specs=(row, row, row, row), out_shape=(shp, shp, shp, shp),
        compiler_params=_params("parallel"),
    )(contrib, w, m, v)


def _lane_block(n, cap):
    if n <= cap:
        return n
    best = None
    for cand in range(LANES, cap + 1, LANES):
        if n % cand == 0:
            best = cand
    assert best is not None, n
    return best


def _matmul(a, b, out_dtype, name, mode="nn", tm=1024, tn=1024, tk=1024, exchange=None):
    g = a.shape[0]
    m, k = (a.shape[2], a.shape[1]) if mode == "tn" else (a.shape[1], a.shape[2])
    n = b.shape[1] if mode == "nt" else b.shape[2]
    tm, tn, tk = _lane_block(m, tm), _lane_block(n, tn), _lane_block(k, tk)
    nk = k // tk
    grid = (g, m // tm, n // tn, nk)
    a_spec = (pl.BlockSpec((None, tk, tm), lambda gi, i, j, kk: (gi, kk, i)) if mode == "tn" else
              pl.BlockSpec((None, tm, tk), lambda gi, i, j, kk: (gi, i, kk)))
    b_spec = (pl.BlockSpec((None, tn, tk), lambda gi, i, j, kk: (gi, j, kk)) if mode == "nt" else
              pl.BlockSpec((None, tk, tn), lambda gi, i, j, kk: (gi, kk, j)))
    ca, cb = (0 if mode == "tn" else 1), (1 if mode == "nt" else 0)
    n_ex = 0 if exchange is None else len(exchange.arrays)

    def body(a_ref, b_ref, *rest):
        ex_in, o_ref, ex_out, scratch = rest[:n_ex], rest[n_ex], rest[n_ex + 1:2 * n_ex + 1], rest[2 * n_ex + 1:]
        if exchange is not None:
            start, finish = exchange.copies(ex_in, ex_out, *scratch[-3:])
            pids = [pl.program_id(ax) for ax in range(4)]
            pl.when((pids[0] == 0) & (pids[1] == 0) & (pids[2] == 0) & (pids[3] == 0))(start)
        part = _dg(a_ref[...], b_ref[...], ca, cb)
        if nk == 1:
            o_ref[...] = part.astype(o_ref.dtype)
        else:
            acc_ref = scratch[0]
            kk = pl.program_id(3)

            @pl.when(kk == 0)
            def _():
                acc_ref[...] = part

            @pl.when((kk > 0) & (kk < nk - 1))
            def _():
                acc_ref[...] += part

            @pl.when(kk == nk - 1)
            def _():
                o_ref[...] = (acc_ref[...] + part).astype(o_ref.dtype)
        if exchange is not None:
            pl.when((pids[0] == grid[0] - 1) & (pids[1] == grid[1] - 1) & (pids[2] == grid[2] - 1)
                    & (pids[3] == grid[3] - 1))(finish)

    any_spec = pl.BlockSpec(memory_space=pl.ANY)
    scratch_shapes = [] if nk == 1 else [pltpu.VMEM((tm, tn), F32)]
    out_shape = [jax.ShapeDtypeStruct((g, m, n), out_dtype)]
    if exchange is not None:
        scratch_shapes += [pltpu.SemaphoreType.DMA((exchange.n_sem,)), pltpu.SemaphoreType.DMA((exchange.n_sem,)),
                           pltpu.SemaphoreType.DMA((n_ex,))]
        out_shape += exchange.out_shapes
    outs = pl.pallas_call(
        body, name=name, grid=grid,
        in_specs=[a_spec, b_spec] + [any_spec] * n_ex,
        out_specs=tuple([pl.BlockSpec((None, tm, tn), lambda gi, i, j, kk: (gi, i, j))] + [any_spec] * n_ex),
        out_shape=tuple(out_shape),
        scratch_shapes=scratch_shapes,
        compiler_params=(_params("parallel", "parallel", "parallel", "arbitrary") if exchange is None else
                         _params("arbitrary", "arbitrary", "arbitrary", "arbitrary")),
    )(a, b, *([] if exchange is None else exchange.arrays))
    return outs[0] if exchange is None else outs


def _rms_fwd(x, w):
    s, d = x.shape
    tm = _row_block(s, 512)

    def body(x_ref, w_ref, h_ref):
        h_ref[...] = _rmsnorm(x_ref[...], w_ref[...]).astype(BF16)

    return pl.pallas_call(
        body, name="input_rmsnorm", grid=(s // tm,),
        in_specs=[pl.BlockSpec((tm, d), lambda i: (i, 0)), pl.BlockSpec((1, d), lambda i: (0, 0))],
        out_specs=pl.BlockSpec((tm, d), lambda i: (i, 0)),
        out_shape=jax.ShapeDtypeStruct((s, d), BF16),
        compiler_params=_params("parallel"),
    )(x, w)


def _rms_bwd(x, w, dh_stacked, dh_parts, dx_res):
    s, d = x.shape
    tm = _row_block(s, 256)
    n_parts = 1 + len(dh_parts)

    def body(x_ref, w_ref, *rest):
        part_refs = rest[:n_parts]
        res_ref, gx_ref, gw_ref = rest[n_parts:]
        dh = part_refs[0][...]
        for r in part_refs[1:]:
            dh = dh + r[...]
        _, vjp = jax.vjp(_rmsnorm, x_ref[...], w_ref[...])
        dx, dw = vjp(dh)
        gx_ref[...] = dx + res_ref[...]

        @pl.when(pl.program_id(0) == 0)
        def _():
            gw_ref[...] = jnp.zeros_like(gw_ref)

        gw_ref[...] += dw

    row = pl.BlockSpec((tm, d), lambda i: (i, 0))
    vec = pl.BlockSpec((1, d), lambda i: (0, 0))
    return pl.pallas_call(
        body, name="input_rmsnorm_bwd", grid=(s // tm,),
        in_specs=[row, vec, pl.BlockSpec((None, tm, d), lambda i: (0, i, 0))] + [row] * (n_parts - 1) + [row],
        out_specs=(row, vec),
        out_shape=(jax.ShapeDtypeStruct((s, d), F32), jax.ShapeDtypeStruct((1, d), F32)),
        compiler_params=_params("arbitrary"),
    )(x, w, dh_stacked, *dh_parts, dx_res)


def _lane_masks(rows):
    lane = lax.broadcasted_iota(jnp.int32, (rows, LANES), 1)
    return lane < HEAD_DIM, (lane & (HEAD_DIM - 1)) < HEAD_DIM // 2


def _swap_halves(t, lo_half):
    return jnp.where(lo_half, pltpu.roll(t, LANES - HEAD_DIM // 2, 1), pltpu.roll(t, HEAD_DIM // 2, 1))


def _rope(t, cos, sin_signed, lo_half):
    return t * cos + _swap_halves(t, lo_half) * sin_signed


def _rope_bwd(d, cos, sin_signed, lo_half):
    return d * cos - _swap_halves(d, lo_half) * sin_signed


def _window_mask(first):
    qi = lax.broadcasted_iota(jnp.int32, (2 * ATT_BLOCK, 2 * ATT_BLOCK), 0) & (ATT_BLOCK - 1)
    kj = lax.broadcasted_iota(jnp.int32, (2 * ATT_BLOCK, 2 * ATT_BLOCK), 1)
    dist = qi + ATT_BLOCK - kj
    return (dist >= 0) & (dist <= N_BACK) & ((kj >= ATT_BLOCK) | jnp.logical_not(first))


def _stack_heads(t, head0):
    zero = jnp.zeros_like(t)
    return jnp.concatenate([jnp.where(head0, t, zero), jnp.where(head0, zero, t)], axis=0)


def _unstack_heads(t2, head0):
    return jnp.where(head0, t2[:ATT_BLOCK], t2[ATT_BLOCK:])


def _blocks_per_subsequence(g, nb):
    return lax.shift_right_logical(jnp.int32(nb), 2 * g)


def _attn_fwd(qkv, cos, sin):
    _, s, _ = qkv.shape
    nb = s // ATT_BLOCK

    def body(qkv_ref, cos_ref, sin_ref, o_ref, lse_ref, kp_ref, vp_ref):
        g, t = pl.program_id(0), pl.program_id(1)
        first = (t & (_blocks_per_subsequence(g, nb) - 1)) == 0

        @pl.when(first)
        def _():
            kp_ref[...] = jnp.zeros_like(kp_ref)
            vp_ref[...] = jnp.zeros_like(vp_ref)

        cos_b, sin_b = cos_ref[...], sin_ref[...]
        head0, lo_half = _lane_masks(ATT_BLOCK)
        valid = _window_mask(first)
        for sl in range(WIDTH // LANES):
            cq = pl.ds(sl * LANES, LANES)
            ck = pl.ds(WIDTH + sl * LANES, LANES)
            cv = pl.ds(2 * WIDTH + sl * LANES, LANES)
            qr = (_rope(qkv_ref[:, cq], cos_b, sin_b, lo_half) * (HEAD_DIM ** -0.5)).astype(BF16)
            kr = _rope(qkv_ref[:, ck], cos_b, sin_b, lo_half).astype(BF16)
            v16 = qkv_ref[:, cv].astype(BF16)
            kcat = jnp.concatenate([kp_ref[:, cq], kr], axis=0)
            vcat = jnp.concatenate([vp_ref[:, cq], v16], axis=0)
            sc = jnp.where(valid, _dg(_stack_heads(qr, head0), kcat, 1, 1), -jnp.inf)
            mx = jnp.max(sc, axis=1, keepdims=True)
            p = jnp.exp(sc - mx)
            den = jnp.sum(p, axis=1, keepdims=True)
            o_ref[:, cq] = _unstack_heads(_dg((p * (1.0 / den)).astype(BF16), vcat, 1, 0), head0)
            lse2 = mx + jnp.log(den)
            lse_ref[:, cq] = jnp.where(head0, lse2[:ATT_BLOCK], lse2[ATT_BLOCK:])
            kp_ref[:, cq] = kr
            vp_ref[:, cq] = v16

    blk = lambda w: pl.BlockSpec((None, ATT_BLOCK, w), lambda g, t: (g, t, 0))
    shp = jax.ShapeDtypeStruct((GROUPS, s, WIDTH), F32)
    return pl.pallas_call(
        body, name="dilated_attention_fwd", grid=(GROUPS, nb),
        in_specs=[blk(3 * WIDTH), blk(LANES), blk(LANES)],
        out_specs=(blk(WIDTH), blk(WIDTH)), out_shape=(shp, shp),
        scratch_shapes=[pltpu.VMEM((ATT_BLOCK, WIDTH), BF16), pltpu.VMEM((ATT_BLOCK, WIDTH), BF16)],
        compiler_params=_params("arbitrary", "arbitrary"),
    )(qkv, cos, sin)


def _attn_bwd(qkv, cos, sin, o, lse, do, dlse):
    _, s, _ = qkv.shape
    nb = s // ATT_BLOCK

    def body(qkv_ref, cos_ref, sin_ref, cosp_ref, sinp_ref, o_ref, lse_ref, do_ref, dlse_ref,
             dqkv_ref, kp_ref, vp_ref, dka_ref, dva_ref, dqp_ref):
        g, t = pl.program_id(0), pl.program_id(1)
        first = (t & (_blocks_per_subsequence(g, nb) - 1)) == 0
        active = t < nb
        head0, lo_half = _lane_masks(ATT_BLOCK)
        cos_p, sin_p = cosp_ref[...], sinp_ref[...]

        @pl.when(t == 0)
        def _():
            dka_ref[...] = jnp.zeros_like(dka_ref)
            dva_ref[...] = jnp.zeros_like(dva_ref)
            dqp_ref[...] = jnp.zeros_like(dqp_ref)

        dqkv_ref[:, pl.ds(0, WIDTH)] = dqp_ref[...]

        @pl.when(active & first)
        def _():
            kp_ref[...] = jnp.zeros_like(kp_ref)
            vp_ref[...] = jnp.zeros_like(vp_ref)

        @pl.when(active)
        def _():
            cos_b, sin_b = cos_ref[...], sin_ref[...]
            valid = _window_mask(first)
            for sl in range(WIDTH // LANES):
                cq = pl.ds(sl * LANES, LANES)
                ck = pl.ds(WIDTH + sl * LANES, LANES)
                cv = pl.ds(2 * WIDTH + sl * LANES, LANES)
                qr = (_rope(qkv_ref[:, cq], cos_b, sin_b, lo_half) * (HEAD_DIM ** -0.5)).astype(BF16)
                kr = _rope(qkv_ref[:, ck], cos_b, sin_b, lo_half).astype(BF16)
                v16 = qkv_ref[:, cv].astype(BF16)
                kcat = jnp.concatenate([kp_ref[:, cq], kr], axis=0)
                vcat = jnp.concatenate([vp_ref[:, cq], v16], axis=0)
                do_b = do_ref[:, cq]
                do16 = do_b.astype(BF16)
                lse_b = lse_ref[:, cq]
                cterm = dlse_ref[:, cq] - do_b * o_ref[:, cq]
                dqs, dkc, dvc = [], None, None
                for hm in (head0, jnp.logical_not(head0)):
                    qm = jnp.where(hm, qr, jnp.zeros_like(qr))
                    dom = jnp.where(hm, do16, jnp.zeros_like(do16))
                    sc = jnp.where(valid[:ATT_BLOCK], _dg(qm, kcat, 1, 1), -jnp.inf)
                    lse_h = jnp.max(jnp.where(hm, lse_b, -jnp.inf), axis=1, keepdims=True)
                    c = jnp.sum(jnp.where(hm, cterm, 0.0), axis=1, keepdims=True)
                    p = jnp.exp(sc - lse_h)
                    ds16 = (p * (_dg(dom, vcat, 1, 1) + c)).astype(BF16)
                    dv_h, dk_h = _dg(p.astype(BF16), dom, 0, 0), _dg(ds16, qm, 0, 0)
                    dvc = dv_h if dvc is None else dvc + dv_h
                    dkc = dk_h if dkc is None else dkc + dk_h
                    dqs.append(_dg(ds16, kcat, 1, 0))
                dq = jnp.where(head0, dqs[0], dqs[1]) * (HEAD_DIM ** -0.5)
                dqp_ref[:, cq] = _rope_bwd(dq, cos_b, sin_b, lo_half).astype(BF16)
                dqkv_ref[:, ck] = _rope_bwd(dka_ref[:, cq] + dkc[:ATT_BLOCK], cos_p, sin_p, lo_half).astype(BF16)
                dqkv_ref[:, cv] = (dva_ref[:, cq] + dvc[:ATT_BLOCK]).astype(BF16)
                dka_ref[:, cq] = dkc[ATT_BLOCK:]
                dva_ref[:, cq] = dvc[ATT_BLOCK:]
                kp_ref[:, cq] = kr
                vp_ref[:, cq] = v16

        @pl.when(jnp.logical_not(active))
        def _():
            for sl in range(WIDTH // LANES):
                cq = pl.ds(sl * LANES, LANES)
                dqkv_ref[:, pl.ds(WIDTH + sl * LANES, LANES)] = _rope_bwd(dka_ref[:, cq], cos_p, sin_p, lo_half).astype(BF16)
                dqkv_ref[:, pl.ds(2 * WIDTH + sl * LANES, LANES)] = dva_ref[:, cq].astype(BF16)

    cur = lambda w: pl.BlockSpec((None, ATT_BLOCK, w), lambda g, t: (g, jnp.minimum(t, nb - 1), 0))
    prev = lambda w: pl.BlockSpec((None, ATT_BLOCK, w), lambda g, t: (g, jnp.maximum(t - 1, 0), 0))
    return pl.pallas_call(
        body, name="dilated_attention_bwd", grid=(GROUPS, nb + 1),
        in_specs=[cur(3 * WIDTH), cur(LANES), cur(LANES), prev(LANES), prev(LANES),
                  cur(WIDTH), cur(WIDTH), cur(WIDTH), cur(WIDTH)],
        out_specs=prev(3 * WIDTH), out_shape=jax.ShapeDtypeStruct((GROUPS, s, 3 * WIDTH), BF16),
        scratch_shapes=[pltpu.VMEM((ATT_BLOCK, WIDTH), BF16), pltpu.VMEM((ATT_BLOCK, WIDTH), BF16),
                        pltpu.VMEM((ATT_BLOCK, WIDTH), F32), pltpu.VMEM((ATT_BLOCK, WIDTH), F32),
                        pltpu.VMEM((ATT_BLOCK, WIDTH), BF16)],
        compiler_params=_params("arbitrary", "arbitrary"),
    )(qkv, cos, sin, cos, sin, o, lse, do, dlse)


CONV_PAD = SUBLANES


def _gdn_post(y, is_q, is_k):
    head0, _ = _lane_masks(y.shape[0])
    c = _silu(y)
    sq = c * c
    ss0 = jnp.sum(jnp.where(head0, sq, 0.0), axis=1, keepdims=True)
    ss1 = jnp.sum(jnp.where(head0, 0.0, sq), axis=1, keepdims=True)
    r = jnp.where(head0, lax.rsqrt(ss0 + NORM_EPS), lax.rsqrt(ss1 + NORM_EPS))
    scale = jnp.where(is_q, HEAD_DIM ** -0.5, 1.0).astype(F32)
    return jnp.where(is_q | is_k, c * r * scale, c)


def _conv_rows(xp_ref, w, c0, rows):
    y = w[0:1, :] * xp_ref[pl.ds(c0 + CONV_PAD - (CONV_K - 1), rows), :]
    for k in range(1, CONV_K):
        y = y + w[k:k + 1, :] * xp_ref[pl.ds(c0 + CONV_PAD - (CONV_K - 1) + k, rows), :]
    return y


def _gdn_pre_fwd(proj_r, conv8, col0):
    s = proj_r.shape[0]
    tr = _row_block(s, 512)
    nblk = QKV_B // LANES
    nq = WIDTH // LANES

    def body(x_ref, w_ref, out_ref, xp_ref):
        j = pl.program_id(0)
        is_q, is_k = j < nq, (j >= nq) & (j < 2 * nq)
        xp_ref[pl.ds(0, CONV_PAD), :] = jnp.zeros((CONV_PAD, LANES), F32)
        xp_ref[pl.ds(CONV_PAD, s), :] = x_ref[...]
        w = w_ref[...]
        for c in range(s // tr):
            out_ref[pl.ds(c * tr, tr), :] = _gdn_post(_conv_rows(xp_ref, w, c * tr, tr), is_q, is_k)

    return pl.pallas_call(
        body, name="gdn_conv_fwd", grid=(nblk,),
        in_specs=[pl.BlockSpec((s, LANES), lambda j: (0, col0 + j)), pl.BlockSpec((SUBLANES, LANES), lambda j: (0, j))],
        out_specs=pl.BlockSpec((s, LANES), lambda j: (0, j)),
        out_shape=jax.ShapeDtypeStruct((s, QKV_B), F32),
        scratch_shapes=[pltpu.VMEM((s + CONV_PAD, LANES), F32)],
        compiler_params=_params("parallel"),
    )(proj_r, conv8)


def _gdn_pre_bwd(proj_r, conv8, dc, col0):
    s = proj_r.shape[0]
    tr = _row_block(s, 512)
    nblk = QKV_B // LANES
    nq = WIDTH // LANES

    def body(x_ref, w_ref, dc_ref, dx_ref, dw_ref, xp_ref, dyp_ref):
        j = pl.program_id(0)
        is_q, is_k = j < nq, (j >= nq) & (j < 2 * nq)
        xp_ref[pl.ds(0, CONV_PAD), :] = jnp.zeros((CONV_PAD, LANES), F32)
        xp_ref[pl.ds(CONV_PAD, s), :] = x_ref[...]
        dyp_ref[pl.ds(s, CONV_PAD), :] = jnp.zeros((CONV_PAD, LANES), F32)
        w = w_ref[...]
        for c in range(s // tr):
            y = _conv_rows(xp_ref, w, c * tr, tr)
            _, vjp = jax.vjp(lambda yy: _gdn_post(yy, is_q, is_k), y)
            dyp_ref[pl.ds(c * tr, tr), :] = vjp(dc_ref[pl.ds(c * tr, tr), :])[0]
        dws = [jnp.zeros((1, LANES), F32) for _ in range(CONV_K)]
        for c in range(s // tr):
            c0 = c * tr
            dy = dyp_ref[pl.ds(c0, tr), :]
            dx = w[0:1, :] * dyp_ref[pl.ds(c0 + CONV_K - 1, tr), :]
            for k in range(1, CONV_K):
                dx = dx + w[k:k + 1, :] * dyp_ref[pl.ds(c0 + CONV_K - 1 - k, tr), :]
            dx_ref[pl.ds(c0, tr), :] = dx.astype(BF16)
            for k in range(CONV_K):
                xs = xp_ref[pl.ds(c0 + CONV_PAD - (CONV_K - 1) + k, tr), :]
                dws[k] = dws[k] + jnp.sum(dy * xs, axis=0, keepdims=True)
        row = lax.broadcasted_iota(jnp.int32, (SUBLANES, LANES), 0)
        dwb = jnp.zeros((SUBLANES, LANES), F32)
        for k in range(CONV_K):
            dwb = dwb + jnp.where(row == k, dws[k], 0.0)
        dw_ref[...] = dwb

    return pl.pallas_call(
        body, name="gdn_conv_bwd", grid=(nblk,),
        in_specs=[pl.BlockSpec((s, LANES), lambda j: (0, col0 + j)), pl.BlockSpec((SUBLANES, LANES), lambda j: (0, j)),
                  pl.BlockSpec((s, LANES), lambda j: (0, j))],
        out_specs=(pl.BlockSpec((s, LANES), lambda j: (0, j)), pl.BlockSpec((SUBLANES, LANES), lambda j: (0, j))),
        out_shape=(jax.ShapeDtypeStruct((s, QKV_B), BF16), jax.ShapeDtypeStruct((SUBLANES, QKV_B), F32)),
        scratch_shapes=[pltpu.VMEM((s + CONV_PAD, LANES), F32), pltpu.VMEM((s + CONV_PAD, LANES), F32)],
        compiler_params=_params("parallel"),
    )(proj_r, conv8, dc)


def _gdn_chunk(q, k, v, bcol, acol, alog, dtb, gnw, state, t_saved=None):
    n = q.shape[-2]
    shp = (1, n, n)
    row = lax.broadcasted_iota(jnp.int32, shp, 1)
    col = lax.broadcasted_iota(jnp.int32, shp, 2)
    beta = _sigmoid(bcol)
    g = -jnp.exp(alog) * _softplus(acol + dtb)
    g_row = jnp.sum(jnp.where(row == col, g, 0.0), axis=-2, keepdims=True)
    big_g = jnp.sum(jnp.where(row >= col, g_row, 0.0), axis=-1, keepdims=True)
    big_g_row = jnp.sum(jnp.where(row <= col, g, 0.0), axis=-2, keepdims=True)
    decay_incl = jnp.exp(jnp.where(row >= col, big_g - big_g_row, -jnp.inf))
    decay_strict = jnp.where(row > col, decay_incl, 0.0)
    k_beta = k * beta
    a_mat = _mm_nt(k_beta, k) * decay_strict
    t_inv = _tri_inv(a_mat) if t_saved is None else _tri_inv_saved(a_mat, t_saved)
    e_g = jnp.exp(big_g)
    u = _mm(t_inv, v * beta)
    w = _mm(t_inv, k_beta * e_g)
    attn = _mm_nt(q, k) * decay_incl
    v_new = u - _mm(w, state)
    o = _mm(q * e_g, state) + _mm(attn, v_new)
    total = jnp.sum(g, axis=-2, keepdims=True)
    new_state = state * jnp.exp(total) + _mm_tn(k * jnp.exp(total - big_g), v_new)
    return _rmsnorm(o, gnw), new_state, t_inv


def _split_heads(x):
    return jnp.stack([x[:, h * HEAD_DIM:(h + 1) * HEAD_DIM] for h in range(HEADS)], axis=0)


def _merge_heads(x):
    return jnp.concatenate([x[h] for h in range(HEADS)], axis=1)


def _logit_columns(ba):
    lane = lax.broadcasted_iota(jnp.int32, ba.shape, 1)

    def cols(off):
        return jnp.stack([jnp.sum(jnp.where(lane == off + h, ba, 0.0), axis=1, keepdims=True) for h in range(HEADS)], axis=0)

    return cols(0), cols(HEADS)


def _logit_block(dbc, dac, shape):
    lane = lax.broadcasted_iota(jnp.int32, shape, 1)
    out = jnp.zeros(shape, F32)
    for h in range(HEADS):
        out = out + jnp.where(lane == h, dbc[h], 0.0) + jnp.where(lane == HEADS + h, dac[h], 0.0)
    return out


def _gdn_scan_fwd(cqkv, proj_r, ba_col, alog, dtb, gnw):
    s = cqkv.shape[0]
    nc = s // CHUNK

    def body(q_ref, k_ref, v_ref, ba_ref, al_ref, dt_ref, gnw_ref, o_ref, st_ref, ti_ref, state_ref):
        @pl.when(pl.program_id(0) == 0)
        def _():
            state_ref[...] = jnp.zeros_like(state_ref)

        st = state_ref[...]
        st_ref[...] = st
        bcol, acol = _logit_columns(ba_ref[...])
        o, new_st, t_inv = _gdn_chunk(_split_heads(q_ref[...]), _split_heads(k_ref[...]), _split_heads(v_ref[...]),
                                      bcol, acol, al_ref[...], dt_ref[...], gnw_ref[...], st)
        o_ref[...] = _merge_heads(o)
        ti_ref[...] = t_inv
        state_ref[...] = new_st

    part = lambda i: pl.BlockSpec((CHUNK, WIDTH), lambda n: (n, i))
    par = pl.BlockSpec((HEADS, 1, 1), lambda n: (0, 0, 0))
    per_chunk = pl.BlockSpec((None, HEADS, HEAD_DIM, HEAD_DIM), lambda n: (n, 0, 0, 0))
    per_chunk_shape = jax.ShapeDtypeStruct((nc, HEADS, HEAD_DIM, HEAD_DIM), F32)
    return pl.pallas_call(
        body, name="gdn_scan_fwd", grid=(nc,),
        in_specs=[part(0), part(1), part(2), pl.BlockSpec((CHUNK, LANES), lambda n: (n, ba_col)), par, par,
                  pl.BlockSpec((1, 1, HEAD_DIM), lambda n: (0, 0, 0))],
        out_specs=(part(0), per_chunk, per_chunk),
        out_shape=(jax.ShapeDtypeStruct((s, WIDTH), F32), per_chunk_shape, per_chunk_shape),
        scratch_shapes=[pltpu.VMEM((HEADS, HEAD_DIM, HEAD_DIM), F32)],
        compiler_params=_params("arbitrary"),
    )(cqkv, cqkv, cqkv, proj_r, alog, dtb, gnw)


def _gdn_scan_bwd(cqkv, proj_r, ba_col, alog, dtb, gnw, states, t_invs, do):
    s = cqkv.shape[0]
    nc = s // CHUNK

    def body(q_ref, k_ref, v_ref, ba_ref, al_ref, dt_ref, gnw_ref, st_ref, ti_ref, do_ref,
             dqkv_ref, dba_ref, dal_ref, ddt_ref, dgnw_ref, dstate_ref):
        @pl.when(pl.program_id(0) == 0)
        def _():
            dstate_ref[...] = jnp.zeros_like(dstate_ref)
            dal_ref[...] = jnp.zeros_like(dal_ref)
            ddt_ref[...] = jnp.zeros_like(ddt_ref)
            dgnw_ref[...] = jnp.zeros_like(dgnw_ref)

        bcol, acol = _logit_columns(ba_ref[...])
        t_saved = ti_ref[...]
        _, vjp = jax.vjp(lambda *a: _gdn_chunk(*a, t_saved=t_saved)[:2],
                         _split_heads(q_ref[...]), _split_heads(k_ref[...]), _split_heads(v_ref[...]),
                         bcol, acol, al_ref[...], dt_ref[...], gnw_ref[...], st_ref[...])
        dq, dk, dv, dbc, dac, dal, ddt, dgn, dst = vjp((_split_heads(do_ref[...]), dstate_ref[...]))
        dqkv_ref[:, pl.ds(0, WIDTH)] = _merge_heads(dq)
        dqkv_ref[:, pl.ds(WIDTH, WIDTH)] = _merge_heads(dk)
        dqkv_ref[:, pl.ds(2 * WIDTH, WIDTH)] = _merge_heads(dv)
        dba_ref[...] = _logit_block(dbc, dac, dba_ref.shape)
        dstate_ref[...] = dst
        dal_ref[...] += dal
        ddt_ref[...] += ddt
        dgnw_ref[...] += dgn

    rev = lambda n: nc - 1 - n
    part = lambda i: pl.BlockSpec((CHUNK, WIDTH), lambda n: (rev(n), i))
    par = pl.BlockSpec((HEADS, 1, 1), lambda n: (0, 0, 0))
    vec = pl.BlockSpec((1, 1, HEAD_DIM), lambda n: (0, 0, 0))
    par_shape = jax.ShapeDtypeStruct((HEADS, 1, 1), F32)
    per_chunk = pl.BlockSpec((None, HEADS, HEAD_DIM, HEAD_DIM), lambda n: (rev(n), 0, 0, 0))
    return pl.pallas_call(
        body, name="gdn_scan_bwd", grid=(nc,),
        in_specs=[part(0), part(1), part(2), pl.BlockSpec((CHUNK, LANES), lambda n: (rev(n), ba_col)), par, par, vec,
                  per_chunk, per_chunk, part(0)],
        out_specs=(pl.BlockSpec((CHUNK, QKV_B), lambda n: (rev(n), 0)), pl.BlockSpec((CHUNK, LANES), lambda n: (rev(n), 0)),
                   par, par, vec),
        out_shape=(jax.ShapeDtypeStruct((s, QKV_B), F32), jax.ShapeDtypeStruct((s, LANES), F32), par_shape, par_shape,
                   jax.ShapeDtypeStruct((1, 1, HEAD_DIM), F32)),
        scratch_shapes=[pltpu.VMEM((HEADS, HEAD_DIM, HEAD_DIM), F32)],
        compiler_params=_params("arbitrary"),
    )(cqkv, cqkv, cqkv, proj_r, alog, dtb, gnw, states, t_invs, do)


def _tail_loss(x, tgt, o0, o1, o2, l0, l1, l2, ga, gb, za, zb, ob, fnw, wua, wub, wo, tap_a, tap_b, tap_o):
    lm = jnp.maximum(jnp.maximum(l0, l1), l2)
    e0, e1, e2 = jnp.exp(l0 - lm), jnp.exp(l1 - lm), jnp.exp(l2 - lm)
    o_a = (e0 * o0 + e1 * o1 + e2 * o2) / (e0 + e1 + e2)
    y_a = _mm_tap(o_a * _silu(za), wua, tap_a)
    y_b = _mm_tap(ob * _silu(zb), wub, tap_b)
    merged = _sigmoid(ga) * y_a + _sigmoid(gb) * y_b
    y = _rmsnorm(x + _mm_tap(merged, wo, tap_o), fnw)
    err = y - tgt
    per_token = jnp.sum(err * err, axis=1, keepdims=True) * (0.5 / x.shape[1])
    return jnp.sum(per_token, axis=0, keepdims=True)


def _tail(x, tgt, o_all, lse_all, og12, lg12, proj_r, ob, wua, wub, wo, fnw):
    s, d = x.shape
    tm = _row_block(s, 128)
    col_za = 2 * d // WIDTH
    col_zb = (2 * d + WIDTH + QKV_B) // WIDTH

    def body(x_ref, t_ref, o0_ref, o1_ref, o2_ref, l0_ref, l1_ref, l2_ref, ga_ref, gb_ref, za_ref, zb_ref, ob_ref,
             wua_ref, wub_ref, wo_ref, fnw_ref,
             loss_ref, dx_ref, do0_ref, do1_ref, do2_ref, dl0_ref, dl1_ref, dl2_ref, dga_ref, dgb_ref, dza_ref,
             dzb_ref, dob_ref, dwua_ref, dwub_ref, dwo_ref, dfnw_ref):
        @pl.when(pl.program_id(0) == 0)
        def _():
            for r in (loss_ref, dwua_ref, dwub_ref, dwo_ref, dfnw_ref):
                r[...] = jnp.zeros_like(r)

        args = (x_ref[...], t_ref[...], o0_ref[...], o1_ref[...], o2_ref[...], l0_ref[...], l1_ref[...], l2_ref[...],
                ga_ref[...], gb_ref[...], za_ref[...], zb_ref[...], ob_ref[...], fnw_ref[...],
                wua_ref[...], wub_ref[...], wo_ref[...],
                jnp.zeros(wua_ref.shape, F32), jnp.zeros(wub_ref.shape, F32), jnp.zeros(wo_ref.shape, F32))
        loss, vjp = jax.vjp(_tail_loss, *args)
        (dx, _, do0, do1, do2, dl0, dl1, dl2, dga, dgb, dza, dzb, dob, dfnw, _, _, _, dwua, dwub, dwo) = vjp(
            jnp.ones((1, 1), F32))
        loss_ref[...] += jnp.broadcast_to(loss, loss_ref.shape)
        dx_ref[...] = dx
        do0_ref[...], do1_ref[...], do2_ref[...] = do0, do1, do2
        dl0_ref[...], dl1_ref[...], dl2_ref[...] = dl0, dl1, dl2
        dga_ref[...] = dga.astype(BF16)
        dgb_ref[...] = dgb.astype(BF16)
        dza_ref[...] = dza.astype(BF16)
        dzb_ref[...] = dzb.astype(BF16)
        dob_ref[...] = dob
        dwua_ref[...] += dwua
        dwub_ref[...] += dwub
        dwo_ref[...] += dwo
        dfnw_ref[...] += dfnw

    row = lambda w, c=0: pl.BlockSpec((tm, w), lambda i: (i, c))
    grp0 = pl.BlockSpec((None, tm, WIDTH), lambda i: (0, i, 0))
    full = lambda a, b: pl.BlockSpec((a, b), lambda i: (0, 0))
    f32 = lambda a, b: jax.ShapeDtypeStruct((a, b), F32)
    b16 = lambda a, b: jax.ShapeDtypeStruct((a, b), BF16)
    stacked = jax.ShapeDtypeStruct((GROUPS, s, WIDTH), F32)
    gspecs = [grp0, row(WIDTH), row(WIDTH)]
    in_specs = ([row(d), row(d)] + gspecs * 2 + [row(d, 0), row(d, 1), row(WIDTH, col_za), row(WIDTH, col_zb),
                row(WIDTH), full(WIDTH, d), full(WIDTH, d), full(d, d), full(1, d)])
    out_specs = ([full(SUBLANES, LANES), row(d)] + gspecs * 2 + [row(d), row(d), row(WIDTH), row(WIDTH), row(WIDTH),
                 full(WIDTH, d), full(WIDTH, d), full(d, d), full(1, d)])
    gshapes = [stacked, f32(s, WIDTH), f32(s, WIDTH)]
    out_shape = ([f32(SUBLANES, LANES), f32(s, d)] + gshapes * 2 + [b16(s, d), b16(s, d), b16(s, WIDTH),
                 b16(s, WIDTH), f32(s, WIDTH), f32(WIDTH, d), f32(WIDTH, d), f32(d, d), f32(1, d)])
    return pl.pallas_call(
        body, name="tail_fwd_bwd", grid=(s // tm,),
        in_specs=in_specs, out_specs=tuple(out_specs), out_shape=tuple(out_shape),
        compiler_params=_params("arbitrary"),
    )(x, tgt, o_all, og12[0], og12[1], lse_all, lg12[0], lg12[1], proj_r, proj_r, proj_r, proj_r, ob, wua, wub, wo, fnw)


PERMUTE_SPAN = 4096


def _permute_span(s):
    return PERMUTE_SPAN if s % PERMUTE_SPAN == 0 else s


def _from_dilated_rows(stacked, g, dil, name):
    n_slots, s, c = stacked.shape
    view = stacked.reshape(n_slots, dil, s // dil, c)
    span = _permute_span(s)

    def body(in_ref, out_ref):
        for r in range(dil):
            out_ref[pl.ds(r, span // dil, stride=dil), :] = in_ref[r]

    return pl.pallas_call(
        body, name=name, grid=(s // span, c // LANES),
        in_specs=[pl.BlockSpec((None, dil, span // dil, LANES), lambda n, j: (g, 0, n, j))],
        out_specs=pl.BlockSpec((span, LANES), lambda n, j: (n, j)),
        out_shape=jax.ShapeDtypeStruct((s, c), stacked.dtype),
        compiler_params=_params("parallel", "parallel"),
    )(view)


def _to_dilated_rows_into(nat, stacked, g, dil, name):
    n_slots, s, c = stacked.shape
    view = stacked.reshape(n_slots, dil, s // dil, c)
    span = _permute_span(s)

    def body(nat_ref, old_ref, out_ref):
        for r in range(dil):
            out_ref[r] = nat_ref[pl.ds(r, span // dil, stride=dil), :]

    out = pl.pallas_call(
        body, name=name, grid=(s // span, c // LANES),
        in_specs=[pl.BlockSpec((span, LANES), lambda n, j: (n, j)), pl.BlockSpec(memory_space=pl.ANY)],
        out_specs=pl.BlockSpec((None, dil, span // dil, LANES), lambda n, j: (g, 0, n, j)),
        out_shape=jax.ShapeDtypeStruct(view.shape, stacked.dtype),
        input_output_aliases={1: 0},
        compiler_params=_params("parallel", "parallel"),
    )(nat, view)
    return out.reshape(stacked.shape)


def _to_dilated(a, dil):
    if dil == 1:
        return a
    s = a.shape[0]
    return a.reshape(s // dil, dil, -1).transpose(1, 0, 2).reshape(a.shape)


def _from_dilated(a, dil):
    if dil == 1:
        return a
    s = a.shape[0]
    return a.reshape(dil, s // dil, -1).transpose(1, 0, 2).reshape(a.shape)


def _head_major(a):
    return a.reshape(a.shape[0], HEADS, HEAD_DIM).transpose(1, 0, 2)


def _from_head_major(a):
    return a.transpose(1, 0, 2).reshape(a.shape[1], WIDTH)


def _rope_tables(s):
    inv_freq = ROPE_THETA ** (-jnp.arange(0, HEAD_DIM, 2, dtype=F32) / HEAD_DIM)
    ang = jnp.arange(s, dtype=F32)[:, None] * inv_freq[None, :]
    cos_n = jnp.tile(jnp.cos(ang), (1, 2 * LANES // HEAD_DIM))
    sin_h = jnp.sin(ang)
    sin_n = jnp.tile(jnp.concatenate([-sin_h, sin_h], axis=1), (1, LANES // HEAD_DIM))
    def per_group(table, tag):
        out = jnp.broadcast_to(table, (GROUPS,) + table.shape)
        for g in range(1, GROUPS):
            out = _to_dilated_rows_into(table, out, g, DILATIONS[g], "rope_%s_to_dilated_%d" % (tag, g))
        return out

    return per_group(cos_n, "cos"), per_group(sin_n, "sin")


def _pack_rows(parts, dtype, row_multiple):
    flat = jnp.concatenate([p.reshape(-1).astype(dtype) for p in parts])
    tile = row_multiple * LANES
    pad = (-flat.shape[0]) % tile
    return jnp.pad(flat, (0, pad)).reshape(-1, LANES)


def _unpack_rows(packed, shapes):
    flat = packed.reshape(-1)
    out, start = [], 0
    for shp in shapes:
        size = 1
        for n in shp:
            size *= n
        out.append(flat[start:start + size].reshape(shp))
        start += size
    return out


def kernel(x, norm_w, w_in, conv_w, a_log, dt_bias, gdn_norm_w, w_up_a, w_up_b, w_out, final_norm_w, loss_target, m_norm_w, m_w_in, m_conv_w, m_a_log, m_dt_bias, m_gdn_norm_w, m_w_up_a, m_w_up_b, m_w_out, m_final_norm_w, v_norm_w, v_w_in, v_conv_w, v_a_log, v_dt_bias, v_gdn_norm_w, v_w_up_a, v_w_up_b, v_w_out, v_final_norm_w):
    x2, tgt = x[0], loss_target[0]
    s, d = x2.shape
    me = 4 * lax.axis_index("x") + 2 * lax.axis_index("y") + lax.axis_index("c")
    win8 = w_in.shape[2]
    conv8w = conv_w.shape[2]

    conv_shard = jnp.pad(conv_w[0], ((0, SUBLANES - CONV_K), (0, 0)))
    w_in_g, wua_g, wub_g, wo_g, conv_g = _all_gather(
        [w_in[0].astype(BF16), w_up_a[0].astype(BF16), w_up_b[0].astype(BF16), w_out[0].astype(BF16), conv_shard])
    w_in_f = jnp.concatenate([w_in_g[i] for i in range(N_DEV)], axis=1)
    wua = jnp.concatenate([wua_g[i] for i in range(N_DEV)], axis=1)
    wub = jnp.concatenate([wub_g[i] for i in range(N_DEV)], axis=1)
    wo = wo_g.reshape(d, d)
    conv8 = jnp.concatenate([conv_g[i] for i in range(N_DEV)], axis=1)

    w_qkv = w_in_f[:, :QKV_A].reshape(d, GROUPS, QKV_B).transpose(1, 0, 2)
    w_rest = jnp.concatenate([
        w_in_f[:, OFF_GATE:OFF_GATE + 2 * d], w_in_f[:, OFF_ZA:OFF_ZA + WIDTH], w_in_f[:, OFF_QKVB:OFF_QKVB + QKV_B],
        w_in_f[:, OFF_ZB:OFF_ZB + WIDTH], w_in_f[:, OFF_BA:OFF_BA + 2 * HEADS],
        jnp.zeros((d, BA_PAD - 2 * HEADS), BF16)], axis=1)
    col_qkvb = (2 * d + WIDTH) // LANES
    col_ba = (2 * d + 2 * WIDTH + QKV_B) // LANES

    h = _rms_fwd(x2, norm_w)
    h_all = jnp.stack([_to_dilated(h, dil) for dil in DILATIONS])
    qkv_all = _matmul(h_all, w_qkv, F32, "in_proj_attention", tn=QKV_B)
    proj_r = _matmul(h[None], w_rest[None], F32, "in_proj_rest")[0]
    cos, sin = _rope_tables(s)
    o_all, lse_all = _attn_fwd(qkv_all, cos, sin)
    og12 = [_from_dilated_rows(o_all, g, DILATIONS[g], "attn_out_to_natural_%d" % g) for g in (1, 2)]
    lg12 = [_from_dilated_rows(lse_all, g, DILATIONS[g], "attn_lse_to_natural_%d" % g) for g in (1, 2)]

    cqkv = _gdn_pre_fwd(proj_r, conv8, col_qkvb)
    alog3, dtb3, gnw3 = a_log.reshape(HEADS, 1, 1), dt_bias.reshape(HEADS, 1, 1), gdn_norm_w.reshape(1, 1, HEAD_DIM)
    ob, states, t_invs = _gdn_scan_fwd(cqkv, proj_r, col_ba, alog3, dtb3, gnw3)

    (loss_blk, dx_res, do_all, do1, do2, dl_all, dl1, dl2, dga, dgb, dza, dzb, dob, dwua, dwub, dwo, dfnw) = _tail(
        x2, tgt, o_all, lse_all, og12, lg12, proj_r, ob, wua, wub, wo, final_norm_w.reshape(1, d))

    for g, (t_o, t_l) in ((1, (do1, dl1)), (2, (do2, dl2))):
        do_all = _to_dilated_rows_into(t_o, do_all, g, DILATIONS[g], "attn_dout_to_dilated_%d" % g)
        dl_all = _to_dilated_rows_into(t_l, dl_all, g, DILATIONS[g], "attn_dlse_to_dilated_%d" % g)
    dqkv_all = _attn_bwd(qkv_all, cos, sin, o_all, lse_all, do_all, dl_all)

    dcqkv, dba, dalog3, ddtb3, dgnw3 = _gdn_scan_bwd(cqkv, proj_r, col_ba, alog3, dtb3, gnw3, states, t_invs, dob)
    dqkv_b, dconv8 = _gdn_pre_bwd(proj_r, conv8, dcqkv, col_qkvb)
    dproj_r = jnp.concatenate([dga, dgb, dza, dqkv_b, dzb,
                               jnp.pad(dba.astype(BF16), ((0, 0), (0, BA_PAD - LANES)))], axis=1)

    dw_qkv = _matmul(h_all, dqkv_all, F32, "in_proj_attention_dw", mode="tn", tk=2048)
    dw_rest = _matmul(h[None], dproj_r[None], F32, "in_proj_rest_dw", mode="tn", tk=2048)[0]
    o2 = 2 * d
    dw_in = jnp.concatenate([
        dw_qkv.transpose(1, 0, 2).reshape(d, QKV_A),
        dw_rest[:, o2:o2 + WIDTH], dw_rest[:, o2 + WIDTH:o2 + WIDTH + QKV_B],
        dw_rest[:, o2 + WIDTH + QKV_B:o2 + 2 * WIDTH + QKV_B],
        dw_rest[:, o2 + 2 * WIDTH + QKV_B:o2 + 2 * WIDTH + QKV_B + 2 * HEADS],
        dw_rest[:, :o2]], axis=1)

    def col_slabs(a, width):
        return jnp.stack([a[:, j * width:(j + 1) * width] for j in range(N_DEV)])

    slabs = [col_slabs(dw_in, win8), col_slabs(dwua, d // N_DEV), col_slabs(dwub, d // N_DEV),
             dwo.reshape(N_DEV, d // N_DEV, d)]
    dh_a, *from_sibling = _matmul(dqkv_all, w_qkv, F32, "in_proj_attention_dh", mode="nt", tk=2048,
                                  exchange=_sibling_exchange(slabs))
    core = lax.axis_index("c").astype(jnp.int32).reshape(1)
    partials = [_pair_sum(a, b, core, "grads_pair_sum_%d" % i) for i, (a, b) in enumerate(zip(slabs, from_sibling))]
    dh_r, *contrib = _matmul(dproj_r[None], w_rest[None], F32, "in_proj_rest_dh", mode="nt", tk=2560,
                             exchange=_chip_exchange(partials))
    dh_parts = [dh_r[0]] + [_from_dilated_rows(dh_a, g, DILATIONS[g], "dh_to_natural_%d" % g) for g in (1, 2)]
    grad_x, dnorm_w = _rms_bwd(x2, norm_w, dh_a, dh_parts, dx_res)

    small_parts = [dnorm_w, dfnw, dconv8[:CONV_K], dalog3[:, 0, 0], ddtb3[:, 0, 0], dgnw3[0], loss_blk[0, 0:1]]
    small_rows = [-(-p.size // LANES) for p in small_parts]
    small = jnp.concatenate([jnp.pad(p.reshape(-1), (0, r * LANES - p.size)).reshape(r, LANES)
                             for p, r in zip(small_parts, small_rows)])
    small = jnp.pad(small, ((0, (-small.shape[0]) % SUBLANES), (0, 0)))
    small_sum = _small_all_reduce(small)
    pieces, r0 = [], 0
    for p, r in zip(small_parts, small_rows):
        pieces.append(small_sum[r0:r0 + r].reshape(-1)[:p.size].reshape(p.shape))
        r0 += r
    g_norm_w, g_fnw, g_conv_full, g_alog, g_dtb, g_gnw, loss_sum = pieces
    g_conv = lax.dynamic_slice(g_conv_full, (0, me * conv8w), (CONV_K, conv8w))

    big = [_adamw(c, w[0], m[0], v[0], name) for c, w, m, v, name in (
        (contrib[0], w_in, m_w_in, v_w_in, "adamw_w_in"), (contrib[1], w_up_a, m_w_up_a, v_w_up_a, "adamw_w_up_a"),
        (contrib[2], w_up_b, m_w_up_b, v_w_up_b, "adamw_w_up_b"), (contrib[3], w_out, m_w_out, v_w_out, "adamw_w_out"))]
    g_big, d_big, nm_big, nv_big = ([t[i] for t in big] for i in range(4))

    small_ws = [norm_w, final_norm_w, conv_w, a_log, dt_bias, gdn_norm_w]
    small_ms = [m_norm_w, m_final_norm_w, m_conv_w, m_a_log, m_dt_bias, m_gdn_norm_w]
    small_vs = [v_norm_w, v_final_norm_w, v_conv_w, v_a_log, v_dt_bias, v_gdn_norm_w]
    small_gs = [g_norm_w, g_fnw, g_conv, g_alog, g_dtb, g_gnw]
    small_shapes = [t.shape for t in small_ws]
    sm = _adamw(_pack_rows(small_gs, F32, SUBLANES)[None], _pack_rows(small_ws, F32, SUBLANES),
                _pack_rows(small_ms, F32, SUBLANES), _pack_rows(small_vs, F32, SUBLANES), "adamw_small")
    g_sm, d_sm, nm_sm, nv_sm = (_unpack_rows(t, small_shapes) for t in sm)

    def ordered(bigs, smalls):
        nw, fnw_, cw, al, dtb, gn = smalls
        wi, ua, ub, wo_ = (t[None] for t in bigs)
        return [nw, wi, cw, al, dtb, gn, ua, ub, wo_, fnw_]

    return (loss_sum.reshape(()), grad_x[None], *ordered(g_big, g_sm), *ordered(d_big, d_sm),
            *ordered(nm_big, nm_sm), *ordered(nv_big, nv_sm))
```

```python
import functools

import jax
import jax.numpy as jnp
from jax import lax
from jax.experimental import pallas as pl
from jax.experimental.pallas import tpu as pltpu

F32 = jnp.float32
BF16 = jnp.bfloat16
MESH = pl.DeviceIdType.MESH
N_DEV = 8
LANES = 128
SUBLANES = 8

GROUPS = 3
HEADS = 8
HEAD_DIM = 64
WIDTH = HEADS * HEAD_DIM
ATT_BLOCK = 128
DILATIONS = (1, 4, 16)
N_BACK = 128
CONV_K = 4
CHUNK = 64
SCAN_CHUNKS = 2
QKV_B = 3 * WIDTH
QKV_A = GROUPS * 3 * WIDTH
BA_PAD = 512
NORM_EPS = 1e-6
ROPE_THETA = 10000.0
ADAM_LR, ADAM_B1, ADAM_B2, ADAM_EPS, ADAM_WD, ADAM_STEP = 0.001, 0.9, 0.999, 1e-08, 0.01, 10

VMEM_LIMIT = 56 * 1024 * 1024

OFF_ZA = QKV_A
OFF_QKVB = OFF_ZA + WIDTH
OFF_ZB = OFF_QKVB + QKV_B
OFF_BA = OFF_ZB + WIDTH
OFF_GATE = OFF_BA + 2 * HEADS


def _params(*sem):
    return pltpu.CompilerParams(dimension_semantics=sem, vmem_limit_bytes=VMEM_LIMIT)


def _dg(a, b, ca, cb):
    nb = a.ndim - 2
    batch = tuple(range(nb))
    return lax.dot_general(a, b, (((nb + ca,), (nb + cb,)), (batch, batch)), preferred_element_type=F32)


@jax.custom_vjp
def _mm(a, b):
    return _dg(a.astype(BF16), b.astype(BF16), 1, 0)


def _mm_fwd(a, b):
    return _mm(a, b), (a.astype(BF16), b.astype(BF16))


def _mm_bwd(res, ct):
    a16, b16 = res
    c16 = ct.astype(BF16)
    return _dg(c16, b16, 1, 1), _dg(a16, c16, 0, 0)


_mm.defvjp(_mm_fwd, _mm_bwd)


@jax.custom_vjp
def _mm_nt(a, b):
    return _dg(a.astype(BF16), b.astype(BF16), 1, 1)


def _mm_nt_fwd(a, b):
    return _mm_nt(a, b), (a.astype(BF16), b.astype(BF16))


def _mm_nt_bwd(res, ct):
    a16, b16 = res
    c16 = ct.astype(BF16)
    return _dg(c16, b16, 1, 0), _dg(c16, a16, 0, 0)


_mm_nt.defvjp(_mm_nt_fwd, _mm_nt_bwd)


@jax.custom_vjp
def _mm_tn(a, b):
    return _dg(a.astype(BF16), b.astype(BF16), 0, 0)


def _mm_tn_fwd(a, b):
    return _mm_tn(a, b), (a.astype(BF16), b.astype(BF16))


def _mm_tn_bwd(res, ct):
    a16, b16 = res
    c16 = ct.astype(BF16)
    return _dg(b16, c16, 1, 1), _dg(a16, c16, 1, 0)


_mm_tn.defvjp(_mm_tn_fwd, _mm_tn_bwd)


@jax.custom_vjp
def _mm_tap(a, w16, tap):
    return _dg(a.astype(BF16), w16, 1, 0)


def _mm_tap_fwd(a, w16, tap):
    return _mm_tap(a, w16, tap), (a.astype(BF16), w16)


def _mm_tap_bwd(res, ct):
    a16, w16 = res
    c16 = ct.astype(BF16)
    return _dg(c16, w16, 1, 1), jnp.zeros_like(w16), _dg(a16, c16, 0, 0)


_mm_tap.defvjp(_mm_tap_fwd, _mm_tap_bwd)


def _split16(a):
    hi = a.astype(BF16)
    lo = (a - hi.astype(F32)).astype(BF16)
    return hi, lo


def _dot3(a, b, ca, cb):
    ah, al = _split16(a)
    bh, bl = _split16(b)
    return _dg(ah, bh, ca, cb) + (_dg(ah, bl, ca, cb) + _dg(al, bh, ca, cb))


def _tri_inv_impl(a):
    n = a.shape[-1]
    shp = (1,) * (a.ndim - 2) + (n, n)
    eye = (lax.broadcasted_iota(jnp.int32, shp, a.ndim - 2) == lax.broadcasted_iota(jnp.int32, shp, a.ndim - 1)).astype(F32)
    x = eye - a
    p = a
    for it in range(5):
        dot = _dot3 if it < 2 else (lambda u, v, cu, cv: _dg(u.astype(BF16), v.astype(BF16), cu, cv))
        p = dot(p, p, 1, 0)
        x = x + dot(x, p, 1, 0)
    return x


@jax.custom_vjp
def _tri_inv(a):
    return _tri_inv_impl(a)


def _tri_inv_fwd(a):
    t = _tri_inv_impl(a)
    return t, t


def _tri_inv_bwd(t, ct):
    t16 = t.astype(BF16)
    return (-_dg(_dg(t16, ct.astype(BF16), 0, 0).astype(BF16), t16, 1, 1),)


_tri_inv.defvjp(_tri_inv_fwd, _tri_inv_bwd)


@jax.custom_vjp
def _tri_inv_saved(a, t):
    return t


def _tri_inv_saved_fwd(a, t):
    return t, t


def _tri_inv_saved_bwd(t, ct):
    return _tri_inv_bwd(t, ct) + (jnp.zeros_like(t),)


_tri_inv_saved.defvjp(_tri_inv_saved_fwd, _tri_inv_saved_bwd)


def _sigmoid(x):
    return 1.0 / (1.0 + jnp.exp(-x))


def _silu(x):
    return x * _sigmoid(x)


def _softplus(x):
    return jnp.maximum(x, 0.0) + jnp.log(1.0 + jnp.exp(-jnp.abs(x)))


def _rmsnorm(x, w):
    return x * lax.rsqrt(jnp.mean(x * x, axis=-1, keepdims=True) + NORM_EPS) * w


def _row_block(rows, cap):
    best = None
    for cand in range(SUBLANES, min(rows, cap) + 1, SUBLANES):
        if rows % cand == 0:
            best = cand
    assert best is not None, rows
    return best


def _mesh_peers():
    x, y, c = lax.axis_index("x"), lax.axis_index("y"), lax.axis_index("c")
    me = 4 * x + 2 * y + c
    peers = []
    for k in range(1, N_DEV):
        px = 1 - x if (k >> 2) & 1 else x
        py = 1 - y if (k >> 1) & 1 else y
        pc = 1 - c if k & 1 else c
        peers.append(((px, py, pc), 4 * px + 2 * py + pc))
    return me, peers


N_CHIPS = 4
OTHER_CHIPS = 3


def _chip_peers():
    x, y, c = lax.axis_index("x"), lax.axis_index("y"), lax.axis_index("c")
    return x, y, c, [(1 - x, y), (x, 1 - y), (1 - x, 1 - y)]


def _all_gather(shards):
    n_arr = len(shards)
    per = 1 + 2 * OTHER_CHIPS

    def body(*refs):
        in_refs, out_refs = refs[:n_arr], refs[n_arr:2 * n_arr]
        send_sems, recv_sems, loc_sems = refs[2 * n_arr:]
        x, y, c, chips = _chip_peers()
        me, sibling = (x, y, c), (x, y, 1 - c)

        def slot(px, py, pc):
            return 4 * px + 2 * py + pc

        def copy(i, k, block, to, src=None):
            dst = out_refs[i].at[slot(*block)]
            return pltpu.make_async_remote_copy(src_ref=dst if src is None else src, dst_ref=dst,
                                                send_sem=send_sems.at[i * per + k], recv_sem=recv_sems.at[i * per + k],
                                                device_id=to, device_id_type=MESH)

        own = [pltpu.make_async_copy(in_refs[i], out_refs[i].at[slot(*me)], loc_sems.at[i]) for i in range(n_arr)]
        for cp in own:
            cp.start()
        first = []
        for i in range(n_arr):
            first += [copy(i, 1 + j, me, (*chip, c), src=in_refs[i]) for j, chip in enumerate(chips)]
            first.append(copy(i, 0, me, sibling, src=in_refs[i]))
        for cp in first:
            cp.start()
        passed = []
        for j, chip in enumerate(chips):
            for i in range(n_arr):
                copy(i, 1 + j, (*chip, c), me).wait_recv()
                fwd = copy(i, 1 + OTHER_CHIPS + j, (*chip, c), sibling)
                fwd.start()
                passed.append(fwd)
        for i in range(n_arr):
            copy(i, 0, sibling, me).wait_recv()
            for j, chip in enumerate(chips):
                copy(i, 1 + OTHER_CHIPS + j, (*chip, 1 - c), me).wait_recv()
        for cp in first + passed:
            cp.wait_send()
        for cp in own:
            cp.wait()

    any_spec = pl.BlockSpec(memory_space=pl.ANY)
    return pl.pallas_call(
        body, name="weights_all_gather",
        out_shape=tuple(jax.ShapeDtypeStruct((N_DEV,) + a.shape, a.dtype) for a in shards),
        in_specs=[any_spec] * n_arr, out_specs=tuple([any_spec] * n_arr),
        scratch_shapes=[pltpu.SemaphoreType.DMA((n_arr * per,)), pltpu.SemaphoreType.DMA((n_arr * per,)),
                        pltpu.SemaphoreType.DMA((n_arr,))],
    )(*shards)


class _Exchange:
    def __init__(self, arrays, out_shapes, n_sem, copies):
        self.arrays, self.out_shapes, self.n_sem, self.copies = list(arrays), list(out_shapes), n_sem, copies


def _sibling_exchange(slabs):
    n_arr = len(slabs)

    def copies(in_refs, out_refs, send_sems, recv_sems, loc_sems):
        x, y, c, _ = _chip_peers()
        sends = [pltpu.make_async_remote_copy(src_ref=in_refs[i].at[2 * q + (1 - c)], dst_ref=out_refs[i].at[q],
                                              send_sem=send_sems.at[i * N_CHIPS + q], recv_sem=recv_sems.at[i * N_CHIPS + q],
                                              device_id=(x, y, 1 - c), device_id_type=MESH)
                 for i in range(n_arr) for q in range(N_CHIPS)]

        def start():
            for cp in sends:
                cp.start()

        def finish():
            for cp in sends:
                cp.wait_recv()
            for cp in sends:
                cp.wait_send()

        return start, finish

    return _Exchange(slabs, [jax.ShapeDtypeStruct((N_CHIPS,) + a.shape[1:], a.dtype) for a in slabs],
                     n_arr * N_CHIPS, copies)


def _pair_sum(slabs, from_sibling, core, name):
    _, rows, cols = slabs.shape
    tr = _row_block(rows, max(SUBLANES, (256 * 1024) // cols // SUBLANES * SUBLANES))

    def body(core_ref, a_ref, b_ref, o_ref):
        o_ref[...] = (a_ref[...] + b_ref[...]).astype(BF16)

    grid_spec = pltpu.PrefetchScalarGridSpec(
        num_scalar_prefetch=1, grid=(N_CHIPS, rows // tr),
        in_specs=[pl.BlockSpec((None, tr, cols), lambda q, r, core_ref: (2 * q + core_ref[0], r, 0)),
                  pl.BlockSpec((None, tr, cols), lambda q, r, core_ref: (q, r, 0))],
        out_specs=pl.BlockSpec((None, tr, cols), lambda q, r, core_ref: (q, r, 0)))
    return pl.pallas_call(
        body, name=name, grid_spec=grid_spec,
        out_shape=jax.ShapeDtypeStruct((N_CHIPS, rows, cols), BF16),
        compiler_params=_params("parallel", "parallel"),
    )(core, slabs, from_sibling)


def _chip_exchange(partials):
    n_arr = len(partials)

    def copies(in_refs, out_refs, send_sems, recv_sems, loc_sems):
        x, y, c, chips = _chip_peers()
        mine = 2 * x + y
        own = [pltpu.make_async_copy(in_refs[i].at[mine], out_refs[i].at[mine], loc_sems.at[i]) for i in range(n_arr)]

        def copy(i, j, chip, src_slot, dst_slot):
            return pltpu.make_async_remote_copy(src_ref=in_refs[i].at[src_slot], dst_ref=out_refs[i].at[dst_slot],
                                                send_sem=send_sems.at[i * OTHER_CHIPS + j],
                                                recv_sem=recv_sems.at[i * OTHER_CHIPS + j],
                                                device_id=(*chip, c), device_id_type=MESH)

        sends = [copy(i, j, chip, 2 * chip[0] + chip[1], mine) for j, chip in enumerate(chips) for i in range(n_arr)]
        recvs = [copy(i, j, chip, mine, 2 * chip[0] + chip[1]) for j, chip in enumerate(chips) for i in range(n_arr)]

        def start():
            for cp in own + sends:
                cp.start()

        def finish():
            for cp in recvs:
                cp.wait_recv()
            for cp in sends:
                cp.wait_send()
            for cp in own:
                cp.wait()

        return start, finish

    return _Exchange(partials, [jax.ShapeDtypeStruct(a.shape, a.dtype) for a in partials], n_arr * OTHER_CHIPS, copies)


def _small_all_reduce(part):
    rows = part.shape[0]

    def body(p_ref, o_ref, buf_ref, send_sems, recv_sems):
        me, peers = _mesh_peers()
        buf_ref[me] = p_ref[...]
        sends = []
        for k, (dev, pid) in enumerate(peers):
            cp = pltpu.make_async_remote_copy(src_ref=p_ref, dst_ref=buf_ref.at[me], send_sem=send_sems.at[k],
                                              recv_sem=recv_sems.at[k], device_id=dev, device_id_type=MESH)
            cp.start()
            sends.append(cp)
        for k, (dev, pid) in enumerate(peers):
            pltpu.make_async_remote_copy(src_ref=p_ref, dst_ref=buf_ref.at[pid], send_sem=send_sems.at[k],
                                         recv_sem=recv_sems.at[k], device_id=dev, device_id_type=MESH).wait_recv()
        for cp in sends:
            cp.wait_send()
        acc = buf_ref[0]
        for i in range(1, N_DEV):
            acc = acc + buf_ref[i]
        o_ref[...] = acc

    vmem = pl.BlockSpec(memory_space=pltpu.VMEM)
    return pl.pallas_call(
        body, name="small_all_reduce",
        out_shape=jax.ShapeDtypeStruct(part.shape, F32),
        in_specs=[vmem], out_specs=vmem,
        scratch_shapes=[pltpu.VMEM((N_DEV, rows, LANES), F32), pltpu.SemaphoreType.DMA((N_DEV - 1,)),
                        pltpu.SemaphoreType.DMA((N_DEV - 1,))],
    )(part)


def _adamw_vals(w, g, m, v):
    m = ADAM_B1 * m + (1.0 - ADAM_B1) * g
    v = ADAM_B2 * v + (1.0 - ADAM_B2) * (g * g)
    m_hat = m / (1.0 - ADAM_B1 ** ADAM_STEP)
    v_hat = v / (1.0 - ADAM_B2 ** ADAM_STEP)
    delta = -ADAM_LR * (m_hat / (jnp.sqrt(v_hat) + ADAM_EPS) + ADAM_WD * w)
    return delta, m, v


def _adamw(contrib, w, m, v, name):
    n, rows, cols = contrib.shape
    tr = _row_block(rows, max(SUBLANES, (128 * 1024) // cols // SUBLANES * SUBLANES))

    def body(c_ref, w_ref, m_ref, v_ref, g_ref, d_ref, nm_ref, nv_ref):
        g = c_ref[0].astype(F32)
        for i in range(1, n):
            g = g + c_ref[i].astype(F32)
        delta, nm, nv = _adamw_vals(w_ref[...], g, m_ref[...], v_ref[...])
        g_ref[...] = g
        d_ref[...] = delta
        nm_ref[...] = nm
        nv_ref[...] = nv

    row = pl.BlockSpec((tr, cols), lambda i: (i, 0))
    shp = jax.ShapeDtypeStruct((rows, cols), F32)
    return pl.pallas_call(
        body, name=name, grid=(rows // tr,),
        in_specs=[pl.BlockSpec((n, tr, cols), lambda i: (0, i, 0)), row, row, row],
        out_specs=(row, row, row, row), out_shape=(shp, shp, shp, shp),
        compiler_params=_params("parallel"),
    )(contrib, w, m, v)


def _lane_block(n, cap):
    if n <= cap:
        return n
    best = None
    for cand in range(LANES, cap + 1, LANES):
        if n % cand == 0:
            best = cand
    assert best is not None, n
    return best


def _matmul(a, b, out_dtype, name, mode="nn", tm=1024, tn=1024, tk=1024, exchange=None):
    g = a.shape[0]
    m, k = (a.shape[2], a.shape[1]) if mode == "tn" else (a.shape[1], a.shape[2])
    n = b.shape[1] if mode == "nt" else b.shape[2]
    tm, tn, tk = _lane_block(m, tm), _lane_block(n, tn), _lane_block(k, tk)
    nk = k // tk
    grid = (g, m // tm, n // tn, nk)
    a_spec = (pl.BlockSpec((None, tk, tm), lambda gi, i, j, kk: (gi, kk, i)) if mode == "tn" else
              pl.BlockSpec((None, tm, tk), lambda gi, i, j, kk: (gi, i, kk)))
    b_spec = (pl.BlockSpec((None, tn, tk), lambda gi, i, j, kk: (gi, j, kk)) if mode == "nt" else
              pl.BlockSpec((None, tk, tn), lambda gi, i, j, kk: (gi, kk, j)))
    ca, cb = (0 if mode == "tn" else 1), (1 if mode == "nt" else 0)
    n_ex = 0 if exchange is None else len(exchange.arrays)

    def body(a_ref, b_ref, *rest):
        ex_in, o_ref, ex_out, scratch = rest[:n_ex], rest[n_ex], rest[n_ex + 1:2 * n_ex + 1], rest[2 * n_ex + 1:]
        if exchange is not None:
            start, finish = exchange.copies(ex_in, ex_out, *scratch[-3:])
            pids = [pl.program_id(ax) for ax in range(4)]
            pl.when((pids[0] == 0) & (pids[1] == 0) & (pids[2] == 0) & (pids[3] == 0))(start)
        part = _dg(a_ref[...], b_ref[...], ca, cb)
        if nk == 1:
            o_ref[...] = part.astype(o_ref.dtype)
        else:
            acc_ref = scratch[0]
            kk = pl.program_id(3)

            @pl.when(kk == 0)
            def _():
                acc_ref[...] = part

            @pl.when((kk > 0) & (kk < nk - 1))
            def _():
                acc_ref[...] += part

            @pl.when(kk == nk - 1)
            def _():
                o_ref[...] = (acc_ref[...] + part).astype(o_ref.dtype)
        if exchange is not None:
            pl.when((pids[0] == grid[0] - 1) & (pids[1] == grid[1] - 1) & (pids[2] == grid[2] - 1)
                    & (pids[3] == grid[3] - 1))(finish)

    any_spec = pl.BlockSpec(memory_space=pl.ANY)
    scratch_shapes = [] if nk == 1 else [pltpu.VMEM((tm, tn), F32)]
    out_shape = [jax.ShapeDtypeStruct((g, m, n), out_dtype)]
    if exchange is not None:
        scratch_shapes += [pltpu.SemaphoreType.DMA((exchange.n_sem,)), pltpu.SemaphoreType.DMA((exchange.n_sem,)),
                           pltpu.SemaphoreType.DMA((n_ex,))]
        out_shape += exchange.out_shapes
    outs = pl.pallas_call(
        body, name=name, grid=grid,
        in_specs=[a_spec, b_spec] + [any_spec] * n_ex,
        out_specs=tuple([pl.BlockSpec((None, tm, tn), lambda gi, i, j, kk: (gi, i, j))] + [any_spec] * n_ex),
        out_shape=tuple(out_shape),
        scratch_shapes=scratch_shapes,
        compiler_params=(_params("parallel", "parallel", "parallel", "arbitrary") if exchange is None else
                         _params("arbitrary", "arbitrary", "arbitrary", "arbitrary")),
    )(a, b, *([] if exchange is None else exchange.arrays))
    return outs[0] if exchange is None else outs


def _rms_fwd(x, w):
    s, d = x.shape
    tm = _row_block(s, 512)

    def body(x_ref, w_ref, h_ref):
        h_ref[...] = _rmsnorm(x_ref[...], w_ref[...]).astype(BF16)

    return pl.pallas_call(
        body, name="input_rmsnorm", grid=(s // tm,),
        in_specs=[pl.BlockSpec((tm, d), lambda i: (i, 0)), pl.BlockSpec((1, d), lambda i: (0, 0))],
        out_specs=pl.BlockSpec((tm, d), lambda i: (i, 0)),
        out_shape=jax.ShapeDtypeStruct((s, d), BF16),
        compiler_params=_params("parallel"),
    )(x, w)


def _rms_bwd(x, w, dh_stacked, dh_parts, dx_res):
    s, d = x.shape
    tm = _row_block(s, 256)
    n_parts = 1 + len(dh_parts)

    def body(x_ref, w_ref, *rest):
        part_refs = rest[:n_parts]
        res_ref, gx_ref, gw_ref = rest[n_parts:]
        dh = part_refs[0][...]
        for r in part_refs[1:]:
            dh = dh + r[...]
        _, vjp = jax.vjp(_rmsnorm, x_ref[...], w_ref[...])
        dx, dw = vjp(dh)
        gx_ref[...] = dx + res_ref[...]

        @pl.when(pl.program_id(0) == 0)
        def _():
            gw_ref[...] = jnp.zeros_like(gw_ref)

        gw_ref[...] += dw

    row = pl.BlockSpec((tm, d), lambda i: (i, 0))
    vec = pl.BlockSpec((1, d), lambda i: (0, 0))
    return pl.pallas_call(
        body, name="input_rmsnorm_bwd", grid=(s // tm,),
        in_specs=[row, vec, pl.BlockSpec((None, tm, d), lambda i: (0, i, 0))] + [row] * (n_parts - 1) + [row],
        out_specs=(row, vec),
        out_shape=(jax.ShapeDtypeStruct((s, d), F32), jax.ShapeDtypeStruct((1, d), F32)),
        compiler_params=_params("arbitrary"),
    )(x, w, dh_stacked, *dh_parts, dx_res)


def _lane_masks(rows):
    lane = lax.broadcasted_iota(jnp.int32, (rows, LANES), 1)
    return lane < HEAD_DIM, (lane & (HEAD_DIM - 1)) < HEAD_DIM // 2


def _swap_halves(t, lo_half):
    return jnp.where(lo_half, pltpu.roll(t, LANES - HEAD_DIM // 2, 1), pltpu.roll(t, HEAD_DIM // 2, 1))


def _rope(t, cos, sin_signed, lo_half):
    return t * cos + _swap_halves(t, lo_half) * sin_signed


def _rope_bwd(d, cos, sin_signed, lo_half):
    return d * cos - _swap_halves(d, lo_half) * sin_signed


def _window_mask(first):
    qi = lax.broadcasted_iota(jnp.int32, (2 * ATT_BLOCK, 2 * ATT_BLOCK), 0) & (ATT_BLOCK - 1)
    kj = lax.broadcasted_iota(jnp.int32, (2 * ATT_BLOCK, 2 * ATT_BLOCK), 1)
    dist = qi + ATT_BLOCK - kj
    return (dist >= 0) & (dist <= N_BACK) & ((kj >= ATT_BLOCK) | jnp.logical_not(first))


def _stack_heads(t, head0):
    zero = jnp.zeros_like(t)
    return jnp.concatenate([jnp.where(head0, t, zero), jnp.where(head0, zero, t)], axis=0)


def _unstack_heads(t2, head0):
    return jnp.where(head0, t2[:ATT_BLOCK], t2[ATT_BLOCK:])


def _blocks_per_subsequence(g, nb):
    return lax.shift_right_logical(jnp.int32(nb), 2 * g)


def _attn_fwd(qkv, cos, sin):
    _, s, _ = qkv.shape
    nb = s // ATT_BLOCK

    def body(qkv_ref, cos_ref, sin_ref, o_ref, lse_ref, kp_ref, vp_ref):
        g, t = pl.program_id(0), pl.program_id(1)
        first = (t & (_blocks_per_subsequence(g, nb) - 1)) == 0

        @pl.when(first)
        def _():
            kp_ref[...] = jnp.zeros_like(kp_ref)
            vp_ref[...] = jnp.zeros_like(vp_ref)

        cos_b, sin_b = cos_ref[...], sin_ref[...]
        head0, lo_half = _lane_masks(ATT_BLOCK)
        valid = _window_mask(first)
        for sl in range(WIDTH // LANES):
            cq = pl.ds(sl * LANES, LANES)
            ck = pl.ds(WIDTH + sl * LANES, LANES)
            cv = pl.ds(2 * WIDTH + sl * LANES, LANES)
            qr = (_rope(qkv_ref[:, cq], cos_b, sin_b, lo_half) * (HEAD_DIM ** -0.5)).astype(BF16)
            kr = _rope(qkv_ref[:, ck], cos_b, sin_b, lo_half).astype(BF16)
            v16 = qkv_ref[:, cv].astype(BF16)
            kcat = jnp.concatenate([kp_ref[:, cq], kr], axis=0)
            vcat = jnp.concatenate([vp_ref[:, cq], v16], axis=0)
            sc = jnp.where(valid, _dg(_stack_heads(qr, head0), kcat, 1, 1), -jnp.inf)
            mx = jnp.max(sc, axis=1, keepdims=True)
            p = jnp.exp(sc - mx)
            den = jnp.sum(p, axis=1, keepdims=True)
            o_ref[:, cq] = _unstack_heads(_dg((p * (1.0 / den)).astype(BF16), vcat, 1, 0), head0)
            lse2 = mx + jnp.log(den)
            lse_ref[:, cq] = jnp.where(head0, lse2[:ATT_BLOCK], lse2[ATT_BLOCK:])
            kp_ref[:, cq] = kr
            vp_ref[:, cq] = v16

    blk = lambda w: pl.BlockSpec((None, ATT_BLOCK, w), lambda g, t: (g, t, 0))
    shp = jax.ShapeDtypeStruct((GROUPS, s, WIDTH), F32)
    return pl.pallas_call(
        body, name="dilated_attention_fwd", grid=(GROUPS, nb),
        in_specs=[blk(3 * WIDTH), blk(LANES), blk(LANES)],
        out_specs=(blk(WIDTH), blk(WIDTH)), out_shape=(shp, shp),
        scratch_shapes=[pltpu.VMEM((ATT_BLOCK, WIDTH), BF16), pltpu.VMEM((ATT_BLOCK, WIDTH), BF16)],
        compiler_params=_params("arbitrary", "arbitrary"),
    )(qkv, cos, sin)


def _attn_bwd(qkv, cos, sin, o, lse, do, dlse):
    _, s, _ = qkv.shape
    nb = s // ATT_BLOCK

    def body(qkv_ref, cos_ref, sin_ref, cosp_ref, sinp_ref, o_ref, lse_ref, do_ref, dlse_ref,
             dqkv_ref, kp_ref, vp_ref, dka_ref, dva_ref, dqp_ref):
        g, t = pl.program_id(0), pl.program_id(1)
        first = (t & (_blocks_per_subsequence(g, nb) - 1)) == 0
        active = t < nb
        head0, lo_half = _lane_masks(ATT_BLOCK)
        cos_p, sin_p = cosp_ref[...], sinp_ref[...]

        @pl.when(t == 0)
        def _():
            dka_ref[...] = jnp.zeros_like(dka_ref)
            dva_ref[...] = jnp.zeros_like(dva_ref)
            dqp_ref[...] = jnp.zeros_like(dqp_ref)

        dqkv_ref[:, pl.ds(0, WIDTH)] = dqp_ref[...]

        @pl.when(active & first)
        def _():
            kp_ref[...] = jnp.zeros_like(kp_ref)
            vp_ref[...] = jnp.zeros_like(vp_ref)

        @pl.when(active)
        def _():
            cos_b, sin_b = cos_ref[...], sin_ref[...]
            valid = _window_mask(first)
            for sl in range(WIDTH // LANES):
                cq = pl.ds(sl * LANES, LANES)
                ck = pl.ds(WIDTH + sl * LANES, LANES)
                cv = pl.ds(2 * WIDTH + sl * LANES, LANES)
                qr = (_rope(qkv_ref[:, cq], cos_b, sin_b, lo_half) * (HEAD_DIM ** -0.5)).astype(BF16)
                kr = _rope(qkv_ref[:, ck], cos_b, sin_b, lo_half).astype(BF16)
                v16 = qkv_ref[:, cv].astype(BF16)
                kcat = jnp.concatenate([kp_ref[:, cq], kr], axis=0)
                vcat = jnp.concatenate([vp_ref[:, cq], v16], axis=0)
                do_b = do_ref[:, cq]
                do16 = do_b.astype(BF16)
                lse_b = lse_ref[:, cq]
                cterm = dlse_ref[:, cq] - do_b * o_ref[:, cq]
                dqs, dkc, dvc = [], None, None
                for hm in (head0, jnp.logical_not(head0)):
                    qm = jnp.where(hm, qr, jnp.zeros_like(qr))
                    dom = jnp.where(hm, do16, jnp.zeros_like(do16))
                    sc = jnp.where(valid[:ATT_BLOCK], _dg(qm, kcat, 1, 1), -jnp.inf)
                    lse_h = jnp.max(jnp.where(hm, lse_b, -jnp.inf), axis=1, keepdims=True)
                    c = jnp.sum(jnp.where(hm, cterm, 0.0), axis=1, keepdims=True)
                    p = jnp.exp(sc - lse_h)
                    ds16 = (p * (_dg(dom, vcat, 1, 1) + c)).astype(BF16)
                    dv_h, dk_h = _dg(p.astype(BF16), dom, 0, 0), _dg(ds16, qm, 0, 0)
                    dvc = dv_h if dvc is None else dvc + dv_h
                    dkc = dk_h if dkc is None else dkc + dk_h
                    dqs.append(_dg(ds16, kcat, 1, 0))
                dq = jnp.where(head0, dqs[0], dqs[1]) * (HEAD_DIM ** -0.5)
                dqp_ref[:, cq] = _rope_bwd(dq, cos_b, sin_b, lo_half).astype(BF16)
                dqkv_ref[:, ck] = _rope_bwd(dka_ref[:, cq] + dkc[:ATT_BLOCK], cos_p, sin_p, lo_half).astype(BF16)
                dqkv_ref[:, cv] = (dva_ref[:, cq] + dvc[:ATT_BLOCK]).astype(BF16)
                dka_ref[:, cq] = dkc[ATT_BLOCK:]
                dva_ref[:, cq] = dvc[ATT_BLOCK:]
                kp_ref[:, cq] = kr
                vp_ref[:, cq] = v16

        @pl.when(jnp.logical_not(active))
        def _():
            for sl in range(WIDTH // LANES):
                cq = pl.ds(sl * LANES, LANES)
                dqkv_ref[:, pl.ds(WIDTH + sl * LANES, LANES)] = _rope_bwd(dka_ref[:, cq], cos_p, sin_p, lo_half).astype(BF16)
                dqkv_ref[:, pl.ds(2 * WIDTH + sl * LANES, LANES)] = dva_ref[:, cq].astype(BF16)

    cur = lambda w: pl.BlockSpec((None, ATT_BLOCK, w), lambda g, t: (g, jnp.minimum(t, nb - 1), 0))
    prev = lambda w: pl.BlockSpec((None, ATT_BLOCK, w), lambda g, t: (g, jnp.maximum(t - 1, 0), 0))
    return pl.pallas_call(
        body, name="dilated_attention_bwd", grid=(GROUPS, nb + 1),
        in_specs=[cur(3 * WIDTH), cur(LANES), cur(LANES), prev(LANES), prev(LANES),
                  cur(WIDTH), cur(WIDTH), cur(WIDTH), cur(WIDTH)],
        out_specs=prev(3 * WIDTH), out_shape=jax.ShapeDtypeStruct((GROUPS, s, 3 * WIDTH), BF16),
        scratch_shapes=[pltpu.VMEM((ATT_BLOCK, WIDTH), BF16), pltpu.VMEM((ATT_BLOCK, WIDTH), BF16),
                        pltpu.VMEM((ATT_BLOCK, WIDTH), F32), pltpu.VMEM((ATT_BLOCK, WIDTH), F32),
                        pltpu.VMEM((ATT_BLOCK, WIDTH), BF16)],
        compiler_params=_params("arbitrary", "arbitrary"),
    )(qkv, cos, sin, cos, sin, o, lse, do, dlse)


CONV_PAD = SUBLANES


def _gdn_post(y, is_q, is_k):
    head0, _ = _lane_masks(y.shape[0])
    c = _silu(y)
    sq = c * c
    ss0 = jnp.sum(jnp.where(head0, sq, 0.0), axis=1, keepdims=True)
    ss1 = jnp.sum(jnp.where(head0, 0.0, sq), axis=1, keepdims=True)
    r = jnp.where(head0, lax.rsqrt(ss0 + NORM_EPS), lax.rsqrt(ss1 + NORM_EPS))
    scale = jnp.where(is_q, HEAD_DIM ** -0.5, 1.0).astype(F32)
    return jnp.where(is_q | is_k, c * r * scale, c)


def _conv_rows(xp_ref, w, c0, rows):
    y = w[0:1, :] * xp_ref[pl.ds(c0 + CONV_PAD - (CONV_K - 1), rows), :]
    for k in range(1, CONV_K):
        y = y + w[k:k + 1, :] * xp_ref[pl.ds(c0 + CONV_PAD - (CONV_K - 1) + k, rows), :]
    return y


def _gdn_pre_fwd(proj_r, conv8, col0):
    s = proj_r.shape[0]
    tr = _row_block(s, 512)
    nblk = QKV_B // LANES
    nq = WIDTH // LANES

    def body(x_ref, w_ref, out_ref, xp_ref):
        j = pl.program_id(0)
        is_q, is_k = j < nq, (j >= nq) & (j < 2 * nq)
        xp_ref[pl.ds(0, CONV_PAD), :] = jnp.zeros((CONV_PAD, LANES), F32)
        xp_ref[pl.ds(CONV_PAD, s), :] = x_ref[...]
        w = w_ref[...]
        for c in range(s // tr):
            out_ref[pl.ds(c * tr, tr), :] = _gdn_post(_conv_rows(xp_ref, w, c * tr, tr), is_q, is_k)

    return pl.pallas_call(
        body, name="gdn_conv_fwd", grid=(nblk,),
        in_specs=[pl.BlockSpec((s, LANES), lambda j: (0, col0 + j)), pl.BlockSpec((SUBLANES, LANES), lambda j: (0, j))],
        out_specs=pl.BlockSpec((s, LANES), lambda j: (0, j)),
        out_shape=jax.ShapeDtypeStruct((s, QKV_B), F32),
        scratch_shapes=[pltpu.VMEM((s + CONV_PAD, LANES), F32)],
        compiler_params=_params("parallel"),
    )(proj_r, conv8)


def _gdn_pre_bwd(proj_r, conv8, dc, col0):
    s = proj_r.shape[0]
    tr = _row_block(s, 512)
    nblk = QKV_B // LANES
    nq = WIDTH // LANES

    def body(x_ref, w_ref, dc_ref, dx_ref, dw_ref, xp_ref, dyp_ref):
        j = pl.program_id(0)
        is_q, is_k = j < nq, (j >= nq) & (j < 2 * nq)
        xp_ref[pl.ds(0, CONV_PAD), :] = jnp.zeros((CONV_PAD, LANES), F32)
        xp_ref[pl.ds(CONV_PAD, s), :] = x_ref[...]
        dyp_ref[pl.ds(s, CONV_PAD), :] = jnp.zeros((CONV_PAD, LANES), F32)
        w = w_ref[...]
        for c in range(s // tr):
            y = _conv_rows(xp_ref, w, c * tr, tr)
            _, vjp = jax.vjp(lambda yy: _gdn_post(yy, is_q, is_k), y)
            dyp_ref[pl.ds(c * tr, tr), :] = vjp(dc_ref[pl.ds(c * tr, tr), :])[0]
        dws = [jnp.zeros((1, LANES), F32) for _ in range(CONV_K)]
        for c in range(s // tr):
            c0 = c * tr
            dy = dyp_ref[pl.ds(c0, tr), :]
            dx = w[0:1, :] * dyp_ref[pl.ds(c0 + CONV_K - 1, tr), :]
            for k in range(1, CONV_K):
                dx = dx + w[k:k + 1, :] * dyp_ref[pl.ds(c0 + CONV_K - 1 - k, tr), :]
            dx_ref[pl.ds(c0, tr), :] = dx.astype(BF16)
            for k in range(CONV_K):
                xs = xp_ref[pl.ds(c0 + CONV_PAD - (CONV_K - 1) + k, tr), :]
                dws[k] = dws[k] + jnp.sum(dy * xs, axis=0, keepdims=True)
        row = lax.broadcasted_iota(jnp.int32, (SUBLANES, LANES), 0)
        dwb = jnp.zeros((SUBLANES, LANES), F32)
        for k in range(CONV_K):
            dwb = dwb + jnp.where(row == k, dws[k], 0.0)
        dw_ref[...] = dwb

    return pl.pallas_call(
        body, name="gdn_conv_bwd", grid=(nblk,),
        in_specs=[pl.BlockSpec((s, LANES), lambda j: (0, col0 + j)), pl.BlockSpec((SUBLANES, LANES), lambda j: (0, j)),
                  pl.BlockSpec((s, LANES), lambda j: (0, j))],
        out_specs=(pl.BlockSpec((s, LANES), lambda j: (0, j)), pl.BlockSpec((SUBLANES, LANES), lambda j: (0, j))),
        out_shape=(jax.ShapeDtypeStruct((s, QKV_B), BF16), jax.ShapeDtypeStruct((SUBLANES, QKV_B), F32)),
        scratch_shapes=[pltpu.VMEM((s + CONV_PAD, LANES), F32), pltpu.VMEM((s + CONV_PAD, LANES), F32)],
        compiler_params=_params("parallel"),
    )(proj_r, conv8, dc)


def _gdn_chunk(q, k, v, bcol, acol, alog, dtb, gnw, state, t_saved=None):
    n = q.shape[-2]
    shp = (1, n, n)
    row = lax.broadcasted_iota(jnp.int32, shp, 1)
    col = lax.broadcasted_iota(jnp.int32, shp, 2)
    beta = _sigmoid(bcol)
    g = -jnp.exp(alog) * _softplus(acol + dtb)
    g_row = jnp.sum(jnp.where(row == col, g, 0.0), axis=-2, keepdims=True)
    big_g = jnp.sum(jnp.where(row >= col, g_row, 0.0), axis=-1, keepdims=True)
    big_g_row = jnp.sum(jnp.where(row <= col, g, 0.0), axis=-2, keepdims=True)
    decay_incl = jnp.exp(jnp.where(row >= col, big_g - big_g_row, -jnp.inf))
    decay_strict = jnp.where(row > col, decay_incl, 0.0)
    k_beta = k * beta
    a_mat = _mm_nt(k_beta, k) * decay_strict
    t_inv = _tri_inv(a_mat) if t_saved is None else _tri_inv_saved(a_mat, t_saved)
    e_g = jnp.exp(big_g)
    u = _mm(t_inv, v * beta)
    w = _mm(t_inv, k_beta * e_g)
    attn = _mm_nt(q, k) * decay_incl
    v_new = u - _mm(w, state)
    o = _mm(q * e_g, state) + _mm(attn, v_new)
    total = jnp.sum(g, axis=-2, keepdims=True)
    new_state = state * jnp.exp(total) + _mm_tn(k * jnp.exp(total - big_g), v_new)
    return _rmsnorm(o, gnw), new_state, t_inv


def _split_heads(x):
    return jnp.stack([x[:, h * HEAD_DIM:(h + 1) * HEAD_DIM] for h in range(HEADS)], axis=0)


def _merge_heads(x):
    return jnp.concatenate([x[h] for h in range(HEADS)], axis=1)


def _logit_columns(ba):
    lane = lax.broadcasted_iota(jnp.int32, ba.shape, 1)

    def cols(off):
        return jnp.stack([jnp.sum(jnp.where(lane == off + h, ba, 0.0), axis=1, keepdims=True) for h in range(HEADS)], axis=0)

    return cols(0), cols(HEADS)


def _logit_block(dbc, dac, shape):
    lane = lax.broadcasted_iota(jnp.int32, shape, 1)
    out = jnp.zeros(shape, F32)
    for h in range(HEADS):
        out = out + jnp.where(lane == h, dbc[h], 0.0) + jnp.where(lane == HEADS + h, dac[h], 0.0)
    return out


def _gdn_scan_fwd(cqkv, proj_r, ba_col, alog, dtb, gnw):
    s = cqkv.shape[0]
    nc = s // CHUNK
    span = SCAN_CHUNKS * CHUNK

    def body(q_ref, k_ref, v_ref, ba_ref, al_ref, dt_ref, gnw_ref, o_ref, st_ref, ti_ref, state_ref):
        @pl.when(pl.program_id(0) == 0)
        def _():
            state_ref[...] = jnp.zeros_like(state_ref)

        st = state_ref[...]
        for u in range(SCAN_CHUNKS):
            rows = pl.ds(u * CHUNK, CHUNK)
            st_ref[u] = st
            bcol, acol = _logit_columns(ba_ref[rows, :])
            o, st, t_inv = _gdn_chunk(_split_heads(q_ref[rows, :]), _split_heads(k_ref[rows, :]),
                                      _split_heads(v_ref[rows, :]), bcol, acol, al_ref[...], dt_ref[...], gnw_ref[...], st)
            o_ref[rows, :] = _merge_heads(o)
            ti_ref[u] = t_inv
        state_ref[...] = st

    part = lambda i: pl.BlockSpec((span, WIDTH), lambda n: (n, i))
    par = pl.BlockSpec((HEADS, 1, 1), lambda n: (0, 0, 0))
    per_chunk = pl.BlockSpec((SCAN_CHUNKS, HEADS, HEAD_DIM, HEAD_DIM), lambda n: (n, 0, 0, 0))
    per_chunk_shape = jax.ShapeDtypeStruct((nc, HEADS, HEAD_DIM, HEAD_DIM), F32)
    return pl.pallas_call(
        body, name="gdn_scan_fwd", grid=(nc // SCAN_CHUNKS,),
        in_specs=[part(0), part(1), part(2), pl.BlockSpec((span, LANES), lambda n: (n, ba_col)), par, par,
                  pl.BlockSpec((1, 1, HEAD_DIM), lambda n: (0, 0, 0))],
        out_specs=(part(0), per_chunk, per_chunk),
        out_shape=(jax.ShapeDtypeStruct((s, WIDTH), F32), per_chunk_shape, per_chunk_shape),
        scratch_shapes=[pltpu.VMEM((HEADS, HEAD_DIM, HEAD_DIM), F32)],
        compiler_params=_params("arbitrary"),
    )(cqkv, cqkv, cqkv, proj_r, alog, dtb, gnw)


def _gdn_scan_bwd(cqkv, proj_r, ba_col, alog, dtb, gnw, states, t_invs, do):
    s = cqkv.shape[0]
    nc = s // CHUNK
    span = SCAN_CHUNKS * CHUNK
    n_steps = nc // SCAN_CHUNKS

    def body(q_ref, k_ref, v_ref, ba_ref, al_ref, dt_ref, gnw_ref, st_ref, ti_ref, do_ref,
             dqkv_ref, dba_ref, dal_ref, ddt_ref, dgnw_ref, dstate_ref):
        @pl.when(pl.program_id(0) == 0)
        def _():
            dstate_ref[...] = jnp.zeros_like(dstate_ref)
            dal_ref[...] = jnp.zeros_like(dal_ref)
            ddt_ref[...] = jnp.zeros_like(ddt_ref)
            dgnw_ref[...] = jnp.zeros_like(dgnw_ref)

        dst = dstate_ref[...]
        for u in reversed(range(SCAN_CHUNKS)):
            rows = pl.ds(u * CHUNK, CHUNK)
            bcol, acol = _logit_columns(ba_ref[rows, :])
            _, vjp = jax.vjp(lambda *a, t_saved=ti_ref[u]: _gdn_chunk(*a, t_saved=t_saved)[:2],
                             _split_heads(q_ref[rows, :]), _split_heads(k_ref[rows, :]), _split_heads(v_ref[rows, :]),
                             bcol, acol, al_ref[...], dt_ref[...], gnw_ref[...], st_ref[u])
            dq, dk, dv, dbc, dac, dal, ddt, dgn, dst = vjp((_split_heads(do_ref[rows, :]), dst))
            dqkv_ref[rows, pl.ds(0, WIDTH)] = _merge_heads(dq)
            dqkv_ref[rows, pl.ds(WIDTH, WIDTH)] = _merge_heads(dk)
            dqkv_ref[rows, pl.ds(2 * WIDTH, WIDTH)] = _merge_heads(dv)
            dba_ref[rows, :] = _logit_block(dbc, dac, (CHUNK, LANES))
            dal_ref[...] += dal
            ddt_ref[...] += ddt
            dgnw_ref[...] += dgn
        dstate_ref[...] = dst

    rev = lambda n: n_steps - 1 - n
    part = lambda i: pl.BlockSpec((span, WIDTH), lambda n: (rev(n), i))
    par = pl.BlockSpec((HEADS, 1, 1), lambda n: (0, 0, 0))
    vec = pl.BlockSpec((1, 1, HEAD_DIM), lambda n: (0, 0, 0))
    par_shape = jax.ShapeDtypeStruct((HEADS, 1, 1), F32)
    per_chunk = pl.BlockSpec((SCAN_CHUNKS, HEADS, HEAD_DIM, HEAD_DIM), lambda n: (rev(n), 0, 0, 0))
    return pl.pallas_call(
        body, name="gdn_scan_bwd", grid=(n_steps,),
        in_specs=[part(0), part(1), part(2), pl.BlockSpec((span, LANES), lambda n: (rev(n), ba_col)), par, par, vec,
                  per_chunk, per_chunk, part(0)],
        out_specs=(pl.BlockSpec((span, QKV_B), lambda n: (rev(n), 0)), pl.BlockSpec((span, LANES), lambda n: (rev(n), 0)),
                   par, par, vec),
        out_shape=(jax.ShapeDtypeStruct((s, QKV_B), F32), jax.ShapeDtypeStruct((s, LANES), F32), par_shape, par_shape,
                   jax.ShapeDtypeStruct((1, 1, HEAD_DIM), F32)),
        scratch_shapes=[pltpu.VMEM((HEADS, HEAD_DIM, HEAD_DIM), F32)],
        compiler_params=_params("arbitrary"),
    )(cqkv, cqkv, cqkv, proj_r, alog, dtb, gnw, states, t_invs, do)


def _tail_loss(x, tgt, o0, o1, o2, l0, l1, l2, ga, gb, za, zb, ob, fnw, wua, wub, wo, tap_a, tap_b, tap_o):
    lm = jnp.maximum(jnp.maximum(l0, l1), l2)
    e0, e1, e2 = jnp.exp(l0 - lm), jnp.exp(l1 - lm), jnp.exp(l2 - lm)
    o_a = (e0 * o0 + e1 * o1 + e2 * o2) / (e0 + e1 + e2)
    y_a = _mm_tap(o_a * _silu(za), wua, tap_a)
    y_b = _mm_tap(ob * _silu(zb), wub, tap_b)
    merged = _sigmoid(ga) * y_a + _sigmoid(gb) * y_b
    y = _rmsnorm(x + _mm_tap(merged, wo, tap_o), fnw)
    err = y - tgt
    per_token = jnp.sum(err * err, axis=1, keepdims=True) * (0.5 / x.shape[1])
    return jnp.sum(per_token, axis=0, keepdims=True)


def _tail(x, tgt, o_all, lse_all, og12, lg12, proj_r, ob, wua, wub, wo, fnw):
    s, d = x.shape
    tm = _row_block(s, 128)
    col_za = 2 * d // WIDTH
    col_zb = (2 * d + WIDTH + QKV_B) // WIDTH

    def body(x_ref, t_ref, o0_ref, o1_ref, o2_ref, l0_ref, l1_ref, l2_ref, ga_ref, gb_ref, za_ref, zb_ref, ob_ref,
             wua_ref, wub_ref, wo_ref, fnw_ref,
             loss_ref, dx_ref, do0_ref, do1_ref, do2_ref, dl0_ref, dl1_ref, dl2_ref, dga_ref, dgb_ref, dza_ref,
             dzb_ref, dob_ref, dwua_ref, dwub_ref, dwo_ref, dfnw_ref):
        @pl.when(pl.program_id(0) == 0)
        def _():
            for r in (loss_ref, dwua_ref, dwub_ref, dwo_ref, dfnw_ref):
                r[...] = jnp.zeros_like(r)

        args = (x_ref[...], t_ref[...], o0_ref[...], o1_ref[...], o2_ref[...], l0_ref[...], l1_ref[...], l2_ref[...],
                ga_ref[...], gb_ref[...], za_ref[...], zb_ref[...], ob_ref[...], fnw_ref[...],
                wua_ref[...], wub_ref[...], wo_ref[...],
                jnp.zeros(wua_ref.shape, F32), jnp.zeros(wub_ref.shape, F32), jnp.zeros(wo_ref.shape, F32))
        loss, vjp = jax.vjp(_tail_loss, *args)
        (dx, _, do0, do1, do2, dl0, dl1, dl2, dga, dgb, dza, dzb, dob, dfnw, _, _, _, dwua, dwub, dwo) = vjp(
            jnp.ones((1, 1), F32))
        loss_ref[...] += jnp.broadcast_to(loss, loss_ref.shape)
        dx_ref[...] = dx
        do0_ref[...], do1_ref[...], do2_ref[...] = do0, do1, do2
        dl0_ref[...], dl1_ref[...], dl2_ref[...] = dl0, dl1, dl2
        dga_ref[...] = dga.astype(BF16)
        dgb_ref[...] = dgb.astype(BF16)
        dza_ref[...] = dza.astype(BF16)
        dzb_ref[...] = dzb.astype(BF16)
        dob_ref[...] = dob
        dwua_ref[...] += dwua
        dwub_ref[...] += dwub
        dwo_ref[...] += dwo
        dfnw_ref[...] += dfnw

    row = lambda w, c=0: pl.BlockSpec((tm, w), lambda i: (i, c))
    grp0 = pl.BlockSpec((None, tm, WIDTH), lambda i: (0, i, 0))
    full = lambda a, b: pl.BlockSpec((a, b), lambda i: (0, 0))
    f32 = lambda a, b: jax.ShapeDtypeStruct((a, b), F32)
    b16 = lambda a, b: jax.ShapeDtypeStruct((a, b), BF16)
    stacked = jax.ShapeDtypeStruct((GROUPS, s, WIDTH), F32)
    gspecs = [grp0, row(WIDTH), row(WIDTH)]
    in_specs = ([row(d), row(d)] + gspecs * 2 + [row(d, 0), row(d, 1), row(WIDTH, col_za), row(WIDTH, col_zb),
                row(WIDTH), full(WIDTH, d), full(WIDTH, d), full(d, d), full(1, d)])
    out_specs = ([full(SUBLANES, LANES), row(d)] + gspecs * 2 + [row(d), row(d), row(WIDTH), row(WIDTH), row(WIDTH),
                 full(WIDTH, d), full(WIDTH, d), full(d, d), full(1, d)])
    gshapes = [stacked, f32(s, WIDTH), f32(s, WIDTH)]
    out_shape = ([f32(SUBLANES, LANES), f32(s, d)] + gshapes * 2 + [b16(s, d), b16(s, d), b16(s, WIDTH),
                 b16(s, WIDTH), f32(s, WIDTH), f32(WIDTH, d), f32(WIDTH, d), f32(d, d), f32(1, d)])
    return pl.pallas_call(
        body, name="tail_fwd_bwd", grid=(s // tm,),
        in_specs=in_specs, out_specs=tuple(out_specs), out_shape=tuple(out_shape),
        compiler_params=_params("arbitrary"),
    )(x, tgt, o_all, og12[0], og12[1], lse_all, lg12[0], lg12[1], proj_r, proj_r, proj_r, proj_r, ob, wua, wub, wo, fnw)


PERMUTE_SPAN = 4096


def _permute_span(s):
    return PERMUTE_SPAN if s % PERMUTE_SPAN == 0 else s


def _from_dilated_rows(stacked, g, dil, name):
    n_slots, s, c = stacked.shape
    view = stacked.reshape(n_slots, dil, s // dil, c)
    span = _permute_span(s)

    def body(in_ref, out_ref):
        for r in range(dil):
            out_ref[pl.ds(r, span // dil, stride=dil), :] = in_ref[r]

    return pl.pallas_call(
        body, name=name, grid=(s // span, c // LANES),
        in_specs=[pl.BlockSpec((None, dil, span // dil, LANES), lambda n, j: (g, 0, n, j))],
        out_specs=pl.BlockSpec((span, LANES), lambda n, j: (n, j)),
        out_shape=jax.ShapeDtypeStruct((s, c), stacked.dtype),
        compiler_params=_params("parallel", "parallel"),
    )(view)


def _to_dilated_rows_into(nat, stacked, g, dil, name):
    n_slots, s, c = stacked.shape
    view = stacked.reshape(n_slots, dil, s // dil, c)
    span = _permute_span(s)

    def body(nat_ref, old_ref, out_ref):
        for r in range(dil):
            out_ref[r] = nat_ref[pl.ds(r, span // dil, stride=dil), :]

    out = pl.pallas_call(
        body, name=name, grid=(s // span, c // LANES),
        in_specs=[pl.BlockSpec((span, LANES), lambda n, j: (n, j)), pl.BlockSpec(memory_space=pl.ANY)],
        out_specs=pl.BlockSpec((None, dil, span // dil, LANES), lambda n, j: (g, 0, n, j)),
        out_shape=jax.ShapeDtypeStruct(view.shape, stacked.dtype),
        input_output_aliases={1: 0},
        compiler_params=_params("parallel", "parallel"),
    )(nat, view)
    return out.reshape(stacked.shape)


def _to_dilated(a, dil):
    if dil == 1:
        return a
    s = a.shape[0]
    return a.reshape(s // dil, dil, -1).transpose(1, 0, 2).reshape(a.shape)


def _from_dilated(a, dil):
    if dil == 1:
        return a
    s = a.shape[0]
    return a.reshape(dil, s // dil, -1).transpose(1, 0, 2).reshape(a.shape)


def _head_major(a):
    return a.reshape(a.shape[0], HEADS, HEAD_DIM).transpose(1, 0, 2)


def _from_head_major(a):
    return a.transpose(1, 0, 2).reshape(a.shape[1], WIDTH)


def _rope_tables(s):
    inv_freq = ROPE_THETA ** (-jnp.arange(0, HEAD_DIM, 2, dtype=F32) / HEAD_DIM)
    ang = jnp.arange(s, dtype=F32)[:, None] * inv_freq[None, :]
    cos_n = jnp.tile(jnp.cos(ang), (1, 2 * LANES // HEAD_DIM))
    sin_h = jnp.sin(ang)
    sin_n = jnp.tile(jnp.concatenate([-sin_h, sin_h], axis=1), (1, LANES // HEAD_DIM))
    def per_group(table, tag):
        out = jnp.broadcast_to(table, (GROUPS,) + table.shape)
        for g in range(1, GROUPS):
            out = _to_dilated_rows_into(table, out, g, DILATIONS[g], "rope_%s_to_dilated_%d" % (tag, g))
        return out

    return per_group(cos_n, "cos"), per_group(sin_n, "sin")


def _regroup_columns(pieces, widths):
    starts, pos = [], 0
    for p in pieces:
        starts.append(pos)
        pos += p.shape[1]
    assert pos == sum(widths), (pos, widths)
    out, lo = [], 0
    for w in widths:
        hi, parts = lo + w, []
        for p, st in zip(pieces, starts):
            a, b = max(lo, st), min(hi, st + p.shape[1])
            if a < b:
                parts.append(p[:, a - st:b - st])
        out.append(parts[0] if len(parts) == 1 else jnp.concatenate(parts, axis=1))
        lo = hi
    return out


def _pack_rows(parts, dtype, row_multiple):
    flat = jnp.concatenate([p.reshape(-1).astype(dtype) for p in parts])
    tile = row_multiple * LANES
    pad = (-flat.shape[0]) % tile
    return jnp.pad(flat, (0, pad)).reshape(-1, LANES)


def _unpack_rows(packed, shapes):
    flat = packed.reshape(-1)
    out, start = [], 0
    for shp in shapes:
        size = 1
        for n in shp:
            size *= n
        out.append(flat[start:start + size].reshape(shp))
        start += size
    return out


def kernel(x, norm_w, w_in, conv_w, a_log, dt_bias, gdn_norm_w, w_up_a, w_up_b, w_out, final_norm_w, loss_target, m_norm_w, m_w_in, m_conv_w, m_a_log, m_dt_bias, m_gdn_norm_w, m_w_up_a, m_w_up_b, m_w_out, m_final_norm_w, v_norm_w, v_w_in, v_conv_w, v_a_log, v_dt_bias, v_gdn_norm_w, v_w_up_a, v_w_up_b, v_w_out, v_final_norm_w):
    x2, tgt = x[0], loss_target[0]
    s, d = x2.shape
    me = 4 * lax.axis_index("x") + 2 * lax.axis_index("y") + lax.axis_index("c")
    win8 = w_in.shape[2]
    conv8w = conv_w.shape[2]

    conv_shard = jnp.pad(conv_w[0], ((0, SUBLANES - CONV_K), (0, 0)))
    w_in_g, wua_g, wub_g, wo_g, conv_g = _all_gather(
        [w_in[0].astype(BF16), w_up_a[0].astype(BF16), w_up_b[0].astype(BF16), w_out[0].astype(BF16), conv_shard])
    wua = jnp.concatenate([wua_g[i] for i in range(N_DEV)], axis=1)
    wub = jnp.concatenate([wub_g[i] for i in range(N_DEV)], axis=1)
    wo = wo_g.reshape(d, d)
    conv8 = jnp.concatenate([conv_g[i] for i in range(N_DEV)], axis=1)

    seg_widths = [QKV_B] * GROUPS + [WIDTH, QKV_B, WIDTH, 2 * HEADS, 2 * d]
    wq0, wq1, wq2, w_za, w_qkvb, w_zb, w_ba, w_gates = _regroup_columns([w_in_g[i] for i in range(N_DEV)], seg_widths)
    w_qkv = jnp.stack([wq0, wq1, wq2])
    w_rest = jnp.concatenate([w_gates, w_za, w_qkvb, w_zb, w_ba,
                              jnp.zeros((d, BA_PAD - 2 * HEADS), BF16)], axis=1)
    col_qkvb = (2 * d + WIDTH) // LANES
    col_ba = (2 * d + 2 * WIDTH + QKV_B) // LANES

    h = _rms_fwd(x2, norm_w)
    h_all = jnp.stack([_to_dilated(h, dil) for dil in DILATIONS])
    qkv_all = _matmul(h_all, w_qkv, F32, "in_proj_attention", tn=QKV_B)
    proj_r = _matmul(h[None], w_rest[None], F32, "in_proj_rest")[0]
    cos, sin = _rope_tables(s)
    o_all, lse_all = _attn_fwd(qkv_all, cos, sin)
    og12 = [_from_dilated_rows(o_all, g, DILATIONS[g], "attn_out_to_natural_%d" % g) for g in (1, 2)]
    lg12 = [_from_dilated_rows(lse_all, g, DILATIONS[g], "attn_lse_to_natural_%d" % g) for g in (1, 2)]

    cqkv = _gdn_pre_fwd(proj_r, conv8, col_qkvb)
    alog3, dtb3, gnw3 = a_log.reshape(HEADS, 1, 1), dt_bias.reshape(HEADS, 1, 1), gdn_norm_w.reshape(1, 1, HEAD_DIM)
    ob, states, t_invs = _gdn_scan_fwd(cqkv, proj_r, col_ba, alog3, dtb3, gnw3)

    (loss_blk, dx_res, do_all, do1, do2, dl_all, dl1, dl2, dga, dgb, dza, dzb, dob, dwua, dwub, dwo, dfnw) = _tail(
        x2, tgt, o_all, lse_all, og12, lg12, proj_r, ob, wua, wub, wo, final_norm_w.reshape(1, d))

    for g, (t_o, t_l) in ((1, (do1, dl1)), (2, (do2, dl2))):
        do_all = _to_dilated_rows_into(t_o, do_all, g, DILATIONS[g], "attn_dout_to_dilated_%d" % g)
        dl_all = _to_dilated_rows_into(t_l, dl_all, g, DILATIONS[g], "attn_dlse_to_dilated_%d" % g)
    dqkv_all = _attn_bwd(qkv_all, cos, sin, o_all, lse_all, do_all, dl_all)

    dcqkv, dba, dalog3, ddtb3, dgnw3 = _gdn_scan_bwd(cqkv, proj_r, col_ba, alog3, dtb3, gnw3, states, t_invs, dob)
    dqkv_b, dconv8 = _gdn_pre_bwd(proj_r, conv8, dcqkv, col_qkvb)
    dproj_r = jnp.concatenate([dga, dgb, dza, dqkv_b, dzb,
                               jnp.pad(dba.astype(BF16), ((0, 0), (0, BA_PAD - LANES)))], axis=1)

    dw_qkv = _matmul(h_all, dqkv_all, F32, "in_proj_attention_dw", mode="tn", tk=2048)
    dw_rest = _matmul(h[None], dproj_r[None], F32, "in_proj_rest_dw", mode="tn", tk=2048)[0]
    o2 = 2 * d
    dw_in_pieces = [dw_qkv[0], dw_qkv[1], dw_qkv[2],
                    dw_rest[:, o2:o2 + WIDTH], dw_rest[:, o2 + WIDTH:o2 + WIDTH + QKV_B],
                    dw_rest[:, o2 + WIDTH + QKV_B:o2 + 2 * WIDTH + QKV_B],
                    dw_rest[:, o2 + 2 * WIDTH + QKV_B:o2 + 2 * WIDTH + QKV_B + 2 * HEADS],
                    dw_rest[:, :o2]]

    def col_slabs(a, width):
        return jnp.stack([a[:, j * width:(j + 1) * width] for j in range(N_DEV)])

    slabs = [jnp.stack(_regroup_columns(dw_in_pieces, [win8] * N_DEV)), col_slabs(dwua, d // N_DEV),
             col_slabs(dwub, d // N_DEV), dwo.reshape(N_DEV, d // N_DEV, d)]
    dh_a, *from_sibling = _matmul(dqkv_all, w_qkv, F32, "in_proj_attention_dh", mode="nt", tk=2048,
                                  exchange=_sibling_exchange(slabs))
    core = lax.axis_index("c").astype(jnp.int32).reshape(1)
    partials = [_pair_sum(a, b, core, "grads_pair_sum_%d" % i) for i, (a, b) in enumerate(zip(slabs, from_sibling))]
    dh_r, *contrib = _matmul(dproj_r[None], w_rest[None], F32, "in_proj_rest_dh", mode="nt", tk=2560,
                             exchange=_chip_exchange(partials))
    dh_parts = [dh_r[0]] + [_from_dilated_rows(dh_a, g, DILATIONS[g], "dh_to_natural_%d" % g) for g in (1, 2)]
    grad_x, dnorm_w = _rms_bwd(x2, norm_w, dh_a, dh_parts, dx_res)

    small_parts = [dnorm_w, dfnw, dconv8[:CONV_K], dalog3[:, 0, 0], ddtb3[:, 0, 0], dgnw3[0], loss_blk[0, 0:1]]
    small_rows = [-(-p.size // LANES) for p in small_parts]
    small = jnp.concatenate([jnp.pad(p.reshape(-1), (0, r * LANES - p.size)).reshape(r, LANES)
                             for p, r in zip(small_parts, small_rows)])
    small = jnp.pad(small, ((0, (-small.shape[0]) % SUBLANES), (0, 0)))
    small_sum = _small_all_reduce(small)
    pieces, r0 = [], 0
    for p, r in zip(small_parts, small_rows):
        pieces.append(small_sum[r0:r0 + r].reshape(-1)[:p.size].reshape(p.shape))
        r0 += r
    g_norm_w, g_fnw, g_conv_full, g_alog, g_dtb, g_gnw, loss_sum = pieces
    g_conv = lax.dynamic_slice(g_conv_full, (0, me * conv8w), (CONV_K, conv8w))

    big = [_adamw(c, w[0], m[0], v[0], name) for c, w, m, v, name in (
        (contrib[0], w_in, m_w_in, v_w_in, "adamw_w_in"), (contrib[1], w_up_a, m_w_up_a, v_w_up_a, "adamw_w_up_a"),
        (contrib[2], w_up_b, m_w_up_b, v_w_up_b, "adamw_w_up_b"), (contrib[3], w_out, m_w_out, v_w_out, "adamw_w_out"))]
    g_big, d_big, nm_big, nv_big = ([t[i] for t in big] for i in range(4))

    small_ws = [norm_w, final_norm_w, conv_w, a_log, dt_bias, gdn_norm_w]
    small_ms = [m_norm_w, m_final_norm_w, m_conv_w, m_a_log, m_dt_bias, m_gdn_norm_w]
    small_vs = [v_norm_w, v_final_norm_w, v_conv_w, v_a_log, v_dt_bias, v_gdn_norm_w]
    small_gs = [g_norm_w, g_fnw, g_conv, g_alog, g_dtb, g_gnw]
    small_shapes = [t.shape for t in small_ws]
    sm = _adamw(_pack_rows(small_gs, F32, SUBLANES)[None], _pack_rows(small_ws, F32, SUBLANES),
                _pack_rows(small_ms, F32, SUBLANES), _pack_rows(small_vs, F32, SUBLANES), "adamw_small")
    g_sm, d_sm, nm_sm, nv_sm = (_unpack_rows(t, small_shapes) for t in sm)

    def ordered(bigs, smalls):
        nw, fnw_, cw, al, dtb, gn = smalls
        wi, ua, ub, wo_ = (t[None] for t in bigs)
        return [nw, wi, cw, al, dtb, gn, ua, ub, wo_, fnw_]

    return (loss_sum.reshape(()), grad_x[None], *ordered(g_big, g_sm), *ordered(d_big, d_sm),
            *ordered(nm_big, nm_sm), *ordered(nv_big, nv_sm))
```

```python
import functools

import jax
import jax.numpy as jnp
from jax import lax
from jax.experimental import pallas as pl
from jax.experimental.pallas import tpu as pltpu

F32 = jnp.float32
BF16 = jnp.bfloat16
MESH = pl.DeviceIdType.MESH
N_DEV = 8
LANES = 128
SUBLANES = 8

GROUPS = 3
HEADS = 8
HEAD_DIM = 64
WIDTH = HEADS * HEAD_DIM
ATT_BLOCK = 128
DILATIONS = (1, 4, 16)
N_BACK = 128
CONV_K = 4
CHUNK = 64
SCAN_CHUNKS = 4
QKV_B = 3 * WIDTH
QKV_A = GROUPS * 3 * WIDTH
BA_PAD = 512
NORM_EPS = 1e-6
ROPE_THETA = 10000.0
ADAM_LR, ADAM_B1, ADAM_B2, ADAM_EPS, ADAM_WD, ADAM_STEP = 0.001, 0.9, 0.999, 1e-08, 0.01, 10

VMEM_LIMIT = 56 * 1024 * 1024

OFF_ZA = QKV_A
OFF_QKVB = OFF_ZA + WIDTH
OFF_ZB = OFF_QKVB + QKV_B
OFF_BA = OFF_ZB + WIDTH
OFF_GATE = OFF_BA + 2 * HEADS


def _params(*sem):
    return pltpu.CompilerParams(dimension_semantics=sem, vmem_limit_bytes=VMEM_LIMIT)


def _dg(a, b, ca, cb):
    nb = a.ndim - 2
    batch = tuple(range(nb))
    return lax.dot_general(a, b, (((nb + ca,), (nb + cb,)), (batch, batch)), preferred_element_type=F32)


@jax.custom_vjp
def _mm(a, b):
    return _dg(a.astype(BF16), b.astype(BF16), 1, 0)


def _mm_fwd(a, b):
    return _mm(a, b), (a.astype(BF16), b.astype(BF16))


def _mm_bwd(res, ct):
    a16, b16 = res
    c16 = ct.astype(BF16)
    return _dg(c16, b16, 1, 1), _dg(a16, c16, 0, 0)


_mm.defvjp(_mm_fwd, _mm_bwd)


@jax.custom_vjp
def _mm_nt(a, b):
    return _dg(a.astype(BF16), b.astype(BF16), 1, 1)


def _mm_nt_fwd(a, b):
    return _mm_nt(a, b), (a.astype(BF16), b.astype(BF16))


def _mm_nt_bwd(res, ct):
    a16, b16 = res
    c16 = ct.astype(BF16)
    return _dg(c16, b16, 1, 0), _dg(c16, a16, 0, 0)


_mm_nt.defvjp(_mm_nt_fwd, _mm_nt_bwd)


@jax.custom_vjp
def _mm_tn(a, b):
    return _dg(a.astype(BF16), b.astype(BF16), 0, 0)


def _mm_tn_fwd(a, b):
    return _mm_tn(a, b), (a.astype(BF16), b.astype(BF16))


def _mm_tn_bwd(res, ct):
    a16, b16 = res
    c16 = ct.astype(BF16)
    return _dg(b16, c16, 1, 1), _dg(a16, c16, 1, 0)


_mm_tn.defvjp(_mm_tn_fwd, _mm_tn_bwd)


@jax.custom_vjp
def _mm_tap(a, w16, tap):
    return _dg(a.astype(BF16), w16, 1, 0)


def _mm_tap_fwd(a, w16, tap):
    return _mm_tap(a, w16, tap), (a.astype(BF16), w16)


def _mm_tap_bwd(res, ct):
    a16, w16 = res
    c16 = ct.astype(BF16)
    return _dg(c16, w16, 1, 1), jnp.zeros_like(w16), _dg(a16, c16, 0, 0)


_mm_tap.defvjp(_mm_tap_fwd, _mm_tap_bwd)


def _split16(a):
    hi = a.astype(BF16)
    lo = (a - hi.astype(F32)).astype(BF16)
    return hi, lo


def _dot3(a, b, ca, cb):
    ah, al = _split16(a)
    bh, bl = _split16(b)
    return _dg(ah, bh, ca, cb) + (_dg(ah, bl, ca, cb) + _dg(al, bh, ca, cb))


def _tri_inv_impl(a):
    n = a.shape[-1]
    shp = (1,) * (a.ndim - 2) + (n, n)
    eye = (lax.broadcasted_iota(jnp.int32, shp, a.ndim - 2) == lax.broadcasted_iota(jnp.int32, shp, a.ndim - 1)).astype(F32)
    x = eye - a
    p = a
    for it in range(5):
        dot = _dot3 if it < 2 else (lambda u, v, cu, cv: _dg(u.astype(BF16), v.astype(BF16), cu, cv))
        p = dot(p, p, 1, 0)
        x = x + dot(x, p, 1, 0)
    return x


@jax.custom_vjp
def _tri_inv(a):
    return _tri_inv_impl(a)


def _tri_inv_fwd(a):
    t = _tri_inv_impl(a)
    return t, t


def _tri_inv_bwd(t, ct):
    t16 = t.astype(BF16)
    return (-_dg(_dg(t16, ct.astype(BF16), 0, 0).astype(BF16), t16, 1, 1),)


_tri_inv.defvjp(_tri_inv_fwd, _tri_inv_bwd)


@jax.custom_vjp
def _tri_inv_saved(a, t):
    return t


def _tri_inv_saved_fwd(a, t):
    return t, t


def _tri_inv_saved_bwd(t, ct):
    return _tri_inv_bwd(t, ct) + (jnp.zeros_like(t),)


_tri_inv_saved.defvjp(_tri_inv_saved_fwd, _tri_inv_saved_bwd)


def _sigmoid(x):
    return 1.0 / (1.0 + jnp.exp(-x))


def _silu(x):
    return x * _sigmoid(x)


def _softplus(x):
    return jnp.maximum(x, 0.0) + jnp.log(1.0 + jnp.exp(-jnp.abs(x)))


def _rmsnorm(x, w):
    return x * lax.rsqrt(jnp.mean(x * x, axis=-1, keepdims=True) + NORM_EPS) * w


def _row_block(rows, cap):
    best = None
    for cand in range(SUBLANES, min(rows, cap) + 1, SUBLANES):
        if rows % cand == 0:
            best = cand
    assert best is not None, rows
    return best


def _mesh_peers():
    x, y, c = lax.axis_index("x"), lax.axis_index("y"), lax.axis_index("c")
    me = 4 * x + 2 * y + c
    peers = []
    for k in range(1, N_DEV):
        px = 1 - x if (k >> 2) & 1 else x
        py = 1 - y if (k >> 1) & 1 else y
        pc = 1 - c if k & 1 else c
        peers.append(((px, py, pc), 4 * px + 2 * py + pc))
    return me, peers


N_CHIPS = 4
OTHER_CHIPS = 3


def _chip_peers():
    x, y, c = lax.axis_index("x"), lax.axis_index("y"), lax.axis_index("c")
    return x, y, c, [(1 - x, y), (x, 1 - y), (1 - x, 1 - y)]


def _all_gather(shards):
    n_arr = len(shards)
    per = 1 + 2 * OTHER_CHIPS

    def body(*refs):
        in_refs, out_refs = refs[:n_arr], refs[n_arr:2 * n_arr]
        send_sems, recv_sems, loc_sems = refs[2 * n_arr:]
        x, y, c, chips = _chip_peers()
        me, sibling = (x, y, c), (x, y, 1 - c)

        def slot(px, py, pc):
            return 4 * px + 2 * py + pc

        def copy(i, k, block, to, src=None):
            dst = out_refs[i].at[slot(*block)]
            return pltpu.make_async_remote_copy(src_ref=dst if src is None else src, dst_ref=dst,
                                                send_sem=send_sems.at[i * per + k], recv_sem=recv_sems.at[i * per + k],
                                                device_id=to, device_id_type=MESH)

        own = [pltpu.make_async_copy(in_refs[i], out_refs[i].at[slot(*me)], loc_sems.at[i]) for i in range(n_arr)]
        for cp in own:
            cp.start()
        first = []
        for i in range(n_arr):
            first += [copy(i, 1 + j, me, (*chip, c), src=in_refs[i]) for j, chip in enumerate(chips)]
            first.append(copy(i, 0, me, sibling, src=in_refs[i]))
        for cp in first:
            cp.start()
        passed = []
        for j, chip in enumerate(chips):
            for i in range(n_arr):
                copy(i, 1 + j, (*chip, c), me).wait_recv()
                fwd = copy(i, 1 + OTHER_CHIPS + j, (*chip, c), sibling)
                fwd.start()
                passed.append(fwd)
        for i in range(n_arr):
            copy(i, 0, sibling, me).wait_recv()
            for j, chip in enumerate(chips):
                copy(i, 1 + OTHER_CHIPS + j, (*chip, 1 - c), me).wait_recv()
        for cp in first + passed:
            cp.wait_send()
        for cp in own:
            cp.wait()

    any_spec = pl.BlockSpec(memory_space=pl.ANY)
    return pl.pallas_call(
        body, name="weights_all_gather",
        out_shape=tuple(jax.ShapeDtypeStruct((N_DEV,) + a.shape, a.dtype) for a in shards),
        in_specs=[any_spec] * n_arr, out_specs=tuple([any_spec] * n_arr),
        scratch_shapes=[pltpu.SemaphoreType.DMA((n_arr * per,)), pltpu.SemaphoreType.DMA((n_arr * per,)),
                        pltpu.SemaphoreType.DMA((n_arr,))],
    )(*shards)


class _Exchange:
    def __init__(self, arrays, out_shapes, n_sem, copies):
        self.arrays, self.out_shapes, self.n_sem, self.copies = list(arrays), list(out_shapes), n_sem, copies


def _sibling_exchange(slabs):
    n_arr = len(slabs)

    def copies(in_refs, out_refs, send_sems, recv_sems, loc_sems):
        x, y, c, _ = _chip_peers()
        sends = [pltpu.make_async_remote_copy(src_ref=in_refs[i].at[2 * q + (1 - c)], dst_ref=out_refs[i].at[q],
                                              send_sem=send_sems.at[i * N_CHIPS + q], recv_sem=recv_sems.at[i * N_CHIPS + q],
                                              device_id=(x, y, 1 - c), device_id_type=MESH)
                 for i in range(n_arr) for q in range(N_CHIPS)]

        def start():
            for cp in sends:
                cp.start()

        def finish():
            for cp in sends:
                cp.wait_recv()
            for cp in sends:
                cp.wait_send()

        return start, finish

    return _Exchange(slabs, [jax.ShapeDtypeStruct((N_CHIPS,) + a.shape[1:], a.dtype) for a in slabs],
                     n_arr * N_CHIPS, copies)


def _pair_sum(slabs, from_sibling, core, name):
    _, rows, cols = slabs.shape
    tr = _row_block(rows, max(SUBLANES, (256 * 1024) // cols // SUBLANES * SUBLANES))

    def body(core_ref, a_ref, b_ref, o_ref):
        o_ref[...] = (a_ref[...] + b_ref[...]).astype(BF16)

    grid_spec = pltpu.PrefetchScalarGridSpec(
        num_scalar_prefetch=1, grid=(N_CHIPS, rows // tr),
        in_specs=[pl.BlockSpec((None, tr, cols), lambda q, r, core_ref: (2 * q + core_ref[0], r, 0)),
                  pl.BlockSpec((None, tr, cols), lambda q, r, core_ref: (q, r, 0))],
        out_specs=pl.BlockSpec((None, tr, cols), lambda q, r, core_ref: (q, r, 0)))
    return pl.pallas_call(
        body, name=name, grid_spec=grid_spec,
        out_shape=jax.ShapeDtypeStruct((N_CHIPS, rows, cols), BF16),
        compiler_params=_params("parallel", "parallel"),
    )(core, slabs, from_sibling)


def _chip_exchange(partials):
    n_arr = len(partials)

    def copies(in_refs, out_refs, send_sems, recv_sems, loc_sems):
        x, y, c, chips = _chip_peers()
        mine = 2 * x + y
        own = [pltpu.make_async_copy(in_refs[i].at[mine], out_refs[i].at[mine], loc_sems.at[i]) for i in range(n_arr)]

        def copy(i, j, chip, src_slot, dst_slot):
            return pltpu.make_async_remote_copy(src_ref=in_refs[i].at[src_slot], dst_ref=out_refs[i].at[dst_slot],
                                                send_sem=send_sems.at[i * OTHER_CHIPS + j],
                                                recv_sem=recv_sems.at[i * OTHER_CHIPS + j],
                                                device_id=(*chip, c), device_id_type=MESH)

        sends = [copy(i, j, chip, 2 * chip[0] + chip[1], mine) for j, chip in enumerate(chips) for i in range(n_arr)]
        recvs = [copy(i, j, chip, mine, 2 * chip[0] + chip[1]) for j, chip in enumerate(chips) for i in range(n_arr)]

        def start():
            for cp in own + sends:
                cp.start()

        def finish():
            for cp in recvs:
                cp.wait_recv()
            for cp in sends:
                cp.wait_send()
            for cp in own:
                cp.wait()

        return start, finish

    return _Exchange(partials, [jax.ShapeDtypeStruct(a.shape, a.dtype) for a in partials], n_arr * OTHER_CHIPS, copies)


def _small_all_reduce(part):
    rows = part.shape[0]

    def body(p_ref, o_ref, buf_ref, send_sems, recv_sems):
        me, peers = _mesh_peers()
        buf_ref[me] = p_ref[...]
        sends = []
        for k, (dev, pid) in enumerate(peers):
            cp = pltpu.make_async_remote_copy(src_ref=p_ref, dst_ref=buf_ref.at[me], send_sem=send_sems.at[k],
                                              recv_sem=recv_sems.at[k], device_id=dev, device_id_type=MESH)
            cp.start()
            sends.append(cp)
        for k, (dev, pid) in enumerate(peers):
            pltpu.make_async_remote_copy(src_ref=p_ref, dst_ref=buf_ref.at[pid], send_sem=send_sems.at[k],
                                         recv_sem=recv_sems.at[k], device_id=dev, device_id_type=MESH).wait_recv()
        for cp in sends:
            cp.wait_send()
        acc = buf_ref[0]
        for i in range(1, N_DEV):
            acc = acc + buf_ref[i]
        o_ref[...] = acc

    vmem = pl.BlockSpec(memory_space=pltpu.VMEM)
    return pl.pallas_call(
        body, name="small_all_reduce",
        out_shape=jax.ShapeDtypeStruct(part.shape, F32),
        in_specs=[vmem], out_specs=vmem,
        scratch_shapes=[pltpu.VMEM((N_DEV, rows, LANES), F32), pltpu.SemaphoreType.DMA((N_DEV - 1,)),
                        pltpu.SemaphoreType.DMA((N_DEV - 1,))],
    )(part)


def _adamw_vals(w, g, m, v):
    m = ADAM_B1 * m + (1.0 - ADAM_B1) * g
    v = ADAM_B2 * v + (1.0 - ADAM_B2) * (g * g)
    m_hat = m / (1.0 - ADAM_B1 ** ADAM_STEP)
    v_hat = v / (1.0 - ADAM_B2 ** ADAM_STEP)
    delta = -ADAM_LR * (m_hat / (jnp.sqrt(v_hat) + ADAM_EPS) + ADAM_WD * w)
    return delta, m, v


def _adamw(contrib, w, m, v, name):
    n, rows, cols = contrib.shape
    tr = _row_block(rows, max(SUBLANES, (128 * 1024) // cols // SUBLANES * SUBLANES))

    def body(c_ref, w_ref, m_ref, v_ref, g_ref, d_ref, nm_ref, nv_ref):
        g = c_ref[0].astype(F32)
        for i in range(1, n):
            g = g + c_ref[i].astype(F32)
        delta, nm, nv = _adamw_vals(w_ref[...], g, m_ref[...], v_ref[...])
        g_ref[...] = g
        d_ref[...] = delta
        nm_ref[...] = nm
        nv_ref[...] = nv

    row = pl.BlockSpec((tr, cols), lambda i: (i, 0))
    shp = jax.ShapeDtypeStruct((rows, cols), F32)
    return pl.pallas_call(
        body, name=name, grid=(rows // tr,),
        in_specs=[pl.BlockSpec((n, tr, cols), lambda i: (0, i, 0)), row, row, row],
        out_specs=(row, row, row, row), out_shape=(shp, shp, shp, shp),
        compiler_params=_params("parallel"),
    )(contrib, w, m, v)


def _lane_block(n, cap):
    if n <= cap:
        return n
    best = None
    for cand in range(LANES, cap + 1, LANES):
        if n % cand == 0:
            best = cand
    assert best is not None, n
    return best


def _matmul(a, b, out_dtype, name, mode="nn", tm=1024, tn=1024, tk=1024, exchange=None):
    g = a.shape[0]
    m, k = (a.shape[2], a.shape[1]) if mode == "tn" else (a.shape[1], a.shape[2])
    n = b.shape[1] if mode == "nt" else b.shape[2]
    tm, tn, tk = _lane_block(m, tm), _lane_block(n, tn), _lane_block(k, tk)
    nk = k // tk
    grid = (g, m // tm, n // tn, nk)
    a_spec = (pl.BlockSpec((None, tk, tm), lambda gi, i, j, kk: (gi, kk, i)) if mode == "tn" else
              pl.BlockSpec((None, tm, tk), lambda gi, i, j, kk: (gi, i, kk)))
    b_spec = (pl.BlockSpec((None, tn, tk), lambda gi, i, j, kk: (gi, j, kk)) if mode == "nt" else
              pl.BlockSpec((None, tk, tn), lambda gi, i, j, kk: (gi, kk, j)))
    ca, cb = (0 if mode == "tn" else 1), (1 if mode == "nt" else 0)
    n_ex = 0 if exchange is None else len(exchange.arrays)

    def body(a_ref, b_ref, *rest):
        ex_in, o_ref, ex_out, scratch = rest[:n_ex], rest[n_ex], rest[n_ex + 1:2 * n_ex + 1], rest[2 * n_ex + 1:]
        if exchange is not None:
            start, finish = exchange.copies(ex_in, ex_out, *scratch[-3:])
            pids = [pl.program_id(ax) for ax in range(4)]
            pl.when((pids[0] == 0) & (pids[1] == 0) & (pids[2] == 0) & (pids[3] == 0))(start)
        part = _dg(a_ref[...], b_ref[...], ca, cb)
        if nk == 1:
            o_ref[...] = part.astype(o_ref.dtype)
        else:
            acc_ref = scratch[0]
            kk = pl.program_id(3)

            @pl.when(kk == 0)
            def _():
                acc_ref[...] = part

            @pl.when((kk > 0) & (kk < nk - 1))
            def _():
                acc_ref[...] += part

            @pl.when(kk == nk - 1)
            def _():
                o_ref[...] = (acc_ref[...] + part).astype(o_ref.dtype)
        if exchange is not None:
            pl.when((pids[0] == grid[0] - 1) & (pids[1] == grid[1] - 1) & (pids[2] == grid[2] - 1)
                    & (pids[3] == grid[3] - 1))(finish)

    any_spec = pl.BlockSpec(memory_space=pl.ANY)
    scratch_shapes = [] if nk == 1 else [pltpu.VMEM((tm, tn), F32)]
    out_shape = [jax.ShapeDtypeStruct((g, m, n), out_dtype)]
    if exchange is not None:
        scratch_shapes += [pltpu.SemaphoreType.DMA((exchange.n_sem,)), pltpu.SemaphoreType.DMA((exchange.n_sem,)),
                           pltpu.SemaphoreType.DMA((n_ex,))]
        out_shape += exchange.out_shapes
    outs = pl.pallas_call(
        body, name=name, grid=grid,
        in_specs=[a_spec, b_spec] + [any_spec] * n_ex,
        out_specs=tuple([pl.BlockSpec((None, tm, tn), lambda gi, i, j, kk: (gi, i, j))] + [any_spec] * n_ex),
        out_shape=tuple(out_shape),
        scratch_shapes=scratch_shapes,
        compiler_params=(_params("parallel", "parallel", "parallel", "arbitrary") if exchange is None else
                         _params("arbitrary", "arbitrary", "arbitrary", "arbitrary")),
    )(a, b, *([] if exchange is None else exchange.arrays))
    return outs[0] if exchange is None else outs


def _rms_fwd(x, w):
    s, d = x.shape
    tm = _row_block(s, 512)

    def body(x_ref, w_ref, h_ref):
        h_ref[...] = _rmsnorm(x_ref[...], w_ref[...]).astype(BF16)

    return pl.pallas_call(
        body, name="input_rmsnorm", grid=(s // tm,),
        in_specs=[pl.BlockSpec((tm, d), lambda i: (i, 0)), pl.BlockSpec((1, d), lambda i: (0, 0))],
        out_specs=pl.BlockSpec((tm, d), lambda i: (i, 0)),
        out_shape=jax.ShapeDtypeStruct((s, d), BF16),
        compiler_params=_params("parallel"),
    )(x, w)


def _rms_bwd(x, w, dh_stacked, dh_parts, dx_res):
    s, d = x.shape
    tm = _row_block(s, 256)
    n_parts = 1 + len(dh_parts)

    def body(x_ref, w_ref, *rest):
        part_refs = rest[:n_parts]
        res_ref, gx_ref, gw_ref = rest[n_parts:]
        dh = part_refs[0][...]
        for r in part_refs[1:]:
            dh = dh + r[...]
        _, vjp = jax.vjp(_rmsnorm, x_ref[...], w_ref[...])
        dx, dw = vjp(dh)
        gx_ref[...] = dx + res_ref[...]

        @pl.when(pl.program_id(0) == 0)
        def _():
            gw_ref[...] = jnp.zeros_like(gw_ref)

        gw_ref[...] += dw

    row = pl.BlockSpec((tm, d), lambda i: (i, 0))
    vec = pl.BlockSpec((1, d), lambda i: (0, 0))
    return pl.pallas_call(
        body, name="input_rmsnorm_bwd", grid=(s // tm,),
        in_specs=[row, vec, pl.BlockSpec((None, tm, d), lambda i: (0, i, 0))] + [row] * (n_parts - 1) + [row],
        out_specs=(row, vec),
        out_shape=(jax.ShapeDtypeStruct((s, d), F32), jax.ShapeDtypeStruct((1, d), F32)),
        compiler_params=_params("arbitrary"),
    )(x, w, dh_stacked, *dh_parts, dx_res)


def _lane_masks(rows):
    lane = lax.broadcasted_iota(jnp.int32, (rows, LANES), 1)
    return lane < HEAD_DIM, (lane & (HEAD_DIM - 1)) < HEAD_DIM // 2


def _swap_halves(t, lo_half):
    return jnp.where(lo_half, pltpu.roll(t, LANES - HEAD_DIM // 2, 1), pltpu.roll(t, HEAD_DIM // 2, 1))


def _rope(t, cos, sin_signed, lo_half):
    return t * cos + _swap_halves(t, lo_half) * sin_signed


def _rope_bwd(d, cos, sin_signed, lo_half):
    return d * cos - _swap_halves(d, lo_half) * sin_signed


def _window_mask(first):
    qi = lax.broadcasted_iota(jnp.int32, (2 * ATT_BLOCK, 2 * ATT_BLOCK), 0) & (ATT_BLOCK - 1)
    kj = lax.broadcasted_iota(jnp.int32, (2 * ATT_BLOCK, 2 * ATT_BLOCK), 1)
    dist = qi + ATT_BLOCK - kj
    return (dist >= 0) & (dist <= N_BACK) & ((kj >= ATT_BLOCK) | jnp.logical_not(first))


def _stack_heads(t, head0):
    zero = jnp.zeros_like(t)
    return jnp.concatenate([jnp.where(head0, t, zero), jnp.where(head0, zero, t)], axis=0)


def _unstack_heads(t2, head0):
    return jnp.where(head0, t2[:ATT_BLOCK], t2[ATT_BLOCK:])


def _blocks_per_subsequence(g, nb):
    return lax.shift_right_logical(jnp.int32(nb), 2 * g)


def _attn_fwd(qkv, cos, sin):
    _, s, _ = qkv.shape
    nb = s // ATT_BLOCK

    def body(qkv_ref, cos_ref, sin_ref, o_ref, lse_ref, kp_ref, vp_ref):
        g, t = pl.program_id(0), pl.program_id(1)
        first = (t & (_blocks_per_subsequence(g, nb) - 1)) == 0

        @pl.when(first)
        def _():
            kp_ref[...] = jnp.zeros_like(kp_ref)
            vp_ref[...] = jnp.zeros_like(vp_ref)

        cos_b, sin_b = cos_ref[...], sin_ref[...]
        head0, lo_half = _lane_masks(ATT_BLOCK)
        valid = _window_mask(first)
        for sl in range(WIDTH // LANES):
            cq = pl.ds(sl * LANES, LANES)
            ck = pl.ds(WIDTH + sl * LANES, LANES)
            cv = pl.ds(2 * WIDTH + sl * LANES, LANES)
            qr = (_rope(qkv_ref[:, cq], cos_b, sin_b, lo_half) * (HEAD_DIM ** -0.5)).astype(BF16)
            kr = _rope(qkv_ref[:, ck], cos_b, sin_b, lo_half).astype(BF16)
            v16 = qkv_ref[:, cv].astype(BF16)
            kcat = jnp.concatenate([kp_ref[:, cq], kr], axis=0)
            vcat = jnp.concatenate([vp_ref[:, cq], v16], axis=0)
            sc = jnp.where(valid, _dg(_stack_heads(qr, head0), kcat, 1, 1), -jnp.inf)
            mx = jnp.max(sc, axis=1, keepdims=True)
            p = jnp.exp(sc - mx)
            den = jnp.sum(p, axis=1, keepdims=True)
            o_ref[:, cq] = _unstack_heads(_dg((p * (1.0 / den)).astype(BF16), vcat, 1, 0), head0)
            lse2 = mx + jnp.log(den)
            lse_ref[:, cq] = jnp.where(head0, lse2[:ATT_BLOCK], lse2[ATT_BLOCK:])
            kp_ref[:, cq] = kr
            vp_ref[:, cq] = v16

    blk = lambda w: pl.BlockSpec((None, ATT_BLOCK, w), lambda g, t: (g, t, 0))
    shp = jax.ShapeDtypeStruct((GROUPS, s, WIDTH), F32)
    return pl.pallas_call(
        body, name="dilated_attention_fwd", grid=(GROUPS, nb),
        in_specs=[blk(3 * WIDTH), blk(LANES), blk(LANES)],
        out_specs=(blk(WIDTH), blk(WIDTH)), out_shape=(shp, shp),
        scratch_shapes=[pltpu.VMEM((ATT_BLOCK, WIDTH), BF16), pltpu.VMEM((ATT_BLOCK, WIDTH), BF16)],
        compiler_params=_params("arbitrary", "arbitrary"),
    )(qkv, cos, sin)


def _attn_bwd(qkv, cos, sin, o, lse, do, dlse):
    _, s, _ = qkv.shape
    nb = s // ATT_BLOCK

    def body(qkv_ref, cos_ref, sin_ref, cosp_ref, sinp_ref, o_ref, lse_ref, do_ref, dlse_ref,
             dqkv_ref, kp_ref, vp_ref, dka_ref, dva_ref, dqp_ref):
        g, t = pl.program_id(0), pl.program_id(1)
        first = (t & (_blocks_per_subsequence(g, nb) - 1)) == 0
        active = t < nb
        head0, lo_half = _lane_masks(ATT_BLOCK)
        cos_p, sin_p = cosp_ref[...], sinp_ref[...]

        @pl.when(t == 0)
        def _():
            dka_ref[...] = jnp.zeros_like(dka_ref)
            dva_ref[...] = jnp.zeros_like(dva_ref)
            dqp_ref[...] = jnp.zeros_like(dqp_ref)

        dqkv_ref[:, pl.ds(0, WIDTH)] = dqp_ref[...]

        @pl.when(active & first)
        def _():
            kp_ref[...] = jnp.zeros_like(kp_ref)
            vp_ref[...] = jnp.zeros_like(vp_ref)

        @pl.when(active)
        def _():
            cos_b, sin_b = cos_ref[...], sin_ref[...]
            valid = _window_mask(first)
            for sl in range(WIDTH // LANES):
                cq = pl.ds(sl * LANES, LANES)
                ck = pl.ds(WIDTH + sl * LANES, LANES)
                cv = pl.ds(2 * WIDTH + sl * LANES, LANES)
                qr = (_rope(qkv_ref[:, cq], cos_b, sin_b, lo_half) * (HEAD_DIM ** -0.5)).astype(BF16)
                kr = _rope(qkv_ref[:, ck], cos_b, sin_b, lo_half).astype(BF16)
                v16 = qkv_ref[:, cv].astype(BF16)
                kcat = jnp.concatenate([kp_ref[:, cq], kr], axis=0)
                vcat = jnp.concatenate([vp_ref[:, cq], v16], axis=0)
                do_b = do_ref[:, cq]
                do16 = do_b.astype(BF16)
                lse_b = lse_ref[:, cq]
                cterm = dlse_ref[:, cq] - do_b * o_ref[:, cq]
                dqs, dkc, dvc = [], None, None
                for hm in (head0, jnp.logical_not(head0)):
                    qm = jnp.where(hm, qr, jnp.zeros_like(qr))
                    dom = jnp.where(hm, do16, jnp.zeros_like(do16))
                    sc = jnp.where(valid[:ATT_BLOCK], _dg(qm, kcat, 1, 1), -jnp.inf)
                    lse_h = jnp.max(jnp.where(hm, lse_b, -jnp.inf), axis=1, keepdims=True)
                    c = jnp.sum(jnp.where(hm, cterm, 0.0), axis=1, keepdims=True)
                    p = jnp.exp(sc - lse_h)
                    ds16 = (p * (_dg(dom, vcat, 1, 1) + c)).astype(BF16)
                    dv_h, dk_h = _dg(p.astype(BF16), dom, 0, 0), _dg(ds16, qm, 0, 0)
                    dvc = dv_h if dvc is None else dvc + dv_h
                    dkc = dk_h if dkc is None else dkc + dk_h
                    dqs.append(_dg(ds16, kcat, 1, 0))
                dq = jnp.where(head0, dqs[0], dqs[1]) * (HEAD_DIM ** -0.5)
                dqp_ref[:, cq] = _rope_bwd(dq, cos_b, sin_b, lo_half).astype(BF16)
                dqkv_ref[:, ck] = _rope_bwd(dka_ref[:, cq] + dkc[:ATT_BLOCK], cos_p, sin_p, lo_half).astype(BF16)
                dqkv_ref[:, cv] = (dva_ref[:, cq] + dvc[:ATT_BLOCK]).astype(BF16)
                dka_ref[:, cq] = dkc[ATT_BLOCK:]
                dva_ref[:, cq] = dvc[ATT_BLOCK:]
                kp_ref[:, cq] = kr
                vp_ref[:, cq] = v16

        @pl.when(jnp.logical_not(active))
        def _():
            for sl in range(WIDTH // LANES):
                cq = pl.ds(sl * LANES, LANES)
                dqkv_ref[:, pl.ds(WIDTH + sl * LANES, LANES)] = _rope_bwd(dka_ref[:, cq], cos_p, sin_p, lo_half).astype(BF16)
                dqkv_ref[:, pl.ds(2 * WIDTH + sl * LANES, LANES)] = dva_ref[:, cq].astype(BF16)

    cur = lambda w: pl.BlockSpec((None, ATT_BLOCK, w), lambda g, t: (g, jnp.minimum(t, nb - 1), 0))
    prev = lambda w: pl.BlockSpec((None, ATT_BLOCK, w), lambda g, t: (g, jnp.maximum(t - 1, 0), 0))
    return pl.pallas_call(
        body, name="dilated_attention_bwd", grid=(GROUPS, nb + 1),
        in_specs=[cur(3 * WIDTH), cur(LANES), cur(LANES), prev(LANES), prev(LANES),
                  cur(WIDTH), cur(WIDTH), cur(WIDTH), cur(WIDTH)],
        out_specs=prev(3 * WIDTH), out_shape=jax.ShapeDtypeStruct((GROUPS, s, 3 * WIDTH), BF16),
        scratch_shapes=[pltpu.VMEM((ATT_BLOCK, WIDTH), BF16), pltpu.VMEM((ATT_BLOCK, WIDTH), BF16),
                        pltpu.VMEM((ATT_BLOCK, WIDTH), F32), pltpu.VMEM((ATT_BLOCK, WIDTH), F32),
                        pltpu.VMEM((ATT_BLOCK, WIDTH), BF16)],
        compiler_params=_params("arbitrary", "arbitrary"),
    )(qkv, cos, sin, cos, sin, o, lse, do, dlse)


CONV_PAD = SUBLANES


def _gdn_post(y, is_q, is_k):
    head0, _ = _lane_masks(y.shape[0])
    c = _silu(y)
    sq = c * c
    ss0 = jnp.sum(jnp.where(head0, sq, 0.0), axis=1, keepdims=True)
    ss1 = jnp.sum(jnp.where(head0, 0.0, sq), axis=1, keepdims=True)
    r = jnp.where(head0, lax.rsqrt(ss0 + NORM_EPS), lax.rsqrt(ss1 + NORM_EPS))
    scale = jnp.where(is_q, HEAD_DIM ** -0.5, 1.0).astype(F32)
    return jnp.where(is_q | is_k, c * r * scale, c)


def _conv_rows(xp_ref, w, c0, rows):
    y = w[0:1, :] * xp_ref[pl.ds(c0 + CONV_PAD - (CONV_K - 1), rows), :]
    for k in range(1, CONV_K):
        y = y + w[k:k + 1, :] * xp_ref[pl.ds(c0 + CONV_PAD - (CONV_K - 1) + k, rows), :]
    return y


def _gdn_pre_fwd(proj_r, conv8, col0):
    s = proj_r.shape[0]
    tr = _row_block(s, 512)
    nblk = QKV_B // LANES
    nq = WIDTH // LANES

    def body(x_ref, w_ref, out_ref, xp_ref):
        j = pl.program_id(0)
        is_q, is_k = j < nq, (j >= nq) & (j < 2 * nq)
        xp_ref[pl.ds(0, CONV_PAD), :] = jnp.zeros((CONV_PAD, LANES), F32)
        xp_ref[pl.ds(CONV_PAD, s), :] = x_ref[...]
        w = w_ref[...]
        for c in range(s // tr):
            out_ref[pl.ds(c * tr, tr), :] = _gdn_post(_conv_rows(xp_ref, w, c * tr, tr), is_q, is_k)

    return pl.pallas_call(
        body, name="gdn_conv_fwd", grid=(nblk,),
        in_specs=[pl.BlockSpec((s, LANES), lambda j: (0, col0 + j)), pl.BlockSpec((SUBLANES, LANES), lambda j: (0, j))],
        out_specs=pl.BlockSpec((s, LANES), lambda j: (0, j)),
        out_shape=jax.ShapeDtypeStruct((s, QKV_B), F32),
        scratch_shapes=[pltpu.VMEM((s + CONV_PAD, LANES), F32)],
        compiler_params=_params("parallel"),
    )(proj_r, conv8)


def _gdn_pre_bwd(proj_r, conv8, dc, col0):
    s = proj_r.shape[0]
    tr = _row_block(s, 512)
    nblk = QKV_B // LANES
    nq = WIDTH // LANES

    def body(x_ref, w_ref, dc_ref, dx_ref, dw_ref, xp_ref, dyp_ref):
        j = pl.program_id(0)
        is_q, is_k = j < nq, (j >= nq) & (j < 2 * nq)
        xp_ref[pl.ds(0, CONV_PAD), :] = jnp.zeros((CONV_PAD, LANES), F32)
        xp_ref[pl.ds(CONV_PAD, s), :] = x_ref[...]
        dyp_ref[pl.ds(s, CONV_PAD), :] = jnp.zeros((CONV_PAD, LANES), F32)
        w = w_ref[...]
        for c in range(s // tr):
            y = _conv_rows(xp_ref, w, c * tr, tr)
            _, vjp = jax.vjp(lambda yy: _gdn_post(yy, is_q, is_k), y)
            dyp_ref[pl.ds(c * tr, tr), :] = vjp(dc_ref[pl.ds(c * tr, tr), :])[0]
        dws = [jnp.zeros((1, LANES), F32) for _ in range(CONV_K)]
        for c in range(s // tr):
            c0 = c * tr
            dy = dyp_ref[pl.ds(c0, tr), :]
            dx = w[0:1, :] * dyp_ref[pl.ds(c0 + CONV_K - 1, tr), :]
            for k in range(1, CONV_K):
                dx = dx + w[k:k + 1, :] * dyp_ref[pl.ds(c0 + CONV_K - 1 - k, tr), :]
            dx_ref[pl.ds(c0, tr), :] = dx.astype(BF16)
            for k in range(CONV_K):
                xs = xp_ref[pl.ds(c0 + CONV_PAD - (CONV_K - 1) + k, tr), :]
                dws[k] = dws[k] + jnp.sum(dy * xs, axis=0, keepdims=True)
        row = lax.broadcasted_iota(jnp.int32, (SUBLANES, LANES), 0)
        dwb = jnp.zeros((SUBLANES, LANES), F32)
        for k in range(CONV_K):
            dwb = dwb + jnp.where(row == k, dws[k], 0.0)
        dw_ref[...] = dwb

    return pl.pallas_call(
        body, name="gdn_conv_bwd", grid=(nblk,),
        in_specs=[pl.BlockSpec((s, LANES), lambda j: (0, col0 + j)), pl.BlockSpec((SUBLANES, LANES), lambda j: (0, j)),
                  pl.BlockSpec((s, LANES), lambda j: (0, j))],
        out_specs=(pl.BlockSpec((s, LANES), lambda j: (0, j)), pl.BlockSpec((SUBLANES, LANES), lambda j: (0, j))),
        out_shape=(jax.ShapeDtypeStruct((s, QKV_B), BF16), jax.ShapeDtypeStruct((SUBLANES, QKV_B), F32)),
        scratch_shapes=[pltpu.VMEM((s + CONV_PAD, LANES), F32), pltpu.VMEM((s + CONV_PAD, LANES), F32)],
        compiler_params=_params("parallel"),
    )(proj_r, conv8, dc)


def _gdn_chunk(q, k, v, bcol, acol, alog, dtb, gnw, state, t_saved=None):
    n = q.shape[-2]
    shp = (1, n, n)
    row = lax.broadcasted_iota(jnp.int32, shp, 1)
    col = lax.broadcasted_iota(jnp.int32, shp, 2)
    beta = _sigmoid(bcol)
    g = -jnp.exp(alog) * _softplus(acol + dtb)
    g_row = jnp.sum(jnp.where(row == col, g, 0.0), axis=-2, keepdims=True)
    big_g = jnp.sum(jnp.where(row >= col, g_row, 0.0), axis=-1, keepdims=True)
    big_g_row = jnp.sum(jnp.where(row <= col, g, 0.0), axis=-2, keepdims=True)
    decay_incl = jnp.exp(jnp.where(row >= col, big_g - big_g_row, -jnp.inf))
    decay_strict = jnp.where(row > col, decay_incl, 0.0)
    k_beta = k * beta
    a_mat = _mm_nt(k_beta, k) * decay_strict
    t_inv = _tri_inv(a_mat) if t_saved is None else _tri_inv_saved(a_mat, t_saved)
    e_g = jnp.exp(big_g)
    u = _mm(t_inv, v * beta)
    w = _mm(t_inv, k_beta * e_g)
    attn = _mm_nt(q, k) * decay_incl
    v_new = u - _mm(w, state)
    o = _mm(q * e_g, state) + _mm(attn, v_new)
    total = jnp.sum(g, axis=-2, keepdims=True)
    new_state = state * jnp.exp(total) + _mm_tn(k * jnp.exp(total - big_g), v_new)
    return _rmsnorm(o, gnw), new_state, t_inv


def _split_heads(x):
    return jnp.stack([x[:, h * HEAD_DIM:(h + 1) * HEAD_DIM] for h in range(HEADS)], axis=0)


def _merge_heads(x):
    return jnp.concatenate([x[h] for h in range(HEADS)], axis=1)


def _logit_columns(ba):
    lane = lax.broadcasted_iota(jnp.int32, ba.shape, 1)

    def cols(off):
        return jnp.stack([jnp.sum(jnp.where(lane == off + h, ba, 0.0), axis=1, keepdims=True) for h in range(HEADS)], axis=0)

    return cols(0), cols(HEADS)


def _logit_block(dbc, dac, shape):
    lane = lax.broadcasted_iota(jnp.int32, shape, 1)
    out = jnp.zeros(shape, F32)
    for h in range(HEADS):
        out = out + jnp.where(lane == h, dbc[h], 0.0) + jnp.where(lane == HEADS + h, dac[h], 0.0)
    return out


def _gdn_scan_fwd(cqkv, proj_r, ba_col, alog, dtb, gnw):
    s = cqkv.shape[0]
    nc = s // CHUNK
    span = SCAN_CHUNKS * CHUNK

    def body(q_ref, k_ref, v_ref, ba_ref, al_ref, dt_ref, gnw_ref, o_ref, st_ref, ti_ref, state_ref):
        @pl.when(pl.program_id(0) == 0)
        def _():
            state_ref[...] = jnp.zeros_like(state_ref)

        st = state_ref[...]
        for u in range(SCAN_CHUNKS):
            rows = pl.ds(u * CHUNK, CHUNK)
            st_ref[u] = st
            bcol, acol = _logit_columns(ba_ref[rows, :])
            o, st, t_inv = _gdn_chunk(_split_heads(q_ref[rows, :]), _split_heads(k_ref[rows, :]),
                                      _split_heads(v_ref[rows, :]), bcol, acol, al_ref[...], dt_ref[...], gnw_ref[...], st)
            o_ref[rows, :] = _merge_heads(o)
            ti_ref[u] = t_inv
        state_ref[...] = st

    part = lambda i: pl.BlockSpec((span, WIDTH), lambda n: (n, i))
    par = pl.BlockSpec((HEADS, 1, 1), lambda n: (0, 0, 0))
    per_chunk = pl.BlockSpec((SCAN_CHUNKS, HEADS, HEAD_DIM, HEAD_DIM), lambda n: (n, 0, 0, 0))
    per_chunk_shape = jax.ShapeDtypeStruct((nc, HEADS, HEAD_DIM, HEAD_DIM), F32)
    return pl.pallas_call(
        body, name="gdn_scan_fwd", grid=(nc // SCAN_CHUNKS,),
        in_specs=[part(0), part(1), part(2), pl.BlockSpec((span, LANES), lambda n: (n, ba_col)), par, par,
                  pl.BlockSpec((1, 1, HEAD_DIM), lambda n: (0, 0, 0))],
        out_specs=(part(0), per_chunk, per_chunk),
        out_shape=(jax.ShapeDtypeStruct((s, WIDTH), F32), per_chunk_shape, per_chunk_shape),
        scratch_shapes=[pltpu.VMEM((HEADS, HEAD_DIM, HEAD_DIM), F32)],
        compiler_params=_params("arbitrary"),
    )(cqkv, cqkv, cqkv, proj_r, alog, dtb, gnw)


def _gdn_scan_bwd(cqkv, proj_r, ba_col, alog, dtb, gnw, states, t_invs, do):
    s = cqkv.shape[0]
    nc = s // CHUNK
    span = SCAN_CHUNKS * CHUNK
    n_steps = nc // SCAN_CHUNKS

    def body(q_ref, k_ref, v_ref, ba_ref, al_ref, dt_ref, gnw_ref, st_ref, ti_ref, do_ref,
             dqkv_ref, dba_ref, dal_ref, ddt_ref, dgnw_ref, dstate_ref):
        @pl.when(pl.program_id(0) == 0)
        def _():
            dstate_ref[...] = jnp.zeros_like(dstate_ref)
            dal_ref[...] = jnp.zeros_like(dal_ref)
            ddt_ref[...] = jnp.zeros_like(ddt_ref)
            dgnw_ref[...] = jnp.zeros_like(dgnw_ref)

        dst = dstate_ref[...]
        for u in reversed(range(SCAN_CHUNKS)):
            rows = pl.ds(u * CHUNK, CHUNK)
            bcol, acol = _logit_columns(ba_ref[rows, :])
            _, vjp = jax.vjp(lambda *a, t_saved=ti_ref[u]: _gdn_chunk(*a, t_saved=t_saved)[:2],
                             _split_heads(q_ref[rows, :]), _split_heads(k_ref[rows, :]), _split_heads(v_ref[rows, :]),
                             bcol, acol, al_ref[...], dt_ref[...], gnw_ref[...], st_ref[u])
            dq, dk, dv, dbc, dac, dal, ddt, dgn, dst = vjp((_split_heads(do_ref[rows, :]), dst))
            dqkv_ref[rows, pl.ds(0, WIDTH)] = _merge_heads(dq)
            dqkv_ref[rows, pl.ds(WIDTH, WIDTH)] = _merge_heads(dk)
            dqkv_ref[rows, pl.ds(2 * WIDTH, WIDTH)] = _merge_heads(dv)
            dba_ref[rows, :] = _logit_block(dbc, dac, (CHUNK, LANES))
            dal_ref[...] += dal
            ddt_ref[...] += ddt
            dgnw_ref[...] += dgn
        dstate_ref[...] = dst

    rev = lambda n: n_steps - 1 - n
    part = lambda i: pl.BlockSpec((span, WIDTH), lambda n: (rev(n), i))
    par = pl.BlockSpec((HEADS, 1, 1), lambda n: (0, 0, 0))
    vec = pl.BlockSpec((1, 1, HEAD_DIM), lambda n: (0, 0, 0))
    par_shape = jax.ShapeDtypeStruct((HEADS, 1, 1), F32)
    per_chunk = pl.BlockSpec((SCAN_CHUNKS, HEADS, HEAD_DIM, HEAD_DIM), lambda n: (rev(n), 0, 0, 0))
    return pl.pallas_call(
        body, name="gdn_scan_bwd", grid=(n_steps,),
        in_specs=[part(0), part(1), part(2), pl.BlockSpec((span, LANES), lambda n: (rev(n), ba_col)), par, par, vec,
                  per_chunk, per_chunk, part(0)],
        out_specs=(pl.BlockSpec((span, QKV_B), lambda n: (rev(n), 0)), pl.BlockSpec((span, LANES), lambda n: (rev(n), 0)),
                   par, par, vec),
        out_shape=(jax.ShapeDtypeStruct((s, QKV_B), F32), jax.ShapeDtypeStruct((s, LANES), F32), par_shape, par_shape,
                   jax.ShapeDtypeStruct((1, 1, HEAD_DIM), F32)),
        scratch_shapes=[pltpu.VMEM((HEADS, HEAD_DIM, HEAD_DIM), F32)],
        compiler_params=_params("arbitrary"),
    )(cqkv, cqkv, cqkv, proj_r, alog, dtb, gnw, states, t_invs, do)


def _tail_loss(x, tgt, o0, o1, o2, l0, l1, l2, ga, gb, za, zb, ob, fnw, wua, wub, wo, tap_a, tap_b, tap_o):
    lm = jnp.maximum(jnp.maximum(l0, l1), l2)
    e0, e1, e2 = jnp.exp(l0 - lm), jnp.exp(l1 - lm), jnp.exp(l2 - lm)
    o_a = (e0 * o0 + e1 * o1 + e2 * o2) / (e0 + e1 + e2)
    y_a = _mm_tap(o_a * _silu(za), wua, tap_a)
    y_b = _mm_tap(ob * _silu(zb), wub, tap_b)
    merged = _sigmoid(ga) * y_a + _sigmoid(gb) * y_b
    y = _rmsnorm(x + _mm_tap(merged, wo, tap_o), fnw)
    err = y - tgt
    per_token = jnp.sum(err * err, axis=1, keepdims=True) * (0.5 / x.shape[1])
    return jnp.sum(per_token, axis=0, keepdims=True)


def _tail(x, tgt, o_all, lse_all, og12, lg12, proj_r, ob, wua, wub, wo, fnw):
    s, d = x.shape
    tm = _row_block(s, 128)
    col_za = 2 * d // WIDTH
    col_zb = (2 * d + WIDTH + QKV_B) // WIDTH

    def body(x_ref, t_ref, o0_ref, o1_ref, o2_ref, l0_ref, l1_ref, l2_ref, ga_ref, gb_ref, za_ref, zb_ref, ob_ref,
             wua_ref, wub_ref, wo_ref, fnw_ref,
             loss_ref, dx_ref, do0_ref, do1_ref, do2_ref, dl0_ref, dl1_ref, dl2_ref, dga_ref, dgb_ref, dza_ref,
             dzb_ref, dob_ref, dwua_ref, dwub_ref, dwo_ref, dfnw_ref):
        @pl.when(pl.program_id(0) == 0)
        def _():
            for r in (loss_ref, dwua_ref, dwub_ref, dwo_ref, dfnw_ref):
                r[...] = jnp.zeros_like(r)

        args = (x_ref[...], t_ref[...], o0_ref[...], o1_ref[...], o2_ref[...], l0_ref[...], l1_ref[...], l2_ref[...],
                ga_ref[...], gb_ref[...], za_ref[...], zb_ref[...], ob_ref[...], fnw_ref[...],
                wua_ref[...], wub_ref[...], wo_ref[...],
                jnp.zeros(wua_ref.shape, F32), jnp.zeros(wub_ref.shape, F32), jnp.zeros(wo_ref.shape, F32))
        loss, vjp = jax.vjp(_tail_loss, *args)
        (dx, _, do0, do1, do2, dl0, dl1, dl2, dga, dgb, dza, dzb, dob, dfnw, _, _, _, dwua, dwub, dwo) = vjp(
            jnp.ones((1, 1), F32))
        loss_ref[...] += jnp.broadcast_to(loss, loss_ref.shape)
        dx_ref[...] = dx
        do0_ref[...], do1_ref[...], do2_ref[...] = do0, do1, do2
        dl0_ref[...], dl1_ref[...], dl2_ref[...] = dl0, dl1, dl2
        dga_ref[...] = dga.astype(BF16)
        dgb_ref[...] = dgb.astype(BF16)
        dza_ref[...] = dza.astype(BF16)
        dzb_ref[...] = dzb.astype(BF16)
        dob_ref[...] = dob
        dwua_ref[...] += dwua
        dwub_ref[...] += dwub
        dwo_ref[...] += dwo
        dfnw_ref[...] += dfnw

    row = lambda w, c=0: pl.BlockSpec((tm, w), lambda i: (i, c))
    grp0 = pl.BlockSpec((None, tm, WIDTH), lambda i: (0, i, 0))
    full = lambda a, b: pl.BlockSpec((a, b), lambda i: (0, 0))
    f32 = lambda a, b: jax.ShapeDtypeStruct((a, b), F32)
    b16 = lambda a, b: jax.ShapeDtypeStruct((a, b), BF16)
    stacked = jax.ShapeDtypeStruct((GROUPS, s, WIDTH), F32)
    gspecs = [grp0, row(WIDTH), row(WIDTH)]
    in_specs = ([row(d), row(d)] + gspecs * 2 + [row(d, 0), row(d, 1), row(WIDTH, col_za), row(WIDTH, col_zb),
                row(WIDTH), full(WIDTH, d), full(WIDTH, d), full(d, d), full(1, d)])
    out_specs = ([full(SUBLANES, LANES), row(d)] + gspecs * 2 + [row(d), row(d), row(WIDTH), row(WIDTH), row(WIDTH),
                 full(WIDTH, d), full(WIDTH, d), full(d, d), full(1, d)])
    gshapes = [stacked, f32(s, WIDTH), f32(s, WIDTH)]
    out_shape = ([f32(SUBLANES, LANES), f32(s, d)] + gshapes * 2 + [b16(s, d), b16(s, d), b16(s, WIDTH),
                 b16(s, WIDTH), f32(s, WIDTH), f32(WIDTH, d), f32(WIDTH, d), f32(d, d), f32(1, d)])
    return pl.pallas_call(
        body, name="tail_fwd_bwd", grid=(s // tm,),
        in_specs=in_specs, out_specs=tuple(out_specs), out_shape=tuple(out_shape),
        compiler_params=_params("arbitrary"),
    )(x, tgt, o_all, og12[0], og12[1], lse_all, lg12[0], lg12[1], proj_r, proj_r, proj_r, proj_r, ob, wua, wub, wo, fnw)


PERMUTE_SPAN = 4096


def _permute_span(s):
    return PERMUTE_SPAN if s % PERMUTE_SPAN == 0 else s


def _from_dilated_rows(stacked, g, dil, name):
    n_slots, s, c = stacked.shape
    view = stacked.reshape(n_slots, dil, s // dil, c)
    span = _permute_span(s)

    def body(in_ref, out_ref):
        for r in range(dil):
            out_ref[pl.ds(r, span // dil, stride=dil), :] = in_ref[r]

    return pl.pallas_call(
        body, name=name, grid=(s // span, c // LANES),
        in_specs=[pl.BlockSpec((None, dil, span // dil, LANES), lambda n, j: (g, 0, n, j))],
        out_specs=pl.BlockSpec((span, LANES), lambda n, j: (n, j)),
        out_shape=jax.ShapeDtypeStruct((s, c), stacked.dtype),
        compiler_params=_params("parallel", "parallel"),
    )(view)


def _to_dilated_rows_into(nat, stacked, g, dil, name):
    n_slots, s, c = stacked.shape
    view = stacked.reshape(n_slots, dil, s // dil, c)
    span = _permute_span(s)

    def body(nat_ref, old_ref, out_ref):
        for r in range(dil):
            out_ref[r] = nat_ref[pl.ds(r, span // dil, stride=dil), :]

    out = pl.pallas_call(
        body, name=name, grid=(s // span, c // LANES),
        in_specs=[pl.BlockSpec((span, LANES), lambda n, j: (n, j)), pl.BlockSpec(memory_space=pl.ANY)],
        out_specs=pl.BlockSpec((None, dil, span // dil, LANES), lambda n, j: (g, 0, n, j)),
        out_shape=jax.ShapeDtypeStruct(view.shape, stacked.dtype),
        input_output_aliases={1: 0},
        compiler_params=_params("parallel", "parallel"),
    )(nat, view)
    return out.reshape(stacked.shape)


def _to_dilated(a, dil):
    if dil == 1:
        return a
    s = a.shape[0]
    return a.reshape(s // dil, dil, -1).transpose(1, 0, 2).reshape(a.shape)


def _from_dilated(a, dil):
    if dil == 1:
        return a
    s = a.shape[0]
    return a.reshape(dil, s // dil, -1).transpose(1, 0, 2).reshape(a.shape)


def _head_major(a):
    return a.reshape(a.shape[0], HEADS, HEAD_DIM).transpose(1, 0, 2)


def _from_head_major(a):
    return a.transpose(1, 0, 2).reshape(a.shape[1], WIDTH)


def _rope_tables(s):
    inv_freq = ROPE_THETA ** (-jnp.arange(0, HEAD_DIM, 2, dtype=F32) / HEAD_DIM)
    ang = jnp.arange(s, dtype=F32)[:, None] * inv_freq[None, :]
    cos_n = jnp.tile(jnp.cos(ang), (1, 2 * LANES // HEAD_DIM))
    sin_h = jnp.sin(ang)
    sin_n = jnp.tile(jnp.concatenate([-sin_h, sin_h], axis=1), (1, LANES // HEAD_DIM))
    def per_group(table, tag):
        out = jnp.broadcast_to(table, (GROUPS,) + table.shape)
        for g in range(1, GROUPS):
            out = _to_dilated_rows_into(table, out, g, DILATIONS[g], "rope_%s_to_dilated_%d" % (tag, g))
        return out

    return per_group(cos_n, "cos"), per_group(sin_n, "sin")


def _regroup_columns(pieces, widths):
    starts, pos = [], 0
    for p in pieces:
        starts.append(pos)
        pos += p.shape[1]
    assert pos == sum(widths), (pos, widths)
    out, lo = [], 0
    for w in widths:
        hi, parts = lo + w, []
        for p, st in zip(pieces, starts):
            a, b = max(lo, st), min(hi, st + p.shape[1])
            if a < b:
                parts.append(p[:, a - st:b - st])
        out.append(parts[0] if len(parts) == 1 else jnp.concatenate(parts, axis=1))
        lo = hi
    return out


def _pack_rows(parts, dtype, row_multiple):
    flat = jnp.concatenate([p.reshape(-1).astype(dtype) for p in parts])
    tile = row_multiple * LANES
    pad = (-flat.shape[0]) % tile
    return jnp.pad(flat, (0, pad)).reshape(-1, LANES)


def _unpack_rows(packed, shapes):
    flat = packed.reshape(-1)
    out, start = [], 0
    for shp in shapes:
        size = 1
        for n in shp:
            size *= n
        out.append(flat[start:start + size].reshape(shp))
        start += size
    return out


def kernel(x, norm_w, w_in, conv_w, a_log, dt_bias, gdn_norm_w, w_up_a, w_up_b, w_out, final_norm_w, loss_target, m_norm_w, m_w_in, m_conv_w, m_a_log, m_dt_bias, m_gdn_norm_w, m_w_up_a, m_w_up_b, m_w_out, m_final_norm_w, v_norm_w, v_w_in, v_conv_w, v_a_log, v_dt_bias, v_gdn_norm_w, v_w_up_a, v_w_up_b, v_w_out, v_final_norm_w):
    x2, tgt = x[0], loss_target[0]
    s, d = x2.shape
    me = 4 * lax.axis_index("x") + 2 * lax.axis_index("y") + lax.axis_index("c")
    win8 = w_in.shape[2]
    conv8w = conv_w.shape[2]

    conv_shard = jnp.pad(conv_w[0], ((0, SUBLANES - CONV_K), (0, 0)))
    w_in_g, wua_g, wub_g, wo_g, conv_g = _all_gather(
        [w_in[0].astype(BF16), w_up_a[0].astype(BF16), w_up_b[0].astype(BF16), w_out[0].astype(BF16), conv_shard])
    wua = jnp.concatenate([wua_g[i] for i in range(N_DEV)], axis=1)
    wub = jnp.concatenate([wub_g[i] for i in range(N_DEV)], axis=1)
    wo = wo_g.reshape(d, d)
    conv8 = jnp.concatenate([conv_g[i] for i in range(N_DEV)], axis=1)

    seg_widths = [QKV_B] * GROUPS + [WIDTH, QKV_B, WIDTH, 2 * HEADS, 2 * d]
    wq0, wq1, wq2, w_za, w_qkvb, w_zb, w_ba, w_gates = _regroup_columns([w_in_g[i] for i in range(N_DEV)], seg_widths)
    w_qkv = jnp.stack([wq0, wq1, wq2])
    w_rest = jnp.concatenate([w_gates, w_za, w_qkvb, w_zb, w_ba,
                              jnp.zeros((d, BA_PAD - 2 * HEADS), BF16)], axis=1)
    col_qkvb = (2 * d + WIDTH) // LANES
    col_ba = (2 * d + 2 * WIDTH + QKV_B) // LANES

    h = _rms_fwd(x2, norm_w)
    h_all = jnp.stack([_to_dilated(h, dil) for dil in DILATIONS])
    qkv_all = _matmul(h_all, w_qkv, F32, "in_proj_attention", tn=QKV_B)
    proj_r = _matmul(h[None], w_rest[None], F32, "in_proj_rest", tn=2560)[0]
    cos, sin = _rope_tables(s)
    o_all, lse_all = _attn_fwd(qkv_all, cos, sin)
    og12 = [_from_dilated_rows(o_all, g, DILATIONS[g], "attn_out_to_natural_%d" % g) for g in (1, 2)]
    lg12 = [_from_dilated_rows(lse_all, g, DILATIONS[g], "attn_lse_to_natural_%d" % g) for g in (1, 2)]

    cqkv = _gdn_pre_fwd(proj_r, conv8, col_qkvb)
    alog3, dtb3, gnw3 = a_log.reshape(HEADS, 1, 1), dt_bias.reshape(HEADS, 1, 1), gdn_norm_w.reshape(1, 1, HEAD_DIM)
    ob, states, t_invs = _gdn_scan_fwd(cqkv, proj_r, col_ba, alog3, dtb3, gnw3)

    (loss_blk, dx_res, do_all, do1, do2, dl_all, dl1, dl2, dga, dgb, dza, dzb, dob, dwua, dwub, dwo, dfnw) = _tail(
        x2, tgt, o_all, lse_all, og12, lg12, proj_r, ob, wua, wub, wo, final_norm_w.reshape(1, d))

    for g, (t_o, t_l) in ((1, (do1, dl1)), (2, (do2, dl2))):
        do_all = _to_dilated_rows_into(t_o, do_all, g, DILATIONS[g], "attn_dout_to_dilated_%d" % g)
        dl_all = _to_dilated_rows_into(t_l, dl_all, g, DILATIONS[g], "attn_dlse_to_dilated_%d" % g)
    dqkv_all = _attn_bwd(qkv_all, cos, sin, o_all, lse_all, do_all, dl_all)

    dcqkv, dba, dalog3, ddtb3, dgnw3 = _gdn_scan_bwd(cqkv, proj_r, col_ba, alog3, dtb3, gnw3, states, t_invs, dob)
    dqkv_b, dconv8 = _gdn_pre_bwd(proj_r, conv8, dcqkv, col_qkvb)
    dproj_r = jnp.concatenate([dga, dgb, dza, dqkv_b, dzb,
                               jnp.pad(dba.astype(BF16), ((0, 0), (0, BA_PAD - LANES)))], axis=1)

    dw_qkv = _matmul(h_all, dqkv_all, F32, "in_proj_attention_dw", mode="tn", tk=2048)
    dw_rest = _matmul(h[None], dproj_r[None], F32, "in_proj_rest_dw", mode="tn", tk=2048)[0]
    o2 = 2 * d
    dw_in_pieces = [dw_qkv[0], dw_qkv[1], dw_qkv[2],
                    dw_rest[:, o2:o2 + WIDTH], dw_rest[:, o2 + WIDTH:o2 + WIDTH + QKV_B],
                    dw_rest[:, o2 + WIDTH + QKV_B:o2 + 2 * WIDTH + QKV_B],
                    dw_rest[:, o2 + 2 * WIDTH + QKV_B:o2 + 2 * WIDTH + QKV_B + 2 * HEADS],
                    dw_rest[:, :o2]]

    def col_slabs(a, width):
        return jnp.stack([a[:, j * width:(j + 1) * width] for j in range(N_DEV)])

    slabs = [jnp.stack(_regroup_columns(dw_in_pieces, [win8] * N_DEV)), col_slabs(dwua, d // N_DEV),
             col_slabs(dwub, d // N_DEV), dwo.reshape(N_DEV, d // N_DEV, d)]
    dh_a, *from_sibling = _matmul(dqkv_all, w_qkv, F32, "in_proj_attention_dh", mode="nt", tk=2048,
                                  exchange=_sibling_exchange(slabs))
    core = lax.axis_index("c").astype(jnp.int32).reshape(1)
    partials = [_pair_sum(a, b, core, "grads_pair_sum_%d" % i) for i, (a, b) in enumerate(zip(slabs, from_sibling))]
    dh_r, *contrib = _matmul(dproj_r[None], w_rest[None], F32, "in_proj_rest_dh", mode="nt", tk=2560,
                             exchange=_chip_exchange(partials))
    dh_parts = [dh_r[0]] + [_from_dilated_rows(dh_a, g, DILATIONS[g], "dh_to_natural_%d" % g) for g in (1, 2)]
    grad_x, dnorm_w = _rms_bwd(x2, norm_w, dh_a, dh_parts, dx_res)

    small_parts = [dnorm_w, dfnw, dconv8[:CONV_K], dalog3[:, 0, 0], ddtb3[:, 0, 0], dgnw3[0], loss_blk[0, 0:1]]
    small_rows = [-(-p.size // LANES) for p in small_parts]
    small = jnp.concatenate([jnp.pad(p.reshape(-1), (0, r * LANES - p.size)).reshape(r, LANES)
                             for p, r in zip(small_parts, small_rows)])
    small = jnp.pad(small, ((0, (-small.shape[0]) % SUBLANES), (0, 0)))
    small_sum = _small_all_reduce(small)
    pieces, r0 = [], 0
    for p, r in zip(small_parts, small_rows):
        pieces.append(small_sum[r0:r0 + r].reshape(-1)[:p.size].reshape(p.shape))
        r0 += r
    g_norm_w, g_fnw, g_conv_full, g_alog, g_dtb, g_gnw, loss_sum = pieces
    g_conv = lax.dynamic_slice(g_conv_full, (0, me * conv8w), (CONV_K, conv8w))

    big = [_adamw(c, w[0], m[0], v[0], name) for c, w, m, v, name in (
        (contrib[0], w_in, m_w_in, v_w_in, "adamw_w_in"), (contrib[1], w_up_a, m_w_up_a, v_w_up_a, "adamw_w_up_a"),
        (contrib[2], w_up_b, m_w_up_b, v_w_up_b, "adamw_w_up_b"), (contrib[3], w_out, m_w_out, v_w_out, "adamw_w_out"))]
    g_big, d_big, nm_big, nv_big = ([t[i] for t in big] for i in range(4))

    small_ws = [norm_w, final_norm_w, conv_w, a_log, dt_bias, gdn_norm_w]
    small_ms = [m_norm_w, m_final_norm_w, m_conv_w, m_a_log, m_dt_bias, m_gdn_norm_w]
    small_vs = [v_norm_w, v_final_norm_w, v_conv_w, v_a_log, v_dt_bias, v_gdn_norm_w]
    small_gs = [g_norm_w, g_fnw, g_conv, g_alog, g_dtb, g_gnw]
    small_shapes = [t.shape for t in small_ws]
    sm = _adamw(_pack_rows(small_gs, F32, SUBLANES)[None], _pack_rows(small_ws, F32, SUBLANES),
                _pack_rows(small_ms, F32, SUBLANES), _pack_rows(small_vs, F32, SUBLANES), "adamw_small")
    g_sm, d_sm, nm_sm, nv_sm = (_unpack_rows(t, small_shapes) for t in sm)

    def ordered(bigs, smalls):
        nw, fnw_, cw, al, dtb, gn = smalls
        wi, ua, ub, wo_ = (t[None] for t in bigs)
        return [nw, wi, cw, al, dtb, gn, ua, ub, wo_, fnw_]

    return (loss_sum.reshape(()), grad_x[None], *ordered(g_big, g_sm), *ordered(d_big, d_sm),
            *ordered(nm_big, nm_sm), *ordered(nv_big, nv_sm))
```

```python
import functools

import jax
import jax.numpy as jnp
from jax import lax
from jax.experimental import pallas as pl
from jax.experimental.pallas import tpu as pltpu

F32 = jnp.float32
BF16 = jnp.bfloat16
MESH = pl.DeviceIdType.MESH
N_DEV = 8
LANES = 128
SUBLANES = 8

GROUPS = 3
HEADS = 8
HEAD_DIM = 64
WIDTH = HEADS * HEAD_DIM
ATT_BLOCK = 128
DILATIONS = (1, 4, 16)
N_BACK = 128
CONV_K = 4
CHUNK = 64
SCAN_CHUNKS = 4
QKV_B = 3 * WIDTH
QKV_A = GROUPS * 3 * WIDTH
BA_PAD = 512
NORM_EPS = 1e-6
ROPE_THETA = 10000.0
ADAM_LR, ADAM_B1, ADAM_B2, ADAM_EPS, ADAM_WD, ADAM_STEP = 0.001, 0.9, 0.999, 1e-08, 0.01, 10

VMEM_LIMIT = 56 * 1024 * 1024

OFF_ZA = QKV_A
OFF_QKVB = OFF_ZA + WIDTH
OFF_ZB = OFF_QKVB + QKV_B
OFF_BA = OFF_ZB + WIDTH
OFF_GATE = OFF_BA + 2 * HEADS


def _params(*sem):
    return pltpu.CompilerParams(dimension_semantics=sem, vmem_limit_bytes=VMEM_LIMIT)


def _dg(a, b, ca, cb):
    nb = a.ndim - 2
    batch = tuple(range(nb))
    return lax.dot_general(a, b, (((nb + ca,), (nb + cb,)), (batch, batch)), preferred_element_type=F32)


@jax.custom_vjp
def _mm(a, b):
    return _dg(a.astype(BF16), b.astype(BF16), 1, 0)


def _mm_fwd(a, b):
    return _mm(a, b), (a.astype(BF16), b.astype(BF16))


def _mm_bwd(res, ct):
    a16, b16 = res
    c16 = ct.astype(BF16)
    return _dg(c16, b16, 1, 1), _dg(a16, c16, 0, 0)


_mm.defvjp(_mm_fwd, _mm_bwd)


@jax.custom_vjp
def _mm_nt(a, b):
    return _dg(a.astype(BF16), b.astype(BF16), 1, 1)


def _mm_nt_fwd(a, b):
    return _mm_nt(a, b), (a.astype(BF16), b.astype(BF16))


def _mm_nt_bwd(res, ct):
    a16, b16 = res
    c16 = ct.astype(BF16)
    return _dg(c16, b16, 1, 0), _dg(c16, a16, 0, 0)


_mm_nt.defvjp(_mm_nt_fwd, _mm_nt_bwd)


@jax.custom_vjp
def _mm_tn(a, b):
    return _dg(a.astype(BF16), b.astype(BF16), 0, 0)


def _mm_tn_fwd(a, b):
    return _mm_tn(a, b), (a.astype(BF16), b.astype(BF16))


def _mm_tn_bwd(res, ct):
    a16, b16 = res
    c16 = ct.astype(BF16)
    return _dg(b16, c16, 1, 1), _dg(a16, c16, 1, 0)


_mm_tn.defvjp(_mm_tn_fwd, _mm_tn_bwd)


@jax.custom_vjp
def _mm_tap(a, w16, tap):
    return _dg(a.astype(BF16), w16, 1, 0)


def _mm_tap_fwd(a, w16, tap):
    return _mm_tap(a, w16, tap), (a.astype(BF16), w16)


def _mm_tap_bwd(res, ct):
    a16, w16 = res
    c16 = ct.astype(BF16)
    return _dg(c16, w16, 1, 1), jnp.zeros_like(w16), _dg(a16, c16, 0, 0)


_mm_tap.defvjp(_mm_tap_fwd, _mm_tap_bwd)


def _split16(a):
    hi = a.astype(BF16)
    lo = (a - hi.astype(F32)).astype(BF16)
    return hi, lo


def _dot3(a, b, ca, cb):
    ah, al = _split16(a)
    bh, bl = _split16(b)
    return _dg(ah, bh, ca, cb) + (_dg(ah, bl, ca, cb) + _dg(al, bh, ca, cb))


def _tri_inv_impl(a):
    n = a.shape[-1]
    shp = (1,) * (a.ndim - 2) + (n, n)
    eye = (lax.broadcasted_iota(jnp.int32, shp, a.ndim - 2) == lax.broadcasted_iota(jnp.int32, shp, a.ndim - 1)).astype(F32)
    x = eye - a
    p = a
    for it in range(5):
        dot = _dot3 if it < 2 else (lambda u, v, cu, cv: _dg(u.astype(BF16), v.astype(BF16), cu, cv))
        p = dot(p, p, 1, 0)
        x = x + dot(x, p, 1, 0)
    return x


@jax.custom_vjp
def _tri_inv(a):
    return _tri_inv_impl(a)


def _tri_inv_fwd(a):
    t = _tri_inv_impl(a)
    return t, t


def _tri_inv_bwd(t, ct):
    t16 = t.astype(BF16)
    return (-_dg(_dg(t16, ct.astype(BF16), 0, 0).astype(BF16), t16, 1, 1),)


_tri_inv.defvjp(_tri_inv_fwd, _tri_inv_bwd)


@jax.custom_vjp
def _tri_inv_saved(a, t):
    return t


def _tri_inv_saved_fwd(a, t):
    return t, t


def _tri_inv_saved_bwd(t, ct):
    return _tri_inv_bwd(t, ct) + (jnp.zeros_like(t),)


_tri_inv_saved.defvjp(_tri_inv_saved_fwd, _tri_inv_saved_bwd)


def _sigmoid(x):
    return 1.0 / (1.0 + jnp.exp(-x))


def _silu(x):
    return x * _sigmoid(x)


def _softplus(x):
    return jnp.maximum(x, 0.0) + jnp.log(1.0 + jnp.exp(-jnp.abs(x)))


def _rmsnorm(x, w):
    return x * lax.rsqrt(jnp.mean(x * x, axis=-1, keepdims=True) + NORM_EPS) * w


def _row_block(rows, cap):
    best = None
    for cand in range(SUBLANES, min(rows, cap) + 1, SUBLANES):
        if rows % cand == 0:
            best = cand
    assert best is not None, rows
    return best


def _mesh_peers():
    x, y, c = lax.axis_index("x"), lax.axis_index("y"), lax.axis_index("c")
    me = 4 * x + 2 * y + c
    peers = []
    for k in range(1, N_DEV):
        px = 1 - x if (k >> 2) & 1 else x
        py = 1 - y if (k >> 1) & 1 else y
        pc = 1 - c if k & 1 else c
        peers.append(((px, py, pc), 4 * px + 2 * py + pc))
    return me, peers


N_CHIPS = 4
OTHER_CHIPS = 3


def _chip_peers():
    x, y, c = lax.axis_index("x"), lax.axis_index("y"), lax.axis_index("c")
    return x, y, c, [(1 - x, y), (x, 1 - y), (1 - x, 1 - y)]


def _all_gather(shards):
    n_arr = len(shards)
    per = 1 + 2 * OTHER_CHIPS

    def body(*refs):
        in_refs, out_refs = refs[:n_arr], refs[n_arr:2 * n_arr]
        send_sems, recv_sems, loc_sems = refs[2 * n_arr:]
        x, y, c, chips = _chip_peers()
        me, sibling = (x, y, c), (x, y, 1 - c)

        def slot(px, py, pc):
            return 4 * px + 2 * py + pc

        def copy(i, k, block, to, src=None):
            dst = out_refs[i].at[slot(*block)]
            return pltpu.make_async_remote_copy(src_ref=dst if src is None else src, dst_ref=dst,
                                                send_sem=send_sems.at[i * per + k], recv_sem=recv_sems.at[i * per + k],
                                                device_id=to, device_id_type=MESH)

        own = [pltpu.make_async_copy(in_refs[i], out_refs[i].at[slot(*me)], loc_sems.at[i]) for i in range(n_arr)]
        for cp in own:
            cp.start()
        first = []
        for i in range(n_arr):
            first += [copy(i, 1 + j, me, (*chip, c), src=in_refs[i]) for j, chip in enumerate(chips)]
            first.append(copy(i, 0, me, sibling, src=in_refs[i]))
        for cp in first:
            cp.start()
        passed = []
        for j, chip in enumerate(chips):
            for i in range(n_arr):
                copy(i, 1 + j, (*chip, c), me).wait_recv()
                fwd = copy(i, 1 + OTHER_CHIPS + j, (*chip, c), sibling)
                fwd.start()
                passed.append(fwd)
        for i in range(n_arr):
            copy(i, 0, sibling, me).wait_recv()
            for j, chip in enumerate(chips):
                copy(i, 1 + OTHER_CHIPS + j, (*chip, 1 - c), me).wait_recv()
        for cp in first + passed:
            cp.wait_send()
        for cp in own:
            cp.wait()

    any_spec = pl.BlockSpec(memory_space=pl.ANY)
    return pl.pallas_call(
        body, name="weights_all_gather",
        out_shape=tuple(jax.ShapeDtypeStruct((N_DEV,) + a.shape, a.dtype) for a in shards),
        in_specs=[any_spec] * n_arr, out_specs=tuple([any_spec] * n_arr),
        scratch_shapes=[pltpu.SemaphoreType.DMA((n_arr * per,)), pltpu.SemaphoreType.DMA((n_arr * per,)),
                        pltpu.SemaphoreType.DMA((n_arr,))],
    )(*shards)


class _Exchange:
    def __init__(self, arrays, out_shapes, n_sem, copies):
        self.arrays, self.out_shapes, self.n_sem, self.copies = list(arrays), list(out_shapes), n_sem, copies


def _sibling_exchange(slabs):
    n_arr = len(slabs)

    def copies(in_refs, out_refs, send_sems, recv_sems, loc_sems):
        x, y, c, _ = _chip_peers()
        sends = [pltpu.make_async_remote_copy(src_ref=in_refs[i].at[2 * q + (1 - c)], dst_ref=out_refs[i].at[q],
                                              send_sem=send_sems.at[i * N_CHIPS + q], recv_sem=recv_sems.at[i * N_CHIPS + q],
                                              device_id=(x, y, 1 - c), device_id_type=MESH)
                 for i in range(n_arr) for q in range(N_CHIPS)]

        def start():
            for cp in sends:
                cp.start()

        def finish():
            for cp in sends:
                cp.wait_recv()
            for cp in sends:
                cp.wait_send()

        return start, finish

    return _Exchange(slabs, [jax.ShapeDtypeStruct((N_CHIPS,) + a.shape[1:], a.dtype) for a in slabs],
                     n_arr * N_CHIPS, copies)


def _pair_sum(slabs, from_sibling, core, name):
    _, rows, cols = slabs.shape
    tr = _row_block(rows, max(SUBLANES, (256 * 1024) // cols // SUBLANES * SUBLANES))

    def body(core_ref, a_ref, b_ref, o_ref):
        o_ref[...] = (a_ref[...] + b_ref[...]).astype(BF16)

    grid_spec = pltpu.PrefetchScalarGridSpec(
        num_scalar_prefetch=1, grid=(N_CHIPS, rows // tr),
        in_specs=[pl.BlockSpec((None, tr, cols), lambda q, r, core_ref: (2 * q + core_ref[0], r, 0)),
                  pl.BlockSpec((None, tr, cols), lambda q, r, core_ref: (q, r, 0))],
        out_specs=pl.BlockSpec((None, tr, cols), lambda q, r, core_ref: (q, r, 0)))
    return pl.pallas_call(
        body, name=name, grid_spec=grid_spec,
        out_shape=jax.ShapeDtypeStruct((N_CHIPS, rows, cols), BF16),
        compiler_params=_params("parallel", "parallel"),
    )(core, slabs, from_sibling)


def _chip_exchange(partials):
    n_arr = len(partials)

    def copies(in_refs, out_refs, send_sems, recv_sems, loc_sems):
        x, y, c, chips = _chip_peers()
        mine = 2 * x + y
        own = [pltpu.make_async_copy(in_refs[i].at[mine], out_refs[i].at[mine], loc_sems.at[i]) for i in range(n_arr)]

        def copy(i, j, chip, src_slot, dst_slot):
            return pltpu.make_async_remote_copy(src_ref=in_refs[i].at[src_slot], dst_ref=out_refs[i].at[dst_slot],
                                                send_sem=send_sems.at[i * OTHER_CHIPS + j],
                                                recv_sem=recv_sems.at[i * OTHER_CHIPS + j],
                                                device_id=(*chip, c), device_id_type=MESH)

        sends = [copy(i, j, chip, 2 * chip[0] + chip[1], mine) for j, chip in enumerate(chips) for i in range(n_arr)]
        recvs = [copy(i, j, chip, mine, 2 * chip[0] + chip[1]) for j, chip in enumerate(chips) for i in range(n_arr)]

        def start():
            for cp in own + sends:
                cp.start()

        def finish():
            for cp in recvs:
                cp.wait_recv()
            for cp in sends:
                cp.wait_send()
            for cp in own:
                cp.wait()

        return start, finish

    return _Exchange(partials, [jax.ShapeDtypeStruct(a.shape, a.dtype) for a in partials], n_arr * OTHER_CHIPS, copies)


def _small_all_reduce(part):
    rows = part.shape[0]

    def body(p_ref, o_ref, buf_ref, send_sems, recv_sems):
        me, peers = _mesh_peers()
        buf_ref[me] = p_ref[...]
        sends = []
        for k, (dev, pid) in enumerate(peers):
            cp = pltpu.make_async_remote_copy(src_ref=p_ref, dst_ref=buf_ref.at[me], send_sem=send_sems.at[k],
                                              recv_sem=recv_sems.at[k], device_id=dev, device_id_type=MESH)
            cp.start()
            sends.append(cp)
        for k, (dev, pid) in enumerate(peers):
            pltpu.make_async_remote_copy(src_ref=p_ref, dst_ref=buf_ref.at[pid], send_sem=send_sems.at[k],
                                         recv_sem=recv_sems.at[k], device_id=dev, device_id_type=MESH).wait_recv()
        for cp in sends:
            cp.wait_send()
        acc = buf_ref[0]
        for i in range(1, N_DEV):
            acc = acc + buf_ref[i]
        o_ref[...] = acc

    vmem = pl.BlockSpec(memory_space=pltpu.VMEM)
    return pl.pallas_call(
        body, name="small_all_reduce",
        out_shape=jax.ShapeDtypeStruct(part.shape, F32),
        in_specs=[vmem], out_specs=vmem,
        scratch_shapes=[pltpu.VMEM((N_DEV, rows, LANES), F32), pltpu.SemaphoreType.DMA((N_DEV - 1,)),
                        pltpu.SemaphoreType.DMA((N_DEV - 1,))],
    )(part)


def _adamw_vals(w, g, m, v):
    m = ADAM_B1 * m + (1.0 - ADAM_B1) * g
    v = ADAM_B2 * v + (1.0 - ADAM_B2) * (g * g)
    m_hat = m / (1.0 - ADAM_B1 ** ADAM_STEP)
    v_hat = v / (1.0 - ADAM_B2 ** ADAM_STEP)
    delta = -ADAM_LR * (m_hat / (jnp.sqrt(v_hat) + ADAM_EPS) + ADAM_WD * w)
    return delta, m, v


def _adamw(contrib, w, m, v, name):
    n, rows, cols = contrib.shape
    tr = _row_block(rows, max(SUBLANES, (128 * 1024) // cols // SUBLANES * SUBLANES))

    def body(c_ref, w_ref, m_ref, v_ref, g_ref, d_ref, nm_ref, nv_ref):
        g = c_ref[0].astype(F32)
        for i in range(1, n):
            g = g + c_ref[i].astype(F32)
        delta, nm, nv = _adamw_vals(w_ref[...], g, m_ref[...], v_ref[...])
        g_ref[...] = g
        d_ref[...] = delta
        nm_ref[...] = nm
        nv_ref[...] = nv

    row = pl.BlockSpec((tr, cols), lambda i: (i, 0))
    shp = jax.ShapeDtypeStruct((rows, cols), F32)
    return pl.pallas_call(
        body, name=name, grid=(rows // tr,),
        in_specs=[pl.BlockSpec((n, tr, cols), lambda i: (0, i, 0)), row, row, row],
        out_specs=(row, row, row, row), out_shape=(shp, shp, shp, shp),
        compiler_params=_params("parallel"),
    )(contrib, w, m, v)


def _lane_block(n, cap):
    if n <= cap:
        return n
    best = None
    for cand in range(LANES, cap + 1, LANES):
        if n % cand == 0:
            best = cand
    assert best is not None, n
    return best


def _matmul(a, b, out_dtype, name, mode="nn", tm=1024, tn=1024, tk=1024, exchange=None):
    g = a.shape[0]
    m, k = (a.shape[2], a.shape[1]) if mode == "tn" else (a.shape[1], a.shape[2])
    n = b.shape[1] if mode == "nt" else b.shape[2]
    tm, tn, tk = _lane_block(m, tm), _lane_block(n, tn), _lane_block(k, tk)
    nk = k // tk
    grid = (g, m // tm, n // tn, nk)
    a_spec = (pl.BlockSpec((None, tk, tm), lambda gi, i, j, kk: (gi, kk, i)) if mode == "tn" else
              pl.BlockSpec((None, tm, tk), lambda gi, i, j, kk: (gi, i, kk)))
    b_spec = (pl.BlockSpec((None, tn, tk), lambda gi, i, j, kk: (gi, j, kk)) if mode == "nt" else
              pl.BlockSpec((None, tk, tn), lambda gi, i, j, kk: (gi, kk, j)))
    ca, cb = (0 if mode == "tn" else 1), (1 if mode == "nt" else 0)
    n_ex = 0 if exchange is None else len(exchange.arrays)

    def body(a_ref, b_ref, *rest):
        ex_in, o_ref, ex_out, scratch = rest[:n_ex], rest[n_ex], rest[n_ex + 1:2 * n_ex + 1], rest[2 * n_ex + 1:]
        if exchange is not None:
            start, finish = exchange.copies(ex_in, ex_out, *scratch[-3:])
            pids = [pl.program_id(ax) for ax in range(4)]
            pl.when((pids[0] == 0) & (pids[1] == 0) & (pids[2] == 0) & (pids[3] == 0))(start)
        part = _dg(a_ref[...], b_ref[...], ca, cb)
        if nk == 1:
            o_ref[...] = part.astype(o_ref.dtype)
        else:
            acc_ref = scratch[0]
            kk = pl.program_id(3)

            @pl.when(kk == 0)
            def _():
                acc_ref[...] = part

            @pl.when((kk > 0) & (kk < nk - 1))
            def _():
                acc_ref[...] += part

            @pl.when(kk == nk - 1)
            def _():
                o_ref[...] = (acc_ref[...] + part).astype(o_ref.dtype)
        if exchange is not None:
            pl.when((pids[0] == grid[0] - 1) & (pids[1] == grid[1] - 1) & (pids[2] == grid[2] - 1)
                    & (pids[3] == grid[3] - 1))(finish)

    any_spec = pl.BlockSpec(memory_space=pl.ANY)
    scratch_shapes = [] if nk == 1 else [pltpu.VMEM((tm, tn), F32)]
    out_shape = [jax.ShapeDtypeStruct((g, m, n), out_dtype)]
    if exchange is not None:
        scratch_shapes += [pltpu.SemaphoreType.DMA((exchange.n_sem,)), pltpu.SemaphoreType.DMA((exchange.n_sem,)),
                           pltpu.SemaphoreType.DMA((n_ex,))]
        out_shape += exchange.out_shapes
    outs = pl.pallas_call(
        body, name=name, grid=grid,
        in_specs=[a_spec, b_spec] + [any_spec] * n_ex,
        out_specs=tuple([pl.BlockSpec((None, tm, tn), lambda gi, i, j, kk: (gi, i, j))] + [any_spec] * n_ex),
        out_shape=tuple(out_shape),
        scratch_shapes=scratch_shapes,
        compiler_params=(_params("parallel", "parallel", "parallel", "arbitrary") if exchange is None else
                         _params("arbitrary", "arbitrary", "arbitrary", "arbitrary")),
    )(a, b, *([] if exchange is None else exchange.arrays))
    return outs[0] if exchange is None else outs


def _rms_fwd(x, w):
    s, d = x.shape
    tm = _row_block(s, 512)

    def body(x_ref, w_ref, h_ref):
        h_ref[...] = _rmsnorm(x_ref[...], w_ref[...]).astype(BF16)

    return pl.pallas_call(
        body, name="input_rmsnorm", grid=(s // tm,),
        in_specs=[pl.BlockSpec((tm, d), lambda i: (i, 0)), pl.BlockSpec((1, d), lambda i: (0, 0))],
        out_specs=pl.BlockSpec((tm, d), lambda i: (i, 0)),
        out_shape=jax.ShapeDtypeStruct((s, d), BF16),
        compiler_params=_params("parallel"),
    )(x, w)


def _rms_bwd(x, w, dh_stacked, dh_parts, dx_res):
    s, d = x.shape
    tm = _row_block(s, 256)
    n_parts = 1 + len(dh_parts)

    def body(x_ref, w_ref, *rest):
        part_refs = rest[:n_parts]
        res_ref, gx_ref, gw_ref = rest[n_parts:]
        dh = part_refs[0][...]
        for r in part_refs[1:]:
            dh = dh + r[...]
        _, vjp = jax.vjp(_rmsnorm, x_ref[...], w_ref[...])
        dx, dw = vjp(dh)
        gx_ref[...] = dx + res_ref[...]

        @pl.when(pl.program_id(0) == 0)
        def _():
            gw_ref[...] = jnp.zeros_like(gw_ref)

        gw_ref[...] += dw

    row = pl.BlockSpec((tm, d), lambda i: (i, 0))
    vec = pl.BlockSpec((1, d), lambda i: (0, 0))
    return pl.pallas_call(
        body, name="input_rmsnorm_bwd", grid=(s // tm,),
        in_specs=[row, vec, pl.BlockSpec((None, tm, d), lambda i: (0, i, 0))] + [row] * (n_parts - 1) + [row],
        out_specs=(row, vec),
        out_shape=(jax.ShapeDtypeStruct((s, d), F32), jax.ShapeDtypeStruct((1, d), F32)),
        compiler_params=_params("arbitrary"),
    )(x, w, dh_stacked, *dh_parts, dx_res)


def _lane_masks(rows):
    lane = lax.broadcasted_iota(jnp.int32, (rows, LANES), 1)
    return lane < HEAD_DIM, (lane & (HEAD_DIM - 1)) < HEAD_DIM // 2


def _swap_halves(t, lo_half):
    return jnp.where(lo_half, pltpu.roll(t, LANES - HEAD_DIM // 2, 1), pltpu.roll(t, HEAD_DIM // 2, 1))


def _rope(t, cos, sin_signed, lo_half):
    return t * cos + _swap_halves(t, lo_half) * sin_signed


def _rope_bwd(d, cos, sin_signed, lo_half):
    return d * cos - _swap_halves(d, lo_half) * sin_signed


def _window_mask(first):
    qi = lax.broadcasted_iota(jnp.int32, (2 * ATT_BLOCK, 2 * ATT_BLOCK), 0) & (ATT_BLOCK - 1)
    kj = lax.broadcasted_iota(jnp.int32, (2 * ATT_BLOCK, 2 * ATT_BLOCK), 1)
    dist = qi + ATT_BLOCK - kj
    return (dist >= 0) & (dist <= N_BACK) & ((kj >= ATT_BLOCK) | jnp.logical_not(first))


def _stack_heads(t, head0):
    zero = jnp.zeros_like(t)
    return jnp.concatenate([jnp.where(head0, t, zero), jnp.where(head0, zero, t)], axis=0)


def _unstack_heads(t2, head0):
    return jnp.where(head0, t2[:ATT_BLOCK], t2[ATT_BLOCK:])


def _blocks_per_subsequence(g, nb):
    return lax.shift_right_logical(jnp.int32(nb), 2 * g)


def _attn_fwd(qkv, cos, sin):
    _, s, _ = qkv.shape
    nb = s // ATT_BLOCK

    def body(qkv_ref, cos_ref, sin_ref, o_ref, lse_ref, kp_ref, vp_ref):
        g, t = pl.program_id(0), pl.program_id(1)
        first = (t & (_blocks_per_subsequence(g, nb) - 1)) == 0

        @pl.when(first)
        def _():
            kp_ref[...] = jnp.zeros_like(kp_ref)
            vp_ref[...] = jnp.zeros_like(vp_ref)

        cos_b, sin_b = cos_ref[...], sin_ref[...]
        head0, lo_half = _lane_masks(ATT_BLOCK)
        valid = _window_mask(first)
        for sl in range(WIDTH // LANES):
            cq = pl.ds(sl * LANES, LANES)
            ck = pl.ds(WIDTH + sl * LANES, LANES)
            cv = pl.ds(2 * WIDTH + sl * LANES, LANES)
            qr = (_rope(qkv_ref[:, cq], cos_b, sin_b, lo_half) * (HEAD_DIM ** -0.5)).astype(BF16)
            kr = _rope(qkv_ref[:, ck], cos_b, sin_b, lo_half).astype(BF16)
            v16 = qkv_ref[:, cv].astype(BF16)
            kcat = jnp.concatenate([kp_ref[:, cq], kr], axis=0)
            vcat = jnp.concatenate([vp_ref[:, cq], v16], axis=0)
            sc = jnp.where(valid, _dg(_stack_heads(qr, head0), kcat, 1, 1), -jnp.inf)
            mx = jnp.max(sc, axis=1, keepdims=True)
            p = jnp.exp(sc - mx)
            den = jnp.sum(p, axis=1, keepdims=True)
            o_ref[:, cq] = _unstack_heads(_dg((p * (1.0 / den)).astype(BF16), vcat, 1, 0), head0)
            lse2 = mx + jnp.log(den)
            lse_ref[:, cq] = jnp.where(head0, lse2[:ATT_BLOCK], lse2[ATT_BLOCK:])
            kp_ref[:, cq] = kr
            vp_ref[:, cq] = v16

    blk = lambda w: pl.BlockSpec((None, ATT_BLOCK, w), lambda g, t: (g, t, 0))
    shp = jax.ShapeDtypeStruct((GROUPS, s, WIDTH), F32)
    return pl.pallas_call(
        body, name="dilated_attention_fwd", grid=(GROUPS, nb),
        in_specs=[blk(3 * WIDTH), blk(LANES), blk(LANES)],
        out_specs=(blk(WIDTH), blk(WIDTH)), out_shape=(shp, shp),
        scratch_shapes=[pltpu.VMEM((ATT_BLOCK, WIDTH), BF16), pltpu.VMEM((ATT_BLOCK, WIDTH), BF16)],
        compiler_params=_params("arbitrary", "arbitrary"),
    )(qkv, cos, sin)


def _attn_bwd(qkv, cos, sin, o, lse, do, dlse):
    _, s, _ = qkv.shape
    nb = s // ATT_BLOCK

    def body(qkv_ref, cos_ref, sin_ref, cosp_ref, sinp_ref, o_ref, lse_ref, do_ref, dlse_ref,
             dqkv_ref, kp_ref, vp_ref, dka_ref, dva_ref, dqp_ref):
        g, t = pl.program_id(0), pl.program_id(1)
        first = (t & (_blocks_per_subsequence(g, nb) - 1)) == 0
        active = t < nb
        head0, lo_half = _lane_masks(ATT_BLOCK)
        cos_p, sin_p = cosp_ref[...], sinp_ref[...]

        @pl.when(t == 0)
        def _():
            dka_ref[...] = jnp.zeros_like(dka_ref)
            dva_ref[...] = jnp.zeros_like(dva_ref)
            dqp_ref[...] = jnp.zeros_like(dqp_ref)

        dqkv_ref[:, pl.ds(0, WIDTH)] = dqp_ref[...]

        @pl.when(active & first)
        def _():
            kp_ref[...] = jnp.zeros_like(kp_ref)
            vp_ref[...] = jnp.zeros_like(vp_ref)

        @pl.when(active)
        def _():
            cos_b, sin_b = cos_ref[...], sin_ref[...]
            valid = _window_mask(first)
            for sl in range(WIDTH // LANES):
                cq = pl.ds(sl * LANES, LANES)
                ck = pl.ds(WIDTH + sl * LANES, LANES)
                cv = pl.ds(2 * WIDTH + sl * LANES, LANES)
                qr = (_rope(qkv_ref[:, cq], cos_b, sin_b, lo_half) * (HEAD_DIM ** -0.5)).astype(BF16)
                kr = _rope(qkv_ref[:, ck], cos_b, sin_b, lo_half).astype(BF16)
                v16 = qkv_ref[:, cv].astype(BF16)
                kcat = jnp.concatenate([kp_ref[:, cq], kr], axis=0)
                vcat = jnp.concatenate([vp_ref[:, cq], v16], axis=0)
                do_b = do_ref[:, cq]
                do16 = do_b.astype(BF16)
                lse_b = lse_ref[:, cq]
                cterm = dlse_ref[:, cq] - do_b * o_ref[:, cq]
                dqs, dkc, dvc = [], None, None
                for hm in (head0, jnp.logical_not(head0)):
                    qm = jnp.where(hm, qr, jnp.zeros_like(qr))
                    dom = jnp.where(hm, do16, jnp.zeros_like(do16))
                    sc = jnp.where(valid[:ATT_BLOCK], _dg(qm, kcat, 1, 1), -jnp.inf)
                    lse_h = jnp.max(jnp.where(hm, lse_b, -jnp.inf), axis=1, keepdims=True)
                    c = jnp.sum(jnp.where(hm, cterm, 0.0), axis=1, keepdims=True)
                    p = jnp.exp(sc - lse_h)
                    ds16 = (p * (_dg(dom, vcat, 1, 1) + c)).astype(BF16)
                    dv_h, dk_h = _dg(p.astype(BF16), dom, 0, 0), _dg(ds16, qm, 0, 0)
                    dvc = dv_h if dvc is None else dvc + dv_h
                    dkc = dk_h if dkc is None else dkc + dk_h
                    dqs.append(_dg(ds16, kcat, 1, 0))
                dq = jnp.where(head0, dqs[0], dqs[1]) * (HEAD_DIM ** -0.5)
                dqp_ref[:, cq] = _rope_bwd(dq, cos_b, sin_b, lo_half).astype(BF16)
                dqkv_ref[:, ck] = _rope_bwd(dka_ref[:, cq] + dkc[:ATT_BLOCK], cos_p, sin_p, lo_half).astype(BF16)
                dqkv_ref[:, cv] = (dva_ref[:, cq] + dvc[:ATT_BLOCK]).astype(BF16)
                dka_ref[:, cq] = dkc[ATT_BLOCK:]
                dva_ref[:, cq] = dvc[ATT_BLOCK:]
                kp_ref[:, cq] = kr
                vp_ref[:, cq] = v16

        @pl.when(jnp.logical_not(active))
        def _():
            for sl in range(WIDTH // LANES):
                cq = pl.ds(sl * LANES, LANES)
                dqkv_ref[:, pl.ds(WIDTH + sl * LANES, LANES)] = _rope_bwd(dka_ref[:, cq], cos_p, sin_p, lo_half).astype(BF16)
                dqkv_ref[:, pl.ds(2 * WIDTH + sl * LANES, LANES)] = dva_ref[:, cq].astype(BF16)

    cur = lambda w: pl.BlockSpec((None, ATT_BLOCK, w), lambda g, t: (g, jnp.minimum(t, nb - 1), 0))
    prev = lambda w: pl.BlockSpec((None, ATT_BLOCK, w), lambda g, t: (g, jnp.maximum(t - 1, 0), 0))
    return pl.pallas_call(
        body, name="dilated_attention_bwd", grid=(GROUPS, nb + 1),
        in_specs=[cur(3 * WIDTH), cur(LANES), cur(LANES), prev(LANES), prev(LANES),
                  cur(WIDTH), cur(WIDTH), cur(WIDTH), cur(WIDTH)],
        out_specs=prev(3 * WIDTH), out_shape=jax.ShapeDtypeStruct((GROUPS, s, 3 * WIDTH), BF16),
        scratch_shapes=[pltpu.VMEM((ATT_BLOCK, WIDTH), BF16), pltpu.VMEM((ATT_BLOCK, WIDTH), BF16),
                        pltpu.VMEM((ATT_BLOCK, WIDTH), F32), pltpu.VMEM((ATT_BLOCK, WIDTH), F32),
                        pltpu.VMEM((ATT_BLOCK, WIDTH), BF16)],
        compiler_params=_params("arbitrary", "arbitrary"),
    )(qkv, cos, sin, cos, sin, o, lse, do, dlse)


CONV_PAD = SUBLANES


def _gdn_post(y, is_q, is_k):
    head0, _ = _lane_masks(y.shape[0])
    c = _silu(y)
    sq = c * c
    ss0 = jnp.sum(jnp.where(head0, sq, 0.0), axis=1, keepdims=True)
    ss1 = jnp.sum(jnp.where(head0, 0.0, sq), axis=1, keepdims=True)
    r = jnp.where(head0, lax.rsqrt(ss0 + NORM_EPS), lax.rsqrt(ss1 + NORM_EPS))
    scale = jnp.where(is_q, HEAD_DIM ** -0.5, 1.0).astype(F32)
    return jnp.where(is_q | is_k, c * r * scale, c)


def _conv_rows(xp_ref, w, c0, rows):
    y = w[0:1, :] * xp_ref[pl.ds(c0 + CONV_PAD - (CONV_K - 1), rows), :]
    for k in range(1, CONV_K):
        y = y + w[k:k + 1, :] * xp_ref[pl.ds(c0 + CONV_PAD - (CONV_K - 1) + k, rows), :]
    return y


def _gdn_pre_fwd(proj_r, conv8, col0):
    s = proj_r.shape[0]
    tr = _row_block(s, 512)
    nblk = QKV_B // LANES
    nq = WIDTH // LANES

    def body(x_ref, w_ref, out_ref, xp_ref):
        j = pl.program_id(0)
        is_q, is_k = j < nq, (j >= nq) & (j < 2 * nq)
        xp_ref[pl.ds(0, CONV_PAD), :] = jnp.zeros((CONV_PAD, LANES), F32)
        xp_ref[pl.ds(CONV_PAD, s), :] = x_ref[...]
        w = w_ref[...]
        for c in range(s // tr):
            out_ref[pl.ds(c * tr, tr), :] = _gdn_post(_conv_rows(xp_ref, w, c * tr, tr), is_q, is_k)

    return pl.pallas_call(
        body, name="gdn_conv_fwd", grid=(nblk,),
        in_specs=[pl.BlockSpec((s, LANES), lambda j: (0, col0 + j)), pl.BlockSpec((SUBLANES, LANES), lambda j: (0, j))],
        out_specs=pl.BlockSpec((s, LANES), lambda j: (0, j)),
        out_shape=jax.ShapeDtypeStruct((s, QKV_B), F32),
        scratch_shapes=[pltpu.VMEM((s + CONV_PAD, LANES), F32)],
        compiler_params=_params("parallel"),
    )(proj_r, conv8)


def _gdn_pre_bwd(proj_r, conv8, dc, col0):
    s = proj_r.shape[0]
    tr = _row_block(s, 512)
    nblk = QKV_B // LANES
    nq = WIDTH // LANES

    def body(x_ref, w_ref, dc_ref, dx_ref, dw_ref, xp_ref, dyp_ref):
        j = pl.program_id(0)
        is_q, is_k = j < nq, (j >= nq) & (j < 2 * nq)
        xp_ref[pl.ds(0, CONV_PAD), :] = jnp.zeros((CONV_PAD, LANES), F32)
        xp_ref[pl.ds(CONV_PAD, s), :] = x_ref[...]
        dyp_ref[pl.ds(s, CONV_PAD), :] = jnp.zeros((CONV_PAD, LANES), F32)
        w = w_ref[...]
        for c in range(s // tr):
            y = _conv_rows(xp_ref, w, c * tr, tr)
            _, vjp = jax.vjp(lambda yy: _gdn_post(yy, is_q, is_k), y)
            dyp_ref[pl.ds(c * tr, tr), :] = vjp(dc_ref[pl.ds(c * tr, tr), :])[0]
        dws = [jnp.zeros((1, LANES), F32) for _ in range(CONV_K)]
        for c in range(s // tr):
            c0 = c * tr
            dy = dyp_ref[pl.ds(c0, tr), :]
            dx = w[0:1, :] * dyp_ref[pl.ds(c0 + CONV_K - 1, tr), :]
            for k in range(1, CONV_K):
                dx = dx + w[k:k + 1, :] * dyp_ref[pl.ds(c0 + CONV_K - 1 - k, tr), :]
            dx_ref[pl.ds(c0, tr), :] = dx.astype(BF16)
            for k in range(CONV_K):
                xs = xp_ref[pl.ds(c0 + CONV_PAD - (CONV_K - 1) + k, tr), :]
                dws[k] = dws[k] + jnp.sum(dy * xs, axis=0, keepdims=True)
        row = lax.broadcasted_iota(jnp.int32, (SUBLANES, LANES), 0)
        dwb = jnp.zeros((SUBLANES, LANES), F32)
        for k in range(CONV_K):
            dwb = dwb + jnp.where(row == k, dws[k], 0.0)
        dw_ref[...] = dwb

    return pl.pallas_call(
        body, name="gdn_conv_bwd", grid=(nblk,),
        in_specs=[pl.BlockSpec((s, LANES), lambda j: (0, col0 + j)), pl.BlockSpec((SUBLANES, LANES), lambda j: (0, j)),
                  pl.BlockSpec((s, LANES), lambda j: (0, j))],
        out_specs=(pl.BlockSpec((s, LANES), lambda j: (0, j)), pl.BlockSpec((SUBLANES, LANES), lambda j: (0, j))),
        out_shape=(jax.ShapeDtypeStruct((s, QKV_B), BF16), jax.ShapeDtypeStruct((SUBLANES, QKV_B), F32)),
        scratch_shapes=[pltpu.VMEM((s + CONV_PAD, LANES), F32), pltpu.VMEM((s + CONV_PAD, LANES), F32)],
        compiler_params=_params("parallel"),
    )(proj_r, conv8, dc)


def _gdn_chunk(q, k, v, bcol, acol, alog, dtb, gnw, state, t_saved=None):
    n = q.shape[-2]
    shp = (1, n, n)
    row = lax.broadcasted_iota(jnp.int32, shp, 1)
    col = lax.broadcasted_iota(jnp.int32, shp, 2)
    beta = _sigmoid(bcol)
    g = -jnp.exp(alog) * _softplus(acol + dtb)
    g_row = jnp.sum(jnp.where(row == col, g, 0.0), axis=-2, keepdims=True)
    big_g = jnp.sum(jnp.where(row >= col, g_row, 0.0), axis=-1, keepdims=True)
    big_g_row = jnp.sum(jnp.where(row <= col, g, 0.0), axis=-2, keepdims=True)
    decay_incl = jnp.exp(jnp.where(row >= col, big_g - big_g_row, -jnp.inf))
    decay_strict = jnp.where(row > col, decay_incl, 0.0)
    k_beta = k * beta
    a_mat = _mm_nt(k_beta, k) * decay_strict
    t_inv = _tri_inv(a_mat) if t_saved is None else _tri_inv_saved(a_mat, t_saved)
    e_g = jnp.exp(big_g)
    u = _mm(t_inv, v * beta)
    w = _mm(t_inv, k_beta * e_g)
    attn = _mm_nt(q, k) * decay_incl
    v_new = u - _mm(w, state)
    o = _mm(q * e_g, state) + _mm(attn, v_new)
    total = jnp.sum(g, axis=-2, keepdims=True)
    new_state = state * jnp.exp(total) + _mm_tn(k * jnp.exp(total - big_g), v_new)
    return _rmsnorm(o, gnw), new_state, t_inv


def _split_heads(x):
    return jnp.stack([x[:, h * HEAD_DIM:(h + 1) * HEAD_DIM] for h in range(HEADS)], axis=0)


def _merge_heads(x):
    return jnp.concatenate([x[h] for h in range(HEADS)], axis=1)


def _logit_columns(ba):
    lane = lax.broadcasted_iota(jnp.int32, ba.shape, 1)

    def cols(off):
        return jnp.stack([jnp.sum(jnp.where(lane == off + h, ba, 0.0), axis=1, keepdims=True) for h in range(HEADS)], axis=0)

    return cols(0), cols(HEADS)


def _logit_block(dbc, dac, shape):
    lane = lax.broadcasted_iota(jnp.int32, shape, 1)
    out = jnp.zeros(shape, F32)
    for h in range(HEADS):
        out = out + jnp.where(lane == h, dbc[h], 0.0) + jnp.where(lane == HEADS + h, dac[h], 0.0)
    return out


def _gdn_scan_fwd(cqkv, proj_r, ba_col, alog, dtb, gnw):
    s = cqkv.shape[0]
    nc = s // CHUNK
    span = SCAN_CHUNKS * CHUNK

    def body(q_ref, k_ref, v_ref, ba_ref, al_ref, dt_ref, gnw_ref, o_ref, st_ref, ti_ref, state_ref):
        @pl.when(pl.program_id(0) == 0)
        def _():
            state_ref[...] = jnp.zeros_like(state_ref)

        st = state_ref[...]
        for u in range(SCAN_CHUNKS):
            rows = pl.ds(u * CHUNK, CHUNK)
            st_ref[u] = st
            bcol, acol = _logit_columns(ba_ref[rows, :])
            o, st, t_inv = _gdn_chunk(_split_heads(q_ref[rows, :]), _split_heads(k_ref[rows, :]),
                                      _split_heads(v_ref[rows, :]), bcol, acol, al_ref[...], dt_ref[...], gnw_ref[...], st)
            o_ref[rows, :] = _merge_heads(o)
            ti_ref[u] = t_inv
        state_ref[...] = st

    part = lambda i: pl.BlockSpec((span, WIDTH), lambda n: (n, i))
    par = pl.BlockSpec((HEADS, 1, 1), lambda n: (0, 0, 0))
    per_chunk = pl.BlockSpec((SCAN_CHUNKS, HEADS, HEAD_DIM, HEAD_DIM), lambda n: (n, 0, 0, 0))
    per_chunk_shape = jax.ShapeDtypeStruct((nc, HEADS, HEAD_DIM, HEAD_DIM), F32)
    return pl.pallas_call(
        body, name="gdn_scan_fwd", grid=(nc // SCAN_CHUNKS,),
        in_specs=[part(0), part(1), part(2), pl.BlockSpec((span, LANES), lambda n: (n, ba_col)), par, par,
                  pl.BlockSpec((1, 1, HEAD_DIM), lambda n: (0, 0, 0))],
        out_specs=(part(0), per_chunk, per_chunk),
        out_shape=(jax.ShapeDtypeStruct((s, WIDTH), F32), per_chunk_shape, per_chunk_shape),
        scratch_shapes=[pltpu.VMEM((HEADS, HEAD_DIM, HEAD_DIM), F32)],
        compiler_params=_params("arbitrary"),
    )(cqkv, cqkv, cqkv, proj_r, alog, dtb, gnw)


def _gdn_scan_bwd(cqkv, proj_r, ba_col, alog, dtb, gnw, states, t_invs, do):
    s = cqkv.shape[0]
    nc = s // CHUNK
    span = SCAN_CHUNKS * CHUNK
    n_steps = nc // SCAN_CHUNKS

    def body(q_ref, k_ref, v_ref, ba_ref, al_ref, dt_ref, gnw_ref, st_ref, ti_ref, do_ref,
             dqkv_ref, dba_ref, dal_ref, ddt_ref, dgnw_ref, dstate_ref):
        @pl.when(pl.program_id(0) == 0)
        def _():
            dstate_ref[...] = jnp.zeros_like(dstate_ref)
            dal_ref[...] = jnp.zeros_like(dal_ref)
            ddt_ref[...] = jnp.zeros_like(ddt_ref)
            dgnw_ref[...] = jnp.zeros_like(dgnw_ref)

        dst = dstate_ref[...]
        for u in reversed(range(SCAN_CHUNKS)):
            rows = pl.ds(u * CHUNK, CHUNK)
            bcol, acol = _logit_columns(ba_ref[rows, :])
            _, vjp = jax.vjp(lambda *a, t_saved=ti_ref[u]: _gdn_chunk(*a, t_saved=t_saved)[:2],
                             _split_heads(q_ref[rows, :]), _split_heads(k_ref[rows, :]), _split_heads(v_ref[rows, :]),
                             bcol, acol, al_ref[...], dt_ref[...], gnw_ref[...], st_ref[u])
            dq, dk, dv, dbc, dac, dal, ddt, dgn, dst = vjp((_split_heads(do_ref[rows, :]), dst))
            dqkv_ref[rows, pl.ds(0, WIDTH)] = _merge_heads(dq)
            dqkv_ref[rows, pl.ds(WIDTH, WIDTH)] = _merge_heads(dk)
            dqkv_ref[rows, pl.ds(2 * WIDTH, WIDTH)] = _merge_heads(dv)
            dba_ref[rows, :] = _logit_block(dbc, dac, (CHUNK, LANES))
            dal_ref[...] += dal
            ddt_ref[...] += ddt
            dgnw_ref[...] += dgn
        dstate_ref[...] = dst

    rev = lambda n: n_steps - 1 - n
    part = lambda i: pl.BlockSpec((span, WIDTH), lambda n: (rev(n), i))
    par = pl.BlockSpec((HEADS, 1, 1), lambda n: (0, 0, 0))
    vec = pl.BlockSpec((1, 1, HEAD_DIM), lambda n: (0, 0, 0))
    par_shape = jax.ShapeDtypeStruct((HEADS, 1, 1), F32)
    per_chunk = pl.BlockSpec((SCAN_CHUNKS, HEADS, HEAD_DIM, HEAD_DIM), lambda n: (rev(n), 0, 0, 0))
    return pl.pallas_call(
        body, name="gdn_scan_bwd", grid=(n_steps,),
        in_specs=[part(0), part(1), part(2), pl.BlockSpec((span, LANES), lambda n: (rev(n), ba_col)), par, par, vec,
                  per_chunk, per_chunk, part(0)],
        out_specs=(pl.BlockSpec((span, QKV_B), lambda n: (rev(n), 0)), pl.BlockSpec((span, LANES), lambda n: (rev(n), 0)),
                   par, par, vec),
        out_shape=(jax.ShapeDtypeStruct((s, QKV_B), F32), jax.ShapeDtypeStruct((s, LANES), F32), par_shape, par_shape,
                   jax.ShapeDtypeStruct((1, 1, HEAD_DIM), F32)),
        scratch_shapes=[pltpu.VMEM((HEADS, HEAD_DIM, HEAD_DIM), F32)],
        compiler_params=_params("arbitrary"),
    )(cqkv, cqkv, cqkv, proj_r, alog, dtb, gnw, states, t_invs, do)


def _tail_loss(x, tgt, o0, o1, o2, l0, l1, l2, ga, gb, za, zb, ob, fnw, wua, wub, wo, tap_a, tap_b, tap_o):
    lm = jnp.maximum(jnp.maximum(l0, l1), l2)
    e0, e1, e2 = jnp.exp(l0 - lm), jnp.exp(l1 - lm), jnp.exp(l2 - lm)
    o_a = (e0 * o0 + e1 * o1 + e2 * o2) / (e0 + e1 + e2)
    y_a = _mm_tap(o_a * _silu(za), wua, tap_a)
    y_b = _mm_tap(ob * _silu(zb), wub, tap_b)
    merged = _sigmoid(ga) * y_a + _sigmoid(gb) * y_b
    y = _rmsnorm(x + _mm_tap(merged, wo, tap_o), fnw)
    err = y - tgt
    per_token = jnp.sum(err * err, axis=1, keepdims=True) * (0.5 / x.shape[1])
    return jnp.sum(per_token, axis=0, keepdims=True)


def _tail(x, tgt, o_all, lse_all, og12, lg12, proj_r, ob, wua, wub, wo, fnw):
    s, d = x.shape
    tm = _row_block(s, 128)
    col_za = 2 * d // WIDTH
    col_zb = (2 * d + WIDTH + QKV_B) // WIDTH

    def body(x_ref, t_ref, o0_ref, o1_ref, o2_ref, l0_ref, l1_ref, l2_ref, ga_ref, gb_ref, za_ref, zb_ref, ob_ref,
             wua_ref, wub_ref, wo_ref, fnw_ref,
             loss_ref, dx_ref, do0_ref, do1_ref, do2_ref, dl0_ref, dl1_ref, dl2_ref, dga_ref, dgb_ref, dza_ref,
             dzb_ref, dob_ref, dwua_ref, dwub_ref, dwo_ref, dfnw_ref):
        @pl.when(pl.program_id(0) == 0)
        def _():
            for r in (loss_ref, dwua_ref, dwub_ref, dwo_ref, dfnw_ref):
                r[...] = jnp.zeros_like(r)

        args = (x_ref[...], t_ref[...], o0_ref[...], o1_ref[...], o2_ref[...], l0_ref[...], l1_ref[...], l2_ref[...],
                ga_ref[...], gb_ref[...], za_ref[...], zb_ref[...], ob_ref[...], fnw_ref[...],
                wua_ref[...], wub_ref[...], wo_ref[...],
                jnp.zeros(wua_ref.shape, F32), jnp.zeros(wub_ref.shape, F32), jnp.zeros(wo_ref.shape, F32))
        loss, vjp = jax.vjp(_tail_loss, *args)
        (dx, _, do0, do1, do2, dl0, dl1, dl2, dga, dgb, dza, dzb, dob, dfnw, _, _, _, dwua, dwub, dwo) = vjp(
            jnp.ones((1, 1), F32))
        loss_ref[...] += jnp.broadcast_to(loss, loss_ref.shape)
        dx_ref[...] = dx
        do0_ref[...], do1_ref[...], do2_ref[...] = do0, do1, do2
        dl0_ref[...], dl1_ref[...], dl2_ref[...] = dl0, dl1, dl2
        dga_ref[...] = dga.astype(BF16)
        dgb_ref[...] = dgb.astype(BF16)
        dza_ref[...] = dza.astype(BF16)
        dzb_ref[...] = dzb.astype(BF16)
        dob_ref[...] = dob
        dwua_ref[...] += dwua
        dwub_ref[...] += dwub
        dwo_ref[...] += dwo
        dfnw_ref[...] += dfnw

    row = lambda w, c=0: pl.BlockSpec((tm, w), lambda i: (i, c))
    grp0 = pl.BlockSpec((None, tm, WIDTH), lambda i: (0, i, 0))
    full = lambda a, b: pl.BlockSpec((a, b), lambda i: (0, 0))
    f32 = lambda a, b: jax.ShapeDtypeStruct((a, b), F32)
    b16 = lambda a, b: jax.ShapeDtypeStruct((a, b), BF16)
    stacked = jax.ShapeDtypeStruct((GROUPS, s, WIDTH), F32)
    gspecs = [grp0, row(WIDTH), row(WIDTH)]
    in_specs = ([row(d), row(d)] + gspecs * 2 + [row(d, 0), row(d, 1), row(WIDTH, col_za), row(WIDTH, col_zb),
                row(WIDTH), full(WIDTH, d), full(WIDTH, d), full(d, d), full(1, d)])
    out_specs = ([full(SUBLANES, LANES), row(d)] + gspecs * 2 + [row(d), row(d), row(WIDTH), row(WIDTH), row(WIDTH),
                 full(WIDTH, d), full(WIDTH, d), full(d, d), full(1, d)])
    gshapes = [stacked, f32(s, WIDTH), f32(s, WIDTH)]
    out_shape = ([f32(SUBLANES, LANES), f32(s, d)] + gshapes * 2 + [b16(s, d), b16(s, d), b16(s, WIDTH),
                 b16(s, WIDTH), f32(s, WIDTH), f32(WIDTH, d), f32(WIDTH, d), f32(d, d), f32(1, d)])
    return pl.pallas_call(
        body, name="tail_fwd_bwd", grid=(s // tm,),
        in_specs=in_specs, out_specs=tuple(out_specs), out_shape=tuple(out_shape),
        compiler_params=_params("arbitrary"),
    )(x, tgt, o_all, og12[0], og12[1], lse_all, lg12[0], lg12[1], proj_r, proj_r, proj_r, proj_r, ob, wua, wub, wo, fnw)


PERMUTE_SPAN = 4096


def _permute_span(s):
    return PERMUTE_SPAN if s % PERMUTE_SPAN == 0 else s


def _from_dilated_rows(stacked, g, dil, name):
    n_slots, s, c = stacked.shape
    view = stacked.reshape(n_slots, dil, s // dil, c)
    span = _permute_span(s)

    def body(in_ref, out_ref):
        for r in range(dil):
            out_ref[pl.ds(r, span // dil, stride=dil), :] = in_ref[r]

    return pl.pallas_call(
        body, name=name, grid=(s // span, c // LANES),
        in_specs=[pl.BlockSpec((None, dil, span // dil, LANES), lambda n, j: (g, 0, n, j))],
        out_specs=pl.BlockSpec((span, LANES), lambda n, j: (n, j)),
        out_shape=jax.ShapeDtypeStruct((s, c), stacked.dtype),
        compiler_params=_params("parallel", "parallel"),
    )(view)


def _to_dilated_rows_into(nat, stacked, g, dil, name):
    n_slots, s, c = stacked.shape
    view = stacked.reshape(n_slots, dil, s // dil, c)
    span = _permute_span(s)

    def body(nat_ref, old_ref, out_ref):
        for r in range(dil):
            out_ref[r] = nat_ref[pl.ds(r, span // dil, stride=dil), :]

    out = pl.pallas_call(
        body, name=name, grid=(s // span, c // LANES),
        in_specs=[pl.BlockSpec((span, LANES), lambda n, j: (n, j)), pl.BlockSpec(memory_space=pl.ANY)],
        out_specs=pl.BlockSpec((None, dil, span // dil, LANES), lambda n, j: (g, 0, n, j)),
        out_shape=jax.ShapeDtypeStruct(view.shape, stacked.dtype),
        input_output_aliases={1: 0},
        compiler_params=_params("parallel", "parallel"),
    )(nat, view)
    return out.reshape(stacked.shape)


def _to_dilated(a, dil):
    if dil == 1:
        return a
    s = a.shape[0]
    return a.reshape(s // dil, dil, -1).transpose(1, 0, 2).reshape(a.shape)


def _from_dilated(a, dil):
    if dil == 1:
        return a
    s = a.shape[0]
    return a.reshape(dil, s // dil, -1).transpose(1, 0, 2).reshape(a.shape)


def _head_major(a):
    return a.reshape(a.shape[0], HEADS, HEAD_DIM).transpose(1, 0, 2)


def _from_head_major(a):
    return a.transpose(1, 0, 2).reshape(a.shape[1], WIDTH)


def _rope_tables(s):
    inv_freq = ROPE_THETA ** (-jnp.arange(0, HEAD_DIM, 2, dtype=F32) / HEAD_DIM)
    ang = jnp.arange(s, dtype=F32)[:, None] * inv_freq[None, :]
    cos_n = jnp.tile(jnp.cos(ang), (1, 2 * LANES // HEAD_DIM))
    sin_h = jnp.sin(ang)
    sin_n = jnp.tile(jnp.concatenate([-sin_h, sin_h], axis=1), (1, LANES // HEAD_DIM))
    def per_group(table, tag):
        out = jnp.broadcast_to(table, (GROUPS,) + table.shape)
        for g in range(1, GROUPS):
            out = _to_dilated_rows_into(table, out, g, DILATIONS[g], "rope_%s_to_dilated_%d" % (tag, g))
        return out

    return per_group(cos_n, "cos"), per_group(sin_n, "sin")


def _regroup_columns(pieces, widths):
    starts, pos = [], 0
    for p in pieces:
        starts.append(pos)
        pos += p.shape[1]
    assert pos == sum(widths), (pos, widths)
    out, lo = [], 0
    for w in widths:
        hi, parts = lo + w, []
        for p, st in zip(pieces, starts):
            a, b = max(lo, st), min(hi, st + p.shape[1])
            if a < b:
                parts.append(p[:, a - st:b - st])
        out.append(parts[0] if len(parts) == 1 else jnp.concatenate(parts, axis=1))
        lo = hi
    return out


def _pack_rows(parts, dtype, row_multiple):
    flat = jnp.concatenate([p.reshape(-1).astype(dtype) for p in parts])
    tile = row_multiple * LANES
    pad = (-flat.shape[0]) % tile
    return jnp.pad(flat, (0, pad)).reshape(-1, LANES)


def _unpack_rows(packed, shapes):
    flat = packed.reshape(-1)
    out, start = [], 0
    for shp in shapes:
        size = 1
        for n in shp:
            size *= n
        out.append(flat[start:start + size].reshape(shp))
        start += size
    return out


def kernel(x, norm_w, w_in, conv_w, a_log, dt_bias, gdn_norm_w, w_up_a, w_up_b, w_out, final_norm_w, loss_target, m_norm_w, m_w_in, m_conv_w, m_a_log, m_dt_bias, m_gdn_norm_w, m_w_up_a, m_w_up_b, m_w_out, m_final_norm_w, v_norm_w, v_w_in, v_conv_w, v_a_log, v_dt_bias, v_gdn_norm_w, v_w_up_a, v_w_up_b, v_w_out, v_final_norm_w):
    x2, tgt = x[0], loss_target[0]
    s, d = x2.shape
    me = 4 * lax.axis_index("x") + 2 * lax.axis_index("y") + lax.axis_index("c")
    win8 = w_in.shape[2]
    conv8w = conv_w.shape[2]

    conv_shard = jnp.pad(conv_w[0], ((0, SUBLANES - CONV_K), (0, 0)))
    w_in_g, wua_g, wub_g, wo_g, conv_g = _all_gather(
        [w_in[0].astype(BF16), w_up_a[0].astype(BF16), w_up_b[0].astype(BF16), w_out[0].astype(BF16), conv_shard])
    wua = jnp.concatenate([wua_g[i] for i in range(N_DEV)], axis=1)
    wub = jnp.concatenate([wub_g[i] for i in range(N_DEV)], axis=1)
    wo = wo_g.reshape(d, d)
    conv8 = jnp.concatenate([conv_g[i] for i in range(N_DEV)], axis=1)

    seg_widths = [QKV_B] * GROUPS + [WIDTH, QKV_B, WIDTH, 2 * HEADS, 2 * d]
    wq0, wq1, wq2, w_za, w_qkvb, w_zb, w_ba, w_gates = _regroup_columns([w_in_g[i] for i in range(N_DEV)], seg_widths)
    w_qkv = jnp.stack([wq0, wq1, wq2])
    w_rest = jnp.concatenate([w_gates, w_za, w_qkvb, w_zb, w_ba,
                              jnp.zeros((d, BA_PAD - 2 * HEADS), BF16)], axis=1)
    col_qkvb = (2 * d + WIDTH) // LANES
    col_ba = (2 * d + 2 * WIDTH + QKV_B) // LANES

    h = _rms_fwd(x2, norm_w)
    h_all = jnp.stack([_to_dilated(h, dil) for dil in DILATIONS])
    qkv_all = _matmul(h_all, w_qkv, F32, "in_proj_attention", tn=QKV_B)
    proj_r = _matmul(h[None], w_rest[None], F32, "in_proj_rest", tn=2560)[0]
    cos, sin = _rope_tables(s)
    o_all, lse_all = _attn_fwd(qkv_all, cos, sin)
    og12 = [_from_dilated_rows(o_all, g, DILATIONS[g], "attn_out_to_natural_%d" % g) for g in (1, 2)]
    lg12 = [_from_dilated_rows(lse_all, g, DILATIONS[g], "attn_lse_to_natural_%d" % g) for g in (1, 2)]

    cqkv = _gdn_pre_fwd(proj_r, conv8, col_qkvb)
    alog3, dtb3, gnw3 = a_log.reshape(HEADS, 1, 1), dt_bias.reshape(HEADS, 1, 1), gdn_norm_w.reshape(1, 1, HEAD_DIM)
    ob, states, t_invs = _gdn_scan_fwd(cqkv, proj_r, col_ba, alog3, dtb3, gnw3)

    (loss_blk, dx_res, do_all, do1, do2, dl_all, dl1, dl2, dga, dgb, dza, dzb, dob, dwua, dwub, dwo, dfnw) = _tail(
        x2, tgt, o_all, lse_all, og12, lg12, proj_r, ob, wua, wub, wo, final_norm_w.reshape(1, d))

    for g, (t_o, t_l) in ((1, (do1, dl1)), (2, (do2, dl2))):
        do_all = _to_dilated_rows_into(t_o, do_all, g, DILATIONS[g], "attn_dout_to_dilated_%d" % g)
        dl_all = _to_dilated_rows_into(t_l, dl_all, g, DILATIONS[g], "attn_dlse_to_dilated_%d" % g)
    dqkv_all = _attn_bwd(qkv_all, cos, sin, o_all, lse_all, do_all, dl_all)

    dcqkv, dba, dalog3, ddtb3, dgnw3 = _gdn_scan_bwd(cqkv, proj_r, col_ba, alog3, dtb3, gnw3, states, t_invs, dob)
    dqkv_b, dconv8 = _gdn_pre_bwd(proj_r, conv8, dcqkv, col_qkvb)
    dproj_r = jnp.concatenate([dga, dgb, dza, dqkv_b, dzb,
                               jnp.pad(dba.astype(BF16), ((0, 0), (0, BA_PAD - LANES)))], axis=1)

    def col_slabs(a, width):
        return jnp.stack([a[:, j * width:(j + 1) * width] for j in range(N_DEV)])

    core = lax.axis_index("c").astype(jnp.int32).reshape(1)
    small_slabs = [col_slabs(dwua, d // N_DEV), col_slabs(dwub, d // N_DEV), dwo.reshape(N_DEV, d // N_DEV, d)]
    dw_qkv, *small_sib = _matmul(h_all, dqkv_all, F32, "in_proj_attention_dw", mode="tn", tk=2048,
                                 exchange=_sibling_exchange(small_slabs))
    small_partials = [_pair_sum(a, b, core, "grads_pair_sum_%d" % (i + 1))
                      for i, (a, b) in enumerate(zip(small_slabs, small_sib))]
    dw_rest, *small_contrib = _matmul(h[None], dproj_r[None], F32, "in_proj_rest_dw", mode="tn", tk=2048,
                                      exchange=_chip_exchange(small_partials))
    dw_rest = dw_rest[0]
    o2 = 2 * d
    dw_in_pieces = [dw_qkv[0], dw_qkv[1], dw_qkv[2],
                    dw_rest[:, o2:o2 + WIDTH], dw_rest[:, o2 + WIDTH:o2 + WIDTH + QKV_B],
                    dw_rest[:, o2 + WIDTH + QKV_B:o2 + 2 * WIDTH + QKV_B],
                    dw_rest[:, o2 + 2 * WIDTH + QKV_B:o2 + 2 * WIDTH + QKV_B + 2 * HEADS],
                    dw_rest[:, :o2]]

    w_in_slabs = jnp.stack(_regroup_columns(dw_in_pieces, [win8] * N_DEV))
    dh_a, w_in_sib = _matmul(dqkv_all, w_qkv, F32, "in_proj_attention_dh", mode="nt", tk=2048,
                             exchange=_sibling_exchange([w_in_slabs]))
    w_in_partial = _pair_sum(w_in_slabs, w_in_sib, core, "grads_pair_sum_0")
    dh_r, w_in_contrib = _matmul(dproj_r[None], w_rest[None], F32, "in_proj_rest_dh", mode="nt", tk=2560,
                                 exchange=_chip_exchange([w_in_partial]))
    contrib = [w_in_contrib] + small_contrib
    dh_parts = [dh_r[0]] + [_from_dilated_rows(dh_a, g, DILATIONS[g], "dh_to_natural_%d" % g) for g in (1, 2)]
    grad_x, dnorm_w = _rms_bwd(x2, norm_w, dh_a, dh_parts, dx_res)

    small_parts = [dnorm_w, dfnw, dconv8[:CONV_K], dalog3[:, 0, 0], ddtb3[:, 0, 0], dgnw3[0], loss_blk[0, 0:1]]
    small_rows = [-(-p.size // LANES) for p in small_parts]
    small = jnp.concatenate([jnp.pad(p.reshape(-1), (0, r * LANES - p.size)).reshape(r, LANES)
                             for p, r in zip(small_parts, small_rows)])
    small = jnp.pad(small, ((0, (-small.shape[0]) % SUBLANES), (0, 0)))
    small_sum = _small_all_reduce(small)
    pieces, r0 = [], 0
    for p, r in zip(small_parts, small_rows):
        pieces.append(small_sum[r0:r0 + r].reshape(-1)[:p.size].reshape(p.shape))
        r0 += r
    g_norm_w, g_fnw, g_conv_full, g_alog, g_dtb, g_gnw, loss_sum = pieces
    g_conv = lax.dynamic_slice(g_conv_full, (0, me * conv8w), (CONV_K, conv8w))

    big = [_adamw(c, w[0], m[0], v[0], name) for c, w, m, v, name in (
        (contrib[0], w_in, m_w_in, v_w_in, "adamw_w_in"), (contrib[1], w_up_a, m_w_up_a, v_w_up_a, "adamw_w_up_a"),
        (contrib[2], w_up_b, m_w_up_b, v_w_up_b, "adamw_w_up_b"), (contrib[3], w_out, m_w_out, v_w_out, "adamw_w_out"))]
    g_big, d_big, nm_big, nv_big = ([t[i] for t in big] for i in range(4))

    small_ws = [norm_w, final_norm_w, conv_w, a_log, dt_bias, gdn_norm_w]
    small_ms = [m_norm_w, m_final_norm_w, m_conv_w, m_a_log, m_dt_bias, m_gdn_norm_w]
    small_vs = [v_norm_w, v_final_norm_w, v_conv_w, v_a_log, v_dt_bias, v_gdn_norm_w]
    small_gs = [g_norm_w, g_fnw, g_conv, g_alog, g_dtb, g_gnw]
    small_shapes = [t.shape for t in small_ws]
    sm = _adamw(_pack_rows(small_gs, F32, SUBLANES)[None], _pack_rows(small_ws, F32, SUBLANES),
                _pack_rows(small_ms, F32, SUBLANES), _pack_rows(small_vs, F32, SUBLANES), "adamw_small")
    g_sm, d_sm, nm_sm, nv_sm = (_unpack_rows(t, small_shapes) for t in sm)

    def ordered(bigs, smalls):
        nw, fnw_, cw, al, dtb, gn = smalls
        wi, ua, ub, wo_ = (t[None] for t in bigs)
        return [nw, wi, cw, al, dtb, gn, ua, ub, wo_, fnw_]

    return (loss_sum.reshape(()), grad_x[None], *ordered(g_big, g_sm), *ordered(d_big, d_sm),
            *ordered(nm_big, nm_sm), *ordered(nv_big, nv_sm))
```

```python
import functools

import jax
import jax.numpy as jnp
from jax import lax
from jax.experimental import pallas as pl
from jax.experimental.pallas import tpu as pltpu

F32 = jnp.float32
BF16 = jnp.bfloat16
MESH = pl.DeviceIdType.MESH
N_DEV = 8
LANES = 128
SUBLANES = 8

GROUPS = 3
HEADS = 8
HEAD_DIM = 64
WIDTH = HEADS * HEAD_DIM
ATT_BLOCK = 128
DILATIONS = (1, 4, 16)
N_BACK = 128
CONV_K = 4
CHUNK = 64
SCAN_CHUNKS = 4
QKV_B = 3 * WIDTH
QKV_A = GROUPS * 3 * WIDTH
BA_PAD = 512
NORM_EPS = 1e-6
ROPE_THETA = 10000.0
ADAM_LR, ADAM_B1, ADAM_B2, ADAM_EPS, ADAM_WD, ADAM_STEP = 0.001, 0.9, 0.999, 1e-08, 0.01, 10

VMEM_LIMIT = 56 * 1024 * 1024
TAIL_ROWS = 256
TAIL_VMEM_LIMIT = 62 * 1024 * 1024

OFF_ZA = QKV_A
OFF_QKVB = OFF_ZA + WIDTH
OFF_ZB = OFF_QKVB + QKV_B
OFF_BA = OFF_ZB + WIDTH
OFF_GATE = OFF_BA + 2 * HEADS


def _params(*sem):
    return pltpu.CompilerParams(dimension_semantics=sem, vmem_limit_bytes=VMEM_LIMIT)


def _dg(a, b, ca, cb):
    nb = a.ndim - 2
    batch = tuple(range(nb))
    return lax.dot_general(a, b, (((nb + ca,), (nb + cb,)), (batch, batch)), preferred_element_type=F32)


@jax.custom_vjp
def _mm(a, b):
    return _dg(a.astype(BF16), b.astype(BF16), 1, 0)


def _mm_fwd(a, b):
    return _mm(a, b), (a.astype(BF16), b.astype(BF16))


def _mm_bwd(res, ct):
    a16, b16 = res
    c16 = ct.astype(BF16)
    return _dg(c16, b16, 1, 1), _dg(a16, c16, 0, 0)


_mm.defvjp(_mm_fwd, _mm_bwd)


@jax.custom_vjp
def _mm_nt(a, b):
    return _dg(a.astype(BF16), b.astype(BF16), 1, 1)


def _mm_nt_fwd(a, b):
    return _mm_nt(a, b), (a.astype(BF16), b.astype(BF16))


def _mm_nt_bwd(res, ct):
    a16, b16 = res
    c16 = ct.astype(BF16)
    return _dg(c16, b16, 1, 0), _dg(c16, a16, 0, 0)


_mm_nt.defvjp(_mm_nt_fwd, _mm_nt_bwd)


@jax.custom_vjp
def _mm_tn(a, b):
    return _dg(a.astype(BF16), b.astype(BF16), 0, 0)


def _mm_tn_fwd(a, b):
    return _mm_tn(a, b), (a.astype(BF16), b.astype(BF16))


def _mm_tn_bwd(res, ct):
    a16, b16 = res
    c16 = ct.astype(BF16)
    return _dg(b16, c16, 1, 1), _dg(a16, c16, 1, 0)


_mm_tn.defvjp(_mm_tn_fwd, _mm_tn_bwd)


@jax.custom_vjp
def _mm_x(a, w16):
    return _dg(a.astype(BF16), w16, 1, 0)


def _mm_x_fwd(a, w16):
    return _mm_x(a, w16), w16


def _mm_x_bwd(w16, ct):
    return _dg(ct.astype(BF16), w16, 1, 1), jnp.zeros_like(w16)


_mm_x.defvjp(_mm_x_fwd, _mm_x_bwd)


def _split16(a):
    hi = a.astype(BF16)
    lo = (a - hi.astype(F32)).astype(BF16)
    return hi, lo


def _dot3(a, b, ca, cb):
    ah, al = _split16(a)
    bh, bl = _split16(b)
    return _dg(ah, bh, ca, cb) + (_dg(ah, bl, ca, cb) + _dg(al, bh, ca, cb))


def _tri_inv_impl(a):
    n = a.shape[-1]
    shp = (1,) * (a.ndim - 2) + (n, n)
    eye = (lax.broadcasted_iota(jnp.int32, shp, a.ndim - 2) == lax.broadcasted_iota(jnp.int32, shp, a.ndim - 1)).astype(F32)
    x = eye - a
    p = a
    for it in range(5):
        dot = _dot3 if it < 2 else (lambda u, v, cu, cv: _dg(u.astype(BF16), v.astype(BF16), cu, cv))
        p = dot(p, p, 1, 0)
        x = x + dot(x, p, 1, 0)
    return x


@jax.custom_vjp
def _tri_inv(a):
    return _tri_inv_impl(a)


def _tri_inv_fwd(a):
    t = _tri_inv_impl(a)
    return t, t


def _tri_inv_bwd(t, ct):
    t16 = t.astype(BF16)
    return (-_dg(_dg(t16, ct.astype(BF16), 0, 0).astype(BF16), t16, 1, 1),)


_tri_inv.defvjp(_tri_inv_fwd, _tri_inv_bwd)


@jax.custom_vjp
def _tri_inv_saved(a, t):
    return t


def _tri_inv_saved_fwd(a, t):
    return t, t


def _tri_inv_saved_bwd(t, ct):
    return _tri_inv_bwd(t, ct) + (jnp.zeros_like(t),)


_tri_inv_saved.defvjp(_tri_inv_saved_fwd, _tri_inv_saved_bwd)


def _sigmoid(x):
    return 1.0 / (1.0 + jnp.exp(-x))


def _silu(x):
    return x * _sigmoid(x)


def _softplus(x):
    return jnp.maximum(x, 0.0) + jnp.log(1.0 + jnp.exp(-jnp.abs(x)))


def _rmsnorm(x, w):
    return x * lax.rsqrt(jnp.mean(x * x, axis=-1, keepdims=True) + NORM_EPS) * w


def _row_block(rows, cap):
    best = None
    for cand in range(SUBLANES, min(rows, cap) + 1, SUBLANES):
        if rows % cand == 0:
            best = cand
    assert best is not None, rows
    return best


def _mesh_peers():
    x, y, c = lax.axis_index("x"), lax.axis_index("y"), lax.axis_index("c")
    me = 4 * x + 2 * y + c
    peers = []
    for k in range(1, N_DEV):
        px = 1 - x if (k >> 2) & 1 else x
        py = 1 - y if (k >> 1) & 1 else y
        pc = 1 - c if k & 1 else c
        peers.append(((px, py, pc), 4 * px + 2 * py + pc))
    return me, peers


N_CHIPS = 4
OTHER_CHIPS = 3


def _chip_peers():
    x, y, c = lax.axis_index("x"), lax.axis_index("y"), lax.axis_index("c")
    return x, y, c, [(1 - x, y), (x, 1 - y), (1 - x, 1 - y)]


def _all_gather(shards):
    n_arr = len(shards)
    per = 1 + 2 * OTHER_CHIPS

    def body(*refs):
        in_refs, out_refs = refs[:n_arr], refs[n_arr:2 * n_arr]
        send_sems, recv_sems, loc_sems = refs[2 * n_arr:]
        x, y, c, chips = _chip_peers()
        me, sibling = (x, y, c), (x, y, 1 - c)

        def slot(px, py, pc):
            return 4 * px + 2 * py + pc

        def copy(i, k, block, to, src=None):
            dst = out_refs[i].at[slot(*block)]
            return pltpu.make_async_remote_copy(src_ref=dst if src is None else src, dst_ref=dst,
                                                send_sem=send_sems.at[i * per + k], recv_sem=recv_sems.at[i * per + k],
                                                device_id=to, device_id_type=MESH)

        own = [pltpu.make_async_copy(in_refs[i], out_refs[i].at[slot(*me)], loc_sems.at[i]) for i in range(n_arr)]
        for cp in own:
            cp.start()
        first = []
        for i in range(n_arr):
            first += [copy(i, 1 + j, me, (*chip, c), src=in_refs[i]) for j, chip in enumerate(chips)]
            first.append(copy(i, 0, me, sibling, src=in_refs[i]))
        for cp in first:
            cp.start()
        passed = []
        for j, chip in enumerate(chips):
            for i in range(n_arr):
                copy(i, 1 + j, (*chip, c), me).wait_recv()
                fwd = copy(i, 1 + OTHER_CHIPS + j, (*chip, c), sibling)
                fwd.start()
                passed.append(fwd)
        for i in range(n_arr):
            copy(i, 0, sibling, me).wait_recv()
            for j, chip in enumerate(chips):
                copy(i, 1 + OTHER_CHIPS + j, (*chip, 1 - c), me).wait_recv()
        for cp in first + passed:
            cp.wait_send()
        for cp in own:
            cp.wait()

    any_spec = pl.BlockSpec(memory_space=pl.ANY)
    return pl.pallas_call(
        body, name="weights_all_gather",
        out_shape=tuple(jax.ShapeDtypeStruct((N_DEV,) + a.shape, a.dtype) for a in shards),
        in_specs=[any_spec] * n_arr, out_specs=tuple([any_spec] * n_arr),
        scratch_shapes=[pltpu.SemaphoreType.DMA((n_arr * per,)), pltpu.SemaphoreType.DMA((n_arr * per,)),
                        pltpu.SemaphoreType.DMA((n_arr,))],
    )(*shards)


class _Exchange:
    def __init__(self, arrays, out_shapes, n_sem, copies):
        self.arrays, self.out_shapes, self.n_sem, self.copies = list(arrays), list(out_shapes), n_sem, copies


def _sibling_exchange(slabs):
    n_arr = len(slabs)

    def copies(in_refs, out_refs, send_sems, recv_sems, loc_sems):
        x, y, c, _ = _chip_peers()
        sends = [pltpu.make_async_remote_copy(src_ref=in_refs[i].at[2 * q + (1 - c)], dst_ref=out_refs[i].at[q],
                                              send_sem=send_sems.at[i * N_CHIPS + q], recv_sem=recv_sems.at[i * N_CHIPS + q],
                                              device_id=(x, y, 1 - c), device_id_type=MESH)
                 for i in range(n_arr) for q in range(N_CHIPS)]

        def start():
            for cp in sends:
                cp.start()

        def finish():
            for cp in sends:
                cp.wait_recv()
            for cp in sends:
                cp.wait_send()

        return start, finish

    return _Exchange(slabs, [jax.ShapeDtypeStruct((N_CHIPS,) + a.shape[1:], a.dtype) for a in slabs],
                     n_arr * N_CHIPS, copies)


def _pair_sum(slabs, from_sibling, core, name):
    _, rows, cols = slabs.shape
    tr = _row_block(rows, max(SUBLANES, (256 * 1024) // cols // SUBLANES * SUBLANES))

    def body(core_ref, a_ref, b_ref, o_ref):
        o_ref[...] = (a_ref[...] + b_ref[...]).astype(BF16)

    grid_spec = pltpu.PrefetchScalarGridSpec(
        num_scalar_prefetch=1, grid=(N_CHIPS, rows // tr),
        in_specs=[pl.BlockSpec((None, tr, cols), lambda q, r, core_ref: (2 * q + core_ref[0], r, 0)),
                  pl.BlockSpec((None, tr, cols), lambda q, r, core_ref: (q, r, 0))],
        out_specs=pl.BlockSpec((None, tr, cols), lambda q, r, core_ref: (q, r, 0)))
    return pl.pallas_call(
        body, name=name, grid_spec=grid_spec,
        out_shape=jax.ShapeDtypeStruct((N_CHIPS, rows, cols), BF16),
        compiler_params=_params("parallel", "parallel"),
    )(core, slabs, from_sibling)


def _chip_exchange(partials):
    n_arr = len(partials)

    def copies(in_refs, out_refs, send_sems, recv_sems, loc_sems):
        x, y, c, chips = _chip_peers()
        mine = 2 * x + y
        own = [pltpu.make_async_copy(in_refs[i].at[mine], out_refs[i].at[mine], loc_sems.at[i]) for i in range(n_arr)]

        def copy(i, j, chip, src_slot, dst_slot):
            return pltpu.make_async_remote_copy(src_ref=in_refs[i].at[src_slot], dst_ref=out_refs[i].at[dst_slot],
                                                send_sem=send_sems.at[i * OTHER_CHIPS + j],
                                                recv_sem=recv_sems.at[i * OTHER_CHIPS + j],
                                                device_id=(*chip, c), device_id_type=MESH)

        sends = [copy(i, j, chip, 2 * chip[0] + chip[1], mine) for j, chip in enumerate(chips) for i in range(n_arr)]
        recvs = [copy(i, j, chip, mine, 2 * chip[0] + chip[1]) for j, chip in enumerate(chips) for i in range(n_arr)]

        def start():
            for cp in own + sends:
                cp.start()

        def finish():
            for cp in recvs:
                cp.wait_recv()
            for cp in sends:
                cp.wait_send()
            for cp in own:
                cp.wait()

        return start, finish

    return _Exchange(partials, [jax.ShapeDtypeStruct(a.shape, a.dtype) for a in partials], n_arr * OTHER_CHIPS, copies)


def _small_all_reduce(part):
    rows = part.shape[0]

    def body(p_ref, o_ref, buf_ref, send_sems, recv_sems):
        me, peers = _mesh_peers()
        buf_ref[me] = p_ref[...]
        sends = []
        for k, (dev, pid) in enumerate(peers):
            cp = pltpu.make_async_remote_copy(src_ref=p_ref, dst_ref=buf_ref.at[me], send_sem=send_sems.at[k],
                                              recv_sem=recv_sems.at[k], device_id=dev, device_id_type=MESH)
            cp.start()
            sends.append(cp)
        for k, (dev, pid) in enumerate(peers):
            pltpu.make_async_remote_copy(src_ref=p_ref, dst_ref=buf_ref.at[pid], send_sem=send_sems.at[k],
                                         recv_sem=recv_sems.at[k], device_id=dev, device_id_type=MESH).wait_recv()
        for cp in sends:
            cp.wait_send()
        acc = buf_ref[0]
        for i in range(1, N_DEV):
            acc = acc + buf_ref[i]
        o_ref[...] = acc

    vmem = pl.BlockSpec(memory_space=pltpu.VMEM)
    return pl.pallas_call(
        body, name="small_all_reduce",
        out_shape=jax.ShapeDtypeStruct(part.shape, F32),
        in_specs=[vmem], out_specs=vmem,
        scratch_shapes=[pltpu.VMEM((N_DEV, rows, LANES), F32), pltpu.SemaphoreType.DMA((N_DEV - 1,)),
                        pltpu.SemaphoreType.DMA((N_DEV - 1,))],
    )(part)


def _adamw_vals(w, g, m, v):
    m = ADAM_B1 * m + (1.0 - ADAM_B1) * g
    v = ADAM_B2 * v + (1.0 - ADAM_B2) * (g * g)
    m_hat = m / (1.0 - ADAM_B1 ** ADAM_STEP)
    v_hat = v / (1.0 - ADAM_B2 ** ADAM_STEP)
    delta = -ADAM_LR * (m_hat / (jnp.sqrt(v_hat) + ADAM_EPS) + ADAM_WD * w)
    return delta, m, v


def _adamw(contrib, w, m, v, name):
    n, rows, cols = contrib.shape
    tr = _row_block(rows, max(SUBLANES, (128 * 1024) // cols // SUBLANES * SUBLANES))

    def body(c_ref, w_ref, m_ref, v_ref, g_ref, d_ref, nm_ref, nv_ref):
        g = c_ref[0].astype(F32)
        for i in range(1, n):
            g = g + c_ref[i].astype(F32)
        delta, nm, nv = _adamw_vals(w_ref[...], g, m_ref[...], v_ref[...])
        g_ref[...] = g
        d_ref[...] = delta
        nm_ref[...] = nm
        nv_ref[...] = nv

    row = pl.BlockSpec((tr, cols), lambda i: (i, 0))
    shp = jax.ShapeDtypeStruct((rows, cols), F32)
    return pl.pallas_call(
        body, name=name, grid=(rows // tr,),
        in_specs=[pl.BlockSpec((n, tr, cols), lambda i: (0, i, 0)), row, row, row],
        out_specs=(row, row, row, row), out_shape=(shp, shp, shp, shp),
        compiler_params=_params("parallel"),
    )(contrib, w, m, v)


def _lane_block(n, cap):
    if n <= cap:
        return n
    best = None
    for cand in range(LANES, cap + 1, LANES):
        if n % cand == 0:
            best = cand
    assert best is not None, n
    return best


def _matmul(a, b, out_dtype, name, mode="nn", tm=1024, tn=1024, tk=1024, exchange=None):
    g = a.shape[0]
    m, k = (a.shape[2], a.shape[1]) if mode == "tn" else (a.shape[1], a.shape[2])
    n = b.shape[1] if mode == "nt" else b.shape[2]
    tm, tn, tk = _lane_block(m, tm), _lane_block(n, tn), _lane_block(k, tk)
    nk = k // tk
    grid = (g, m // tm, n // tn, nk)
    a_spec = (pl.BlockSpec((None, tk, tm), lambda gi, i, j, kk: (gi, kk, i)) if mode == "tn" else
              pl.BlockSpec((None, tm, tk), lambda gi, i, j, kk: (gi, i, kk)))
    b_spec = (pl.BlockSpec((None, tn, tk), lambda gi, i, j, kk: (gi, j, kk)) if mode == "nt" else
              pl.BlockSpec((None, tk, tn), lambda gi, i, j, kk: (gi, kk, j)))
    ca, cb = (0 if mode == "tn" else 1), (1 if mode == "nt" else 0)
    n_ex = 0 if exchange is None else len(exchange.arrays)

    def body(a_ref, b_ref, *rest):
        ex_in, o_ref, ex_out, scratch = rest[:n_ex], rest[n_ex], rest[n_ex + 1:2 * n_ex + 1], rest[2 * n_ex + 1:]
        if exchange is not None:
            start, finish = exchange.copies(ex_in, ex_out, *scratch[-3:])
            pids = [pl.program_id(ax) for ax in range(4)]
            pl.when((pids[0] == 0) & (pids[1] == 0) & (pids[2] == 0) & (pids[3] == 0))(start)
        part = _dg(a_ref[...], b_ref[...], ca, cb)
        if nk == 1:
            o_ref[...] = part.astype(o_ref.dtype)
        else:
            acc_ref = scratch[0]
            kk = pl.program_id(3)

            @pl.when(kk == 0)
            def _():
                acc_ref[...] = part

            @pl.when((kk > 0) & (kk < nk - 1))
            def _():
                acc_ref[...] += part

            @pl.when(kk == nk - 1)
            def _():
                o_ref[...] = (acc_ref[...] + part).astype(o_ref.dtype)
        if exchange is not None:
            pl.when((pids[0] == grid[0] - 1) & (pids[1] == grid[1] - 1) & (pids[2] == grid[2] - 1)
                    & (pids[3] == grid[3] - 1))(finish)

    any_spec = pl.BlockSpec(memory_space=pl.ANY)
    scratch_shapes = [] if nk == 1 else [pltpu.VMEM((tm, tn), F32)]
    out_shape = [jax.ShapeDtypeStruct((g, m, n), out_dtype)]
    if exchange is not None:
        scratch_shapes += [pltpu.SemaphoreType.DMA((exchange.n_sem,)), pltpu.SemaphoreType.DMA((exchange.n_sem,)),
                           pltpu.SemaphoreType.DMA((n_ex,))]
        out_shape += exchange.out_shapes
    outs = pl.pallas_call(
        body, name=name, grid=grid,
        in_specs=[a_spec, b_spec] + [any_spec] * n_ex,
        out_specs=tuple([pl.BlockSpec((None, tm, tn), lambda gi, i, j, kk: (gi, i, j))] + [any_spec] * n_ex),
        out_shape=tuple(out_shape),
        scratch_shapes=scratch_shapes,
        compiler_params=(_params("parallel", "parallel", "parallel", "arbitrary") if exchange is None else
                         _params("arbitrary", "arbitrary", "arbitrary", "arbitrary")),
    )(a, b, *([] if exchange is None else exchange.arrays))
    return outs[0] if exchange is None else outs


def _rms_fwd(x, w):
    s, d = x.shape
    tm = _row_block(s, 512)

    def body(x_ref, w_ref, h_ref):
        h_ref[...] = _rmsnorm(x_ref[...], w_ref[...]).astype(BF16)

    return pl.pallas_call(
        body, name="input_rmsnorm", grid=(s // tm,),
        in_specs=[pl.BlockSpec((tm, d), lambda i: (i, 0)), pl.BlockSpec((1, d), lambda i: (0, 0))],
        out_specs=pl.BlockSpec((tm, d), lambda i: (i, 0)),
        out_shape=jax.ShapeDtypeStruct((s, d), BF16),
        compiler_params=_params("parallel"),
    )(x, w)


def _rms_bwd(x, w, dh_stacked, dh_parts, dx_res):
    s, d = x.shape
    tm = _row_block(s, 256)
    n_parts = 1 + len(dh_parts)

    def body(x_ref, w_ref, *rest):
        part_refs = rest[:n_parts]
        res_ref, gx_ref, gw_ref = rest[n_parts:]
        dh = part_refs[0][...]
        for r in part_refs[1:]:
            dh = dh + r[...]
        _, vjp = jax.vjp(_rmsnorm, x_ref[...], w_ref[...])
        dx, dw = vjp(dh)
        gx_ref[...] = dx + res_ref[...]

        @pl.when(pl.program_id(0) == 0)
        def _():
            gw_ref[...] = jnp.zeros_like(gw_ref)

        gw_ref[...] += dw

    row = pl.BlockSpec((tm, d), lambda i: (i, 0))
    vec = pl.BlockSpec((1, d), lambda i: (0, 0))
    return pl.pallas_call(
        body, name="input_rmsnorm_bwd", grid=(s // tm,),
        in_specs=[row, vec, pl.BlockSpec((None, tm, d), lambda i: (0, i, 0))] + [row] * (n_parts - 1) + [row],
        out_specs=(row, vec),
        out_shape=(jax.ShapeDtypeStruct((s, d), F32), jax.ShapeDtypeStruct((1, d), F32)),
        compiler_params=_params("arbitrary"),
    )(x, w, dh_stacked, *dh_parts, dx_res)


def _lane_masks(rows):
    lane = lax.broadcasted_iota(jnp.int32, (rows, LANES), 1)
    return lane < HEAD_DIM, (lane & (HEAD_DIM - 1)) < HEAD_DIM // 2


def _swap_halves(t, lo_half):
    return jnp.where(lo_half, pltpu.roll(t, LANES - HEAD_DIM // 2, 1), pltpu.roll(t, HEAD_DIM // 2, 1))


def _rope(t, cos, sin_signed, lo_half):
    return t * cos + _swap_halves(t, lo_half) * sin_signed


def _rope_bwd(d, cos, sin_signed, lo_half):
    return d * cos - _swap_halves(d, lo_half) * sin_signed


def _window_mask(first):
    qi = lax.broadcasted_iota(jnp.int32, (2 * ATT_BLOCK, 2 * ATT_BLOCK), 0) & (ATT_BLOCK - 1)
    kj = lax.broadcasted_iota(jnp.int32, (2 * ATT_BLOCK, 2 * ATT_BLOCK), 1)
    dist = qi + ATT_BLOCK - kj
    return (dist >= 0) & (dist <= N_BACK) & ((kj >= ATT_BLOCK) | jnp.logical_not(first))


def _stack_heads(t, head0):
    zero = jnp.zeros_like(t)
    return jnp.concatenate([jnp.where(head0, t, zero), jnp.where(head0, zero, t)], axis=0)


def _unstack_heads(t2, head0):
    return jnp.where(head0, t2[:ATT_BLOCK], t2[ATT_BLOCK:])


def _blocks_per_subsequence(g, nb):
    return lax.shift_right_logical(jnp.int32(nb), 2 * g)


def _attn_fwd(qkv, cos, sin):
    _, s, _ = qkv.shape
    nb = s // ATT_BLOCK

    def body(qkv_ref, cos_ref, sin_ref, o_ref, lse_ref, kp_ref, vp_ref):
        g, t = pl.program_id(0), pl.program_id(1)
        first = (t & (_blocks_per_subsequence(g, nb) - 1)) == 0

        @pl.when(first)
        def _():
            kp_ref[...] = jnp.zeros_like(kp_ref)
            vp_ref[...] = jnp.zeros_like(vp_ref)

        cos_b, sin_b = cos_ref[...], sin_ref[...]
        head0, lo_half = _lane_masks(ATT_BLOCK)
        valid = _window_mask(first)
        for sl in range(WIDTH // LANES):
            cq = pl.ds(sl * LANES, LANES)
            ck = pl.ds(WIDTH + sl * LANES, LANES)
            cv = pl.ds(2 * WIDTH + sl * LANES, LANES)
            qr = (_rope(qkv_ref[:, cq], cos_b, sin_b, lo_half) * (HEAD_DIM ** -0.5)).astype(BF16)
            kr = _rope(qkv_ref[:, ck], cos_b, sin_b, lo_half).astype(BF16)
            v16 = qkv_ref[:, cv].astype(BF16)
            kcat = jnp.concatenate([kp_ref[:, cq], kr], axis=0)
            vcat = jnp.concatenate([vp_ref[:, cq], v16], axis=0)
            sc = jnp.where(valid, _dg(_stack_heads(qr, head0), kcat, 1, 1), -jnp.inf)
            mx = jnp.max(sc, axis=1, keepdims=True)
            p = jnp.exp(sc - mx)
            den = jnp.sum(p, axis=1, keepdims=True)
            o_ref[:, cq] = _unstack_heads(_dg((p * (1.0 / den)).astype(BF16), vcat, 1, 0), head0)
            lse2 = mx + jnp.log(den)
            lse_ref[:, cq] = jnp.where(head0, lse2[:ATT_BLOCK], lse2[ATT_BLOCK:])
            kp_ref[:, cq] = kr
            vp_ref[:, cq] = v16

    blk = lambda w: pl.BlockSpec((None, ATT_BLOCK, w), lambda g, t: (g, t, 0))
    shp = jax.ShapeDtypeStruct((GROUPS, s, WIDTH), F32)
    return pl.pallas_call(
        body, name="dilated_attention_fwd", grid=(GROUPS, nb),
        in_specs=[blk(3 * WIDTH), blk(LANES), blk(LANES)],
        out_specs=(blk(WIDTH), blk(WIDTH)), out_shape=(shp, shp),
        scratch_shapes=[pltpu.VMEM((ATT_BLOCK, WIDTH), BF16), pltpu.VMEM((ATT_BLOCK, WIDTH), BF16)],
        compiler_params=_params("arbitrary", "arbitrary"),
    )(qkv, cos, sin)


def _attn_bwd(qkv, cos, sin, o, lse, do, dlse):
    _, s, _ = qkv.shape
    nb = s // ATT_BLOCK

    def body(qkv_ref, cos_ref, sin_ref, cosp_ref, sinp_ref, o_ref, lse_ref, do_ref, dlse_ref,
             dqkv_ref, kp_ref, vp_ref, dka_ref, dva_ref, dqp_ref):
        g, t = pl.program_id(0), pl.program_id(1)
        first = (t & (_blocks_per_subsequence(g, nb) - 1)) == 0
        active = t < nb
        head0, lo_half = _lane_masks(ATT_BLOCK)
        cos_p, sin_p = cosp_ref[...], sinp_ref[...]

        @pl.when(t == 0)
        def _():
            dka_ref[...] = jnp.zeros_like(dka_ref)
            dva_ref[...] = jnp.zeros_like(dva_ref)
            dqp_ref[...] = jnp.zeros_like(dqp_ref)

        dqkv_ref[:, pl.ds(0, WIDTH)] = dqp_ref[...]

        @pl.when(active & first)
        def _():
            kp_ref[...] = jnp.zeros_like(kp_ref)
            vp_ref[...] = jnp.zeros_like(vp_ref)

        @pl.when(active)
        def _():
            cos_b, sin_b = cos_ref[...], sin_ref[...]
            valid = _window_mask(first)
            for sl in range(WIDTH // LANES):
                cq = pl.ds(sl * LANES, LANES)
                ck = pl.ds(WIDTH + sl * LANES, LANES)
                cv = pl.ds(2 * WIDTH + sl * LANES, LANES)
                qr = (_rope(qkv_ref[:, cq], cos_b, sin_b, lo_half) * (HEAD_DIM ** -0.5)).astype(BF16)
                kr = _rope(qkv_ref[:, ck], cos_b, sin_b, lo_half).astype(BF16)
                v16 = qkv_ref[:, cv].astype(BF16)
                kcat = jnp.concatenate([kp_ref[:, cq], kr], axis=0)
                vcat = jnp.concatenate([vp_ref[:, cq], v16], axis=0)
                do_b = do_ref[:, cq]
                do16 = do_b.astype(BF16)
                lse_b = lse_ref[:, cq]
                cterm = dlse_ref[:, cq] - do_b * o_ref[:, cq]
                dqs, dkc, dvc = [], None, None
                for hm in (head0, jnp.logical_not(head0)):
                    qm = jnp.where(hm, qr, jnp.zeros_like(qr))
                    dom = jnp.where(hm, do16, jnp.zeros_like(do16))
                    sc = jnp.where(valid[:ATT_BLOCK], _dg(qm, kcat, 1, 1), -jnp.inf)
                    lse_h = jnp.max(jnp.where(hm, lse_b, -jnp.inf), axis=1, keepdims=True)
                    c = jnp.sum(jnp.where(hm, cterm, 0.0), axis=1, keepdims=True)
                    p = jnp.exp(sc - lse_h)
                    ds16 = (p * (_dg(dom, vcat, 1, 1) + c)).astype(BF16)
                    dv_h, dk_h = _dg(p.astype(BF16), dom, 0, 0), _dg(ds16, qm, 0, 0)
                    dvc = dv_h if dvc is None else dvc + dv_h
                    dkc = dk_h if dkc is None else dkc + dk_h
                    dqs.append(_dg(ds16, kcat, 1, 0))
                dq = jnp.where(head0, dqs[0], dqs[1]) * (HEAD_DIM ** -0.5)
                dqp_ref[:, cq] = _rope_bwd(dq, cos_b, sin_b, lo_half).astype(BF16)
                dqkv_ref[:, ck] = _rope_bwd(dka_ref[:, cq] + dkc[:ATT_BLOCK], cos_p, sin_p, lo_half).astype(BF16)
                dqkv_ref[:, cv] = (dva_ref[:, cq] + dvc[:ATT_BLOCK]).astype(BF16)
                dka_ref[:, cq] = dkc[ATT_BLOCK:]
                dva_ref[:, cq] = dvc[ATT_BLOCK:]
                kp_ref[:, cq] = kr
                vp_ref[:, cq] = v16

        @pl.when(jnp.logical_not(active))
        def _():
            for sl in range(WIDTH // LANES):
                cq = pl.ds(sl * LANES, LANES)
                dqkv_ref[:, pl.ds(WIDTH + sl * LANES, LANES)] = _rope_bwd(dka_ref[:, cq], cos_p, sin_p, lo_half).astype(BF16)
                dqkv_ref[:, pl.ds(2 * WIDTH + sl * LANES, LANES)] = dva_ref[:, cq].astype(BF16)

    cur = lambda w: pl.BlockSpec((None, ATT_BLOCK, w), lambda g, t: (g, jnp.minimum(t, nb - 1), 0))
    prev = lambda w: pl.BlockSpec((None, ATT_BLOCK, w), lambda g, t: (g, jnp.maximum(t - 1, 0), 0))
    return pl.pallas_call(
        body, name="dilated_attention_bwd", grid=(GROUPS, nb + 1),
        in_specs=[cur(3 * WIDTH), cur(LANES), cur(LANES), prev(LANES), prev(LANES),
                  cur(WIDTH), cur(WIDTH), cur(WIDTH), cur(WIDTH)],
        out_specs=prev(3 * WIDTH), out_shape=jax.ShapeDtypeStruct((GROUPS, s, 3 * WIDTH), BF16),
        scratch_shapes=[pltpu.VMEM((ATT_BLOCK, WIDTH), BF16), pltpu.VMEM((ATT_BLOCK, WIDTH), BF16),
                        pltpu.VMEM((ATT_BLOCK, WIDTH), F32), pltpu.VMEM((ATT_BLOCK, WIDTH), F32),
                        pltpu.VMEM((ATT_BLOCK, WIDTH), BF16)],
        compiler_params=_params("arbitrary", "arbitrary"),
    )(qkv, cos, sin, cos, sin, o, lse, do, dlse)


CONV_PAD = SUBLANES


def _gdn_post(y, is_q, is_k):
    head0, _ = _lane_masks(y.shape[0])
    c = _silu(y)
    sq = c * c
    ss0 = jnp.sum(jnp.where(head0, sq, 0.0), axis=1, keepdims=True)
    ss1 = jnp.sum(jnp.where(head0, 0.0, sq), axis=1, keepdims=True)
    r = jnp.where(head0, lax.rsqrt(ss0 + NORM_EPS), lax.rsqrt(ss1 + NORM_EPS))
    scale = jnp.where(is_q, HEAD_DIM ** -0.5, 1.0).astype(F32)
    return jnp.where(is_q | is_k, c * r * scale, c)


def _conv_rows(xp_ref, w, c0, rows):
    y = w[0:1, :] * xp_ref[pl.ds(c0 + CONV_PAD - (CONV_K - 1), rows), :]
    for k in range(1, CONV_K):
        y = y + w[k:k + 1, :] * xp_ref[pl.ds(c0 + CONV_PAD - (CONV_K - 1) + k, rows), :]
    return y


def _gdn_pre_fwd(proj_r, conv8, col0):
    s = proj_r.shape[0]
    tr = _row_block(s, 512)
    nblk = QKV_B // LANES
    nq = WIDTH // LANES

    def body(x_ref, w_ref, out_ref, xp_ref):
        j = pl.program_id(0)
        is_q, is_k = j < nq, (j >= nq) & (j < 2 * nq)
        xp_ref[pl.ds(0, CONV_PAD), :] = jnp.zeros((CONV_PAD, LANES), F32)
        xp_ref[pl.ds(CONV_PAD, s), :] = x_ref[...]
        w = w_ref[...]
        for c in range(s // tr):
            out_ref[pl.ds(c * tr, tr), :] = _gdn_post(_conv_rows(xp_ref, w, c * tr, tr), is_q, is_k)

    return pl.pallas_call(
        body, name="gdn_conv_fwd", grid=(nblk,),
        in_specs=[pl.BlockSpec((s, LANES), lambda j: (0, col0 + j)), pl.BlockSpec((SUBLANES, LANES), lambda j: (0, j))],
        out_specs=pl.BlockSpec((s, LANES), lambda j: (0, j)),
        out_shape=jax.ShapeDtypeStruct((s, QKV_B), F32),
        scratch_shapes=[pltpu.VMEM((s + CONV_PAD, LANES), F32)],
        compiler_params=_params("parallel"),
    )(proj_r, conv8)


def _gdn_pre_bwd(proj_r, conv8, dc, col0):
    s = proj_r.shape[0]
    tr = _row_block(s, 512)
    nblk = QKV_B // LANES
    nq = WIDTH // LANES

    def body(x_ref, w_ref, dc_ref, dx_ref, dw_ref, xp_ref, dyp_ref):
        j = pl.program_id(0)
        is_q, is_k = j < nq, (j >= nq) & (j < 2 * nq)
        xp_ref[pl.ds(0, CONV_PAD), :] = jnp.zeros((CONV_PAD, LANES), F32)
        xp_ref[pl.ds(CONV_PAD, s), :] = x_ref[...]
        dyp_ref[pl.ds(s, CONV_PAD), :] = jnp.zeros((CONV_PAD, LANES), F32)
        w = w_ref[...]
        for c in range(s // tr):
            y = _conv_rows(xp_ref, w, c * tr, tr)
            _, vjp = jax.vjp(lambda yy: _gdn_post(yy, is_q, is_k), y)
            dyp_ref[pl.ds(c * tr, tr), :] = vjp(dc_ref[pl.ds(c * tr, tr), :])[0]
        dws = [jnp.zeros((1, LANES), F32) for _ in range(CONV_K)]
        for c in range(s // tr):
            c0 = c * tr
            dy = dyp_ref[pl.ds(c0, tr), :]
            dx = w[0:1, :] * dyp_ref[pl.ds(c0 + CONV_K - 1, tr), :]
            for k in range(1, CONV_K):
                dx = dx + w[k:k + 1, :] * dyp_ref[pl.ds(c0 + CONV_K - 1 - k, tr), :]
            dx_ref[pl.ds(c0, tr), :] = dx.astype(BF16)
            for k in range(CONV_K):
                xs = xp_ref[pl.ds(c0 + CONV_PAD - (CONV_K - 1) + k, tr), :]
                dws[k] = dws[k] + jnp.sum(dy * xs, axis=0, keepdims=True)
        row = lax.broadcasted_iota(jnp.int32, (SUBLANES, LANES), 0)
        dwb = jnp.zeros((SUBLANES, LANES), F32)
        for k in range(CONV_K):
            dwb = dwb + jnp.where(row == k, dws[k], 0.0)
        dw_ref[...] = dwb

    return pl.pallas_call(
        body, name="gdn_conv_bwd", grid=(nblk,),
        in_specs=[pl.BlockSpec((s, LANES), lambda j: (0, col0 + j)), pl.BlockSpec((SUBLANES, LANES), lambda j: (0, j)),
                  pl.BlockSpec((s, LANES), lambda j: (0, j))],
        out_specs=(pl.BlockSpec((s, LANES), lambda j: (0, j)), pl.BlockSpec((SUBLANES, LANES), lambda j: (0, j))),
        out_shape=(jax.ShapeDtypeStruct((s, QKV_B), BF16), jax.ShapeDtypeStruct((SUBLANES, QKV_B), F32)),
        scratch_shapes=[pltpu.VMEM((s + CONV_PAD, LANES), F32), pltpu.VMEM((s + CONV_PAD, LANES), F32)],
        compiler_params=_params("parallel"),
    )(proj_r, conv8, dc)


def _gdn_chunk(q, k, v, bcol, acol, alog, dtb, gnw, state, t_saved=None):
    n = q.shape[-2]
    shp = (1, n, n)
    row = lax.broadcasted_iota(jnp.int32, shp, 1)
    col = lax.broadcasted_iota(jnp.int32, shp, 2)
    beta = _sigmoid(bcol)
    g = -jnp.exp(alog) * _softplus(acol + dtb)
    g_row = jnp.sum(jnp.where(row == col, g, 0.0), axis=-2, keepdims=True)
    big_g = jnp.sum(jnp.where(row >= col, g_row, 0.0), axis=-1, keepdims=True)
    big_g_row = jnp.sum(jnp.where(row <= col, g, 0.0), axis=-2, keepdims=True)
    decay_incl = jnp.exp(jnp.where(row >= col, big_g - big_g_row, -jnp.inf))
    decay_strict = jnp.where(row > col, decay_incl, 0.0)
    k_beta = k * beta
    a_mat = _mm_nt(k_beta, k) * decay_strict
    t_inv = _tri_inv(a_mat) if t_saved is None else _tri_inv_saved(a_mat, t_saved)
    e_g = jnp.exp(big_g)
    u = _mm(t_inv, v * beta)
    w = _mm(t_inv, k_beta * e_g)
    attn = _mm_nt(q, k) * decay_incl
    v_new = u - _mm(w, state)
    o = _mm(q * e_g, state) + _mm(attn, v_new)
    total = jnp.sum(g, axis=-2, keepdims=True)
    new_state = state * jnp.exp(total) + _mm_tn(k * jnp.exp(total - big_g), v_new)
    return _rmsnorm(o, gnw), new_state, t_inv


def _split_heads(x):
    return jnp.stack([x[:, h * HEAD_DIM:(h + 1) * HEAD_DIM] for h in range(HEADS)], axis=0)


def _merge_heads(x):
    return jnp.concatenate([x[h] for h in range(HEADS)], axis=1)


def _logit_columns(ba):
    lane = lax.broadcasted_iota(jnp.int32, ba.shape, 1)

    def cols(off):
        return jnp.stack([jnp.sum(jnp.where(lane == off + h, ba, 0.0), axis=1, keepdims=True) for h in range(HEADS)], axis=0)

    return cols(0), cols(HEADS)


def _logit_block(dbc, dac, shape):
    lane = lax.broadcasted_iota(jnp.int32, shape, 1)
    out = jnp.zeros(shape, F32)
    for h in range(HEADS):
        out = out + jnp.where(lane == h, dbc[h], 0.0) + jnp.where(lane == HEADS + h, dac[h], 0.0)
    return out


def _gdn_scan_fwd(cqkv, proj_r, ba_col, alog, dtb, gnw):
    s = cqkv.shape[0]
    nc = s // CHUNK
    span = SCAN_CHUNKS * CHUNK

    def body(q_ref, k_ref, v_ref, ba_ref, al_ref, dt_ref, gnw_ref, o_ref, st_ref, ti_ref, state_ref):
        @pl.when(pl.program_id(0) == 0)
        def _():
            state_ref[...] = jnp.zeros_like(state_ref)

        st = state_ref[...]
        for u in range(SCAN_CHUNKS):
            rows = pl.ds(u * CHUNK, CHUNK)
            st_ref[u] = st
            bcol, acol = _logit_columns(ba_ref[rows, :])
            o, st, t_inv = _gdn_chunk(_split_heads(q_ref[rows, :]), _split_heads(k_ref[rows, :]),
                                      _split_heads(v_ref[rows, :]), bcol, acol, al_ref[...], dt_ref[...], gnw_ref[...], st)
            o_ref[rows, :] = _merge_heads(o)
            ti_ref[u] = t_inv
        state_ref[...] = st

    part = lambda i: pl.BlockSpec((span, WIDTH), lambda n: (n, i))
    par = pl.BlockSpec((HEADS, 1, 1), lambda n: (0, 0, 0))
    per_chunk = pl.BlockSpec((SCAN_CHUNKS, HEADS, HEAD_DIM, HEAD_DIM), lambda n: (n, 0, 0, 0))
    per_chunk_shape = jax.ShapeDtypeStruct((nc, HEADS, HEAD_DIM, HEAD_DIM), F32)
    return pl.pallas_call(
        body, name="gdn_scan_fwd", grid=(nc // SCAN_CHUNKS,),
        in_specs=[part(0), part(1), part(2), pl.BlockSpec((span, LANES), lambda n: (n, ba_col)), par, par,
                  pl.BlockSpec((1, 1, HEAD_DIM), lambda n: (0, 0, 0))],
        out_specs=(part(0), per_chunk, per_chunk),
        out_shape=(jax.ShapeDtypeStruct((s, WIDTH), F32), per_chunk_shape, per_chunk_shape),
        scratch_shapes=[pltpu.VMEM((HEADS, HEAD_DIM, HEAD_DIM), F32)],
        compiler_params=_params("arbitrary"),
    )(cqkv, cqkv, cqkv, proj_r, alog, dtb, gnw)


def _gdn_scan_bwd(cqkv, proj_r, ba_col, alog, dtb, gnw, states, t_invs, do):
    s = cqkv.shape[0]
    nc = s // CHUNK
    span = SCAN_CHUNKS * CHUNK
    n_steps = nc // SCAN_CHUNKS

    def body(q_ref, k_ref, v_ref, ba_ref, al_ref, dt_ref, gnw_ref, st_ref, ti_ref, do_ref,
             dqkv_ref, dba_ref, dal_ref, ddt_ref, dgnw_ref, dstate_ref):
        @pl.when(pl.program_id(0) == 0)
        def _():
            dstate_ref[...] = jnp.zeros_like(dstate_ref)
            dal_ref[...] = jnp.zeros_like(dal_ref)
            ddt_ref[...] = jnp.zeros_like(ddt_ref)
            dgnw_ref[...] = jnp.zeros_like(dgnw_ref)

        dst = dstate_ref[...]
        for u in reversed(range(SCAN_CHUNKS)):
            rows = pl.ds(u * CHUNK, CHUNK)
            bcol, acol = _logit_columns(ba_ref[rows, :])
            _, vjp = jax.vjp(lambda *a, t_saved=ti_ref[u]: _gdn_chunk(*a, t_saved=t_saved)[:2],
                             _split_heads(q_ref[rows, :]), _split_heads(k_ref[rows, :]), _split_heads(v_ref[rows, :]),
                             bcol, acol, al_ref[...], dt_ref[...], gnw_ref[...], st_ref[u])
            dq, dk, dv, dbc, dac, dal, ddt, dgn, dst = vjp((_split_heads(do_ref[rows, :]), dst))
            dqkv_ref[rows, pl.ds(0, WIDTH)] = _merge_heads(dq)
            dqkv_ref[rows, pl.ds(WIDTH, WIDTH)] = _merge_heads(dk)
            dqkv_ref[rows, pl.ds(2 * WIDTH, WIDTH)] = _merge_heads(dv)
            dba_ref[rows, :] = _logit_block(dbc, dac, (CHUNK, LANES))
            dal_ref[...] += dal
            ddt_ref[...] += ddt
            dgnw_ref[...] += dgn
        dstate_ref[...] = dst

    rev = lambda n: n_steps - 1 - n
    part = lambda i: pl.BlockSpec((span, WIDTH), lambda n: (rev(n), i))
    par = pl.BlockSpec((HEADS, 1, 1), lambda n: (0, 0, 0))
    vec = pl.BlockSpec((1, 1, HEAD_DIM), lambda n: (0, 0, 0))
    par_shape = jax.ShapeDtypeStruct((HEADS, 1, 1), F32)
    per_chunk = pl.BlockSpec((SCAN_CHUNKS, HEADS, HEAD_DIM, HEAD_DIM), lambda n: (rev(n), 0, 0, 0))
    return pl.pallas_call(
        body, name="gdn_scan_bwd", grid=(n_steps,),
        in_specs=[part(0), part(1), part(2), pl.BlockSpec((span, LANES), lambda n: (rev(n), ba_col)), par, par, vec,
                  per_chunk, per_chunk, part(0)],
        out_specs=(pl.BlockSpec((span, QKV_B), lambda n: (rev(n), 0)), pl.BlockSpec((span, LANES), lambda n: (rev(n), 0)),
                   par, par, vec),
        out_shape=(jax.ShapeDtypeStruct((s, QKV_B), F32), jax.ShapeDtypeStruct((s, LANES), F32), par_shape, par_shape,
                   jax.ShapeDtypeStruct((1, 1, HEAD_DIM), F32)),
        scratch_shapes=[pltpu.VMEM((HEADS, HEAD_DIM, HEAD_DIM), F32)],
        compiler_params=_params("arbitrary"),
    )(cqkv, cqkv, cqkv, proj_r, alog, dtb, gnw, states, t_invs, do)


def _tail_loss(x, tgt, o0, o1, o2, l0, l1, l2, ga, gb, za, zb, ob, fnw, wua, wub, wo, tap_a, tap_b, tap_o):
    lm = jnp.maximum(jnp.maximum(l0, l1), l2)
    e0, e1, e2 = jnp.exp(l0 - lm), jnp.exp(l1 - lm), jnp.exp(l2 - lm)
    o_a = (e0 * o0 + e1 * o1 + e2 * o2) / (e0 + e1 + e2)
    xa, xb = o_a * _silu(za), ob * _silu(zb)
    y_a = _mm_x(xa, wua) + tap_a
    y_b = _mm_x(xb, wub) + tap_b
    merged = _sigmoid(ga) * y_a + _sigmoid(gb) * y_b
    y = _rmsnorm(x + _mm_x(merged, wo) + tap_o, fnw)
    err = y - tgt
    per_token = jnp.sum(err * err, axis=1, keepdims=True) * (0.5 / x.shape[1])
    return jnp.sum(per_token, axis=0, keepdims=True), (xa.astype(BF16), xb.astype(BF16), merged.astype(BF16))


def _tail(x, tgt, o_all, lse_all, og12, lg12, proj_r, ob, wua, wub, wo, fnw):
    s, d = x.shape
    tm = _row_block(s, TAIL_ROWS)
    col_za = 2 * d // WIDTH
    col_zb = (2 * d + WIDTH + QKV_B) // WIDTH

    def body(x_ref, t_ref, o0_ref, o1_ref, o2_ref, l0_ref, l1_ref, l2_ref, ga_ref, gb_ref, za_ref, zb_ref, ob_ref,
             wua_ref, wub_ref, wo_ref, fnw_ref,
             loss_ref, dx_ref, do0_ref, do1_ref, do2_ref, dl0_ref, dl1_ref, dl2_ref, dga_ref, dgb_ref, dza_ref,
             dzb_ref, dob_ref, xa_ref, xb_ref, mg_ref, dya_ref, dyb_ref, dmo_ref, dfnw_ref):
        @pl.when(pl.program_id(0) == 0)
        def _():
            for r in (loss_ref, dfnw_ref):
                r[...] = jnp.zeros_like(r)

        tap = jnp.zeros((tm, d), F32)
        args = (x_ref[...], t_ref[...], o0_ref[...], o1_ref[...], o2_ref[...], l0_ref[...], l1_ref[...], l2_ref[...],
                ga_ref[...], gb_ref[...], za_ref[...], zb_ref[...], ob_ref[...], fnw_ref[...],
                wua_ref[...], wub_ref[...], wo_ref[...], tap, tap, tap)
        loss, vjp, (xa16, xb16, mg16) = jax.vjp(_tail_loss, *args, has_aux=True)
        (dx, _, do0, do1, do2, dl0, dl1, dl2, dga, dgb, dza, dzb, dob, dfnw, _, _, _, dya, dyb, dmo) = vjp(
            jnp.ones((1, 1), F32))
        loss_ref[...] += jnp.broadcast_to(loss, loss_ref.shape)
        dx_ref[...] = dx
        do0_ref[...], do1_ref[...], do2_ref[...] = do0, do1, do2
        dl0_ref[...], dl1_ref[...], dl2_ref[...] = dl0, dl1, dl2
        dga_ref[...] = dga.astype(BF16)
        dgb_ref[...] = dgb.astype(BF16)
        dza_ref[...] = dza.astype(BF16)
        dzb_ref[...] = dzb.astype(BF16)
        dob_ref[...] = dob
        xa_ref[...], xb_ref[...], mg_ref[...] = xa16, xb16, mg16
        dya_ref[...] = dya.astype(BF16)
        dyb_ref[...] = dyb.astype(BF16)
        dmo_ref[...] = dmo.astype(BF16)
        dfnw_ref[...] += dfnw

    row = lambda w, c=0: pl.BlockSpec((tm, w), lambda i: (i, c))
    grp0 = pl.BlockSpec((None, tm, WIDTH), lambda i: (0, i, 0))
    full = lambda a, b: pl.BlockSpec((a, b), lambda i: (0, 0))
    f32 = lambda a, b: jax.ShapeDtypeStruct((a, b), F32)
    b16 = lambda a, b: jax.ShapeDtypeStruct((a, b), BF16)
    stacked = jax.ShapeDtypeStruct((GROUPS, s, WIDTH), F32)
    gspecs = [grp0, row(WIDTH), row(WIDTH)]
    in_specs = ([row(d), row(d)] + gspecs * 2 + [row(d, 0), row(d, 1), row(WIDTH, col_za), row(WIDTH, col_zb),
                row(WIDTH), full(WIDTH, d), full(WIDTH, d), full(d, d), full(1, d)])
    out_specs = ([full(SUBLANES, LANES), row(d)] + gspecs * 2 + [row(d), row(d), row(WIDTH), row(WIDTH), row(WIDTH),
                 row(WIDTH), row(WIDTH), row(d), row(d), row(d), row(d), full(1, d)])
    gshapes = [stacked, f32(s, WIDTH), f32(s, WIDTH)]
    out_shape = ([f32(SUBLANES, LANES), f32(s, d)] + gshapes * 2 + [b16(s, d), b16(s, d), b16(s, WIDTH),
                 b16(s, WIDTH), f32(s, WIDTH), b16(s, WIDTH), b16(s, WIDTH), b16(s, d), b16(s, d), b16(s, d), b16(s, d),
                 f32(1, d)])
    return pl.pallas_call(
        body, name="tail_fwd_bwd", grid=(s // tm,),
        in_specs=in_specs, out_specs=tuple(out_specs), out_shape=tuple(out_shape),
        compiler_params=pltpu.CompilerParams(dimension_semantics=("arbitrary",), vmem_limit_bytes=TAIL_VMEM_LIMIT),
    )(x, tgt, o_all, og12[0], og12[1], lse_all, lg12[0], lg12[1], proj_r, proj_r, proj_r, proj_r, ob, wua, wub, wo, fnw)


PERMUTE_SPAN = 4096


def _permute_span(s):
    return PERMUTE_SPAN if s % PERMUTE_SPAN == 0 else s


def _from_dilated_rows(stacked, g, dil, name):
    n_slots, s, c = stacked.shape
    view = stacked.reshape(n_slots, dil, s // dil, c)
    span = _permute_span(s)

    def body(in_ref, out_ref):
        for r in range(dil):
            out_ref[pl.ds(r, span // dil, stride=dil), :] = in_ref[r]

    return pl.pallas_call(
        body, name=name, grid=(s // span, c // LANES),
        in_specs=[pl.BlockSpec((None, dil, span // dil, LANES), lambda n, j: (g, 0, n, j))],
        out_specs=pl.BlockSpec((span, LANES), lambda n, j: (n, j)),
        out_shape=jax.ShapeDtypeStruct((s, c), stacked.dtype),
        compiler_params=_params("parallel", "parallel"),
    )(view)


def _to_dilated_rows_into(nat, stacked, g, dil, name):
    n_slots, s, c = stacked.shape
    view = stacked.reshape(n_slots, dil, s // dil, c)
    span = _permute_span(s)

    def body(nat_ref, old_ref, out_ref):
        for r in range(dil):
            out_ref[r] = nat_ref[pl.ds(r, span // dil, stride=dil), :]

    out = pl.pallas_call(
        body, name=name, grid=(s // span, c // LANES),
        in_specs=[pl.BlockSpec((span, LANES), lambda n, j: (n, j)), pl.BlockSpec(memory_space=pl.ANY)],
        out_specs=pl.BlockSpec((None, dil, span // dil, LANES), lambda n, j: (g, 0, n, j)),
        out_shape=jax.ShapeDtypeStruct(view.shape, stacked.dtype),
        input_output_aliases={1: 0},
        compiler_params=_params("parallel", "parallel"),
    )(nat, view)
    return out.reshape(stacked.shape)


def _to_dilated(a, dil):
    if dil == 1:
        return a
    s = a.shape[0]
    return a.reshape(s // dil, dil, -1).transpose(1, 0, 2).reshape(a.shape)


def _from_dilated(a, dil):
    if dil == 1:
        return a
    s = a.shape[0]
    return a.reshape(dil, s // dil, -1).transpose(1, 0, 2).reshape(a.shape)


def _head_major(a):
    return a.reshape(a.shape[0], HEADS, HEAD_DIM).transpose(1, 0, 2)


def _from_head_major(a):
    return a.transpose(1, 0, 2).reshape(a.shape[1], WIDTH)


def _rope_tables(s):
    inv_freq = ROPE_THETA ** (-jnp.arange(0, HEAD_DIM, 2, dtype=F32) / HEAD_DIM)
    ang = jnp.arange(s, dtype=F32)[:, None] * inv_freq[None, :]
    cos_n = jnp.tile(jnp.cos(ang), (1, 2 * LANES // HEAD_DIM))
    sin_h = jnp.sin(ang)
    sin_n = jnp.tile(jnp.concatenate([-sin_h, sin_h], axis=1), (1, LANES // HEAD_DIM))
    def per_group(table, tag):
        out = jnp.broadcast_to(table, (GROUPS,) + table.shape)
        for g in range(1, GROUPS):
            out = _to_dilated_rows_into(table, out, g, DILATIONS[g], "rope_%s_to_dilated_%d" % (tag, g))
        return out

    return per_group(cos_n, "cos"), per_group(sin_n, "sin")


def _regroup_columns(pieces, widths):
    starts, pos = [], 0
    for p in pieces:
        starts.append(pos)
        pos += p.shape[1]
    assert pos == sum(widths), (pos, widths)
    out, lo = [], 0
    for w in widths:
        hi, parts = lo + w, []
        for p, st in zip(pieces, starts):
            a, b = max(lo, st), min(hi, st + p.shape[1])
            if a < b:
                parts.append(p[:, a - st:b - st])
        out.append(parts[0] if len(parts) == 1 else jnp.concatenate(parts, axis=1))
        lo = hi
    return out


def _pack_rows(parts, dtype, row_multiple):
    flat = jnp.concatenate([p.reshape(-1).astype(dtype) for p in parts])
    tile = row_multiple * LANES
    pad = (-flat.shape[0]) % tile
    return jnp.pad(flat, (0, pad)).reshape(-1, LANES)


def _unpack_rows(packed, shapes):
    flat = packed.reshape(-1)
    out, start = [], 0
    for shp in shapes:
        size = 1
        for n in shp:
            size *= n
        out.append(flat[start:start + size].reshape(shp))
        start += size
    return out


def kernel(x, norm_w, w_in, conv_w, a_log, dt_bias, gdn_norm_w, w_up_a, w_up_b, w_out, final_norm_w, loss_target, m_norm_w, m_w_in, m_conv_w, m_a_log, m_dt_bias, m_gdn_norm_w, m_w_up_a, m_w_up_b, m_w_out, m_final_norm_w, v_norm_w, v_w_in, v_conv_w, v_a_log, v_dt_bias, v_gdn_norm_w, v_w_up_a, v_w_up_b, v_w_out, v_final_norm_w):
    x2, tgt = x[0], loss_target[0]
    s, d = x2.shape
    me = 4 * lax.axis_index("x") + 2 * lax.axis_index("y") + lax.axis_index("c")
    win8 = w_in.shape[2]
    conv8w = conv_w.shape[2]

    conv_shard = jnp.pad(conv_w[0], ((0, SUBLANES - CONV_K), (0, 0)))
    w_in_g, wua_g, wub_g, wo_g, conv_g = _all_gather(
        [w_in[0].astype(BF16), w_up_a[0].astype(BF16), w_up_b[0].astype(BF16), w_out[0].astype(BF16), conv_shard])
    wua = jnp.concatenate([wua_g[i] for i in range(N_DEV)], axis=1)
    wub = jnp.concatenate([wub_g[i] for i in range(N_DEV)], axis=1)
    wo = wo_g.reshape(d, d)
    conv8 = jnp.concatenate([conv_g[i] for i in range(N_DEV)], axis=1)

    seg_widths = [QKV_B] * GROUPS + [WIDTH, QKV_B, WIDTH, 2 * HEADS, 2 * d]
    wq0, wq1, wq2, w_za, w_qkvb, w_zb, w_ba, w_gates = _regroup_columns([w_in_g[i] for i in range(N_DEV)], seg_widths)
    w_qkv = jnp.stack([wq0, wq1, wq2])
    w_rest = jnp.concatenate([w_gates, w_za, w_qkvb, w_zb, w_ba,
                              jnp.zeros((d, BA_PAD - 2 * HEADS), BF16)], axis=1)
    col_qkvb = (2 * d + WIDTH) // LANES
    col_ba = (2 * d + 2 * WIDTH + QKV_B) // LANES

    h = _rms_fwd(x2, norm_w)
    h_all = jnp.stack([_to_dilated(h, dil) for dil in DILATIONS])
    qkv_all = _matmul(h_all, w_qkv, F32, "in_proj_attention", tn=QKV_B)
    proj_r = _matmul(h[None], w_rest[None], F32, "in_proj_rest", tn=2560)[0]
    cos, sin = _rope_tables(s)
    o_all, lse_all = _attn_fwd(qkv_all, cos, sin)
    og12 = [_from_dilated_rows(o_all, g, DILATIONS[g], "attn_out_to_natural_%d" % g) for g in (1, 2)]
    lg12 = [_from_dilated_rows(lse_all, g, DILATIONS[g], "attn_lse_to_natural_%d" % g) for g in (1, 2)]

    cqkv = _gdn_pre_fwd(proj_r, conv8, col_qkvb)
    alog3, dtb3, gnw3 = a_log.reshape(HEADS, 1, 1), dt_bias.reshape(HEADS, 1, 1), gdn_norm_w.reshape(1, 1, HEAD_DIM)
    ob, states, t_invs = _gdn_scan_fwd(cqkv, proj_r, col_ba, alog3, dtb3, gnw3)

    (loss_blk, dx_res, do_all, do1, do2, dl_all, dl1, dl2, dga, dgb, dza, dzb, dob,
     xa16, xb16, mg16, dya16, dyb16, dmo16, dfnw) = _tail(
        x2, tgt, o_all, lse_all, og12, lg12, proj_r, ob, wua, wub, wo, final_norm_w.reshape(1, d))
    dwua = _matmul(xa16[None], dya16[None], F32, "up_a_dw", mode="tn", tk=2048)[0]
    dwub = _matmul(xb16[None], dyb16[None], F32, "up_b_dw", mode="tn", tk=2048)[0]
    dwo = _matmul(mg16[None], dmo16[None], F32, "out_proj_dw", mode="tn", tk=2048)[0]

    for g, (t_o, t_l) in ((1, (do1, dl1)), (2, (do2, dl2))):
        do_all = _to_dilated_rows_into(t_o, do_all, g, DILATIONS[g], "attn_dout_to_dilated_%d" % g)
        dl_all = _to_dilated_rows_into(t_l, dl_all, g, DILATIONS[g], "attn_dlse_to_dilated_%d" % g)
    dqkv_all = _attn_bwd(qkv_all, cos, sin, o_all, lse_all, do_all, dl_all)

    dcqkv, dba, dalog3, ddtb3, dgnw3 = _gdn_scan_bwd(cqkv, proj_r, col_ba, alog3, dtb3, gnw3, states, t_invs, dob)
    dqkv_b, dconv8 = _gdn_pre_bwd(proj_r, conv8, dcqkv, col_qkvb)
    dproj_r = jnp.concatenate([dga, dgb, dza, dqkv_b, dzb,
                               jnp.pad(dba.astype(BF16), ((0, 0), (0, BA_PAD - LANES)))], axis=1)

    def col_slabs(a, width):
        return jnp.stack([a[:, j * width:(j + 1) * width] for j in range(N_DEV)])

    core = lax.axis_index("c").astype(jnp.int32).reshape(1)
    small_slabs = [col_slabs(dwua, d // N_DEV), col_slabs(dwub, d // N_DEV), dwo.reshape(N_DEV, d // N_DEV, d)]
    dw_qkv, *small_sib = _matmul(h_all, dqkv_all, F32, "in_proj_attention_dw", mode="tn", tk=2048,
                                 exchange=_sibling_exchange(small_slabs))
    small_partials = [_pair_sum(a, b, core, "grads_pair_sum_%d" % (i + 1))
                      for i, (a, b) in enumerate(zip(small_slabs, small_sib))]
    dw_rest, *small_contrib = _matmul(h[None], dproj_r[None], F32, "in_proj_rest_dw", mode="tn", tk=2048,
                                      exchange=_chip_exchange(small_partials))
    dw_rest = dw_rest[0]
    o2 = 2 * d
    dw_in_pieces = [dw_qkv[0], dw_qkv[1], dw_qkv[2],
                    dw_rest[:, o2:o2 + WIDTH], dw_rest[:, o2 + WIDTH:o2 + WIDTH + QKV_B],
                    dw_rest[:, o2 + WIDTH + QKV_B:o2 + 2 * WIDTH + QKV_B],
                    dw_rest[:, o2 + 2 * WIDTH + QKV_B:o2 + 2 * WIDTH + QKV_B + 2 * HEADS],
                    dw_rest[:, :o2]]

    w_in_slabs = jnp.stack(_regroup_columns(dw_in_pieces, [win8] * N_DEV))
    dh_a, w_in_sib = _matmul(dqkv_all, w_qkv, F32, "in_proj_attention_dh", mode="nt", tk=2048,
                             exchange=_sibling_exchange([w_in_slabs]))
    w_in_partial = _pair_sum(w_in_slabs, w_in_sib, core, "grads_pair_sum_0")
    dh_r, w_in_contrib = _matmul(dproj_r[None], w_rest[None], F32, "in_proj_rest_dh", mode="nt", tk=2560,
                                 exchange=_chip_exchange([w_in_partial]))
    contrib = [w_in_contrib] + small_contrib
    dh_parts = [dh_r[0]] + [_from_dilated_rows(dh_a, g, DILATIONS[g], "dh_to_natural_%d" % g) for g in (1, 2)]
    grad_x, dnorm_w = _rms_bwd(x2, norm_w, dh_a, dh_parts, dx_res)

    small_parts = [dnorm_w, dfnw, dconv8[:CONV_K], dalog3[:, 0, 0], ddtb3[:, 0, 0], dgnw3[0], loss_blk[0, 0:1]]
    small_rows = [-(-p.size // LANES) for p in small_parts]
    small = jnp.concatenate([jnp.pad(p.reshape(-1), (0, r * LANES - p.size)).reshape(r, LANES)
                             for p, r in zip(small_parts, small_rows)])
    small = jnp.pad(small, ((0, (-small.shape[0]) % SUBLANES), (0, 0)))
    small_sum = _small_all_reduce(small)
    pieces, r0 = [], 0
    for p, r in zip(small_parts, small_rows):
        pieces.append(small_sum[r0:r0 + r].reshape(-1)[:p.size].reshape(p.shape))
        r0 += r
    g_norm_w, g_fnw, g_conv_full, g_alog, g_dtb, g_gnw, loss_sum = pieces
    g_conv = lax.dynamic_slice(g_conv_full, (0, me * conv8w), (CONV_K, conv8w))

    big = [_adamw(c, w[0], m[0], v[0], name) for c, w, m, v, name in (
        (contrib[0], w_in, m_w_in, v_w_in, "adamw_w_in"), (contrib[1], w_up_a, m_w_up_a, v_w_up_a, "adamw_w_up_a"),
        (contrib[2], w_up_b, m_w_up_b, v_w_up_b, "adamw_w_up_b"), (contrib[3], w_out, m_w_out, v_w_out, "adamw_w_out"))]
    g_big, d_big, nm_big, nv_big = ([t[i] for t in big] for i in range(4))

    small_ws = [norm_w, final_norm_w, conv_w, a_log, dt_bias, gdn_norm_w]
    small_ms = [m_norm_w, m_final_norm_w, m_conv_w, m_a_log, m_dt_bias, m_gdn_norm_w]
    small_vs = [v_norm_w, v_final_norm_w, v_conv_w, v_a_log, v_dt_bias, v_gdn_norm_w]
    small_gs = [g_norm_w, g_fnw, g_conv, g_alog, g_dtb, g_gnw]
    small_shapes = [t.shape for t in small_ws]
    sm = _adamw(_pack_rows(small_gs, F32, SUBLANES)[None], _pack_rows(small_ws, F32, SUBLANES),
                _pack_rows(small_ms, F32, SUBLANES), _pack_rows(small_vs, F32, SUBLANES), "adamw_small")
    g_sm, d_sm, nm_sm, nv_sm = (_unpack_rows(t, small_shapes) for t in sm)

    def ordered(bigs, smalls):
        nw, fnw_, cw, al, dtb, gn = smalls
        wi, ua, ub, wo_ = (t[None] for t in bigs)
        return [nw, wi, cw, al, dtb, gn, ua, ub, wo_, fnw_]

    return (loss_sum.reshape(()), grad_x[None], *ordered(g_big, g_sm), *ordered(d_big, d_sm),
            *ordered(nm_big, nm_sm), *ordered(nv_big, nv_sm))
```

```python
import functools

import jax
import jax.numpy as jnp
from jax import lax
from jax.experimental import pallas as pl
from jax.experimental.pallas import tpu as pltpu

F32 = jnp.float32
BF16 = jnp.bfloat16
MESH = pl.DeviceIdType.MESH
N_DEV = 8
LANES = 128
SUBLANES = 8

GROUPS = 3
HEADS = 8
HEAD_DIM = 64
WIDTH = HEADS * HEAD_DIM
ATT_BLOCK = 128
DILATIONS = (1, 4, 16)
N_BACK = 128
CONV_K = 4
CHUNK = 64
SCAN_CHUNKS = 4
QKV_B = 3 * WIDTH
QKV_A = GROUPS * 3 * WIDTH
BA_PAD = 512
NORM_EPS = 1e-6
ROPE_THETA = 10000.0
ADAM_LR, ADAM_B1, ADAM_B2, ADAM_EPS, ADAM_WD, ADAM_STEP = 0.001, 0.9, 0.999, 1e-08, 0.01, 10

VMEM_LIMIT = 56 * 1024 * 1024
TAIL_ROWS = 256
TAIL_VMEM_LIMIT = 62 * 1024 * 1024


def _params(*sem):
    return pltpu.CompilerParams(dimension_semantics=sem, vmem_limit_bytes=VMEM_LIMIT)


def _dg(a, b, ca, cb):
    nb = a.ndim - 2
    batch = tuple(range(nb))
    return lax.dot_general(a, b, (((nb + ca,), (nb + cb,)), (batch, batch)), preferred_element_type=F32)


@jax.custom_vjp
def _mm(a, b):
    return _dg(a.astype(BF16), b.astype(BF16), 1, 0)


def _mm_fwd(a, b):
    return _mm(a, b), (a.astype(BF16), b.astype(BF16))


def _mm_bwd(res, ct):
    a16, b16 = res
    c16 = ct.astype(BF16)
    return _dg(c16, b16, 1, 1), _dg(a16, c16, 0, 0)


_mm.defvjp(_mm_fwd, _mm_bwd)


@jax.custom_vjp
def _mm_nt(a, b):
    return _dg(a.astype(BF16), b.astype(BF16), 1, 1)


def _mm_nt_fwd(a, b):
    return _mm_nt(a, b), (a.astype(BF16), b.astype(BF16))


def _mm_nt_bwd(res, ct):
    a16, b16 = res
    c16 = ct.astype(BF16)
    return _dg(c16, b16, 1, 0), _dg(c16, a16, 0, 0)


_mm_nt.defvjp(_mm_nt_fwd, _mm_nt_bwd)


@jax.custom_vjp
def _mm_tn(a, b):
    return _dg(a.astype(BF16), b.astype(BF16), 0, 0)


def _mm_tn_fwd(a, b):
    return _mm_tn(a, b), (a.astype(BF16), b.astype(BF16))


def _mm_tn_bwd(res, ct):
    a16, b16 = res
    c16 = ct.astype(BF16)
    return _dg(b16, c16, 1, 1), _dg(a16, c16, 1, 0)


_mm_tn.defvjp(_mm_tn_fwd, _mm_tn_bwd)


@jax.custom_vjp
def _mm_x(a, w16):
    return _dg(a.astype(BF16), w16, 1, 0)


def _mm_x_fwd(a, w16):
    return _mm_x(a, w16), w16


def _mm_x_bwd(w16, ct):
    return _dg(ct.astype(BF16), w16, 1, 1), jnp.zeros_like(w16)


_mm_x.defvjp(_mm_x_fwd, _mm_x_bwd)


def _split16(a):
    hi = a.astype(BF16)
    lo = (a - hi.astype(F32)).astype(BF16)
    return hi, lo


def _dot3(a, b, ca, cb):
    ah, al = _split16(a)
    bh, bl = _split16(b)
    return _dg(ah, bh, ca, cb) + (_dg(ah, bl, ca, cb) + _dg(al, bh, ca, cb))


def _tri_inv_impl(a):
    n = a.shape[-1]
    shp = (1,) * (a.ndim - 2) + (n, n)
    eye = (lax.broadcasted_iota(jnp.int32, shp, a.ndim - 2) == lax.broadcasted_iota(jnp.int32, shp, a.ndim - 1)).astype(F32)
    x = eye - a
    p = a
    for it in range(5):
        dot = _dot3 if it < 2 else (lambda u, v, cu, cv: _dg(u.astype(BF16), v.astype(BF16), cu, cv))
        p = dot(p, p, 1, 0)
        x = x + dot(x, p, 1, 0)
    return x


@jax.custom_vjp
def _tri_inv(a):
    return _tri_inv_impl(a)


def _tri_inv_fwd(a):
    t = _tri_inv_impl(a)
    return t, t


def _tri_inv_bwd(t, ct):
    t16 = t.astype(BF16)
    return (-_dg(_dg(t16, ct.astype(BF16), 0, 0).astype(BF16), t16, 1, 1),)


_tri_inv.defvjp(_tri_inv_fwd, _tri_inv_bwd)


@jax.custom_vjp
def _tri_inv_saved(a, t):
    return t


def _tri_inv_saved_fwd(a, t):
    return t, t


def _tri_inv_saved_bwd(t, ct):
    return _tri_inv_bwd(t, ct) + (jnp.zeros_like(t),)


_tri_inv_saved.defvjp(_tri_inv_saved_fwd, _tri_inv_saved_bwd)


def _sigmoid(x):
    return 1.0 / (1.0 + jnp.exp(-x))


def _silu(x):
    return x * _sigmoid(x)


def _softplus(x):
    return jnp.maximum(x, 0.0) + jnp.log(1.0 + jnp.exp(-jnp.abs(x)))


def _rmsnorm(x, w):
    return x * lax.rsqrt(jnp.mean(x * x, axis=-1, keepdims=True) + NORM_EPS) * w


def _row_block(rows, cap):
    best = None
    for cand in range(SUBLANES, min(rows, cap) + 1, SUBLANES):
        if rows % cand == 0:
            best = cand
    assert best is not None, rows
    return best


def _mesh_peers():
    x, y, c = lax.axis_index("x"), lax.axis_index("y"), lax.axis_index("c")
    me = 4 * x + 2 * y + c
    peers = []
    for k in range(1, N_DEV):
        px = 1 - x if (k >> 2) & 1 else x
        py = 1 - y if (k >> 1) & 1 else y
        pc = 1 - c if k & 1 else c
        peers.append(((px, py, pc), 4 * px + 2 * py + pc))
    return me, peers


N_CHIPS = 4
OTHER_CHIPS = 3


def _chip_peers():
    x, y, c = lax.axis_index("x"), lax.axis_index("y"), lax.axis_index("c")
    return x, y, c, [(1 - x, y), (x, 1 - y), (1 - x, 1 - y)]


def _all_gather(shards):
    n_arr = len(shards)
    per = 1 + 2 * OTHER_CHIPS

    def body(*refs):
        in_refs, out_refs = refs[:n_arr], refs[n_arr:2 * n_arr]
        send_sems, recv_sems, loc_sems = refs[2 * n_arr:]
        x, y, c, chips = _chip_peers()
        me, sibling = (x, y, c), (x, y, 1 - c)

        def slot(px, py, pc):
            return 4 * px + 2 * py + pc

        def copy(i, k, block, to, src=None):
            dst = out_refs[i].at[slot(*block)]
            return pltpu.make_async_remote_copy(src_ref=dst if src is None else src, dst_ref=dst,
                                                send_sem=send_sems.at[i * per + k], recv_sem=recv_sems.at[i * per + k],
                                                device_id=to, device_id_type=MESH)

        own = [pltpu.make_async_copy(in_refs[i], out_refs[i].at[slot(*me)], loc_sems.at[i]) for i in range(n_arr)]
        for cp in own:
            cp.start()
        first = []
        for i in range(n_arr):
            first += [copy(i, 1 + j, me, (*chip, c), src=in_refs[i]) for j, chip in enumerate(chips)]
            first.append(copy(i, 0, me, sibling, src=in_refs[i]))
        for cp in first:
            cp.start()
        passed = []
        for j, chip in enumerate(chips):
            for i in range(n_arr):
                copy(i, 1 + j, (*chip, c), me).wait_recv()
                fwd = copy(i, 1 + OTHER_CHIPS + j, (*chip, c), sibling)
                fwd.start()
                passed.append(fwd)
        for i in range(n_arr):
            copy(i, 0, sibling, me).wait_recv()
            for j, chip in enumerate(chips):
                copy(i, 1 + OTHER_CHIPS + j, (*chip, 1 - c), me).wait_recv()
        for cp in first + passed:
            cp.wait_send()
        for cp in own:
            cp.wait()

    any_spec = pl.BlockSpec(memory_space=pl.ANY)
    return pl.pallas_call(
        body, name="weights_all_gather",
        out_shape=tuple(jax.ShapeDtypeStruct((N_DEV,) + a.shape, a.dtype) for a in shards),
        in_specs=[any_spec] * n_arr, out_specs=tuple([any_spec] * n_arr),
        scratch_shapes=[pltpu.SemaphoreType.DMA((n_arr * per,)), pltpu.SemaphoreType.DMA((n_arr * per,)),
                        pltpu.SemaphoreType.DMA((n_arr,))],
    )(*shards)


class _Exchange:
    def __init__(self, arrays, out_shapes, n_sem, copies):
        self.arrays, self.out_shapes, self.n_sem, self.copies = list(arrays), list(out_shapes), n_sem, copies


def _sibling_exchange(slabs):
    n_arr = len(slabs)

    def copies(in_refs, out_refs, send_sems, recv_sems, loc_sems):
        x, y, c, _ = _chip_peers()
        sends = [pltpu.make_async_remote_copy(src_ref=in_refs[i].at[2 * q + (1 - c)], dst_ref=out_refs[i].at[q],
                                              send_sem=send_sems.at[i * N_CHIPS + q], recv_sem=recv_sems.at[i * N_CHIPS + q],
                                              device_id=(x, y, 1 - c), device_id_type=MESH)
                 for i in range(n_arr) for q in range(N_CHIPS)]

        def start():
            for cp in sends:
                cp.start()

        def finish():
            for cp in sends:
                cp.wait_recv()
            for cp in sends:
                cp.wait_send()

        return start, finish

    return _Exchange(slabs, [jax.ShapeDtypeStruct((N_CHIPS,) + a.shape[1:], a.dtype) for a in slabs],
                     n_arr * N_CHIPS, copies)


def _pair_sum(slabs, from_sibling, core, name):
    _, rows, cols = slabs.shape
    tr = _row_block(rows, max(SUBLANES, (256 * 1024) // cols // SUBLANES * SUBLANES))

    def body(core_ref, a_ref, b_ref, o_ref):
        o_ref[...] = (a_ref[...] + b_ref[...]).astype(BF16)

    grid_spec = pltpu.PrefetchScalarGridSpec(
        num_scalar_prefetch=1, grid=(N_CHIPS, rows // tr),
        in_specs=[pl.BlockSpec((None, tr, cols), lambda q, r, core_ref: (2 * q + core_ref[0], r, 0)),
                  pl.BlockSpec((None, tr, cols), lambda q, r, core_ref: (q, r, 0))],
        out_specs=pl.BlockSpec((None, tr, cols), lambda q, r, core_ref: (q, r, 0)))
    return pl.pallas_call(
        body, name=name, grid_spec=grid_spec,
        out_shape=jax.ShapeDtypeStruct((N_CHIPS, rows, cols), BF16),
        compiler_params=_params("parallel", "parallel"),
    )(core, slabs, from_sibling)


def _chip_exchange(partials):
    n_arr = len(partials)

    def copies(in_refs, out_refs, send_sems, recv_sems, loc_sems):
        x, y, c, chips = _chip_peers()
        mine = 2 * x + y
        own = [pltpu.make_async_copy(in_refs[i].at[mine], out_refs[i].at[mine], loc_sems.at[i]) for i in range(n_arr)]

        def copy(i, j, chip, src_slot, dst_slot):
            return pltpu.make_async_remote_copy(src_ref=in_refs[i].at[src_slot], dst_ref=out_refs[i].at[dst_slot],
                                                send_sem=send_sems.at[i * OTHER_CHIPS + j],
                                                recv_sem=recv_sems.at[i * OTHER_CHIPS + j],
                                                device_id=(*chip, c), device_id_type=MESH)

        sends = [copy(i, j, chip, 2 * chip[0] + chip[1], mine) for j, chip in enumerate(chips) for i in range(n_arr)]
        recvs = [copy(i, j, chip, mine, 2 * chip[0] + chip[1]) for j, chip in enumerate(chips) for i in range(n_arr)]

        def start():
            for cp in own + sends:
                cp.start()

        def finish():
            for cp in recvs:
                cp.wait_recv()
            for cp in sends:
                cp.wait_send()
            for cp in own:
                cp.wait()

        return start, finish

    return _Exchange(partials, [jax.ShapeDtypeStruct(a.shape, a.dtype) for a in partials], n_arr * OTHER_CHIPS, copies)


def _small_all_reduce(part):
    rows = part.shape[0]

    def body(p_ref, o_ref, buf_ref, send_sems, recv_sems):
        me, peers = _mesh_peers()
        buf_ref[me] = p_ref[...]
        sends = []
        for k, (dev, pid) in enumerate(peers):
            cp = pltpu.make_async_remote_copy(src_ref=p_ref, dst_ref=buf_ref.at[me], send_sem=send_sems.at[k],
                                              recv_sem=recv_sems.at[k], device_id=dev, device_id_type=MESH)
            cp.start()
            sends.append(cp)
        for k, (dev, pid) in enumerate(peers):
            pltpu.make_async_remote_copy(src_ref=p_ref, dst_ref=buf_ref.at[pid], send_sem=send_sems.at[k],
                                         recv_sem=recv_sems.at[k], device_id=dev, device_id_type=MESH).wait_recv()
        for cp in sends:
            cp.wait_send()
        acc = buf_ref[0]
        for i in range(1, N_DEV):
            acc = acc + buf_ref[i]
        o_ref[...] = acc

    vmem = pl.BlockSpec(memory_space=pltpu.VMEM)
    return pl.pallas_call(
        body, name="small_all_reduce",
        out_shape=jax.ShapeDtypeStruct(part.shape, F32),
        in_specs=[vmem], out_specs=vmem,
        scratch_shapes=[pltpu.VMEM((N_DEV, rows, LANES), F32), pltpu.SemaphoreType.DMA((N_DEV - 1,)),
                        pltpu.SemaphoreType.DMA((N_DEV - 1,))],
    )(part)


def _adamw_vals(w, g, m, v):
    m = ADAM_B1 * m + (1.0 - ADAM_B1) * g
    v = ADAM_B2 * v + (1.0 - ADAM_B2) * (g * g)
    m_hat = m / (1.0 - ADAM_B1 ** ADAM_STEP)
    v_hat = v / (1.0 - ADAM_B2 ** ADAM_STEP)
    delta = -ADAM_LR * (m_hat / (jnp.sqrt(v_hat) + ADAM_EPS) + ADAM_WD * w)
    return delta, m, v


def _adamw(contrib, w, m, v, name):
    n, rows, cols = contrib.shape
    tr = _row_block(rows, max(SUBLANES, (128 * 1024) // cols // SUBLANES * SUBLANES))

    def body(c_ref, w_ref, m_ref, v_ref, g_ref, d_ref, nm_ref, nv_ref):
        g = c_ref[0].astype(F32)
        for i in range(1, n):
            g = g + c_ref[i].astype(F32)
        delta, nm, nv = _adamw_vals(w_ref[...], g, m_ref[...], v_ref[...])
        g_ref[...] = g
        d_ref[...] = delta
        nm_ref[...] = nm
        nv_ref[...] = nv

    row = pl.BlockSpec((tr, cols), lambda i: (i, 0))
    shp = jax.ShapeDtypeStruct((rows, cols), F32)
    return pl.pallas_call(
        body, name=name, grid=(rows // tr,),
        in_specs=[pl.BlockSpec((n, tr, cols), lambda i: (0, i, 0)), row, row, row],
        out_specs=(row, row, row, row), out_shape=(shp, shp, shp, shp),
        compiler_params=_params("parallel"),
    )(contrib, w, m, v)


def _lane_block(n, cap):
    if n <= cap:
        return n
    best = None
    for cand in range(LANES, cap + 1, LANES):
        if n % cand == 0:
            best = cand
    assert best is not None, n
    return best


def _matmul(a, b, out_dtype, name, mode="nn", tm=1024, tn=1024, tk=1024, exchange=None):
    g = a.shape[0]
    m, k = (a.shape[2], a.shape[1]) if mode == "tn" else (a.shape[1], a.shape[2])
    n = b.shape[1] if mode == "nt" else b.shape[2]
    tm, tn, tk = _lane_block(m, tm), _lane_block(n, tn), _lane_block(k, tk)
    nk = k // tk
    grid = (g, m // tm, n // tn, nk)
    a_spec = (pl.BlockSpec((None, tk, tm), lambda gi, i, j, kk: (gi, kk, i)) if mode == "tn" else
              pl.BlockSpec((None, tm, tk), lambda gi, i, j, kk: (gi, i, kk)))
    b_spec = (pl.BlockSpec((None, tn, tk), lambda gi, i, j, kk: (gi, j, kk)) if mode == "nt" else
              pl.BlockSpec((None, tk, tn), lambda gi, i, j, kk: (gi, kk, j)))
    ca, cb = (0 if mode == "tn" else 1), (1 if mode == "nt" else 0)
    n_ex = 0 if exchange is None else len(exchange.arrays)

    def body(a_ref, b_ref, *rest):
        ex_in, o_ref, ex_out, scratch = rest[:n_ex], rest[n_ex], rest[n_ex + 1:2 * n_ex + 1], rest[2 * n_ex + 1:]
        if exchange is not None:
            start, finish = exchange.copies(ex_in, ex_out, *scratch[-3:])
            pids = [pl.program_id(ax) for ax in range(4)]
            pl.when((pids[0] == 0) & (pids[1] == 0) & (pids[2] == 0) & (pids[3] == 0))(start)
        part = _dg(a_ref[...], b_ref[...], ca, cb)
        if nk == 1:
            o_ref[...] = part.astype(o_ref.dtype)
        else:
            acc_ref = scratch[0]
            kk = pl.program_id(3)

            @pl.when(kk == 0)
            def _():
                acc_ref[...] = part

            @pl.when((kk > 0) & (kk < nk - 1))
            def _():
                acc_ref[...] += part

            @pl.when(kk == nk - 1)
            def _():
                o_ref[...] = (acc_ref[...] + part).astype(o_ref.dtype)
        if exchange is not None:
            pl.when((pids[0] == grid[0] - 1) & (pids[1] == grid[1] - 1) & (pids[2] == grid[2] - 1)
                    & (pids[3] == grid[3] - 1))(finish)

    any_spec = pl.BlockSpec(memory_space=pl.ANY)
    scratch_shapes = [] if nk == 1 else [pltpu.VMEM((tm, tn), F32)]
    out_shape = [jax.ShapeDtypeStruct((g, m, n), out_dtype)]
    if exchange is not None:
        scratch_shapes += [pltpu.SemaphoreType.DMA((exchange.n_sem,)), pltpu.SemaphoreType.DMA((exchange.n_sem,)),
                           pltpu.SemaphoreType.DMA((n_ex,))]
        out_shape += exchange.out_shapes
    outs = pl.pallas_call(
        body, name=name, grid=grid,
        in_specs=[a_spec, b_spec] + [any_spec] * n_ex,
        out_specs=tuple([pl.BlockSpec((None, tm, tn), lambda gi, i, j, kk: (gi, i, j))] + [any_spec] * n_ex),
        out_shape=tuple(out_shape),
        scratch_shapes=scratch_shapes,
        compiler_params=(_params("parallel", "parallel", "parallel", "arbitrary") if exchange is None else
                         _params("arbitrary", "arbitrary", "arbitrary", "arbitrary")),
    )(a, b, *([] if exchange is None else exchange.arrays))
    return outs[0] if exchange is None else outs


def _rms_fwd(x, w):
    s, d = x.shape
    tm = _row_block(s, 512)

    def body(x_ref, w_ref, h_ref):
        h_ref[...] = _rmsnorm(x_ref[...], w_ref[...]).astype(BF16)

    return pl.pallas_call(
        body, name="input_rmsnorm", grid=(s // tm,),
        in_specs=[pl.BlockSpec((tm, d), lambda i: (i, 0)), pl.BlockSpec((1, d), lambda i: (0, 0))],
        out_specs=pl.BlockSpec((tm, d), lambda i: (i, 0)),
        out_shape=jax.ShapeDtypeStruct((s, d), BF16),
        compiler_params=_params("parallel"),
    )(x, w)


def _rms_bwd(x, w, dh_stacked, dh_parts, dx_res):
    s, d = x.shape
    tm = _row_block(s, 256)
    n_parts = 1 + len(dh_parts)

    def body(x_ref, w_ref, *rest):
        part_refs = rest[:n_parts]
        res_ref, gx_ref, gw_ref = rest[n_parts:]
        dh = part_refs[0][...]
        for r in part_refs[1:]:
            dh = dh + r[...]
        _, vjp = jax.vjp(_rmsnorm, x_ref[...], w_ref[...])
        dx, dw = vjp(dh)
        gx_ref[...] = dx + res_ref[...]

        @pl.when(pl.program_id(0) == 0)
        def _():
            gw_ref[...] = jnp.zeros_like(gw_ref)

        gw_ref[...] += dw

    row = pl.BlockSpec((tm, d), lambda i: (i, 0))
    vec = pl.BlockSpec((1, d), lambda i: (0, 0))
    return pl.pallas_call(
        body, name="input_rmsnorm_bwd", grid=(s // tm,),
        in_specs=[row, vec, pl.BlockSpec((None, tm, d), lambda i: (0, i, 0))] + [row] * (n_parts - 1) + [row],
        out_specs=(row, vec),
        out_shape=(jax.ShapeDtypeStruct((s, d), F32), jax.ShapeDtypeStruct((1, d), F32)),
        compiler_params=_params("arbitrary"),
    )(x, w, dh_stacked, *dh_parts, dx_res)


def _lane_masks(rows):
    lane = lax.broadcasted_iota(jnp.int32, (rows, LANES), 1)
    return lane < HEAD_DIM, (lane & (HEAD_DIM - 1)) < HEAD_DIM // 2


def _swap_halves(t, lo_half):
    return jnp.where(lo_half, pltpu.roll(t, LANES - HEAD_DIM // 2, 1), pltpu.roll(t, HEAD_DIM // 2, 1))


def _rope(t, cos, sin_signed, lo_half):
    return t * cos + _swap_halves(t, lo_half) * sin_signed


def _rope_bwd(d, cos, sin_signed, lo_half):
    return d * cos - _swap_halves(d, lo_half) * sin_signed


def _window_mask(first):
    qi = lax.broadcasted_iota(jnp.int32, (2 * ATT_BLOCK, 2 * ATT_BLOCK), 0) & (ATT_BLOCK - 1)
    kj = lax.broadcasted_iota(jnp.int32, (2 * ATT_BLOCK, 2 * ATT_BLOCK), 1)
    dist = qi + ATT_BLOCK - kj
    return (dist >= 0) & (dist <= N_BACK) & ((kj >= ATT_BLOCK) | jnp.logical_not(first))


def _stack_heads(t, head0):
    zero = jnp.zeros_like(t)
    return jnp.concatenate([jnp.where(head0, t, zero), jnp.where(head0, zero, t)], axis=0)


def _unstack_heads(t2, head0):
    return jnp.where(head0, t2[:ATT_BLOCK], t2[ATT_BLOCK:])


def _blocks_per_subsequence(g, nb):
    return lax.shift_right_logical(jnp.int32(nb), 2 * g)


def _attn_fwd(qkv, cos, sin):
    _, s, _ = qkv.shape
    nb = s // ATT_BLOCK

    def body(qkv_ref, cos_ref, sin_ref, o_ref, lse_ref, kp_ref, vp_ref):
        g, t = pl.program_id(0), pl.program_id(1)
        first = (t & (_blocks_per_subsequence(g, nb) - 1)) == 0

        @pl.when(first)
        def _():
            kp_ref[...] = jnp.zeros_like(kp_ref)
            vp_ref[...] = jnp.zeros_like(vp_ref)

        cos_b, sin_b = cos_ref[...], sin_ref[...]
        head0, lo_half = _lane_masks(ATT_BLOCK)
        valid = _window_mask(first)
        for sl in range(WIDTH // LANES):
            cq = pl.ds(sl * LANES, LANES)
            ck = pl.ds(WIDTH + sl * LANES, LANES)
            cv = pl.ds(2 * WIDTH + sl * LANES, LANES)
            qr = (_rope(qkv_ref[:, cq], cos_b, sin_b, lo_half) * (HEAD_DIM ** -0.5)).astype(BF16)
            kr = _rope(qkv_ref[:, ck], cos_b, sin_b, lo_half).astype(BF16)
            v16 = qkv_ref[:, cv].astype(BF16)
            kcat = jnp.concatenate([kp_ref[:, cq], kr], axis=0)
            vcat = jnp.concatenate([vp_ref[:, cq], v16], axis=0)
            sc = jnp.where(valid, _dg(_stack_heads(qr, head0), kcat, 1, 1), -jnp.inf)
            mx = jnp.max(sc, axis=1, keepdims=True)
            p = jnp.exp(sc - mx)
            den = jnp.sum(p, axis=1, keepdims=True)
            o_ref[:, cq] = _unstack_heads(_dg((p * (1.0 / den)).astype(BF16), vcat, 1, 0), head0)
            lse2 = mx + jnp.log(den)
            lse_ref[:, cq] = jnp.where(head0, lse2[:ATT_BLOCK], lse2[ATT_BLOCK:])
            kp_ref[:, cq] = kr
            vp_ref[:, cq] = v16

    blk = lambda w: pl.BlockSpec((None, ATT_BLOCK, w), lambda g, t: (g, t, 0))
    shp = jax.ShapeDtypeStruct((GROUPS, s, WIDTH), F32)
    return pl.pallas_call(
        body, name="dilated_attention_fwd", grid=(GROUPS, nb),
        in_specs=[blk(3 * WIDTH), blk(LANES), blk(LANES)],
        out_specs=(blk(WIDTH), blk(WIDTH)), out_shape=(shp, shp),
        scratch_shapes=[pltpu.VMEM((ATT_BLOCK, WIDTH), BF16), pltpu.VMEM((ATT_BLOCK, WIDTH), BF16)],
        compiler_params=_params("arbitrary", "arbitrary"),
    )(qkv, cos, sin)


def _attn_bwd(qkv, cos, sin, o, lse, do, dlse):
    _, s, _ = qkv.shape
    nb = s // ATT_BLOCK

    def body(qkv_ref, cos_ref, sin_ref, cosp_ref, sinp_ref, o_ref, lse_ref, do_ref, dlse_ref,
             dqkv_ref, kp_ref, vp_ref, dka_ref, dva_ref, dqp_ref):
        g, t = pl.program_id(0), pl.program_id(1)
        first = (t & (_blocks_per_subsequence(g, nb) - 1)) == 0
        active = t < nb
        head0, lo_half = _lane_masks(ATT_BLOCK)
        cos_p, sin_p = cosp_ref[...], sinp_ref[...]

        @pl.when(t == 0)
        def _():
            dka_ref[...] = jnp.zeros_like(dka_ref)
            dva_ref[...] = jnp.zeros_like(dva_ref)
            dqp_ref[...] = jnp.zeros_like(dqp_ref)

        dqkv_ref[:, pl.ds(0, WIDTH)] = dqp_ref[...]

        @pl.when(active & first)
        def _():
            kp_ref[...] = jnp.zeros_like(kp_ref)
            vp_ref[...] = jnp.zeros_like(vp_ref)

        @pl.when(active)
        def _():
            cos_b, sin_b = cos_ref[...], sin_ref[...]
            valid = _window_mask(first)
            for sl in range(WIDTH // LANES):
                cq = pl.ds(sl * LANES, LANES)
                ck = pl.ds(WIDTH + sl * LANES, LANES)
                cv = pl.ds(2 * WIDTH + sl * LANES, LANES)
                qr = (_rope(qkv_ref[:, cq], cos_b, sin_b, lo_half) * (HEAD_DIM ** -0.5)).astype(BF16)
                kr = _rope(qkv_ref[:, ck], cos_b, sin_b, lo_half).astype(BF16)
                v16 = qkv_ref[:, cv].astype(BF16)
                kcat = jnp.concatenate([kp_ref[:, cq], kr], axis=0)
                vcat = jnp.concatenate([vp_ref[:, cq], v16], axis=0)
                do_b = do_ref[:, cq]
                do16 = do_b.astype(BF16)
                lse_b = lse_ref[:, cq]
                cterm = dlse_ref[:, cq] - do_b * o_ref[:, cq]
                dqs, dkc, dvc = [], None, None
                for hm in (head0, jnp.logical_not(head0)):
                    qm = jnp.where(hm, qr, jnp.zeros_like(qr))
                    dom = jnp.where(hm, do16, jnp.zeros_like(do16))
                    sc = jnp.where(valid[:ATT_BLOCK], _dg(qm, kcat, 1, 1), -jnp.inf)
                    lse_h = jnp.max(jnp.where(hm, lse_b, -jnp.inf), axis=1, keepdims=True)
                    c = jnp.sum(jnp.where(hm, cterm, 0.0), axis=1, keepdims=True)
                    p = jnp.exp(sc - lse_h)
                    ds16 = (p * (_dg(dom, vcat, 1, 1) + c)).astype(BF16)
                    dv_h, dk_h = _dg(p.astype(BF16), dom, 0, 0), _dg(ds16, qm, 0, 0)
                    dvc = dv_h if dvc is None else dvc + dv_h
                    dkc = dk_h if dkc is None else dkc + dk_h
                    dqs.append(_dg(ds16, kcat, 1, 0))
                dq = jnp.where(head0, dqs[0], dqs[1]) * (HEAD_DIM ** -0.5)
                dqp_ref[:, cq] = _rope_bwd(dq, cos_b, sin_b, lo_half).astype(BF16)
                dqkv_ref[:, ck] = _rope_bwd(dka_ref[:, cq] + dkc[:ATT_BLOCK], cos_p, sin_p, lo_half).astype(BF16)
                dqkv_ref[:, cv] = (dva_ref[:, cq] + dvc[:ATT_BLOCK]).astype(BF16)
                dka_ref[:, cq] = dkc[ATT_BLOCK:]
                dva_ref[:, cq] = dvc[ATT_BLOCK:]
                kp_ref[:, cq] = kr
                vp_ref[:, cq] = v16

        @pl.when(jnp.logical_not(active))
        def _():
            for sl in range(WIDTH // LANES):
                cq = pl.ds(sl * LANES, LANES)
                dqkv_ref[:, pl.ds(WIDTH + sl * LANES, LANES)] = _rope_bwd(dka_ref[:, cq], cos_p, sin_p, lo_half).astype(BF16)
                dqkv_ref[:, pl.ds(2 * WIDTH + sl * LANES, LANES)] = dva_ref[:, cq].astype(BF16)

    cur = lambda w: pl.BlockSpec((None, ATT_BLOCK, w), lambda g, t: (g, jnp.minimum(t, nb - 1), 0))
    prev = lambda w: pl.BlockSpec((None, ATT_BLOCK, w), lambda g, t: (g, jnp.maximum(t - 1, 0), 0))
    return pl.pallas_call(
        body, name="dilated_attention_bwd", grid=(GROUPS, nb + 1),
        in_specs=[cur(3 * WIDTH), cur(LANES), cur(LANES), prev(LANES), prev(LANES),
                  cur(WIDTH), cur(WIDTH), cur(WIDTH), cur(WIDTH)],
        out_specs=prev(3 * WIDTH), out_shape=jax.ShapeDtypeStruct((GROUPS, s, 3 * WIDTH), BF16),
        scratch_shapes=[pltpu.VMEM((ATT_BLOCK, WIDTH), BF16), pltpu.VMEM((ATT_BLOCK, WIDTH), BF16),
                        pltpu.VMEM((ATT_BLOCK, WIDTH), F32), pltpu.VMEM((ATT_BLOCK, WIDTH), F32),
                        pltpu.VMEM((ATT_BLOCK, WIDTH), BF16)],
        compiler_params=_params("arbitrary", "arbitrary"),
    )(qkv, cos, sin, cos, sin, o, lse, do, dlse)


CONV_PAD = SUBLANES


def _gdn_post(y, is_q, is_k):
    head0, _ = _lane_masks(y.shape[0])
    c = _silu(y)
    sq = c * c
    ss0 = jnp.sum(jnp.where(head0, sq, 0.0), axis=1, keepdims=True)
    ss1 = jnp.sum(jnp.where(head0, 0.0, sq), axis=1, keepdims=True)
    r = jnp.where(head0, lax.rsqrt(ss0 + NORM_EPS), lax.rsqrt(ss1 + NORM_EPS))
    scale = jnp.where(is_q, HEAD_DIM ** -0.5, 1.0).astype(F32)
    return jnp.where(is_q | is_k, c * r * scale, c)


def _conv_rows(xp_ref, w, c0, rows):
    y = w[0:1, :] * xp_ref[pl.ds(c0 + CONV_PAD - (CONV_K - 1), rows), :]
    for k in range(1, CONV_K):
        y = y + w[k:k + 1, :] * xp_ref[pl.ds(c0 + CONV_PAD - (CONV_K - 1) + k, rows), :]
    return y


def _gdn_pre_fwd(proj_r, conv8, col0):
    s = proj_r.shape[0]
    tr = _row_block(s, 512)
    nblk = QKV_B // LANES
    nq = WIDTH // LANES

    def body(x_ref, w_ref, out_ref, xp_ref):
        j = pl.program_id(0)
        is_q, is_k = j < nq, (j >= nq) & (j < 2 * nq)
        xp_ref[pl.ds(0, CONV_PAD), :] = jnp.zeros((CONV_PAD, LANES), F32)
        xp_ref[pl.ds(CONV_PAD, s), :] = x_ref[...]
        w = w_ref[...]
        for c in range(s // tr):
            out_ref[pl.ds(c * tr, tr), :] = _gdn_post(_conv_rows(xp_ref, w, c * tr, tr), is_q, is_k)

    return pl.pallas_call(
        body, name="gdn_conv_fwd", grid=(nblk,),
        in_specs=[pl.BlockSpec((s, LANES), lambda j: (0, col0 + j)), pl.BlockSpec((SUBLANES, LANES), lambda j: (0, j))],
        out_specs=pl.BlockSpec((s, LANES), lambda j: (0, j)),
        out_shape=jax.ShapeDtypeStruct((s, QKV_B), F32),
        scratch_shapes=[pltpu.VMEM((s + CONV_PAD, LANES), F32)],
        compiler_params=_params("parallel"),
    )(proj_r, conv8)


def _gdn_pre_bwd(proj_r, conv8, dc, col0):
    s = proj_r.shape[0]
    tr = _row_block(s, 512)
    nblk = QKV_B // LANES
    nq = WIDTH // LANES

    def body(x_ref, w_ref, dc_ref, dx_ref, dw_ref, xp_ref, dyp_ref):
        j = pl.program_id(0)
        is_q, is_k = j < nq, (j >= nq) & (j < 2 * nq)
        xp_ref[pl.ds(0, CONV_PAD), :] = jnp.zeros((CONV_PAD, LANES), F32)
        xp_ref[pl.ds(CONV_PAD, s), :] = x_ref[...]
        dyp_ref[pl.ds(s, CONV_PAD), :] = jnp.zeros((CONV_PAD, LANES), F32)
        w = w_ref[...]
        for c in range(s // tr):
            y = _conv_rows(xp_ref, w, c * tr, tr)
            _, vjp = jax.vjp(lambda yy: _gdn_post(yy, is_q, is_k), y)
            dyp_ref[pl.ds(c * tr, tr), :] = vjp(dc_ref[pl.ds(c * tr, tr), :])[0]
        dws = [jnp.zeros((1, LANES), F32) for _ in range(CONV_K)]
        for c in range(s // tr):
            c0 = c * tr
            dy = dyp_ref[pl.ds(c0, tr), :]
            dx = w[0:1, :] * dyp_ref[pl.ds(c0 + CONV_K - 1, tr), :]
            for k in range(1, CONV_K):
                dx = dx + w[k:k + 1, :] * dyp_ref[pl.ds(c0 + CONV_K - 1 - k, tr), :]
            dx_ref[pl.ds(c0, tr), :] = dx.astype(BF16)
            for k in range(CONV_K):
                xs = xp_ref[pl.ds(c0 + CONV_PAD - (CONV_K - 1) + k, tr), :]
                dws[k] = dws[k] + jnp.sum(dy * xs, axis=0, keepdims=True)
        row = lax.broadcasted_iota(jnp.int32, (SUBLANES, LANES), 0)
        dwb = jnp.zeros((SUBLANES, LANES), F32)
        for k in range(CONV_K):
            dwb = dwb + jnp.where(row == k, dws[k], 0.0)
        dw_ref[...] = dwb

    return pl.pallas_call(
        body, name="gdn_conv_bwd", grid=(nblk,),
        in_specs=[pl.BlockSpec((s, LANES), lambda j: (0, col0 + j)), pl.BlockSpec((SUBLANES, LANES), lambda j: (0, j)),
                  pl.BlockSpec((s, LANES), lambda j: (0, j))],
        out_specs=(pl.BlockSpec((s, LANES), lambda j: (0, j)), pl.BlockSpec((SUBLANES, LANES), lambda j: (0, j))),
        out_shape=(jax.ShapeDtypeStruct((s, QKV_B), BF16), jax.ShapeDtypeStruct((SUBLANES, QKV_B), F32)),
        scratch_shapes=[pltpu.VMEM((s + CONV_PAD, LANES), F32), pltpu.VMEM((s + CONV_PAD, LANES), F32)],
        compiler_params=_params("parallel"),
    )(proj_r, conv8, dc)


def _gdn_chunk(q, k, v, bcol, acol, alog, dtb, gnw, state, t_saved=None):
    n = q.shape[-2]
    shp = (1, n, n)
    row = lax.broadcasted_iota(jnp.int32, shp, 1)
    col = lax.broadcasted_iota(jnp.int32, shp, 2)
    beta = _sigmoid(bcol)
    g = -jnp.exp(alog) * _softplus(acol + dtb)
    g_row = jnp.sum(jnp.where(row == col, g, 0.0), axis=-2, keepdims=True)
    big_g = jnp.sum(jnp.where(row >= col, g_row, 0.0), axis=-1, keepdims=True)
    big_g_row = jnp.sum(jnp.where(row <= col, g, 0.0), axis=-2, keepdims=True)
    decay_incl = jnp.exp(jnp.where(row >= col, big_g - big_g_row, -jnp.inf))
    decay_strict = jnp.where(row > col, decay_incl, 0.0)
    k_beta = k * beta
    a_mat = _mm_nt(k_beta, k) * decay_strict
    t_inv = _tri_inv(a_mat) if t_saved is None else _tri_inv_saved(a_mat, t_saved)
    e_g = jnp.exp(big_g)
    u = _mm(t_inv, v * beta)
    w = _mm(t_inv, k_beta * e_g)
    attn = _mm_nt(q, k) * decay_incl
    v_new = u - _mm(w, state)
    o = _mm(q * e_g, state) + _mm(attn, v_new)
    total = jnp.sum(g, axis=-2, keepdims=True)
    new_state = state * jnp.exp(total) + _mm_tn(k * jnp.exp(total - big_g), v_new)
    return _rmsnorm(o, gnw), new_state, t_inv


def _split_heads(x):
    return jnp.stack([x[:, h * HEAD_DIM:(h + 1) * HEAD_DIM] for h in range(HEADS)], axis=0)


def _merge_heads(x):
    return jnp.concatenate([x[h] for h in range(HEADS)], axis=1)


def _logit_columns(ba):
    lane = lax.broadcasted_iota(jnp.int32, ba.shape, 1)

    def cols(off):
        return jnp.stack([jnp.sum(jnp.where(lane == off + h, ba, 0.0), axis=1, keepdims=True) for h in range(HEADS)], axis=0)

    return cols(0), cols(HEADS)


def _logit_block(dbc, dac, shape):
    lane = lax.broadcasted_iota(jnp.int32, shape, 1)
    out = jnp.zeros(shape, F32)
    for h in range(HEADS):
        out = out + jnp.where(lane == h, dbc[h], 0.0) + jnp.where(lane == HEADS + h, dac[h], 0.0)
    return out


def _gdn_scan_fwd(cqkv, proj_r, ba_col, alog, dtb, gnw):
    s = cqkv.shape[0]
    nc = s // CHUNK
    span = SCAN_CHUNKS * CHUNK

    def body(q_ref, k_ref, v_ref, ba_ref, al_ref, dt_ref, gnw_ref, o_ref, st_ref, ti_ref, state_ref):
        @pl.when(pl.program_id(0) == 0)
        def _():
            state_ref[...] = jnp.zeros_like(state_ref)

        st = state_ref[...]
        for u in range(SCAN_CHUNKS):
            rows = pl.ds(u * CHUNK, CHUNK)
            st_ref[u] = st
            bcol, acol = _logit_columns(ba_ref[rows, :])
            o, st, t_inv = _gdn_chunk(_split_heads(q_ref[rows, :]), _split_heads(k_ref[rows, :]),
                                      _split_heads(v_ref[rows, :]), bcol, acol, al_ref[...], dt_ref[...], gnw_ref[...], st)
            o_ref[rows, :] = _merge_heads(o)
            ti_ref[u] = t_inv
        state_ref[...] = st

    part = lambda i: pl.BlockSpec((span, WIDTH), lambda n: (n, i))
    par = pl.BlockSpec((HEADS, 1, 1), lambda n: (0, 0, 0))
    per_chunk = pl.BlockSpec((SCAN_CHUNKS, HEADS, HEAD_DIM, HEAD_DIM), lambda n: (n, 0, 0, 0))
    per_chunk_shape = jax.ShapeDtypeStruct((nc, HEADS, HEAD_DIM, HEAD_DIM), F32)
    return pl.pallas_call(
        body, name="gdn_scan_fwd", grid=(nc // SCAN_CHUNKS,),
        in_specs=[part(0), part(1), part(2), pl.BlockSpec((span, LANES), lambda n: (n, ba_col)), par, par,
                  pl.BlockSpec((1, 1, HEAD_DIM), lambda n: (0, 0, 0))],
        out_specs=(part(0), per_chunk, per_chunk),
        out_shape=(jax.ShapeDtypeStruct((s, WIDTH), F32), per_chunk_shape, per_chunk_shape),
        scratch_shapes=[pltpu.VMEM((HEADS, HEAD_DIM, HEAD_DIM), F32)],
        compiler_params=_params("arbitrary"),
    )(cqkv, cqkv, cqkv, proj_r, alog, dtb, gnw)


def _gdn_scan_bwd(cqkv, proj_r, ba_col, alog, dtb, gnw, states, t_invs, do):
    s = cqkv.shape[0]
    nc = s // CHUNK
    span = SCAN_CHUNKS * CHUNK
    n_steps = nc // SCAN_CHUNKS

    def body(q_ref, k_ref, v_ref, ba_ref, al_ref, dt_ref, gnw_ref, st_ref, ti_ref, do_ref,
             dqkv_ref, dba_ref, dal_ref, ddt_ref, dgnw_ref, dstate_ref):
        @pl.when(pl.program_id(0) == 0)
        def _():
            dstate_ref[...] = jnp.zeros_like(dstate_ref)
            dal_ref[...] = jnp.zeros_like(dal_ref)
            ddt_ref[...] = jnp.zeros_like(ddt_ref)
            dgnw_ref[...] = jnp.zeros_like(dgnw_ref)

        dst = dstate_ref[...]
        for u in reversed(range(SCAN_CHUNKS)):
            rows = pl.ds(u * CHUNK, CHUNK)
            bcol, acol = _logit_columns(ba_ref[rows, :])
            _, vjp = jax.vjp(lambda *a, t_saved=ti_ref[u]: _gdn_chunk(*a, t_saved=t_saved)[:2],
                             _split_heads(q_ref[rows, :]), _split_heads(k_ref[rows, :]), _split_heads(v_ref[rows, :]),
                             bcol, acol, al_ref[...], dt_ref[...], gnw_ref[...], st_ref[u])
            dq, dk, dv, dbc, dac, dal, ddt, dgn, dst = vjp((_split_heads(do_ref[rows, :]), dst))
            dqkv_ref[rows, pl.ds(0, WIDTH)] = _merge_heads(dq)
            dqkv_ref[rows, pl.ds(WIDTH, WIDTH)] = _merge_heads(dk)
            dqkv_ref[rows, pl.ds(2 * WIDTH, WIDTH)] = _merge_heads(dv)
            dba_ref[rows, :] = _logit_block(dbc, dac, (CHUNK, LANES))
            dal_ref[...] += dal
            ddt_ref[...] += ddt
            dgnw_ref[...] += dgn
        dstate_ref[...] = dst

    rev = lambda n: n_steps - 1 - n
    part = lambda i: pl.BlockSpec((span, WIDTH), lambda n: (rev(n), i))
    par = pl.BlockSpec((HEADS, 1, 1), lambda n: (0, 0, 0))
    vec = pl.BlockSpec((1, 1, HEAD_DIM), lambda n: (0, 0, 0))
    par_shape = jax.ShapeDtypeStruct((HEADS, 1, 1), F32)
    per_chunk = pl.BlockSpec((SCAN_CHUNKS, HEADS, HEAD_DIM, HEAD_DIM), lambda n: (rev(n), 0, 0, 0))
    return pl.pallas_call(
        body, name="gdn_scan_bwd", grid=(n_steps,),
        in_specs=[part(0), part(1), part(2), pl.BlockSpec((span, LANES), lambda n: (rev(n), ba_col)), par, par, vec,
                  per_chunk, per_chunk, part(0)],
        out_specs=(pl.BlockSpec((span, QKV_B), lambda n: (rev(n), 0)), pl.BlockSpec((span, LANES), lambda n: (rev(n), 0)),
                   par, par, vec),
        out_shape=(jax.ShapeDtypeStruct((s, QKV_B), F32), jax.ShapeDtypeStruct((s, LANES), F32), par_shape, par_shape,
                   jax.ShapeDtypeStruct((1, 1, HEAD_DIM), F32)),
        scratch_shapes=[pltpu.VMEM((HEADS, HEAD_DIM, HEAD_DIM), F32)],
        compiler_params=_params("arbitrary"),
    )(cqkv, cqkv, cqkv, proj_r, alog, dtb, gnw, states, t_invs, do)


def _tail_loss(x, tgt, o0, o1, o2, l0, l1, l2, ga, gb, za, zb, ob, fnw, wua, wub, wo, tap_a, tap_b, tap_o):
    lm = jnp.maximum(jnp.maximum(l0, l1), l2)
    e0, e1, e2 = jnp.exp(l0 - lm), jnp.exp(l1 - lm), jnp.exp(l2 - lm)
    o_a = (e0 * o0 + e1 * o1 + e2 * o2) / (e0 + e1 + e2)
    xa, xb = o_a * _silu(za), ob * _silu(zb)
    y_a = _mm_x(xa, wua) + tap_a
    y_b = _mm_x(xb, wub) + tap_b
    merged = _sigmoid(ga) * y_a + _sigmoid(gb) * y_b
    y = _rmsnorm(x + _mm_x(merged, wo) + tap_o, fnw)
    err = y - tgt
    per_token = jnp.sum(err * err, axis=1, keepdims=True) * (0.5 / x.shape[1])
    return jnp.sum(per_token, axis=0, keepdims=True), (xa.astype(BF16), xb.astype(BF16), merged.astype(BF16))


def _tail(x, tgt, o_all, lse_all, og12, lg12, proj_r, ob, wua, wub, wo, fnw):
    s, d = x.shape
    tm = _row_block(s, TAIL_ROWS)
    col_za = 2 * d // WIDTH
    col_zb = (2 * d + WIDTH + QKV_B) // WIDTH

    def body(x_ref, t_ref, o0_ref, o1_ref, o2_ref, l0_ref, l1_ref, l2_ref, ga_ref, gb_ref, za_ref, zb_ref, ob_ref,
             wua_ref, wub_ref, wo_ref, fnw_ref,
             loss_ref, dx_ref, do0_ref, do1_ref, do2_ref, dl0_ref, dl1_ref, dl2_ref, dga_ref, dgb_ref, dza_ref,
             dzb_ref, dob_ref, xa_ref, xb_ref, mg_ref, dya_ref, dyb_ref, dmo_ref, dfnw_ref):
        @pl.when(pl.program_id(0) == 0)
        def _():
            for r in (loss_ref, dfnw_ref):
                r[...] = jnp.zeros_like(r)

        tap = jnp.zeros((tm, d), F32)
        args = (x_ref[...], t_ref[...], o0_ref[...], o1_ref[...], o2_ref[...], l0_ref[...], l1_ref[...], l2_ref[...],
                ga_ref[...], gb_ref[...], za_ref[...], zb_ref[...], ob_ref[...], fnw_ref[...],
                wua_ref[...], wub_ref[...], wo_ref[...], tap, tap, tap)
        loss, vjp, (xa16, xb16, mg16) = jax.vjp(_tail_loss, *args, has_aux=True)
        (dx, _, do0, do1, do2, dl0, dl1, dl2, dga, dgb, dza, dzb, dob, dfnw, _, _, _, dya, dyb, dmo) = vjp(
            jnp.ones((1, 1), F32))
        loss_ref[...] += jnp.broadcast_to(loss, loss_ref.shape)
        dx_ref[...] = dx
        do0_ref[...], do1_ref[...], do2_ref[...] = do0, do1, do2
        dl0_ref[...], dl1_ref[...], dl2_ref[...] = dl0, dl1, dl2
        dga_ref[...] = dga.astype(BF16)
        dgb_ref[...] = dgb.astype(BF16)
        dza_ref[...] = dza.astype(BF16)
        dzb_ref[...] = dzb.astype(BF16)
        dob_ref[...] = dob
        xa_ref[...], xb_ref[...], mg_ref[...] = xa16, xb16, mg16
        dya_ref[...] = dya.astype(BF16)
        dyb_ref[...] = dyb.astype(BF16)
        dmo_ref[...] = dmo.astype(BF16)
        dfnw_ref[...] += dfnw

    row = lambda w, c=0: pl.BlockSpec((tm, w), lambda i: (i, c))
    grp0 = pl.BlockSpec((None, tm, WIDTH), lambda i: (0, i, 0))
    full = lambda a, b: pl.BlockSpec((a, b), lambda i: (0, 0))
    f32 = lambda a, b: jax.ShapeDtypeStruct((a, b), F32)
    b16 = lambda a, b: jax.ShapeDtypeStruct((a, b), BF16)
    stacked = jax.ShapeDtypeStruct((GROUPS, s, WIDTH), F32)
    gspecs = [grp0, row(WIDTH), row(WIDTH)]
    in_specs = ([row(d), row(d)] + gspecs * 2 + [row(d, 0), row(d, 1), row(WIDTH, col_za), row(WIDTH, col_zb),
                row(WIDTH), full(WIDTH, d), full(WIDTH, d), full(d, d), full(1, d)])
    out_specs = ([full(SUBLANES, LANES), row(d)] + gspecs * 2 + [row(d), row(d), row(WIDTH), row(WIDTH), row(WIDTH),
                 row(WIDTH), row(WIDTH), row(d), row(d), row(d), row(d), full(1, d)])
    gshapes = [stacked, f32(s, WIDTH), f32(s, WIDTH)]
    out_shape = ([f32(SUBLANES, LANES), f32(s, d)] + gshapes * 2 + [b16(s, d), b16(s, d), b16(s, WIDTH),
                 b16(s, WIDTH), f32(s, WIDTH), b16(s, WIDTH), b16(s, WIDTH), b16(s, d), b16(s, d), b16(s, d), b16(s, d),
                 f32(1, d)])
    return pl.pallas_call(
        body, name="tail_fwd_bwd", grid=(s // tm,),
        in_specs=in_specs, out_specs=tuple(out_specs), out_shape=tuple(out_shape),
        compiler_params=pltpu.CompilerParams(dimension_semantics=("arbitrary",), vmem_limit_bytes=TAIL_VMEM_LIMIT),
    )(x, tgt, o_all, og12[0], og12[1], lse_all, lg12[0], lg12[1], proj_r, proj_r, proj_r, proj_r, ob, wua, wub, wo, fnw)


PERMUTE_SPAN = 4096


def _permute_span(s):
    return PERMUTE_SPAN if s % PERMUTE_SPAN == 0 else s


def _from_dilated_rows(stacked, g, dil, name):
    n_slots, s, c = stacked.shape
    view = stacked.reshape(n_slots, dil, s // dil, c)
    span = _permute_span(s)

    def body(in_ref, out_ref):
        for r in range(dil):
            out_ref[pl.ds(r, span // dil, stride=dil), :] = in_ref[r]

    return pl.pallas_call(
        body, name=name, grid=(s // span, c // LANES),
        in_specs=[pl.BlockSpec((None, dil, span // dil, LANES), lambda n, j: (g, 0, n, j))],
        out_specs=pl.BlockSpec((span, LANES), lambda n, j: (n, j)),
        out_shape=jax.ShapeDtypeStruct((s, c), stacked.dtype),
        compiler_params=_params("parallel", "parallel"),
    )(view)


def _to_dilated_rows_into(nat, stacked, g, dil, name):
    n_slots, s, c = stacked.shape
    view = stacked.reshape(n_slots, dil, s // dil, c)
    span = _permute_span(s)

    def body(nat_ref, old_ref, out_ref):
        for r in range(dil):
            out_ref[r] = nat_ref[pl.ds(r, span // dil, stride=dil), :]

    out = pl.pallas_call(
        body, name=name, grid=(s // span, c // LANES),
        in_specs=[pl.BlockSpec((span, LANES), lambda n, j: (n, j)), pl.BlockSpec(memory_space=pl.ANY)],
        out_specs=pl.BlockSpec((None, dil, span // dil, LANES), lambda n, j: (g, 0, n, j)),
        out_shape=jax.ShapeDtypeStruct(view.shape, stacked.dtype),
        input_output_aliases={1: 0},
        compiler_params=_params("parallel", "parallel"),
    )(nat, view)
    return out.reshape(stacked.shape)


def _to_dilated(a, dil):
    if dil == 1:
        return a
    s = a.shape[0]
    return a.reshape(s // dil, dil, -1).transpose(1, 0, 2).reshape(a.shape)


def _rope_tables(s):
    inv_freq = ROPE_THETA ** (-jnp.arange(0, HEAD_DIM, 2, dtype=F32) / HEAD_DIM)
    ang = jnp.arange(s, dtype=F32)[:, None] * inv_freq[None, :]
    cos_n = jnp.tile(jnp.cos(ang), (1, 2 * LANES // HEAD_DIM))
    sin_h = jnp.sin(ang)
    sin_n = jnp.tile(jnp.concatenate([-sin_h, sin_h], axis=1), (1, LANES // HEAD_DIM))

    def per_group(table, tag):
        out = jnp.broadcast_to(table, (GROUPS,) + table.shape)
        for g in range(1, GROUPS):
            out = _to_dilated_rows_into(table, out, g, DILATIONS[g], "rope_%s_to_dilated_%d" % (tag, g))
        return out

    return per_group(cos_n, "cos"), per_group(sin_n, "sin")


def _regroup_columns(pieces, widths):
    starts, pos = [], 0
    for p in pieces:
        starts.append(pos)
        pos += p.shape[1]
    assert pos == sum(widths), (pos, widths)
    out, lo = [], 0
    for w in widths:
        hi, parts = lo + w, []
        for p, st in zip(pieces, starts):
            a, b = max(lo, st), min(hi, st + p.shape[1])
            if a < b:
                parts.append(p[:, a - st:b - st])
        out.append(parts[0] if len(parts) == 1 else jnp.concatenate(parts, axis=1))
        lo = hi
    return out


def _pack_rows(parts, dtype, row_multiple):
    flat = jnp.concatenate([p.reshape(-1).astype(dtype) for p in parts])
    tile = row_multiple * LANES
    pad = (-flat.shape[0]) % tile
    return jnp.pad(flat, (0, pad)).reshape(-1, LANES)


def _unpack_rows(packed, shapes):
    flat = packed.reshape(-1)
    out, start = [], 0
    for shp in shapes:
        size = 1
        for n in shp:
            size *= n
        out.append(flat[start:start + size].reshape(shp))
        start += size
    return out


def kernel(x, norm_w, w_in, conv_w, a_log, dt_bias, gdn_norm_w, w_up_a, w_up_b, w_out, final_norm_w, loss_target, m_norm_w, m_w_in, m_conv_w, m_a_log, m_dt_bias, m_gdn_norm_w, m_w_up_a, m_w_up_b, m_w_out, m_final_norm_w, v_norm_w, v_w_in, v_conv_w, v_a_log, v_dt_bias, v_gdn_norm_w, v_w_up_a, v_w_up_b, v_w_out, v_final_norm_w):
    x2, tgt = x[0], loss_target[0]
    s, d = x2.shape
    me = 4 * lax.axis_index("x") + 2 * lax.axis_index("y") + lax.axis_index("c")
    win8 = w_in.shape[2]
    conv8w = conv_w.shape[2]

    conv_shard = jnp.pad(conv_w[0], ((0, SUBLANES - CONV_K), (0, 0)))
    w_in_g, wua_g, wub_g, wo_g, conv_g = _all_gather(
        [w_in[0].astype(BF16), w_up_a[0].astype(BF16), w_up_b[0].astype(BF16), w_out[0].astype(BF16), conv_shard])
    wua = jnp.concatenate([wua_g[i] for i in range(N_DEV)], axis=1)
    wub = jnp.concatenate([wub_g[i] for i in range(N_DEV)], axis=1)
    wo = wo_g.reshape(d, d)
    conv8 = jnp.concatenate([conv_g[i] for i in range(N_DEV)], axis=1)

    seg_widths = [QKV_B] * GROUPS + [WIDTH, QKV_B, WIDTH, 2 * HEADS, 2 * d]
    wq0, wq1, wq2, w_za, w_qkvb, w_zb, w_ba, w_gates = _regroup_columns([w_in_g[i] for i in range(N_DEV)], seg_widths)
    w_qkv = jnp.stack([wq0, wq1, wq2])
    w_rest = jnp.concatenate([w_gates, w_za, w_qkvb, w_zb, w_ba,
                              jnp.zeros((d, BA_PAD - 2 * HEADS), BF16)], axis=1)
    col_qkvb = (2 * d + WIDTH) // LANES
    col_ba = (2 * d + 2 * WIDTH + QKV_B) // LANES

    h = _rms_fwd(x2, norm_w)
    h_all = jnp.stack([_to_dilated(h, dil) for dil in DILATIONS])
    qkv_all = _matmul(h_all, w_qkv, F32, "in_proj_attention", tn=QKV_B)
    proj_r = _matmul(h[None], w_rest[None], F32, "in_proj_rest", tn=2560)[0]
    cos, sin = _rope_tables(s)
    o_all, lse_all = _attn_fwd(qkv_all, cos, sin)
    og12 = [_from_dilated_rows(o_all, g, DILATIONS[g], "attn_out_to_natural_%d" % g) for g in (1, 2)]
    lg12 = [_from_dilated_rows(lse_all, g, DILATIONS[g], "attn_lse_to_natural_%d" % g) for g in (1, 2)]

    cqkv = _gdn_pre_fwd(proj_r, conv8, col_qkvb)
    alog3, dtb3, gnw3 = a_log.reshape(HEADS, 1, 1), dt_bias.reshape(HEADS, 1, 1), gdn_norm_w.reshape(1, 1, HEAD_DIM)
    ob, states, t_invs = _gdn_scan_fwd(cqkv, proj_r, col_ba, alog3, dtb3, gnw3)

    (loss_blk, dx_res, do_all, do1, do2, dl_all, dl1, dl2, dga, dgb, dza, dzb, dob,
     xa16, xb16, mg16, dya16, dyb16, dmo16, dfnw) = _tail(
        x2, tgt, o_all, lse_all, og12, lg12, proj_r, ob, wua, wub, wo, final_norm_w.reshape(1, d))
    dwua = _matmul(xa16[None], dya16[None], F32, "up_a_dw", mode="tn", tk=2048)[0]
    dwub = _matmul(xb16[None], dyb16[None], F32, "up_b_dw", mode="tn", tk=2048)[0]
    dwo = _matmul(mg16[None], dmo16[None], F32, "out_proj_dw", mode="tn", tk=2048)[0]

    for g, (t_o, t_l) in ((1, (do1, dl1)), (2, (do2, dl2))):
        do_all = _to_dilated_rows_into(t_o, do_all, g, DILATIONS[g], "attn_dout_to_dilated_%d" % g)
        dl_all = _to_dilated_rows_into(t_l, dl_all, g, DILATIONS[g], "attn_dlse_to_dilated_%d" % g)
    dqkv_all = _attn_bwd(qkv_all, cos, sin, o_all, lse_all, do_all, dl_all)

    dcqkv, dba, dalog3, ddtb3, dgnw3 = _gdn_scan_bwd(cqkv, proj_r, col_ba, alog3, dtb3, gnw3, states, t_invs, dob)
    dqkv_b, dconv8 = _gdn_pre_bwd(proj_r, conv8, dcqkv, col_qkvb)
    dproj_r = jnp.concatenate([dga, dgb, dza, dqkv_b, dzb,
                               jnp.pad(dba.astype(BF16), ((0, 0), (0, BA_PAD - LANES)))], axis=1)

    def col_slabs(a, width):
        return jnp.stack([a[:, j * width:(j + 1) * width] for j in range(N_DEV)])

    core = lax.axis_index("c").astype(jnp.int32).reshape(1)
    small_slabs = [col_slabs(dwua, d // N_DEV), col_slabs(dwub, d // N_DEV), dwo.reshape(N_DEV, d // N_DEV, d)]
    dw_qkv, *small_sib = _matmul(h_all, dqkv_all, F32, "in_proj_attention_dw", mode="tn", tk=2048, tn=QKV_B,
                                 exchange=_sibling_exchange(small_slabs))
    small_partials = [_pair_sum(a, b, core, "grads_pair_sum_%d" % (i + 1))
                      for i, (a, b) in enumerate(zip(small_slabs, small_sib))]
    dw_rest, *small_contrib = _matmul(h[None], dproj_r[None], F32, "in_proj_rest_dw", mode="tn", tk=2048,
                                      exchange=_chip_exchange(small_partials))
    dw_rest = dw_rest[0]
    o2 = 2 * d
    dw_in_pieces = [dw_qkv[0], dw_qkv[1], dw_qkv[2],
                    dw_rest[:, o2:o2 + WIDTH], dw_rest[:, o2 + WIDTH:o2 + WIDTH + QKV_B],
                    dw_rest[:, o2 + WIDTH + QKV_B:o2 + 2 * WIDTH + QKV_B],
                    dw_rest[:, o2 + 2 * WIDTH + QKV_B:o2 + 2 * WIDTH + QKV_B + 2 * HEADS],
                    dw_rest[:, :o2]]

    w_in_slabs = jnp.stack(_regroup_columns(dw_in_pieces, [win8] * N_DEV))
    dh_a, w_in_sib = _matmul(dqkv_all, w_qkv, F32, "in_proj_attention_dh", mode="nt", tk=2048,
                             exchange=_sibling_exchange([w_in_slabs]))
    w_in_partial = _pair_sum(w_in_slabs, w_in_sib, core, "grads_pair_sum_0")
    dh_r, w_in_contrib = _matmul(dproj_r[None], w_rest[None], F32, "in_proj_rest_dh", mode="nt", tk=2560,
                                 exchange=_chip_exchange([w_in_partial]))
    contrib = [w_in_contrib] + small_contrib
    dh_parts = [dh_r[0]] + [_from_dilated_rows(dh_a, g, DILATIONS[g], "dh_to_natural_%d" % g) for g in (1, 2)]
    grad_x, dnorm_w = _rms_bwd(x2, norm_w, dh_a, dh_parts, dx_res)

    small_parts = [dnorm_w, dfnw, dconv8[:CONV_K], dalog3[:, 0, 0], ddtb3[:, 0, 0], dgnw3[0], loss_blk[0, 0:1]]
    small_rows = [-(-p.size // LANES) for p in small_parts]
    small = jnp.concatenate([jnp.pad(p.reshape(-1), (0, r * LANES - p.size)).reshape(r, LANES)
                             for p, r in zip(small_parts, small_rows)])
    small = jnp.pad(small, ((0, (-small.shape[0]) % SUBLANES), (0, 0)))
    small_sum = _small_all_reduce(small)
    pieces, r0 = [], 0
    for p, r in zip(small_parts, small_rows):
        pieces.append(small_sum[r0:r0 + r].reshape(-1)[:p.size].reshape(p.shape))
        r0 += r
    g_norm_w, g_fnw, g_conv_full, g_alog, g_dtb, g_gnw, loss_sum = pieces
    g_conv = lax.dynamic_slice(g_conv_full, (0, me * conv8w), (CONV_K, conv8w))

    big = [_adamw(c, w[0], m[0], v[0], name) for c, w, m, v, name in (
        (contrib[0], w_in, m_w_in, v_w_in, "adamw_w_in"), (contrib[1], w_up_a, m_w_up_a, v_w_up_a, "adamw_w_up_a"),
        (contrib[2], w_up_b, m_w_up_b, v_w_up_b, "adamw_w_up_b"), (contrib[3], w_out, m_w_out, v_w_out, "adamw_w_out"))]
    g_big, d_big, nm_big, nv_big = ([t[i] for t in big] for i in range(4))

    small_ws = [norm_w, final_norm_w, conv_w, a_log, dt_bias, gdn_norm_w]
    small_ms = [m_norm_w, m_final_norm_w, m_conv_w, m_a_log, m_dt_bias, m_gdn_norm_w]
    small_vs = [v_norm_w, v_final_norm_w, v_conv_w, v_a_log, v_dt_bias, v_gdn_norm_w]
    small_gs = [g_norm_w, g_fnw, g_conv, g_alog, g_dtb, g_gnw]
    small_shapes = [t.shape for t in small_ws]
    sm = _adamw(_pack_rows(small_gs, F32, SUBLANES)[None], _pack_rows(small_ws, F32, SUBLANES),
                _pack_rows(small_ms, F32, SUBLANES), _pack_rows(small_vs, F32, SUBLANES), "adamw_small")
    g_sm, d_sm, nm_sm, nv_sm = (_unpack_rows(t, small_shapes) for t in sm)

    def ordered(bigs, smalls):
        nw, fnw_, cw, al, dtb, gn = smalls
        wi, ua, ub, wo_ = (t[None] for t in bigs)
        return [nw, wi, cw, al, dtb, gn, ua, ub, wo_, fnw_]

    return (loss_sum.reshape(()), grad_x[None], *ordered(g_big, g_sm), *ordered(d_big, d_sm),
            *ordered(nm_big, nm_sm), *ordered(nv_big, nv_sm))
```

```python
import functools

import jax
import jax.numpy as jnp
from jax import lax
from jax.experimental import pallas as pl
from jax.experimental.pallas import tpu as pltpu

F32 = jnp.float32
BF16 = jnp.bfloat16
MESH = pl.DeviceIdType.MESH
N_DEV = 8
LANES = 128
SUBLANES = 8

GROUPS = 3
HEADS = 8
HEAD_DIM = 64
WIDTH = HEADS * HEAD_DIM
ATT_BLOCK = 128
DILATIONS = (1, 4, 16)
N_BACK = 128
CONV_K = 4
CHUNK = 64
SCAN_CHUNKS = 4
QKV_B = 3 * WIDTH
QKV_A = GROUPS * 3 * WIDTH
BA_PAD = 512
NORM_EPS = 1e-6
ROPE_THETA = 10000.0
ADAM_LR, ADAM_B1, ADAM_B2, ADAM_EPS, ADAM_WD, ADAM_STEP = 0.001, 0.9, 0.999, 1e-08, 0.01, 10

VMEM_LIMIT = 56 * 1024 * 1024
TAIL_ROWS = 256
TAIL_VMEM_LIMIT = 62 * 1024 * 1024


def _params(*sem):
    return pltpu.CompilerParams(dimension_semantics=sem, vmem_limit_bytes=VMEM_LIMIT)


def _dg(a, b, ca, cb):
    nb = a.ndim - 2
    batch = tuple(range(nb))
    return lax.dot_general(a, b, (((nb + ca,), (nb + cb,)), (batch, batch)), preferred_element_type=F32)


@jax.custom_vjp
def _mm(a, b):
    return _dg(a.astype(BF16), b.astype(BF16), 1, 0)


def _mm_fwd(a, b):
    return _mm(a, b), (a.astype(BF16), b.astype(BF16))


def _mm_bwd(res, ct):
    a16, b16 = res
    c16 = ct.astype(BF16)
    return _dg(c16, b16, 1, 1), _dg(a16, c16, 0, 0)


_mm.defvjp(_mm_fwd, _mm_bwd)


@jax.custom_vjp
def _mm_nt(a, b):
    return _dg(a.astype(BF16), b.astype(BF16), 1, 1)


def _mm_nt_fwd(a, b):
    return _mm_nt(a, b), (a.astype(BF16), b.astype(BF16))


def _mm_nt_bwd(res, ct):
    a16, b16 = res
    c16 = ct.astype(BF16)
    return _dg(c16, b16, 1, 0), _dg(c16, a16, 0, 0)


_mm_nt.defvjp(_mm_nt_fwd, _mm_nt_bwd)


@jax.custom_vjp
def _mm_tn(a, b):
    return _dg(a.astype(BF16), b.astype(BF16), 0, 0)


def _mm_tn_fwd(a, b):
    return _mm_tn(a, b), (a.astype(BF16), b.astype(BF16))


def _mm_tn_bwd(res, ct):
    a16, b16 = res
    c16 = ct.astype(BF16)
    return _dg(b16, c16, 1, 1), _dg(a16, c16, 1, 0)


_mm_tn.defvjp(_mm_tn_fwd, _mm_tn_bwd)


@jax.custom_vjp
def _mm_x(a, w16):
    return _dg(a.astype(BF16), w16, 1, 0)


def _mm_x_fwd(a, w16):
    return _mm_x(a, w16), w16


def _mm_x_bwd(w16, ct):
    return _dg(ct.astype(BF16), w16, 1, 1), jnp.zeros_like(w16)


_mm_x.defvjp(_mm_x_fwd, _mm_x_bwd)


def _split16(a):
    hi = a.astype(BF16)
    lo = (a - hi.astype(F32)).astype(BF16)
    return hi, lo


def _dot3(a, b, ca, cb):
    ah, al = _split16(a)
    bh, bl = _split16(b)
    return _dg(ah, bh, ca, cb) + (_dg(ah, bl, ca, cb) + _dg(al, bh, ca, cb))


def _tri_inv_impl(a):
    n = a.shape[-1]
    shp = (1,) * (a.ndim - 2) + (n, n)
    eye = (lax.broadcasted_iota(jnp.int32, shp, a.ndim - 2) == lax.broadcasted_iota(jnp.int32, shp, a.ndim - 1)).astype(F32)
    x = eye - a
    p = a
    for it in range(5):
        dot = _dot3 if it < 2 else (lambda u, v, cu, cv: _dg(u.astype(BF16), v.astype(BF16), cu, cv))
        p = dot(p, p, 1, 0)
        x = x + dot(x, p, 1, 0)
    return x


@jax.custom_vjp
def _tri_inv(a):
    return _tri_inv_impl(a)


def _tri_inv_fwd(a):
    t = _tri_inv_impl(a)
    return t, t


def _tri_inv_bwd(t, ct):
    t16 = t.astype(BF16)
    return (-_dg(_dg(t16, ct.astype(BF16), 0, 0).astype(BF16), t16, 1, 1),)


_tri_inv.defvjp(_tri_inv_fwd, _tri_inv_bwd)


@jax.custom_vjp
def _tri_inv_saved(a, t):
    return t


def _tri_inv_saved_fwd(a, t):
    return t, t


def _tri_inv_saved_bwd(t, ct):
    return _tri_inv_bwd(t, ct) + (jnp.zeros_like(t),)


_tri_inv_saved.defvjp(_tri_inv_saved_fwd, _tri_inv_saved_bwd)


def _sigmoid(x):
    return 1.0 / (1.0 + jnp.exp(-x))


def _silu(x):
    return x * _sigmoid(x)


def _softplus(x):
    return jnp.maximum(x, 0.0) + jnp.log(1.0 + jnp.exp(-jnp.abs(x)))


def _rmsnorm(x, w):
    return x * lax.rsqrt(jnp.mean(x * x, axis=-1, keepdims=True) + NORM_EPS) * w


def _row_block(rows, cap):
    best = None
    for cand in range(SUBLANES, min(rows, cap) + 1, SUBLANES):
        if rows % cand == 0:
            best = cand
    assert best is not None, rows
    return best


def _mesh_peers():
    x, y, c = lax.axis_index("x"), lax.axis_index("y"), lax.axis_index("c")
    me = 4 * x + 2 * y + c
    peers = []
    for k in range(1, N_DEV):
        px = 1 - x if (k >> 2) & 1 else x
        py = 1 - y if (k >> 1) & 1 else y
        pc = 1 - c if k & 1 else c
        peers.append(((px, py, pc), 4 * px + 2 * py + pc))
    return me, peers


N_CHIPS = 4
OTHER_CHIPS = 3


def _chip_peers():
    x, y, c = lax.axis_index("x"), lax.axis_index("y"), lax.axis_index("c")
    return x, y, c, [(1 - x, y), (x, 1 - y), (1 - x, 1 - y)]


def _all_gather(shards):
    n_arr = len(shards)
    per = 1 + 2 * OTHER_CHIPS

    def body(*refs):
        in_refs, out_refs = refs[:n_arr], refs[n_arr:2 * n_arr]
        send_sems, recv_sems, loc_sems = refs[2 * n_arr:]
        x, y, c, chips = _chip_peers()
        me, sibling = (x, y, c), (x, y, 1 - c)

        def slot(px, py, pc):
            return 4 * px + 2 * py + pc

        def copy(i, k, block, to, src=None):
            dst = out_refs[i].at[slot(*block)]
            return pltpu.make_async_remote_copy(src_ref=dst if src is None else src, dst_ref=dst,
                                                send_sem=send_sems.at[i * per + k], recv_sem=recv_sems.at[i * per + k],
                                                device_id=to, device_id_type=MESH)

        own = [pltpu.make_async_copy(in_refs[i], out_refs[i].at[slot(*me)], loc_sems.at[i]) for i in range(n_arr)]
        for cp in own:
            cp.start()
        first = []
        for i in range(n_arr):
            first += [copy(i, 1 + j, me, (*chip, c), src=in_refs[i]) for j, chip in enumerate(chips)]
            first.append(copy(i, 0, me, sibling, src=in_refs[i]))
        for cp in first:
            cp.start()
        passed = []
        for j, chip in enumerate(chips):
            for i in range(n_arr):
                copy(i, 1 + j, (*chip, c), me).wait_recv()
                fwd = copy(i, 1 + OTHER_CHIPS + j, (*chip, c), sibling)
                fwd.start()
                passed.append(fwd)
        for i in range(n_arr):
            copy(i, 0, sibling, me).wait_recv()
            for j, chip in enumerate(chips):
                copy(i, 1 + OTHER_CHIPS + j, (*chip, 1 - c), me).wait_recv()
        for cp in first + passed:
            cp.wait_send()
        for cp in own:
            cp.wait()

    any_spec = pl.BlockSpec(memory_space=pl.ANY)
    return pl.pallas_call(
        body, name="weights_all_gather",
        out_shape=tuple(jax.ShapeDtypeStruct((N_DEV,) + a.shape, a.dtype) for a in shards),
        in_specs=[any_spec] * n_arr, out_specs=tuple([any_spec] * n_arr),
        scratch_shapes=[pltpu.SemaphoreType.DMA((n_arr * per,)), pltpu.SemaphoreType.DMA((n_arr * per,)),
                        pltpu.SemaphoreType.DMA((n_arr,))],
    )(*shards)


class _Exchange:
    def __init__(self, arrays, out_shapes, n_sem, copies):
        self.arrays, self.out_shapes, self.n_sem, self.copies = list(arrays), list(out_shapes), n_sem, copies


def _sibling_exchange(slabs):
    n_arr = len(slabs)

    def copies(in_refs, out_refs, send_sems, recv_sems, loc_sems):
        x, y, c, _ = _chip_peers()
        sends = [pltpu.make_async_remote_copy(src_ref=in_refs[i].at[2 * q + (1 - c)], dst_ref=out_refs[i].at[q],
                                              send_sem=send_sems.at[i * N_CHIPS + q], recv_sem=recv_sems.at[i * N_CHIPS + q],
                                              device_id=(x, y, 1 - c), device_id_type=MESH)
                 for i in range(n_arr) for q in range(N_CHIPS)]

        def start():
            for cp in sends:
                cp.start()

        def finish():
            for cp in sends:
                cp.wait_recv()
            for cp in sends:
                cp.wait_send()

        return start, finish

    return _Exchange(slabs, [jax.ShapeDtypeStruct((N_CHIPS,) + a.shape[1:], a.dtype) for a in slabs],
                     n_arr * N_CHIPS, copies)


def _pair_sum(slabs, from_sibling, core, name):
    _, rows, cols = slabs.shape
    tr = _row_block(rows, max(SUBLANES, (256 * 1024) // cols // SUBLANES * SUBLANES))

    def body(core_ref, a_ref, b_ref, o_ref):
        o_ref[...] = (a_ref[...] + b_ref[...]).astype(BF16)

    grid_spec = pltpu.PrefetchScalarGridSpec(
        num_scalar_prefetch=1, grid=(N_CHIPS, rows // tr),
        in_specs=[pl.BlockSpec((None, tr, cols), lambda q, r, core_ref: (2 * q + core_ref[0], r, 0)),
                  pl.BlockSpec((None, tr, cols), lambda q, r, core_ref: (q, r, 0))],
        out_specs=pl.BlockSpec((None, tr, cols), lambda q, r, core_ref: (q, r, 0)))
    return pl.pallas_call(
        body, name=name, grid_spec=grid_spec,
        out_shape=jax.ShapeDtypeStruct((N_CHIPS, rows, cols), BF16),
        compiler_params=_params("parallel", "parallel"),
    )(core, slabs, from_sibling)


def _chip_exchange(partials):
    n_arr = len(partials)

    def copies(in_refs, out_refs, send_sems, recv_sems, loc_sems):
        x, y, c, chips = _chip_peers()
        mine = 2 * x + y
        own = [pltpu.make_async_copy(in_refs[i].at[mine], out_refs[i].at[mine], loc_sems.at[i]) for i in range(n_arr)]

        def copy(i, j, chip, src_slot, dst_slot):
            return pltpu.make_async_remote_copy(src_ref=in_refs[i].at[src_slot], dst_ref=out_refs[i].at[dst_slot],
                                                send_sem=send_sems.at[i * OTHER_CHIPS + j],
                                                recv_sem=recv_sems.at[i * OTHER_CHIPS + j],
                                                device_id=(*chip, c), device_id_type=MESH)

        sends = [copy(i, j, chip, 2 * chip[0] + chip[1], mine) for j, chip in enumerate(chips) for i in range(n_arr)]
        recvs = [copy(i, j, chip, mine, 2 * chip[0] + chip[1]) for j, chip in enumerate(chips) for i in range(n_arr)]

        def start():
            for cp in own + sends:
                cp.start()

        def finish():
            for cp in recvs:
                cp.wait_recv()
            for cp in sends:
                cp.wait_send()
            for cp in own:
                cp.wait()

        return start, finish

    return _Exchange(partials, [jax.ShapeDtypeStruct(a.shape, a.dtype) for a in partials], n_arr * OTHER_CHIPS, copies)


def _small_all_reduce(part):
    rows = part.shape[0]

    def body(p_ref, o_ref, buf_ref, send_sems, recv_sems):
        me, peers = _mesh_peers()
        buf_ref[me] = p_ref[...]
        sends = []
        for k, (dev, pid) in enumerate(peers):
            cp = pltpu.make_async_remote_copy(src_ref=p_ref, dst_ref=buf_ref.at[me], send_sem=send_sems.at[k],
                                              recv_sem=recv_sems.at[k], device_id=dev, device_id_type=MESH)
            cp.start()
            sends.append(cp)
        for k, (dev, pid) in enumerate(peers):
            pltpu.make_async_remote_copy(src_ref=p_ref, dst_ref=buf_ref.at[pid], send_sem=send_sems.at[k],
                                         recv_sem=recv_sems.at[k], device_id=dev, device_id_type=MESH).wait_recv()
        for cp in sends:
            cp.wait_send()
        acc = buf_ref[0]
        for i in range(1, N_DEV):
            acc = acc + buf_ref[i]
        o_ref[...] = acc

    vmem = pl.BlockSpec(memory_space=pltpu.VMEM)
    return pl.pallas_call(
        body, name="small_all_reduce",
        out_shape=jax.ShapeDtypeStruct(part.shape, F32),
        in_specs=[vmem], out_specs=vmem,
        scratch_shapes=[pltpu.VMEM((N_DEV, rows, LANES), F32), pltpu.SemaphoreType.DMA((N_DEV - 1,)),
                        pltpu.SemaphoreType.DMA((N_DEV - 1,))],
    )(part)


def _adamw_vals(w, g, m, v):
    m = ADAM_B1 * m + (1.0 - ADAM_B1) * g
    v = ADAM_B2 * v + (1.0 - ADAM_B2) * (g * g)
    m_hat = m / (1.0 - ADAM_B1 ** ADAM_STEP)
    v_hat = v / (1.0 - ADAM_B2 ** ADAM_STEP)
    delta = -ADAM_LR * (m_hat / (jnp.sqrt(v_hat) + ADAM_EPS) + ADAM_WD * w)
    return delta, m, v


def _adamw(contrib, w, m, v, name):
    n, rows, cols = contrib.shape
    tr = _row_block(rows, max(SUBLANES, (128 * 1024) // cols // SUBLANES * SUBLANES))

    def body(c_ref, w_ref, m_ref, v_ref, g_ref, d_ref, nm_ref, nv_ref):
        g = c_ref[0].astype(F32)
        for i in range(1, n):
            g = g + c_ref[i].astype(F32)
        delta, nm, nv = _adamw_vals(w_ref[...], g, m_ref[...], v_ref[...])
        g_ref[...] = g
        d_ref[...] = delta
        nm_ref[...] = nm
        nv_ref[...] = nv

    row = pl.BlockSpec((tr, cols), lambda i: (i, 0))
    shp = jax.ShapeDtypeStruct((rows, cols), F32)
    return pl.pallas_call(
        body, name=name, grid=(rows // tr,),
        in_specs=[pl.BlockSpec((n, tr, cols), lambda i: (0, i, 0)), row, row, row],
        out_specs=(row, row, row, row), out_shape=(shp, shp, shp, shp),
        compiler_params=_params("parallel"),
    )(contrib, w, m, v)


def _lane_block(n, cap):
    if n <= cap:
        return n
    best = None
    for cand in range(LANES, cap + 1, LANES):
        if n % cand == 0:
            best = cand
    assert best is not None, n
    return best


def _matmul(a, b, out_dtype, name, mode="nn", tm=1024, tn=1024, tk=1024, exchange=None):
    g = a.shape[0]
    m, k = (a.shape[2], a.shape[1]) if mode == "tn" else (a.shape[1], a.shape[2])
    n = b.shape[1] if mode == "nt" else b.shape[2]
    tm, tn, tk = _lane_block(m, tm), _lane_block(n, tn), _lane_block(k, tk)
    nk = k // tk
    grid = (g, m // tm, n // tn, nk)
    a_spec = (pl.BlockSpec((None, tk, tm), lambda gi, i, j, kk: (gi, kk, i)) if mode == "tn" else
              pl.BlockSpec((None, tm, tk), lambda gi, i, j, kk: (gi, i, kk)))
    b_spec = (pl.BlockSpec((None, tn, tk), lambda gi, i, j, kk: (gi, j, kk)) if mode == "nt" else
              pl.BlockSpec((None, tk, tn), lambda gi, i, j, kk: (gi, kk, j)))
    ca, cb = (0 if mode == "tn" else 1), (1 if mode == "nt" else 0)
    n_ex = 0 if exchange is None else len(exchange.arrays)

    def body(a_ref, b_ref, *rest):
        ex_in, o_ref, ex_out, scratch = rest[:n_ex], rest[n_ex], rest[n_ex + 1:2 * n_ex + 1], rest[2 * n_ex + 1:]
        if exchange is not None:
            start, finish = exchange.copies(ex_in, ex_out, *scratch[-3:])
            pids = [pl.program_id(ax) for ax in range(4)]
            pl.when((pids[0] == 0) & (pids[1] == 0) & (pids[2] == 0) & (pids[3] == 0))(start)
        part = _dg(a_ref[...], b_ref[...], ca, cb)
        if nk == 1:
            o_ref[...] = part.astype(o_ref.dtype)
        else:
            acc_ref = scratch[0]
            kk = pl.program_id(3)

            @pl.when(kk == 0)
            def _():
                acc_ref[...] = part

            @pl.when((kk > 0) & (kk < nk - 1))
            def _():
                acc_ref[...] += part

            @pl.when(kk == nk - 1)
            def _():
                o_ref[...] = (acc_ref[...] + part).astype(o_ref.dtype)
        if exchange is not None:
            pl.when((pids[0] == grid[0] - 1) & (pids[1] == grid[1] - 1) & (pids[2] == grid[2] - 1)
                    & (pids[3] == grid[3] - 1))(finish)

    any_spec = pl.BlockSpec(memory_space=pl.ANY)
    scratch_shapes = [] if nk == 1 else [pltpu.VMEM((tm, tn), F32)]
    out_shape = [jax.ShapeDtypeStruct((g, m, n), out_dtype)]
    if exchange is not None:
        scratch_shapes += [pltpu.SemaphoreType.DMA((exchange.n_sem,)), pltpu.SemaphoreType.DMA((exchange.n_sem,)),
                           pltpu.SemaphoreType.DMA((n_ex,))]
        out_shape += exchange.out_shapes
    outs = pl.pallas_call(
        body, name=name, grid=grid,
        in_specs=[a_spec, b_spec] + [any_spec] * n_ex,
        out_specs=tuple([pl.BlockSpec((None, tm, tn), lambda gi, i, j, kk: (gi, i, j))] + [any_spec] * n_ex),
        out_shape=tuple(out_shape),
        scratch_shapes=scratch_shapes,
        compiler_params=(_params("parallel", "parallel", "parallel", "arbitrary") if exchange is None else
                         _params("arbitrary", "arbitrary", "arbitrary", "arbitrary")),
    )(a, b, *([] if exchange is None else exchange.arrays))
    return outs[0] if exchange is None else outs


def _rms_fwd(x, w):
    s, d = x.shape
    tm = _row_block(s, 512)

    def body(x_ref, w_ref, h_ref):
        h_ref[...] = _rmsnorm(x_ref[...], w_ref[...]).astype(BF16)

    return pl.pallas_call(
        body, name="input_rmsnorm", grid=(s // tm,),
        in_specs=[pl.BlockSpec((tm, d), lambda i: (i, 0)), pl.BlockSpec((1, d), lambda i: (0, 0))],
        out_specs=pl.BlockSpec((tm, d), lambda i: (i, 0)),
        out_shape=jax.ShapeDtypeStruct((s, d), BF16),
        compiler_params=_params("parallel"),
    )(x, w)


def _rms_bwd(x, w, dh_stacked, dh_parts, dx_res):
    s, d = x.shape
    tm = _row_block(s, 256)
    n_parts = 1 + len(dh_parts)

    def body(x_ref, w_ref, *rest):
        part_refs = rest[:n_parts]
        res_ref, gx_ref, gw_ref = rest[n_parts:]
        dh = part_refs[0][...]
        for r in part_refs[1:]:
            dh = dh + r[...]
        _, vjp = jax.vjp(_rmsnorm, x_ref[...], w_ref[...])
        dx, dw = vjp(dh)
        gx_ref[...] = dx + res_ref[...]

        @pl.when(pl.program_id(0) == 0)
        def _():
            gw_ref[...] = jnp.zeros_like(gw_ref)

        gw_ref[...] += dw

    row = pl.BlockSpec((tm, d), lambda i: (i, 0))
    vec = pl.BlockSpec((1, d), lambda i: (0, 0))
    return pl.pallas_call(
        body, name="input_rmsnorm_bwd", grid=(s // tm,),
        in_specs=[row, vec, pl.BlockSpec((None, tm, d), lambda i: (0, i, 0))] + [row] * (n_parts - 1) + [row],
        out_specs=(row, vec),
        out_shape=(jax.ShapeDtypeStruct((s, d), F32), jax.ShapeDtypeStruct((1, d), F32)),
        compiler_params=_params("arbitrary"),
    )(x, w, dh_stacked, *dh_parts, dx_res)


def _lane_masks(rows):
    lane = lax.broadcasted_iota(jnp.int32, (rows, LANES), 1)
    return lane < HEAD_DIM, (lane & (HEAD_DIM - 1)) < HEAD_DIM // 2


def _swap_halves(t, lo_half):
    return jnp.where(lo_half, pltpu.roll(t, LANES - HEAD_DIM // 2, 1), pltpu.roll(t, HEAD_DIM // 2, 1))


def _rope(t, cos, sin_signed, lo_half):
    return t * cos + _swap_halves(t, lo_half) * sin_signed


def _rope_bwd(d, cos, sin_signed, lo_half):
    return d * cos - _swap_halves(d, lo_half) * sin_signed


def _window_mask(first):
    qi = lax.broadcasted_iota(jnp.int32, (2 * ATT_BLOCK, 2 * ATT_BLOCK), 0) & (ATT_BLOCK - 1)
    kj = lax.broadcasted_iota(jnp.int32, (2 * ATT_BLOCK, 2 * ATT_BLOCK), 1)
    dist = qi + ATT_BLOCK - kj
    return (dist >= 0) & (dist <= N_BACK) & ((kj >= ATT_BLOCK) | jnp.logical_not(first))


def _stack_heads(t, head0):
    zero = jnp.zeros_like(t)
    return jnp.concatenate([jnp.where(head0, t, zero), jnp.where(head0, zero, t)], axis=0)


def _unstack_heads(t2, head0):
    return jnp.where(head0, t2[:ATT_BLOCK], t2[ATT_BLOCK:])


def _blocks_per_subsequence(g, nb):
    return lax.shift_right_logical(jnp.int32(nb), 2 * g)


def _attn_fwd(qkv, cos, sin):
    _, s, _ = qkv.shape
    nb = s // ATT_BLOCK

    def body(qkv_ref, cos_ref, sin_ref, o_ref, lse_ref, kp_ref, vp_ref):
        g, t = pl.program_id(0), pl.program_id(1)
        first = (t & (_blocks_per_subsequence(g, nb) - 1)) == 0

        @pl.when(first)
        def _():
            kp_ref[...] = jnp.zeros_like(kp_ref)
            vp_ref[...] = jnp.zeros_like(vp_ref)

        cos_b, sin_b = cos_ref[...], sin_ref[...]
        head0, lo_half = _lane_masks(ATT_BLOCK)
        valid = _window_mask(first)
        for sl in range(WIDTH // LANES):
            cq = pl.ds(sl * LANES, LANES)
            ck = pl.ds(WIDTH + sl * LANES, LANES)
            cv = pl.ds(2 * WIDTH + sl * LANES, LANES)
            qr = (_rope(qkv_ref[:, cq], cos_b, sin_b, lo_half) * (HEAD_DIM ** -0.5)).astype(BF16)
            kr = _rope(qkv_ref[:, ck], cos_b, sin_b, lo_half).astype(BF16)
            v16 = qkv_ref[:, cv].astype(BF16)
            kcat = jnp.concatenate([kp_ref[:, cq], kr], axis=0)
            vcat = jnp.concatenate([vp_ref[:, cq], v16], axis=0)
            sc = jnp.where(valid, _dg(_stack_heads(qr, head0), kcat, 1, 1), -jnp.inf)
            mx = jnp.max(sc, axis=1, keepdims=True)
            p = jnp.exp(sc - mx)
            den = jnp.sum(p, axis=1, keepdims=True)
            o_ref[:, cq] = _unstack_heads(_dg((p * (1.0 / den)).astype(BF16), vcat, 1, 0), head0)
            lse2 = mx + jnp.log(den)
            lse_ref[:, cq] = jnp.where(head0, lse2[:ATT_BLOCK], lse2[ATT_BLOCK:])
            kp_ref[:, cq] = kr
            vp_ref[:, cq] = v16

    blk = lambda w: pl.BlockSpec((None, ATT_BLOCK, w), lambda g, t: (g, t, 0))
    shp = jax.ShapeDtypeStruct((GROUPS, s, WIDTH), F32)
    return pl.pallas_call(
        body, name="dilated_attention_fwd", grid=(GROUPS, nb),
        in_specs=[blk(3 * WIDTH), blk(LANES), blk(LANES)],
        out_specs=(blk(WIDTH), blk(WIDTH)), out_shape=(shp, shp),
        scratch_shapes=[pltpu.VMEM((ATT_BLOCK, WIDTH), BF16), pltpu.VMEM((ATT_BLOCK, WIDTH), BF16)],
        compiler_params=_params("arbitrary", "arbitrary"),
    )(qkv, cos, sin)


def _attn_bwd(qkv, cos, sin, o, lse, do, dlse):
    _, s, _ = qkv.shape
    nb = s // ATT_BLOCK

    def body(qkv_ref, cos_ref, sin_ref, cosp_ref, sinp_ref, o_ref, lse_ref, do_ref, dlse_ref,
             dqkv_ref, kp_ref, vp_ref, dka_ref, dva_ref, dqp_ref):
        g, t = pl.program_id(0), pl.program_id(1)
        first = (t & (_blocks_per_subsequence(g, nb) - 1)) == 0
        active = t < nb
        head0, lo_half = _lane_masks(ATT_BLOCK)
        cos_p, sin_p = cosp_ref[...], sinp_ref[...]

        @pl.when(t == 0)
        def _():
            dka_ref[...] = jnp.zeros_like(dka_ref)
            dva_ref[...] = jnp.zeros_like(dva_ref)
            dqp_ref[...] = jnp.zeros_like(dqp_ref)

        dqkv_ref[:, pl.ds(0, WIDTH)] = dqp_ref[...]

        @pl.when(active & first)
        def _():
            kp_ref[...] = jnp.zeros_like(kp_ref)
            vp_ref[...] = jnp.zeros_like(vp_ref)

        @pl.when(active)
        def _():
            cos_b, sin_b = cos_ref[...], sin_ref[...]
            valid = _window_mask(first)
            for sl in range(WIDTH // LANES):
                cq = pl.ds(sl * LANES, LANES)
                ck = pl.ds(WIDTH + sl * LANES, LANES)
                cv = pl.ds(2 * WIDTH + sl * LANES, LANES)
                qr = (_rope(qkv_ref[:, cq], cos_b, sin_b, lo_half) * (HEAD_DIM ** -0.5)).astype(BF16)
                kr = _rope(qkv_ref[:, ck], cos_b, sin_b, lo_half).astype(BF16)
                v16 = qkv_ref[:, cv].astype(BF16)
                kcat = jnp.concatenate([kp_ref[:, cq], kr], axis=0)
                vcat = jnp.concatenate([vp_ref[:, cq], v16], axis=0)
                do_b = do_ref[:, cq]
                do16 = do_b.astype(BF16)
                lse_b = lse_ref[:, cq]
                cterm = dlse_ref[:, cq] - do_b * o_ref[:, cq]
                dqs, dkc, dvc = [], None, None
                for hm in (head0, jnp.logical_not(head0)):
                    qm = jnp.where(hm, qr, jnp.zeros_like(qr))
                    dom = jnp.where(hm, do16, jnp.zeros_like(do16))
                    sc = jnp.where(valid[:ATT_BLOCK], _dg(qm, kcat, 1, 1), -jnp.inf)
                    lse_h = jnp.max(jnp.where(hm, lse_b, -jnp.inf), axis=1, keepdims=True)
                    c = jnp.sum(jnp.where(hm, cterm, 0.0), axis=1, keepdims=True)
                    p = jnp.exp(sc - lse_h)
                    ds16 = (p * (_dg(dom, vcat, 1, 1) + c)).astype(BF16)
                    dv_h, dk_h = _dg(p.astype(BF16), dom, 0, 0), _dg(ds16, qm, 0, 0)
                    dvc = dv_h if dvc is None else dvc + dv_h
                    dkc = dk_h if dkc is None else dkc + dk_h
                    dqs.append(_dg(ds16, kcat, 1, 0))
                dq = jnp.where(head0, dqs[0], dqs[1]) * (HEAD_DIM ** -0.5)
                dqp_ref[:, cq] = _rope_bwd(dq, cos_b, sin_b, lo_half).astype(BF16)
                dqkv_ref[:, ck] = _rope_bwd(dka_ref[:, cq] + dkc[:ATT_BLOCK], cos_p, sin_p, lo_half).astype(BF16)
                dqkv_ref[:, cv] = (dva_ref[:, cq] + dvc[:ATT_BLOCK]).astype(BF16)
                dka_ref[:, cq] = dkc[ATT_BLOCK:]
                dva_ref[:, cq] = dvc[ATT_BLOCK:]
                kp_ref[:, cq] = kr
                vp_ref[:, cq] = v16

        @pl.when(jnp.logical_not(active))
        def _():
            for sl in range(WIDTH // LANES):
                cq = pl.ds(sl * LANES, LANES)
                dqkv_ref[:, pl.ds(WIDTH + sl * LANES, LANES)] = _rope_bwd(dka_ref[:, cq], cos_p, sin_p, lo_half).astype(BF16)
                dqkv_ref[:, pl.ds(2 * WIDTH + sl * LANES, LANES)] = dva_ref[:, cq].astype(BF16)

    cur = lambda w: pl.BlockSpec((None, ATT_BLOCK, w), lambda g, t: (g, jnp.minimum(t, nb - 1), 0))
    prev = lambda w: pl.BlockSpec((None, ATT_BLOCK, w), lambda g, t: (g, jnp.maximum(t - 1, 0), 0))
    return pl.pallas_call(
        body, name="dilated_attention_bwd", grid=(GROUPS, nb + 1),
        in_specs=[cur(3 * WIDTH), cur(LANES), cur(LANES), prev(LANES), prev(LANES),
                  cur(WIDTH), cur(WIDTH), cur(WIDTH), cur(WIDTH)],
        out_specs=prev(3 * WIDTH), out_shape=jax.ShapeDtypeStruct((GROUPS, s, 3 * WIDTH), BF16),
        scratch_shapes=[pltpu.VMEM((ATT_BLOCK, WIDTH), BF16), pltpu.VMEM((ATT_BLOCK, WIDTH), BF16),
                        pltpu.VMEM((ATT_BLOCK, WIDTH), F32), pltpu.VMEM((ATT_BLOCK, WIDTH), F32),
                        pltpu.VMEM((ATT_BLOCK, WIDTH), BF16)],
        compiler_params=_params("arbitrary", "arbitrary"),
    )(qkv, cos, sin, cos, sin, o, lse, do, dlse)


CONV_PAD = SUBLANES


def _gdn_post(y, scale):
    head0, _ = _lane_masks(y.shape[0])
    c = _silu(y)
    sq = c * c
    ss0 = jnp.sum(jnp.where(head0, sq, 0.0), axis=1, keepdims=True)
    ss1 = jnp.sum(jnp.where(head0, 0.0, sq), axis=1, keepdims=True)
    r = jnp.where(head0, lax.rsqrt(ss0 + NORM_EPS), lax.rsqrt(ss1 + NORM_EPS))
    return c * r * scale


def _block_kind(j, nq):
    return j < 2 * nq, jnp.where(j < nq, HEAD_DIM ** -0.5, 1.0).astype(F32)


def _conv_rows(xp_ref, w, c0, rows):
    y = w[0:1, :] * xp_ref[pl.ds(c0 + CONV_PAD - (CONV_K - 1), rows), :]
    for k in range(1, CONV_K):
        y = y + w[k:k + 1, :] * xp_ref[pl.ds(c0 + CONV_PAD - (CONV_K - 1) + k, rows), :]
    return y


def _gdn_pre_fwd(proj_r, conv8, col0):
    s = proj_r.shape[0]
    tr = _row_block(s, 512)
    nblk = QKV_B // LANES
    nq = WIDTH // LANES

    def body(x_ref, w_ref, out_ref, xp_ref):
        normed, scale = _block_kind(pl.program_id(0), nq)
        xp_ref[pl.ds(0, CONV_PAD), :] = jnp.zeros((CONV_PAD, LANES), F32)
        xp_ref[pl.ds(CONV_PAD, s), :] = x_ref[...]
        w = w_ref[...]

        @pl.when(normed)
        def _():
            for c in range(s // tr):
                out_ref[pl.ds(c * tr, tr), :] = _gdn_post(_conv_rows(xp_ref, w, c * tr, tr), scale)

        @pl.when(jnp.logical_not(normed))
        def _():
            for c in range(s // tr):
                out_ref[pl.ds(c * tr, tr), :] = _silu(_conv_rows(xp_ref, w, c * tr, tr))

    return pl.pallas_call(
        body, name="gdn_conv_fwd", grid=(nblk,),
        in_specs=[pl.BlockSpec((s, LANES), lambda j: (0, col0 + j)), pl.BlockSpec((SUBLANES, LANES), lambda j: (0, j))],
        out_specs=pl.BlockSpec((s, LANES), lambda j: (0, j)),
        out_shape=jax.ShapeDtypeStruct((s, QKV_B), F32),
        scratch_shapes=[pltpu.VMEM((s + CONV_PAD, LANES), F32)],
        compiler_params=_params("parallel"),
    )(proj_r, conv8)


def _gdn_pre_bwd(proj_r, conv8, dc, col0):
    s = proj_r.shape[0]
    tr = _row_block(s, 512)
    nblk = QKV_B // LANES
    nq = WIDTH // LANES

    def body(x_ref, w_ref, dc_ref, dx_ref, dw_ref, xp_ref, dyp_ref):
        normed, scale = _block_kind(pl.program_id(0), nq)
        xp_ref[pl.ds(0, CONV_PAD), :] = jnp.zeros((CONV_PAD, LANES), F32)
        xp_ref[pl.ds(CONV_PAD, s), :] = x_ref[...]
        dyp_ref[pl.ds(s, CONV_PAD), :] = jnp.zeros((CONV_PAD, LANES), F32)
        w = w_ref[...]

        def conv_output_cotangents(post):
            for c in range(s // tr):
                _, vjp = jax.vjp(post, _conv_rows(xp_ref, w, c * tr, tr))
                dyp_ref[pl.ds(c * tr, tr), :] = vjp(dc_ref[pl.ds(c * tr, tr), :])[0]

        pl.when(normed)(lambda: conv_output_cotangents(lambda yy: _gdn_post(yy, scale)))
        pl.when(jnp.logical_not(normed))(lambda: conv_output_cotangents(_silu))
        dws = [jnp.zeros((1, LANES), F32) for _ in range(CONV_K)]
        for c in range(s // tr):
            c0 = c * tr
            dy = dyp_ref[pl.ds(c0, tr), :]
            dx = w[0:1, :] * dyp_ref[pl.ds(c0 + CONV_K - 1, tr), :]
            for k in range(1, CONV_K):
                dx = dx + w[k:k + 1, :] * dyp_ref[pl.ds(c0 + CONV_K - 1 - k, tr), :]
            dx_ref[pl.ds(c0, tr), :] = dx.astype(BF16)
            for k in range(CONV_K):
                xs = xp_ref[pl.ds(c0 + CONV_PAD - (CONV_K - 1) + k, tr), :]
                dws[k] = dws[k] + jnp.sum(dy * xs, axis=0, keepdims=True)
        row = lax.broadcasted_iota(jnp.int32, (SUBLANES, LANES), 0)
        dwb = jnp.zeros((SUBLANES, LANES), F32)
        for k in range(CONV_K):
            dwb = dwb + jnp.where(row == k, dws[k], 0.0)
        dw_ref[...] = dwb

    return pl.pallas_call(
        body, name="gdn_conv_bwd", grid=(nblk,),
        in_specs=[pl.BlockSpec((s, LANES), lambda j: (0, col0 + j)), pl.BlockSpec((SUBLANES, LANES), lambda j: (0, j)),
                  pl.BlockSpec((s, LANES), lambda j: (0, j))],
        out_specs=(pl.BlockSpec((s, LANES), lambda j: (0, j)), pl.BlockSpec((SUBLANES, LANES), lambda j: (0, j))),
        out_shape=(jax.ShapeDtypeStruct((s, QKV_B), BF16), jax.ShapeDtypeStruct((SUBLANES, QKV_B), F32)),
        scratch_shapes=[pltpu.VMEM((s + CONV_PAD, LANES), F32), pltpu.VMEM((s + CONV_PAD, LANES), F32)],
        compiler_params=_params("parallel"),
    )(proj_r, conv8, dc)


def _gdn_chunk(q, k, v, bcol, acol, alog, dtb, gnw, state, t_saved=None):
    n = q.shape[-2]
    shp = (1, n, n)
    row = lax.broadcasted_iota(jnp.int32, shp, 1)
    col = lax.broadcasted_iota(jnp.int32, shp, 2)
    beta = _sigmoid(bcol)
    g = -jnp.exp(alog) * _softplus(acol + dtb)
    g_row = jnp.sum(jnp.where(row == col, g, 0.0), axis=-2, keepdims=True)
    big_g = jnp.sum(jnp.where(row >= col, g_row, 0.0), axis=-1, keepdims=True)
    big_g_row = jnp.sum(jnp.where(row <= col, g, 0.0), axis=-2, keepdims=True)
    decay_incl = jnp.exp(jnp.where(row >= col, big_g - big_g_row, -jnp.inf))
    decay_strict = jnp.where(row > col, decay_incl, 0.0)
    k_beta = k * beta
    a_mat = _mm_nt(k_beta, k) * decay_strict
    t_inv = _tri_inv(a_mat) if t_saved is None else _tri_inv_saved(a_mat, t_saved)
    e_g = jnp.exp(big_g)
    u = _mm(t_inv, v * beta)
    w = _mm(t_inv, k_beta * e_g)
    attn = _mm_nt(q, k) * decay_incl
    v_new = u - _mm(w, state)
    o = _mm(q * e_g, state) + _mm(attn, v_new)
    total = jnp.sum(g, axis=-2, keepdims=True)
    new_state = state * jnp.exp(total) + _mm_tn(k * jnp.exp(total - big_g), v_new)
    return _rmsnorm(o, gnw), new_state, t_inv


def _split_heads(x):
    return jnp.stack([x[:, h * HEAD_DIM:(h + 1) * HEAD_DIM] for h in range(HEADS)], axis=0)


def _merge_heads(x):
    return jnp.concatenate([x[h] for h in range(HEADS)], axis=1)


def _logit_columns(ba):
    lane = lax.broadcasted_iota(jnp.int32, ba.shape, 1)

    def cols(off):
        return jnp.stack([jnp.sum(jnp.where(lane == off + h, ba, 0.0), axis=1, keepdims=True) for h in range(HEADS)], axis=0)

    return cols(0), cols(HEADS)


def _logit_block(dbc, dac, shape):
    lane = lax.broadcasted_iota(jnp.int32, shape, 1)
    out = jnp.zeros(shape, F32)
    for h in range(HEADS):
        out = out + jnp.where(lane == h, dbc[h], 0.0) + jnp.where(lane == HEADS + h, dac[h], 0.0)
    return out


def _gdn_scan_fwd(cqkv, proj_r, ba_col, alog, dtb, gnw):
    s = cqkv.shape[0]
    nc = s // CHUNK
    span = SCAN_CHUNKS * CHUNK

    def body(q_ref, k_ref, v_ref, ba_ref, al_ref, dt_ref, gnw_ref, o_ref, st_ref, ti_ref, state_ref):
        @pl.when(pl.program_id(0) == 0)
        def _():
            state_ref[...] = jnp.zeros_like(state_ref)

        st = state_ref[...]
        for u in range(SCAN_CHUNKS):
            rows = pl.ds(u * CHUNK, CHUNK)
            st_ref[u] = st
            bcol, acol = _logit_columns(ba_ref[rows, :])
            o, st, t_inv = _gdn_chunk(_split_heads(q_ref[rows, :]), _split_heads(k_ref[rows, :]),
                                      _split_heads(v_ref[rows, :]), bcol, acol, al_ref[...], dt_ref[...], gnw_ref[...], st)
            o_ref[rows, :] = _merge_heads(o)
            ti_ref[u] = t_inv
        state_ref[...] = st

    part = lambda i: pl.BlockSpec((span, WIDTH), lambda n: (n, i))
    par = pl.BlockSpec((HEADS, 1, 1), lambda n: (0, 0, 0))
    per_chunk = pl.BlockSpec((SCAN_CHUNKS, HEADS, HEAD_DIM, HEAD_DIM), lambda n: (n, 0, 0, 0))
    per_chunk_shape = jax.ShapeDtypeStruct((nc, HEADS, HEAD_DIM, HEAD_DIM), F32)
    return pl.pallas_call(
        body, name="gdn_scan_fwd", grid=(nc // SCAN_CHUNKS,),
        in_specs=[part(0), part(1), part(2), pl.BlockSpec((span, LANES), lambda n: (n, ba_col)), par, par,
                  pl.BlockSpec((1, 1, HEAD_DIM), lambda n: (0, 0, 0))],
        out_specs=(part(0), per_chunk, per_chunk),
        out_shape=(jax.ShapeDtypeStruct((s, WIDTH), F32), per_chunk_shape, per_chunk_shape),
        scratch_shapes=[pltpu.VMEM((HEADS, HEAD_DIM, HEAD_DIM), F32)],
        compiler_params=_params("arbitrary"),
    )(cqkv, cqkv, cqkv, proj_r, alog, dtb, gnw)


def _gdn_scan_bwd(cqkv, proj_r, ba_col, alog, dtb, gnw, states, t_invs, do):
    s = cqkv.shape[0]
    nc = s // CHUNK
    span = SCAN_CHUNKS * CHUNK
    n_steps = nc // SCAN_CHUNKS

    def body(q_ref, k_ref, v_ref, ba_ref, al_ref, dt_ref, gnw_ref, st_ref, ti_ref, do_ref,
             dqkv_ref, dba_ref, dal_ref, ddt_ref, dgnw_ref, dstate_ref):
        @pl.when(pl.program_id(0) == 0)
        def _():
            dstate_ref[...] = jnp.zeros_like(dstate_ref)
            dal_ref[...] = jnp.zeros_like(dal_ref)
            ddt_ref[...] = jnp.zeros_like(ddt_ref)
            dgnw_ref[...] = jnp.zeros_like(dgnw_ref)

        dst = dstate_ref[...]
        for u in reversed(range(SCAN_CHUNKS)):
            rows = pl.ds(u * CHUNK, CHUNK)
            bcol, acol = _logit_columns(ba_ref[rows, :])
            _, vjp = jax.vjp(lambda *a, t_saved=ti_ref[u]: _gdn_chunk(*a, t_saved=t_saved)[:2],
                             _split_heads(q_ref[rows, :]), _split_heads(k_ref[rows, :]), _split_heads(v_ref[rows, :]),
                             bcol, acol, al_ref[...], dt_ref[...], gnw_ref[...], st_ref[u])
            dq, dk, dv, dbc, dac, dal, ddt, dgn, dst = vjp((_split_heads(do_ref[rows, :]), dst))
            dqkv_ref[rows, pl.ds(0, WIDTH)] = _merge_heads(dq)
            dqkv_ref[rows, pl.ds(WIDTH, WIDTH)] = _merge_heads(dk)
            dqkv_ref[rows, pl.ds(2 * WIDTH, WIDTH)] = _merge_heads(dv)
            dba_ref[rows, :] = _logit_block(dbc, dac, (CHUNK, LANES))
            dal_ref[...] += dal
            ddt_ref[...] += ddt
            dgnw_ref[...] += dgn
        dstate_ref[...] = dst

    rev = lambda n: n_steps - 1 - n
    part = lambda i: pl.BlockSpec((span, WIDTH), lambda n: (rev(n), i))
    par = pl.BlockSpec((HEADS, 1, 1), lambda n: (0, 0, 0))
    vec = pl.BlockSpec((1, 1, HEAD_DIM), lambda n: (0, 0, 0))
    par_shape = jax.ShapeDtypeStruct((HEADS, 1, 1), F32)
    per_chunk = pl.BlockSpec((SCAN_CHUNKS, HEADS, HEAD_DIM, HEAD_DIM), lambda n: (rev(n), 0, 0, 0))
    return pl.pallas_call(
        body, name="gdn_scan_bwd", grid=(n_steps,),
        in_specs=[part(0), part(1), part(2), pl.BlockSpec((span, LANES), lambda n: (rev(n), ba_col)), par, par, vec,
                  per_chunk, per_chunk, part(0)],
        out_specs=(pl.BlockSpec((span, QKV_B), lambda n: (rev(n), 0)), pl.BlockSpec((span, LANES), lambda n: (rev(n), 0)),
                   par, par, vec),
        out_shape=(jax.ShapeDtypeStruct((s, QKV_B), F32), jax.ShapeDtypeStruct((s, LANES), F32), par_shape, par_shape,
                   jax.ShapeDtypeStruct((1, 1, HEAD_DIM), F32)),
        scratch_shapes=[pltpu.VMEM((HEADS, HEAD_DIM, HEAD_DIM), F32)],
        compiler_params=_params("arbitrary"),
    )(cqkv, cqkv, cqkv, proj_r, alog, dtb, gnw, states, t_invs, do)


def _tail_loss(x, tgt, o0, o1, o2, l0, l1, l2, ga, gb, za, zb, ob, fnw, wua, wub, wo, tap_a, tap_b, tap_o):
    lm = jnp.maximum(jnp.maximum(l0, l1), l2)
    e0, e1, e2 = jnp.exp(l0 - lm), jnp.exp(l1 - lm), jnp.exp(l2 - lm)
    o_a = (e0 * o0 + e1 * o1 + e2 * o2) / (e0 + e1 + e2)
    xa, xb = o_a * _silu(za), ob * _silu(zb)
    y_a = _mm_x(xa, wua) + tap_a
    y_b = _mm_x(xb, wub) + tap_b
    merged = _sigmoid(ga) * y_a + _sigmoid(gb) * y_b
    y = _rmsnorm(x + _mm_x(merged, wo) + tap_o, fnw)
    err = y - tgt
    per_token = jnp.sum(err * err, axis=1, keepdims=True) * (0.5 / x.shape[1])
    return jnp.sum(per_token, axis=0, keepdims=True), (xa.astype(BF16), xb.astype(BF16), merged.astype(BF16))


def _tail(x, tgt, o_all, lse_all, og12, lg12, proj_r, ob, wua, wub, wo, fnw):
    s, d = x.shape
    tm = _row_block(s, TAIL_ROWS)
    col_za = 2 * d // WIDTH
    col_zb = (2 * d + WIDTH + QKV_B) // WIDTH

    def body(x_ref, t_ref, o0_ref, o1_ref, o2_ref, l0_ref, l1_ref, l2_ref, ga_ref, gb_ref, za_ref, zb_ref, ob_ref,
             wua_ref, wub_ref, wo_ref, fnw_ref,
             loss_ref, dx_ref, do0_ref, do1_ref, do2_ref, dl0_ref, dl1_ref, dl2_ref, dga_ref, dgb_ref, dza_ref,
             dzb_ref, dob_ref, xa_ref, xb_ref, mg_ref, dya_ref, dyb_ref, dmo_ref, dfnw_ref):
        @pl.when(pl.program_id(0) == 0)
        def _():
            for r in (loss_ref, dfnw_ref):
                r[...] = jnp.zeros_like(r)

        tap = jnp.zeros((tm, d), F32)
        args = (x_ref[...], t_ref[...], o0_ref[...], o1_ref[...], o2_ref[...], l0_ref[...], l1_ref[...], l2_ref[...],
                ga_ref[...], gb_ref[...], za_ref[...], zb_ref[...], ob_ref[...], fnw_ref[...],
                wua_ref[...], wub_ref[...], wo_ref[...], tap, tap, tap)
        loss, vjp, (xa16, xb16, mg16) = jax.vjp(_tail_loss, *args, has_aux=True)
        (dx, _, do0, do1, do2, dl0, dl1, dl2, dga, dgb, dza, dzb, dob, dfnw, _, _, _, dya, dyb, dmo) = vjp(
            jnp.ones((1, 1), F32))
        loss_ref[...] += jnp.broadcast_to(loss, loss_ref.shape)
        dx_ref[...] = dx
        do0_ref[...], do1_ref[...], do2_ref[...] = do0, do1, do2
        dl0_ref[...], dl1_ref[...], dl2_ref[...] = dl0, dl1, dl2
        dga_ref[...] = dga.astype(BF16)
        dgb_ref[...] = dgb.astype(BF16)
        dza_ref[...] = dza.astype(BF16)
        dzb_ref[...] = dzb.astype(BF16)
        dob_ref[...] = dob
        xa_ref[...], xb_ref[...], mg_ref[...] = xa16, xb16, mg16
        dya_ref[...] = dya.astype(BF16)
        dyb_ref[...] = dyb.astype(BF16)
        dmo_ref[...] = dmo.astype(BF16)
        dfnw_ref[...] += dfnw

    row = lambda w, c=0: pl.BlockSpec((tm, w), lambda i: (i, c))
    grp0 = pl.BlockSpec((None, tm, WIDTH), lambda i: (0, i, 0))
    full = lambda a, b: pl.BlockSpec((a, b), lambda i: (0, 0))
    f32 = lambda a, b: jax.ShapeDtypeStruct((a, b), F32)
    b16 = lambda a, b: jax.ShapeDtypeStruct((a, b), BF16)
    stacked = jax.ShapeDtypeStruct((GROUPS, s, WIDTH), F32)
    gspecs = [grp0, row(WIDTH), row(WIDTH)]
    in_specs = ([row(d), row(d)] + gspecs * 2 + [row(d, 0), row(d, 1), row(WIDTH, col_za), row(WIDTH, col_zb),
                row(WIDTH), full(WIDTH, d), full(WIDTH, d), full(d, d), full(1, d)])
    out_specs = ([full(SUBLANES, LANES), row(d)] + gspecs * 2 + [row(d), row(d), row(WIDTH), row(WIDTH), row(WIDTH),
                 row(WIDTH), row(WIDTH), row(d), row(d), row(d), row(d), full(1, d)])
    gshapes = [stacked, f32(s, WIDTH), f32(s, WIDTH)]
    out_shape = ([f32(SUBLANES, LANES), f32(s, d)] + gshapes * 2 + [b16(s, d), b16(s, d), b16(s, WIDTH),
                 b16(s, WIDTH), f32(s, WIDTH), b16(s, WIDTH), b16(s, WIDTH), b16(s, d), b16(s, d), b16(s, d), b16(s, d),
                 f32(1, d)])
    return pl.pallas_call(
        body, name="tail_fwd_bwd", grid=(s // tm,),
        in_specs=in_specs, out_specs=tuple(out_specs), out_shape=tuple(out_shape),
        compiler_params=pltpu.CompilerParams(dimension_semantics=("arbitrary",), vmem_limit_bytes=TAIL_VMEM_LIMIT),
    )(x, tgt, o_all, og12[0], og12[1], lse_all, lg12[0], lg12[1], proj_r, proj_r, proj_r, proj_r, ob, wua, wub, wo, fnw)


PERMUTE_SPAN = 4096


def _permute_span(s):
    return PERMUTE_SPAN if s % PERMUTE_SPAN == 0 else s


def _from_dilated_rows(stacked, g, dil, name):
    n_slots, s, c = stacked.shape
    view = stacked.reshape(n_slots, dil, s // dil, c)
    span = _permute_span(s)

    def body(in_ref, out_ref):
        for r in range(dil):
            out_ref[pl.ds(r, span // dil, stride=dil), :] = in_ref[r]

    return pl.pallas_call(
        body, name=name, grid=(s // span, c // LANES),
        in_specs=[pl.BlockSpec((None, dil, span // dil, LANES), lambda n, j: (g, 0, n, j))],
        out_specs=pl.BlockSpec((span, LANES), lambda n, j: (n, j)),
        out_shape=jax.ShapeDtypeStruct((s, c), stacked.dtype),
        compiler_params=_params("parallel", "parallel"),
    )(view)


def _to_dilated_rows_into(nat, stacked, g, dil, name):
    n_slots, s, c = stacked.shape
    view = stacked.reshape(n_slots, dil, s // dil, c)
    span = _permute_span(s)

    def body(nat_ref, old_ref, out_ref):
        for r in range(dil):
            out_ref[r] = nat_ref[pl.ds(r, span // dil, stride=dil), :]

    out = pl.pallas_call(
        body, name=name, grid=(s // span, c // LANES),
        in_specs=[pl.BlockSpec((span, LANES), lambda n, j: (n, j)), pl.BlockSpec(memory_space=pl.ANY)],
        out_specs=pl.BlockSpec((None, dil, span // dil, LANES), lambda n, j: (g, 0, n, j)),
        out_shape=jax.ShapeDtypeStruct(view.shape, stacked.dtype),
        input_output_aliases={1: 0},
        compiler_params=_params("parallel", "parallel"),
    )(nat, view)
    return out.reshape(stacked.shape)


def _to_dilated(a, dil):
    if dil == 1:
        return a
    s = a.shape[0]
    return a.reshape(s // dil, dil, -1).transpose(1, 0, 2).reshape(a.shape)


def _rope_tables(s):
    inv_freq = ROPE_THETA ** (-jnp.arange(0, HEAD_DIM, 2, dtype=F32) / HEAD_DIM)
    ang = jnp.arange(s, dtype=F32)[:, None] * inv_freq[None, :]
    cos_n = jnp.tile(jnp.cos(ang), (1, 2 * LANES // HEAD_DIM))
    sin_h = jnp.sin(ang)
    sin_n = jnp.tile(jnp.concatenate([-sin_h, sin_h], axis=1), (1, LANES // HEAD_DIM))

    def per_group(table, tag):
        out = jnp.broadcast_to(table, (GROUPS,) + table.shape)
        for g in range(1, GROUPS):
            out = _to_dilated_rows_into(table, out, g, DILATIONS[g], "rope_%s_to_dilated_%d" % (tag, g))
        return out

    return per_group(cos_n, "cos"), per_group(sin_n, "sin")


def _regroup_columns(pieces, widths):
    starts, pos = [], 0
    for p in pieces:
        starts.append(pos)
        pos += p.shape[1]
    assert pos == sum(widths), (pos, widths)
    out, lo = [], 0
    for w in widths:
        hi, parts = lo + w, []
        for p, st in zip(pieces, starts):
            a, b = max(lo, st), min(hi, st + p.shape[1])
            if a < b:
                parts.append(p[:, a - st:b - st])
        out.append(parts[0] if len(parts) == 1 else jnp.concatenate(parts, axis=1))
        lo = hi
    return out


def _pack_rows(parts, dtype, row_multiple):
    flat = jnp.concatenate([p.reshape(-1).astype(dtype) for p in parts])
    tile = row_multiple * LANES
    pad = (-flat.shape[0]) % tile
    return jnp.pad(flat, (0, pad)).reshape(-1, LANES)


def _unpack_rows(packed, shapes):
    flat = packed.reshape(-1)
    out, start = [], 0
    for shp in shapes:
        size = 1
        for n in shp:
            size *= n
        out.append(flat[start:start + size].reshape(shp))
        start += size
    return out


def kernel(x, norm_w, w_in, conv_w, a_log, dt_bias, gdn_norm_w, w_up_a, w_up_b, w_out, final_norm_w, loss_target, m_norm_w, m_w_in, m_conv_w, m_a_log, m_dt_bias, m_gdn_norm_w, m_w_up_a, m_w_up_b, m_w_out, m_final_norm_w, v_norm_w, v_w_in, v_conv_w, v_a_log, v_dt_bias, v_gdn_norm_w, v_w_up_a, v_w_up_b, v_w_out, v_final_norm_w):
    x2, tgt = x[0], loss_target[0]
    s, d = x2.shape
    me = 4 * lax.axis_index("x") + 2 * lax.axis_index("y") + lax.axis_index("c")
    win8 = w_in.shape[2]
    conv8w = conv_w.shape[2]

    conv_shard = jnp.pad(conv_w[0], ((0, SUBLANES - CONV_K), (0, 0)))
    w_in_g, wua_g, wub_g, wo_g, conv_g = _all_gather(
        [w_in[0].astype(BF16), w_up_a[0].astype(BF16), w_up_b[0].astype(BF16), w_out[0].astype(BF16), conv_shard])
    wua = jnp.concatenate([wua_g[i] for i in range(N_DEV)], axis=1)
    wub = jnp.concatenate([wub_g[i] for i in range(N_DEV)], axis=1)
    wo = wo_g.reshape(d, d)
    conv8 = jnp.concatenate([conv_g[i] for i in range(N_DEV)], axis=1)

    seg_widths = [QKV_B] * GROUPS + [WIDTH, QKV_B, WIDTH, 2 * HEADS, 2 * d]
    wq0, wq1, wq2, w_za, w_qkvb, w_zb, w_ba, w_gates = _regroup_columns([w_in_g[i] for i in range(N_DEV)], seg_widths)
    w_qkv = jnp.stack([wq0, wq1, wq2])
    w_rest = jnp.concatenate([w_gates, w_za, w_qkvb, w_zb, w_ba,
                              jnp.zeros((d, BA_PAD - 2 * HEADS), BF16)], axis=1)
    col_qkvb = (2 * d + WIDTH) // LANES
    col_ba = (2 * d + 2 * WIDTH + QKV_B) // LANES

    h = _rms_fwd(x2, norm_w)
    h_all = jnp.stack([_to_dilated(h, dil) for dil in DILATIONS])
    qkv_all = _matmul(h_all, w_qkv, F32, "in_proj_attention", tn=QKV_B)
    proj_r = _matmul(h[None], w_rest[None], F32, "in_proj_rest", tn=2560)[0]
    cos, sin = _rope_tables(s)
    o_all, lse_all = _attn_fwd(qkv_all, cos, sin)
    og12 = [_from_dilated_rows(o_all, g, DILATIONS[g], "attn_out_to_natural_%d" % g) for g in (1, 2)]
    lg12 = [_from_dilated_rows(lse_all, g, DILATIONS[g], "attn_lse_to_natural_%d" % g) for g in (1, 2)]

    cqkv = _gdn_pre_fwd(proj_r, conv8, col_qkvb)
    alog3, dtb3, gnw3 = a_log.reshape(HEADS, 1, 1), dt_bias.reshape(HEADS, 1, 1), gdn_norm_w.reshape(1, 1, HEAD_DIM)
    ob, states, t_invs = _gdn_scan_fwd(cqkv, proj_r, col_ba, alog3, dtb3, gnw3)

    (loss_blk, dx_res, do_all, do1, do2, dl_all, dl1, dl2, dga, dgb, dza, dzb, dob,
     xa16, xb16, mg16, dya16, dyb16, dmo16, dfnw) = _tail(
        x2, tgt, o_all, lse_all, og12, lg12, proj_r, ob, wua, wub, wo, final_norm_w.reshape(1, d))
    dwua = _matmul(xa16[None], dya16[None], F32, "up_a_dw", mode="tn", tk=2048)[0]
    dwub = _matmul(xb16[None], dyb16[None], F32, "up_b_dw", mode="tn", tk=2048)[0]
    dwo = _matmul(mg16[None], dmo16[None], F32, "out_proj_dw", mode="tn", tk=2048)[0]

    for g, (t_o, t_l) in ((1, (do1, dl1)), (2, (do2, dl2))):
        do_all = _to_dilated_rows_into(t_o, do_all, g, DILATIONS[g], "attn_dout_to_dilated_%d" % g)
        dl_all = _to_dilated_rows_into(t_l, dl_all, g, DILATIONS[g], "attn_dlse_to_dilated_%d" % g)
    dqkv_all = _attn_bwd(qkv_all, cos, sin, o_all, lse_all, do_all, dl_all)

    dcqkv, dba, dalog3, ddtb3, dgnw3 = _gdn_scan_bwd(cqkv, proj_r, col_ba, alog3, dtb3, gnw3, states, t_invs, dob)
    dqkv_b, dconv8 = _gdn_pre_bwd(proj_r, conv8, dcqkv, col_qkvb)
    dproj_r = jnp.concatenate([dga, dgb, dza, dqkv_b, dzb,
                               jnp.pad(dba.astype(BF16), ((0, 0), (0, BA_PAD - LANES)))], axis=1)

    def col_slabs(a, width):
        return jnp.stack([a[:, j * width:(j + 1) * width] for j in range(N_DEV)])

    core = lax.axis_index("c").astype(jnp.int32).reshape(1)
    small_slabs = [col_slabs(dwua, d // N_DEV), col_slabs(dwub, d // N_DEV), dwo.reshape(N_DEV, d // N_DEV, d)]
    dw_qkv, *small_sib = _matmul(h_all, dqkv_all, F32, "in_proj_attention_dw", mode="tn", tk=2048, tn=QKV_B,
                                 exchange=_sibling_exchange(small_slabs))
    small_partials = [_pair_sum(a, b, core, "grads_pair_sum_%d" % (i + 1))
                      for i, (a, b) in enumerate(zip(small_slabs, small_sib))]
    dw_rest, *small_contrib = _matmul(h[None], dproj_r[None], F32, "in_proj_rest_dw", mode="tn", tk=2048,
                                      exchange=_chip_exchange(small_partials))
    dw_rest = dw_rest[0]
    o2 = 2 * d
    dw_in_pieces = [dw_qkv[0], dw_qkv[1], dw_qkv[2],
                    dw_rest[:, o2:o2 + WIDTH], dw_rest[:, o2 + WIDTH:o2 + WIDTH + QKV_B],
                    dw_rest[:, o2 + WIDTH + QKV_B:o2 + 2 * WIDTH + QKV_B],
                    dw_rest[:, o2 + 2 * WIDTH + QKV_B:o2 + 2 * WIDTH + QKV_B + 2 * HEADS],
                    dw_rest[:, :o2]]

    w_in_slabs = jnp.stack(_regroup_columns(dw_in_pieces, [win8] * N_DEV))
    dh_a, w_in_sib = _matmul(dqkv_all, w_qkv, F32, "in_proj_attention_dh", mode="nt", tk=2048,
                             exchange=_sibling_exchange([w_in_slabs]))
    w_in_partial = _pair_sum(w_in_slabs, w_in_sib, core, "grads_pair_sum_0")
    dh_r, w_in_contrib = _matmul(dproj_r[None], w_rest[None], F32, "in_proj_rest_dh", mode="nt", tk=2560,
                                 exchange=_chip_exchange([w_in_partial]))
    contrib = [w_in_contrib] + small_contrib
    dh_parts = [dh_r[0]] + [_from_dilated_rows(dh_a, g, DILATIONS[g], "dh_to_natural_%d" % g) for g in (1, 2)]
    grad_x, dnorm_w = _rms_bwd(x2, norm_w, dh_a, dh_parts, dx_res)

    small_parts = [dnorm_w, dfnw, dconv8[:CONV_K], dalog3[:, 0, 0], ddtb3[:, 0, 0], dgnw3[0], loss_blk[0, 0:1]]
    small_rows = [-(-p.size // LANES) for p in small_parts]
    small = jnp.concatenate([jnp.pad(p.reshape(-1), (0, r * LANES - p.size)).reshape(r, LANES)
                             for p, r in zip(small_parts, small_rows)])
    small = jnp.pad(small, ((0, (-small.shape[0]) % SUBLANES), (0, 0)))
    small_sum = _small_all_reduce(small)
    pieces, r0 = [], 0
    for p, r in zip(small_parts, small_rows):
        pieces.append(small_sum[r0:r0 + r].reshape(-1)[:p.size].reshape(p.shape))
        r0 += r
    g_norm_w, g_fnw, g_conv_full, g_alog, g_dtb, g_gnw, loss_sum = pieces
    g_conv = lax.dynamic_slice(g_conv_full, (0, me * conv8w), (CONV_K, conv8w))

    big = [_adamw(c, w[0], m[0], v[0], name) for c, w, m, v, name in (
        (contrib[0], w_in, m_w_in, v_w_in, "adamw_w_in"), (contrib[1], w_up_a, m_w_up_a, v_w_up_a, "adamw_w_up_a"),
        (contrib[2], w_up_b, m_w_up_b, v_w_up_b, "adamw_w_up_b"), (contrib[3], w_out, m_w_out, v_w_out, "adamw_w_out"))]
    g_big, d_big, nm_big, nv_big = ([t[i] for t in big] for i in range(4))

    small_ws = [norm_w, final_norm_w, conv_w, a_log, dt_bias, gdn_norm_w]
    small_ms = [m_norm_w, m_final_norm_w, m_conv_w, m_a_log, m_dt_bias, m_gdn_norm_w]
    small_vs = [v_norm_w, v_final_norm_w, v_conv_w, v_a_log, v_dt_bias, v_gdn_norm_w]
    small_gs = [g_norm_w, g_fnw, g_conv, g_alog, g_dtb, g_gnw]
    small_shapes = [t.shape for t in small_ws]
    sm = _adamw(_pack_rows(small_gs, F32, SUBLANES)[None], _pack_rows(small_ws, F32, SUBLANES),
                _pack_rows(small_ms, F32, SUBLANES), _pack_rows(small_vs, F32, SUBLANES), "adamw_small")
    g_sm, d_sm, nm_sm, nv_sm = (_unpack_rows(t, small_shapes) for t in sm)

    def ordered(bigs, smalls):
        nw, fnw_, cw, al, dtb, gn = smalls
        wi, ua, ub, wo_ = (t[None] for t in bigs)
        return [nw, wi, cw, al, dtb, gn, ua, ub, wo_, fnw_]

    return (loss_sum.reshape(()), grad_x[None], *ordered(g_big, g_sm), *ordered(d_big, d_sm),
            *ordered(nm_big, nm_sm), *ordered(nv_big, nv_sm))
```

```python
import functools

import jax
import jax.numpy as jnp
from jax import lax
from jax.experimental import pallas as pl
from jax.experimental.pallas import tpu as pltpu

F32 = jnp.float32
BF16 = jnp.bfloat16
MESH = pl.DeviceIdType.MESH
N_DEV = 8
LANES = 128
SUBLANES = 8

GROUPS = 3
HEADS = 8
HEAD_DIM = 64
WIDTH = HEADS * HEAD_DIM
ATT_BLOCK = 128
ATT_HALF = ATT_BLOCK // 2
DILATIONS = (1, 4, 16)
N_BACK = 128
CONV_K = 4
CHUNK = 64
SCAN_CHUNKS = 4
QKV_B = 3 * WIDTH
QKV_A = GROUPS * 3 * WIDTH
BA_PAD = 512
NORM_EPS = 1e-6
ROPE_THETA = 10000.0
ADAM_LR, ADAM_B1, ADAM_B2, ADAM_EPS, ADAM_WD, ADAM_STEP = 0.001, 0.9, 0.999, 1e-08, 0.01, 10

VMEM_LIMIT = 56 * 1024 * 1024
TAIL_ROWS = 256
TAIL_VMEM_LIMIT = 62 * 1024 * 1024


def _params(*sem):
    return pltpu.CompilerParams(dimension_semantics=sem, vmem_limit_bytes=VMEM_LIMIT)


def _dg(a, b, ca, cb):
    nb = a.ndim - 2
    batch = tuple(range(nb))
    return lax.dot_general(a, b, (((nb + ca,), (nb + cb,)), (batch, batch)), preferred_element_type=F32)


@jax.custom_vjp
def _mm(a, b):
    return _dg(a.astype(BF16), b.astype(BF16), 1, 0)


def _mm_fwd(a, b):
    return _mm(a, b), (a.astype(BF16), b.astype(BF16))


def _mm_bwd(res, ct):
    a16, b16 = res
    c16 = ct.astype(BF16)
    return _dg(c16, b16, 1, 1), _dg(a16, c16, 0, 0)


_mm.defvjp(_mm_fwd, _mm_bwd)


@jax.custom_vjp
def _mm_nt(a, b):
    return _dg(a.astype(BF16), b.astype(BF16), 1, 1)


def _mm_nt_fwd(a, b):
    return _mm_nt(a, b), (a.astype(BF16), b.astype(BF16))


def _mm_nt_bwd(res, ct):
    a16, b16 = res
    c16 = ct.astype(BF16)
    return _dg(c16, b16, 1, 0), _dg(c16, a16, 0, 0)


_mm_nt.defvjp(_mm_nt_fwd, _mm_nt_bwd)


@jax.custom_vjp
def _mm_tn(a, b):
    return _dg(a.astype(BF16), b.astype(BF16), 0, 0)


def _mm_tn_fwd(a, b):
    return _mm_tn(a, b), (a.astype(BF16), b.astype(BF16))


def _mm_tn_bwd(res, ct):
    a16, b16 = res
    c16 = ct.astype(BF16)
    return _dg(b16, c16, 1, 1), _dg(a16, c16, 1, 0)


_mm_tn.defvjp(_mm_tn_fwd, _mm_tn_bwd)


@jax.custom_vjp
def _mm_x(a, w16):
    return _dg(a.astype(BF16), w16, 1, 0)


def _mm_x_fwd(a, w16):
    return _mm_x(a, w16), w16


def _mm_x_bwd(w16, ct):
    return _dg(ct.astype(BF16), w16, 1, 1), jnp.zeros_like(w16)


_mm_x.defvjp(_mm_x_fwd, _mm_x_bwd)


def _split16(a):
    hi = a.astype(BF16)
    lo = (a - hi.astype(F32)).astype(BF16)
    return hi, lo


def _dot3(a, b, ca, cb):
    ah, al = _split16(a)
    bh, bl = _split16(b)
    return _dg(ah, bh, ca, cb) + (_dg(ah, bl, ca, cb) + _dg(al, bh, ca, cb))


def _tri_inv_impl(a):
    n = a.shape[-1]
    shp = (1,) * (a.ndim - 2) + (n, n)
    eye = (lax.broadcasted_iota(jnp.int32, shp, a.ndim - 2) == lax.broadcasted_iota(jnp.int32, shp, a.ndim - 1)).astype(F32)
    x = eye - a
    p = a
    for it in range(5):
        dot = _dot3 if it < 2 else (lambda u, v, cu, cv: _dg(u.astype(BF16), v.astype(BF16), cu, cv))
        p = dot(p, p, 1, 0)
        x = x + dot(x, p, 1, 0)
    return x


@jax.custom_vjp
def _tri_inv(a):
    return _tri_inv_impl(a)


def _tri_inv_fwd(a):
    t = _tri_inv_impl(a)
    return t, t


def _tri_inv_bwd(t, ct):
    t16 = t.astype(BF16)
    return (-_dg(_dg(t16, ct.astype(BF16), 0, 0).astype(BF16), t16, 1, 1),)


_tri_inv.defvjp(_tri_inv_fwd, _tri_inv_bwd)


@jax.custom_vjp
def _tri_inv_saved(a, t):
    return t


def _tri_inv_saved_fwd(a, t):
    return t, t


def _tri_inv_saved_bwd(t, ct):
    return _tri_inv_bwd(t, ct) + (jnp.zeros_like(t),)


_tri_inv_saved.defvjp(_tri_inv_saved_fwd, _tri_inv_saved_bwd)


def _sigmoid(x):
    return 1.0 / (1.0 + jnp.exp(-x))


def _silu(x):
    return x * _sigmoid(x)


def _softplus(x):
    return jnp.maximum(x, 0.0) + jnp.log(1.0 + jnp.exp(-jnp.abs(x)))


def _rmsnorm(x, w):
    return x * lax.rsqrt(jnp.mean(x * x, axis=-1, keepdims=True) + NORM_EPS) * w


def _row_block(rows, cap):
    best = None
    for cand in range(SUBLANES, min(rows, cap) + 1, SUBLANES):
        if rows % cand == 0:
            best = cand
    assert best is not None, rows
    return best


def _mesh_peers():
    x, y, c = lax.axis_index("x"), lax.axis_index("y"), lax.axis_index("c")
    me = 4 * x + 2 * y + c
    peers = []
    for k in range(1, N_DEV):
        px = 1 - x if (k >> 2) & 1 else x
        py = 1 - y if (k >> 1) & 1 else y
        pc = 1 - c if k & 1 else c
        peers.append(((px, py, pc), 4 * px + 2 * py + pc))
    return me, peers


N_CHIPS = 4
OTHER_CHIPS = 3


def _chip_peers():
    x, y, c = lax.axis_index("x"), lax.axis_index("y"), lax.axis_index("c")
    return x, y, c, [(1 - x, y), (x, 1 - y), (1 - x, 1 - y)]


def _all_gather(shards):
    n_arr = len(shards)
    per = 1 + 2 * OTHER_CHIPS

    def body(*refs):
        in_refs, out_refs = refs[:n_arr], refs[n_arr:2 * n_arr]
        send_sems, recv_sems, loc_sems = refs[2 * n_arr:]
        x, y, c, chips = _chip_peers()
        me, sibling = (x, y, c), (x, y, 1 - c)

        def slot(px, py, pc):
            return 4 * px + 2 * py + pc

        def copy(i, k, block, to, src=None):
            dst = out_refs[i].at[slot(*block)]
            return pltpu.make_async_remote_copy(src_ref=dst if src is None else src, dst_ref=dst,
                                                send_sem=send_sems.at[i * per + k], recv_sem=recv_sems.at[i * per + k],
                                                device_id=to, device_id_type=MESH)

        own = [pltpu.make_async_copy(in_refs[i], out_refs[i].at[slot(*me)], loc_sems.at[i]) for i in range(n_arr)]
        for cp in own:
            cp.start()
        first = []
        for i in range(n_arr):
            first += [copy(i, 1 + j, me, (*chip, c), src=in_refs[i]) for j, chip in enumerate(chips)]
            first.append(copy(i, 0, me, sibling, src=in_refs[i]))
        for cp in first:
            cp.start()
        passed = []
        for j, chip in enumerate(chips):
            for i in range(n_arr):
                copy(i, 1 + j, (*chip, c), me).wait_recv()
                fwd = copy(i, 1 + OTHER_CHIPS + j, (*chip, c), sibling)
                fwd.start()
                passed.append(fwd)
        for i in range(n_arr):
            copy(i, 0, sibling, me).wait_recv()
            for j, chip in enumerate(chips):
                copy(i, 1 + OTHER_CHIPS + j, (*chip, 1 - c), me).wait_recv()
        for cp in first + passed:
            cp.wait_send()
        for cp in own:
            cp.wait()

    any_spec = pl.BlockSpec(memory_space=pl.ANY)
    return pl.pallas_call(
        body, name="weights_all_gather",
        out_shape=tuple(jax.ShapeDtypeStruct((N_DEV,) + a.shape, a.dtype) for a in shards),
        in_specs=[any_spec] * n_arr, out_specs=tuple([any_spec] * n_arr),
        scratch_shapes=[pltpu.SemaphoreType.DMA((n_arr * per,)), pltpu.SemaphoreType.DMA((n_arr * per,)),
                        pltpu.SemaphoreType.DMA((n_arr,))],
    )(*shards)


class _Exchange:
    def __init__(self, arrays, out_shapes, n_sem, copies):
        self.arrays, self.out_shapes, self.n_sem, self.copies = list(arrays), list(out_shapes), n_sem, copies


def _sibling_exchange(slabs):
    n_arr = len(slabs)

    def copies(in_refs, out_refs, send_sems, recv_sems, loc_sems):
        x, y, c, _ = _chip_peers()
        sends = [pltpu.make_async_remote_copy(src_ref=in_refs[i].at[2 * q + (1 - c)], dst_ref=out_refs[i].at[q],
                                              send_sem=send_sems.at[i * N_CHIPS + q], recv_sem=recv_sems.at[i * N_CHIPS + q],
                                              device_id=(x, y, 1 - c), device_id_type=MESH)
                 for i in range(n_arr) for q in range(N_CHIPS)]

        def start():
            for cp in sends:
                cp.start()

        def finish():
            for cp in sends:
                cp.wait_recv()
            for cp in sends:
                cp.wait_send()

        return start, finish

    return _Exchange(slabs, [jax.ShapeDtypeStruct((N_CHIPS,) + a.shape[1:], a.dtype) for a in slabs],
                     n_arr * N_CHIPS, copies)


def _pair_sum(slabs, from_sibling, core, name):
    _, rows, cols = slabs.shape
    tr = _row_block(rows, max(SUBLANES, (256 * 1024) // cols // SUBLANES * SUBLANES))

    def body(core_ref, a_ref, b_ref, o_ref):
        o_ref[...] = (a_ref[...] + b_ref[...]).astype(BF16)

    grid_spec = pltpu.PrefetchScalarGridSpec(
        num_scalar_prefetch=1, grid=(N_CHIPS, rows // tr),
        in_specs=[pl.BlockSpec((None, tr, cols), lambda q, r, core_ref: (2 * q + core_ref[0], r, 0)),
                  pl.BlockSpec((None, tr, cols), lambda q, r, core_ref: (q, r, 0))],
        out_specs=pl.BlockSpec((None, tr, cols), lambda q, r, core_ref: (q, r, 0)))
    return pl.pallas_call(
        body, name=name, grid_spec=grid_spec,
        out_shape=jax.ShapeDtypeStruct((N_CHIPS, rows, cols), BF16),
        compiler_params=_params("parallel", "parallel"),
    )(core, slabs, from_sibling)


def _chip_exchange(partials):
    n_arr = len(partials)

    def copies(in_refs, out_refs, send_sems, recv_sems, loc_sems):
        x, y, c, chips = _chip_peers()
        mine = 2 * x + y
        own = [pltpu.make_async_copy(in_refs[i].at[mine], out_refs[i].at[mine], loc_sems.at[i]) for i in range(n_arr)]

        def copy(i, j, chip, src_slot, dst_slot):
            return pltpu.make_async_remote_copy(src_ref=in_refs[i].at[src_slot], dst_ref=out_refs[i].at[dst_slot],
                                                send_sem=send_sems.at[i * OTHER_CHIPS + j],
                                                recv_sem=recv_sems.at[i * OTHER_CHIPS + j],
                                                device_id=(*chip, c), device_id_type=MESH)

        sends = [copy(i, j, chip, 2 * chip[0] + chip[1], mine) for j, chip in enumerate(chips) for i in range(n_arr)]
        recvs = [copy(i, j, chip, mine, 2 * chip[0] + chip[1]) for j, chip in enumerate(chips) for i in range(n_arr)]

        def start():
            for cp in own + sends:
                cp.start()

        def finish():
            for cp in recvs:
                cp.wait_recv()
            for cp in sends:
                cp.wait_send()
            for cp in own:
                cp.wait()

        return start, finish

    return _Exchange(partials, [jax.ShapeDtypeStruct(a.shape, a.dtype) for a in partials], n_arr * OTHER_CHIPS, copies)


def _small_all_reduce(part):
    rows = part.shape[0]

    def body(p_ref, o_ref, buf_ref, send_sems, recv_sems):
        me, peers = _mesh_peers()
        buf_ref[me] = p_ref[...]
        sends = []
        for k, (dev, pid) in enumerate(peers):
            cp = pltpu.make_async_remote_copy(src_ref=p_ref, dst_ref=buf_ref.at[me], send_sem=send_sems.at[k],
                                              recv_sem=recv_sems.at[k], device_id=dev, device_id_type=MESH)
            cp.start()
            sends.append(cp)
        for k, (dev, pid) in enumerate(peers):
            pltpu.make_async_remote_copy(src_ref=p_ref, dst_ref=buf_ref.at[pid], send_sem=send_sems.at[k],
                                         recv_sem=recv_sems.at[k], device_id=dev, device_id_type=MESH).wait_recv()
        for cp in sends:
            cp.wait_send()
        acc = buf_ref[0]
        for i in range(1, N_DEV):
            acc = acc + buf_ref[i]
        o_ref[...] = acc

    vmem = pl.BlockSpec(memory_space=pltpu.VMEM)
    return pl.pallas_call(
        body, name="small_all_reduce",
        out_shape=jax.ShapeDtypeStruct(part.shape, F32),
        in_specs=[vmem], out_specs=vmem,
        scratch_shapes=[pltpu.VMEM((N_DEV, rows, LANES), F32), pltpu.SemaphoreType.DMA((N_DEV - 1,)),
                        pltpu.SemaphoreType.DMA((N_DEV - 1,))],
    )(part)


def _adamw_vals(w, g, m, v):
    m = ADAM_B1 * m + (1.0 - ADAM_B1) * g
    v = ADAM_B2 * v + (1.0 - ADAM_B2) * (g * g)
    m_hat = m / (1.0 - ADAM_B1 ** ADAM_STEP)
    v_hat = v / (1.0 - ADAM_B2 ** ADAM_STEP)
    delta = -ADAM_LR * (m_hat / (jnp.sqrt(v_hat) + ADAM_EPS) + ADAM_WD * w)
    return delta, m, v


def _adamw(contrib, w, m, v, name):
    n, rows, cols = contrib.shape
    tr = _row_block(rows, max(SUBLANES, (128 * 1024) // cols // SUBLANES * SUBLANES))

    def body(c_ref, w_ref, m_ref, v_ref, g_ref, d_ref, nm_ref, nv_ref):
        g = c_ref[0].astype(F32)
        for i in range(1, n):
            g = g + c_ref[i].astype(F32)
        delta, nm, nv = _adamw_vals(w_ref[...], g, m_ref[...], v_ref[...])
        g_ref[...] = g
        d_ref[...] = delta
        nm_ref[...] = nm
        nv_ref[...] = nv

    row = pl.BlockSpec((tr, cols), lambda i: (i, 0))
    shp = jax.ShapeDtypeStruct((rows, cols), F32)
    return pl.pallas_call(
        body, name=name, grid=(rows // tr,),
        in_specs=[pl.BlockSpec((n, tr, cols), lambda i: (0, i, 0)), row, row, row],
        out_specs=(row, row, row, row), out_shape=(shp, shp, shp, shp),
        compiler_params=_params("parallel"),
    )(contrib, w, m, v)


def _lane_block(n, cap):
    if n <= cap:
        return n
    best = None
    for cand in range(LANES, cap + 1, LANES):
        if n % cand == 0:
            best = cand
    assert best is not None, n
    return best


def _matmul(a, b, out_dtype, name, mode="nn", tm=1024, tn=1024, tk=1024, exchange=None):
    g = a.shape[0]
    m, k = (a.shape[2], a.shape[1]) if mode == "tn" else (a.shape[1], a.shape[2])
    n = b.shape[1] if mode == "nt" else b.shape[2]
    tm, tn, tk = _lane_block(m, tm), _lane_block(n, tn), _lane_block(k, tk)
    nk = k // tk
    grid = (g, m // tm, n // tn, nk)
    a_spec = (pl.BlockSpec((None, tk, tm), lambda gi, i, j, kk: (gi, kk, i)) if mode == "tn" else
              pl.BlockSpec((None, tm, tk), lambda gi, i, j, kk: (gi, i, kk)))
    b_spec = (pl.BlockSpec((None, tn, tk), lambda gi, i, j, kk: (gi, j, kk)) if mode == "nt" else
              pl.BlockSpec((None, tk, tn), lambda gi, i, j, kk: (gi, kk, j)))
    ca, cb = (0 if mode == "tn" else 1), (1 if mode == "nt" else 0)
    n_ex = 0 if exchange is None else len(exchange.arrays)

    def body(a_ref, b_ref, *rest):
        ex_in, o_ref, ex_out, scratch = rest[:n_ex], rest[n_ex], rest[n_ex + 1:2 * n_ex + 1], rest[2 * n_ex + 1:]
        if exchange is not None:
            start, finish = exchange.copies(ex_in, ex_out, *scratch[-3:])
            pids = [pl.program_id(ax) for ax in range(4)]
            pl.when((pids[0] == 0) & (pids[1] == 0) & (pids[2] == 0) & (pids[3] == 0))(start)
        part = _dg(a_ref[...], b_ref[...], ca, cb)
        if nk == 1:
            o_ref[...] = part.astype(o_ref.dtype)
        else:
            acc_ref = scratch[0]
            kk = pl.program_id(3)

            @pl.when(kk == 0)
            def _():
                acc_ref[...] = part

            @pl.when((kk > 0) & (kk < nk - 1))
            def _():
                acc_ref[...] += part

            @pl.when(kk == nk - 1)
            def _():
                o_ref[...] = (acc_ref[...] + part).astype(o_ref.dtype)
        if exchange is not None:
            pl.when((pids[0] == grid[0] - 1) & (pids[1] == grid[1] - 1) & (pids[2] == grid[2] - 1)
                    & (pids[3] == grid[3] - 1))(finish)

    any_spec = pl.BlockSpec(memory_space=pl.ANY)
    scratch_shapes = [] if nk == 1 else [pltpu.VMEM((tm, tn), F32)]
    out_shape = [jax.ShapeDtypeStruct((g, m, n), out_dtype)]
    if exchange is not None:
        scratch_shapes += [pltpu.SemaphoreType.DMA((exchange.n_sem,)), pltpu.SemaphoreType.DMA((exchange.n_sem,)),
                           pltpu.SemaphoreType.DMA((n_ex,))]
        out_shape += exchange.out_shapes
    outs = pl.pallas_call(
        body, name=name, grid=grid,
        in_specs=[a_spec, b_spec] + [any_spec] * n_ex,
        out_specs=tuple([pl.BlockSpec((None, tm, tn), lambda gi, i, j, kk: (gi, i, j))] + [any_spec] * n_ex),
        out_shape=tuple(out_shape),
        scratch_shapes=scratch_shapes,
        compiler_params=(_params("parallel", "parallel", "parallel", "arbitrary") if exchange is None else
                         _params("arbitrary", "arbitrary", "arbitrary", "arbitrary")),
    )(a, b, *([] if exchange is None else exchange.arrays))
    return outs[0] if exchange is None else outs


def _rms_fwd(x, w):
    s, d = x.shape
    tm = _row_block(s, 512)

    def body(x_ref, w_ref, h_ref):
        h_ref[...] = _rmsnorm(x_ref[...], w_ref[...]).astype(BF16)

    return pl.pallas_call(
        body, name="input_rmsnorm", grid=(s // tm,),
        in_specs=[pl.BlockSpec((tm, d), lambda i: (i, 0)), pl.BlockSpec((1, d), lambda i: (0, 0))],
        out_specs=pl.BlockSpec((tm, d), lambda i: (i, 0)),
        out_shape=jax.ShapeDtypeStruct((s, d), BF16),
        compiler_params=_params("parallel"),
    )(x, w)


def _rms_bwd(x, w, dh_stacked, dh_parts, dx_res):
    s, d = x.shape
    tm = _row_block(s, 256)
    n_parts = 1 + len(dh_parts)

    def body(x_ref, w_ref, *rest):
        part_refs = rest[:n_parts]
        res_ref, gx_ref, gw_ref = rest[n_parts:]
        dh = part_refs[0][...]
        for r in part_refs[1:]:
            dh = dh + r[...]
        _, vjp = jax.vjp(_rmsnorm, x_ref[...], w_ref[...])
        dx, dw = vjp(dh)
        gx_ref[...] = dx + res_ref[...]

        @pl.when(pl.program_id(0) == 0)
        def _():
            gw_ref[...] = jnp.zeros_like(gw_ref)

        gw_ref[...] += dw

    row = pl.BlockSpec((tm, d), lambda i: (i, 0))
    vec = pl.BlockSpec((1, d), lambda i: (0, 0))
    return pl.pallas_call(
        body, name="input_rmsnorm_bwd", grid=(s // tm,),
        in_specs=[row, vec, pl.BlockSpec((None, tm, d), lambda i: (0, i, 0))] + [row] * (n_parts - 1) + [row],
        out_specs=(row, vec),
        out_shape=(jax.ShapeDtypeStruct((s, d), F32), jax.ShapeDtypeStruct((1, d), F32)),
        compiler_params=_params("arbitrary"),
    )(x, w, dh_stacked, *dh_parts, dx_res)


def _lane_masks(rows):
    lane = lax.broadcasted_iota(jnp.int32, (rows, LANES), 1)
    return lane < HEAD_DIM, (lane & (HEAD_DIM - 1)) < HEAD_DIM // 2


def _swap_halves(t, lo_half):
    return jnp.where(lo_half, pltpu.roll(t, LANES - HEAD_DIM // 2, 1), pltpu.roll(t, HEAD_DIM // 2, 1))


def _rope(t, cos, sin_signed, lo_half):
    return t * cos + _swap_halves(t, lo_half) * sin_signed


def _rope_bwd(d, cos, sin_signed, lo_half):
    return d * cos - _swap_halves(d, lo_half) * sin_signed


def _window_mask(first):
    qi = lax.broadcasted_iota(jnp.int32, (2 * ATT_BLOCK, 2 * ATT_BLOCK), 0) & (ATT_BLOCK - 1)
    kj = lax.broadcasted_iota(jnp.int32, (2 * ATT_BLOCK, 2 * ATT_BLOCK), 1)
    dist = qi + ATT_BLOCK - kj
    return (dist >= 0) & (dist <= N_BACK) & ((kj >= ATT_BLOCK) | jnp.logical_not(first))


def _stack_heads(t, head0):
    zero = jnp.zeros_like(t)
    return jnp.concatenate([jnp.where(head0, t, zero), jnp.where(head0, zero, t)], axis=0)


def _unstack_heads(t2, head0):
    return jnp.where(head0, t2[:ATT_BLOCK], t2[ATT_BLOCK:])


def _blocks_per_subsequence(g, nb):
    return lax.shift_right_logical(jnp.int32(nb), 2 * g)


def _attn_fwd(qkv, cos, sin):
    _, s, _ = qkv.shape
    nb = s // ATT_BLOCK

    def body(qkv_ref, cos_ref, sin_ref, o_ref, lse_ref, kp_ref, vp_ref):
        g, t = pl.program_id(0), pl.program_id(1)
        first = (t & (_blocks_per_subsequence(g, nb) - 1)) == 0

        @pl.when(first)
        def _():
            kp_ref[...] = jnp.zeros_like(kp_ref)
            vp_ref[...] = jnp.zeros_like(vp_ref)

        cos_b, sin_b = cos_ref[...], sin_ref[...]
        head0, lo_half = _lane_masks(ATT_BLOCK)
        valid = _window_mask(first)
        for sl in range(WIDTH // LANES):
            cq = pl.ds(sl * LANES, LANES)
            ck = pl.ds(WIDTH + sl * LANES, LANES)
            cv = pl.ds(2 * WIDTH + sl * LANES, LANES)
            qr = (_rope(qkv_ref[:, cq], cos_b, sin_b, lo_half) * (HEAD_DIM ** -0.5)).astype(BF16)
            kr = _rope(qkv_ref[:, ck], cos_b, sin_b, lo_half).astype(BF16)
            v16 = qkv_ref[:, cv].astype(BF16)
            kcat = jnp.concatenate([kp_ref[:, cq], kr], axis=0)
            vcat = jnp.concatenate([vp_ref[:, cq], v16], axis=0)
            sc = jnp.where(valid, _dg(_stack_heads(qr, head0), kcat, 1, 1), -jnp.inf)
            mx = jnp.max(sc, axis=1, keepdims=True)
            p = jnp.exp(sc - mx)
            den = jnp.sum(p, axis=1, keepdims=True)
            o_ref[:, cq] = _unstack_heads(_dg((p * (1.0 / den)).astype(BF16), vcat, 1, 0), head0)
            lse2 = mx + jnp.log(den)
            lse_ref[:, cq] = jnp.where(head0, lse2[:ATT_BLOCK], lse2[ATT_BLOCK:])
            kp_ref[:, cq] = kr
            vp_ref[:, cq] = v16

    blk = lambda w: pl.BlockSpec((None, ATT_BLOCK, w), lambda g, t: (g, t, 0))
    shp = jax.ShapeDtypeStruct((GROUPS, s, WIDTH), F32)
    return pl.pallas_call(
        body, name="dilated_attention_fwd", grid=(GROUPS, nb),
        in_specs=[blk(3 * WIDTH), blk(LANES), blk(LANES)],
        out_specs=(blk(WIDTH), blk(WIDTH)), out_shape=(shp, shp),
        scratch_shapes=[pltpu.VMEM((ATT_BLOCK, WIDTH), BF16), pltpu.VMEM((ATT_BLOCK, WIDTH), BF16)],
        compiler_params=_params("arbitrary", "arbitrary"),
    )(qkv, cos, sin)


def _attn_bwd(qkv, cos, sin, o, lse, do, dlse):
    _, s, _ = qkv.shape
    nb = s // ATT_BLOCK

    def body(qkv_ref, cos_ref, sin_ref, cosp_ref, sinp_ref, o_ref, lse_ref, do_ref, dlse_ref,
             dqkv_ref, kp_ref, vp_ref, dka_ref, dva_ref, dqp_ref):
        g, t = pl.program_id(0), pl.program_id(1)
        first = (t & (_blocks_per_subsequence(g, nb) - 1)) == 0
        active = t < nb
        head0, lo_half = _lane_masks(ATT_BLOCK)
        cos_p, sin_p = cosp_ref[...], sinp_ref[...]

        @pl.when(t == 0)
        def _():
            dka_ref[...] = jnp.zeros_like(dka_ref)
            dva_ref[...] = jnp.zeros_like(dva_ref)
            dqp_ref[...] = jnp.zeros_like(dqp_ref)

        dqkv_ref[:, pl.ds(0, WIDTH)] = dqp_ref[...]

        @pl.when(active & first)
        def _():
            kp_ref[...] = jnp.zeros_like(kp_ref)
            vp_ref[...] = jnp.zeros_like(vp_ref)

        @pl.when(active)
        def _():
            cos_b, sin_b = cos_ref[...], sin_ref[...]
            valid = _window_mask(first)
            for sl in range(WIDTH // LANES):
                cq = pl.ds(sl * LANES, LANES)
                ck = pl.ds(WIDTH + sl * LANES, LANES)
                cv = pl.ds(2 * WIDTH + sl * LANES, LANES)
                qr = (_rope(qkv_ref[:, cq], cos_b, sin_b, lo_half) * (HEAD_DIM ** -0.5)).astype(BF16)
                kr = _rope(qkv_ref[:, ck], cos_b, sin_b, lo_half).astype(BF16)
                v16 = qkv_ref[:, cv].astype(BF16)
                kcat = jnp.concatenate([kp_ref[:, cq], kr], axis=0)
                vcat = jnp.concatenate([vp_ref[:, cq], v16], axis=0)
                do_b = do_ref[:, cq]
                do16 = do_b.astype(BF16)
                lse_b = lse_ref[:, cq]
                cterm = dlse_ref[:, cq] - do_b * o_ref[:, cq]
                dqs, dkc, dvc = [], None, None
                for hm in (head0, jnp.logical_not(head0)):
                    qm = jnp.where(hm, qr, jnp.zeros_like(qr))
                    dom = jnp.where(hm, do16, jnp.zeros_like(do16))
                    lse_h = jnp.max(jnp.where(hm, lse_b, -jnp.inf), axis=1, keepdims=True)
                    c = jnp.sum(jnp.where(hm, cterm, 0.0), axis=1, keepdims=True)
                    dq_halves = []
                    for r0 in range(0, ATT_BLOCK, ATT_HALF):
                        qh, doh = qm[r0:r0 + ATT_HALF], dom[r0:r0 + ATT_HALF]
                        sc = jnp.where(valid[r0:r0 + ATT_HALF], _dg(qh, kcat, 1, 1), -jnp.inf)
                        p = jnp.exp(sc - lse_h[r0:r0 + ATT_HALF])
                        ds16 = (p * (_dg(doh, vcat, 1, 1) + c[r0:r0 + ATT_HALF])).astype(BF16)
                        dv_h, dk_h = _dg(p.astype(BF16), doh, 0, 0), _dg(ds16, qh, 0, 0)
                        dvc = dv_h if dvc is None else dvc + dv_h
                        dkc = dk_h if dkc is None else dkc + dk_h
                        dq_halves.append(_dg(ds16, kcat, 1, 0))
                    dqs.append(jnp.concatenate(dq_halves, axis=0))
                dq = jnp.where(head0, dqs[0], dqs[1]) * (HEAD_DIM ** -0.5)
                dqp_ref[:, cq] = _rope_bwd(dq, cos_b, sin_b, lo_half).astype(BF16)
                dqkv_ref[:, ck] = _rope_bwd(dka_ref[:, cq] + dkc[:ATT_BLOCK], cos_p, sin_p, lo_half).astype(BF16)
                dqkv_ref[:, cv] = (dva_ref[:, cq] + dvc[:ATT_BLOCK]).astype(BF16)
                dka_ref[:, cq] = dkc[ATT_BLOCK:]
                dva_ref[:, cq] = dvc[ATT_BLOCK:]
                kp_ref[:, cq] = kr
                vp_ref[:, cq] = v16

        @pl.when(jnp.logical_not(active))
        def _():
            for sl in range(WIDTH // LANES):
                cq = pl.ds(sl * LANES, LANES)
                dqkv_ref[:, pl.ds(WIDTH + sl * LANES, LANES)] = _rope_bwd(dka_ref[:, cq], cos_p, sin_p, lo_half).astype(BF16)
                dqkv_ref[:, pl.ds(2 * WIDTH + sl * LANES, LANES)] = dva_ref[:, cq].astype(BF16)

    cur = lambda w: pl.BlockSpec((None, ATT_BLOCK, w), lambda g, t: (g, jnp.minimum(t, nb - 1), 0))
    prev = lambda w: pl.BlockSpec((None, ATT_BLOCK, w), lambda g, t: (g, jnp.maximum(t - 1, 0), 0))
    return pl.pallas_call(
        body, name="dilated_attention_bwd", grid=(GROUPS, nb + 1),
        in_specs=[cur(3 * WIDTH), cur(LANES), cur(LANES), prev(LANES), prev(LANES),
                  cur(WIDTH), cur(WIDTH), cur(WIDTH), cur(WIDTH)],
        out_specs=prev(3 * WIDTH), out_shape=jax.ShapeDtypeStruct((GROUPS, s, 3 * WIDTH), BF16),
        scratch_shapes=[pltpu.VMEM((ATT_BLOCK, WIDTH), BF16), pltpu.VMEM((ATT_BLOCK, WIDTH), BF16),
                        pltpu.VMEM((ATT_BLOCK, WIDTH), F32), pltpu.VMEM((ATT_BLOCK, WIDTH), F32),
                        pltpu.VMEM((ATT_BLOCK, WIDTH), BF16)],
        compiler_params=_params("arbitrary", "arbitrary"),
    )(qkv, cos, sin, cos, sin, o, lse, do, dlse)


CONV_PAD = SUBLANES


def _gdn_post(y, scale):
    head0, _ = _lane_masks(y.shape[0])
    c = _silu(y)
    sq = c * c
    ss0 = jnp.sum(jnp.where(head0, sq, 0.0), axis=1, keepdims=True)
    ss1 = jnp.sum(jnp.where(head0, 0.0, sq), axis=1, keepdims=True)
    r = jnp.where(head0, lax.rsqrt(ss0 + NORM_EPS), lax.rsqrt(ss1 + NORM_EPS))
    return c * r * scale


def _block_kind(j, nq):
    return j < 2 * nq, jnp.where(j < nq, HEAD_DIM ** -0.5, 1.0).astype(F32)


def _conv_rows(xp_ref, w, c0, rows):
    y = w[0:1, :] * xp_ref[pl.ds(c0 + CONV_PAD - (CONV_K - 1), rows), :]
    for k in range(1, CONV_K):
        y = y + w[k:k + 1, :] * xp_ref[pl.ds(c0 + CONV_PAD - (CONV_K - 1) + k, rows), :]
    return y


def _gdn_pre_fwd(proj_r, conv8, col0):
    s = proj_r.shape[0]
    tr = _row_block(s, 512)
    nblk = QKV_B // LANES
    nq = WIDTH // LANES

    def body(x_ref, w_ref, out_ref, xp_ref):
        normed, scale = _block_kind(pl.program_id(0), nq)
        xp_ref[pl.ds(0, CONV_PAD), :] = jnp.zeros((CONV_PAD, LANES), F32)
        xp_ref[pl.ds(CONV_PAD, s), :] = x_ref[...]
        w = w_ref[...]

        @pl.when(normed)
        def _():
            for c in range(s // tr):
                out_ref[pl.ds(c * tr, tr), :] = _gdn_post(_conv_rows(xp_ref, w, c * tr, tr), scale)

        @pl.when(jnp.logical_not(normed))
        def _():
            for c in range(s // tr):
                out_ref[pl.ds(c * tr, tr), :] = _silu(_conv_rows(xp_ref, w, c * tr, tr))

    return pl.pallas_call(
        body, name="gdn_conv_fwd", grid=(nblk,),
        in_specs=[pl.BlockSpec((s, LANES), lambda j: (0, col0 + j)), pl.BlockSpec((SUBLANES, LANES), lambda j: (0, j))],
        out_specs=pl.BlockSpec((s, LANES), lambda j: (0, j)),
        out_shape=jax.ShapeDtypeStruct((s, QKV_B), F32),
        scratch_shapes=[pltpu.VMEM((s + CONV_PAD, LANES), F32)],
        compiler_params=_params("parallel"),
    )(proj_r, conv8)


def _gdn_pre_bwd(proj_r, conv8, dc, col0):
    s = proj_r.shape[0]
    tr = _row_block(s, 512)
    nblk = QKV_B // LANES
    nq = WIDTH // LANES

    def body(x_ref, w_ref, dc_ref, dx_ref, dw_ref, xp_ref, dyp_ref):
        normed, scale = _block_kind(pl.program_id(0), nq)
        xp_ref[pl.ds(0, CONV_PAD), :] = jnp.zeros((CONV_PAD, LANES), F32)
        xp_ref[pl.ds(CONV_PAD, s), :] = x_ref[...]
        dyp_ref[pl.ds(s, CONV_PAD), :] = jnp.zeros((CONV_PAD, LANES), F32)
        w = w_ref[...]

        def conv_output_cotangents(post):
            for c in range(s // tr):
                _, vjp = jax.vjp(post, _conv_rows(xp_ref, w, c * tr, tr))
                dyp_ref[pl.ds(c * tr, tr), :] = vjp(dc_ref[pl.ds(c * tr, tr), :])[0]

        pl.when(normed)(lambda: conv_output_cotangents(lambda yy: _gdn_post(yy, scale)))
        pl.when(jnp.logical_not(normed))(lambda: conv_output_cotangents(_silu))
        dws = [jnp.zeros((1, LANES), F32) for _ in range(CONV_K)]
        for c in range(s // tr):
            c0 = c * tr
            dy = dyp_ref[pl.ds(c0, tr), :]
            dx = w[0:1, :] * dyp_ref[pl.ds(c0 + CONV_K - 1, tr), :]
            for k in range(1, CONV_K):
                dx = dx + w[k:k + 1, :] * dyp_ref[pl.ds(c0 + CONV_K - 1 - k, tr), :]
            dx_ref[pl.ds(c0, tr), :] = dx.astype(BF16)
            for k in range(CONV_K):
                xs = xp_ref[pl.ds(c0 + CONV_PAD - (CONV_K - 1) + k, tr), :]
                dws[k] = dws[k] + jnp.sum(dy * xs, axis=0, keepdims=True)
        row = lax.broadcasted_iota(jnp.int32, (SUBLANES, LANES), 0)
        dwb = jnp.zeros((SUBLANES, LANES), F32)
        for k in range(CONV_K):
            dwb = dwb + jnp.where(row == k, dws[k], 0.0)
        dw_ref[...] = dwb

    return pl.pallas_call(
        body, name="gdn_conv_bwd", grid=(nblk,),
        in_specs=[pl.BlockSpec((s, LANES), lambda j: (0, col0 + j)), pl.BlockSpec((SUBLANES, LANES), lambda j: (0, j)),
                  pl.BlockSpec((s, LANES), lambda j: (0, j))],
        out_specs=(pl.BlockSpec((s, LANES), lambda j: (0, j)), pl.BlockSpec((SUBLANES, LANES), lambda j: (0, j))),
        out_shape=(jax.ShapeDtypeStruct((s, QKV_B), BF16), jax.ShapeDtypeStruct((SUBLANES, QKV_B), F32)),
        scratch_shapes=[pltpu.VMEM((s + CONV_PAD, LANES), F32), pltpu.VMEM((s + CONV_PAD, LANES), F32)],
        compiler_params=_params("parallel"),
    )(proj_r, conv8, dc)


def _gdn_chunk(q, k, v, bcol, acol, alog, dtb, gnw, state, t_saved=None):
    n = q.shape[-2]
    shp = (1, n, n)
    row = lax.broadcasted_iota(jnp.int32, shp, 1)
    col = lax.broadcasted_iota(jnp.int32, shp, 2)
    beta = _sigmoid(bcol)
    g = -jnp.exp(alog) * _softplus(acol + dtb)
    g_row = jnp.sum(jnp.where(row == col, g, 0.0), axis=-2, keepdims=True)
    big_g = jnp.sum(jnp.where(row >= col, g_row, 0.0), axis=-1, keepdims=True)
    big_g_row = jnp.sum(jnp.where(row <= col, g, 0.0), axis=-2, keepdims=True)
    decay_incl = jnp.exp(jnp.where(row >= col, big_g - big_g_row, -jnp.inf))
    decay_strict = jnp.where(row > col, decay_incl, 0.0)
    k_beta = k * beta
    a_mat = _mm_nt(k_beta, k) * decay_strict
    t_inv = _tri_inv(a_mat) if t_saved is None else _tri_inv_saved(a_mat, t_saved)
    e_g = jnp.exp(big_g)
    u = _mm(t_inv, v * beta)
    w = _mm(t_inv, k_beta * e_g)
    attn = _mm_nt(q, k) * decay_incl
    v_new = u - _mm(w, state)
    o = _mm(q * e_g, state) + _mm(attn, v_new)
    total = jnp.sum(g, axis=-2, keepdims=True)
    new_state = state * jnp.exp(total) + _mm_tn(k * jnp.exp(total - big_g), v_new)
    return _rmsnorm(o, gnw), new_state, t_inv


def _split_heads(x):
    return jnp.stack([x[:, h * HEAD_DIM:(h + 1) * HEAD_DIM] for h in range(HEADS)], axis=0)


def _merge_heads(x):
    return jnp.concatenate([x[h] for h in range(HEADS)], axis=1)


def _logit_columns(ba):
    lane = lax.broadcasted_iota(jnp.int32, ba.shape, 1)

    def cols(off):
        return jnp.stack([jnp.sum(jnp.where(lane == off + h, ba, 0.0), axis=1, keepdims=True) for h in range(HEADS)], axis=0)

    return cols(0), cols(HEADS)


def _logit_block(dbc, dac, shape):
    lane = lax.broadcasted_iota(jnp.int32, shape, 1)
    out = jnp.zeros(shape, F32)
    for h in range(HEADS):
        out = out + jnp.where(lane == h, dbc[h], 0.0) + jnp.where(lane == HEADS + h, dac[h], 0.0)
    return out


def _gdn_scan_fwd(cqkv, proj_r, ba_col, alog, dtb, gnw):
    s = cqkv.shape[0]
    nc = s // CHUNK
    span = SCAN_CHUNKS * CHUNK

    def body(q_ref, k_ref, v_ref, ba_ref, al_ref, dt_ref, gnw_ref, o_ref, st_ref, ti_ref, state_ref):
        @pl.when(pl.program_id(0) == 0)
        def _():
            state_ref[...] = jnp.zeros_like(state_ref)

        st = state_ref[...]
        for u in range(SCAN_CHUNKS):
            rows = pl.ds(u * CHUNK, CHUNK)
            st_ref[u] = st
            bcol, acol = _logit_columns(ba_ref[rows, :])
            o, st, t_inv = _gdn_chunk(_split_heads(q_ref[rows, :]), _split_heads(k_ref[rows, :]),
                                      _split_heads(v_ref[rows, :]), bcol, acol, al_ref[...], dt_ref[...], gnw_ref[...], st)
            o_ref[rows, :] = _merge_heads(o)
            ti_ref[u] = t_inv
        state_ref[...] = st

    part = lambda i: pl.BlockSpec((span, WIDTH), lambda n: (n, i))
    par = pl.BlockSpec((HEADS, 1, 1), lambda n: (0, 0, 0))
    per_chunk = pl.BlockSpec((SCAN_CHUNKS, HEADS, HEAD_DIM, HEAD_DIM), lambda n: (n, 0, 0, 0))
    per_chunk_shape = jax.ShapeDtypeStruct((nc, HEADS, HEAD_DIM, HEAD_DIM), F32)
    return pl.pallas_call(
        body, name="gdn_scan_fwd", grid=(nc // SCAN_CHUNKS,),
        in_specs=[part(0), part(1), part(2), pl.BlockSpec((span, LANES), lambda n: (n, ba_col)), par, par,
                  pl.BlockSpec((1, 1, HEAD_DIM), lambda n: (0, 0, 0))],
        out_specs=(part(0), per_chunk, per_chunk),
        out_shape=(jax.ShapeDtypeStruct((s, WIDTH), F32), per_chunk_shape, per_chunk_shape),
        scratch_shapes=[pltpu.VMEM((HEADS, HEAD_DIM, HEAD_DIM), F32)],
        compiler_params=_params("arbitrary"),
    )(cqkv, cqkv, cqkv, proj_r, alog, dtb, gnw)


def _gdn_scan_bwd(cqkv, proj_r, ba_col, alog, dtb, gnw, states, t_invs, do):
    s = cqkv.shape[0]
    nc = s // CHUNK
    span = SCAN_CHUNKS * CHUNK
    n_steps = nc // SCAN_CHUNKS

    def body(q_ref, k_ref, v_ref, ba_ref, al_ref, dt_ref, gnw_ref, st_ref, ti_ref, do_ref,
             dqkv_ref, dba_ref, dal_ref, ddt_ref, dgnw_ref, dstate_ref):
        @pl.when(pl.program_id(0) == 0)
        def _():
            dstate_ref[...] = jnp.zeros_like(dstate_ref)
            dal_ref[...] = jnp.zeros_like(dal_ref)
            ddt_ref[...] = jnp.zeros_like(ddt_ref)
            dgnw_ref[...] = jnp.zeros_like(dgnw_ref)

        dst = dstate_ref[...]
        for u in reversed(range(SCAN_CHUNKS)):
            rows = pl.ds(u * CHUNK, CHUNK)
            bcol, acol = _logit_columns(ba_ref[rows, :])
            _, vjp = jax.vjp(lambda *a, t_saved=ti_ref[u]: _gdn_chunk(*a, t_saved=t_saved)[:2],
                             _split_heads(q_ref[rows, :]), _split_heads(k_ref[rows, :]), _split_heads(v_ref[rows, :]),
                             bcol, acol, al_ref[...], dt_ref[...], gnw_ref[...], st_ref[u])
            dq, dk, dv, dbc, dac, dal, ddt, dgn, dst = vjp((_split_heads(do_ref[rows, :]), dst))
            dqkv_ref[rows, pl.ds(0, WIDTH)] = _merge_heads(dq)
            dqkv_ref[rows, pl.ds(WIDTH, WIDTH)] = _merge_heads(dk)
            dqkv_ref[rows, pl.ds(2 * WIDTH, WIDTH)] = _merge_heads(dv)
            dba_ref[rows, :] = _logit_block(dbc, dac, (CHUNK, LANES))
            dal_ref[...] += dal
            ddt_ref[...] += ddt
            dgnw_ref[...] += dgn
        dstate_ref[...] = dst

    rev = lambda n: n_steps - 1 - n
    part = lambda i: pl.BlockSpec((span, WIDTH), lambda n: (rev(n), i))
    par = pl.BlockSpec((HEADS, 1, 1), lambda n: (0, 0, 0))
    vec = pl.BlockSpec((1, 1, HEAD_DIM), lambda n: (0, 0, 0))
    par_shape = jax.ShapeDtypeStruct((HEADS, 1, 1), F32)
    per_chunk = pl.BlockSpec((SCAN_CHUNKS, HEADS, HEAD_DIM, HEAD_DIM), lambda n: (rev(n), 0, 0, 0))
    return pl.pallas_call(
        body, name="gdn_scan_bwd", grid=(n_steps,),
        in_specs=[part(0), part(1), part(2), pl.BlockSpec((span, LANES), lambda n: (rev(n), ba_col)), par, par, vec,
                  per_chunk, per_chunk, part(0)],
        out_specs=(pl.BlockSpec((span, QKV_B), lambda n: (rev(n), 0)), pl.BlockSpec((span, LANES), lambda n: (rev(n), 0)),
                   par, par, vec),
        out_shape=(jax.ShapeDtypeStruct((s, QKV_B), F32), jax.ShapeDtypeStruct((s, LANES), F32), par_shape, par_shape,
                   jax.ShapeDtypeStruct((1, 1, HEAD_DIM), F32)),
        scratch_shapes=[pltpu.VMEM((HEADS, HEAD_DIM, HEAD_DIM), F32)],
        compiler_params=_params("arbitrary"),
    )(cqkv, cqkv, cqkv, proj_r, alog, dtb, gnw, states, t_invs, do)


def _tail_loss(x, tgt, o0, o1, o2, l0, l1, l2, ga, gb, za, zb, ob, fnw, wua, wub, wo, tap_a, tap_b, tap_o):
    lm = jnp.maximum(jnp.maximum(l0, l1), l2)
    e0, e1, e2 = jnp.exp(l0 - lm), jnp.exp(l1 - lm), jnp.exp(l2 - lm)
    o_a = (e0 * o0 + e1 * o1 + e2 * o2) / (e0 + e1 + e2)
    xa, xb = o_a * _silu(za), ob * _silu(zb)
    y_a = _mm_x(xa, wua) + tap_a
    y_b = _mm_x(xb, wub) + tap_b
    merged = _sigmoid(ga) * y_a + _sigmoid(gb) * y_b
    y = _rmsnorm(x + _mm_x(merged, wo) + tap_o, fnw)
    err = y - tgt
    per_token = jnp.sum(err * err, axis=1, keepdims=True) * (0.5 / x.shape[1])
    return jnp.sum(per_token, axis=0, keepdims=True), (xa.astype(BF16), xb.astype(BF16), merged.astype(BF16))


def _tail(x, tgt, o_all, lse_all, og12, lg12, proj_r, ob, wua, wub, wo, fnw):
    s, d = x.shape
    tm = _row_block(s, TAIL_ROWS)
    col_za = 2 * d // WIDTH
    col_zb = (2 * d + WIDTH + QKV_B) // WIDTH

    def body(x_ref, t_ref, o0_ref, o1_ref, o2_ref, l0_ref, l1_ref, l2_ref, ga_ref, gb_ref, za_ref, zb_ref, ob_ref,
             wua_ref, wub_ref, wo_ref, fnw_ref,
             loss_ref, dx_ref, do0_ref, do1_ref, do2_ref, dl0_ref, dl1_ref, dl2_ref, dga_ref, dgb_ref, dza_ref,
             dzb_ref, dob_ref, xa_ref, xb_ref, mg_ref, dya_ref, dyb_ref, dmo_ref, dfnw_ref):
        @pl.when(pl.program_id(0) == 0)
        def _():
            for r in (loss_ref, dfnw_ref):
                r[...] = jnp.zeros_like(r)

        tap = jnp.zeros((tm, d), F32)
        args = (x_ref[...], t_ref[...], o0_ref[...], o1_ref[...], o2_ref[...], l0_ref[...], l1_ref[...], l2_ref[...],
                ga_ref[...], gb_ref[...], za_ref[...], zb_ref[...], ob_ref[...], fnw_ref[...],
                wua_ref[...], wub_ref[...], wo_ref[...], tap, tap, tap)
        loss, vjp, (xa16, xb16, mg16) = jax.vjp(_tail_loss, *args, has_aux=True)
        (dx, _, do0, do1, do2, dl0, dl1, dl2, dga, dgb, dza, dzb, dob, dfnw, _, _, _, dya, dyb, dmo) = vjp(
            jnp.ones((1, 1), F32))
        loss_ref[...] += jnp.broadcast_to(loss, loss_ref.shape)
        dx_ref[...] = dx
        do0_ref[...], do1_ref[...], do2_ref[...] = do0, do1, do2
        dl0_ref[...], dl1_ref[...], dl2_ref[...] = dl0, dl1, dl2
        dga_ref[...] = dga.astype(BF16)
        dgb_ref[...] = dgb.astype(BF16)
        dza_ref[...] = dza.astype(BF16)
        dzb_ref[...] = dzb.astype(BF16)
        dob_ref[...] = dob
        xa_ref[...], xb_ref[...], mg_ref[...] = xa16, xb16, mg16
        dya_ref[...] = dya.astype(BF16)
        dyb_ref[...] = dyb.astype(BF16)
        dmo_ref[...] = dmo.astype(BF16)
        dfnw_ref[...] += dfnw

    row = lambda w, c=0: pl.BlockSpec((tm, w), lambda i: (i, c))
    grp0 = pl.BlockSpec((None, tm, WIDTH), lambda i: (0, i, 0))
    full = lambda a, b: pl.BlockSpec((a, b), lambda i: (0, 0))
    f32 = lambda a, b: jax.ShapeDtypeStruct((a, b), F32)
    b16 = lambda a, b: jax.ShapeDtypeStruct((a, b), BF16)
    stacked = jax.ShapeDtypeStruct((GROUPS, s, WIDTH), F32)
    gspecs = [grp0, row(WIDTH), row(WIDTH)]
    in_specs = ([row(d), row(d)] + gspecs * 2 + [row(d, 0), row(d, 1), row(WIDTH, col_za), row(WIDTH, col_zb),
                row(WIDTH), full(WIDTH, d), full(WIDTH, d), full(d, d), full(1, d)])
    out_specs = ([full(SUBLANES, LANES), row(d)] + gspecs * 2 + [row(d), row(d), row(WIDTH), row(WIDTH), row(WIDTH),
                 row(WIDTH), row(WIDTH), row(d), row(d), row(d), row(d), full(1, d)])
    gshapes = [stacked, f32(s, WIDTH), f32(s, WIDTH)]
    out_shape = ([f32(SUBLANES, LANES), f32(s, d)] + gshapes * 2 + [b16(s, d), b16(s, d), b16(s, WIDTH),
                 b16(s, WIDTH), f32(s, WIDTH), b16(s, WIDTH), b16(s, WIDTH), b16(s, d), b16(s, d), b16(s, d), b16(s, d),
                 f32(1, d)])
    return pl.pallas_call(
        body, name="tail_fwd_bwd", grid=(s // tm,),
        in_specs=in_specs, out_specs=tuple(out_specs), out_shape=tuple(out_shape),
        compiler_params=pltpu.CompilerParams(dimension_semantics=("arbitrary",), vmem_limit_bytes=TAIL_VMEM_LIMIT),
    )(x, tgt, o_all, og12[0], og12[1], lse_all, lg12[0], lg12[1], proj_r, proj_r, proj_r, proj_r, ob, wua, wub, wo, fnw)


PERMUTE_SPAN = 4096


def _permute_span(s):
    return PERMUTE_SPAN if s % PERMUTE_SPAN == 0 else s


def _from_dilated_rows(stacked, g, dil, name):
    n_slots, s, c = stacked.shape
    view = stacked.reshape(n_slots, dil, s // dil, c)
    span = _permute_span(s)

    def body(in_ref, out_ref):
        for r in range(dil):
            out_ref[pl.ds(r, span // dil, stride=dil), :] = in_ref[r]

    return pl.pallas_call(
        body, name=name, grid=(s // span, c // LANES),
        in_specs=[pl.BlockSpec((None, dil, span // dil, LANES), lambda n, j: (g, 0, n, j))],
        out_specs=pl.BlockSpec((span, LANES), lambda n, j: (n, j)),
        out_shape=jax.ShapeDtypeStruct((s, c), stacked.dtype),
        compiler_params=_params("parallel", "parallel"),
    )(view)


def _to_dilated_rows_into(nat, stacked, g, dil, name):
    n_slots, s, c = stacked.shape
    view = stacked.reshape(n_slots, dil, s // dil, c)
    span = _permute_span(s)

    def body(nat_ref, old_ref, out_ref):
        for r in range(dil):
            out_ref[r] = nat_ref[pl.ds(r, span // dil, stride=dil), :]

    out = pl.pallas_call(
        body, name=name, grid=(s // span, c // LANES),
        in_specs=[pl.BlockSpec((span, LANES), lambda n, j: (n, j)), pl.BlockSpec(memory_space=pl.ANY)],
        out_specs=pl.BlockSpec((None, dil, span // dil, LANES), lambda n, j: (g, 0, n, j)),
        out_shape=jax.ShapeDtypeStruct(view.shape, stacked.dtype),
        input_output_aliases={1: 0},
        compiler_params=_params("parallel", "parallel"),
    )(nat, view)
    return out.reshape(stacked.shape)


def _to_dilated(a, dil):
    if dil == 1:
        return a
    s = a.shape[0]
    return a.reshape(s // dil, dil, -1).transpose(1, 0, 2).reshape(a.shape)


def _rope_tables(s):
    inv_freq = ROPE_THETA ** (-jnp.arange(0, HEAD_DIM, 2, dtype=F32) / HEAD_DIM)
    ang = jnp.arange(s, dtype=F32)[:, None] * inv_freq[None, :]
    cos_n = jnp.tile(jnp.cos(ang), (1, 2 * LANES // HEAD_DIM))
    sin_h = jnp.sin(ang)
    sin_n = jnp.tile(jnp.concatenate([-sin_h, sin_h], axis=1), (1, LANES // HEAD_DIM))

    def per_group(table, tag):
        out = jnp.broadcast_to(table, (GROUPS,) + table.shape)
        for g in range(1, GROUPS):
            out = _to_dilated_rows_into(table, out, g, DILATIONS[g], "rope_%s_to_dilated_%d" % (tag, g))
        return out

    return per_group(cos_n, "cos"), per_group(sin_n, "sin")


def _regroup_columns(pieces, widths):
    starts, pos = [], 0
    for p in pieces:
        starts.append(pos)
        pos += p.shape[1]
    assert pos == sum(widths), (pos, widths)
    out, lo = [], 0
    for w in widths:
        hi, parts = lo + w, []
        for p, st in zip(pieces, starts):
            a, b = max(lo, st), min(hi, st + p.shape[1])
            if a < b:
                parts.append(p[:, a - st:b - st])
        out.append(parts[0] if len(parts) == 1 else jnp.concatenate(parts, axis=1))
        lo = hi
    return out


def _pack_rows(parts, dtype, row_multiple):
    flat = jnp.concatenate([p.reshape(-1).astype(dtype) for p in parts])
    tile = row_multiple * LANES
    pad = (-flat.shape[0]) % tile
    return jnp.pad(flat, (0, pad)).reshape(-1, LANES)


def _unpack_rows(packed, shapes):
    flat = packed.reshape(-1)
    out, start = [], 0
    for shp in shapes:
        size = 1
        for n in shp:
            size *= n
        out.append(flat[start:start + size].reshape(shp))
        start += size
    return out


def kernel(x, norm_w, w_in, conv_w, a_log, dt_bias, gdn_norm_w, w_up_a, w_up_b, w_out, final_norm_w, loss_target, m_norm_w, m_w_in, m_conv_w, m_a_log, m_dt_bias, m_gdn_norm_w, m_w_up_a, m_w_up_b, m_w_out, m_final_norm_w, v_norm_w, v_w_in, v_conv_w, v_a_log, v_dt_bias, v_gdn_norm_w, v_w_up_a, v_w_up_b, v_w_out, v_final_norm_w):
    x2, tgt = x[0], loss_target[0]
    s, d = x2.shape
    me = 4 * lax.axis_index("x") + 2 * lax.axis_index("y") + lax.axis_index("c")
    win8 = w_in.shape[2]
    conv8w = conv_w.shape[2]

    conv_shard = jnp.pad(conv_w[0], ((0, SUBLANES - CONV_K), (0, 0)))
    w_in_g, wua_g, wub_g, wo_g, conv_g = _all_gather(
        [w_in[0].astype(BF16), w_up_a[0].astype(BF16), w_up_b[0].astype(BF16), w_out[0].astype(BF16), conv_shard])
    wua = jnp.concatenate([wua_g[i] for i in range(N_DEV)], axis=1)
    wub = jnp.concatenate([wub_g[i] for i in range(N_DEV)], axis=1)
    wo = wo_g.reshape(d, d)
    conv8 = jnp.concatenate([conv_g[i] for i in range(N_DEV)], axis=1)

    seg_widths = [QKV_B] * GROUPS + [WIDTH, QKV_B, WIDTH, 2 * HEADS, 2 * d]
    wq0, wq1, wq2, w_za, w_qkvb, w_zb, w_ba, w_gates = _regroup_columns([w_in_g[i] for i in range(N_DEV)], seg_widths)
    w_qkv = jnp.stack([wq0, wq1, wq2])
    w_rest = jnp.concatenate([w_gates, w_za, w_qkvb, w_zb, w_ba,
                              jnp.zeros((d, BA_PAD - 2 * HEADS), BF16)], axis=1)
    col_qkvb = (2 * d + WIDTH) // LANES
    col_ba = (2 * d + 2 * WIDTH + QKV_B) // LANES

    h = _rms_fwd(x2, norm_w)
    h_all = jnp.stack([_to_dilated(h, dil) for dil in DILATIONS])
    qkv_all = _matmul(h_all, w_qkv, F32, "in_proj_attention", tn=QKV_B)
    proj_r = _matmul(h[None], w_rest[None], F32, "in_proj_rest", tn=2560)[0]
    cos, sin = _rope_tables(s)
    o_all, lse_all = _attn_fwd(qkv_all, cos, sin)
    og12 = [_from_dilated_rows(o_all, g, DILATIONS[g], "attn_out_to_natural_%d" % g) for g in (1, 2)]
    lg12 = [_from_dilated_rows(lse_all, g, DILATIONS[g], "attn_lse_to_natural_%d" % g) for g in (1, 2)]

    cqkv = _gdn_pre_fwd(proj_r, conv8, col_qkvb)
    alog3, dtb3, gnw3 = a_log.reshape(HEADS, 1, 1), dt_bias.reshape(HEADS, 1, 1), gdn_norm_w.reshape(1, 1, HEAD_DIM)
    ob, states, t_invs = _gdn_scan_fwd(cqkv, proj_r, col_ba, alog3, dtb3, gnw3)

    (loss_blk, dx_res, do_all, do1, do2, dl_all, dl1, dl2, dga, dgb, dza, dzb, dob,
     xa16, xb16, mg16, dya16, dyb16, dmo16, dfnw) = _tail(
        x2, tgt, o_all, lse_all, og12, lg12, proj_r, ob, wua, wub, wo, final_norm_w.reshape(1, d))
    dwua = _matmul(xa16[None], dya16[None], F32, "up_a_dw", mode="tn", tk=2048)[0]
    dwub = _matmul(xb16[None], dyb16[None], F32, "up_b_dw", mode="tn", tk=2048)[0]
    dwo = _matmul(mg16[None], dmo16[None], F32, "out_proj_dw", mode="tn", tk=2048)[0]

    for g, (t_o, t_l) in ((1, (do1, dl1)), (2, (do2, dl2))):
        do_all = _to_dilated_rows_into(t_o, do_all, g, DILATIONS[g], "attn_dout_to_dilated_%d" % g)
        dl_all = _to_dilated_rows_into(t_l, dl_all, g, DILATIONS[g], "attn_dlse_to_dilated_%d" % g)
    dqkv_all = _attn_bwd(qkv_all, cos, sin, o_all, lse_all, do_all, dl_all)

    dcqkv, dba, dalog3, ddtb3, dgnw3 = _gdn_scan_bwd(cqkv, proj_r, col_ba, alog3, dtb3, gnw3, states, t_invs, dob)
    dqkv_b, dconv8 = _gdn_pre_bwd(proj_r, conv8, dcqkv, col_qkvb)
    dproj_r = jnp.concatenate([dga, dgb, dza, dqkv_b, dzb,
                               jnp.pad(dba.astype(BF16), ((0, 0), (0, BA_PAD - LANES)))], axis=1)

    def col_slabs(a, width):
        return jnp.stack([a[:, j * width:(j + 1) * width] for j in range(N_DEV)])

    core = lax.axis_index("c").astype(jnp.int32).reshape(1)
    small_slabs = [col_slabs(dwua, d // N_DEV), col_slabs(dwub, d // N_DEV), dwo.reshape(N_DEV, d // N_DEV, d)]
    dw_qkv, *small_sib = _matmul(h_all, dqkv_all, F32, "in_proj_attention_dw", mode="tn", tk=2048, tn=QKV_B,
                                 exchange=_sibling_exchange(small_slabs))
    small_partials = [_pair_sum(a, b, core, "grads_pair_sum_%d" % (i + 1))
                      for i, (a, b) in enumerate(zip(small_slabs, small_sib))]
    dw_rest, *small_contrib = _matmul(h[None], dproj_r[None], F32, "in_proj_rest_dw", mode="tn", tk=2048,
                                      exchange=_chip_exchange(small_partials))
    dw_rest = dw_rest[0]
    o2 = 2 * d
    dw_in_pieces = [dw_qkv[0], dw_qkv[1], dw_qkv[2],
                    dw_rest[:, o2:o2 + WIDTH], dw_rest[:, o2 + WIDTH:o2 + WIDTH + QKV_B],
                    dw_rest[:, o2 + WIDTH + QKV_B:o2 + 2 * WIDTH + QKV_B],
                    dw_rest[:, o2 + 2 * WIDTH + QKV_B:o2 + 2 * WIDTH + QKV_B + 2 * HEADS],
                    dw_rest[:, :o2]]

    w_in_slabs = jnp.stack(_regroup_columns(dw_in_pieces, [win8] * N_DEV))
    dh_a, w_in_sib = _matmul(dqkv_all, w_qkv, F32, "in_proj_attention_dh", mode="nt", tk=2048,
                             exchange=_sibling_exchange([w_in_slabs]))
    w_in_partial = _pair_sum(w_in_slabs, w_in_sib, core, "grads_pair_sum_0")
    dh_r, w_in_contrib = _matmul(dproj_r[None], w_rest[None], F32, "in_proj_rest_dh", mode="nt", tk=2560,
                                 exchange=_chip_exchange([w_in_partial]))
    contrib = [w_in_contrib] + small_contrib
    dh_parts = [dh_r[0]] + [_from_dilated_rows(dh_a, g, DILATIONS[g], "dh_to_natural_%d" % g) for g in (1, 2)]
    grad_x, dnorm_w = _rms_bwd(x2, norm_w, dh_a, dh_parts, dx_res)

    small_parts = [dnorm_w, dfnw, dconv8[:CONV_K], dalog3[:, 0, 0], ddtb3[:, 0, 0], dgnw3[0], loss_blk[0, 0:1]]
    small_rows = [-(-p.size // LANES) for p in small_parts]
    small = jnp.concatenate([jnp.pad(p.reshape(-1), (0, r * LANES - p.size)).reshape(r, LANES)
                             for p, r in zip(small_parts, small_rows)])
    small = jnp.pad(small, ((0, (-small.shape[0]) % SUBLANES), (0, 0)))
    small_sum = _small_all_reduce(small)
    pieces, r0 = [], 0
    for p, r in zip(small_parts, small_rows):
        pieces.append(small_sum[r0:r0 + r].reshape(-1)[:p.size].reshape(p.shape))
        r0 += r
    g_norm_w, g_fnw, g_conv_full, g_alog, g_dtb, g_gnw, loss_sum = pieces
    g_conv = lax.dynamic_slice(g_conv_full, (0, me * conv8w), (CONV_K, conv8w))

    big = [_adamw(c, w[0], m[0], v[0], name) for c, w, m, v, name in (
        (contrib[0], w_in, m_w_in, v_w_in, "adamw_w_in"), (contrib[1], w_up_a, m_w_up_a, v_w_up_a, "adamw_w_up_a"),
        (contrib[2], w_up_b, m_w_up_b, v_w_up_b, "adamw_w_up_b"), (contrib[3], w_out, m_w_out, v_w_out, "adamw_w_out"))]
    g_big, d_big, nm_big, nv_big = ([t[i] for t in big] for i in range(4))

    small_ws = [norm_w, final_norm_w, conv_w, a_log, dt_bias, gdn_norm_w]
    small_ms = [m_norm_w, m_final_norm_w, m_conv_w, m_a_log, m_dt_bias, m_gdn_norm_w]
    small_vs = [v_norm_w, v_final_norm_w, v_conv_w, v_a_log, v_dt_bias, v_gdn_norm_w]
    small_gs = [g_norm_w, g_fnw, g_conv, g_alog, g_dtb, g_gnw]
    small_shapes = [t.shape for t in small_ws]
    sm = _adamw(_pack_rows(small_gs, F32, SUBLANES)[None], _pack_rows(small_ws, F32, SUBLANES),
                _pack_rows(small_ms, F32, SUBLANES), _pack_rows(small_vs, F32, SUBLANES), "adamw_small")
    g_sm, d_sm, nm_sm, nv_sm = (_unpack_rows(t, small_shapes) for t in sm)

    def ordered(bigs, smalls):
        nw, fnw_, cw, al, dtb, gn = smalls
        wi, ua, ub, wo_ = (t[None] for t in bigs)
        return [nw, wi, cw, al, dtb, gn, ua, ub, wo_, fnw_]

    return (loss_sum.reshape(()), grad_x[None], *ordered(g_big, g_sm), *ordered(d_big, d_sm),
            *ordered(nm_big, nm_sm), *ordered(nv_big, nv_sm))
```

```python
import functools

import jax
import jax.numpy as jnp
from jax import lax
from jax.experimental import pallas as pl
from jax.experimental.pallas import tpu as pltpu

F32 = jnp.float32
BF16 = jnp.bfloat16
MESH = pl.DeviceIdType.MESH
N_DEV = 8
LANES = 128
SUBLANES = 8

GROUPS = 3
HEADS = 8
HEAD_DIM = 64
WIDTH = HEADS * HEAD_DIM
ATT_BLOCK = 128
DILATIONS = (1, 4, 16)
N_BACK = 128
CONV_K = 4
CHUNK = 64
SCAN_CHUNKS = 4
QKV_B = 3 * WIDTH
QKV_A = GROUPS * 3 * WIDTH
BA_PAD = 512
NORM_EPS = 1e-6
ROPE_THETA = 10000.0
ADAM_LR, ADAM_B1, ADAM_B2, ADAM_EPS, ADAM_WD, ADAM_STEP = 0.001, 0.9, 0.999, 1e-08, 0.01, 10

VMEM_LIMIT = 56 * 1024 * 1024
TAIL_ROWS = 256
TAIL_VMEM_LIMIT = 62 * 1024 * 1024


def _params(*sem):
    return pltpu.CompilerParams(dimension_semantics=sem, vmem_limit_bytes=VMEM_LIMIT)


def _dg(a, b, ca, cb):
    nb = a.ndim - 2
    batch = tuple(range(nb))
    return lax.dot_general(a, b, (((nb + ca,), (nb + cb,)), (batch, batch)), preferred_element_type=F32)


@jax.custom_vjp
def _mm(a, b):
    return _dg(a.astype(BF16), b.astype(BF16), 1, 0)


def _mm_fwd(a, b):
    return _mm(a, b), (a.astype(BF16), b.astype(BF16))


def _mm_bwd(res, ct):
    a16, b16 = res
    c16 = ct.astype(BF16)
    return _dg(c16, b16, 1, 1), _dg(a16, c16, 0, 0)


_mm.defvjp(_mm_fwd, _mm_bwd)


@jax.custom_vjp
def _mm_nt(a, b):
    return _dg(a.astype(BF16), b.astype(BF16), 1, 1)


def _mm_nt_fwd(a, b):
    return _mm_nt(a, b), (a.astype(BF16), b.astype(BF16))


def _mm_nt_bwd(res, ct):
    a16, b16 = res
    c16 = ct.astype(BF16)
    return _dg(c16, b16, 1, 0), _dg(c16, a16, 0, 0)


_mm_nt.defvjp(_mm_nt_fwd, _mm_nt_bwd)


@jax.custom_vjp
def _mm_tn(a, b):
    return _dg(a.astype(BF16), b.astype(BF16), 0, 0)


def _mm_tn_fwd(a, b):
    return _mm_tn(a, b), (a.astype(BF16), b.astype(BF16))


def _mm_tn_bwd(res, ct):
    a16, b16 = res
    c16 = ct.astype(BF16)
    return _dg(b16, c16, 1, 1), _dg(a16, c16, 1, 0)


_mm_tn.defvjp(_mm_tn_fwd, _mm_tn_bwd)


@jax.custom_vjp
def _mm_x(a, w16):
    return _dg(a.astype(BF16), w16, 1, 0)


def _mm_x_fwd(a, w16):
    return _mm_x(a, w16), w16


def _mm_x_bwd(w16, ct):
    return _dg(ct.astype(BF16), w16, 1, 1), jnp.zeros_like(w16)


_mm_x.defvjp(_mm_x_fwd, _mm_x_bwd)


def _split16(a):
    hi = a.astype(BF16)
    lo = (a - hi.astype(F32)).astype(BF16)
    return hi, lo


def _dot3(a, b, ca, cb):
    ah, al = _split16(a)
    bh, bl = _split16(b)
    return _dg(ah, bh, ca, cb) + (_dg(ah, bl, ca, cb) + _dg(al, bh, ca, cb))


def _tri_inv_impl(a):
    n = a.shape[-1]
    shp = (1,) * (a.ndim - 2) + (n, n)
    eye = (lax.broadcasted_iota(jnp.int32, shp, a.ndim - 2) == lax.broadcasted_iota(jnp.int32, shp, a.ndim - 1)).astype(F32)
    x = eye - a
    p = a
    for it in range(5):
        dot = _dot3 if it < 2 else (lambda u, v, cu, cv: _dg(u.astype(BF16), v.astype(BF16), cu, cv))
        p = dot(p, p, 1, 0)
        x = x + dot(x, p, 1, 0)
    return x


@jax.custom_vjp
def _tri_inv(a):
    return _tri_inv_impl(a)


def _tri_inv_fwd(a):
    t = _tri_inv_impl(a)
    return t, t


def _tri_inv_bwd(t, ct):
    t16 = t.astype(BF16)
    return (-_dg(_dg(t16, ct.astype(BF16), 0, 0).astype(BF16), t16, 1, 1),)


_tri_inv.defvjp(_tri_inv_fwd, _tri_inv_bwd)


@jax.custom_vjp
def _tri_inv_saved(a, t):
    return t


def _tri_inv_saved_fwd(a, t):
    return t, t


def _tri_inv_saved_bwd(t, ct):
    return _tri_inv_bwd(t, ct) + (jnp.zeros_like(t),)


_tri_inv_saved.defvjp(_tri_inv_saved_fwd, _tri_inv_saved_bwd)


def _sigmoid(x):
    return 1.0 / (1.0 + jnp.exp(-x))


def _silu(x):
    return x * _sigmoid(x)


def _softplus(x):
    return jnp.maximum(x, 0.0) + jnp.log(1.0 + jnp.exp(-jnp.abs(x)))


def _rmsnorm(x, w):
    return x * lax.rsqrt(jnp.mean(x * x, axis=-1, keepdims=True) + NORM_EPS) * w


def _row_block(rows, cap):
    best = None
    for cand in range(SUBLANES, min(rows, cap) + 1, SUBLANES):
        if rows % cand == 0:
            best = cand
    assert best is not None, rows
    return best


def _mesh_peers():
    x, y, c = lax.axis_index("x"), lax.axis_index("y"), lax.axis_index("c")
    me = 4 * x + 2 * y + c
    peers = []
    for k in range(1, N_DEV):
        px = 1 - x if (k >> 2) & 1 else x
        py = 1 - y if (k >> 1) & 1 else y
        pc = 1 - c if k & 1 else c
        peers.append(((px, py, pc), 4 * px + 2 * py + pc))
    return me, peers


N_CHIPS = 4
OTHER_CHIPS = 3


def _chip_peers():
    x, y, c = lax.axis_index("x"), lax.axis_index("y"), lax.axis_index("c")
    return x, y, c, [(1 - x, y), (x, 1 - y), (1 - x, 1 - y)]


def _all_gather(shards):
    n_arr = len(shards)
    per = 1 + 2 * OTHER_CHIPS

    def body(*refs):
        in_refs, out_refs = refs[:n_arr], refs[n_arr:2 * n_arr]
        send_sems, recv_sems, loc_sems = refs[2 * n_arr:]
        x, y, c, chips = _chip_peers()
        me, sibling = (x, y, c), (x, y, 1 - c)

        def slot(px, py, pc):
            return 4 * px + 2 * py + pc

        def copy(i, k, block, to, src=None):
            dst = out_refs[i].at[slot(*block)]
            return pltpu.make_async_remote_copy(src_ref=dst if src is None else src, dst_ref=dst,
                                                send_sem=send_sems.at[i * per + k], recv_sem=recv_sems.at[i * per + k],
                                                device_id=to, device_id_type=MESH)

        own = [pltpu.make_async_copy(in_refs[i], out_refs[i].at[slot(*me)], loc_sems.at[i]) for i in range(n_arr)]
        for cp in own:
            cp.start()
        first = []
        for i in range(n_arr):
            first += [copy(i, 1 + j, me, (*chip, c), src=in_refs[i]) for j, chip in enumerate(chips)]
            first.append(copy(i, 0, me, sibling, src=in_refs[i]))
        for cp in first:
            cp.start()
        passed = []
        for j, chip in enumerate(chips):
            for i in range(n_arr):
                copy(i, 1 + j, (*chip, c), me).wait_recv()
                fwd = copy(i, 1 + OTHER_CHIPS + j, (*chip, c), sibling)
                fwd.start()
                passed.append(fwd)
        for i in range(n_arr):
            copy(i, 0, sibling, me).wait_recv()
            for j, chip in enumerate(chips):
                copy(i, 1 + OTHER_CHIPS + j, (*chip, 1 - c), me).wait_recv()
        for cp in first + passed:
            cp.wait_send()
        for cp in own:
            cp.wait()

    any_spec = pl.BlockSpec(memory_space=pl.ANY)
    return pl.pallas_call(
        body, name="weights_all_gather",
        out_shape=tuple(jax.ShapeDtypeStruct((N_DEV,) + a.shape, a.dtype) for a in shards),
        in_specs=[any_spec] * n_arr, out_specs=tuple([any_spec] * n_arr),
        scratch_shapes=[pltpu.SemaphoreType.DMA((n_arr * per,)), pltpu.SemaphoreType.DMA((n_arr * per,)),
                        pltpu.SemaphoreType.DMA((n_arr,))],
    )(*shards)


class _Exchange:
    def __init__(self, arrays, out_shapes, n_sem, copies):
        self.arrays, self.out_shapes, self.n_sem, self.copies = list(arrays), list(out_shapes), n_sem, copies


def _sibling_exchange(slabs):
    n_arr = len(slabs)

    def copies(in_refs, out_refs, send_sems, recv_sems, loc_sems):
        x, y, c, _ = _chip_peers()
        sends = [pltpu.make_async_remote_copy(src_ref=in_refs[i].at[2 * q + (1 - c)], dst_ref=out_refs[i].at[q],
                                              send_sem=send_sems.at[i * N_CHIPS + q], recv_sem=recv_sems.at[i * N_CHIPS + q],
                                              device_id=(x, y, 1 - c), device_id_type=MESH)
                 for i in range(n_arr) for q in range(N_CHIPS)]

        def start():
            for cp in sends:
                cp.start()

        def finish():
            for cp in sends:
                cp.wait_recv()
            for cp in sends:
                cp.wait_send()

        return start, finish

    return _Exchange(slabs, [jax.ShapeDtypeStruct((N_CHIPS,) + a.shape[1:], a.dtype) for a in slabs],
                     n_arr * N_CHIPS, copies)


def _pair_sum(slabs, from_sibling, core, name):
    _, rows, cols = slabs.shape
    tr = _row_block(rows, max(SUBLANES, (256 * 1024) // cols // SUBLANES * SUBLANES))

    def body(core_ref, a_ref, b_ref, o_ref):
        o_ref[...] = (a_ref[...] + b_ref[...]).astype(BF16)

    grid_spec = pltpu.PrefetchScalarGridSpec(
        num_scalar_prefetch=1, grid=(N_CHIPS, rows // tr),
        in_specs=[pl.BlockSpec((None, tr, cols), lambda q, r, core_ref: (2 * q + core_ref[0], r, 0)),
                  pl.BlockSpec((None, tr, cols), lambda q, r, core_ref: (q, r, 0))],
        out_specs=pl.BlockSpec((None, tr, cols), lambda q, r, core_ref: (q, r, 0)))
    return pl.pallas_call(
        body, name=name, grid_spec=grid_spec,
        out_shape=jax.ShapeDtypeStruct((N_CHIPS, rows, cols), BF16),
        compiler_params=_params("parallel", "parallel"),
    )(core, slabs, from_sibling)


def _chip_exchange(partials):
    n_arr = len(partials)

    def copies(in_refs, out_refs, send_sems, recv_sems, loc_sems):
        x, y, c, chips = _chip_peers()
        mine = 2 * x + y
        own = [pltpu.make_async_copy(in_refs[i].at[mine], out_refs[i].at[mine], loc_sems.at[i]) for i in range(n_arr)]

        def copy(i, j, chip, src_slot, dst_slot):
            return pltpu.make_async_remote_copy(src_ref=in_refs[i].at[src_slot], dst_ref=out_refs[i].at[dst_slot],
                                                send_sem=send_sems.at[i * OTHER_CHIPS + j],
                                                recv_sem=recv_sems.at[i * OTHER_CHIPS + j],
                                                device_id=(*chip, c), device_id_type=MESH)

        sends = [copy(i, j, chip, 2 * chip[0] + chip[1], mine) for j, chip in enumerate(chips) for i in range(n_arr)]
        recvs = [copy(i, j, chip, mine, 2 * chip[0] + chip[1]) for j, chip in enumerate(chips) for i in range(n_arr)]

        def start():
            for cp in own + sends:
                cp.start()

        def finish():
            for cp in recvs:
                cp.wait_recv()
            for cp in sends:
                cp.wait_send()
            for cp in own:
                cp.wait()

        return start, finish

    return _Exchange(partials, [jax.ShapeDtypeStruct(a.shape, a.dtype) for a in partials], n_arr * OTHER_CHIPS, copies)


def _small_all_reduce(part):
    rows = part.shape[0]

    def body(p_ref, o_ref, buf_ref, send_sems, recv_sems):
        me, peers = _mesh_peers()
        buf_ref[me] = p_ref[...]
        sends = []
        for k, (dev, pid) in enumerate(peers):
            cp = pltpu.make_async_remote_copy(src_ref=p_ref, dst_ref=buf_ref.at[me], send_sem=send_sems.at[k],
                                              recv_sem=recv_sems.at[k], device_id=dev, device_id_type=MESH)
            cp.start()
            sends.append(cp)
        for k, (dev, pid) in enumerate(peers):
            pltpu.make_async_remote_copy(src_ref=p_ref, dst_ref=buf_ref.at[pid], send_sem=send_sems.at[k],
                                         recv_sem=recv_sems.at[k], device_id=dev, device_id_type=MESH).wait_recv()
        for cp in sends:
            cp.wait_send()
        acc = buf_ref[0]
        for i in range(1, N_DEV):
            acc = acc + buf_ref[i]
        o_ref[...] = acc

    vmem = pl.BlockSpec(memory_space=pltpu.VMEM)
    return pl.pallas_call(
        body, name="small_all_reduce",
        out_shape=jax.ShapeDtypeStruct(part.shape, F32),
        in_specs=[vmem], out_specs=vmem,
        scratch_shapes=[pltpu.VMEM((N_DEV, rows, LANES), F32), pltpu.SemaphoreType.DMA((N_DEV - 1,)),
                        pltpu.SemaphoreType.DMA((N_DEV - 1,))],
    )(part)


def _adamw_vals(w, g, m, v):
    m = ADAM_B1 * m + (1.0 - ADAM_B1) * g
    v = ADAM_B2 * v + (1.0 - ADAM_B2) * (g * g)
    m_hat = m / (1.0 - ADAM_B1 ** ADAM_STEP)
    v_hat = v / (1.0 - ADAM_B2 ** ADAM_STEP)
    delta = -ADAM_LR * (m_hat / (jnp.sqrt(v_hat) + ADAM_EPS) + ADAM_WD * w)
    return delta, m, v


def _adamw(contrib, w, m, v, name):
    n, rows, cols = contrib.shape
    tr = _row_block(rows, max(SUBLANES, (128 * 1024) // cols // SUBLANES * SUBLANES))

    def body(c_ref, w_ref, m_ref, v_ref, g_ref, d_ref, nm_ref, nv_ref):
        g = c_ref[0].astype(F32)
        for i in range(1, n):
            g = g + c_ref[i].astype(F32)
        delta, nm, nv = _adamw_vals(w_ref[...], g, m_ref[...], v_ref[...])
        g_ref[...] = g
        d_ref[...] = delta
        nm_ref[...] = nm
        nv_ref[...] = nv

    row = pl.BlockSpec((tr, cols), lambda i: (i, 0))
    shp = jax.ShapeDtypeStruct((rows, cols), F32)
    return pl.pallas_call(
        body, name=name, grid=(rows // tr,),
        in_specs=[pl.BlockSpec((n, tr, cols), lambda i: (0, i, 0)), row, row, row],
        out_specs=(row, row, row, row), out_shape=(shp, shp, shp, shp),
        compiler_params=_params("parallel"),
    )(contrib, w, m, v)


def _lane_block(n, cap):
    if n <= cap:
        return n
    best = None
    for cand in range(LANES, cap + 1, LANES):
        if n % cand == 0:
            best = cand
    assert best is not None, n
    return best


def _matmul(a, b, out_dtype, name, mode="nn", tm=1024, tn=1024, tk=1024, exchange=None):
    g = a.shape[0]
    m, k = (a.shape[2], a.shape[1]) if mode == "tn" else (a.shape[1], a.shape[2])
    n = b.shape[1] if mode == "nt" else b.shape[2]
    tm, tn, tk = _lane_block(m, tm), _lane_block(n, tn), _lane_block(k, tk)
    nk = k // tk
    grid = (g, m // tm, n // tn, nk)
    a_spec = (pl.BlockSpec((None, tk, tm), lambda gi, i, j, kk: (gi, kk, i)) if mode == "tn" else
              pl.BlockSpec((None, tm, tk), lambda gi, i, j, kk: (gi, i, kk)))
    b_spec = (pl.BlockSpec((None, tn, tk), lambda gi, i, j, kk: (gi, j, kk)) if mode == "nt" else
              pl.BlockSpec((None, tk, tn), lambda gi, i, j, kk: (gi, kk, j)))
    ca, cb = (0 if mode == "tn" else 1), (1 if mode == "nt" else 0)
    n_ex = 0 if exchange is None else len(exchange.arrays)

    def body(a_ref, b_ref, *rest):
        ex_in, o_ref, ex_out, scratch = rest[:n_ex], rest[n_ex], rest[n_ex + 1:2 * n_ex + 1], rest[2 * n_ex + 1:]
        if exchange is not None:
            start, finish = exchange.copies(ex_in, ex_out, *scratch[-3:])
            pids = [pl.program_id(ax) for ax in range(4)]
            pl.when((pids[0] == 0) & (pids[1] == 0) & (pids[2] == 0) & (pids[3] == 0))(start)
        part = _dg(a_ref[...], b_ref[...], ca, cb)
        if nk == 1:
            o_ref[...] = part.astype(o_ref.dtype)
        else:
            acc_ref = scratch[0]
            kk = pl.program_id(3)

            @pl.when(kk == 0)
            def _():
                acc_ref[...] = part

            @pl.when((kk > 0) & (kk < nk - 1))
            def _():
                acc_ref[...] += part

            @pl.when(kk == nk - 1)
            def _():
                o_ref[...] = (acc_ref[...] + part).astype(o_ref.dtype)
        if exchange is not None:
            pl.when((pids[0] == grid[0] - 1) & (pids[1] == grid[1] - 1) & (pids[2] == grid[2] - 1)
                    & (pids[3] == grid[3] - 1))(finish)

    any_spec = pl.BlockSpec(memory_space=pl.ANY)
    scratch_shapes = [] if nk == 1 else [pltpu.VMEM((tm, tn), F32)]
    out_shape = [jax.ShapeDtypeStruct((g, m, n), out_dtype)]
    if exchange is not None:
        scratch_shapes += [pltpu.SemaphoreType.DMA((exchange.n_sem,)), pltpu.SemaphoreType.DMA((exchange.n_sem,)),
                           pltpu.SemaphoreType.DMA((n_ex,))]
        out_shape += exchange.out_shapes
    outs = pl.pallas_call(
        body, name=name, grid=grid,
        in_specs=[a_spec, b_spec] + [any_spec] * n_ex,
        out_specs=tuple([pl.BlockSpec((None, tm, tn), lambda gi, i, j, kk: (gi, i, j))] + [any_spec] * n_ex),
        out_shape=tuple(out_shape),
        scratch_shapes=scratch_shapes,
        compiler_params=(_params("parallel", "parallel", "parallel", "arbitrary") if exchange is None else
                         _params("arbitrary", "arbitrary", "arbitrary", "arbitrary")),
    )(a, b, *([] if exchange is None else exchange.arrays))
    return outs[0] if exchange is None else outs


def _rms_fwd(x, w):
    s, d = x.shape
    tm = _row_block(s, 512)

    def body(x_ref, w_ref, h_ref):
        h_ref[...] = _rmsnorm(x_ref[...], w_ref[...]).astype(BF16)

    return pl.pallas_call(
        body, name="input_rmsnorm", grid=(s // tm,),
        in_specs=[pl.BlockSpec((tm, d), lambda i: (i, 0)), pl.BlockSpec((1, d), lambda i: (0, 0))],
        out_specs=pl.BlockSpec((tm, d), lambda i: (i, 0)),
        out_shape=jax.ShapeDtypeStruct((s, d), BF16),
        compiler_params=_params("parallel"),
    )(x, w)


def _rms_bwd(x, w, dh_stacked, dh_parts, dx_res):
    s, d = x.shape
    tm = _row_block(s, 256)
    n_parts = 1 + len(dh_parts)

    def body(x_ref, w_ref, *rest):
        part_refs = rest[:n_parts]
        res_ref, gx_ref, gw_ref = rest[n_parts:]
        dh = part_refs[0][...]
        for r in part_refs[1:]:
            dh = dh + r[...]
        _, vjp = jax.vjp(_rmsnorm, x_ref[...], w_ref[...])
        dx, dw = vjp(dh)
        gx_ref[...] = dx + res_ref[...]

        @pl.when(pl.program_id(0) == 0)
        def _():
            gw_ref[...] = jnp.zeros_like(gw_ref)

        gw_ref[...] += dw

    row = pl.BlockSpec((tm, d), lambda i: (i, 0))
    vec = pl.BlockSpec((1, d), lambda i: (0, 0))
    return pl.pallas_call(
        body, name="input_rmsnorm_bwd", grid=(s // tm,),
        in_specs=[row, vec, pl.BlockSpec((None, tm, d), lambda i: (0, i, 0))] + [row] * (n_parts - 1) + [row],
        out_specs=(row, vec),
        out_shape=(jax.ShapeDtypeStruct((s, d), F32), jax.ShapeDtypeStruct((1, d), F32)),
        compiler_params=_params("arbitrary"),
    )(x, w, dh_stacked, *dh_parts, dx_res)


def _lane_masks(rows):
    lane = lax.broadcasted_iota(jnp.int32, (rows, LANES), 1)
    return lane < HEAD_DIM, (lane & (HEAD_DIM - 1)) < HEAD_DIM // 2


def _swap_halves(t, lo_half):
    return jnp.where(lo_half, pltpu.roll(t, LANES - HEAD_DIM // 2, 1), pltpu.roll(t, HEAD_DIM // 2, 1))


def _rope(t, cos, sin_signed, lo_half):
    return t * cos + _swap_halves(t, lo_half) * sin_signed


def _rope_bwd(d, cos, sin_signed, lo_half):
    return d * cos - _swap_halves(d, lo_half) * sin_signed


def _window_mask(first):
    qi = lax.broadcasted_iota(jnp.int32, (2 * ATT_BLOCK, 2 * ATT_BLOCK), 0) & (ATT_BLOCK - 1)
    kj = lax.broadcasted_iota(jnp.int32, (2 * ATT_BLOCK, 2 * ATT_BLOCK), 1)
    dist = qi + ATT_BLOCK - kj
    return (dist >= 0) & (dist <= N_BACK) & ((kj >= ATT_BLOCK) | jnp.logical_not(first))


def _stack_heads(t, head0):
    zero = jnp.zeros_like(t)
    return jnp.concatenate([jnp.where(head0, t, zero), jnp.where(head0, zero, t)], axis=0)


def _unstack_heads(t2, head0):
    return jnp.where(head0, t2[:ATT_BLOCK], t2[ATT_BLOCK:])


def _blocks_per_subsequence(g, nb):
    return lax.shift_right_logical(jnp.int32(nb), 2 * g)


def _attn_fwd(qkv, cos, sin):
    _, s, _ = qkv.shape
    nb = s // ATT_BLOCK

    def body(qkv_ref, cos_ref, sin_ref, o_ref, lse_ref, kp_ref, vp_ref):
        g, t = pl.program_id(0), pl.program_id(1)
        first = (t & (_blocks_per_subsequence(g, nb) - 1)) == 0

        @pl.when(first)
        def _():
            kp_ref[...] = jnp.zeros_like(kp_ref)
            vp_ref[...] = jnp.zeros_like(vp_ref)

        cos_b, sin_b = cos_ref[...], sin_ref[...]
        head0, lo_half = _lane_masks(ATT_BLOCK)
        valid = _window_mask(first)
        for sl in range(WIDTH // LANES):
            cq = pl.ds(sl * LANES, LANES)
            ck = pl.ds(WIDTH + sl * LANES, LANES)
            cv = pl.ds(2 * WIDTH + sl * LANES, LANES)
            qr = (_rope(qkv_ref[:, cq], cos_b, sin_b, lo_half) * (HEAD_DIM ** -0.5)).astype(BF16)
            kr = _rope(qkv_ref[:, ck], cos_b, sin_b, lo_half).astype(BF16)
            v16 = qkv_ref[:, cv].astype(BF16)
            kcat = jnp.concatenate([kp_ref[:, cq], kr], axis=0)
            vcat = jnp.concatenate([vp_ref[:, cq], v16], axis=0)
            sc = jnp.where(valid, _dg(_stack_heads(qr, head0), kcat, 1, 1), -jnp.inf)
            mx = jnp.max(sc, axis=1, keepdims=True)
            p = jnp.exp(sc - mx)
            den = jnp.sum(p, axis=1, keepdims=True)
            o_ref[:, cq] = _unstack_heads(_dg((p * (1.0 / den)).astype(BF16), vcat, 1, 0), head0)
            lse2 = mx + jnp.log(den)
            lse_ref[:, cq] = jnp.where(head0, lse2[:ATT_BLOCK], lse2[ATT_BLOCK:])
            kp_ref[:, cq] = kr
            vp_ref[:, cq] = v16

    blk = lambda w: pl.BlockSpec((None, ATT_BLOCK, w), lambda g, t: (g, t, 0))
    shp = jax.ShapeDtypeStruct((GROUPS, s, WIDTH), F32)
    return pl.pallas_call(
        body, name="dilated_attention_fwd", grid=(GROUPS, nb),
        in_specs=[blk(3 * WIDTH), blk(LANES), blk(LANES)],
        out_specs=(blk(WIDTH), blk(WIDTH)), out_shape=(shp, shp),
        scratch_shapes=[pltpu.VMEM((ATT_BLOCK, WIDTH), BF16), pltpu.VMEM((ATT_BLOCK, WIDTH), BF16)],
        compiler_params=_params("arbitrary", "arbitrary"),
    )(qkv, cos, sin)


def _attn_bwd(qkv, cos, sin, o, lse, do, dlse):
    _, s, _ = qkv.shape
    nb = s // ATT_BLOCK

    def body(qkv_ref, cos_ref, sin_ref, cosp_ref, sinp_ref, o_ref, lse_ref, do_ref, dlse_ref,
             dqkv_ref, kp_ref, vp_ref, dka_ref, dva_ref, dqp_ref):
        g, t = pl.program_id(0), pl.program_id(1)
        first = (t & (_blocks_per_subsequence(g, nb) - 1)) == 0
        active = t < nb
        head0, lo_half = _lane_masks(ATT_BLOCK)
        cos_p, sin_p = cosp_ref[...], sinp_ref[...]

        @pl.when(t == 0)
        def _():
            dka_ref[...] = jnp.zeros_like(dka_ref)
            dva_ref[...] = jnp.zeros_like(dva_ref)
            dqp_ref[...] = jnp.zeros_like(dqp_ref)

        dqkv_ref[:, pl.ds(0, WIDTH)] = dqp_ref[...]

        @pl.when(active & first)
        def _():
            kp_ref[...] = jnp.zeros_like(kp_ref)
            vp_ref[...] = jnp.zeros_like(vp_ref)

        @pl.when(active)
        def _():
            cos_b, sin_b = cos_ref[...], sin_ref[...]
            valid = _window_mask(first)
            for sl in range(WIDTH // LANES):
                cq = pl.ds(sl * LANES, LANES)
                ck = pl.ds(WIDTH + sl * LANES, LANES)
                cv = pl.ds(2 * WIDTH + sl * LANES, LANES)
                qr = (_rope(qkv_ref[:, cq], cos_b, sin_b, lo_half) * (HEAD_DIM ** -0.5)).astype(BF16)
                kr = _rope(qkv_ref[:, ck], cos_b, sin_b, lo_half).astype(BF16)
                v16 = qkv_ref[:, cv].astype(BF16)
                kcat = jnp.concatenate([kp_ref[:, cq], kr], axis=0)
                vcat = jnp.concatenate([vp_ref[:, cq], v16], axis=0)
                do_b = do_ref[:, cq]
                do16 = do_b.astype(BF16)
                lse_b = lse_ref[:, cq]
                cterm = dlse_ref[:, cq] - do_b * o_ref[:, cq]
                qms, doms, p16s, ds16s = [], [], [], []
                for hm in (head0, jnp.logical_not(head0)):
                    qm = jnp.where(hm, qr, jnp.zeros_like(qr))
                    dom = jnp.where(hm, do16, jnp.zeros_like(do16))
                    sc = jnp.where(valid[:ATT_BLOCK], _dg(qm, kcat, 1, 1), -jnp.inf)
                    lse_h = jnp.max(jnp.where(hm, lse_b, -jnp.inf), axis=1, keepdims=True)
                    c = jnp.sum(jnp.where(hm, cterm, 0.0), axis=1, keepdims=True)
                    p = jnp.exp(sc - lse_h)
                    ds16s.append((p * (_dg(dom, vcat, 1, 1) + c)).astype(BF16))
                    p16s.append(p.astype(BF16))
                    qms.append(qm)
                    doms.append(dom)
                p2, ds2 = jnp.concatenate(p16s, axis=0), jnp.concatenate(ds16s, axis=0)
                dvc = _dg(p2, jnp.concatenate(doms, axis=0), 0, 0)
                dkc = _dg(ds2, jnp.concatenate(qms, axis=0), 0, 0)
                dq = _unstack_heads(_dg(ds2, kcat, 1, 0), head0) * (HEAD_DIM ** -0.5)
                dqp_ref[:, cq] = _rope_bwd(dq, cos_b, sin_b, lo_half).astype(BF16)
                dqkv_ref[:, ck] = _rope_bwd(dka_ref[:, cq] + dkc[:ATT_BLOCK], cos_p, sin_p, lo_half).astype(BF16)
                dqkv_ref[:, cv] = (dva_ref[:, cq] + dvc[:ATT_BLOCK]).astype(BF16)
                dka_ref[:, cq] = dkc[ATT_BLOCK:]
                dva_ref[:, cq] = dvc[ATT_BLOCK:]
                kp_ref[:, cq] = kr
                vp_ref[:, cq] = v16

        @pl.when(jnp.logical_not(active))
        def _():
            for sl in range(WIDTH // LANES):
                cq = pl.ds(sl * LANES, LANES)
                dqkv_ref[:, pl.ds(WIDTH + sl * LANES, LANES)] = _rope_bwd(dka_ref[:, cq], cos_p, sin_p, lo_half).astype(BF16)
                dqkv_ref[:, pl.ds(2 * WIDTH + sl * LANES, LANES)] = dva_ref[:, cq].astype(BF16)

    cur = lambda w: pl.BlockSpec((None, ATT_BLOCK, w), lambda g, t: (g, jnp.minimum(t, nb - 1), 0))
    prev = lambda w: pl.BlockSpec((None, ATT_BLOCK, w), lambda g, t: (g, jnp.maximum(t - 1, 0), 0))
    return pl.pallas_call(
        body, name="dilated_attention_bwd", grid=(GROUPS, nb + 1),
        in_specs=[cur(3 * WIDTH), cur(LANES), cur(LANES), prev(LANES), prev(LANES),
                  cur(WIDTH), cur(WIDTH), cur(WIDTH), cur(WIDTH)],
        out_specs=prev(3 * WIDTH), out_shape=jax.ShapeDtypeStruct((GROUPS, s, 3 * WIDTH), BF16),
        scratch_shapes=[pltpu.VMEM((ATT_BLOCK, WIDTH), BF16), pltpu.VMEM((ATT_BLOCK, WIDTH), BF16),
                        pltpu.VMEM((ATT_BLOCK, WIDTH), F32), pltpu.VMEM((ATT_BLOCK, WIDTH), F32),
                        pltpu.VMEM((ATT_BLOCK, WIDTH), BF16)],
        compiler_params=_params("arbitrary", "arbitrary"),
    )(qkv, cos, sin, cos, sin, o, lse, do, dlse)


CONV_PAD = SUBLANES


def _gdn_post(y, scale):
    head0, _ = _lane_masks(y.shape[0])
    c = _silu(y)
    sq = c * c
    ss0 = jnp.sum(jnp.where(head0, sq, 0.0), axis=1, keepdims=True)
    ss1 = jnp.sum(jnp.where(head0, 0.0, sq), axis=1, keepdims=True)
    r = jnp.where(head0, lax.rsqrt(ss0 + NORM_EPS), lax.rsqrt(ss1 + NORM_EPS))
    return c * r * scale


def _block_kind(j, nq):
    return j < 2 * nq, jnp.where(j < nq, HEAD_DIM ** -0.5, 1.0).astype(F32)


def _conv_rows(xp_ref, w, c0, rows):
    y = w[0:1, :] * xp_ref[pl.ds(c0 + CONV_PAD - (CONV_K - 1), rows), :]
    for k in range(1, CONV_K):
        y = y + w[k:k + 1, :] * xp_ref[pl.ds(c0 + CONV_PAD - (CONV_K - 1) + k, rows), :]
    return y


def _gdn_pre_fwd(proj_r, conv8, col0):
    s = proj_r.shape[0]
    tr = _row_block(s, 512)
    nblk = QKV_B // LANES
    nq = WIDTH // LANES

    def body(x_ref, w_ref, out_ref, xp_ref):
        normed, scale = _block_kind(pl.program_id(0), nq)
        xp_ref[pl.ds(0, CONV_PAD), :] = jnp.zeros((CONV_PAD, LANES), F32)
        xp_ref[pl.ds(CONV_PAD, s), :] = x_ref[...]
        w = w_ref[...]

        @pl.when(normed)
        def _():
            for c in range(s // tr):
                out_ref[pl.ds(c * tr, tr), :] = _gdn_post(_conv_rows(xp_ref, w, c * tr, tr), scale)

        @pl.when(jnp.logical_not(normed))
        def _():
            for c in range(s // tr):
                out_ref[pl.ds(c * tr, tr), :] = _silu(_conv_rows(xp_ref, w, c * tr, tr))

    return pl.pallas_call(
        body, name="gdn_conv_fwd", grid=(nblk,),
        in_specs=[pl.BlockSpec((s, LANES), lambda j: (0, col0 + j)), pl.BlockSpec((SUBLANES, LANES), lambda j: (0, j))],
        out_specs=pl.BlockSpec((s, LANES), lambda j: (0, j)),
        out_shape=jax.ShapeDtypeStruct((s, QKV_B), F32),
        scratch_shapes=[pltpu.VMEM((s + CONV_PAD, LANES), F32)],
        compiler_params=_params("parallel"),
    )(proj_r, conv8)


def _gdn_pre_bwd(proj_r, conv8, dc, col0):
    s = proj_r.shape[0]
    tr = _row_block(s, 512)
    nblk = QKV_B // LANES
    nq = WIDTH // LANES

    def body(x_ref, w_ref, dc_ref, dx_ref, dw_ref, xp_ref, dyp_ref):
        normed, scale = _block_kind(pl.program_id(0), nq)
        xp_ref[pl.ds(0, CONV_PAD), :] = jnp.zeros((CONV_PAD, LANES), F32)
        xp_ref[pl.ds(CONV_PAD, s), :] = x_ref[...]
        dyp_ref[pl.ds(s, CONV_PAD), :] = jnp.zeros((CONV_PAD, LANES), F32)
        w = w_ref[...]

        def conv_output_cotangents(post):
            for c in range(s // tr):
                _, vjp = jax.vjp(post, _conv_rows(xp_ref, w, c * tr, tr))
                dyp_ref[pl.ds(c * tr, tr), :] = vjp(dc_ref[pl.ds(c * tr, tr), :])[0]

        pl.when(normed)(lambda: conv_output_cotangents(lambda yy: _gdn_post(yy, scale)))
        pl.when(jnp.logical_not(normed))(lambda: conv_output_cotangents(_silu))
        dws = [jnp.zeros((1, LANES), F32) for _ in range(CONV_K)]
        for c in range(s // tr):
            c0 = c * tr
            dy = dyp_ref[pl.ds(c0, tr), :]
            dx = w[0:1, :] * dyp_ref[pl.ds(c0 + CONV_K - 1, tr), :]
            for k in range(1, CONV_K):
                dx = dx + w[k:k + 1, :] * dyp_ref[pl.ds(c0 + CONV_K - 1 - k, tr), :]
            dx_ref[pl.ds(c0, tr), :] = dx.astype(BF16)
            for k in range(CONV_K):
                xs = xp_ref[pl.ds(c0 + CONV_PAD - (CONV_K - 1) + k, tr), :]
                dws[k] = dws[k] + jnp.sum(dy * xs, axis=0, keepdims=True)
        row = lax.broadcasted_iota(jnp.int32, (SUBLANES, LANES), 0)
        dwb = jnp.zeros((SUBLANES, LANES), F32)
        for k in range(CONV_K):
            dwb = dwb + jnp.where(row == k, dws[k], 0.0)
        dw_ref[...] = dwb

    return pl.pallas_call(
        body, name="gdn_conv_bwd", grid=(nblk,),
        in_specs=[pl.BlockSpec((s, LANES), lambda j: (0, col0 + j)), pl.BlockSpec((SUBLANES, LANES), lambda j: (0, j)),
                  pl.BlockSpec((s, LANES), lambda j: (0, j))],
        out_specs=(pl.BlockSpec((s, LANES), lambda j: (0, j)), pl.BlockSpec((SUBLANES, LANES), lambda j: (0, j))),
        out_shape=(jax.ShapeDtypeStruct((s, QKV_B), BF16), jax.ShapeDtypeStruct((SUBLANES, QKV_B), F32)),
        scratch_shapes=[pltpu.VMEM((s + CONV_PAD, LANES), F32), pltpu.VMEM((s + CONV_PAD, LANES), F32)],
        compiler_params=_params("parallel"),
    )(proj_r, conv8, dc)


def _gdn_chunk(q, k, v, bcol, acol, alog, dtb, gnw, state, t_saved=None):
    n = q.shape[-2]
    shp = (1, n, n)
    row = lax.broadcasted_iota(jnp.int32, shp, 1)
    col = lax.broadcasted_iota(jnp.int32, shp, 2)
    beta = _sigmoid(bcol)
    g = -jnp.exp(alog) * _softplus(acol + dtb)
    g_row = jnp.sum(jnp.where(row == col, g, 0.0), axis=-2, keepdims=True)
    big_g = jnp.sum(jnp.where(row >= col, g_row, 0.0), axis=-1, keepdims=True)
    big_g_row = jnp.sum(jnp.where(row <= col, g, 0.0), axis=-2, keepdims=True)
    decay_incl = jnp.exp(jnp.where(row >= col, big_g - big_g_row, -jnp.inf))
    decay_strict = jnp.where(row > col, decay_incl, 0.0)
    k_beta = k * beta
    a_mat = _mm_nt(k_beta, k) * decay_strict
    t_inv = _tri_inv(a_mat) if t_saved is None else _tri_inv_saved(a_mat, t_saved)
    e_g = jnp.exp(big_g)
    u = _mm(t_inv, v * beta)
    w = _mm(t_inv, k_beta * e_g)
    attn = _mm_nt(q, k) * decay_incl
    v_new = u - _mm(w, state)
    o = _mm(q * e_g, state) + _mm(attn, v_new)
    total = jnp.sum(g, axis=-2, keepdims=True)
    new_state = state * jnp.exp(total) + _mm_tn(k * jnp.exp(total - big_g), v_new)
    return _rmsnorm(o, gnw), new_state, t_inv


def _split_heads(x):
    return jnp.stack([x[:, h * HEAD_DIM:(h + 1) * HEAD_DIM] for h in range(HEADS)], axis=0)


def _merge_heads(x):
    return jnp.concatenate([x[h] for h in range(HEADS)], axis=1)


def _logit_columns(ba):
    lane = lax.broadcasted_iota(jnp.int32, ba.shape, 1)

    def cols(off):
        return jnp.stack([jnp.sum(jnp.where(lane == off + h, ba, 0.0), axis=1, keepdims=True) for h in range(HEADS)], axis=0)

    return cols(0), cols(HEADS)


def _logit_block(dbc, dac, shape):
    lane = lax.broadcasted_iota(jnp.int32, shape, 1)
    out = jnp.zeros(shape, F32)
    for h in range(HEADS):
        out = out + jnp.where(lane == h, dbc[h], 0.0) + jnp.where(lane == HEADS + h, dac[h], 0.0)
    return out


def _gdn_scan_fwd(cqkv, proj_r, ba_col, alog, dtb, gnw):
    s = cqkv.shape[0]
    nc = s // CHUNK
    span = SCAN_CHUNKS * CHUNK

    def body(q_ref, k_ref, v_ref, ba_ref, al_ref, dt_ref, gnw_ref, o_ref, st_ref, ti_ref, state_ref):
        @pl.when(pl.program_id(0) == 0)
        def _():
            state_ref[...] = jnp.zeros_like(state_ref)

        st = state_ref[...]
        for u in range(SCAN_CHUNKS):
            rows = pl.ds(u * CHUNK, CHUNK)
            st_ref[u] = st
            bcol, acol = _logit_columns(ba_ref[rows, :])
            o, st, t_inv = _gdn_chunk(_split_heads(q_ref[rows, :]), _split_heads(k_ref[rows, :]),
                                      _split_heads(v_ref[rows, :]), bcol, acol, al_ref[...], dt_ref[...], gnw_ref[...], st)
            o_ref[rows, :] = _merge_heads(o)
            ti_ref[u] = t_inv
        state_ref[...] = st

    part = lambda i: pl.BlockSpec((span, WIDTH), lambda n: (n, i))
    par = pl.BlockSpec((HEADS, 1, 1), lambda n: (0, 0, 0))
    per_chunk = pl.BlockSpec((SCAN_CHUNKS, HEADS, HEAD_DIM, HEAD_DIM), lambda n: (n, 0, 0, 0))
    per_chunk_shape = jax.ShapeDtypeStruct((nc, HEADS, HEAD_DIM, HEAD_DIM), F32)
    return pl.pallas_call(
        body, name="gdn_scan_fwd", grid=(nc // SCAN_CHUNKS,),
        in_specs=[part(0), part(1), part(2), pl.BlockSpec((span, LANES), lambda n: (n, ba_col)), par, par,
                  pl.BlockSpec((1, 1, HEAD_DIM), lambda n: (0, 0, 0))],
        out_specs=(part(0), per_chunk, per_chunk),
        out_shape=(jax.ShapeDtypeStruct((s, WIDTH), F32), per_chunk_shape, per_chunk_shape),
        scratch_shapes=[pltpu.VMEM((HEADS, HEAD_DIM, HEAD_DIM), F32)],
        compiler_params=_params("arbitrary"),
    )(cqkv, cqkv, cqkv, proj_r, alog, dtb, gnw)


def _gdn_scan_bwd(cqkv, proj_r, ba_col, alog, dtb, gnw, states, t_invs, do):
    s = cqkv.shape[0]
    nc = s // CHUNK
    span = SCAN_CHUNKS * CHUNK
    n_steps = nc // SCAN_CHUNKS

    def body(q_ref, k_ref, v_ref, ba_ref, al_ref, dt_ref, gnw_ref, st_ref, ti_ref, do_ref,
             dqkv_ref, dba_ref, dal_ref, ddt_ref, dgnw_ref, dstate_ref):
        @pl.when(pl.program_id(0) == 0)
        def _():
            dstate_ref[...] = jnp.zeros_like(dstate_ref)
            dal_ref[...] = jnp.zeros_like(dal_ref)
            ddt_ref[...] = jnp.zeros_like(ddt_ref)
            dgnw_ref[...] = jnp.zeros_like(dgnw_ref)

        dst = dstate_ref[...]
        for u in reversed(range(SCAN_CHUNKS)):
            rows = pl.ds(u * CHUNK, CHUNK)
            bcol, acol = _logit_columns(ba_ref[rows, :])
            _, vjp = jax.vjp(lambda *a, t_saved=ti_ref[u]: _gdn_chunk(*a, t_saved=t_saved)[:2],
                             _split_heads(q_ref[rows, :]), _split_heads(k_ref[rows, :]), _split_heads(v_ref[rows, :]),
                             bcol, acol, al_ref[...], dt_ref[...], gnw_ref[...], st_ref[u])
            dq, dk, dv, dbc, dac, dal, ddt, dgn, dst = vjp((_split_heads(do_ref[rows, :]), dst))
            dqkv_ref[rows, pl.ds(0, WIDTH)] = _merge_heads(dq)
            dqkv_ref[rows, pl.ds(WIDTH, WIDTH)] = _merge_heads(dk)
            dqkv_ref[rows, pl.ds(2 * WIDTH, WIDTH)] = _merge_heads(dv)
            dba_ref[rows, :] = _logit_block(dbc, dac, (CHUNK, LANES))
            dal_ref[...] += dal
            ddt_ref[...] += ddt
            dgnw_ref[...] += dgn
        dstate_ref[...] = dst

    rev = lambda n: n_steps - 1 - n
    part = lambda i: pl.BlockSpec((span, WIDTH), lambda n: (rev(n), i))
    par = pl.BlockSpec((HEADS, 1, 1), lambda n: (0, 0, 0))
    vec = pl.BlockSpec((1, 1, HEAD_DIM), lambda n: (0, 0, 0))
    par_shape = jax.ShapeDtypeStruct((HEADS, 1, 1), F32)
    per_chunk = pl.BlockSpec((SCAN_CHUNKS, HEADS, HEAD_DIM, HEAD_DIM), lambda n: (rev(n), 0, 0, 0))
    return pl.pallas_call(
        body, name="gdn_scan_bwd", grid=(n_steps,),
        in_specs=[part(0), part(1), part(2), pl.BlockSpec((span, LANES), lambda n: (rev(n), ba_col)), par, par, vec,
                  per_chunk, per_chunk, part(0)],
        out_specs=(pl.BlockSpec((span, QKV_B), lambda n: (rev(n), 0)), pl.BlockSpec((span, LANES), lambda n: (rev(n), 0)),
                   par, par, vec),
        out_shape=(jax.ShapeDtypeStruct((s, QKV_B), F32), jax.ShapeDtypeStruct((s, LANES), F32), par_shape, par_shape,
                   jax.ShapeDtypeStruct((1, 1, HEAD_DIM), F32)),
        scratch_shapes=[pltpu.VMEM((HEADS, HEAD_DIM, HEAD_DIM), F32)],
        compiler_params=_params("arbitrary"),
    )(cqkv, cqkv, cqkv, proj_r, alog, dtb, gnw, states, t_invs, do)


def _tail_loss(x, tgt, o0, o1, o2, l0, l1, l2, ga, gb, za, zb, ob, fnw, wua, wub, wo, tap_a, tap_b, tap_o):
    lm = jnp.maximum(jnp.maximum(l0, l1), l2)
    e0, e1, e2 = jnp.exp(l0 - lm), jnp.exp(l1 - lm), jnp.exp(l2 - lm)
    o_a = (e0 * o0 + e1 * o1 + e2 * o2) / (e0 + e1 + e2)
    xa, xb = o_a * _silu(za), ob * _silu(zb)
    y_a = _mm_x(xa, wua) + tap_a
    y_b = _mm_x(xb, wub) + tap_b
    merged = _sigmoid(ga) * y_a + _sigmoid(gb) * y_b
    y = _rmsnorm(x + _mm_x(merged, wo) + tap_o, fnw)
    err = y - tgt
    per_token = jnp.sum(err * err, axis=1, keepdims=True) * (0.5 / x.shape[1])
    return jnp.sum(per_token, axis=0, keepdims=True), (xa.astype(BF16), xb.astype(BF16), merged.astype(BF16))


def _tail(x, tgt, o_all, lse_all, og12, lg12, proj_r, ob, wua, wub, wo, fnw):
    s, d = x.shape
    tm = _row_block(s, TAIL_ROWS)
    col_za = 2 * d // WIDTH
    col_zb = (2 * d + WIDTH + QKV_B) // WIDTH

    def body(x_ref, t_ref, o0_ref, o1_ref, o2_ref, l0_ref, l1_ref, l2_ref, ga_ref, gb_ref, za_ref, zb_ref, ob_ref,
             wua_ref, wub_ref, wo_ref, fnw_ref,
             loss_ref, dx_ref, do0_ref, do1_ref, do2_ref, dl0_ref, dl1_ref, dl2_ref, dga_ref, dgb_ref, dza_ref,
             dzb_ref, dob_ref, xa_ref, xb_ref, mg_ref, dya_ref, dyb_ref, dmo_ref, dfnw_ref):
        @pl.when(pl.program_id(0) == 0)
        def _():
            for r in (loss_ref, dfnw_ref):
                r[...] = jnp.zeros_like(r)

        tap = jnp.zeros((tm, d), F32)
        args = (x_ref[...], t_ref[...], o0_ref[...], o1_ref[...], o2_ref[...], l0_ref[...], l1_ref[...], l2_ref[...],
                ga_ref[...], gb_ref[...], za_ref[...], zb_ref[...], ob_ref[...], fnw_ref[...],
                wua_ref[...], wub_ref[...], wo_ref[...], tap, tap, tap)
        loss, vjp, (xa16, xb16, mg16) = jax.vjp(_tail_loss, *args, has_aux=True)
        (dx, _, do0, do1, do2, dl0, dl1, dl2, dga, dgb, dza, dzb, dob, dfnw, _, _, _, dya, dyb, dmo) = vjp(
            jnp.ones((1, 1), F32))
        loss_ref[...] += jnp.broadcast_to(loss, loss_ref.shape)
        dx_ref[...] = dx
        do0_ref[...], do1_ref[...], do2_ref[...] = do0, do1, do2
        dl0_ref[...], dl1_ref[...], dl2_ref[...] = dl0, dl1, dl2
        dga_ref[...] = dga.astype(BF16)
        dgb_ref[...] = dgb.astype(BF16)
        dza_ref[...] = dza.astype(BF16)
        dzb_ref[...] = dzb.astype(BF16)
        dob_ref[...] = dob
        xa_ref[...], xb_ref[...], mg_ref[...] = xa16, xb16, mg16
        dya_ref[...] = dya.astype(BF16)
        dyb_ref[...] = dyb.astype(BF16)
        dmo_ref[...] = dmo.astype(BF16)
        dfnw_ref[...] += dfnw

    row = lambda w, c=0: pl.BlockSpec((tm, w), lambda i: (i, c))
    grp0 = pl.BlockSpec((None, tm, WIDTH), lambda i: (0, i, 0))
    full = lambda a, b: pl.BlockSpec((a, b), lambda i: (0, 0))
    f32 = lambda a, b: jax.ShapeDtypeStruct((a, b), F32)
    b16 = lambda a, b: jax.ShapeDtypeStruct((a, b), BF16)
    stacked = jax.ShapeDtypeStruct((GROUPS, s, WIDTH), F32)
    gspecs = [grp0, row(WIDTH), row(WIDTH)]
    in_specs = ([row(d), row(d)] + gspecs * 2 + [row(d, 0), row(d, 1), row(WIDTH, col_za), row(WIDTH, col_zb),
                row(WIDTH), full(WIDTH, d), full(WIDTH, d), full(d, d), full(1, d)])
    out_specs = ([full(SUBLANES, LANES), row(d)] + gspecs * 2 + [row(d), row(d), row(WIDTH), row(WIDTH), row(WIDTH),
                 row(WIDTH), row(WIDTH), row(d), row(d), row(d), row(d), full(1, d)])
    gshapes = [stacked, f32(s, WIDTH), f32(s, WIDTH)]
    out_shape = ([f32(SUBLANES, LANES), f32(s, d)] + gshapes * 2 + [b16(s, d), b16(s, d), b16(s, WIDTH),
                 b16(s, WIDTH), f32(s, WIDTH), b16(s, WIDTH), b16(s, WIDTH), b16(s, d), b16(s, d), b16(s, d), b16(s, d),
                 f32(1, d)])
    return pl.pallas_call(
        body, name="tail_fwd_bwd", grid=(s // tm,),
        in_specs=in_specs, out_specs=tuple(out_specs), out_shape=tuple(out_shape),
        compiler_params=pltpu.CompilerParams(dimension_semantics=("arbitrary",), vmem_limit_bytes=TAIL_VMEM_LIMIT),
    )(x, tgt, o_all, og12[0], og12[1], lse_all, lg12[0], lg12[1], proj_r, proj_r, proj_r, proj_r, ob, wua, wub, wo, fnw)


PERMUTE_SPAN = 4096


def _permute_span(s):
    return PERMUTE_SPAN if s % PERMUTE_SPAN == 0 else s


def _from_dilated_rows(stacked, g, dil, name):
    n_slots, s, c = stacked.shape
    view = stacked.reshape(n_slots, dil, s // dil, c)
    span = _permute_span(s)

    def body(in_ref, out_ref):
        for r in range(dil):
            out_ref[pl.ds(r, span // dil, stride=dil), :] = in_ref[r]

    return pl.pallas_call(
        body, name=name, grid=(s // span, c // LANES),
        in_specs=[pl.BlockSpec((None, dil, span // dil, LANES), lambda n, j: (g, 0, n, j))],
        out_specs=pl.BlockSpec((span, LANES), lambda n, j: (n, j)),
        out_shape=jax.ShapeDtypeStruct((s, c), stacked.dtype),
        compiler_params=_params("parallel", "parallel"),
    )(view)


def _to_dilated_rows_into(nat, stacked, g, dil, name):
    n_slots, s, c = stacked.shape
    view = stacked.reshape(n_slots, dil, s // dil, c)
    span = _permute_span(s)

    def body(nat_ref, old_ref, out_ref):
        for r in range(dil):
            out_ref[r] = nat_ref[pl.ds(r, span // dil, stride=dil), :]

    out = pl.pallas_call(
        body, name=name, grid=(s // span, c // LANES),
        in_specs=[pl.BlockSpec((span, LANES), lambda n, j: (n, j)), pl.BlockSpec(memory_space=pl.ANY)],
        out_specs=pl.BlockSpec((None, dil, span // dil, LANES), lambda n, j: (g, 0, n, j)),
        out_shape=jax.ShapeDtypeStruct(view.shape, stacked.dtype),
        input_output_aliases={1: 0},
        compiler_params=_params("parallel", "parallel"),
    )(nat, view)
    return out.reshape(stacked.shape)


def _to_dilated(a, dil):
    if dil == 1:
        return a
    s = a.shape[0]
    return a.reshape(s // dil, dil, -1).transpose(1, 0, 2).reshape(a.shape)


def _rope_tables(s):
    inv_freq = ROPE_THETA ** (-jnp.arange(0, HEAD_DIM, 2, dtype=F32) / HEAD_DIM)
    ang = jnp.arange(s, dtype=F32)[:, None] * inv_freq[None, :]
    cos_n = jnp.tile(jnp.cos(ang), (1, 2 * LANES // HEAD_DIM))
    sin_h = jnp.sin(ang)
    sin_n = jnp.tile(jnp.concatenate([-sin_h, sin_h], axis=1), (1, LANES // HEAD_DIM))

    def per_group(table, tag):
        out = jnp.broadcast_to(table, (GROUPS,) + table.shape)
        for g in range(1, GROUPS):
            out = _to_dilated_rows_into(table, out, g, DILATIONS[g], "rope_%s_to_dilated_%d" % (tag, g))
        return out

    return per_group(cos_n, "cos"), per_group(sin_n, "sin")


def _regroup_columns(pieces, widths):
    starts, pos = [], 0
    for p in pieces:
        starts.append(pos)
        pos += p.shape[1]
    assert pos == sum(widths), (pos, widths)
    out, lo = [], 0
    for w in widths:
        hi, parts = lo + w, []
        for p, st in zip(pieces, starts):
            a, b = max(lo, st), min(hi, st + p.shape[1])
            if a < b:
                parts.append(p[:, a - st:b - st])
        out.append(parts[0] if len(parts) == 1 else jnp.concatenate(parts, axis=1))
        lo = hi
    return out


def _pack_rows(parts, dtype, row_multiple):
    flat = jnp.concatenate([p.reshape(-1).astype(dtype) for p in parts])
    tile = row_multiple * LANES
    pad = (-flat.shape[0]) % tile
    return jnp.pad(flat, (0, pad)).reshape(-1, LANES)


def _unpack_rows(packed, shapes):
    flat = packed.reshape(-1)
    out, start = [], 0
    for shp in shapes:
        size = 1
        for n in shp:
            size *= n
        out.append(flat[start:start + size].reshape(shp))
        start += size
    return out


def kernel(x, norm_w, w_in, conv_w, a_log, dt_bias, gdn_norm_w, w_up_a, w_up_b, w_out, final_norm_w, loss_target, m_norm_w, m_w_in, m_conv_w, m_a_log, m_dt_bias, m_gdn_norm_w, m_w_up_a, m_w_up_b, m_w_out, m_final_norm_w, v_norm_w, v_w_in, v_conv_w, v_a_log, v_dt_bias, v_gdn_norm_w, v_w_up_a, v_w_up_b, v_w_out, v_final_norm_w):
    x2, tgt = x[0], loss_target[0]
    s, d = x2.shape
    me = 4 * lax.axis_index("x") + 2 * lax.axis_index("y") + lax.axis_index("c")
    win8 = w_in.shape[2]
    conv8w = conv_w.shape[2]

    conv_shard = jnp.pad(conv_w[0], ((0, SUBLANES - CONV_K), (0, 0)))
    w_in_g, wua_g, wub_g, wo_g, conv_g = _all_gather(
        [w_in[0].astype(BF16), w_up_a[0].astype(BF16), w_up_b[0].astype(BF16), w_out[0].astype(BF16), conv_shard])
    wua = jnp.concatenate([wua_g[i] for i in range(N_DEV)], axis=1)
    wub = jnp.concatenate([wub_g[i] for i in range(N_DEV)], axis=1)
    wo = wo_g.reshape(d, d)
    conv8 = jnp.concatenate([conv_g[i] for i in range(N_DEV)], axis=1)

    seg_widths = [QKV_B] * GROUPS + [WIDTH, QKV_B, WIDTH, 2 * HEADS, 2 * d]
    wq0, wq1, wq2, w_za, w_qkvb, w_zb, w_ba, w_gates = _regroup_columns([w_in_g[i] for i in range(N_DEV)], seg_widths)
    w_qkv = jnp.stack([wq0, wq1, wq2])
    w_rest = jnp.concatenate([w_gates, w_za, w_qkvb, w_zb, w_ba,
                              jnp.zeros((d, BA_PAD - 2 * HEADS), BF16)], axis=1)
    col_qkvb = (2 * d + WIDTH) // LANES
    col_ba = (2 * d + 2 * WIDTH + QKV_B) // LANES

    h = _rms_fwd(x2, norm_w)
    h_all = jnp.stack([_to_dilated(h, dil) for dil in DILATIONS])
    qkv_all = _matmul(h_all, w_qkv, F32, "in_proj_attention", tn=QKV_B)
    proj_r = _matmul(h[None], w_rest[None], F32, "in_proj_rest", tn=2560)[0]
    cos, sin = _rope_tables(s)
    o_all, lse_all = _attn_fwd(qkv_all, cos, sin)
    og12 = [_from_dilated_rows(o_all, g, DILATIONS[g], "attn_out_to_natural_%d" % g) for g in (1, 2)]
    lg12 = [_from_dilated_rows(lse_all, g, DILATIONS[g], "attn_lse_to_natural_%d" % g) for g in (1, 2)]

    cqkv = _gdn_pre_fwd(proj_r, conv8, col_qkvb)
    alog3, dtb3, gnw3 = a_log.reshape(HEADS, 1, 1), dt_bias.reshape(HEADS, 1, 1), gdn_norm_w.reshape(1, 1, HEAD_DIM)
    ob, states, t_invs = _gdn_scan_fwd(cqkv, proj_r, col_ba, alog3, dtb3, gnw3)

    (loss_blk, dx_res, do_all, do1, do2, dl_all, dl1, dl2, dga, dgb, dza, dzb, dob,
     xa16, xb16, mg16, dya16, dyb16, dmo16, dfnw) = _tail(
        x2, tgt, o_all, lse_all, og12, lg12, proj_r, ob, wua, wub, wo, final_norm_w.reshape(1, d))
    dwua = _matmul(xa16[None], dya16[None], F32, "up_a_dw", mode="tn", tk=2048)[0]
    dwub = _matmul(xb16[None], dyb16[None], F32, "up_b_dw", mode="tn", tk=2048)[0]
    dwo = _matmul(mg16[None], dmo16[None], F32, "out_proj_dw", mode="tn", tk=2048)[0]

    for g, (t_o, t_l) in ((1, (do1, dl1)), (2, (do2, dl2))):
        do_all = _to_dilated_rows_into(t_o, do_all, g, DILATIONS[g], "attn_dout_to_dilated_%d" % g)
        dl_all = _to_dilated_rows_into(t_l, dl_all, g, DILATIONS[g], "attn_dlse_to_dilated_%d" % g)
    dqkv_all = _attn_bwd(qkv_all, cos, sin, o_all, lse_all, do_all, dl_all)

    dcqkv, dba, dalog3, ddtb3, dgnw3 = _gdn_scan_bwd(cqkv, proj_r, col_ba, alog3, dtb3, gnw3, states, t_invs, dob)
    dqkv_b, dconv8 = _gdn_pre_bwd(proj_r, conv8, dcqkv, col_qkvb)
    dproj_r = jnp.concatenate([dga, dgb, dza, dqkv_b, dzb,
                               jnp.pad(dba.astype(BF16), ((0, 0), (0, BA_PAD - LANES)))], axis=1)

    def col_slabs(a, width):
        return jnp.stack([a[:, j * width:(j + 1) * width] for j in range(N_DEV)])

    core = lax.axis_index("c").astype(jnp.int32).reshape(1)
    small_slabs = [col_slabs(dwua, d // N_DEV), col_slabs(dwub, d // N_DEV), dwo.reshape(N_DEV, d // N_DEV, d)]
    dw_qkv, *small_sib = _matmul(h_all, dqkv_all, F32, "in_proj_attention_dw", mode="tn", tk=2048, tn=QKV_B,
                                 exchange=_sibling_exchange(small_slabs))
    small_partials = [_pair_sum(a, b, core, "grads_pair_sum_%d" % (i + 1))
                      for i, (a, b) in enumerate(zip(small_slabs, small_sib))]
    dw_rest, *small_contrib = _matmul(h[None], dproj_r[None], F32, "in_proj_rest_dw", mode="tn", tk=2048,
                                      exchange=_chip_exchange(small_partials))
    dw_rest = dw_rest[0]
    o2 = 2 * d
    dw_in_pieces = [dw_qkv[0], dw_qkv[1], dw_qkv[2],
                    dw_rest[:, o2:o2 + WIDTH], dw_rest[:, o2 + WIDTH:o2 + WIDTH + QKV_B],
                    dw_rest[:, o2 + WIDTH + QKV_B:o2 + 2 * WIDTH + QKV_B],
                    dw_rest[:, o2 + 2 * WIDTH + QKV_B:o2 + 2 * WIDTH + QKV_B + 2 * HEADS],
                    dw_rest[:, :o2]]

    w_in_slabs = jnp.stack(_regroup_columns(dw_in_pieces, [win8] * N_DEV))
    dh_a, w_in_sib = _matmul(dqkv_all, w_qkv, F32, "in_proj_attention_dh", mode="nt", tk=2048,
                             exchange=_sibling_exchange([w_in_slabs]))
    w_in_partial = _pair_sum(w_in_slabs, w_in_sib, core, "grads_pair_sum_0")
    dh_r, w_in_contrib = _matmul(dproj_r[None], w_rest[None], F32, "in_proj_rest_dh", mode="nt", tk=2560,
                                 exchange=_chip_exchange([w_in_partial]))
    contrib = [w_in_contrib] + small_contrib
    dh_parts = [dh_r[0]] + [_from_dilated_rows(dh_a, g, DILATIONS[g], "dh_to_natural_%d" % g) for g in (1, 2)]
    grad_x, dnorm_w = _rms_bwd(x2, norm_w, dh_a, dh_parts, dx_res)

    small_parts = [dnorm_w, dfnw, dconv8[:CONV_K], dalog3[:, 0, 0], ddtb3[:, 0, 0], dgnw3[0], loss_blk[0, 0:1]]
    small_rows = [-(-p.size // LANES) for p in small_parts]
    small = jnp.concatenate([jnp.pad(p.reshape(-1), (0, r * LANES - p.size)).reshape(r, LANES)
                             for p, r in zip(small_parts, small_rows)])
    small = jnp.pad(small, ((0, (-small.shape[0]) % SUBLANES), (0, 0)))
    small_sum = _small_all_reduce(small)
    pieces, r0 = [], 0
    for p, r in zip(small_parts, small_rows):
        pieces.append(small_sum[r0:r0 + r].reshape(-1)[:p.size].reshape(p.shape))
        r0 += r
    g_norm_w, g_fnw, g_conv_full, g_alog, g_dtb, g_gnw, loss_sum = pieces
    g_conv = lax.dynamic_slice(g_conv_full, (0, me * conv8w), (CONV_K, conv8w))

    big = [_adamw(c, w[0], m[0], v[0], name) for c, w, m, v, name in (
        (contrib[0], w_in, m_w_in, v_w_in, "adamw_w_in"), (contrib[1], w_up_a, m_w_up_a, v_w_up_a, "adamw_w_up_a"),
        (contrib[2], w_up_b, m_w_up_b, v_w_up_b, "adamw_w_up_b"), (contrib[3], w_out, m_w_out, v_w_out, "adamw_w_out"))]
    g_big, d_big, nm_big, nv_big = ([t[i] for t in big] for i in range(4))

    small_ws = [norm_w, final_norm_w, conv_w, a_log, dt_bias, gdn_norm_w]
    small_ms = [m_norm_w, m_final_norm_w, m_conv_w, m_a_log, m_dt_bias, m_gdn_norm_w]
    small_vs = [v_norm_w, v_final_norm_w, v_conv_w, v_a_log, v_dt_bias, v_gdn_norm_w]
    small_gs = [g_norm_w, g_fnw, g_conv, g_alog, g_dtb, g_gnw]
    small_shapes = [t.shape for t in small_ws]
    sm = _adamw(_pack_rows(small_gs, F32, SUBLANES)[None], _pack_rows(small_ws, F32, SUBLANES),
                _pack_rows(small_ms, F32, SUBLANES), _pack_rows(small_vs, F32, SUBLANES), "adamw_small")
    g_sm, d_sm, nm_sm, nv_sm = (_unpack_rows(t, small_shapes) for t in sm)

    def ordered(bigs, smalls):
        nw, fnw_, cw, al, dtb, gn = smalls
        wi, ua, ub, wo_ = (t[None] for t in bigs)
        return [nw, wi, cw, al, dtb, gn, ua, ub, wo_, fnw_]

    return (loss_sum.reshape(()), grad_x[None], *ordered(g_big, g_sm), *ordered(d_big, d_sm),
            *ordered(nm_big, nm_sm), *ordered(nv_big, nv_sm))
```
